```python
import math
import jax, jax.numpy as jnp
from jax import lax
import numpy as np

D_MODEL = 2048
BATCH = 2
SEQ = 16384
DEPTH = 2

HEAD_DIM = 128
ROPE_THETA = 10000.0
EPS = 1e-6
F32 = jnp.float32

A_HEADS = 4
MOBA_BLOCK = 256
MOBA_TOPK = 3
MOBA_QCHUNK = 64

B_GROUPS = ((128, 1), (512, 4), (2048, 16))
B_HEADS = 4
B_QBLOCK = 128
B_PAD = B_QBLOCK * max(d for _, d in B_GROUPS)

C_HEADS = 8
Q_LORA = 1536
KV_LORA = 512
NOPE_DIM = 128
ROPE_DIM = 64
V_DIM = 128
C_QBLOCK = 128

MEM_LEN = 256
X_HEADS = 4

D_FF = 5504
CONV_W = 3

N_BRANCH = 3
A_W = A_HEADS * HEAD_DIM
B_QKV_W = len(B_GROUPS) * B_HEADS * HEAD_DIM
B_W = B_HEADS * HEAD_DIM
C_W = C_HEADS * V_DIM
X_W = X_HEADS * HEAD_DIM
IN_WIDTHS = (A_W, A_W, A_W, B_QKV_W, B_QKV_W, B_QKV_W, Q_LORA, KV_LORA, ROPE_DIM, N_BRANCH * D_MODEL)
D_IN = sum(IN_WIDTHS)
IN_OFFSETS = tuple(int(o) for o in np.cumsum(IN_WIDTHS)[:-1])

kernel_name = 'hybrid_moba_dilated_mla_block'


def _rmsnorm(x, g):
    x32 = x.astype(F32)
    y = x32 * lax.rsqrt(jnp.mean(x32 * x32, axis=-1, keepdims=True) + EPS)
    return (y * g.astype(F32)).astype(x.dtype)


def _rope_tables(seq, dim):
    inv_freq = jnp.exp(jnp.arange(0, dim, 2, dtype=F32) * (-math.log(ROPE_THETA) / dim))
    ang = jnp.arange(seq, dtype=F32)[:, None] * inv_freq[None, :]
    return jnp.cos(ang), jnp.sin(ang)


def _apply_rope(x, cos, sin):
    x1, x2 = jnp.split(x.astype(F32), 2, axis=-1)
    return jnp.concatenate([x1 * cos - x2 * sin, x2 * cos + x1 * sin], axis=-1).astype(x.dtype)


def _heads(t, nh):
    bsz, s = t.shape[:2]
    return t.reshape(bsz, s, nh, -1).transpose(0, 2, 1, 3)


def _merge_heads(o):
    bsz, nh, s, dh = o.shape
    return o.transpose(0, 2, 1, 3).reshape(bsz, s, nh * dh)


def _pad_axis(t, size, axis):
    pad = [(0, 0)] * t.ndim
    pad[axis] = (0, size - t.shape[axis])
    return jnp.pad(t, pad)


def _moba_attention(q, k, v):
    bsz, nh, s, dh = q.shape
    nb = -(-s // MOBA_BLOCK)
    sp = nb * MOBA_BLOCK
    q, k, v = (_pad_axis(t, sp, 2) for t in (q, k, v))
    k_blocks = k.reshape(bsz, nh, nb, MOBA_BLOCK, dh)
    v_blocks = v.reshape(bsz, nh, nb, MOBA_BLOCK, dh)
    k_mean = jnp.mean(k_blocks.astype(F32), axis=3)
    topk = min(MOBA_TOPK, nb)
    scale = dh ** -0.5
    b_idx = jnp.arange(bsz)[:, None, None, None]
    h_idx = jnp.arange(nh)[None, :, None, None]
    q_off = jnp.arange(MOBA_QCHUNK)
    j_off = jnp.arange(MOBA_BLOCK)
    n_ids = jnp.arange(nb)

    def chunk(c):
        t0 = c * MOBA_QCHUNK
        own = t0 // MOBA_BLOCK
        qc = lax.dynamic_slice_in_dim(q, t0, MOBA_QCHUNK, axis=2).astype(F32)
        gate = jnp.einsum('bhqd,bhnd->bhqn', qc, k_mean)
        gate = jnp.where(n_ids < own, gate, -jnp.inf)
        gval, gidx = lax.top_k(gate, topk)
        sel_ok = jnp.isfinite(gval)
        k_sel = k_blocks[b_idx, h_idx, gidx].astype(F32)
        v_sel = v_blocks[b_idx, h_idx, gidx].astype(F32)
        s_sel = jnp.einsum('bhqd,bhqnjd->bhqnj', qc, k_sel) * scale
        s_sel = jnp.where(sel_ok[..., None], s_sel, -jnp.inf)
        s_sel = s_sel.reshape(bsz, nh, MOBA_QCHUNK, topk * MOBA_BLOCK)
        k_own = lax.dynamic_slice_in_dim(k, own * MOBA_BLOCK, MOBA_BLOCK, axis=2).astype(F32)
        v_own = lax.dynamic_slice_in_dim(v, own * MOBA_BLOCK, MOBA_BLOCK, axis=2).astype(F32)
        s_own = jnp.einsum('bhqd,bhjd->bhqj', qc, k_own) * scale
        causal = (own * MOBA_BLOCK + j_off)[None, :] <= (t0 + q_off)[:, None]
        s_own = jnp.where(causal, s_own, -jnp.inf)
        p = jax.nn.softmax(jnp.concatenate([s_sel, s_own], axis=-1), axis=-1)
        p_sel = p[..., :topk * MOBA_BLOCK].reshape(bsz, nh, MOBA_QCHUNK, topk, MOBA_BLOCK)
        o = (jnp.einsum('bhqnj,bhqnjd->bhqd', p_sel, v_sel)
             + jnp.einsum('bhqj,bhjd->bhqd', p[..., topk * MOBA_BLOCK:], v_own))
        return o.astype(q.dtype)

    out = lax.map(chunk, jnp.arange(sp // MOBA_QCHUNK))
    out = jnp.moveaxis(out, 0, 2).reshape(bsz, nh, sp, dh)
    return out[:, :, :s]


def _dilated_group(q, k, v, window, dilation):
    bsz, nh, sp, dh = q.shape
    span = window // dilation
    length = sp // dilation
    nb = length // B_QBLOCK
    scale = dh ** -0.5

    def to_blocks(t):
        t = t.astype(F32).reshape(bsz, nh, length, dilation, dh).transpose(0, 1, 3, 2, 4)
        return t.reshape(bsz, nh, dilation, nb, B_QBLOCK, dh)

    def with_prev(t):
        prev = jnp.pad(t[:, :, :, :-1], ((0, 0), (0, 0), (0, 0), (1, 0), (0, 0), (0, 0)))
        return jnp.concatenate([prev, t], axis=4)

    qb = to_blocks(q)
    kk = with_prev(to_blocks(k))
    vv = with_prev(to_blocks(v))
    sc = jnp.einsum('bhrnqd,bhrnjd->bhrnqj', qb, kk) * scale
    i = jnp.arange(B_QBLOCK)[:, None]
    j = jnp.arange(2 * B_QBLOCK)[None, :]
    dist = i + B_QBLOCK - j
    blk = jnp.arange(nb)[:, None, None]
    ok = (dist >= 0) & (dist <= span) & ((blk > 0) | (j >= B_QBLOCK))
    sc = jnp.where(ok, sc, -jnp.inf)
    m = jnp.max(sc, axis=-1, keepdims=True)
    e = jnp.exp(sc - m)
    den = jnp.sum(e, axis=-1, keepdims=True)
    o = jnp.einsum('bhrnqj,bhrnjd->bhrnqd', e, vv) / den
    lse = (m + jnp.log(den))[..., 0]
    o = o.reshape(bsz, nh, dilation, length, dh).transpose(0, 1, 3, 2, 4).reshape(bsz, nh, sp, dh)
    lse = lse.reshape(bsz, nh, dilation, length).transpose(0, 1, 3, 2).reshape(bsz, nh, sp)
    return o, lse


def _dilated_attention(q, k, v):
    s = q.shape[2]
    sp = -(-s // B_PAD) * B_PAD
    q, k, v = (_pad_axis(t, sp, 2) for t in (q, k, v))
    outs, lses = [], []
    for g, (window, dilation) in enumerate(B_GROUPS):
        sl = slice(g * B_HEADS, (g + 1) * B_HEADS)
        o, lse = _dilated_group(q[:, sl], k[:, sl], v[:, sl], window, dilation)
        outs.append(o)
        lses.append(lse)
    w = jax.nn.softmax(jnp.stack(lses), axis=0)
    o = jnp.sum(w[..., None] * jnp.stack(outs), axis=0)
    return o[:, :, :s].astype(q.dtype)


def _mla_attention(c_q, c_kv, k_rope, g_cq, g_ckv, w_uq, w_ukv, cos_r, sin_r):
    bsz, s, _ = c_q.shape
    q = _heads(_rmsnorm(c_q, g_cq) @ w_uq, C_HEADS)
    q_nope = q[..., :NOPE_DIM].astype(F32)
    q_rope = _apply_rope(q[..., NOPE_DIM:], cos_r, sin_r).astype(F32)
    kv = _heads(_rmsnorm(c_kv, g_ckv) @ w_ukv, C_HEADS)
    k_nope = kv[..., :NOPE_DIM].astype(F32)
    v = kv[..., NOPE_DIM:].astype(F32)
    k_r = _apply_rope(k_rope, cos_r, sin_r).astype(F32)
    scale = (NOPE_DIM + ROPE_DIM) ** -0.5
    kpos = jnp.arange(s)
    qoff = jnp.arange(C_QBLOCK)

    def qblock(c):
        t0 = c * C_QBLOCK
        qn = lax.dynamic_slice_in_dim(q_nope, t0, C_QBLOCK, axis=2)
        qr = lax.dynamic_slice_in_dim(q_rope, t0, C_QBLOCK, axis=2)
        sc = (jnp.einsum('bhqd,bhkd->bhqk', qn, k_nope)
              + jnp.einsum('bhqd,bkd->bhqk', qr, k_r)) * scale
        causal = kpos[None, :] <= (t0 + qoff)[:, None]
        p = jax.nn.softmax(jnp.where(causal, sc, -jnp.inf), axis=-1)
        return jnp.einsum('bhqk,bhkd->bhqd', p, v)

    out = lax.map(qblock, jnp.arange(s // C_QBLOCK))
    out = jnp.moveaxis(out, 0, 2).reshape(bsz, C_HEADS, s, V_DIM)
    return _merge_heads(out).astype(c_q.dtype)


def _hybrid_mixer(h, w_in, g_cq, g_ckv, w_uq, w_ukv, w_pa, w_pb, w_pc, w_o, cos_h, sin_h, cos_r, sin_r):
    bsz, s, _ = h.shape
    z = h @ w_in
    qa, ka, va, qb, kb, vb, cq, ckv, kr, gates = jnp.split(z, IN_OFFSETS, axis=-1)
    out_a = _moba_attention(_apply_rope(_heads(qa, A_HEADS), cos_h, sin_h),
                            _apply_rope(_heads(ka, A_HEADS), cos_h, sin_h),
                            _heads(va, A_HEADS))
    out_a = _merge_heads(out_a).astype(h.dtype)
    nbh = len(B_GROUPS) * B_HEADS
    out_b = _dilated_attention(_apply_rope(_heads(qb, nbh), cos_h, sin_h),
                               _apply_rope(_heads(kb, nbh), cos_h, sin_h),
                               _heads(vb, nbh))
    out_b = _merge_heads(out_b).astype(h.dtype)
    out_c = _mla_attention(cq, ckv, kr, g_cq, g_ckv, w_uq, w_ukv, cos_r, sin_r)
    g = jax.nn.sigmoid(gates.reshape(bsz, s, N_BRANCH, D_MODEL))
    merged = (g[:, :, 0] * (out_a @ w_pa) + g[:, :, 1] * (out_b @ w_pb)
              + g[:, :, 2] * (out_c @ w_pc))
    return merged @ w_o


def _memory_attention(h, mem, g_memkv, w_xq, w_xk, w_xv, w_xo):
    memn = _rmsnorm(mem, g_memkv)
    q = _heads(h @ w_xq, X_HEADS).astype(F32)
    k = _heads(memn @ w_xk, X_HEADS).astype(F32)
    v = _heads(memn @ w_xv, X_HEADS).astype(F32)
    p = jax.nn.softmax(jnp.einsum('bhqd,bhkd->bhqk', q, k) * HEAD_DIM ** -0.5, axis=-1)
    o = jnp.einsum('bhqk,bhkd->bhqd', p, v)
    return _merge_heads(o).astype(h.dtype) @ w_xo


def _conv_ffn(h, w_up, conv_w, conv_b, w_down):
    u = h @ w_up
    s = u.shape[1]
    up = jnp.pad(u, ((0, 0), (CONV_W - 1, 0), (0, 0)))
    c = conv_b
    for tap in range(CONV_W):
        c = c + conv_w[tap] * up[:, tap:tap + s]
    gate, val = jnp.split(c, 2, axis=-1)
    return (jax.nn.silu(gate) * val) @ w_down


def setup_inputs(seed: int = 0) -> dict:
    key = jax.random.key(seed)
    ks = jax.random.split(key, 24)

    def dense(k, shape):
        return jax.random.normal(k, shape, F32) * shape[-2] ** -0.5

    def gain(k, shape):
        return 1.0 + 0.02 * jax.random.normal(k, shape, F32)

    L = DEPTH
    return {
        'x': jax.random.normal(ks[0], (BATCH, SEQ, D_MODEL), F32),
        'mem': jax.random.normal(ks[1], (BATCH, MEM_LEN, D_MODEL), F32),
        'g_mix': gain(ks[2], (L, D_MODEL)),
        'w_in': dense(ks[3], (L, D_MODEL, D_IN)),
        'g_cq': gain(ks[4], (L, Q_LORA)),
        'g_ckv': gain(ks[5], (L, KV_LORA)),
        'w_uq': dense(ks[6], (L, Q_LORA, C_HEADS * (NOPE_DIM + ROPE_DIM))),
        'w_ukv': dense(ks[7], (L, KV_LORA, C_HEADS * (NOPE_DIM + V_DIM))),
        'w_pa': dense(ks[8], (L, A_W, D_MODEL)),
        'w_pb': dense(ks[9], (L, B_W, D_MODEL)),
        'w_pc': dense(ks[10], (L, C_W, D_MODEL)),
        'w_o': dense(ks[11], (L, D_MODEL, D_MODEL)),
        'g_mem': gain(ks[12], (L, D_MODEL)),
        'g_memkv': gain(ks[13], (L, D_MODEL)),
        'w_xq': dense(ks[14], (L, D_MODEL, X_W)),
        'w_xk': dense(ks[15], (L, D_MODEL, X_W)),
        'w_xv': dense(ks[16], (L, D_MODEL, X_W)),
        'w_xo': dense(ks[17], (L, X_W, D_MODEL)),
        'g_ffn': gain(ks[18], (L, D_MODEL)),
        'w_up': dense(ks[19], (L, D_MODEL, 2 * D_FF)),
        'conv_w': jax.random.normal(ks[20], (L, CONV_W, 2 * D_FF), F32) * CONV_W ** -0.5,
        'conv_b': 0.01 * jax.random.normal(ks[21], (L, 2 * D_FF), F32),
        'w_down': dense(ks[22], (L, D_FF, D_MODEL)),
        'g_final': gain(ks[23], (D_MODEL,)),
    }


def reference(x, mem, g_mix, w_in, g_cq, g_ckv, w_uq, w_ukv, w_pa, w_pb, w_pc, w_o,
              g_mem, g_memkv, w_xq, w_xk, w_xv, w_xo, g_ffn, w_up, conv_w, conv_b,
              w_down, g_final):
    s = x.shape[1]
    cos_h, sin_h = _rope_tables(s, HEAD_DIM)
    cos_r, sin_r = _rope_tables(s, ROPE_DIM)
    for l in range(DEPTH):
        h = _rmsnorm(x, g_mix[l])
        x = x + _hybrid_mixer(h, w_in[l], g_cq[l], g_ckv[l], w_uq[l], w_ukv[l], w_pa[l],
                              w_pb[l], w_pc[l], w_o[l], cos_h, sin_h, cos_r, sin_r)
        h = _rmsnorm(x, g_mem[l])
        x = x + _memory_attention(h, mem, g_memkv[l], w_xq[l], w_xk[l], w_xv[l], w_xo[l])
        h = _rmsnorm(x, g_ffn[l])
        x = x + _conv_ffn(h, w_up[l], conv_w[l], conv_b[l], w_down[l])
    return _rmsnorm(x, g_final)
```

```python
import functools
import math

import jax
import jax.numpy as jnp
import numpy as np
from jax import lax
from jax.experimental import pallas as pl
from jax.experimental.pallas import tpu as pltpu

F32 = jnp.float32
BF16 = jnp.bfloat16

LANES = 128
SUBLANES = 8
VMEM_LIMIT = 56 * 1024 * 1024

D_MODEL = 2048
DEPTH = 2
HEAD_DIM = 128
ROPE_THETA = 10000.0
EPS = 1e-6

A_HEADS = 4
MOBA_BLOCK = 256
MOBA_TOPK = 3

B_GROUPS = ((128, 1), (512, 4), (2048, 16))
B_HEADS = 4
B_QBLOCK = 128

C_HEADS = 8
Q_LORA = 1536
KV_LORA = 512
NOPE_DIM = 128
ROPE_DIM = 64
V_DIM = 128

X_HEADS = 4
D_FF = 5504
CONV_W = 3

A_W = A_HEADS * HEAD_DIM
B_QKV_W = len(B_GROUPS) * B_HEADS * HEAD_DIM
B_W = B_HEADS * HEAD_DIM
C_W = C_HEADS * V_DIM
X_W = X_HEADS * HEAD_DIM
IN_WIDTHS = (A_W, A_W, A_W, B_QKV_W, B_QKV_W, B_QKV_W, Q_LORA, KV_LORA, ROPE_DIM, 3 * D_MODEL)
IN_OFFSETS = tuple(int(o) for o in np.cumsum((0,) + IN_WIDTHS))

QK_W = 2 * A_W + 2 * B_QKV_W
V_W = A_W + B_QKV_W
QA_BLK, KA_BLK = 0, A_W // LANES
QB_BLK = 2 * A_W // LANES
KB_BLK = QB_BLK + B_QKV_W // LANES
VA_BLK, VB_BLK = 0, A_W // LANES

C_QK = 2 * LANES
MASKED = -1e30

D_FF_PAD = 5632
FFN_TF = 512
HALO = SUBLANES


def _cparams(*sem):
    return pltpu.CompilerParams(dimension_semantics=sem, vmem_limit_bytes=VMEM_LIMIT)


def _resident(arr):
    zeros = (0,) * arr.ndim
    return pl.BlockSpec(arr.shape, lambda *_: zeros, pipeline_mode=pl.Buffered(1))


def _dot(a, b):
    return jnp.dot(a, b, preferred_element_type=F32)


def _dot_nt(a, b):
    return lax.dot_general(a, b, (((1,), (1,)), ((), ())), preferred_element_type=F32)


def _rms(x, g):
    return x * lax.rsqrt(jnp.mean(x * x, axis=-1, keepdims=True) + EPS) * g


def _rmsnorm_kernel(x_ref, g_ref, o_ref):
    o_ref[...] = _rms(x_ref[...], g_ref[...]).astype(o_ref.dtype)


def _rmsnorm(x, g, out_dtype, tm=512):
    m, d = x.shape
    return pl.pallas_call(
        _rmsnorm_kernel,
        grid=(m // tm,),
        in_specs=[pl.BlockSpec((tm, d), lambda i: (i, 0)),
                  pl.BlockSpec((1, d), lambda i: (0, 0))],
        out_specs=pl.BlockSpec((tm, d), lambda i: (i, 0)),
        out_shape=jax.ShapeDtypeStruct((m, d), out_dtype),
        compiler_params=_cparams("parallel"),
        name="rmsnorm",
    )(x, g.reshape(1, d))


def _rope128(x, c, s):
    return x * c + pltpu.roll(x, HEAD_DIM // 2, 1) * s


def _rope64(x, c, sa, sb):
    half = ROPE_DIM // 2
    return x * c + pltpu.roll(x, LANES - half, 1) * sa + pltpu.roll(x, half, 1) * sb


def _mm_plain_kernel(a_ref, w_ref, o_ref):
    o_ref[...] = _dot(a_ref[...], w_ref[...]).astype(o_ref.dtype)


def _mm_sigmoid_kernel(a_ref, w_ref, o_ref):
    o_ref[...] = jax.nn.sigmoid(_dot(a_ref[...], w_ref[...])).astype(o_ref.dtype)


def _mm_rope_kernel(a_ref, w_ref, cs_ref, c_ref, s_ref, o_ref):
    acc = _dot(a_ref[...], w_ref[...])
    c = c_ref[...]
    s = s_ref[...]
    for j in range(acc.shape[1] // LANES):
        sl = slice(j * LANES, (j + 1) * LANES)
        o_ref[:, sl] = (_rope128(acc[:, sl], c, s) * cs_ref[:, sl]).astype(o_ref.dtype)


def _matmul(a, w, kernel, out_dtype, tm, tn, seq=None, extras=(), name="matmul"):
    m, k = a.shape
    n = w.shape[1]
    in_specs = [pl.BlockSpec((tm, k), lambda i, j: (i, 0)),
                pl.BlockSpec((k, tn), lambda i, j: (0, j))]
    args = [a, w]
    for kind, arr in extras:
        if kind == "col":
            in_specs.append(pl.BlockSpec((1, tn), lambda i, j: (0, j)))
        else:
            nt = seq // tm
            in_specs.append(pl.BlockSpec((tm, LANES), lambda i, j: (i % nt, 0)))
        args.append(arr)
    return pl.pallas_call(
        kernel,
        grid=(m // tm, n // tn),
        in_specs=in_specs,
        out_specs=pl.BlockSpec((tm, tn), lambda i, j: (i, j)),
        out_shape=jax.ShapeDtypeStruct((m, n), out_dtype),
        compiler_params=_cparams("parallel", "parallel"),
        name=name,
    )(*args)


def _mla_down_kernel(h_ref, w_ref, gq_ref, gkv_ref, c_ref, sa_ref, sb_ref,
                     cq_ref, ckv_ref, kr_ref):
    acc = _dot(h_ref[...], w_ref[...])
    cq_ref[...] = _rms(acc[:, :Q_LORA], gq_ref[...]).astype(cq_ref.dtype)
    ckv_ref[...] = _rms(acc[:, Q_LORA:Q_LORA + KV_LORA], gkv_ref[...]).astype(ckv_ref.dtype)
    kr = acc[:, Q_LORA + KV_LORA:]
    kr_ref[...] = _rope64(kr, c_ref[...], sa_ref[...], sb_ref[...]).astype(kr_ref.dtype)


def _mla_down(h, w, g_cq, g_ckv, rope_r, seq, tm=512):
    m, k = h.shape
    n = w.shape[1]
    nt = seq // tm
    row = lambda width: pl.BlockSpec((tm, width), lambda i: (i, 0))
    full = lambda r, c: pl.BlockSpec((r, c), lambda i: (0, 0))
    pos = pl.BlockSpec((tm, LANES), lambda i: (i % nt, 0))
    return pl.pallas_call(
        _mla_down_kernel,
        grid=(m // tm,),
        in_specs=[row(k), full(k, n), full(1, Q_LORA), full(1, KV_LORA), pos, pos, pos],
        out_specs=[row(Q_LORA), row(KV_LORA), row(LANES)],
        out_shape=[jax.ShapeDtypeStruct((m, Q_LORA), BF16),
                   jax.ShapeDtypeStruct((m, KV_LORA), BF16),
                   jax.ShapeDtypeStruct((m, LANES), BF16)],
        compiler_params=_cparams("parallel"),
        name="mla_down",
    )(h, w, g_cq.reshape(1, -1), g_ckv.reshape(1, -1), *rope_r)


def _mla_q_kernel(cq_ref, w_ref, c_ref, sa_ref, sb_ref, q_ref, *, scale):
    acc = _dot(cq_ref[...], w_ref[...])
    c, sa, sb = c_ref[...], sa_ref[...], sb_ref[...]
    for hd in range(C_HEADS):
        lo = hd * C_QK
        q_ref[:, lo:lo + LANES] = (acc[:, lo:lo + LANES] * scale).astype(q_ref.dtype)
        rope = _rope64(acc[:, lo + LANES:lo + C_QK], c, sa, sb)
        q_ref[:, lo + LANES:lo + C_QK] = (rope * scale).astype(q_ref.dtype)


def _mla_q(cq, w, rope_r, seq, tm=512):
    m, k = cq.shape
    n = w.shape[1]
    nt = seq // tm
    pos = pl.BlockSpec((tm, LANES), lambda i: (i % nt, 0))
    return pl.pallas_call(
        functools.partial(_mla_q_kernel, scale=(NOPE_DIM + ROPE_DIM) ** -0.5),
        grid=(m // tm,),
        in_specs=[pl.BlockSpec((tm, k), lambda i: (i, 0)),
                  pl.BlockSpec((k, n), lambda i: (0, 0)), pos, pos, pos],
        out_specs=pl.BlockSpec((tm, n), lambda i: (i, 0)),
        out_shape=jax.ShapeDtypeStruct((m, n), BF16),
        compiler_params=_cparams("parallel"),
        name="mla_q",
    )(cq, w, *rope_r)


def _mla_kv_kernel(ckv_ref, kr_ref, wk_ref, wv_ref, k_ref, v_ref):
    ckv = ckv_ref[...]
    kn = _dot(ckv, wk_ref[...])
    kr = kr_ref[...]
    for hd in range(C_HEADS):
        lo = hd * C_QK
        k_ref[:, lo:lo + LANES] = kn[:, hd * LANES:(hd + 1) * LANES].astype(k_ref.dtype)
        k_ref[:, lo + LANES:lo + C_QK] = kr
    v_ref[...] = _dot(ckv, wv_ref[...]).astype(v_ref.dtype)


def _mla_kv(ckv, kr, wk, wv, tm=512):
    m, k = ckv.shape
    return pl.pallas_call(
        _mla_kv_kernel,
        grid=(m // tm,),
        in_specs=[pl.BlockSpec((tm, k), lambda i: (i, 0)),
                  pl.BlockSpec((tm, LANES), lambda i: (i, 0)),
                  pl.BlockSpec(wk.shape, lambda i: (0, 0)),
                  pl.BlockSpec(wv.shape, lambda i: (0, 0))],
        out_specs=[pl.BlockSpec((tm, C_HEADS * C_QK), lambda i: (i, 0)),
                   pl.BlockSpec((tm, C_W), lambda i: (i, 0))],
        out_shape=[jax.ShapeDtypeStruct((m, C_HEADS * C_QK), BF16),
                   jax.ShapeDtypeStruct((m, C_W), BF16)],
        compiler_params=_cparams("parallel"),
        name="mla_kv",
    )(ckv, kr, wk, wv)


def _online_update(s, v, m_sc, l_sc, acc_sc):
    m_old = m_sc[...]
    m_new = jnp.maximum(m_old, jnp.max(s, axis=-1, keepdims=True))
    alpha = jnp.exp(m_old - m_new)
    p = jnp.exp(s - m_new)
    l_sc[...] = alpha * l_sc[...] + jnp.sum(p, axis=-1, keepdims=True)
    acc_sc[...] = alpha * acc_sc[...] + _dot(p.astype(v.dtype), v)
    m_sc[...] = m_new


def _online_init(s, v, m_sc, l_sc, acc_sc):
    m = jnp.max(s, axis=-1, keepdims=True)
    p = jnp.exp(s - m)
    m_sc[...] = m
    l_sc[...] = jnp.sum(p, axis=-1, keepdims=True)
    acc_sc[...] = _dot(p.astype(v.dtype), v)


def _causal_mask(t):
    row = lax.broadcasted_iota(jnp.int32, (t, t), 0)
    col = lax.broadcasted_iota(jnp.int32, (t, t), 1)
    return col <= row


def _flash_kernel(q_ref, k_ref, v_ref, o_ref, m_sc, l_sc, acc_sc, *, t):
    i = pl.program_id(2)
    q = q_ref[...]
    d0 = pl.multiple_of(i * t, t)
    s = _dot_nt(q, k_ref[pl.ds(d0, t), :])
    s = jnp.where(_causal_mask(t), s, MASKED)
    _online_init(s, v_ref[pl.ds(d0, t), :], m_sc, l_sc, acc_sc)

    def body(j, carry):
        j0 = pl.multiple_of(j * t, t)
        _online_update(_dot_nt(q, k_ref[pl.ds(j0, t), :]), v_ref[pl.ds(j0, t), :],
                       m_sc, l_sc, acc_sc)
        return carry

    lax.fori_loop(0, i, body, 0)
    o_ref[...] = (acc_sc[...] / l_sc[...]).astype(o_ref.dtype)


def _flash_attention(q, k, v, heads, qk_w, v_w, t=512):
    bsz, seq, _ = q.shape
    return pl.pallas_call(
        functools.partial(_flash_kernel, t=t),
        grid=(bsz, heads, seq // t),
        in_specs=[pl.BlockSpec((None, t, qk_w), lambda b, h, i: (b, i, h)),
                  pl.BlockSpec((None, seq, qk_w), lambda b, h, i: (b, 0, h)),
                  pl.BlockSpec((None, seq, v_w), lambda b, h, i: (b, 0, h))],
        out_specs=pl.BlockSpec((None, t, v_w), lambda b, h, i: (b, i, h)),
        out_shape=jax.ShapeDtypeStruct((bsz, seq, heads * v_w), BF16),
        scratch_shapes=[pltpu.VMEM((t, 1), F32), pltpu.VMEM((t, 1), F32),
                        pltpu.VMEM((t, v_w), F32)],
        compiler_params=_cparams("parallel", "parallel", "arbitrary"),
        name="mla_flash",
    )(q, k, v)


def _kmean_kernel(k_ref, o_ref):
    k = k_ref[...].astype(F32)
    o_ref[...] = jnp.mean(k.reshape(SUBLANES, MOBA_BLOCK, k.shape[-1]), axis=1)


def _kmean(qk):
    bsz, seq, _ = qk.shape
    rows = SUBLANES * MOBA_BLOCK
    return pl.pallas_call(
        _kmean_kernel,
        grid=(bsz, seq // rows),
        in_specs=[pl.BlockSpec((None, rows, A_W), lambda b, i: (b, i, KA_BLK * LANES // A_W))],
        out_specs=pl.BlockSpec((None, SUBLANES, A_W), lambda b, i: (b, i, 0)),
        out_shape=jax.ShapeDtypeStruct((bsz, seq // MOBA_BLOCK, A_W), F32),
        compiler_params=_cparams("parallel", "parallel"),
        name="moba_kmean",
    )(qk)


def _moba_kernel(q_ref, k_ref, v_ref, km_ref, o_ref, m_sc, l_sc, acc_sc, sel_sc):
    t = MOBA_BLOCK
    i = pl.program_id(2)
    q = q_ref[...]
    nb = km_ref.shape[0]

    km = km_ref[...]
    km_hi = km.astype(BF16)
    km_lo = (km - km_hi.astype(F32)).astype(BF16)
    gate = _dot_nt(q, km_hi) + _dot_nt(q, km_lo)
    blk = lax.broadcasted_iota(jnp.int32, gate.shape, 1)
    neg_inf = jnp.float32(-jnp.inf)
    g = jnp.where(blk < i, gate, neg_inf)
    sel = jnp.zeros(gate.shape, F32)
    for _ in range(MOBA_TOPK):
        mx = jnp.max(g, axis=-1, keepdims=True)
        is_max = (g == mx) & (mx > neg_inf)
        first = jnp.min(jnp.where(is_max, blk, nb), axis=-1, keepdims=True)
        pick = blk == first
        sel = jnp.where(pick, 1.0, sel)
        g = jnp.where(pick, neg_inf, g)
    sel_sc[...] = sel

    d0 = pl.multiple_of(i * t, t)
    s = _dot_nt(q, k_ref[pl.ds(d0, t), :])
    s = jnp.where(_causal_mask(t), s, MASKED)
    _online_init(s, v_ref[pl.ds(d0, t), :], m_sc, l_sc, acc_sc)

    def body(j, carry):
        j0 = pl.multiple_of(j * t, t)
        s = _dot_nt(q, k_ref[pl.ds(j0, t), :])
        picked = jnp.sum(jnp.where(blk == j, sel_sc[...], 0.0), axis=-1, keepdims=True) > 0.0
        s = jnp.where(picked, s, MASKED)
        _online_update(s, v_ref[pl.ds(j0, t), :], m_sc, l_sc, acc_sc)
        return carry

    lax.fori_loop(0, i, body, 0)
    o_ref[...] = (acc_sc[...] / l_sc[...]).astype(o_ref.dtype)


def _moba_attention(qk, v, kmean):
    bsz, seq, _ = qk.shape
    t = MOBA_BLOCK
    nb = seq // t
    return pl.pallas_call(
        _moba_kernel,
        grid=(bsz, A_HEADS, nb),
        in_specs=[pl.BlockSpec((None, t, HEAD_DIM), lambda b, h, i: (b, i, QA_BLK + h)),
                  pl.BlockSpec((None, seq, HEAD_DIM), lambda b, h, i: (b, 0, KA_BLK + h)),
                  pl.BlockSpec((None, seq, HEAD_DIM), lambda b, h, i: (b, 0, VA_BLK + h)),
                  pl.BlockSpec((None, nb, HEAD_DIM), lambda b, h, i: (b, 0, h))],
        out_specs=pl.BlockSpec((None, t, HEAD_DIM), lambda b, h, i: (b, i, h)),
        out_shape=jax.ShapeDtypeStruct((bsz, seq, A_W), BF16),
        scratch_shapes=[pltpu.VMEM((t, 1), F32), pltpu.VMEM((t, 1), F32),
                        pltpu.VMEM((t, HEAD_DIM), F32), pltpu.VMEM((t, nb), F32)],
        compiler_params=_cparams("parallel", "parallel", "arbitrary"),
        name="moba",
    )(qk, qk, v, kmean)


def _dilated_kernel(q_ref, kp_ref, kc_ref, vp_ref, vc_ref, o_ref, lse_ref, *, span):
    t = B_QBLOCK
    n = pl.program_id(3)
    q = q_ref[...]
    row = lax.broadcasted_iota(jnp.int32, (t, t), 0)
    col = lax.broadcasted_iota(jnp.int32, (t, t), 1)
    dist_c = row - col
    dist_p = dist_c + t
    ok_c = (dist_c >= 0) & (dist_c <= span)
    ok_p = (dist_p <= span) & (n > 0)
    s_c = jnp.where(ok_c, _dot_nt(q, kc_ref[...]), MASKED)
    s_p = jnp.where(ok_p, _dot_nt(q, kp_ref[...]), MASKED)
    m = jnp.maximum(jnp.max(s_c, axis=-1, keepdims=True), jnp.max(s_p, axis=-1, keepdims=True))
    e_c = jnp.exp(s_c - m)
    e_p = jnp.exp(s_p - m)
    den = jnp.sum(e_c, axis=-1, keepdims=True) + jnp.sum(e_p, axis=-1, keepdims=True)
    o = _dot(e_c.astype(BF16), vc_ref[...]) + _dot(e_p.astype(BF16), vp_ref[...])
    o_ref[...] = o / den
    lse_ref[...] = jnp.broadcast_to(m + jnp.log(den), lse_ref.shape)


def _dilated_group(qk, v, group):
    window, d = B_GROUPS[group]
    bsz, seq, _ = qk.shape
    t = B_QBLOCK
    length = seq // d
    qk_blocks = QK_W // LANES
    v_blocks = V_W // LANES
    out_blocks = B_W // LANES
    qk_v = qk.reshape(bsz, length, d * QK_W)
    v_v = v.reshape(bsz, length, d * V_W)
    q_col = QB_BLK + group * B_HEADS
    k_col = KB_BLK + group * B_HEADS
    v_col = VB_BLK + group * B_HEADS
    blk = lambda f: pl.BlockSpec((None, t, LANES), f)
    prev = lambda n: jnp.maximum(n - 1, 0)
    out_spec = blk(lambda b, h, r, n: (b, n, r * out_blocks + h))
    o, lse = pl.pallas_call(
        functools.partial(_dilated_kernel, span=window // d),
        grid=(bsz, B_HEADS, d, length // t),
        in_specs=[blk(lambda b, h, r, n: (b, n, r * qk_blocks + q_col + h)),
                  blk(lambda b, h, r, n: (b, prev(n), r * qk_blocks + k_col + h)),
                  blk(lambda b, h, r, n: (b, n, r * qk_blocks + k_col + h)),
                  blk(lambda b, h, r, n: (b, prev(n), r * v_blocks + v_col + h)),
                  blk(lambda b, h, r, n: (b, n, r * v_blocks + v_col + h))],
        out_specs=[out_spec, out_spec],
        out_shape=[jax.ShapeDtypeStruct((bsz, length, d * B_W), F32)] * 2,
        compiler_params=_cparams("parallel", "parallel", "parallel", "parallel"),
        name=f"dilated_g{group}",
    )(qk_v, qk_v, qk_v, v_v, v_v)
    return o.reshape(bsz * seq, B_W), lse.reshape(bsz * seq, B_W)


def _mixer_tail_kernel(x_ref, oa_ref, o0_ref, o1_ref, o2_ref, l0_ref, l1_ref, l2_ref, oc_ref,
                       g_ref, wpa_ref, wpb_ref, wpc_ref, wo_ref, y_ref):
    l0, l1, l2 = l0_ref[...], l1_ref[...], l2_ref[...]
    mx = jnp.maximum(jnp.maximum(l0, l1), l2)
    e0, e1, e2 = jnp.exp(l0 - mx), jnp.exp(l1 - mx), jnp.exp(l2 - mx)
    ob = (e0 * o0_ref[...] + e1 * o1_ref[...] + e2 * o2_ref[...]) / (e0 + e1 + e2)
    pa = _dot(oa_ref[...], wpa_ref[...])
    pb = _dot(ob.astype(BF16), wpb_ref[...])
    pc = _dot(oc_ref[...], wpc_ref[...])
    d = D_MODEL
    merged = (g_ref[:, 0:d].astype(F32) * pa + g_ref[:, d:2 * d].astype(F32) * pb
              + g_ref[:, 2 * d:3 * d].astype(F32) * pc)
    y_ref[...] = x_ref[...] + _dot(merged.astype(BF16), wo_ref[...])


def _mixer_tail(x, out_a, o_groups, lse_groups, out_c, gates, w_pa, w_pb, w_pc, w_o, tm=256):
    m, d = x.shape
    row = lambda width: pl.BlockSpec((tm, width), lambda i: (i, 0))
    weights = [_resident(w) for w in (w_pa, w_pb, w_pc, w_o)]
    return pl.pallas_call(
        _mixer_tail_kernel,
        grid=(m // tm,),
        in_specs=[row(d), row(A_W)] + [row(B_W)] * 6 + [row(C_W), row(3 * d)] + weights,
        out_specs=row(d),
        out_shape=jax.ShapeDtypeStruct((m, d), F32),
        compiler_params=_cparams("parallel"),
        name="mixer_tail",
    )(x, out_a, *o_groups, *lse_groups, out_c, gates, w_pa, w_pb, w_pc, w_o)


def _mem_kv_kernel(mem_ref, g_ref, wk_ref, wv_ref, k_ref, v_ref):
    memn = _rms(mem_ref[...], g_ref[...]).astype(BF16)
    k_ref[...] = _dot(memn, wk_ref[...]).astype(k_ref.dtype)
    v_ref[...] = _dot(memn, wv_ref[...]).astype(v_ref.dtype)


def _mem_kv(mem, g, wk, wv):
    bsz, n, d = mem.shape
    out = pl.BlockSpec((None, n, X_W), lambda b: (b, 0, 0))
    return pl.pallas_call(
        _mem_kv_kernel,
        grid=(bsz,),
        in_specs=[pl.BlockSpec((None, n, d), lambda b: (b, 0, 0)),
                  pl.BlockSpec((1, d), lambda b: (0, 0)),
                  pl.BlockSpec(wk.shape, lambda b: (0, 0)),
                  pl.BlockSpec(wv.shape, lambda b: (0, 0))],
        out_specs=[out, out],
        out_shape=[jax.ShapeDtypeStruct((bsz, n, X_W), BF16)] * 2,
        compiler_params=_cparams("parallel"),
        name="mem_kv",
    )(mem, g.reshape(1, d), wk, wv)


def _mem_attn_kernel(x_ref, g_ref, wq_ref, k_ref, v_ref, wo_ref, y_ref):
    x = x_ref[...]
    h = _rms(x, g_ref[...]).astype(BF16)
    q = (_dot(h, wq_ref[...]) * HEAD_DIM ** -0.5).astype(BF16)
    heads = []
    for hd in range(X_HEADS):
        sl = slice(hd * HEAD_DIM, (hd + 1) * HEAD_DIM)
        s = _dot_nt(q[:, sl], k_ref[:, sl])
        p = jnp.exp(s - jnp.max(s, axis=-1, keepdims=True))
        o = _dot(p.astype(BF16), v_ref[:, sl]) / jnp.sum(p, axis=-1, keepdims=True)
        heads.append(o.astype(BF16))
    y_ref[...] = x + _dot(jnp.concatenate(heads, axis=-1), wo_ref[...])


def _mem_attention(x, g, wq, kmem, vmem, wo, seq, tm=512):
    m, d = x.shape
    nt = seq // tm
    n = kmem.shape[1]
    kv = pl.BlockSpec((None, n, X_W), lambda i: (i // nt, 0, 0))
    return pl.pallas_call(
        _mem_attn_kernel,
        grid=(m // tm,),
        in_specs=[pl.BlockSpec((tm, d), lambda i: (i, 0)),
                  pl.BlockSpec((1, d), lambda i: (0, 0)),
                  pl.BlockSpec(wq.shape, lambda i: (0, 0)), kv, kv,
                  pl.BlockSpec(wo.shape, lambda i: (0, 0))],
        out_specs=pl.BlockSpec((tm, d), lambda i: (i, 0)),
        out_shape=jax.ShapeDtypeStruct((m, d), F32),
        compiler_params=_cparams("parallel"),
        name="mem_attention",
    )(x, g.reshape(1, d), wq, kmem, vmem, wo)


def _ffn_kernel(x_ref, halo_ref, g_ref, wg_ref, wv_ref, cwg_ref, cwv_ref, cbg_ref, cbv_ref,
                wd_ref, y_ref, h_sc, acc_sc, *, tiles_per_seq):
    i = pl.program_id(0)
    f = pl.program_id(1)
    tm = x_ref.shape[0]

    @pl.when(f == 0)
    def _():
        g = g_ref[...]
        keep = (i % tiles_per_seq != 0).astype(F32)
        h_sc[0:HALO, :] = (_rms(halo_ref[...], g) * keep).astype(h_sc.dtype)
        h_sc[HALO:, :] = _rms(x_ref[...], g).astype(h_sc.dtype)
        acc_sc[...] = jnp.zeros_like(acc_sc)

    h = h_sc[...]

    def conv(w_ref, cw_ref, cb_ref):
        u = _dot(h, w_ref[...])
        c = cb_ref[...]
        for tap in range(CONV_W):
            lo = HALO - (CONV_W - 1) + tap
            c = c + cw_ref[tap:tap + 1, :] * u[lo:lo + tm, :]
        return c

    act = jax.nn.silu(conv(wg_ref, cwg_ref, cbg_ref)) * conv(wv_ref, cwv_ref, cbv_ref)
    acc_sc[...] += _dot(act.astype(BF16), wd_ref[...])

    @pl.when(f == pl.num_programs(1) - 1)
    def _():
        y_ref[...] = x_ref[...] + acc_sc[...]


def _conv_ffn(x, g, w_up, conv_w, conv_b, w_down, seq, tm=512):
    m, d = x.shape
    tf = FFN_TF
    nf = D_FF_PAD // tf
    halo_blocks = tm // HALO
    return pl.pallas_call(
        functools.partial(_ffn_kernel, tiles_per_seq=seq // tm),
        grid=(m // tm, nf),
        in_specs=[pl.BlockSpec((tm, d), lambda i, f: (i, 0)),
                  pl.BlockSpec((HALO, d), lambda i, f: (jnp.maximum(i * halo_blocks - 1, 0), 0)),
                  pl.BlockSpec((1, d), lambda i, f: (0, 0)),
                  pl.BlockSpec((d, tf), lambda i, f: (0, f)),
                  pl.BlockSpec((d, tf), lambda i, f: (0, f + nf)),
                  pl.BlockSpec((CONV_W, tf), lambda i, f: (0, f)),
                  pl.BlockSpec((CONV_W, tf), lambda i, f: (0, f + nf)),
                  pl.BlockSpec((1, tf), lambda i, f: (0, f)),
                  pl.BlockSpec((1, tf), lambda i, f: (0, f + nf)),
                  pl.BlockSpec((tf, d), lambda i, f: (f, 0))],
        out_specs=pl.BlockSpec((tm, d), lambda i, f: (i, 0)),
        out_shape=jax.ShapeDtypeStruct((m, d), F32),
        scratch_shapes=[pltpu.VMEM((HALO + tm, d), BF16), pltpu.VMEM((tm, d), F32)],
        compiler_params=_cparams("parallel", "arbitrary"),
        name="conv_ffn",
    )(x, x, g.reshape(1, d), w_up, w_up, conv_w, conv_w, conv_b, conv_b, w_down)


def _rope_tables(seq):
    def angles(dim):
        inv_freq = jnp.exp(jnp.arange(0, dim, 2, dtype=F32) * (-math.log(ROPE_THETA) / dim))
        ang = jnp.arange(seq, dtype=F32)[:, None] * inv_freq[None, :]
        return jnp.cos(ang), jnp.sin(ang)

    cos_h, sin_h = angles(HEAD_DIM)
    rope_h = (jnp.concatenate([cos_h, cos_h], axis=-1), jnp.concatenate([-sin_h, sin_h], axis=-1))
    cos_r, sin_r = angles(ROPE_DIM)
    z = jnp.zeros_like(cos_r)
    rope_r = (jnp.concatenate([cos_r, cos_r, z, z], axis=-1),
              jnp.concatenate([-sin_r, z, z, z], axis=-1),
              jnp.concatenate([z, sin_r, z, z], axis=-1))
    return rope_h, rope_r


def _split_in(w_in):
    return [w_in[:, IN_OFFSETS[k]:IN_OFFSETS[k + 1]] for k in range(len(IN_WIDTHS))]


def _pad_cols(w, width):
    return jnp.pad(w, ((0, 0), (0, width - w.shape[1])))


def _layer_params(w_in, w_uq, w_ukv, w_up, conv_w, conv_b, w_down):
    qa, ka, va, qb, kb, vb, cq, ckv, kr, gates = _split_in(w_in)
    w_qk = jnp.concatenate([qa, ka, qb, kb], axis=1).astype(BF16)
    w_v = jnp.concatenate([va, vb], axis=1).astype(BF16)
    w_down_in = jnp.concatenate([cq, ckv, _pad_cols(kr, LANES)], axis=1).astype(BF16)
    uq = w_uq.reshape(Q_LORA, C_HEADS, NOPE_DIM + ROPE_DIM)
    uq = jnp.pad(uq, ((0, 0), (0, 0), (0, C_QK - NOPE_DIM - ROPE_DIM)))
    ukv = w_ukv.reshape(KV_LORA, C_HEADS, NOPE_DIM + V_DIM)
    pad_ff = lambda w: jnp.pad(w, ((0, 0), (0, D_FF_PAD - D_FF)))
    two_halves = lambda w: jnp.concatenate([pad_ff(w[:, :D_FF]), pad_ff(w[:, D_FF:])], axis=1)
    return dict(
        w_qk=w_qk, w_v=w_v, w_gates=gates.astype(BF16), w_down_in=w_down_in,
        w_uq=uq.reshape(Q_LORA, C_HEADS * C_QK).astype(BF16),
        w_uk=ukv[:, :, :NOPE_DIM].reshape(KV_LORA, C_HEADS * NOPE_DIM).astype(BF16),
        w_uv=ukv[:, :, NOPE_DIM:].reshape(KV_LORA, C_W).astype(BF16),
        w_up=two_halves(w_up).astype(BF16),
        conv_w=two_halves(conv_w),
        conv_b=two_halves(conv_b.reshape(1, -1)),
        w_down=jnp.pad(w_down, ((0, D_FF_PAD - D_FF), (0, 0))).astype(BF16),
    )


def _qk_col_scale():
    q_scale = HEAD_DIM ** -0.5
    parts = [jnp.full((A_W,), q_scale, F32), jnp.ones((A_W,), F32),
             jnp.full((B_QKV_W,), q_scale, F32), jnp.ones((B_QKV_W,), F32)]
    return jnp.concatenate(parts).reshape(1, QK_W)


def _mixer(x, g_mix, p, g_cq, g_ckv, w_pa, w_pb, w_pc, w_o, rope_h, rope_r, bsz, seq):
    m = x.shape[0]
    h = _rmsnorm(x, g_mix, BF16)
    qk = _matmul(h, p["w_qk"], _mm_rope_kernel, BF16, 1024, 1024, seq=seq,
                 extras=(("col", _qk_col_scale()), ("pos", rope_h[0]), ("pos", rope_h[1])),
                 name="proj_qk_rope")
    v = _matmul(h, p["w_v"], _mm_plain_kernel, BF16, 1024, 1024, name="proj_v")
    gates = _matmul(h, p["w_gates"], _mm_sigmoid_kernel, BF16, 1024, 1024, name="proj_gates")
    cq, ckv, kr = _mla_down(h, p["w_down_in"], g_cq, g_ckv, rope_r, seq)
    q_c = _mla_q(cq, p["w_uq"], rope_r, seq)
    k_c, v_c = _mla_kv(ckv, kr, p["w_uk"], p["w_uv"])

    qk3 = qk.reshape(bsz, seq, QK_W)
    v3 = v.reshape(bsz, seq, V_W)
    out_a = _moba_attention(qk3, v3, _kmean(qk3)).reshape(m, A_W)
    groups = [_dilated_group(qk3, v3, g) for g in range(len(B_GROUPS))]
    out_c = _flash_attention(q_c.reshape(bsz, seq, -1), k_c.reshape(bsz, seq, -1),
                             v_c.reshape(bsz, seq, -1), C_HEADS, C_QK, V_DIM).reshape(m, C_W)
    return _mixer_tail(x, out_a, [g[0] for g in groups], [g[1] for g in groups], out_c, gates,
                       w_pa.astype(BF16), w_pb.astype(BF16), w_pc.astype(BF16), w_o.astype(BF16))


def kernel(x, mem, g_mix, w_in, g_cq, g_ckv, w_uq, w_ukv, w_pa, w_pb, w_pc, w_o, g_mem, g_memkv,
           w_xq, w_xk, w_xv, w_xo, g_ffn, w_up, conv_w, conv_b, w_down, g_final):
    bsz, seq, d = x.shape
    rope_h, rope_r = _rope_tables(seq)
    xf = x.reshape(bsz * seq, d)
    for l in range(DEPTH):
        p = _layer_params(w_in[l], w_uq[l], w_ukv[l], w_up[l], conv_w[l], conv_b[l], w_down[l])
        xf = _mixer(xf, g_mix[l], p, g_cq[l], g_ckv[l], w_pa[l], w_pb[l], w_pc[l], w_o[l],
                    rope_h, rope_r, bsz, seq)
        kmem, vmem = _mem_kv(mem, g_memkv[l], w_xk[l].astype(BF16), w_xv[l].astype(BF16))
        xf = _mem_attention(xf, g_mem[l], w_xq[l].astype(BF16), kmem, vmem,
                            w_xo[l].astype(BF16), seq)
        xf = _conv_ffn(xf, g_ffn[l], p["w_up"], p["conv_w"], p["conv_b"], p["w_down"], seq)
    return _rmsnorm(xf, g_final, F32).reshape(bsz, seq, d)
```

```python
import functools
import math

import jax
import jax.numpy as jnp
import numpy as np
from jax import lax
from jax.experimental import pallas as pl
from jax.experimental.pallas import tpu as pltpu

F32 = jnp.float32
BF16 = jnp.bfloat16

LANES = 128
SUBLANES = 8
VMEM_LIMIT = 56 * 1024 * 1024

D_MODEL = 2048
DEPTH = 2
HEAD_DIM = 128
ROPE_THETA = 10000.0
EPS = 1e-6

A_HEADS = 4
MOBA_BLOCK = 256
MOBA_TOPK = 3

B_GROUPS = ((128, 1), (512, 4), (2048, 16))
B_HEADS = 4
B_QBLOCK = 128

C_HEADS = 8
Q_LORA = 1536
KV_LORA = 512
NOPE_DIM = 128
ROPE_DIM = 64
V_DIM = 128

X_HEADS = 4
D_FF = 5504
CONV_W = 3

A_W = A_HEADS * HEAD_DIM
B_QKV_W = len(B_GROUPS) * B_HEADS * HEAD_DIM
B_W = B_HEADS * HEAD_DIM
C_W = C_HEADS * V_DIM
X_W = X_HEADS * HEAD_DIM
IN_WIDTHS = (A_W, A_W, A_W, B_QKV_W, B_QKV_W, B_QKV_W, Q_LORA, KV_LORA, ROPE_DIM, 3 * D_MODEL)
IN_OFFSETS = tuple(int(o) for o in np.cumsum((0,) + IN_WIDTHS))

QK_W = 2 * A_W + 2 * B_QKV_W
V_W = B_QKV_W
QA_BLK, KA_BLK = 0, A_W // LANES
QB_BLK = 2 * A_W // LANES
KB_BLK = QB_BLK + B_QKV_W // LANES
VB_BLK = 0

C_QK = 2 * LANES
MASKED = -1e30
LOG2E = math.log2(math.e)
BF16_ROWS = 16
VT_ROWS = V_DIM + BF16_ROWS
MOBA_CHUNK = 4 * MOBA_BLOCK

D_FF_PAD = 5632
FFN_TF = 512
HALO = SUBLANES


def _cparams(*sem):
    return pltpu.CompilerParams(dimension_semantics=sem, vmem_limit_bytes=VMEM_LIMIT)


def _resident(arr):
    zeros = (0,) * arr.ndim
    return pl.BlockSpec(arr.shape, lambda *_: zeros, pipeline_mode=pl.Buffered(1))


def _dot(a, b):
    return jnp.dot(a, b, preferred_element_type=F32)


def _dot_nt(a, b):
    return lax.dot_general(a, b, (((1,), (1,)), ((), ())), preferred_element_type=F32)


def _rms(x, g):
    return x * lax.rsqrt(jnp.mean(x * x, axis=-1, keepdims=True) + EPS) * g


def _rmsnorm_kernel(x_ref, g_ref, o_ref):
    o_ref[...] = _rms(x_ref[...], g_ref[...]).astype(o_ref.dtype)


def _rmsnorm(x, g, out_dtype, tm=512):
    m, d = x.shape
    return pl.pallas_call(
        _rmsnorm_kernel,
        grid=(m // tm,),
        in_specs=[pl.BlockSpec((tm, d), lambda i: (i, 0)),
                  pl.BlockSpec((1, d), lambda i: (0, 0))],
        out_specs=pl.BlockSpec((tm, d), lambda i: (i, 0)),
        out_shape=jax.ShapeDtypeStruct((m, d), out_dtype),
        compiler_params=_cparams("parallel"),
        name="rmsnorm",
    )(x, g.reshape(1, d))


def _rope128(x, c, s):
    return x * c + pltpu.roll(x, HEAD_DIM // 2, 1) * s


def _rope64(x, c, sa, sb):
    half = ROPE_DIM // 2
    return x * c + pltpu.roll(x, LANES - half, 1) * sa + pltpu.roll(x, half, 1) * sb


def _mm_plain_kernel(a_ref, w_ref, o_ref):
    o_ref[...] = _dot(a_ref[...], w_ref[...]).astype(o_ref.dtype)


def _mm_sigmoid_kernel(a_ref, w_ref, o_ref):
    o_ref[...] = jax.nn.sigmoid(_dot(a_ref[...], w_ref[...])).astype(o_ref.dtype)


def _mm_rope_kernel(a_ref, w_ref, cs_ref, c_ref, s_ref, o_ref):
    acc = _dot(a_ref[...], w_ref[...])
    c = c_ref[...]
    s = s_ref[...]
    for j in range(acc.shape[1] // LANES):
        sl = slice(j * LANES, (j + 1) * LANES)
        o_ref[:, sl] = (_rope128(acc[:, sl], c, s) * cs_ref[:, sl]).astype(o_ref.dtype)


def _matmul(a, w, kernel, out_dtype, tm, tn, seq=None, extras=(), name="matmul"):
    m, k = a.shape
    n = w.shape[1]
    in_specs = [pl.BlockSpec((tm, k), lambda i, j: (i, 0)),
                pl.BlockSpec((k, tn), lambda i, j: (0, j))]
    args = [a, w]
    for kind, arr in extras:
        if kind == "col":
            in_specs.append(pl.BlockSpec((1, tn), lambda i, j: (0, j)))
        else:
            nt = seq // tm
            in_specs.append(pl.BlockSpec((tm, LANES), lambda i, j: (i % nt, 0)))
        args.append(arr)
    return pl.pallas_call(
        kernel,
        grid=(m // tm, n // tn),
        in_specs=in_specs,
        out_specs=pl.BlockSpec((tm, tn), lambda i, j: (i, j)),
        out_shape=jax.ShapeDtypeStruct((m, n), out_dtype),
        compiler_params=_cparams("parallel", "parallel"),
        name=name,
    )(*args)


def _mla_down_kernel(h_ref, w_ref, gq_ref, gkv_ref, c_ref, sa_ref, sb_ref,
                     cq_ref, ckv_ref, kr_ref):
    acc = _dot(h_ref[...], w_ref[...])
    cq_ref[...] = _rms(acc[:, :Q_LORA], gq_ref[...]).astype(cq_ref.dtype)
    ckv_ref[...] = _rms(acc[:, Q_LORA:Q_LORA + KV_LORA], gkv_ref[...]).astype(ckv_ref.dtype)
    kr = acc[:, Q_LORA + KV_LORA:]
    kr_ref[...] = _rope64(kr, c_ref[...], sa_ref[...], sb_ref[...]).astype(kr_ref.dtype)


def _mla_down(h, w, g_cq, g_ckv, rope_r, seq, tm=512):
    m, k = h.shape
    n = w.shape[1]
    nt = seq // tm
    row = lambda width: pl.BlockSpec((tm, width), lambda i: (i, 0))
    full = lambda r, c: pl.BlockSpec((r, c), lambda i: (0, 0))
    pos = pl.BlockSpec((tm, LANES), lambda i: (i % nt, 0))
    return pl.pallas_call(
        _mla_down_kernel,
        grid=(m // tm,),
        in_specs=[row(k), full(k, n), full(1, Q_LORA), full(1, KV_LORA), pos, pos, pos],
        out_specs=[row(Q_LORA), row(KV_LORA), row(LANES)],
        out_shape=[jax.ShapeDtypeStruct((m, Q_LORA), BF16),
                   jax.ShapeDtypeStruct((m, KV_LORA), BF16),
                   jax.ShapeDtypeStruct((m, LANES), BF16)],
        compiler_params=_cparams("parallel"),
        name="mla_down",
    )(h, w, g_cq.reshape(1, -1), g_ckv.reshape(1, -1), *rope_r)


def _mla_q_kernel(cq_ref, w_ref, c_ref, sa_ref, sb_ref, q_ref, *, scale):
    acc = _dot(cq_ref[...], w_ref[...])
    c, sa, sb = c_ref[...], sa_ref[...], sb_ref[...]
    for hd in range(C_HEADS):
        lo = hd * C_QK
        q_ref[:, lo:lo + LANES] = (acc[:, lo:lo + LANES] * scale).astype(q_ref.dtype)
        rope = _rope64(acc[:, lo + LANES:lo + C_QK], c, sa, sb)
        q_ref[:, lo + LANES:lo + C_QK] = (rope * scale).astype(q_ref.dtype)


def _mla_q(cq, w, rope_r, seq, tm=512):
    m, k = cq.shape
    n = w.shape[1]
    nt = seq // tm
    pos = pl.BlockSpec((tm, LANES), lambda i: (i % nt, 0))
    return pl.pallas_call(
        functools.partial(_mla_q_kernel, scale=(NOPE_DIM + ROPE_DIM) ** -0.5 * LOG2E),
        grid=(m // tm,),
        in_specs=[pl.BlockSpec((tm, k), lambda i: (i, 0)),
                  pl.BlockSpec((k, n), lambda i: (0, 0)), pos, pos, pos],
        out_specs=pl.BlockSpec((tm, n), lambda i: (i, 0)),
        out_shape=jax.ShapeDtypeStruct((m, n), BF16),
        compiler_params=_cparams("parallel"),
        name="mla_q",
    )(cq, w, *rope_r)


def _store_vt(v, vt_ref):
    vt = v.T
    for hd in range(vt_ref.shape[0]):
        vt_ref[hd, 0:V_DIM, :] = vt[hd * V_DIM:(hd + 1) * V_DIM, :].astype(vt_ref.dtype)
        vt_ref[hd, V_DIM:VT_ROWS, :] = jnp.ones((VT_ROWS - V_DIM, vt.shape[1]), vt_ref.dtype)


def _mla_kv_kernel(ckv_ref, kr_ref, wk_ref, wv_ref, k_ref, vt_ref):
    ckv = ckv_ref[...]
    kn = _dot(ckv, wk_ref[...])
    kr = kr_ref[...]
    for hd in range(C_HEADS):
        lo = hd * C_QK
        k_ref[:, lo:lo + LANES] = kn[:, hd * LANES:(hd + 1) * LANES].astype(k_ref.dtype)
        k_ref[:, lo + LANES:lo + C_QK] = kr
    _store_vt(_dot(ckv, wv_ref[...]), vt_ref)


def _mla_kv(ckv, kr, wk, wv, bsz, seq, tm=512):
    m, k = ckv.shape
    nt = seq // tm
    return pl.pallas_call(
        _mla_kv_kernel,
        grid=(m // tm,),
        in_specs=[pl.BlockSpec((tm, k), lambda i: (i, 0)),
                  pl.BlockSpec((tm, LANES), lambda i: (i, 0)),
                  pl.BlockSpec(wk.shape, lambda i: (0, 0)),
                  pl.BlockSpec(wv.shape, lambda i: (0, 0))],
        out_specs=[pl.BlockSpec((tm, C_HEADS * C_QK), lambda i: (i, 0)),
                   pl.BlockSpec((None, C_HEADS, VT_ROWS, tm), lambda i: (i // nt, 0, 0, i % nt))],
        out_shape=[jax.ShapeDtypeStruct((m, C_HEADS * C_QK), BF16),
                   jax.ShapeDtypeStruct((bsz, C_HEADS, VT_ROWS, seq), BF16)],
        compiler_params=_cparams("parallel"),
        name="mla_kv",
    )(ckv, kr, wk, wv)


def _proj_vt_kernel(h_ref, w_ref, vt_ref):
    _store_vt(_dot(h_ref[...], w_ref[...]), vt_ref)


def _proj_vt(h, w, heads, bsz, seq, tm=512):
    m, k = h.shape
    nt = seq // tm
    return pl.pallas_call(
        _proj_vt_kernel,
        grid=(m // tm,),
        in_specs=[pl.BlockSpec((tm, k), lambda i: (i, 0)), pl.BlockSpec(w.shape, lambda i: (0, 0))],
        out_specs=pl.BlockSpec((None, heads, VT_ROWS, tm), lambda i: (i // nt, 0, 0, i % nt)),
        out_shape=jax.ShapeDtypeStruct((bsz, heads, VT_ROWS, seq), BF16),
        compiler_params=_cparams("parallel"),
        name="proj_vt",
    )(h, w)


def _attend_chunks(qT, k_ref, vt_ref, scratch, *, tk, n_full, mask_main, mask_tail):
    m_sc, al_sc, acc_sc, sa_sc, sb_sc, pa_sc, pb_sc = scratch
    s_bufs = (sa_sc, sb_sc)
    p_bufs = (pa_sc, pb_sc)
    last_chunk = k_ref.shape[0] // tk - 1
    m_sc[...] = jnp.full(m_sc.shape, MASKED, F32)
    al_sc[...] = jnp.ones(al_sc.shape, F32)
    acc_sc[...] = jnp.zeros(acc_sc.shape, F32)
    pb_sc[...] = jnp.zeros(pb_sc.shape, pb_sc.dtype)

    def rows(c):
        return pl.ds(pl.multiple_of(jnp.minimum(c, last_chunk) * tk, tk), tk)

    def scores(c):
        return _dot(k_ref[rows(c), :], qT)

    def flush(c, p_ref):
        acc_sc[...] = al_sc[...] * acc_sc[...] + _dot(vt_ref[:, rows(c)], p_ref[...])

    def softmax(sT, p_ref):
        m_old = m_sc[...]
        m_new = jnp.maximum(m_old, jnp.max(sT, axis=0, keepdims=True))
        al_sc[...] = jnp.exp2(m_old - m_new)
        p_ref[...] = jnp.exp2(sT - m_new).astype(p_ref.dtype)
        m_sc[...] = m_new

    def step(tau, slot, mask, prefetch):
        flush(jnp.maximum(tau - 1, 0), p_bufs[1 - slot])
        if prefetch:
            s_bufs[1 - slot][...] = scores(tau + 1)
        sT = s_bufs[slot][...]
        softmax(sT if mask is None else mask(sT, tau), p_bufs[slot])

    sa_sc[...] = scores(0)

    def pair(u, carry):
        step(2 * u, 0, mask_main, True)
        step(2 * u + 1, 1, mask_main, True)
        return carry

    pairs = n_full // 2
    lax.fori_loop(0, pairs, pair, 0)
    tau = 2 * pairs
    step(tau, 0, mask_tail, True)
    step(tau + 1, 1, mask_tail, False)
    flush(tau + 1, pb_sc)
    acc = acc_sc[...]
    return acc[:V_DIM, :] / acc[V_DIM:V_DIM + 1, :]


def _attend_scratch(tq, tk):
    return [pltpu.VMEM((1, tq), F32), pltpu.VMEM((1, tq), F32), pltpu.VMEM((VT_ROWS, tq), F32),
            pltpu.VMEM((tk, tq), F32), pltpu.VMEM((tk, tq), F32),
            pltpu.VMEM((tk, tq), BF16), pltpu.VMEM((tk, tq), BF16)]


def _transpose_q(q_ref):
    return q_ref[...].astype(F32).T.astype(BF16)


def _flash_kernel(q_ref, k_ref, vt_ref, o_ref, *scratch, t):
    i = pl.program_id(2)

    def causal(sT, c):
        key = lax.broadcasted_iota(jnp.int32, sT.shape, 0) + c * t
        qry = lax.broadcasted_iota(jnp.int32, sT.shape, 1) + i * t
        return jnp.where(key <= qry, sT, MASKED)

    oT = _attend_chunks(_transpose_q(q_ref), k_ref, vt_ref, scratch, tk=t, n_full=i,
                        mask_main=None, mask_tail=causal)
    o_ref[...] = oT.T.astype(o_ref.dtype)


def _flash_attention(q, k, vt, heads, qk_w, t=512):
    bsz, seq, _ = q.shape
    return pl.pallas_call(
        functools.partial(_flash_kernel, t=t),
        grid=(bsz, heads, seq // t),
        in_specs=[pl.BlockSpec((None, t, qk_w), lambda b, h, i: (b, i, h)),
                  pl.BlockSpec((None, seq, qk_w), lambda b, h, i: (b, 0, h)),
                  pl.BlockSpec((None, None, VT_ROWS, seq), lambda b, h, i: (b, h, 0, 0))],
        out_specs=pl.BlockSpec((None, t, V_DIM), lambda b, h, i: (b, i, h)),
        out_shape=jax.ShapeDtypeStruct((bsz, seq, heads * V_DIM), BF16),
        scratch_shapes=_attend_scratch(t, t),
        compiler_params=_cparams("parallel", "parallel", "arbitrary"),
        name="mla_flash",
    )(q, k, vt)


def _kmean_kernel(k_ref, o_ref):
    k = k_ref[...].astype(F32)
    o_ref[...] = jnp.mean(k.reshape(SUBLANES, MOBA_BLOCK, k.shape[-1]), axis=1)


def _kmean(qk):
    bsz, seq, _ = qk.shape
    rows = SUBLANES * MOBA_BLOCK
    return pl.pallas_call(
        _kmean_kernel,
        grid=(bsz, seq // rows),
        in_specs=[pl.BlockSpec((None, rows, A_W), lambda b, i: (b, i, KA_BLK * LANES // A_W))],
        out_specs=pl.BlockSpec((None, SUBLANES, A_W), lambda b, i: (b, i, 0)),
        out_shape=jax.ShapeDtypeStruct((bsz, seq // MOBA_BLOCK, A_W), F32),
        compiler_params=_cparams("parallel", "parallel"),
        name="moba_kmean",
    )(qk)


def _moba_kernel(q_ref, k_ref, vt_ref, km_ref, o_ref, sel_sc, *scratch):
    t = MOBA_BLOCK
    per_chunk = MOBA_CHUNK // t
    i = pl.program_id(2)
    nb = km_ref.shape[0]
    qT = _transpose_q(q_ref)

    km = km_ref[...]
    km_hi = km.astype(BF16)
    km_lo = (km - km_hi.astype(F32)).astype(BF16)
    gate = _dot(km_hi, qT) + _dot(km_lo, qT)
    blk = lax.broadcasted_iota(jnp.int32, gate.shape, 0)
    neg_inf = jnp.float32(-jnp.inf)
    g = jnp.where(blk < i, gate, neg_inf)
    sel = jnp.zeros(gate.shape, F32)
    for _ in range(MOBA_TOPK):
        mx = jnp.max(g, axis=0, keepdims=True)
        is_max = (g == mx) & (mx > neg_inf)
        first = jnp.min(jnp.where(is_max, blk, nb), axis=0, keepdims=True)
        pick = blk == first
        sel = jnp.where(pick, 1.0, sel)
        g = jnp.where(pick, neg_inf, g)
    sel_sc[...] = sel

    def per_block(sT, c, block_mask):
        parts = [block_mask(sT[j * t:(j + 1) * t, :], c * per_chunk + j) for j in range(per_chunk)]
        return jnp.concatenate(parts, axis=0)

    def picked(g_idx):
        return sel_sc[pl.ds(jnp.minimum(g_idx, nb - 1), 1), :] > 0.0

    def past(sT, c):
        return per_block(sT, c, lambda s, g_idx: jnp.where(picked(g_idx), s, MASKED))

    def general(sT, c):
        key = lax.broadcasted_iota(jnp.int32, (t, t), 0)
        qry = lax.broadcasted_iota(jnp.int32, (t, t), 1)
        causal = key <= qry

        def block_mask(s, g_idx):
            ok = (picked(g_idx) & (g_idx < i)) | (causal & (g_idx == i))
            return jnp.where(ok, s, MASKED)

        return per_block(sT, c, block_mask)

    oT = _attend_chunks(qT, k_ref, vt_ref, scratch, tk=MOBA_CHUNK, n_full=i // per_chunk,
                        mask_main=past, mask_tail=general)
    o_ref[...] = oT.T.astype(o_ref.dtype)


def _moba_attention(qk, vt, kmean):
    bsz, seq, _ = qk.shape
    t = MOBA_BLOCK
    nb = seq // t
    return pl.pallas_call(
        _moba_kernel,
        grid=(bsz, A_HEADS, nb),
        in_specs=[pl.BlockSpec((None, t, HEAD_DIM), lambda b, h, i: (b, i, QA_BLK + h)),
                  pl.BlockSpec((None, seq, HEAD_DIM), lambda b, h, i: (b, 0, KA_BLK + h)),
                  pl.BlockSpec((None, None, VT_ROWS, seq), lambda b, h, i: (b, h, 0, 0)),
                  pl.BlockSpec((None, nb, HEAD_DIM), lambda b, h, i: (b, 0, h))],
        out_specs=pl.BlockSpec((None, t, HEAD_DIM), lambda b, h, i: (b, i, h)),
        out_shape=jax.ShapeDtypeStruct((bsz, seq, A_W), BF16),
        scratch_shapes=[pltpu.VMEM((nb, t), F32)] + _attend_scratch(t, MOBA_CHUNK),
        compiler_params=_cparams("parallel", "parallel", "arbitrary"),
        name="moba",
    )(qk, qk, vt, kmean)


def _dilated_kernel(q_ref, kp_ref, kc_ref, vp_ref, vc_ref, o_ref, lse_ref, *, span):
    t = B_QBLOCK
    n = pl.program_id(3)
    q = q_ref[...]
    row = lax.broadcasted_iota(jnp.int32, (t, t), 0)
    col = lax.broadcasted_iota(jnp.int32, (t, t), 1)
    dist_c = row - col
    dist_p = dist_c + t
    ok_c = (dist_c >= 0) & (dist_c <= span)
    ok_p = (dist_p <= span) & (n > 0)
    s_c = jnp.where(ok_c, _dot_nt(q, kc_ref[...]), MASKED)
    s_p = jnp.where(ok_p, _dot_nt(q, kp_ref[...]), MASKED)
    m = jnp.maximum(jnp.max(s_c, axis=-1, keepdims=True), jnp.max(s_p, axis=-1, keepdims=True))
    e_c = jnp.exp(s_c - m)
    e_p = jnp.exp(s_p - m)
    den = jnp.sum(e_c, axis=-1, keepdims=True) + jnp.sum(e_p, axis=-1, keepdims=True)
    o = _dot(e_c.astype(BF16), vc_ref[...]) + _dot(e_p.astype(BF16), vp_ref[...])
    o_ref[...] = o / den
    lse_ref[...] = jnp.broadcast_to(m + jnp.log(den), lse_ref.shape)


def _dilated_group(qk, v, group):
    window, d = B_GROUPS[group]
    bsz, seq, _ = qk.shape
    t = B_QBLOCK
    length = seq // d
    qk_blocks = QK_W // LANES
    v_blocks = V_W // LANES
    out_blocks = B_W // LANES
    qk_v = qk.reshape(bsz, length, d * QK_W)
    v_v = v.reshape(bsz, length, d * V_W)
    q_col = QB_BLK + group * B_HEADS
    k_col = KB_BLK + group * B_HEADS
    v_col = VB_BLK + group * B_HEADS
    blk = lambda f: pl.BlockSpec((None, t, LANES), f)
    prev = lambda n: jnp.maximum(n - 1, 0)
    out_spec = blk(lambda b, h, r, n: (b, n, r * out_blocks + h))
    o, lse = pl.pallas_call(
        functools.partial(_dilated_kernel, span=window // d),
        grid=(bsz, B_HEADS, d, length // t),
        in_specs=[blk(lambda b, h, r, n: (b, n, r * qk_blocks + q_col + h)),
                  blk(lambda b, h, r, n: (b, prev(n), r * qk_blocks + k_col + h)),
                  blk(lambda b, h, r, n: (b, n, r * qk_blocks + k_col + h)),
                  blk(lambda b, h, r, n: (b, prev(n), r * v_blocks + v_col + h)),
                  blk(lambda b, h, r, n: (b, n, r * v_blocks + v_col + h))],
        out_specs=[out_spec, out_spec],
        out_shape=[jax.ShapeDtypeStruct((bsz, length, d * B_W), F32)] * 2,
        compiler_params=_cparams("parallel", "parallel", "parallel", "parallel"),
        name=f"dilated_g{group}",
    )(qk_v, qk_v, qk_v, v_v, v_v)
    return o.reshape(bsz * seq, B_W), lse.reshape(bsz * seq, B_W)


def _mixer_tail_kernel(x_ref, oa_ref, o0_ref, o1_ref, o2_ref, l0_ref, l1_ref, l2_ref, oc_ref,
                       g_ref, wpa_ref, wpb_ref, wpc_ref, wo_ref, y_ref):
    l0, l1, l2 = l0_ref[...], l1_ref[...], l2_ref[...]
    mx = jnp.maximum(jnp.maximum(l0, l1), l2)
    e0, e1, e2 = jnp.exp(l0 - mx), jnp.exp(l1 - mx), jnp.exp(l2 - mx)
    ob = (e0 * o0_ref[...] + e1 * o1_ref[...] + e2 * o2_ref[...]) / (e0 + e1 + e2)
    pa = _dot(oa_ref[...], wpa_ref[...])
    pb = _dot(ob.astype(BF16), wpb_ref[...])
    pc = _dot(oc_ref[...], wpc_ref[...])
    d = D_MODEL
    merged = (g_ref[:, 0:d].astype(F32) * pa + g_ref[:, d:2 * d].astype(F32) * pb
              + g_ref[:, 2 * d:3 * d].astype(F32) * pc)
    y_ref[...] = x_ref[...] + _dot(merged.astype(BF16), wo_ref[...])


def _mixer_tail(x, out_a, o_groups, lse_groups, out_c, gates, w_pa, w_pb, w_pc, w_o, tm=256):
    m, d = x.shape
    row = lambda width: pl.BlockSpec((tm, width), lambda i: (i, 0))
    weights = [_resident(w) for w in (w_pa, w_pb, w_pc, w_o)]
    return pl.pallas_call(
        _mixer_tail_kernel,
        grid=(m // tm,),
        in_specs=[row(d), row(A_W)] + [row(B_W)] * 6 + [row(C_W), row(3 * d)] + weights,
        out_specs=row(d),
        out_shape=jax.ShapeDtypeStruct((m, d), F32),
        compiler_params=_cparams("parallel"),
        name="mixer_tail",
    )(x, out_a, *o_groups, *lse_groups, out_c, gates, w_pa, w_pb, w_pc, w_o)


def _mem_kv_kernel(mem_ref, g_ref, wk_ref, wv_ref, k_ref, v_ref):
    memn = _rms(mem_ref[...], g_ref[...]).astype(BF16)
    k_ref[...] = _dot(memn, wk_ref[...]).astype(k_ref.dtype)
    v_ref[...] = _dot(memn, wv_ref[...]).astype(v_ref.dtype)


def _mem_kv(mem, g, wk, wv):
    bsz, n, d = mem.shape
    out = pl.BlockSpec((None, n, X_W), lambda b: (b, 0, 0))
    return pl.pallas_call(
        _mem_kv_kernel,
        grid=(bsz,),
        in_specs=[pl.BlockSpec((None, n, d), lambda b: (b, 0, 0)),
                  pl.BlockSpec((1, d), lambda b: (0, 0)),
                  pl.BlockSpec(wk.shape, lambda b: (0, 0)),
                  pl.BlockSpec(wv.shape, lambda b: (0, 0))],
        out_specs=[out, out],
        out_shape=[jax.ShapeDtypeStruct((bsz, n, X_W), BF16)] * 2,
        compiler_params=_cparams("parallel"),
        name="mem_kv",
    )(mem, g.reshape(1, d), wk, wv)


def _mem_attn_kernel(x_ref, g_ref, wq_ref, k_ref, v_ref, wo_ref, y_ref):
    x = x_ref[...]
    h = _rms(x, g_ref[...]).astype(BF16)
    q = (_dot(h, wq_ref[...]) * HEAD_DIM ** -0.5).astype(BF16)
    heads = []
    for hd in range(X_HEADS):
        sl = slice(hd * HEAD_DIM, (hd + 1) * HEAD_DIM)
        s = _dot_nt(q[:, sl], k_ref[:, sl])
        p = jnp.exp(s - jnp.max(s, axis=-1, keepdims=True))
        o = _dot(p.astype(BF16), v_ref[:, sl]) / jnp.sum(p, axis=-1, keepdims=True)
        heads.append(o.astype(BF16))
    y_ref[...] = x + _dot(jnp.concatenate(heads, axis=-1), wo_ref[...])


def _mem_attention(x, g, wq, kmem, vmem, wo, seq, tm=512):
    m, d = x.shape
    nt = seq // tm
    n = kmem.shape[1]
    kv = pl.BlockSpec((None, n, X_W), lambda i: (i // nt, 0, 0))
    return pl.pallas_call(
        _mem_attn_kernel,
        grid=(m // tm,),
        in_specs=[pl.BlockSpec((tm, d), lambda i: (i, 0)),
                  pl.BlockSpec((1, d), lambda i: (0, 0)),
                  pl.BlockSpec(wq.shape, lambda i: (0, 0)), kv, kv,
                  pl.BlockSpec(wo.shape, lambda i: (0, 0))],
        out_specs=pl.BlockSpec((tm, d), lambda i: (i, 0)),
        out_shape=jax.ShapeDtypeStruct((m, d), F32),
        compiler_params=_cparams("parallel"),
        name="mem_attention",
    )(x, g.reshape(1, d), wq, kmem, vmem, wo)


def _ffn_kernel(x_ref, halo_ref, g_ref, wg_ref, wv_ref, cwg_ref, cwv_ref, cbg_ref, cbv_ref,
                wd_ref, y_ref, h_sc, acc_sc, *, tiles_per_seq):
    i = pl.program_id(0)
    f = pl.program_id(1)
    tm = x_ref.shape[0]

    @pl.when(f == 0)
    def _():
        g = g_ref[...]
        keep = (i % tiles_per_seq != 0).astype(F32)
        h_sc[0:HALO, :] = (_rms(halo_ref[...], g) * keep).astype(h_sc.dtype)
        h_sc[HALO:, :] = _rms(x_ref[...], g).astype(h_sc.dtype)
        acc_sc[...] = jnp.zeros_like(acc_sc)

    h = h_sc[...]

    def conv(w_ref, cw_ref, cb_ref):
        u = _dot(h, w_ref[...])
        c = cb_ref[...]
        for tap in range(CONV_W):
            lo = HALO - (CONV_W - 1) + tap
            c = c + cw_ref[tap:tap + 1, :] * u[lo:lo + tm, :]
        return c

    act = jax.nn.silu(conv(wg_ref, cwg_ref, cbg_ref)) * conv(wv_ref, cwv_ref, cbv_ref)
    acc_sc[...] += _dot(act.astype(BF16), wd_ref[...])

    @pl.when(f == pl.num_programs(1) - 1)
    def _():
        y_ref[...] = x_ref[...] + acc_sc[...]


def _conv_ffn(x, g, w_up, conv_w, conv_b, w_down, seq, tm=512):
    m, d = x.shape
    tf = FFN_TF
    nf = D_FF_PAD // tf
    halo_blocks = tm // HALO
    return pl.pallas_call(
        functools.partial(_ffn_kernel, tiles_per_seq=seq // tm),
        grid=(m // tm, nf),
        in_specs=[pl.BlockSpec((tm, d), lambda i, f: (i, 0)),
                  pl.BlockSpec((HALO, d), lambda i, f: (jnp.maximum(i * halo_blocks - 1, 0), 0)),
                  pl.BlockSpec((1, d), lambda i, f: (0, 0)),
                  pl.BlockSpec((d, tf), lambda i, f: (0, f)),
                  pl.BlockSpec((d, tf), lambda i, f: (0, f + nf)),
                  pl.BlockSpec((CONV_W, tf), lambda i, f: (0, f)),
                  pl.BlockSpec((CONV_W, tf), lambda i, f: (0, f + nf)),
                  pl.BlockSpec((1, tf), lambda i, f: (0, f)),
                  pl.BlockSpec((1, tf), lambda i, f: (0, f + nf)),
                  pl.BlockSpec((tf, d), lambda i, f: (f, 0))],
        out_specs=pl.BlockSpec((tm, d), lambda i, f: (i, 0)),
        out_shape=jax.ShapeDtypeStruct((m, d), F32),
        scratch_shapes=[pltpu.VMEM((HALO + tm, d), BF16), pltpu.VMEM((tm, d), F32)],
        compiler_params=_cparams("parallel", "arbitrary"),
        name="conv_ffn",
    )(x, x, g.reshape(1, d), w_up, w_up, conv_w, conv_w, conv_b, conv_b, w_down)


def _rope_tables(seq):
    def angles(dim):
        inv_freq = jnp.exp(jnp.arange(0, dim, 2, dtype=F32) * (-math.log(ROPE_THETA) / dim))
        ang = jnp.arange(seq, dtype=F32)[:, None] * inv_freq[None, :]
        return jnp.cos(ang), jnp.sin(ang)

    cos_h, sin_h = angles(HEAD_DIM)
    rope_h = (jnp.concatenate([cos_h, cos_h], axis=-1), jnp.concatenate([-sin_h, sin_h], axis=-1))
    cos_r, sin_r = angles(ROPE_DIM)
    z = jnp.zeros_like(cos_r)
    rope_r = (jnp.concatenate([cos_r, cos_r, z, z], axis=-1),
              jnp.concatenate([-sin_r, z, z, z], axis=-1),
              jnp.concatenate([z, sin_r, z, z], axis=-1))
    return rope_h, rope_r


def _split_in(w_in):
    return [w_in[:, IN_OFFSETS[k]:IN_OFFSETS[k + 1]] for k in range(len(IN_WIDTHS))]


def _pad_cols(w, width):
    return jnp.pad(w, ((0, 0), (0, width - w.shape[1])))


def _layer_params(w_in, w_uq, w_ukv, w_up, conv_w, conv_b, w_down):
    qa, ka, va, qb, kb, vb, cq, ckv, kr, gates = _split_in(w_in)
    w_qk = jnp.concatenate([qa, ka, qb, kb], axis=1).astype(BF16)
    w_down_in = jnp.concatenate([cq, ckv, _pad_cols(kr, LANES)], axis=1).astype(BF16)
    uq = w_uq.reshape(Q_LORA, C_HEADS, NOPE_DIM + ROPE_DIM)
    uq = jnp.pad(uq, ((0, 0), (0, 0), (0, C_QK - NOPE_DIM - ROPE_DIM)))
    ukv = w_ukv.reshape(KV_LORA, C_HEADS, NOPE_DIM + V_DIM)
    pad_ff = lambda w: jnp.pad(w, ((0, 0), (0, D_FF_PAD - D_FF)))
    two_halves = lambda w: jnp.concatenate([pad_ff(w[:, :D_FF]), pad_ff(w[:, D_FF:])], axis=1)
    return dict(
        w_qk=w_qk, w_va=va.astype(BF16), w_vb=vb.astype(BF16), w_gates=gates.astype(BF16),
        w_down_in=w_down_in,
        w_uq=uq.reshape(Q_LORA, C_HEADS * C_QK).astype(BF16),
        w_uk=ukv[:, :, :NOPE_DIM].reshape(KV_LORA, C_HEADS * NOPE_DIM).astype(BF16),
        w_uv=ukv[:, :, NOPE_DIM:].reshape(KV_LORA, C_W).astype(BF16),
        w_up=two_halves(w_up).astype(BF16),
        conv_w=two_halves(conv_w),
        conv_b=two_halves(conv_b.reshape(1, -1)),
        w_down=jnp.pad(w_down, ((0, D_FF_PAD - D_FF), (0, 0))).astype(BF16),
    )


def _qk_col_scale():
    q_scale = HEAD_DIM ** -0.5
    parts = [jnp.full((A_W,), q_scale * LOG2E, F32), jnp.ones((A_W,), F32),
             jnp.full((B_QKV_W,), q_scale, F32), jnp.ones((B_QKV_W,), F32)]
    return jnp.concatenate(parts).reshape(1, QK_W)


def _mixer(x, g_mix, p, g_cq, g_ckv, w_pa, w_pb, w_pc, w_o, rope_h, rope_r, bsz, seq):
    m = x.shape[0]
    h = _rmsnorm(x, g_mix, BF16)
    qk = _matmul(h, p["w_qk"], _mm_rope_kernel, BF16, 1024, 1024, seq=seq,
                 extras=(("col", _qk_col_scale()), ("pos", rope_h[0]), ("pos", rope_h[1])),
                 name="proj_qk_rope")
    vt_a = _proj_vt(h, p["w_va"], A_HEADS, bsz, seq)
    v_b = _matmul(h, p["w_vb"], _mm_plain_kernel, BF16, 1024, 512, name="proj_vb")
    gates = _matmul(h, p["w_gates"], _mm_sigmoid_kernel, BF16, 1024, 1024, name="proj_gates")
    cq, ckv, kr = _mla_down(h, p["w_down_in"], g_cq, g_ckv, rope_r, seq)
    q_c = _mla_q(cq, p["w_uq"], rope_r, seq)
    k_c, vt_c = _mla_kv(ckv, kr, p["w_uk"], p["w_uv"], bsz, seq)

    qk3 = qk.reshape(bsz, seq, QK_W)
    v3 = v_b.reshape(bsz, seq, V_W)
    out_a = _moba_attention(qk3, vt_a, _kmean(qk3)).reshape(m, A_W)
    groups = [_dilated_group(qk3, v3, g) for g in range(len(B_GROUPS))]
    out_c = _flash_attention(q_c.reshape(bsz, seq, -1), k_c.reshape(bsz, seq, -1), vt_c,
                             C_HEADS, C_QK).reshape(m, C_W)
    return _mixer_tail(x, out_a, [g[0] for g in groups], [g[1] for g in groups], out_c, gates,
                       w_pa.astype(BF16), w_pb.astype(BF16), w_pc.astype(BF16), w_o.astype(BF16))


def kernel(x, mem, g_mix, w_in, g_cq, g_ckv, w_uq, w_ukv, w_pa, w_pb, w_pc, w_o, g_mem, g_memkv,
           w_xq, w_xk, w_xv, w_xo, g_ffn, w_up, conv_w, conv_b, w_down, g_final):
    bsz, seq, d = x.shape
    rope_h, rope_r = _rope_tables(seq)
    xf = x.reshape(bsz * seq, d)
    for l in range(DEPTH):
        p = _layer_params(w_in[l], w_uq[l], w_ukv[l], w_up[l], conv_w[l], conv_b[l], w_down[l])
        xf = _mixer(xf, g_mix[l], p, g_cq[l], g_ckv[l], w_pa[l], w_pb[l], w_pc[l], w_o[l],
                    rope_h, rope_r, bsz, seq)
        kmem, vmem = _mem_kv(mem, g_memkv[l], w_xk[l].astype(BF16), w_xv[l].astype(BF16))
        xf = _mem_attention(xf, g_mem[l], w_xq[l].astype(BF16), kmem, vmem,
                            w_xo[l].astype(BF16), seq)
        xf = _conv_ffn(xf, g_ffn[l], p["w_up"], p["conv_w"], p["conv_b"], p["w_down"], seq)
    return _rmsnorm(xf, g_final, F32).reshape(bsz, seq, d)
```

```python
import functools
import math

import jax
import jax.numpy as jnp
import numpy as np
from jax import lax
from jax.experimental import pallas as pl
from jax.experimental.pallas import tpu as pltpu

F32 = jnp.float32
BF16 = jnp.bfloat16

LANES = 128
SUBLANES = 8
VMEM_LIMIT = 56 * 1024 * 1024

D_MODEL = 2048
DEPTH = 2
HEAD_DIM = 128
ROPE_THETA = 10000.0
EPS = 1e-6

A_HEADS = 4
MOBA_BLOCK = 256
MOBA_TOPK = 3

B_GROUPS = ((128, 1), (512, 4), (2048, 16))
B_HEADS = 4
B_QBLOCK = 128

C_HEADS = 8
Q_LORA = 1536
KV_LORA = 512
NOPE_DIM = 128
ROPE_DIM = 64
V_DIM = 128

X_HEADS = 4
D_FF = 5504
CONV_W = 3

A_W = A_HEADS * HEAD_DIM
B_QKV_W = len(B_GROUPS) * B_HEADS * HEAD_DIM
B_W = B_HEADS * HEAD_DIM
C_W = C_HEADS * V_DIM
X_W = X_HEADS * HEAD_DIM
IN_WIDTHS = (A_W, A_W, A_W, B_QKV_W, B_QKV_W, B_QKV_W, Q_LORA, KV_LORA, ROPE_DIM, 3 * D_MODEL)
IN_OFFSETS = tuple(int(o) for o in np.cumsum((0,) + IN_WIDTHS))

QK_W = 2 * A_W
QA_BLK, KA_BLK = 0, A_W // LANES

C_QK = 2 * LANES
MASKED = -1e30
LOG2E = math.log2(math.e)
BF16_ROWS = 16
VT_ROWS = V_DIM + BF16_ROWS
MOBA_CHUNK = 4 * MOBA_BLOCK

D_FF_PAD = 5632
FFN_TF = 512
HALO = SUBLANES


def _cparams(*sem):
    return pltpu.CompilerParams(dimension_semantics=sem, vmem_limit_bytes=VMEM_LIMIT)


def _resident(arr):
    zeros = (0,) * arr.ndim
    return pl.BlockSpec(arr.shape, lambda *_: zeros, pipeline_mode=pl.Buffered(1))


def _dot(a, b):
    return jnp.dot(a, b, preferred_element_type=F32)


def _dot_nt(a, b):
    return lax.dot_general(a, b, (((1,), (1,)), ((), ())), preferred_element_type=F32)


def _rms(x, g):
    return x * lax.rsqrt(jnp.mean(x * x, axis=-1, keepdims=True) + EPS) * g


def _rmsnorm_kernel(x_ref, g_ref, o_ref):
    o_ref[...] = _rms(x_ref[...], g_ref[...]).astype(o_ref.dtype)


def _rmsnorm(x, g, out_dtype, tm=512):
    m, d = x.shape
    return pl.pallas_call(
        _rmsnorm_kernel,
        grid=(m // tm,),
        in_specs=[pl.BlockSpec((tm, d), lambda i: (i, 0)),
                  pl.BlockSpec((1, d), lambda i: (0, 0))],
        out_specs=pl.BlockSpec((tm, d), lambda i: (i, 0)),
        out_shape=jax.ShapeDtypeStruct((m, d), out_dtype),
        compiler_params=_cparams("parallel"),
        name="rmsnorm",
    )(x, g.reshape(1, d))


def _rope128(x, c, s):
    return x * c + pltpu.roll(x, HEAD_DIM // 2, 1) * s


def _rope64(x, c, sa, sb):
    half = ROPE_DIM // 2
    return x * c + pltpu.roll(x, LANES - half, 1) * sa + pltpu.roll(x, half, 1) * sb


def _mm_plain_kernel(a_ref, w_ref, o_ref):
    o_ref[...] = _dot(a_ref[...], w_ref[...]).astype(o_ref.dtype)


def _mm_sigmoid_kernel(a_ref, w_ref, o_ref):
    o_ref[...] = jax.nn.sigmoid(_dot(a_ref[...], w_ref[...])).astype(o_ref.dtype)


def _mm_rope_kernel(a_ref, w_ref, cs_ref, c_ref, s_ref, o_ref):
    acc = _dot(a_ref[...], w_ref[...])
    c = c_ref[...]
    s = s_ref[...]
    for j in range(acc.shape[1] // LANES):
        sl = slice(j * LANES, (j + 1) * LANES)
        o_ref[:, sl] = (_rope128(acc[:, sl], c, s) * cs_ref[:, sl]).astype(o_ref.dtype)


def _matmul(a, w, kernel, out_dtype, tm, tn, seq=None, extras=(), name="matmul"):
    m, k = a.shape
    n = w.shape[1]
    in_specs = [pl.BlockSpec((tm, k), lambda i, j: (i, 0)),
                pl.BlockSpec((k, tn), lambda i, j: (0, j))]
    args = [a, w]
    for kind, arr in extras:
        if kind == "col":
            in_specs.append(pl.BlockSpec((1, tn), lambda i, j: (0, j)))
        else:
            nt = seq // tm
            in_specs.append(pl.BlockSpec((tm, LANES), lambda i, j: (i % nt, 0)))
        args.append(arr)
    return pl.pallas_call(
        kernel,
        grid=(m // tm, n // tn),
        in_specs=in_specs,
        out_specs=pl.BlockSpec((tm, tn), lambda i, j: (i, j)),
        out_shape=jax.ShapeDtypeStruct((m, n), out_dtype),
        compiler_params=_cparams("parallel", "parallel"),
        name=name,
    )(*args)


def _mla_down_kernel(h_ref, w_ref, gq_ref, gkv_ref, c_ref, sa_ref, sb_ref,
                     cq_ref, ckv_ref, kr_ref):
    acc = _dot(h_ref[...], w_ref[...])
    cq_ref[...] = _rms(acc[:, :Q_LORA], gq_ref[...]).astype(cq_ref.dtype)
    ckv_ref[...] = _rms(acc[:, Q_LORA:Q_LORA + KV_LORA], gkv_ref[...]).astype(ckv_ref.dtype)
    kr = acc[:, Q_LORA + KV_LORA:]
    kr_ref[...] = _rope64(kr, c_ref[...], sa_ref[...], sb_ref[...]).astype(kr_ref.dtype)


def _mla_down(h, w, g_cq, g_ckv, rope_r, seq, tm=512):
    m, k = h.shape
    n = w.shape[1]
    nt = seq // tm
    row = lambda width: pl.BlockSpec((tm, width), lambda i: (i, 0))
    full = lambda r, c: pl.BlockSpec((r, c), lambda i: (0, 0))
    pos = pl.BlockSpec((tm, LANES), lambda i: (i % nt, 0))
    return pl.pallas_call(
        _mla_down_kernel,
        grid=(m // tm,),
        in_specs=[row(k), full(k, n), full(1, Q_LORA), full(1, KV_LORA), pos, pos, pos],
        out_specs=[row(Q_LORA), row(KV_LORA), row(LANES)],
        out_shape=[jax.ShapeDtypeStruct((m, Q_LORA), BF16),
                   jax.ShapeDtypeStruct((m, KV_LORA), BF16),
                   jax.ShapeDtypeStruct((m, LANES), BF16)],
        compiler_params=_cparams("parallel"),
        name="mla_down",
    )(h, w, g_cq.reshape(1, -1), g_ckv.reshape(1, -1), *rope_r)


def _mla_q_kernel(cq_ref, w_ref, c_ref, sa_ref, sb_ref, q_ref, *, scale):
    acc = _dot(cq_ref[...], w_ref[...])
    c, sa, sb = c_ref[...], sa_ref[...], sb_ref[...]
    for hd in range(C_HEADS):
        lo = hd * C_QK
        q_ref[:, lo:lo + LANES] = (acc[:, lo:lo + LANES] * scale).astype(q_ref.dtype)
        rope = _rope64(acc[:, lo + LANES:lo + C_QK], c, sa, sb)
        q_ref[:, lo + LANES:lo + C_QK] = (rope * scale).astype(q_ref.dtype)


def _mla_q(cq, w, rope_r, seq, tm=512):
    m, k = cq.shape
    n = w.shape[1]
    nt = seq // tm
    pos = pl.BlockSpec((tm, LANES), lambda i: (i % nt, 0))
    return pl.pallas_call(
        functools.partial(_mla_q_kernel, scale=(NOPE_DIM + ROPE_DIM) ** -0.5 * LOG2E),
        grid=(m // tm,),
        in_specs=[pl.BlockSpec((tm, k), lambda i: (i, 0)),
                  pl.BlockSpec((k, n), lambda i: (0, 0)), pos, pos, pos],
        out_specs=pl.BlockSpec((tm, n), lambda i: (i, 0)),
        out_shape=jax.ShapeDtypeStruct((m, n), BF16),
        compiler_params=_cparams("parallel"),
        name="mla_q",
    )(cq, w, *rope_r)


def _store_vt(v, vt_ref):
    vt = v.T
    for hd in range(vt_ref.shape[0]):
        vt_ref[hd, 0:V_DIM, :] = vt[hd * V_DIM:(hd + 1) * V_DIM, :].astype(vt_ref.dtype)
        vt_ref[hd, V_DIM:VT_ROWS, :] = jnp.ones((VT_ROWS - V_DIM, vt.shape[1]), vt_ref.dtype)


def _mla_kv_kernel(ckv_ref, kr_ref, wk_ref, wv_ref, k_ref, vt_ref):
    ckv = ckv_ref[...]
    kn = _dot(ckv, wk_ref[...])
    kr = kr_ref[...]
    for hd in range(C_HEADS):
        lo = hd * C_QK
        k_ref[:, lo:lo + LANES] = kn[:, hd * LANES:(hd + 1) * LANES].astype(k_ref.dtype)
        k_ref[:, lo + LANES:lo + C_QK] = kr
    _store_vt(_dot(ckv, wv_ref[...]), vt_ref)


def _mla_kv(ckv, kr, wk, wv, bsz, seq, tm=512):
    m, k = ckv.shape
    nt = seq // tm
    return pl.pallas_call(
        _mla_kv_kernel,
        grid=(m // tm,),
        in_specs=[pl.BlockSpec((tm, k), lambda i: (i, 0)),
                  pl.BlockSpec((tm, LANES), lambda i: (i, 0)),
                  pl.BlockSpec(wk.shape, lambda i: (0, 0)),
                  pl.BlockSpec(wv.shape, lambda i: (0, 0))],
        out_specs=[pl.BlockSpec((tm, C_HEADS * C_QK), lambda i: (i, 0)),
                   pl.BlockSpec((None, C_HEADS, VT_ROWS, tm), lambda i: (i // nt, 0, 0, i % nt))],
        out_shape=[jax.ShapeDtypeStruct((m, C_HEADS * C_QK), BF16),
                   jax.ShapeDtypeStruct((bsz, C_HEADS, VT_ROWS, seq), BF16)],
        compiler_params=_cparams("parallel"),
        name="mla_kv",
    )(ckv, kr, wk, wv)


def _proj_vt_kernel(h_ref, w_ref, vt_ref):
    _store_vt(_dot(h_ref[...], w_ref[...]), vt_ref)


def _proj_vt(h, w, heads, bsz, seq, tm=512):
    m, k = h.shape
    nt = seq // tm
    return pl.pallas_call(
        _proj_vt_kernel,
        grid=(m // tm,),
        in_specs=[pl.BlockSpec((tm, k), lambda i: (i, 0)), pl.BlockSpec(w.shape, lambda i: (0, 0))],
        out_specs=pl.BlockSpec((None, heads, VT_ROWS, tm), lambda i: (i // nt, 0, 0, i % nt)),
        out_shape=jax.ShapeDtypeStruct((bsz, heads, VT_ROWS, seq), BF16),
        compiler_params=_cparams("parallel"),
        name="proj_vt",
    )(h, w)


def _attend_chunks(qT, k_ref, vt_ref, scratch, *, tk, n_full, mask_main, mask_tail):
    m_sc, al_sc, acc_sc, sa_sc, sb_sc, pa_sc, pb_sc = scratch
    s_bufs = (sa_sc, sb_sc)
    p_bufs = (pa_sc, pb_sc)
    last_chunk = k_ref.shape[0] // tk - 1
    m_sc[...] = jnp.full(m_sc.shape, MASKED, F32)
    al_sc[...] = jnp.ones(al_sc.shape, F32)
    acc_sc[...] = jnp.zeros(acc_sc.shape, F32)
    pb_sc[...] = jnp.zeros(pb_sc.shape, pb_sc.dtype)

    def rows(c):
        return pl.ds(pl.multiple_of(jnp.minimum(c, last_chunk) * tk, tk), tk)

    def scores(c):
        return _dot(k_ref[rows(c), :], qT)

    def flush(c, p_ref):
        acc_sc[...] = al_sc[...] * acc_sc[...] + _dot(vt_ref[:, rows(c)], p_ref[...])

    def softmax(sT, p_ref):
        m_old = m_sc[...]
        m_new = jnp.maximum(m_old, jnp.max(sT, axis=0, keepdims=True))
        al_sc[...] = jnp.exp2(m_old - m_new)
        p_ref[...] = jnp.exp2(sT - m_new).astype(p_ref.dtype)
        m_sc[...] = m_new

    def step(tau, slot, mask, prefetch):
        flush(jnp.maximum(tau - 1, 0), p_bufs[1 - slot])
        if prefetch:
            s_bufs[1 - slot][...] = scores(tau + 1)
        sT = s_bufs[slot][...]
        softmax(sT if mask is None else mask(sT, tau), p_bufs[slot])

    sa_sc[...] = scores(0)

    def pair(u, carry):
        step(2 * u, 0, mask_main, True)
        step(2 * u + 1, 1, mask_main, True)
        return carry

    pairs = n_full // 2
    lax.fori_loop(0, pairs, pair, 0)
    tau = 2 * pairs
    step(tau, 0, mask_tail, True)
    step(tau + 1, 1, mask_tail, False)
    flush(tau + 1, pb_sc)
    acc = acc_sc[...]
    return acc[:V_DIM, :] / acc[V_DIM:V_DIM + 1, :]


def _attend_scratch(tq, tk):
    return [pltpu.VMEM((1, tq), F32), pltpu.VMEM((1, tq), F32), pltpu.VMEM((VT_ROWS, tq), F32),
            pltpu.VMEM((tk, tq), F32), pltpu.VMEM((tk, tq), F32),
            pltpu.VMEM((tk, tq), BF16), pltpu.VMEM((tk, tq), BF16)]


def _transpose_q(q_ref):
    return q_ref[...].astype(F32).T.astype(BF16)


def _flash_kernel(q_ref, k_ref, vt_ref, o_ref, *scratch, t):
    i = pl.program_id(2)

    def causal(sT, c):
        key = lax.broadcasted_iota(jnp.int32, sT.shape, 0) + c * t
        qry = lax.broadcasted_iota(jnp.int32, sT.shape, 1) + i * t
        return jnp.where(key <= qry, sT, MASKED)

    oT = _attend_chunks(_transpose_q(q_ref), k_ref, vt_ref, scratch, tk=t, n_full=i,
                        mask_main=None, mask_tail=causal)
    o_ref[...] = oT.T.astype(o_ref.dtype)


def _flash_attention(q, k, vt, heads, qk_w, t=512):
    bsz, seq, _ = q.shape
    return pl.pallas_call(
        functools.partial(_flash_kernel, t=t),
        grid=(bsz, heads, seq // t),
        in_specs=[pl.BlockSpec((None, t, qk_w), lambda b, h, i: (b, i, h)),
                  pl.BlockSpec((None, seq, qk_w), lambda b, h, i: (b, 0, h)),
                  pl.BlockSpec((None, None, VT_ROWS, seq), lambda b, h, i: (b, h, 0, 0))],
        out_specs=pl.BlockSpec((None, t, V_DIM), lambda b, h, i: (b, i, h)),
        out_shape=jax.ShapeDtypeStruct((bsz, seq, heads * V_DIM), BF16),
        scratch_shapes=_attend_scratch(t, t),
        compiler_params=_cparams("parallel", "parallel", "arbitrary"),
        name="mla_flash",
    )(q, k, vt)


def _kmean_kernel(k_ref, o_ref):
    k = k_ref[...].astype(F32)
    o_ref[...] = jnp.mean(k.reshape(SUBLANES, MOBA_BLOCK, k.shape[-1]), axis=1)


def _kmean(qk):
    bsz, seq, _ = qk.shape
    rows = SUBLANES * MOBA_BLOCK
    return pl.pallas_call(
        _kmean_kernel,
        grid=(bsz, seq // rows),
        in_specs=[pl.BlockSpec((None, rows, A_W), lambda b, i: (b, i, KA_BLK * LANES // A_W))],
        out_specs=pl.BlockSpec((None, SUBLANES, A_W), lambda b, i: (b, i, 0)),
        out_shape=jax.ShapeDtypeStruct((bsz, seq // MOBA_BLOCK, A_W), F32),
        compiler_params=_cparams("parallel", "parallel"),
        name="moba_kmean",
    )(qk)


def _moba_kernel(q_ref, k_ref, vt_ref, km_ref, o_ref, sel_sc, *scratch):
    t = MOBA_BLOCK
    per_chunk = MOBA_CHUNK // t
    i = pl.program_id(2)
    nb = km_ref.shape[0]
    qT = _transpose_q(q_ref)

    km = km_ref[...]
    km_hi = km.astype(BF16)
    km_lo = (km - km_hi.astype(F32)).astype(BF16)
    gate = _dot(km_hi, qT) + _dot(km_lo, qT)
    blk = lax.broadcasted_iota(jnp.int32, gate.shape, 0)
    neg_inf = jnp.float32(-jnp.inf)
    g = jnp.where(blk < i, gate, neg_inf)
    sel = jnp.zeros(gate.shape, F32)
    for _ in range(MOBA_TOPK):
        mx = jnp.max(g, axis=0, keepdims=True)
        is_max = (g == mx) & (mx > neg_inf)
        first = jnp.min(jnp.where(is_max, blk, nb), axis=0, keepdims=True)
        pick = blk == first
        sel = jnp.where(pick, 1.0, sel)
        g = jnp.where(pick, neg_inf, g)
    sel_sc[...] = sel

    def per_block(sT, c, block_mask):
        parts = [block_mask(sT[j * t:(j + 1) * t, :], c * per_chunk + j) for j in range(per_chunk)]
        return jnp.concatenate(parts, axis=0)

    def picked(g_idx):
        return sel_sc[pl.ds(jnp.minimum(g_idx, nb - 1), 1), :] > 0.0

    def past(sT, c):
        return per_block(sT, c, lambda s, g_idx: jnp.where(picked(g_idx), s, MASKED))

    def general(sT, c):
        key = lax.broadcasted_iota(jnp.int32, (t, t), 0)
        qry = lax.broadcasted_iota(jnp.int32, (t, t), 1)
        causal = key <= qry

        def block_mask(s, g_idx):
            ok = (picked(g_idx) & (g_idx < i)) | (causal & (g_idx == i))
            return jnp.where(ok, s, MASKED)

        return per_block(sT, c, block_mask)

    oT = _attend_chunks(qT, k_ref, vt_ref, scratch, tk=MOBA_CHUNK, n_full=i // per_chunk,
                        mask_main=past, mask_tail=general)
    o_ref[...] = oT.T.astype(o_ref.dtype)


def _moba_attention(qk, vt, kmean):
    bsz, seq, _ = qk.shape
    t = MOBA_BLOCK
    nb = seq // t
    return pl.pallas_call(
        _moba_kernel,
        grid=(bsz, A_HEADS, nb),
        in_specs=[pl.BlockSpec((None, t, HEAD_DIM), lambda b, h, i: (b, i, QA_BLK + h)),
                  pl.BlockSpec((None, seq, HEAD_DIM), lambda b, h, i: (b, 0, KA_BLK + h)),
                  pl.BlockSpec((None, None, VT_ROWS, seq), lambda b, h, i: (b, h, 0, 0)),
                  pl.BlockSpec((None, nb, HEAD_DIM), lambda b, h, i: (b, 0, h))],
        out_specs=pl.BlockSpec((None, t, HEAD_DIM), lambda b, h, i: (b, i, h)),
        out_shape=jax.ShapeDtypeStruct((bsz, seq, A_W), BF16),
        scratch_shapes=[pltpu.VMEM((nb, t), F32)] + _attend_scratch(t, MOBA_CHUNK),
        compiler_params=_cparams("parallel", "parallel", "arbitrary"),
        name="moba",
    )(qk, qk, vt, kmean)


def _proj_dilated_kernel(h_ref, w_ref, c_ref, s_ref, q_ref, k_ref, v_ref, sc, *, d):
    acc = _dot(h_ref[...], w_ref[...])
    c, s = c_ref[...], s_ref[...]
    q_scale = HEAD_DIM ** -0.5 * LOG2E
    for j in range(acc.shape[1] // LANES):
        blk = acc[:, j * LANES:(j + 1) * LANES]
        if j < B_HEADS:
            blk = _rope128(blk, c, s) * q_scale
        elif j < 2 * B_HEADS:
            blk = _rope128(blk, c, s)
        sc[j] = blk
    rows = acc.shape[0] // d
    for r in range(d):
        for j in range(acc.shape[1] // LANES):
            dst = (q_ref, k_ref, v_ref)[j // B_HEADS]
            col = (j % B_HEADS) * LANES
            dst[r, :, col:col + LANES] = sc[j, pl.ds(r, rows, stride=d), :].astype(dst.dtype)


def _proj_dilated(h, w, rope_h, d, bsz, seq, tm=512):
    m, k = h.shape
    nt = seq // tm
    pos = pl.BlockSpec((tm, LANES), lambda i: (i % nt, 0))
    out = pl.BlockSpec((None, d, tm // d, B_W), lambda i: (i // nt, 0, i % nt, 0))
    return pl.pallas_call(
        functools.partial(_proj_dilated_kernel, d=d),
        grid=(m // tm,),
        in_specs=[pl.BlockSpec((tm, k), lambda i: (i, 0)), pl.BlockSpec(w.shape, lambda i: (0, 0)),
                  pos, pos],
        out_specs=[out] * 3,
        out_shape=[jax.ShapeDtypeStruct((bsz, d, seq // d, B_W), BF16)] * 3,
        scratch_shapes=[pltpu.VMEM((w.shape[1] // LANES, tm, LANES), F32)],
        compiler_params=_cparams("parallel"),
        name=f"proj_dilated_d{d}",
    )(h, w, *rope_h)


def _dilated_kernel(q_ref, kc_ref, kp_ref, vc_ref, vp_ref, o_ref, lse_ref, *, span):
    t, tp = q_ref.shape[0], kp_ref.shape[0]
    i = pl.program_id(2)
    dist = (lax.broadcasted_iota(jnp.int32, (t, t), 1)
            - lax.broadcasted_iota(jnp.int32, (t, t), 0))
    bias_c = jnp.where((dist >= 0) & (dist <= span), 0.0, MASKED)
    dist_p = (lax.broadcasted_iota(jnp.int32, (tp, t), 1) + tp
              - lax.broadcasted_iota(jnp.int32, (tp, t), 0))
    bias_p = jnp.where((dist_p <= span) & (i > 0), 0.0, MASKED)
    ones_c = jnp.ones((BF16_ROWS, t), BF16)
    ones_p = jnp.ones((BF16_ROWS, tp), BF16)

    def transposed(ref, sl):
        return ref[:, sl].astype(F32).T.astype(BF16)

    for j in range(B_HEADS):
        sl = slice(j * LANES, (j + 1) * LANES)
        qT = transposed(q_ref, sl)
        s_c = _dot(kc_ref[:, sl], qT) + bias_c
        s_p = _dot(kp_ref[:, sl], qT) + bias_p
        m = jnp.maximum(jnp.max(s_c, axis=0, keepdims=True), jnp.max(s_p, axis=0, keepdims=True))
        p_c = jnp.exp2(s_c - m).astype(BF16)
        p_p = jnp.exp2(s_p - m).astype(BF16)
        vt_c = jnp.concatenate([transposed(vc_ref, sl), ones_c], axis=0)
        vt_p = jnp.concatenate([transposed(vp_ref, sl), ones_p], axis=0)
        acc = _dot(vt_c, p_c) + _dot(vt_p, p_p)
        den = acc[V_DIM:V_DIM + 1, :]
        o_ref[:, sl] = (acc[:V_DIM, :] / den).T
        lse = m + jnp.log2(den)
        lse_ref[:, sl] = jnp.broadcast_to(lse, (LANES, t)).T


def _dilated_attention(q, k, v, span, t=512):
    bsz, d, length, _ = q.shape
    t = min(t, length)
    tp = B_QBLOCK
    assert span <= tp and t % tp == 0
    cur = pl.BlockSpec((None, None, t, B_W), lambda b, r, i: (b, r, i, 0))
    prev = pl.BlockSpec((None, None, tp, B_W),
                        lambda b, r, i: (b, r, jnp.maximum(i * (t // tp) - 1, 0), 0))
    return pl.pallas_call(
        functools.partial(_dilated_kernel, span=span),
        grid=(bsz, d, length // t),
        in_specs=[cur, cur, prev, cur, prev],
        out_specs=[cur, cur],
        out_shape=[jax.ShapeDtypeStruct(q.shape, F32)] * 2,
        compiler_params=_cparams("parallel", "parallel", "parallel"),
        name=f"dilated_d{d}",
    )(q, k, k, v, v)


def _natural_rows(ref, sc):
    d, rows = ref.shape[0], ref.shape[1]
    if d == 1:
        return ref[0]
    for r in range(d):
        for j in range(B_HEADS):
            sc[j, pl.ds(r, rows, stride=d), :] = ref[r, :, j * LANES:(j + 1) * LANES]
    return jnp.concatenate([sc[j] for j in range(B_HEADS)], axis=-1)


def _mixer_tail_kernel(x_ref, oa_ref, o0_ref, o1_ref, o2_ref, l0_ref, l1_ref, l2_ref, oc_ref,
                       g_ref, wpa_ref, wpb_ref, wpc_ref, wo_ref, y_ref, *scratch):
    o0, o1, o2, l0, l1, l2 = [
        _natural_rows(ref, sc)
        for ref, sc in zip((o0_ref, o1_ref, o2_ref, l0_ref, l1_ref, l2_ref), scratch)]
    mx = jnp.maximum(jnp.maximum(l0, l1), l2)
    e0, e1, e2 = jnp.exp2(l0 - mx), jnp.exp2(l1 - mx), jnp.exp2(l2 - mx)
    ob = (e0 * o0 + e1 * o1 + e2 * o2) / (e0 + e1 + e2)
    pa = _dot(oa_ref[...], wpa_ref[...])
    pb = _dot(ob.astype(BF16), wpb_ref[...])
    pc = _dot(oc_ref[...], wpc_ref[...])
    d = D_MODEL
    merged = (g_ref[:, 0:d].astype(F32) * pa + g_ref[:, d:2 * d].astype(F32) * pb
              + g_ref[:, 2 * d:3 * d].astype(F32) * pc)
    y_ref[...] = x_ref[...] + _dot(merged.astype(BF16), wo_ref[...])


def _mixer_tail(x, out_a, o_groups, lse_groups, out_c, gates, w_pa, w_pb, w_pc, w_o, seq, tm=256):
    m, d = x.shape
    nt = seq // tm
    row = lambda width: pl.BlockSpec((tm, width), lambda i: (i, 0))
    residue = lambda g: pl.BlockSpec((None, g.shape[1], tm // g.shape[1], B_W),
                                     lambda i: (i // nt, 0, i % nt, 0))
    weights = [_resident(w) for w in (w_pa, w_pb, w_pc, w_o)]
    groups = list(o_groups) + list(lse_groups)
    return pl.pallas_call(
        _mixer_tail_kernel,
        grid=(m // tm,),
        in_specs=([row(d), row(A_W)] + [residue(g) for g in groups]
                  + [row(C_W), row(3 * d)] + weights),
        out_specs=row(d),
        out_shape=jax.ShapeDtypeStruct((m, d), F32),
        scratch_shapes=[pltpu.VMEM((B_HEADS, tm, LANES), F32) for _ in groups],
        compiler_params=_cparams("parallel"),
        name="mixer_tail",
    )(x, out_a, *groups, out_c, gates, w_pa, w_pb, w_pc, w_o)


def _mem_kv_kernel(mem_ref, g_ref, wk_ref, wv_ref, k_ref, v_ref):
    memn = _rms(mem_ref[...], g_ref[...]).astype(BF16)
    k_ref[...] = _dot(memn, wk_ref[...]).astype(k_ref.dtype)
    v_ref[...] = _dot(memn, wv_ref[...]).astype(v_ref.dtype)


def _mem_kv(mem, g, wk, wv):
    bsz, n, d = mem.shape
    out = pl.BlockSpec((None, n, X_W), lambda b: (b, 0, 0))
    return pl.pallas_call(
        _mem_kv_kernel,
        grid=(bsz,),
        in_specs=[pl.BlockSpec((None, n, d), lambda b: (b, 0, 0)),
                  pl.BlockSpec((1, d), lambda b: (0, 0)),
                  pl.BlockSpec(wk.shape, lambda b: (0, 0)),
                  pl.BlockSpec(wv.shape, lambda b: (0, 0))],
        out_specs=[out, out],
        out_shape=[jax.ShapeDtypeStruct((bsz, n, X_W), BF16)] * 2,
        compiler_params=_cparams("parallel"),
        name="mem_kv",
    )(mem, g.reshape(1, d), wk, wv)


def _mem_attn_kernel(x_ref, g_ref, wq_ref, k_ref, v_ref, wo_ref, y_ref):
    x = x_ref[...]
    h = _rms(x, g_ref[...]).astype(BF16)
    q = (_dot(h, wq_ref[...]) * HEAD_DIM ** -0.5).astype(BF16)
    heads = []
    for hd in range(X_HEADS):
        sl = slice(hd * HEAD_DIM, (hd + 1) * HEAD_DIM)
        s = _dot_nt(q[:, sl], k_ref[:, sl])
        p = jnp.exp(s - jnp.max(s, axis=-1, keepdims=True))
        o = _dot(p.astype(BF16), v_ref[:, sl]) / jnp.sum(p, axis=-1, keepdims=True)
        heads.append(o.astype(BF16))
    y_ref[...] = x + _dot(jnp.concatenate(heads, axis=-1), wo_ref[...])


def _mem_attention(x, g, wq, kmem, vmem, wo, seq, tm=512):
    m, d = x.shape
    nt = seq // tm
    n = kmem.shape[1]
    kv = pl.BlockSpec((None, n, X_W), lambda i: (i // nt, 0, 0))
    return pl.pallas_call(
        _mem_attn_kernel,
        grid=(m // tm,),
        in_specs=[pl.BlockSpec((tm, d), lambda i: (i, 0)),
                  pl.BlockSpec((1, d), lambda i: (0, 0)),
                  pl.BlockSpec(wq.shape, lambda i: (0, 0)), kv, kv,
                  pl.BlockSpec(wo.shape, lambda i: (0, 0))],
        out_specs=pl.BlockSpec((tm, d), lambda i: (i, 0)),
        out_shape=jax.ShapeDtypeStruct((m, d), F32),
        compiler_params=_cparams("parallel"),
        name="mem_attention",
    )(x, g.reshape(1, d), wq, kmem, vmem, wo)


def _ffn_kernel(x_ref, halo_ref, g_ref, wg_ref, wv_ref, cwg_ref, cwv_ref, cbg_ref, cbv_ref,
                wd_ref, y_ref, h_sc, acc_sc, *, tiles_per_seq):
    i = pl.program_id(0)
    f = pl.program_id(1)
    tm = x_ref.shape[0]

    @pl.when(f == 0)
    def _():
        g = g_ref[...]
        keep = (i % tiles_per_seq != 0).astype(F32)
        h_sc[0:HALO, :] = (_rms(halo_ref[...], g) * keep).astype(h_sc.dtype)
        h_sc[HALO:, :] = _rms(x_ref[...], g).astype(h_sc.dtype)
        acc_sc[...] = jnp.zeros_like(acc_sc)

    h = h_sc[...]

    def conv(w_ref, cw_ref, cb_ref):
        u = _dot(h, w_ref[...])
        c = cb_ref[...]
        for tap in range(CONV_W):
            lo = HALO - (CONV_W - 1) + tap
            c = c + cw_ref[tap:tap + 1, :] * u[lo:lo + tm, :]
        return c

    act = jax.nn.silu(conv(wg_ref, cwg_ref, cbg_ref)) * conv(wv_ref, cwv_ref, cbv_ref)
    acc_sc[...] += _dot(act.astype(BF16), wd_ref[...])

    @pl.when(f == pl.num_programs(1) - 1)
    def _():
        y_ref[...] = x_ref[...] + acc_sc[...]


def _conv_ffn(x, g, w_up, conv_w, conv_b, w_down, seq, tm=512):
    m, d = x.shape
    tf = FFN_TF
    nf = D_FF_PAD // tf
    halo_blocks = tm // HALO
    return pl.pallas_call(
        functools.partial(_ffn_kernel, tiles_per_seq=seq // tm),
        grid=(m // tm, nf),
        in_specs=[pl.BlockSpec((tm, d), lambda i, f: (i, 0)),
                  pl.BlockSpec((HALO, d), lambda i, f: (jnp.maximum(i * halo_blocks - 1, 0), 0)),
                  pl.BlockSpec((1, d), lambda i, f: (0, 0)),
                  pl.BlockSpec((d, tf), lambda i, f: (0, f)),
                  pl.BlockSpec((d, tf), lambda i, f: (0, f + nf)),
                  pl.BlockSpec((CONV_W, tf), lambda i, f: (0, f)),
                  pl.BlockSpec((CONV_W, tf), lambda i, f: (0, f + nf)),
                  pl.BlockSpec((1, tf), lambda i, f: (0, f)),
                  pl.BlockSpec((1, tf), lambda i, f: (0, f + nf)),
                  pl.BlockSpec((tf, d), lambda i, f: (f, 0))],
        out_specs=pl.BlockSpec((tm, d), lambda i, f: (i, 0)),
        out_shape=jax.ShapeDtypeStruct((m, d), F32),
        scratch_shapes=[pltpu.VMEM((HALO + tm, d), BF16), pltpu.VMEM((tm, d), F32)],
        compiler_params=_cparams("parallel", "arbitrary"),
        name="conv_ffn",
    )(x, x, g.reshape(1, d), w_up, w_up, conv_w, conv_w, conv_b, conv_b, w_down)


def _rope_tables(seq):
    def angles(dim):
        inv_freq = jnp.exp(jnp.arange(0, dim, 2, dtype=F32) * (-math.log(ROPE_THETA) / dim))
        ang = jnp.arange(seq, dtype=F32)[:, None] * inv_freq[None, :]
        return jnp.cos(ang), jnp.sin(ang)

    cos_h, sin_h = angles(HEAD_DIM)
    rope_h = (jnp.concatenate([cos_h, cos_h], axis=-1), jnp.concatenate([-sin_h, sin_h], axis=-1))
    cos_r, sin_r = angles(ROPE_DIM)
    z = jnp.zeros_like(cos_r)
    rope_r = (jnp.concatenate([cos_r, cos_r, z, z], axis=-1),
              jnp.concatenate([-sin_r, z, z, z], axis=-1),
              jnp.concatenate([z, sin_r, z, z], axis=-1))
    return rope_h, rope_r


def _split_in(w_in):
    return [w_in[:, IN_OFFSETS[k]:IN_OFFSETS[k + 1]] for k in range(len(IN_WIDTHS))]


def _pad_cols(w, width):
    return jnp.pad(w, ((0, 0), (0, width - w.shape[1])))


def _layer_params(w_in, w_uq, w_ukv, w_up, conv_w, conv_b, w_down):
    qa, ka, va, qb, kb, vb, cq, ckv, kr, gates = _split_in(w_in)
    w_qk = jnp.concatenate([qa, ka], axis=1).astype(BF16)
    group_cols = lambda w, g: w[:, g * B_W:(g + 1) * B_W]
    w_b = [jnp.concatenate([group_cols(qb, g), group_cols(kb, g), group_cols(vb, g)],
                           axis=1).astype(BF16) for g in range(len(B_GROUPS))]
    w_down_in = jnp.concatenate([cq, ckv, _pad_cols(kr, LANES)], axis=1).astype(BF16)
    uq = w_uq.reshape(Q_LORA, C_HEADS, NOPE_DIM + ROPE_DIM)
    uq = jnp.pad(uq, ((0, 0), (0, 0), (0, C_QK - NOPE_DIM - ROPE_DIM)))
    ukv = w_ukv.reshape(KV_LORA, C_HEADS, NOPE_DIM + V_DIM)
    pad_ff = lambda w: jnp.pad(w, ((0, 0), (0, D_FF_PAD - D_FF)))
    two_halves = lambda w: jnp.concatenate([pad_ff(w[:, :D_FF]), pad_ff(w[:, D_FF:])], axis=1)
    return dict(
        w_qk=w_qk, w_va=va.astype(BF16), w_b=w_b, w_gates=gates.astype(BF16),
        w_down_in=w_down_in,
        w_uq=uq.reshape(Q_LORA, C_HEADS * C_QK).astype(BF16),
        w_uk=ukv[:, :, :NOPE_DIM].reshape(KV_LORA, C_HEADS * NOPE_DIM).astype(BF16),
        w_uv=ukv[:, :, NOPE_DIM:].reshape(KV_LORA, C_W).astype(BF16),
        w_up=two_halves(w_up).astype(BF16),
        conv_w=two_halves(conv_w),
        conv_b=two_halves(conv_b.reshape(1, -1)),
        w_down=jnp.pad(w_down, ((0, D_FF_PAD - D_FF), (0, 0))).astype(BF16),
    )


def _qk_col_scale():
    q_scale = HEAD_DIM ** -0.5
    parts = [jnp.full((A_W,), q_scale * LOG2E, F32), jnp.ones((A_W,), F32)]
    return jnp.concatenate(parts).reshape(1, QK_W)


def _mixer(x, g_mix, p, g_cq, g_ckv, w_pa, w_pb, w_pc, w_o, rope_h, rope_r, bsz, seq):
    m = x.shape[0]
    h = _rmsnorm(x, g_mix, BF16)
    qk = _matmul(h, p["w_qk"], _mm_rope_kernel, BF16, 1024, 1024, seq=seq,
                 extras=(("col", _qk_col_scale()), ("pos", rope_h[0]), ("pos", rope_h[1])),
                 name="proj_qk_rope")
    vt_a = _proj_vt(h, p["w_va"], A_HEADS, bsz, seq)
    gates = _matmul(h, p["w_gates"], _mm_sigmoid_kernel, BF16, 1024, 1024, name="proj_gates")
    cq, ckv, kr = _mla_down(h, p["w_down_in"], g_cq, g_ckv, rope_r, seq)
    q_c = _mla_q(cq, p["w_uq"], rope_r, seq)
    k_c, vt_c = _mla_kv(ckv, kr, p["w_uk"], p["w_uv"], bsz, seq)

    qk3 = qk.reshape(bsz, seq, QK_W)
    out_a = _moba_attention(qk3, vt_a, _kmean(qk3)).reshape(m, A_W)
    groups = []
    for (window, d), w_g in zip(B_GROUPS, p["w_b"]):
        q_g, k_g, v_g = _proj_dilated(h, w_g, rope_h, d, bsz, seq)
        groups.append(_dilated_attention(q_g, k_g, v_g, window // d))
    out_c = _flash_attention(q_c.reshape(bsz, seq, -1), k_c.reshape(bsz, seq, -1), vt_c,
                             C_HEADS, C_QK).reshape(m, C_W)
    return _mixer_tail(x, out_a, [g[0] for g in groups], [g[1] for g in groups], out_c, gates,
                       w_pa.astype(BF16), w_pb.astype(BF16), w_pc.astype(BF16), w_o.astype(BF16),
                       seq)


def kernel(x, mem, g_mix, w_in, g_cq, g_ckv, w_uq, w_ukv, w_pa, w_pb, w_pc, w_o, g_mem, g_memkv,
           w_xq, w_xk, w_xv, w_xo, g_ffn, w_up, conv_w, conv_b, w_down, g_final):
    bsz, seq, d = x.shape
    rope_h, rope_r = _rope_tables(seq)
    xf = x.reshape(bsz * seq, d)
    for l in range(DEPTH):
        p = _layer_params(w_in[l], w_uq[l], w_ukv[l], w_up[l], conv_w[l], conv_b[l], w_down[l])
        xf = _mixer(xf, g_mix[l], p, g_cq[l], g_ckv[l], w_pa[l], w_pb[l], w_pc[l], w_o[l],
                    rope_h, rope_r, bsz, seq)
        kmem, vmem = _mem_kv(mem, g_memkv[l], w_xk[l].astype(BF16), w_xv[l].astype(BF16))
        xf = _mem_attention(xf, g_mem[l], w_xq[l].astype(BF16), kmem, vmem,
                            w_xo[l].astype(BF16), seq)
        xf = _conv_ffn(xf, g_ffn[l], p["w_up"], p["conv_w"], p["conv_b"], p["w_down"], seq)
    return _rmsnorm(xf, g_final, F32).reshape(bsz, seq, d)
```

```python
import functools
import math

import jax
import jax.numpy as jnp
import numpy as np
from jax import lax
from jax.experimental import pallas as pl
from jax.experimental.pallas import tpu as pltpu

F32 = jnp.float32
BF16 = jnp.bfloat16

LANES = 128
SUBLANES = 8
VMEM_LIMIT = 56 * 1024 * 1024

D_MODEL = 2048
DEPTH = 2
HEAD_DIM = 128
ROPE_THETA = 10000.0
EPS = 1e-6

A_HEADS = 4
MOBA_BLOCK = 256
MOBA_TOPK = 3

B_GROUPS = ((128, 1), (512, 4), (2048, 16))
B_HEADS = 4
B_QBLOCK = 128

C_HEADS = 8
Q_LORA = 1536
KV_LORA = 512
NOPE_DIM = 128
ROPE_DIM = 64
V_DIM = 128

X_HEADS = 4
D_FF = 5504
CONV_W = 3

A_W = A_HEADS * HEAD_DIM
B_QKV_W = len(B_GROUPS) * B_HEADS * HEAD_DIM
B_W = B_HEADS * HEAD_DIM
C_W = C_HEADS * V_DIM
X_W = X_HEADS * HEAD_DIM
IN_WIDTHS = (A_W, A_W, A_W, B_QKV_W, B_QKV_W, B_QKV_W, Q_LORA, KV_LORA, ROPE_DIM, 3 * D_MODEL)
IN_OFFSETS = tuple(int(o) for o in np.cumsum((0,) + IN_WIDTHS))

QK_W = 2 * A_W
QA_BLK, KA_BLK = 0, A_W // LANES

C_QK = 2 * LANES
MASKED = -1e30
LOG2E = math.log2(math.e)
BF16_ROWS = 16
VT_ROWS = V_DIM + BF16_ROWS
MOBA_CHUNK = 4 * MOBA_BLOCK

D_FF_PAD = 5632
FFN_TF = 512
HALO = SUBLANES


def _cparams(*sem):
    return pltpu.CompilerParams(dimension_semantics=sem, vmem_limit_bytes=VMEM_LIMIT)


def _resident(arr):
    zeros = (0,) * arr.ndim
    return pl.BlockSpec(arr.shape, lambda *_: zeros, pipeline_mode=pl.Buffered(1))


def _dot(a, b):
    return jnp.dot(a, b, preferred_element_type=F32)


def _dot_nt(a, b):
    return lax.dot_general(a, b, (((1,), (1,)), ((), ())), preferred_element_type=F32)


def _rms(x, g):
    return x * lax.rsqrt(jnp.mean(x * x, axis=-1, keepdims=True) + EPS) * g


def _rmsnorm_kernel(x_ref, g_ref, o_ref):
    o_ref[...] = _rms(x_ref[...], g_ref[...]).astype(o_ref.dtype)


def _rmsnorm(x, g, out_dtype, tm=512):
    m, d = x.shape
    return pl.pallas_call(
        _rmsnorm_kernel,
        grid=(m // tm,),
        in_specs=[pl.BlockSpec((tm, d), lambda i: (i, 0)),
                  pl.BlockSpec((1, d), lambda i: (0, 0))],
        out_specs=pl.BlockSpec((tm, d), lambda i: (i, 0)),
        out_shape=jax.ShapeDtypeStruct((m, d), out_dtype),
        compiler_params=_cparams("parallel"),
        name="rmsnorm",
    )(x, g.reshape(1, d))


def _rope128(x, c, s):
    return x * c + pltpu.roll(x, HEAD_DIM // 2, 1) * s


def _rope64(x, c, sa, sb):
    half = ROPE_DIM // 2
    return x * c + pltpu.roll(x, LANES - half, 1) * sa + pltpu.roll(x, half, 1) * sb


def _mm_plain_kernel(a_ref, w_ref, o_ref):
    o_ref[...] = _dot(a_ref[...], w_ref[...]).astype(o_ref.dtype)


def _mm_sigmoid_kernel(a_ref, w_ref, o_ref):
    o_ref[...] = jax.nn.sigmoid(_dot(a_ref[...], w_ref[...])).astype(o_ref.dtype)


def _mm_rope_kernel(a_ref, w_ref, cs_ref, c_ref, s_ref, o_ref):
    acc = _dot(a_ref[...], w_ref[...])
    c = c_ref[...]
    s = s_ref[...]
    for j in range(acc.shape[1] // LANES):
        sl = slice(j * LANES, (j + 1) * LANES)
        o_ref[:, sl] = (_rope128(acc[:, sl], c, s) * cs_ref[:, sl]).astype(o_ref.dtype)


def _matmul(a, w, kernel, out_dtype, tm, tn, seq=None, extras=(), name="matmul"):
    m, k = a.shape
    n = w.shape[1]
    in_specs = [pl.BlockSpec((tm, k), lambda i, j: (i, 0)),
                pl.BlockSpec((k, tn), lambda i, j: (0, j))]
    args = [a, w]
    for kind, arr in extras:
        if kind == "col":
            in_specs.append(pl.BlockSpec((1, tn), lambda i, j: (0, j)))
        else:
            nt = seq // tm
            in_specs.append(pl.BlockSpec((tm, LANES), lambda i, j: (i % nt, 0)))
        args.append(arr)
    return pl.pallas_call(
        kernel,
        grid=(m // tm, n // tn),
        in_specs=in_specs,
        out_specs=pl.BlockSpec((tm, tn), lambda i, j: (i, j)),
        out_shape=jax.ShapeDtypeStruct((m, n), out_dtype),
        compiler_params=_cparams("parallel", "parallel"),
        name=name,
    )(*args)


def _mla_down_kernel(h_ref, w_ref, gq_ref, gkv_ref, c_ref, sa_ref, sb_ref,
                     cq_ref, ckv_ref, kr_ref):
    acc = _dot(h_ref[...], w_ref[...])
    cq_ref[...] = _rms(acc[:, :Q_LORA], gq_ref[...]).astype(cq_ref.dtype)
    ckv_ref[...] = _rms(acc[:, Q_LORA:Q_LORA + KV_LORA], gkv_ref[...]).astype(ckv_ref.dtype)
    kr = acc[:, Q_LORA + KV_LORA:]
    kr_ref[...] = _rope64(kr, c_ref[...], sa_ref[...], sb_ref[...]).astype(kr_ref.dtype)


def _mla_down(h, w, g_cq, g_ckv, rope_r, seq, tm=512):
    m, k = h.shape
    n = w.shape[1]
    nt = seq // tm
    row = lambda width: pl.BlockSpec((tm, width), lambda i: (i, 0))
    full = lambda r, c: pl.BlockSpec((r, c), lambda i: (0, 0))
    pos = pl.BlockSpec((tm, LANES), lambda i: (i % nt, 0))
    return pl.pallas_call(
        _mla_down_kernel,
        grid=(m // tm,),
        in_specs=[row(k), full(k, n), full(1, Q_LORA), full(1, KV_LORA), pos, pos, pos],
        out_specs=[row(Q_LORA), row(KV_LORA), row(LANES)],
        out_shape=[jax.ShapeDtypeStruct((m, Q_LORA), BF16),
                   jax.ShapeDtypeStruct((m, KV_LORA), BF16),
                   jax.ShapeDtypeStruct((m, LANES), BF16)],
        compiler_params=_cparams("parallel"),
        name="mla_down",
    )(h, w, g_cq.reshape(1, -1), g_ckv.reshape(1, -1), *rope_r)


def _mla_q_kernel(cq_ref, w_ref, c_ref, sa_ref, sb_ref, q_ref, *, scale):
    acc = _dot(cq_ref[...], w_ref[...])
    c, sa, sb = c_ref[...], sa_ref[...], sb_ref[...]
    for hd in range(C_HEADS):
        lo = hd * C_QK
        q_ref[:, lo:lo + LANES] = (acc[:, lo:lo + LANES] * scale).astype(q_ref.dtype)
        rope = _rope64(acc[:, lo + LANES:lo + C_QK], c, sa, sb)
        q_ref[:, lo + LANES:lo + C_QK] = (rope * scale).astype(q_ref.dtype)


def _mla_q(cq, w, rope_r, seq, tm=512):
    m, k = cq.shape
    n = w.shape[1]
    nt = seq // tm
    pos = pl.BlockSpec((tm, LANES), lambda i: (i % nt, 0))
    return pl.pallas_call(
        functools.partial(_mla_q_kernel, scale=(NOPE_DIM + ROPE_DIM) ** -0.5 * LOG2E),
        grid=(m // tm,),
        in_specs=[pl.BlockSpec((tm, k), lambda i: (i, 0)),
                  pl.BlockSpec((k, n), lambda i: (0, 0)), pos, pos, pos],
        out_specs=pl.BlockSpec((tm, n), lambda i: (i, 0)),
        out_shape=jax.ShapeDtypeStruct((m, n), BF16),
        compiler_params=_cparams("parallel"),
        name="mla_q",
    )(cq, w, *rope_r)


def _store_vt(v, vt_ref):
    vt = v.T
    for hd in range(vt_ref.shape[0]):
        vt_ref[hd, 0:V_DIM, :] = vt[hd * V_DIM:(hd + 1) * V_DIM, :].astype(vt_ref.dtype)
        vt_ref[hd, V_DIM:VT_ROWS, :] = jnp.ones((VT_ROWS - V_DIM, vt.shape[1]), vt_ref.dtype)


def _mla_kv_kernel(ckv_ref, kr_ref, wk_ref, wv_ref, k_ref, vt_ref):
    ckv = ckv_ref[...]
    kn = _dot(ckv, wk_ref[...])
    kr = kr_ref[...]
    for hd in range(C_HEADS):
        lo = hd * C_QK
        k_ref[:, lo:lo + LANES] = kn[:, hd * LANES:(hd + 1) * LANES].astype(k_ref.dtype)
        k_ref[:, lo + LANES:lo + C_QK] = kr
    _store_vt(_dot(ckv, wv_ref[...]), vt_ref)


def _mla_kv(ckv, kr, wk, wv, bsz, seq, tm=512):
    m, k = ckv.shape
    nt = seq // tm
    return pl.pallas_call(
        _mla_kv_kernel,
        grid=(m // tm,),
        in_specs=[pl.BlockSpec((tm, k), lambda i: (i, 0)),
                  pl.BlockSpec((tm, LANES), lambda i: (i, 0)),
                  pl.BlockSpec(wk.shape, lambda i: (0, 0)),
                  pl.BlockSpec(wv.shape, lambda i: (0, 0))],
        out_specs=[pl.BlockSpec((tm, C_HEADS * C_QK), lambda i: (i, 0)),
                   pl.BlockSpec((None, C_HEADS, VT_ROWS, tm), lambda i: (i // nt, 0, 0, i % nt))],
        out_shape=[jax.ShapeDtypeStruct((m, C_HEADS * C_QK), BF16),
                   jax.ShapeDtypeStruct((bsz, C_HEADS, VT_ROWS, seq), BF16)],
        compiler_params=_cparams("parallel"),
        name="mla_kv",
    )(ckv, kr, wk, wv)


def _proj_vt_kernel(h_ref, w_ref, vt_ref):
    _store_vt(_dot(h_ref[...], w_ref[...]), vt_ref)


def _proj_vt(h, w, heads, bsz, seq, tm=512):
    m, k = h.shape
    nt = seq // tm
    return pl.pallas_call(
        _proj_vt_kernel,
        grid=(m // tm,),
        in_specs=[pl.BlockSpec((tm, k), lambda i: (i, 0)), pl.BlockSpec(w.shape, lambda i: (0, 0))],
        out_specs=pl.BlockSpec((None, heads, VT_ROWS, tm), lambda i: (i // nt, 0, 0, i % nt)),
        out_shape=jax.ShapeDtypeStruct((bsz, heads, VT_ROWS, seq), BF16),
        compiler_params=_cparams("parallel"),
        name="proj_vt",
    )(h, w)


def _attend_chunks(qT, k_ref, vt_ref, scratch, *, tk, n_full, mask_main, mask_tail):
    m_sc, acc_sc, sa_sc, sb_sc, pa_sc, pb_sc, ala_sc, alb_sc = scratch
    s_bufs = (sa_sc, sb_sc)
    p_bufs = (pa_sc, pb_sc)
    al_bufs = (ala_sc, alb_sc)
    last_chunk = k_ref.shape[0] // tk - 1
    m_sc[...] = jnp.full(m_sc.shape, MASKED, F32)
    acc_sc[...] = jnp.zeros(acc_sc.shape, F32)
    for p_ref, al_ref in zip(p_bufs, al_bufs):
        p_ref[...] = jnp.zeros(p_ref.shape, p_ref.dtype)
        al_ref[...] = jnp.ones(al_ref.shape, F32)

    def rows(c):
        return pl.ds(pl.multiple_of(jnp.clip(c, 0, last_chunk) * tk, tk), tk)

    def scores(c):
        return _dot(k_ref[rows(c), :], qT)

    def flush(c, slot):
        acc_sc[...] = (al_bufs[slot][...] * acc_sc[...]
                       + _dot(vt_ref[:, rows(c)], p_bufs[slot][...]))

    def softmax(sT, slot):
        m_old = m_sc[...]
        m_new = jnp.maximum(m_old, jnp.max(sT, axis=0, keepdims=True))
        al_bufs[slot][...] = jnp.exp2(m_old - m_new)
        p_bufs[slot][...] = jnp.exp2(sT - m_new).astype(p_bufs[slot].dtype)
        m_sc[...] = m_new

    def step(tau, slot, mask, prefetch):
        flush(tau - 2, slot)
        if prefetch:
            s_bufs[1 - slot][...] = scores(tau + 1)
        sT = s_bufs[slot][...]
        softmax(sT if mask is None else mask(sT, tau), slot)

    def steps(tau0, count, mask, prefetch_last=True):
        for j in range(count):
            step(tau0 + j, j % 2, mask, prefetch_last or j < count - 1)

    sa_sc[...] = scores(0)
    quads = n_full // 4
    lax.fori_loop(0, quads, lambda u, c: (steps(4 * u, 4, mask_main), c)[1], 0)
    pairs = (n_full - 4 * quads) // 2
    lax.fori_loop(0, pairs, lambda u, c: (steps(4 * quads + 2 * u, 2, mask_main), c)[1], 0)
    tau = 4 * quads + 2 * pairs
    steps(tau, 2, mask_tail, prefetch_last=False)
    flush(tau, 0)
    flush(tau + 1, 1)
    acc = acc_sc[...]
    return acc[:V_DIM, :] / acc[V_DIM:V_DIM + 1, :]


def _attend_scratch(tq, tk):
    return [pltpu.VMEM((1, tq), F32), pltpu.VMEM((VT_ROWS, tq), F32),
            pltpu.VMEM((tk, tq), F32), pltpu.VMEM((tk, tq), F32),
            pltpu.VMEM((tk, tq), BF16), pltpu.VMEM((tk, tq), BF16),
            pltpu.VMEM((1, tq), F32), pltpu.VMEM((1, tq), F32)]


def _transpose_q(q_ref):
    return q_ref[...].astype(F32).T.astype(BF16)


def _flash_kernel(q_ref, k_ref, vt_ref, o_ref, *scratch, tq, tk):
    i = pl.program_id(2)

    def causal(sT, c):
        key = lax.broadcasted_iota(jnp.int32, sT.shape, 0) + c * tk
        qry = lax.broadcasted_iota(jnp.int32, sT.shape, 1) + i * tq
        return jnp.where(key <= qry, sT, MASKED)

    oT = _attend_chunks(_transpose_q(q_ref), k_ref, vt_ref, scratch, tk=tk,
                        n_full=i * (tq // tk), mask_main=None, mask_tail=causal)
    o_ref[...] = oT.T.astype(o_ref.dtype)


def _flash_attention(q, k, vt, heads, qk_w, tq=1024, tk=512):
    bsz, seq, _ = q.shape
    assert tq in (tk, 2 * tk) and seq % tq == 0
    return pl.pallas_call(
        functools.partial(_flash_kernel, tq=tq, tk=tk),
        grid=(bsz, heads, seq // tq),
        in_specs=[pl.BlockSpec((None, tq, qk_w), lambda b, h, i: (b, i, h)),
                  pl.BlockSpec((None, seq, qk_w), lambda b, h, i: (b, 0, h)),
                  pl.BlockSpec((None, None, VT_ROWS, seq), lambda b, h, i: (b, h, 0, 0))],
        out_specs=pl.BlockSpec((None, tq, V_DIM), lambda b, h, i: (b, i, h)),
        out_shape=jax.ShapeDtypeStruct((bsz, seq, heads * V_DIM), BF16),
        scratch_shapes=_attend_scratch(tq, tk),
        compiler_params=_cparams("parallel", "parallel", "arbitrary"),
        name="mla_flash",
    )(q, k, vt)


def _kmean_kernel(k_ref, o_ref):
    k = k_ref[...].astype(F32)
    o_ref[...] = jnp.mean(k.reshape(SUBLANES, MOBA_BLOCK, k.shape[-1]), axis=1)


def _kmean(qk):
    bsz, seq, _ = qk.shape
    rows = SUBLANES * MOBA_BLOCK
    return pl.pallas_call(
        _kmean_kernel,
        grid=(bsz, seq // rows),
        in_specs=[pl.BlockSpec((None, rows, A_W), lambda b, i: (b, i, KA_BLK * LANES // A_W))],
        out_specs=pl.BlockSpec((None, SUBLANES, A_W), lambda b, i: (b, i, 0)),
        out_shape=jax.ShapeDtypeStruct((bsz, seq // MOBA_BLOCK, A_W), F32),
        compiler_params=_cparams("parallel", "parallel"),
        name="moba_kmean",
    )(qk)


def _moba_kernel(q_ref, k_ref, vt_ref, km_ref, o_ref, sel_sc, *scratch):
    t = MOBA_BLOCK
    per_chunk = MOBA_CHUNK // t
    i = pl.program_id(2)
    nb = km_ref.shape[0]
    qT = _transpose_q(q_ref)

    km = km_ref[...]
    km_hi = km.astype(BF16)
    km_lo = (km - km_hi.astype(F32)).astype(BF16)
    gate = _dot(km_hi, qT) + _dot(km_lo, qT)
    blk = lax.broadcasted_iota(jnp.int32, gate.shape, 0)
    neg_inf = jnp.float32(-jnp.inf)
    g = jnp.where(blk < i, gate, neg_inf)
    sel = jnp.zeros(gate.shape, F32)
    for _ in range(MOBA_TOPK):
        mx = jnp.max(g, axis=0, keepdims=True)
        is_max = (g == mx) & (mx > neg_inf)
        first = jnp.min(jnp.where(is_max, blk, nb), axis=0, keepdims=True)
        pick = blk == first
        sel = jnp.where(pick, 1.0, sel)
        g = jnp.where(pick, neg_inf, g)
    sel_sc[...] = sel

    def per_block(sT, c, block_mask):
        parts = [block_mask(sT[j * t:(j + 1) * t, :], c * per_chunk + j) for j in range(per_chunk)]
        return jnp.concatenate(parts, axis=0)

    def picked(g_idx):
        return sel_sc[pl.ds(jnp.minimum(g_idx, nb - 1), 1), :] > 0.0

    def past(sT, c):
        return per_block(sT, c, lambda s, g_idx: jnp.where(picked(g_idx), s, MASKED))

    def general(sT, c):
        key = lax.broadcasted_iota(jnp.int32, (t, t), 0)
        qry = lax.broadcasted_iota(jnp.int32, (t, t), 1)
        causal = key <= qry

        def block_mask(s, g_idx):
            ok = (picked(g_idx) & (g_idx < i)) | (causal & (g_idx == i))
            return jnp.where(ok, s, MASKED)

        return per_block(sT, c, block_mask)

    oT = _attend_chunks(qT, k_ref, vt_ref, scratch, tk=MOBA_CHUNK, n_full=i // per_chunk,
                        mask_main=past, mask_tail=general)
    o_ref[...] = oT.T.astype(o_ref.dtype)


def _moba_attention(qk, vt, kmean):
    bsz, seq, _ = qk.shape
    t = MOBA_BLOCK
    nb = seq // t
    return pl.pallas_call(
        _moba_kernel,
        grid=(bsz, A_HEADS, nb),
        in_specs=[pl.BlockSpec((None, t, HEAD_DIM), lambda b, h, i: (b, i, QA_BLK + h)),
                  pl.BlockSpec((None, seq, HEAD_DIM), lambda b, h, i: (b, 0, KA_BLK + h)),
                  pl.BlockSpec((None, None, VT_ROWS, seq), lambda b, h, i: (b, h, 0, 0)),
                  pl.BlockSpec((None, nb, HEAD_DIM), lambda b, h, i: (b, 0, h))],
        out_specs=pl.BlockSpec((None, t, HEAD_DIM), lambda b, h, i: (b, i, h)),
        out_shape=jax.ShapeDtypeStruct((bsz, seq, A_W), BF16),
        scratch_shapes=[pltpu.VMEM((nb, t), F32)] + _attend_scratch(t, MOBA_CHUNK),
        compiler_params=_cparams("parallel", "parallel", "arbitrary"),
        name="moba",
    )(qk, qk, vt, kmean)


def _proj_dilated_kernel(h_ref, w_ref, c_ref, s_ref, q_ref, k_ref, v_ref, sc, *, d):
    acc = _dot(h_ref[...], w_ref[...])
    c, s = c_ref[...], s_ref[...]
    q_scale = HEAD_DIM ** -0.5 * LOG2E
    for j in range(acc.shape[1] // LANES):
        blk = acc[:, j * LANES:(j + 1) * LANES]
        if j < B_HEADS:
            blk = _rope128(blk, c, s) * q_scale
        elif j < 2 * B_HEADS:
            blk = _rope128(blk, c, s)
        sc[j] = blk
    rows = acc.shape[0] // d
    for r in range(d):
        for j in range(acc.shape[1] // LANES):
            dst = (q_ref, k_ref, v_ref)[j // B_HEADS]
            col = (j % B_HEADS) * LANES
            dst[r, :, col:col + LANES] = sc[j, pl.ds(r, rows, stride=d), :].astype(dst.dtype)


def _proj_dilated(h, w, rope_h, d, bsz, seq, tm=512):
    m, k = h.shape
    nt = seq // tm
    pos = pl.BlockSpec((tm, LANES), lambda i: (i % nt, 0))
    out = pl.BlockSpec((None, d, tm // d, B_W), lambda i: (i // nt, 0, i % nt, 0))
    return pl.pallas_call(
        functools.partial(_proj_dilated_kernel, d=d),
        grid=(m // tm,),
        in_specs=[pl.BlockSpec((tm, k), lambda i: (i, 0)), pl.BlockSpec(w.shape, lambda i: (0, 0)),
                  pos, pos],
        out_specs=[out] * 3,
        out_shape=[jax.ShapeDtypeStruct((bsz, d, seq // d, B_W), BF16)] * 3,
        scratch_shapes=[pltpu.VMEM((w.shape[1] // LANES, tm, LANES), F32)],
        compiler_params=_cparams("parallel"),
        name=f"proj_dilated_d{d}",
    )(h, w, *rope_h)


def _dilated_kernel(q_ref, kc_ref, kp_ref, vc_ref, vp_ref, o_ref, lse_ref, *, span):
    t, tp = q_ref.shape[0], kp_ref.shape[0]
    i = pl.program_id(2)
    dist = (lax.broadcasted_iota(jnp.int32, (t, t), 1)
            - lax.broadcasted_iota(jnp.int32, (t, t), 0))
    bias_c = jnp.where((dist >= 0) & (dist <= span), 0.0, MASKED)
    dist_p = (lax.broadcasted_iota(jnp.int32, (tp, t), 1) + tp
              - lax.broadcasted_iota(jnp.int32, (tp, t), 0))
    bias_p = jnp.where((dist_p <= span) & (i > 0), 0.0, MASKED)
    ones_c = jnp.ones((BF16_ROWS, t), BF16)
    ones_p = jnp.ones((BF16_ROWS, tp), BF16)

    def transposed(ref, sl):
        return ref[:, sl].astype(F32).T.astype(BF16)

    for j in range(B_HEADS):
        sl = slice(j * LANES, (j + 1) * LANES)
        qT = transposed(q_ref, sl)
        s_c = _dot(kc_ref[:, sl], qT) + bias_c
        s_p = _dot(kp_ref[:, sl], qT) + bias_p
        m = jnp.maximum(jnp.max(s_c, axis=0, keepdims=True), jnp.max(s_p, axis=0, keepdims=True))
        p_c = jnp.exp2(s_c - m).astype(BF16)
        p_p = jnp.exp2(s_p - m).astype(BF16)
        vt_c = jnp.concatenate([transposed(vc_ref, sl), ones_c], axis=0)
        vt_p = jnp.concatenate([transposed(vp_ref, sl), ones_p], axis=0)
        acc = _dot(vt_c, p_c) + _dot(vt_p, p_p)
        den = acc[V_DIM:V_DIM + 1, :]
        o_ref[:, sl] = (acc[:V_DIM, :] / den).T
        lse = m + jnp.log2(den)
        lse_ref[:, sl] = jnp.broadcast_to(lse, (LANES, t)).T


def _dilated_attention(q, k, v, span, t=512):
    bsz, d, length, _ = q.shape
    t = min(t, length)
    tp = B_QBLOCK
    assert span <= tp and t % tp == 0
    cur = pl.BlockSpec((None, None, t, B_W), lambda b, r, i: (b, r, i, 0))
    prev = pl.BlockSpec((None, None, tp, B_W),
                        lambda b, r, i: (b, r, jnp.maximum(i * (t // tp) - 1, 0), 0))
    return pl.pallas_call(
        functools.partial(_dilated_kernel, span=span),
        grid=(bsz, d, length // t),
        in_specs=[cur, cur, prev, cur, prev],
        out_specs=[cur, cur],
        out_shape=[jax.ShapeDtypeStruct(q.shape, F32)] * 2,
        compiler_params=_cparams("parallel", "parallel", "parallel"),
        name=f"dilated_d{d}",
    )(q, k, k, v, v)


def _natural_rows(ref, sc):
    d, rows = ref.shape[0], ref.shape[1]
    if d == 1:
        return ref[0]
    for r in range(d):
        for j in range(B_HEADS):
            sc[j, pl.ds(r, rows, stride=d), :] = ref[r, :, j * LANES:(j + 1) * LANES]
    return jnp.concatenate([sc[j] for j in range(B_HEADS)], axis=-1)


def _mixer_tail_kernel(x_ref, oa_ref, o0_ref, o1_ref, o2_ref, l0_ref, l1_ref, l2_ref, oc_ref,
                       g_ref, wpa_ref, wpb_ref, wpc_ref, wo_ref, y_ref, *scratch):
    o0, o1, o2, l0, l1, l2 = [
        _natural_rows(ref, sc)
        for ref, sc in zip((o0_ref, o1_ref, o2_ref, l0_ref, l1_ref, l2_ref), scratch)]
    mx = jnp.maximum(jnp.maximum(l0, l1), l2)
    e0, e1, e2 = jnp.exp2(l0 - mx), jnp.exp2(l1 - mx), jnp.exp2(l2 - mx)
    ob = (e0 * o0 + e1 * o1 + e2 * o2) / (e0 + e1 + e2)
    pa = _dot(oa_ref[...], wpa_ref[...])
    pb = _dot(ob.astype(BF16), wpb_ref[...])
    pc = _dot(oc_ref[...], wpc_ref[...])
    d = D_MODEL
    merged = (g_ref[:, 0:d].astype(F32) * pa + g_ref[:, d:2 * d].astype(F32) * pb
              + g_ref[:, 2 * d:3 * d].astype(F32) * pc)
    y_ref[...] = x_ref[...] + _dot(merged.astype(BF16), wo_ref[...])


def _mixer_tail(x, out_a, o_groups, lse_groups, out_c, gates, w_pa, w_pb, w_pc, w_o, seq, tm=256):
    m, d = x.shape
    nt = seq // tm
    row = lambda width: pl.BlockSpec((tm, width), lambda i: (i, 0))
    residue = lambda g: pl.BlockSpec((None, g.shape[1], tm // g.shape[1], B_W),
                                     lambda i: (i // nt, 0, i % nt, 0))
    weights = [_resident(w) for w in (w_pa, w_pb, w_pc, w_o)]
    groups = list(o_groups) + list(lse_groups)
    return pl.pallas_call(
        _mixer_tail_kernel,
        grid=(m // tm,),
        in_specs=([row(d), row(A_W)] + [residue(g) for g in groups]
                  + [row(C_W), row(3 * d)] + weights),
        out_specs=row(d),
        out_shape=jax.ShapeDtypeStruct((m, d), F32),
        scratch_shapes=[pltpu.VMEM((B_HEADS, tm, LANES), F32) for _ in groups],
        compiler_params=_cparams("parallel"),
        name="mixer_tail",
    )(x, out_a, *groups, out_c, gates, w_pa, w_pb, w_pc, w_o)


def _mem_kv_kernel(mem_ref, g_ref, wk_ref, wv_ref, k_ref, v_ref):
    memn = _rms(mem_ref[...], g_ref[...]).astype(BF16)
    k_ref[...] = _dot(memn, wk_ref[...]).astype(k_ref.dtype)
    v_ref[...] = _dot(memn, wv_ref[...]).astype(v_ref.dtype)


def _mem_kv(mem, g, wk, wv):
    bsz, n, d = mem.shape
    out = pl.BlockSpec((None, n, X_W), lambda b: (b, 0, 0))
    return pl.pallas_call(
        _mem_kv_kernel,
        grid=(bsz,),
        in_specs=[pl.BlockSpec((None, n, d), lambda b: (b, 0, 0)),
                  pl.BlockSpec((1, d), lambda b: (0, 0)),
                  pl.BlockSpec(wk.shape, lambda b: (0, 0)),
                  pl.BlockSpec(wv.shape, lambda b: (0, 0))],
        out_specs=[out, out],
        out_shape=[jax.ShapeDtypeStruct((bsz, n, X_W), BF16)] * 2,
        compiler_params=_cparams("parallel"),
        name="mem_kv",
    )(mem, g.reshape(1, d), wk, wv)


def _mem_attn_kernel(x_ref, g_ref, wq_ref, k_ref, v_ref, wo_ref, y_ref):
    x = x_ref[...]
    h = _rms(x, g_ref[...]).astype(BF16)
    q = (_dot(h, wq_ref[...]) * HEAD_DIM ** -0.5).astype(BF16)
    heads = []
    for hd in range(X_HEADS):
        sl = slice(hd * HEAD_DIM, (hd + 1) * HEAD_DIM)
        s = _dot_nt(q[:, sl], k_ref[:, sl])
        p = jnp.exp(s - jnp.max(s, axis=-1, keepdims=True))
        o = _dot(p.astype(BF16), v_ref[:, sl]) / jnp.sum(p, axis=-1, keepdims=True)
        heads.append(o.astype(BF16))
    y_ref[...] = x + _dot(jnp.concatenate(heads, axis=-1), wo_ref[...])


def _mem_attention(x, g, wq, kmem, vmem, wo, seq, tm=512):
    m, d = x.shape
    nt = seq // tm
    n = kmem.shape[1]
    kv = pl.BlockSpec((None, n, X_W), lambda i: (i // nt, 0, 0))
    return pl.pallas_call(
        _mem_attn_kernel,
        grid=(m // tm,),
        in_specs=[pl.BlockSpec((tm, d), lambda i: (i, 0)),
                  pl.BlockSpec((1, d), lambda i: (0, 0)),
                  pl.BlockSpec(wq.shape, lambda i: (0, 0)), kv, kv,
                  pl.BlockSpec(wo.shape, lambda i: (0, 0))],
        out_specs=pl.BlockSpec((tm, d), lambda i: (i, 0)),
        out_shape=jax.ShapeDtypeStruct((m, d), F32),
        compiler_params=_cparams("parallel"),
        name="mem_attention",
    )(x, g.reshape(1, d), wq, kmem, vmem, wo)


def _ffn_kernel(x_ref, halo_ref, g_ref, wg_ref, wv_ref, cwg_ref, cwv_ref, cbg_ref, cbv_ref,
                wd_ref, y_ref, h_sc, acc_sc, *, tiles_per_seq):
    i = pl.program_id(0)
    f = pl.program_id(1)
    tm = x_ref.shape[0]

    @pl.when(f == 0)
    def _():
        g = g_ref[...]
        keep = (i % tiles_per_seq != 0).astype(F32)
        h_sc[0:HALO, :] = (_rms(halo_ref[...], g) * keep).astype(h_sc.dtype)
        h_sc[HALO:, :] = _rms(x_ref[...], g).astype(h_sc.dtype)
        acc_sc[...] = jnp.zeros_like(acc_sc)

    h = h_sc[...]

    def conv(w_ref, cw_ref, cb_ref):
        u = _dot(h, w_ref[...])
        c = cb_ref[...]
        for tap in range(CONV_W):
            lo = HALO - (CONV_W - 1) + tap
            c = c + cw_ref[tap:tap + 1, :] * u[lo:lo + tm, :]
        return c

    act = jax.nn.silu(conv(wg_ref, cwg_ref, cbg_ref)) * conv(wv_ref, cwv_ref, cbv_ref)
    acc_sc[...] += _dot(act.astype(BF16), wd_ref[...])

    @pl.when(f == pl.num_programs(1) - 1)
    def _():
        y_ref[...] = x_ref[...] + acc_sc[...]


def _conv_ffn(x, g, w_up, conv_w, conv_b, w_down, seq, tm=512):
    m, d = x.shape
    tf = FFN_TF
    nf = D_FF_PAD // tf
    halo_blocks = tm // HALO
    return pl.pallas_call(
        functools.partial(_ffn_kernel, tiles_per_seq=seq // tm),
        grid=(m // tm, nf),
        in_specs=[pl.BlockSpec((tm, d), lambda i, f: (i, 0)),
                  pl.BlockSpec((HALO, d), lambda i, f: (jnp.maximum(i * halo_blocks - 1, 0), 0)),
                  pl.BlockSpec((1, d), lambda i, f: (0, 0)),
                  pl.BlockSpec((d, tf), lambda i, f: (0, f)),
                  pl.BlockSpec((d, tf), lambda i, f: (0, f + nf)),
                  pl.BlockSpec((CONV_W, tf), lambda i, f: (0, f)),
                  pl.BlockSpec((CONV_W, tf), lambda i, f: (0, f + nf)),
                  pl.BlockSpec((1, tf), lambda i, f: (0, f)),
                  pl.BlockSpec((1, tf), lambda i, f: (0, f + nf)),
                  pl.BlockSpec((tf, d), lambda i, f: (f, 0))],
        out_specs=pl.BlockSpec((tm, d), lambda i, f: (i, 0)),
        out_shape=jax.ShapeDtypeStruct((m, d), F32),
        scratch_shapes=[pltpu.VMEM((HALO + tm, d), BF16), pltpu.VMEM((tm, d), F32)],
        compiler_params=_cparams("parallel", "arbitrary"),
        name="conv_ffn",
    )(x, x, g.reshape(1, d), w_up, w_up, conv_w, conv_w, conv_b, conv_b, w_down)


def _rope_tables(seq):
    def angles(dim):
        inv_freq = jnp.exp(jnp.arange(0, dim, 2, dtype=F32) * (-math.log(ROPE_THETA) / dim))
        ang = jnp.arange(seq, dtype=F32)[:, None] * inv_freq[None, :]
        return jnp.cos(ang), jnp.sin(ang)

    cos_h, sin_h = angles(HEAD_DIM)
    rope_h = (jnp.concatenate([cos_h, cos_h], axis=-1), jnp.concatenate([-sin_h, sin_h], axis=-1))
    cos_r, sin_r = angles(ROPE_DIM)
    z = jnp.zeros_like(cos_r)
    rope_r = (jnp.concatenate([cos_r, cos_r, z, z], axis=-1),
              jnp.concatenate([-sin_r, z, z, z], axis=-1),
              jnp.concatenate([z, sin_r, z, z], axis=-1))
    return rope_h, rope_r


def _split_in(w_in):
    return [w_in[:, IN_OFFSETS[k]:IN_OFFSETS[k + 1]] for k in range(len(IN_WIDTHS))]


def _pad_cols(w, width):
    return jnp.pad(w, ((0, 0), (0, width - w.shape[1])))


def _layer_params(w_in, w_uq, w_ukv, w_up, conv_w, conv_b, w_down):
    qa, ka, va, qb, kb, vb, cq, ckv, kr, gates = _split_in(w_in)
    w_qk = jnp.concatenate([qa, ka], axis=1).astype(BF16)
    group_cols = lambda w, g: w[:, g * B_W:(g + 1) * B_W]
    w_b = [jnp.concatenate([group_cols(qb, g), group_cols(kb, g), group_cols(vb, g)],
                           axis=1).astype(BF16) for g in range(len(B_GROUPS))]
    w_down_in = jnp.concatenate([cq, ckv, _pad_cols(kr, LANES)], axis=1).astype(BF16)
    uq = w_uq.reshape(Q_LORA, C_HEADS, NOPE_DIM + ROPE_DIM)
    uq = jnp.pad(uq, ((0, 0), (0, 0), (0, C_QK - NOPE_DIM - ROPE_DIM)))
    ukv = w_ukv.reshape(KV_LORA, C_HEADS, NOPE_DIM + V_DIM)
    pad_ff = lambda w: jnp.pad(w, ((0, 0), (0, D_FF_PAD - D_FF)))
    two_halves = lambda w: jnp.concatenate([pad_ff(w[:, :D_FF]), pad_ff(w[:, D_FF:])], axis=1)
    return dict(
        w_qk=w_qk, w_va=va.astype(BF16), w_b=w_b, w_gates=gates.astype(BF16),
        w_down_in=w_down_in,
        w_uq=uq.reshape(Q_LORA, C_HEADS * C_QK).astype(BF16),
        w_uk=ukv[:, :, :NOPE_DIM].reshape(KV_LORA, C_HEADS * NOPE_DIM).astype(BF16),
        w_uv=ukv[:, :, NOPE_DIM:].reshape(KV_LORA, C_W).astype(BF16),
        w_up=two_halves(w_up).astype(BF16),
        conv_w=two_halves(conv_w),
        conv_b=two_halves(conv_b.reshape(1, -1)),
        w_down=jnp.pad(w_down, ((0, D_FF_PAD - D_FF), (0, 0))).astype(BF16),
    )


def _qk_col_scale():
    q_scale = HEAD_DIM ** -0.5
    parts = [jnp.full((A_W,), q_scale * LOG2E, F32), jnp.ones((A_W,), F32)]
    return jnp.concatenate(parts).reshape(1, QK_W)


def _mixer(x, g_mix, p, g_cq, g_ckv, w_pa, w_pb, w_pc, w_o, rope_h, rope_r, bsz, seq):
    m = x.shape[0]
    h = _rmsnorm(x, g_mix, BF16)
    qk = _matmul(h, p["w_qk"], _mm_rope_kernel, BF16, 1024, 1024, seq=seq,
                 extras=(("col", _qk_col_scale()), ("pos", rope_h[0]), ("pos", rope_h[1])),
                 name="proj_qk_rope")
    vt_a = _proj_vt(h, p["w_va"], A_HEADS, bsz, seq)
    gates = _matmul(h, p["w_gates"], _mm_sigmoid_kernel, BF16, 1024, 1024, name="proj_gates")
    cq, ckv, kr = _mla_down(h, p["w_down_in"], g_cq, g_ckv, rope_r, seq)
    q_c = _mla_q(cq, p["w_uq"], rope_r, seq)
    k_c, vt_c = _mla_kv(ckv, kr, p["w_uk"], p["w_uv"], bsz, seq)

    qk3 = qk.reshape(bsz, seq, QK_W)
    out_a = _moba_attention(qk3, vt_a, _kmean(qk3)).reshape(m, A_W)
    groups = []
    for (window, d), w_g in zip(B_GROUPS, p["w_b"]):
        q_g, k_g, v_g = _proj_dilated(h, w_g, rope_h, d, bsz, seq)
        groups.append(_dilated_attention(q_g, k_g, v_g, window // d))
    out_c = _flash_attention(q_c.reshape(bsz, seq, -1), k_c.reshape(bsz, seq, -1), vt_c,
                             C_HEADS, C_QK).reshape(m, C_W)
    return _mixer_tail(x, out_a, [g[0] for g in groups], [g[1] for g in groups], out_c, gates,
                       w_pa.astype(BF16), w_pb.astype(BF16), w_pc.astype(BF16), w_o.astype(BF16),
                       seq)


def kernel(x, mem, g_mix, w_in, g_cq, g_ckv, w_uq, w_ukv, w_pa, w_pb, w_pc, w_o, g_mem, g_memkv,
           w_xq, w_xk, w_xv, w_xo, g_ffn, w_up, conv_w, conv_b, w_down, g_final):
    bsz, seq, d = x.shape
    rope_h, rope_r = _rope_tables(seq)
    xf = x.reshape(bsz * seq, d)
    for l in range(DEPTH):
        p = _layer_params(w_in[l], w_uq[l], w_ukv[l], w_up[l], conv_w[l], conv_b[l], w_down[l])
        xf = _mixer(xf, g_mix[l], p, g_cq[l], g_ckv[l], w_pa[l], w_pb[l], w_pc[l], w_o[l],
                    rope_h, rope_r, bsz, seq)
        kmem, vmem = _mem_kv(mem, g_memkv[l], w_xk[l].astype(BF16), w_xv[l].astype(BF16))
        xf = _mem_attention(xf, g_mem[l], w_xq[l].astype(BF16), kmem, vmem,
                            w_xo[l].astype(BF16), seq)
        xf = _conv_ffn(xf, g_ffn[l], p["w_up"], p["conv_w"], p["conv_b"], p["w_down"], seq)
    return _rmsnorm(xf, g_final, F32).reshape(bsz, seq, d)
```

```python
import functools
import math

import jax
import jax.numpy as jnp
import numpy as np
from jax import lax
from jax.experimental import pallas as pl
from jax.experimental.pallas import tpu as pltpu

F32 = jnp.float32
BF16 = jnp.bfloat16

LANES = 128
SUBLANES = 8
VMEM_LIMIT = 56 * 1024 * 1024

D_MODEL = 2048
DEPTH = 2
HEAD_DIM = 128
ROPE_THETA = 10000.0
EPS = 1e-6

A_HEADS = 4
MOBA_BLOCK = 256
MOBA_TOPK = 3

B_GROUPS = ((128, 1), (512, 4), (2048, 16))
B_HEADS = 4
B_QBLOCK = 128

C_HEADS = 8
Q_LORA = 1536
KV_LORA = 512
NOPE_DIM = 128
ROPE_DIM = 64
V_DIM = 128

X_HEADS = 4
D_FF = 5504
CONV_W = 3

A_W = A_HEADS * HEAD_DIM
B_QKV_W = len(B_GROUPS) * B_HEADS * HEAD_DIM
B_W = B_HEADS * HEAD_DIM
C_W = C_HEADS * V_DIM
X_W = X_HEADS * HEAD_DIM
IN_WIDTHS = (A_W, A_W, A_W, B_QKV_W, B_QKV_W, B_QKV_W, Q_LORA, KV_LORA, ROPE_DIM, 3 * D_MODEL)
IN_OFFSETS = tuple(int(o) for o in np.cumsum((0,) + IN_WIDTHS))

QK_W = 2 * A_W
QA_BLK, KA_BLK = 0, A_W // LANES

C_QK = 2 * LANES
MASKED = -1e30
LOG2E = math.log2(math.e)
BF16_ROWS = 16
VT_ROWS = V_DIM + BF16_ROWS
MOBA_CHUNK = 2 * MOBA_BLOCK

D_FF_PAD = 5632
FFN_TF = 512
HALO = SUBLANES


def _cparams(*sem):
    return pltpu.CompilerParams(dimension_semantics=sem, vmem_limit_bytes=VMEM_LIMIT)


def _resident(arr):
    zeros = (0,) * arr.ndim
    return pl.BlockSpec(arr.shape, lambda *_: zeros, pipeline_mode=pl.Buffered(1))


def _dot(a, b):
    return jnp.dot(a, b, preferred_element_type=F32)


def _dot_nt(a, b):
    return lax.dot_general(a, b, (((1,), (1,)), ((), ())), preferred_element_type=F32)


def _rms(x, g):
    return x * lax.rsqrt(jnp.mean(x * x, axis=-1, keepdims=True) + EPS) * g


def _rmsnorm_kernel(x_ref, g_ref, o_ref):
    o_ref[...] = _rms(x_ref[...], g_ref[...]).astype(o_ref.dtype)


def _rmsnorm(x, g, out_dtype, tm=512):
    m, d = x.shape
    return pl.pallas_call(
        _rmsnorm_kernel,
        grid=(m // tm,),
        in_specs=[pl.BlockSpec((tm, d), lambda i: (i, 0)),
                  pl.BlockSpec((1, d), lambda i: (0, 0))],
        out_specs=pl.BlockSpec((tm, d), lambda i: (i, 0)),
        out_shape=jax.ShapeDtypeStruct((m, d), out_dtype),
        compiler_params=_cparams("parallel"),
        name="rmsnorm",
    )(x, g.reshape(1, d))


def _rope128(x, c, s):
    return x * c + pltpu.roll(x, HEAD_DIM // 2, 1) * s


def _rope64(x, c, sa, sb):
    half = ROPE_DIM // 2
    return x * c + pltpu.roll(x, LANES - half, 1) * sa + pltpu.roll(x, half, 1) * sb


def _mm_plain_kernel(a_ref, w_ref, o_ref):
    o_ref[...] = _dot(a_ref[...], w_ref[...]).astype(o_ref.dtype)


def _mm_sigmoid_kernel(a_ref, w_ref, o_ref):
    o_ref[...] = jax.nn.sigmoid(_dot(a_ref[...], w_ref[...])).astype(o_ref.dtype)


def _mm_rope_kernel(a_ref, w_ref, cs_ref, c_ref, s_ref, o_ref):
    acc = _dot(a_ref[...], w_ref[...])
    c = c_ref[...]
    s = s_ref[...]
    for j in range(acc.shape[1] // LANES):
        sl = slice(j * LANES, (j + 1) * LANES)
        o_ref[:, sl] = (_rope128(acc[:, sl], c, s) * cs_ref[:, sl]).astype(o_ref.dtype)


def _matmul(a, w, kernel, out_dtype, tm, tn, seq=None, extras=(), name="matmul"):
    m, k = a.shape
    n = w.shape[1]
    in_specs = [pl.BlockSpec((tm, k), lambda i, j: (i, 0)),
                pl.BlockSpec((k, tn), lambda i, j: (0, j))]
    args = [a, w]
    for kind, arr in extras:
        if kind == "col":
            in_specs.append(pl.BlockSpec((1, tn), lambda i, j: (0, j)))
        else:
            nt = seq // tm
            in_specs.append(pl.BlockSpec((tm, LANES), lambda i, j: (i % nt, 0)))
        args.append(arr)
    return pl.pallas_call(
        kernel,
        grid=(m // tm, n // tn),
        in_specs=in_specs,
        out_specs=pl.BlockSpec((tm, tn), lambda i, j: (i, j)),
        out_shape=jax.ShapeDtypeStruct((m, n), out_dtype),
        compiler_params=_cparams("parallel", "parallel"),
        name=name,
    )(*args)


def _mla_down_kernel(h_ref, w_ref, gq_ref, gkv_ref, c_ref, sa_ref, sb_ref,
                     cq_ref, ckv_ref, kr_ref):
    acc = _dot(h_ref[...], w_ref[...])
    cq_ref[...] = _rms(acc[:, :Q_LORA], gq_ref[...]).astype(cq_ref.dtype)
    ckv_ref[...] = _rms(acc[:, Q_LORA:Q_LORA + KV_LORA], gkv_ref[...]).astype(ckv_ref.dtype)
    kr = acc[:, Q_LORA + KV_LORA:]
    kr_ref[...] = _rope64(kr, c_ref[...], sa_ref[...], sb_ref[...]).astype(kr_ref.dtype)


def _mla_down(h, w, g_cq, g_ckv, rope_r, seq, tm=512):
    m, k = h.shape
    n = w.shape[1]
    nt = seq // tm
    row = lambda width: pl.BlockSpec((tm, width), lambda i: (i, 0))
    full = lambda r, c: pl.BlockSpec((r, c), lambda i: (0, 0))
    pos = pl.BlockSpec((tm, LANES), lambda i: (i % nt, 0))
    return pl.pallas_call(
        _mla_down_kernel,
        grid=(m // tm,),
        in_specs=[row(k), full(k, n), full(1, Q_LORA), full(1, KV_LORA), pos, pos, pos],
        out_specs=[row(Q_LORA), row(KV_LORA), row(LANES)],
        out_shape=[jax.ShapeDtypeStruct((m, Q_LORA), BF16),
                   jax.ShapeDtypeStruct((m, KV_LORA), BF16),
                   jax.ShapeDtypeStruct((m, LANES), BF16)],
        compiler_params=_cparams("parallel"),
        name="mla_down",
    )(h, w, g_cq.reshape(1, -1), g_ckv.reshape(1, -1), *rope_r)


def _mla_q_kernel(cq_ref, w_ref, c_ref, sa_ref, sb_ref, q_ref, *, scale):
    acc = _dot(cq_ref[...], w_ref[...])
    c, sa, sb = c_ref[...], sa_ref[...], sb_ref[...]
    for hd in range(C_HEADS):
        lo = hd * C_QK
        q_ref[:, lo:lo + LANES] = (acc[:, lo:lo + LANES] * scale).astype(q_ref.dtype)
        rope = _rope64(acc[:, lo + LANES:lo + C_QK], c, sa, sb)
        q_ref[:, lo + LANES:lo + C_QK] = (rope * scale).astype(q_ref.dtype)


def _mla_q(cq, w, rope_r, seq, tm=512):
    m, k = cq.shape
    n = w.shape[1]
    nt = seq // tm
    pos = pl.BlockSpec((tm, LANES), lambda i: (i % nt, 0))
    return pl.pallas_call(
        functools.partial(_mla_q_kernel, scale=(NOPE_DIM + ROPE_DIM) ** -0.5 * LOG2E),
        grid=(m // tm,),
        in_specs=[pl.BlockSpec((tm, k), lambda i: (i, 0)),
                  pl.BlockSpec((k, n), lambda i: (0, 0)), pos, pos, pos],
        out_specs=pl.BlockSpec((tm, n), lambda i: (i, 0)),
        out_shape=jax.ShapeDtypeStruct((m, n), BF16),
        compiler_params=_cparams("parallel"),
        name="mla_q",
    )(cq, w, *rope_r)


def _store_vt(v, vt_ref):
    vt = v.T
    for hd in range(vt_ref.shape[0]):
        vt_ref[hd, 0:V_DIM, :] = vt[hd * V_DIM:(hd + 1) * V_DIM, :].astype(vt_ref.dtype)
        vt_ref[hd, V_DIM:VT_ROWS, :] = jnp.ones((VT_ROWS - V_DIM, vt.shape[1]), vt_ref.dtype)


def _mla_kv_kernel(ckv_ref, kr_ref, wk_ref, wv_ref, k_ref, vt_ref):
    ckv = ckv_ref[...]
    kn = _dot(ckv, wk_ref[...])
    kr = kr_ref[...]
    for hd in range(C_HEADS):
        lo = hd * C_QK
        k_ref[:, lo:lo + LANES] = kn[:, hd * LANES:(hd + 1) * LANES].astype(k_ref.dtype)
        k_ref[:, lo + LANES:lo + C_QK] = kr
    _store_vt(_dot(ckv, wv_ref[...]), vt_ref)


def _mla_kv(ckv, kr, wk, wv, bsz, seq, tm=512):
    m, k = ckv.shape
    nt = seq // tm
    return pl.pallas_call(
        _mla_kv_kernel,
        grid=(m // tm,),
        in_specs=[pl.BlockSpec((tm, k), lambda i: (i, 0)),
                  pl.BlockSpec((tm, LANES), lambda i: (i, 0)),
                  pl.BlockSpec(wk.shape, lambda i: (0, 0)),
                  pl.BlockSpec(wv.shape, lambda i: (0, 0))],
        out_specs=[pl.BlockSpec((tm, C_HEADS * C_QK), lambda i: (i, 0)),
                   pl.BlockSpec((None, C_HEADS, VT_ROWS, tm), lambda i: (i // nt, 0, 0, i % nt))],
        out_shape=[jax.ShapeDtypeStruct((m, C_HEADS * C_QK), BF16),
                   jax.ShapeDtypeStruct((bsz, C_HEADS, VT_ROWS, seq), BF16)],
        compiler_params=_cparams("parallel"),
        name="mla_kv",
    )(ckv, kr, wk, wv)


def _proj_vt_kernel(h_ref, w_ref, vt_ref):
    _store_vt(_dot(h_ref[...], w_ref[...]), vt_ref)


def _proj_vt(h, w, heads, bsz, seq, tm=512):
    m, k = h.shape
    nt = seq // tm
    return pl.pallas_call(
        _proj_vt_kernel,
        grid=(m // tm,),
        in_specs=[pl.BlockSpec((tm, k), lambda i: (i, 0)), pl.BlockSpec(w.shape, lambda i: (0, 0))],
        out_specs=pl.BlockSpec((None, heads, VT_ROWS, tm), lambda i: (i // nt, 0, 0, i % nt)),
        out_shape=jax.ShapeDtypeStruct((bsz, heads, VT_ROWS, seq), BF16),
        compiler_params=_cparams("parallel"),
        name="proj_vt",
    )(h, w)


def _attend_chunks(qT, k_ref, vt_ref, scratch, *, tk, n_full, mask_main, mask_tail,
                   unroll=4, tail_steps=2):
    m_sc, acc_sc, sa_sc, sb_sc, pa_sc, pb_sc, ala_sc, alb_sc = scratch
    s_bufs = (sa_sc, sb_sc)
    p_bufs = (pa_sc, pb_sc)
    al_bufs = (ala_sc, alb_sc)
    last_chunk = k_ref.shape[0] // tk - 1
    m_sc[...] = jnp.full(m_sc.shape, MASKED, F32)
    acc_sc[...] = jnp.zeros(acc_sc.shape, F32)
    for p_ref, al_ref in zip(p_bufs, al_bufs):
        p_ref[...] = jnp.zeros(p_ref.shape, p_ref.dtype)
        al_ref[...] = jnp.ones(al_ref.shape, F32)

    def rows(c):
        return pl.ds(pl.multiple_of(jnp.clip(c, 0, last_chunk) * tk, tk), tk)

    def scores(c):
        return _dot(k_ref[rows(c), :], qT)

    def flush(c, slot):
        acc_sc[...] = (al_bufs[slot][...] * acc_sc[...]
                       + _dot(vt_ref[:, rows(c)], p_bufs[slot][...]))

    def softmax(sT, slot):
        m_old = m_sc[...]
        m_new = jnp.maximum(m_old, jnp.max(sT, axis=0, keepdims=True))
        al_bufs[slot][...] = jnp.exp2(m_old - m_new)
        p_bufs[slot][...] = jnp.exp2(sT - m_new).astype(p_bufs[slot].dtype)
        m_sc[...] = m_new

    def step(tau, slot, mask, prefetch):
        flush(tau - 2, slot)
        if prefetch:
            s_bufs[1 - slot][...] = scores(tau + 1)
        sT = s_bufs[slot][...]
        softmax(sT if mask is None else mask(sT, tau), slot)

    def steps(tau0, count, mask, prefetch_last=True):
        for j in range(count):
            step(tau0 + j, j % 2, mask, prefetch_last or j < count - 1)

    sa_sc[...] = scores(0)
    trips = n_full // unroll
    lax.fori_loop(0, trips, lambda u, c: (steps(unroll * u, unroll, mask_main), c)[1], 0)
    done = unroll * trips
    pairs = (n_full - done) // 2
    lax.fori_loop(0, pairs, lambda u, c: (steps(done + 2 * u, 2, mask_main), c)[1], 0)
    tau = done + 2 * pairs
    steps(tau, tail_steps, mask_tail, prefetch_last=False)
    flush(tau + tail_steps - 2, 0)
    flush(tau + tail_steps - 1, 1)
    acc = acc_sc[...]
    return acc[:V_DIM, :] / acc[V_DIM:V_DIM + 1, :]


def _attend_scratch(tq, tk):
    return [pltpu.VMEM((1, tq), F32), pltpu.VMEM((VT_ROWS, tq), F32),
            pltpu.VMEM((tk, tq), F32), pltpu.VMEM((tk, tq), F32),
            pltpu.VMEM((tk, tq), BF16), pltpu.VMEM((tk, tq), BF16),
            pltpu.VMEM((1, tq), F32), pltpu.VMEM((1, tq), F32)]


def _transpose_q(q_ref):
    return q_ref[...].astype(F32).T.astype(BF16)


def _flash_kernel(q_ref, k_ref, vt_ref, o_ref, *scratch, tq, tk):
    i = pl.program_id(2)

    def causal(sT, c):
        key = lax.broadcasted_iota(jnp.int32, sT.shape, 0) + c * tk
        qry = lax.broadcasted_iota(jnp.int32, sT.shape, 1) + i * tq
        return jnp.where(key <= qry, sT, MASKED)

    per_tile = tq // tk
    oT = _attend_chunks(_transpose_q(q_ref), k_ref, vt_ref, scratch, tk=tk, n_full=i * per_tile,
                        mask_main=None, mask_tail=causal, unroll=8, tail_steps=per_tile)
    o_ref[...] = oT.T.astype(o_ref.dtype)


def _flash_attention(q, k, vt, heads, qk_w, tq=1024, tk=256):
    bsz, seq, _ = q.shape
    assert tq % (2 * tk) == 0 and seq % tq == 0
    return pl.pallas_call(
        functools.partial(_flash_kernel, tq=tq, tk=tk),
        grid=(bsz, heads, seq // tq),
        in_specs=[pl.BlockSpec((None, tq, qk_w), lambda b, h, i: (b, i, h)),
                  pl.BlockSpec((None, seq, qk_w), lambda b, h, i: (b, 0, h)),
                  pl.BlockSpec((None, None, VT_ROWS, seq), lambda b, h, i: (b, h, 0, 0))],
        out_specs=pl.BlockSpec((None, tq, V_DIM), lambda b, h, i: (b, i, h)),
        out_shape=jax.ShapeDtypeStruct((bsz, seq, heads * V_DIM), BF16),
        scratch_shapes=_attend_scratch(tq, tk),
        compiler_params=_cparams("parallel", "parallel", "arbitrary"),
        name="mla_flash",
    )(q, k, vt)


def _kmean_kernel(k_ref, o_ref):
    k = k_ref[...].astype(F32)
    o_ref[...] = jnp.mean(k.reshape(SUBLANES, MOBA_BLOCK, k.shape[-1]), axis=1)


def _kmean(qk):
    bsz, seq, _ = qk.shape
    rows = SUBLANES * MOBA_BLOCK
    return pl.pallas_call(
        _kmean_kernel,
        grid=(bsz, seq // rows),
        in_specs=[pl.BlockSpec((None, rows, A_W), lambda b, i: (b, i, KA_BLK * LANES // A_W))],
        out_specs=pl.BlockSpec((None, SUBLANES, A_W), lambda b, i: (b, i, 0)),
        out_shape=jax.ShapeDtypeStruct((bsz, seq // MOBA_BLOCK, A_W), F32),
        compiler_params=_cparams("parallel", "parallel"),
        name="moba_kmean",
    )(qk)


def _moba_kernel(q_ref, k_ref, vt_ref, km_ref, o_ref, sel_sc, *scratch):
    t = MOBA_BLOCK
    per_chunk = MOBA_CHUNK // t
    i = pl.program_id(2)
    nb = km_ref.shape[0]
    qT = _transpose_q(q_ref)

    km = km_ref[...]
    km_hi = km.astype(BF16)
    km_lo = (km - km_hi.astype(F32)).astype(BF16)
    gate = _dot(km_hi, qT) + _dot(km_lo, qT)
    blk = lax.broadcasted_iota(jnp.int32, gate.shape, 0)
    neg_inf = jnp.float32(-jnp.inf)
    g = jnp.where(blk < i, gate, neg_inf)
    sel = jnp.zeros(gate.shape, F32)
    for _ in range(MOBA_TOPK):
        mx = jnp.max(g, axis=0, keepdims=True)
        is_max = (g == mx) & (mx > neg_inf)
        first = jnp.min(jnp.where(is_max, blk, nb), axis=0, keepdims=True)
        pick = blk == first
        sel = jnp.where(pick, 1.0, sel)
        g = jnp.where(pick, neg_inf, g)
    sel_sc[...] = sel

    def per_block(sT, c, block_mask):
        parts = [block_mask(sT[j * t:(j + 1) * t, :], c * per_chunk + j) for j in range(per_chunk)]
        return jnp.concatenate(parts, axis=0)

    def picked(g_idx):
        return sel_sc[pl.ds(jnp.minimum(g_idx, nb - 1), 1), :] > 0.0

    def past(sT, c):
        return per_block(sT, c, lambda s, g_idx: jnp.where(picked(g_idx), s, MASKED))

    def general(sT, c):
        key = lax.broadcasted_iota(jnp.int32, (t, t), 0)
        qry = lax.broadcasted_iota(jnp.int32, (t, t), 1)
        causal = key <= qry

        def block_mask(s, g_idx):
            ok = (picked(g_idx) & (g_idx < i)) | (causal & (g_idx == i))
            return jnp.where(ok, s, MASKED)

        return per_block(sT, c, block_mask)

    oT = _attend_chunks(qT, k_ref, vt_ref, scratch, tk=MOBA_CHUNK, n_full=i // per_chunk,
                        mask_main=past, mask_tail=general, unroll=8)
    o_ref[...] = oT.T.astype(o_ref.dtype)


def _moba_attention(qk, vt, kmean):
    bsz, seq, _ = qk.shape
    t = MOBA_BLOCK
    nb = seq // t
    return pl.pallas_call(
        _moba_kernel,
        grid=(bsz, A_HEADS, nb),
        in_specs=[pl.BlockSpec((None, t, HEAD_DIM), lambda b, h, i: (b, i, QA_BLK + h)),
                  pl.BlockSpec((None, seq, HEAD_DIM), lambda b, h, i: (b, 0, KA_BLK + h)),
                  pl.BlockSpec((None, None, VT_ROWS, seq), lambda b, h, i: (b, h, 0, 0)),
                  pl.BlockSpec((None, nb, HEAD_DIM), lambda b, h, i: (b, 0, h))],
        out_specs=pl.BlockSpec((None, t, HEAD_DIM), lambda b, h, i: (b, i, h)),
        out_shape=jax.ShapeDtypeStruct((bsz, seq, A_W), BF16),
        scratch_shapes=[pltpu.VMEM((nb, t), F32)] + _attend_scratch(t, MOBA_CHUNK),
        compiler_params=_cparams("parallel", "parallel", "arbitrary"),
        name="moba",
    )(qk, qk, vt, kmean)


def _proj_dilated_kernel(h_ref, w_ref, c_ref, s_ref, q_ref, k_ref, v_ref, sc, *, d):
    acc = _dot(h_ref[...], w_ref[...])
    c, s = c_ref[...], s_ref[...]
    q_scale = HEAD_DIM ** -0.5 * LOG2E
    for j in range(acc.shape[1] // LANES):
        blk = acc[:, j * LANES:(j + 1) * LANES]
        if j < B_HEADS:
            blk = _rope128(blk, c, s) * q_scale
        elif j < 2 * B_HEADS:
            blk = _rope128(blk, c, s)
        sc[j] = blk
    rows = acc.shape[0] // d
    for r in range(d):
        for j in range(acc.shape[1] // LANES):
            dst = (q_ref, k_ref, v_ref)[j // B_HEADS]
            col = (j % B_HEADS) * LANES
            dst[r, :, col:col + LANES] = sc[j, pl.ds(r, rows, stride=d), :].astype(dst.dtype)


def _proj_dilated(h, w, rope_h, d, bsz, seq, tm=512):
    m, k = h.shape
    nt = seq // tm
    pos = pl.BlockSpec((tm, LANES), lambda i: (i % nt, 0))
    out = pl.BlockSpec((None, d, tm // d, B_W), lambda i: (i // nt, 0, i % nt, 0))
    return pl.pallas_call(
        functools.partial(_proj_dilated_kernel, d=d),
        grid=(m // tm,),
        in_specs=[pl.BlockSpec((tm, k), lambda i: (i, 0)), pl.BlockSpec(w.shape, lambda i: (0, 0)),
                  pos, pos],
        out_specs=[out] * 3,
        out_shape=[jax.ShapeDtypeStruct((bsz, d, seq // d, B_W), BF16)] * 3,
        scratch_shapes=[pltpu.VMEM((w.shape[1] // LANES, tm, LANES), F32)],
        compiler_params=_cparams("parallel"),
        name=f"proj_dilated_d{d}",
    )(h, w, *rope_h)


def _dilated_kernel(q_ref, kc_ref, kp_ref, vc_ref, vp_ref, o_ref, lse_ref, *, span):
    t, tp = q_ref.shape[0], kp_ref.shape[0]
    i = pl.program_id(2)
    dist = (lax.broadcasted_iota(jnp.int32, (t, t), 1)
            - lax.broadcasted_iota(jnp.int32, (t, t), 0))
    bias_c = jnp.where((dist >= 0) & (dist <= span), 0.0, MASKED)
    dist_p = (lax.broadcasted_iota(jnp.int32, (tp, t), 1) + tp
              - lax.broadcasted_iota(jnp.int32, (tp, t), 0))
    bias_p = jnp.where((dist_p <= span) & (i > 0), 0.0, MASKED)
    ones_c = jnp.ones((BF16_ROWS, t), BF16)
    ones_p = jnp.ones((BF16_ROWS, tp), BF16)

    def transposed(ref, sl):
        return ref[:, sl].astype(F32).T.astype(BF16)

    for j in range(B_HEADS):
        sl = slice(j * LANES, (j + 1) * LANES)
        qT = transposed(q_ref, sl)
        s_c = _dot(kc_ref[:, sl], qT) + bias_c
        s_p = _dot(kp_ref[:, sl], qT) + bias_p
        m = jnp.maximum(jnp.max(s_c, axis=0, keepdims=True), jnp.max(s_p, axis=0, keepdims=True))
        p_c = jnp.exp2(s_c - m).astype(BF16)
        p_p = jnp.exp2(s_p - m).astype(BF16)
        vt_c = jnp.concatenate([transposed(vc_ref, sl), ones_c], axis=0)
        vt_p = jnp.concatenate([transposed(vp_ref, sl), ones_p], axis=0)
        acc = _dot(vt_c, p_c) + _dot(vt_p, p_p)
        den = acc[V_DIM:V_DIM + 1, :]
        o_ref[:, sl] = (acc[:V_DIM, :] / den).T
        lse = m + jnp.log2(den)
        lse_ref[:, sl] = jnp.broadcast_to(lse, (LANES, t)).T


def _dilated_attention(q, k, v, span, t=512):
    bsz, d, length, _ = q.shape
    t = min(t, length)
    tp = B_QBLOCK
    assert span <= tp and t % tp == 0
    cur = pl.BlockSpec((None, None, t, B_W), lambda b, r, i: (b, r, i, 0))
    prev = pl.BlockSpec((None, None, tp, B_W),
                        lambda b, r, i: (b, r, jnp.maximum(i * (t // tp) - 1, 0), 0))
    return pl.pallas_call(
        functools.partial(_dilated_kernel, span=span),
        grid=(bsz, d, length // t),
        in_specs=[cur, cur, prev, cur, prev],
        out_specs=[cur, cur],
        out_shape=[jax.ShapeDtypeStruct(q.shape, F32)] * 2,
        compiler_params=_cparams("parallel", "parallel", "parallel"),
        name=f"dilated_d{d}",
    )(q, k, k, v, v)


def _natural_rows(ref, sc):
    d, rows = ref.shape[0], ref.shape[1]
    if d == 1:
        return ref[0]
    for r in range(d):
        for j in range(B_HEADS):
            sc[j, pl.ds(r, rows, stride=d), :] = ref[r, :, j * LANES:(j + 1) * LANES]
    return jnp.concatenate([sc[j] for j in range(B_HEADS)], axis=-1)


def _mixer_tail_kernel(x_ref, oa_ref, o0_ref, o1_ref, o2_ref, l0_ref, l1_ref, l2_ref, oc_ref,
                       g_ref, wpa_ref, wpb_ref, wpc_ref, wo_ref, y_ref, *scratch):
    o0, o1, o2, l0, l1, l2 = [
        _natural_rows(ref, sc)
        for ref, sc in zip((o0_ref, o1_ref, o2_ref, l0_ref, l1_ref, l2_ref), scratch)]
    mx = jnp.maximum(jnp.maximum(l0, l1), l2)
    e0, e1, e2 = jnp.exp2(l0 - mx), jnp.exp2(l1 - mx), jnp.exp2(l2 - mx)
    ob = (e0 * o0 + e1 * o1 + e2 * o2) / (e0 + e1 + e2)
    pa = _dot(oa_ref[...], wpa_ref[...])
    pb = _dot(ob.astype(BF16), wpb_ref[...])
    pc = _dot(oc_ref[...], wpc_ref[...])
    d = D_MODEL
    merged = (g_ref[:, 0:d].astype(F32) * pa + g_ref[:, d:2 * d].astype(F32) * pb
              + g_ref[:, 2 * d:3 * d].astype(F32) * pc)
    y_ref[...] = x_ref[...] + _dot(merged.astype(BF16), wo_ref[...])


def _mixer_tail(x, out_a, o_groups, lse_groups, out_c, gates, w_pa, w_pb, w_pc, w_o, seq, tm=256):
    m, d = x.shape
    nt = seq // tm
    row = lambda width: pl.BlockSpec((tm, width), lambda i: (i, 0))
    residue = lambda g: pl.BlockSpec((None, g.shape[1], tm // g.shape[1], B_W),
                                     lambda i: (i // nt, 0, i % nt, 0))
    weights = [_resident(w) for w in (w_pa, w_pb, w_pc, w_o)]
    groups = list(o_groups) + list(lse_groups)
    return pl.pallas_call(
        _mixer_tail_kernel,
        grid=(m // tm,),
        in_specs=([row(d), row(A_W)] + [residue(g) for g in groups]
                  + [row(C_W), row(3 * d)] + weights),
        out_specs=row(d),
        out_shape=jax.ShapeDtypeStruct((m, d), F32),
        scratch_shapes=[pltpu.VMEM((B_HEADS, tm, LANES), F32) for _ in groups],
        compiler_params=_cparams("parallel"),
        name="mixer_tail",
    )(x, out_a, *groups, out_c, gates, w_pa, w_pb, w_pc, w_o)


def _mem_kv_kernel(mem_ref, g_ref, wk_ref, wv_ref, k_ref, v_ref):
    memn = _rms(mem_ref[...], g_ref[...]).astype(BF16)
    k_ref[...] = _dot(memn, wk_ref[...]).astype(k_ref.dtype)
    v_ref[...] = _dot(memn, wv_ref[...]).astype(v_ref.dtype)


def _mem_kv(mem, g, wk, wv):
    bsz, n, d = mem.shape
    out = pl.BlockSpec((None, n, X_W), lambda b: (b, 0, 0))
    return pl.pallas_call(
        _mem_kv_kernel,
        grid=(bsz,),
        in_specs=[pl.BlockSpec((None, n, d), lambda b: (b, 0, 0)),
                  pl.BlockSpec((1, d), lambda b: (0, 0)),
                  pl.BlockSpec(wk.shape, lambda b: (0, 0)),
                  pl.BlockSpec(wv.shape, lambda b: (0, 0))],
        out_specs=[out, out],
        out_shape=[jax.ShapeDtypeStruct((bsz, n, X_W), BF16)] * 2,
        compiler_params=_cparams("parallel"),
        name="mem_kv",
    )(mem, g.reshape(1, d), wk, wv)


def _mem_attn_kernel(x_ref, g_ref, wq_ref, k_ref, v_ref, wo_ref, y_ref):
    x = x_ref[...]
    h = _rms(x, g_ref[...]).astype(BF16)
    q = (_dot(h, wq_ref[...]) * HEAD_DIM ** -0.5).astype(BF16)
    heads = []
    for hd in range(X_HEADS):
        sl = slice(hd * HEAD_DIM, (hd + 1) * HEAD_DIM)
        s = _dot_nt(q[:, sl], k_ref[:, sl])
        p = jnp.exp(s - jnp.max(s, axis=-1, keepdims=True))
        o = _dot(p.astype(BF16), v_ref[:, sl]) / jnp.sum(p, axis=-1, keepdims=True)
        heads.append(o.astype(BF16))
    y_ref[...] = x + _dot(jnp.concatenate(heads, axis=-1), wo_ref[...])


def _mem_attention(x, g, wq, kmem, vmem, wo, seq, tm=512):
    m, d = x.shape
    nt = seq // tm
    n = kmem.shape[1]
    kv = pl.BlockSpec((None, n, X_W), lambda i: (i // nt, 0, 0))
    return pl.pallas_call(
        _mem_attn_kernel,
        grid=(m // tm,),
        in_specs=[pl.BlockSpec((tm, d), lambda i: (i, 0)),
                  pl.BlockSpec((1, d), lambda i: (0, 0)),
                  pl.BlockSpec(wq.shape, lambda i: (0, 0)), kv, kv,
                  pl.BlockSpec(wo.shape, lambda i: (0, 0))],
        out_specs=pl.BlockSpec((tm, d), lambda i: (i, 0)),
        out_shape=jax.ShapeDtypeStruct((m, d), F32),
        compiler_params=_cparams("parallel"),
        name="mem_attention",
    )(x, g.reshape(1, d), wq, kmem, vmem, wo)


def _ffn_kernel(x_ref, halo_ref, g_ref, wg_ref, wv_ref, cwg_ref, cwv_ref, cbg_ref, cbv_ref,
                wd_ref, y_ref, h_sc, acc_sc, *, tiles_per_seq):
    i = pl.program_id(0)
    f = pl.program_id(1)
    tm = x_ref.shape[0]

    @pl.when(f == 0)
    def _():
        g = g_ref[...]
        keep = (i % tiles_per_seq != 0).astype(F32)
        h_sc[0:HALO, :] = (_rms(halo_ref[...], g) * keep).astype(h_sc.dtype)
        h_sc[HALO:, :] = _rms(x_ref[...], g).astype(h_sc.dtype)
        acc_sc[...] = jnp.zeros_like(acc_sc)

    h = h_sc[...]

    def conv(w_ref, cw_ref, cb_ref):
        u = _dot(h, w_ref[...])
        c = cb_ref[...]
        for tap in range(CONV_W):
            lo = HALO - (CONV_W - 1) + tap
            c = c + cw_ref[tap:tap + 1, :] * u[lo:lo + tm, :]
        return c

    act = jax.nn.silu(conv(wg_ref, cwg_ref, cbg_ref)) * conv(wv_ref, cwv_ref, cbv_ref)
    acc_sc[...] += _dot(act.astype(BF16), wd_ref[...])

    @pl.when(f == pl.num_programs(1) - 1)
    def _():
        y_ref[...] = x_ref[...] + acc_sc[...]


def _conv_ffn(x, g, w_up, conv_w, conv_b, w_down, seq, tm=512):
    m, d = x.shape
    tf = FFN_TF
    nf = D_FF_PAD // tf
    halo_blocks = tm // HALO
    return pl.pallas_call(
        functools.partial(_ffn_kernel, tiles_per_seq=seq // tm),
        grid=(m // tm, nf),
        in_specs=[pl.BlockSpec((tm, d), lambda i, f: (i, 0)),
                  pl.BlockSpec((HALO, d), lambda i, f: (jnp.maximum(i * halo_blocks - 1, 0), 0)),
                  pl.BlockSpec((1, d), lambda i, f: (0, 0)),
                  pl.BlockSpec((d, tf), lambda i, f: (0, f)),
                  pl.BlockSpec((d, tf), lambda i, f: (0, f + nf)),
                  pl.BlockSpec((CONV_W, tf), lambda i, f: (0, f)),
                  pl.BlockSpec((CONV_W, tf), lambda i, f: (0, f + nf)),
                  pl.BlockSpec((1, tf), lambda i, f: (0, f)),
                  pl.BlockSpec((1, tf), lambda i, f: (0, f + nf)),
                  pl.BlockSpec((tf, d), lambda i, f: (f, 0))],
        out_specs=pl.BlockSpec((tm, d), lambda i, f: (i, 0)),
        out_shape=jax.ShapeDtypeStruct((m, d), F32),
        scratch_shapes=[pltpu.VMEM((HALO + tm, d), BF16), pltpu.VMEM((tm, d), F32)],
        compiler_params=_cparams("parallel", "arbitrary"),
        name="conv_ffn",
    )(x, x, g.reshape(1, d), w_up, w_up, conv_w, conv_w, conv_b, conv_b, w_down)


def _rope_tables(seq):
    def angles(dim):
        inv_freq = jnp.exp(jnp.arange(0, dim, 2, dtype=F32) * (-math.log(ROPE_THETA) / dim))
        ang = jnp.arange(seq, dtype=F32)[:, None] * inv_freq[None, :]
        return jnp.cos(ang), jnp.sin(ang)

    cos_h, sin_h = angles(HEAD_DIM)
    rope_h = (jnp.concatenate([cos_h, cos_h], axis=-1), jnp.concatenate([-sin_h, sin_h], axis=-1))
    cos_r, sin_r = angles(ROPE_DIM)
    z = jnp.zeros_like(cos_r)
    rope_r = (jnp.concatenate([cos_r, cos_r, z, z], axis=-1),
              jnp.concatenate([-sin_r, z, z, z], axis=-1),
              jnp.concatenate([z, sin_r, z, z], axis=-1))
    return rope_h, rope_r


def _split_in(w_in):
    return [w_in[:, IN_OFFSETS[k]:IN_OFFSETS[k + 1]] for k in range(len(IN_WIDTHS))]


def _pad_cols(w, width):
    return jnp.pad(w, ((0, 0), (0, width - w.shape[1])))


def _layer_params(w_in, w_uq, w_ukv, w_up, conv_w, conv_b, w_down):
    qa, ka, va, qb, kb, vb, cq, ckv, kr, gates = _split_in(w_in)
    w_qk = jnp.concatenate([qa, ka], axis=1).astype(BF16)
    group_cols = lambda w, g: w[:, g * B_W:(g + 1) * B_W]
    w_b = [jnp.concatenate([group_cols(qb, g), group_cols(kb, g), group_cols(vb, g)],
                           axis=1).astype(BF16) for g in range(len(B_GROUPS))]
    w_down_in = jnp.concatenate([cq, ckv, _pad_cols(kr, LANES)], axis=1).astype(BF16)
    uq = w_uq.reshape(Q_LORA, C_HEADS, NOPE_DIM + ROPE_DIM)
    uq = jnp.pad(uq, ((0, 0), (0, 0), (0, C_QK - NOPE_DIM - ROPE_DIM)))
    ukv = w_ukv.reshape(KV_LORA, C_HEADS, NOPE_DIM + V_DIM)
    pad_ff = lambda w: jnp.pad(w, ((0, 0), (0, D_FF_PAD - D_FF)))
    two_halves = lambda w: jnp.concatenate([pad_ff(w[:, :D_FF]), pad_ff(w[:, D_FF:])], axis=1)
    return dict(
        w_qk=w_qk, w_va=va.astype(BF16), w_b=w_b, w_gates=gates.astype(BF16),
        w_down_in=w_down_in,
        w_uq=uq.reshape(Q_LORA, C_HEADS * C_QK).astype(BF16),
        w_uk=ukv[:, :, :NOPE_DIM].reshape(KV_LORA, C_HEADS * NOPE_DIM).astype(BF16),
        w_uv=ukv[:, :, NOPE_DIM:].reshape(KV_LORA, C_W).astype(BF16),
        w_up=two_halves(w_up).astype(BF16),
        conv_w=two_halves(conv_w),
        conv_b=two_halves(conv_b.reshape(1, -1)),
        w_down=jnp.pad(w_down, ((0, D_FF_PAD - D_FF), (0, 0))).astype(BF16),
    )


def _qk_col_scale():
    q_scale = HEAD_DIM ** -0.5
    parts = [jnp.full((A_W,), q_scale * LOG2E, F32), jnp.ones((A_W,), F32)]
    return jnp.concatenate(parts).reshape(1, QK_W)


def _mixer(x, g_mix, p, g_cq, g_ckv, w_pa, w_pb, w_pc, w_o, rope_h, rope_r, bsz, seq):
    m = x.shape[0]
    h = _rmsnorm(x, g_mix, BF16)
    qk = _matmul(h, p["w_qk"], _mm_rope_kernel, BF16, 1024, 1024, seq=seq,
                 extras=(("col", _qk_col_scale()), ("pos", rope_h[0]), ("pos", rope_h[1])),
                 name="proj_qk_rope")
    vt_a = _proj_vt(h, p["w_va"], A_HEADS, bsz, seq)
    gates = _matmul(h, p["w_gates"], _mm_sigmoid_kernel, BF16, 1024, 1024, name="proj_gates")
    cq, ckv, kr = _mla_down(h, p["w_down_in"], g_cq, g_ckv, rope_r, seq)
    q_c = _mla_q(cq, p["w_uq"], rope_r, seq)
    k_c, vt_c = _mla_kv(ckv, kr, p["w_uk"], p["w_uv"], bsz, seq)

    qk3 = qk.reshape(bsz, seq, QK_W)
    out_a = _moba_attention(qk3, vt_a, _kmean(qk3)).reshape(m, A_W)
    groups = []
    for (window, d), w_g in zip(B_GROUPS, p["w_b"]):
        q_g, k_g, v_g = _proj_dilated(h, w_g, rope_h, d, bsz, seq)
        groups.append(_dilated_attention(q_g, k_g, v_g, window // d))
    out_c = _flash_attention(q_c.reshape(bsz, seq, -1), k_c.reshape(bsz, seq, -1), vt_c,
                             C_HEADS, C_QK).reshape(m, C_W)
    return _mixer_tail(x, out_a, [g[0] for g in groups], [g[1] for g in groups], out_c, gates,
                       w_pa.astype(BF16), w_pb.astype(BF16), w_pc.astype(BF16), w_o.astype(BF16),
                       seq)


def kernel(x, mem, g_mix, w_in, g_cq, g_ckv, w_uq, w_ukv, w_pa, w_pb, w_pc, w_o, g_mem, g_memkv,
           w_xq, w_xk, w_xv, w_xo, g_ffn, w_up, conv_w, conv_b, w_down, g_final):
    bsz, seq, d = x.shape
    rope_h, rope_r = _rope_tables(seq)
    xf = x.reshape(bsz * seq, d)
    for l in range(DEPTH):
        p = _layer_params(w_in[l], w_uq[l], w_ukv[l], w_up[l], conv_w[l], conv_b[l], w_down[l])
        xf = _mixer(xf, g_mix[l], p, g_cq[l], g_ckv[l], w_pa[l], w_pb[l], w_pc[l], w_o[l],
                    rope_h, rope_r, bsz, seq)
        kmem, vmem = _mem_kv(mem, g_memkv[l], w_xk[l].astype(BF16), w_xv[l].astype(BF16))
        xf = _mem_attention(xf, g_mem[l], w_xq[l].astype(BF16), kmem, vmem,
                            w_xo[l].astype(BF16), seq)
        xf = _conv_ffn(xf, g_ffn[l], p["w_up"], p["conv_w"], p["conv_b"], p["w_down"], seq)
    return _rmsnorm(xf, g_final, F32).reshape(bsz, seq, d)
```

```python
import functools
import math

import jax
import jax.numpy as jnp
import numpy as np
from jax import lax
from jax.experimental import pallas as pl
from jax.experimental.pallas import tpu as pltpu
from jax.experimental.pallas import tpu_sc as plsc

F32 = jnp.float32
BF16 = jnp.bfloat16

LANES = 128
SUBLANES = 8
VMEM_LIMIT = 56 * 1024 * 1024

D_MODEL = 2048
DEPTH = 2
HEAD_DIM = 128
ROPE_THETA = 10000.0
EPS = 1e-6

A_HEADS = 4
MOBA_BLOCK = 256
MOBA_TOPK = 3

B_GROUPS = ((128, 1), (512, 4), (2048, 16))
B_HEADS = 4
B_QBLOCK = 128

C_HEADS = 8
Q_LORA = 1536
KV_LORA = 512
NOPE_DIM = 128
ROPE_DIM = 64
V_DIM = 128

X_HEADS = 4
D_FF = 5504
CONV_W = 3

A_W = A_HEADS * HEAD_DIM
B_QKV_W = len(B_GROUPS) * B_HEADS * HEAD_DIM
B_W = B_HEADS * HEAD_DIM
C_W = C_HEADS * V_DIM
X_W = X_HEADS * HEAD_DIM
IN_WIDTHS = (A_W, A_W, A_W, B_QKV_W, B_QKV_W, B_QKV_W, Q_LORA, KV_LORA, ROPE_DIM, 3 * D_MODEL)
IN_OFFSETS = tuple(int(o) for o in np.cumsum((0,) + IN_WIDTHS))

QK_W = 2 * A_W
QA_BLK, KA_BLK = 0, A_W // LANES

C_QK = 2 * LANES
MASKED = -1e30
LOG2E = math.log2(math.e)
BF16_ROWS = 16
VT_ROWS = V_DIM + BF16_ROWS
GROUP_STEP = 8
SC_WINDOW = 128

D_FF_PAD = 5632
FFN_TF = 512
HALO = SUBLANES


def _cparams(*sem):
    return pltpu.CompilerParams(dimension_semantics=sem, vmem_limit_bytes=VMEM_LIMIT)


def _resident(arr):
    zeros = (0,) * arr.ndim
    return pl.BlockSpec(arr.shape, lambda *_: zeros, pipeline_mode=pl.Buffered(1))


def _dot(a, b):
    return jnp.dot(a, b, preferred_element_type=F32)


def _dot_nt(a, b):
    return lax.dot_general(a, b, (((1,), (1,)), ((), ())), preferred_element_type=F32)


def _rms(x, g):
    return x * lax.rsqrt(jnp.mean(x * x, axis=-1, keepdims=True) + EPS) * g


def _rmsnorm_kernel(x_ref, g_ref, o_ref):
    o_ref[...] = _rms(x_ref[...], g_ref[...]).astype(o_ref.dtype)


def _rmsnorm(x, g, out_dtype, tm=512):
    m, d = x.shape
    return pl.pallas_call(
        _rmsnorm_kernel,
        grid=(m // tm,),
        in_specs=[pl.BlockSpec((tm, d), lambda i: (i, 0)),
                  pl.BlockSpec((1, d), lambda i: (0, 0))],
        out_specs=pl.BlockSpec((tm, d), lambda i: (i, 0)),
        out_shape=jax.ShapeDtypeStruct((m, d), out_dtype),
        compiler_params=_cparams("parallel"),
        name="rmsnorm",
    )(x, g.reshape(1, d))


def _rope128(x, c, s):
    return x * c + pltpu.roll(x, HEAD_DIM // 2, 1) * s


def _rope64(x, c, sa, sb):
    half = ROPE_DIM // 2
    return x * c + pltpu.roll(x, LANES - half, 1) * sa + pltpu.roll(x, half, 1) * sb


def _mm_plain_kernel(a_ref, w_ref, o_ref):
    o_ref[...] = _dot(a_ref[...], w_ref[...]).astype(o_ref.dtype)


def _mm_sigmoid_kernel(a_ref, w_ref, o_ref):
    o_ref[...] = jax.nn.sigmoid(_dot(a_ref[...], w_ref[...])).astype(o_ref.dtype)


def _mm_rope_kernel(a_ref, w_ref, cs_ref, c_ref, s_ref, o_ref):
    acc = _dot(a_ref[...], w_ref[...])
    c = c_ref[...]
    s = s_ref[...]
    for j in range(acc.shape[1] // LANES):
        sl = slice(j * LANES, (j + 1) * LANES)
        o_ref[:, sl] = (_rope128(acc[:, sl], c, s) * cs_ref[:, sl]).astype(o_ref.dtype)


def _matmul(a, w, kernel, out_dtype, tm, tn, seq=None, extras=(), name="matmul"):
    m, k = a.shape
    n = w.shape[1]
    in_specs = [pl.BlockSpec((tm, k), lambda i, j: (i, 0)),
                pl.BlockSpec((k, tn), lambda i, j: (0, j))]
    args = [a, w]
    for kind, arr in extras:
        if kind == "col":
            in_specs.append(pl.BlockSpec((1, tn), lambda i, j: (0, j)))
        else:
            nt = seq // tm
            in_specs.append(pl.BlockSpec((tm, LANES), lambda i, j: (i % nt, 0)))
        args.append(arr)
    return pl.pallas_call(
        kernel,
        grid=(m // tm, n // tn),
        in_specs=in_specs,
        out_specs=pl.BlockSpec((tm, tn), lambda i, j: (i, j)),
        out_shape=jax.ShapeDtypeStruct((m, n), out_dtype),
        compiler_params=_cparams("parallel", "parallel"),
        name=name,
    )(*args)


def _mla_down_kernel(h_ref, w_ref, gq_ref, gkv_ref, c_ref, sa_ref, sb_ref,
                     cq_ref, ckv_ref, kr_ref):
    acc = _dot(h_ref[...], w_ref[...])
    cq_ref[...] = _rms(acc[:, :Q_LORA], gq_ref[...]).astype(cq_ref.dtype)
    ckv_ref[...] = _rms(acc[:, Q_LORA:Q_LORA + KV_LORA], gkv_ref[...]).astype(ckv_ref.dtype)
    kr = acc[:, Q_LORA + KV_LORA:]
    kr_ref[...] = _rope64(kr, c_ref[...], sa_ref[...], sb_ref[...]).astype(kr_ref.dtype)


def _mla_down(h, w, g_cq, g_ckv, rope_r, seq, tm=512):
    m, k = h.shape
    n = w.shape[1]
    nt = seq // tm
    row = lambda width: pl.BlockSpec((tm, width), lambda i: (i, 0))
    full = lambda r, c: pl.BlockSpec((r, c), lambda i: (0, 0))
    pos = pl.BlockSpec((tm, LANES), lambda i: (i % nt, 0))
    return pl.pallas_call(
        _mla_down_kernel,
        grid=(m // tm,),
        in_specs=[row(k), full(k, n), full(1, Q_LORA), full(1, KV_LORA), pos, pos, pos],
        out_specs=[row(Q_LORA), row(KV_LORA), row(LANES)],
        out_shape=[jax.ShapeDtypeStruct((m, Q_LORA), BF16),
                   jax.ShapeDtypeStruct((m, KV_LORA), BF16),
                   jax.ShapeDtypeStruct((m, LANES), BF16)],
        compiler_params=_cparams("parallel"),
        name="mla_down",
    )(h, w, g_cq.reshape(1, -1), g_ckv.reshape(1, -1), *rope_r)


def _mla_q_kernel(cq_ref, w_ref, c_ref, sa_ref, sb_ref, q_ref, *, scale):
    acc = _dot(cq_ref[...], w_ref[...])
    c, sa, sb = c_ref[...], sa_ref[...], sb_ref[...]
    for hd in range(C_HEADS):
        lo = hd * C_QK
        q_ref[:, lo:lo + LANES] = (acc[:, lo:lo + LANES] * scale).astype(q_ref.dtype)
        rope = _rope64(acc[:, lo + LANES:lo + C_QK], c, sa, sb)
        q_ref[:, lo + LANES:lo + C_QK] = (rope * scale).astype(q_ref.dtype)


def _mla_q(cq, w, rope_r, seq, tm=512):
    m, k = cq.shape
    n = w.shape[1]
    nt = seq // tm
    pos = pl.BlockSpec((tm, LANES), lambda i: (i % nt, 0))
    return pl.pallas_call(
        functools.partial(_mla_q_kernel, scale=(NOPE_DIM + ROPE_DIM) ** -0.5 * LOG2E),
        grid=(m // tm,),
        in_specs=[pl.BlockSpec((tm, k), lambda i: (i, 0)),
                  pl.BlockSpec((k, n), lambda i: (0, 0)), pos, pos, pos],
        out_specs=pl.BlockSpec((tm, n), lambda i: (i, 0)),
        out_shape=jax.ShapeDtypeStruct((m, n), BF16),
        compiler_params=_cparams("parallel"),
        name="mla_q",
    )(cq, w, *rope_r)


def _store_vt(v, vt_ref):
    vt = v.T
    for hd in range(vt_ref.shape[0]):
        vt_ref[hd, 0:V_DIM, :] = vt[hd * V_DIM:(hd + 1) * V_DIM, :].astype(vt_ref.dtype)
        vt_ref[hd, V_DIM:VT_ROWS, :] = jnp.ones((VT_ROWS - V_DIM, vt.shape[1]), vt_ref.dtype)


def _mla_kv_kernel(ckv_ref, kr_ref, wk_ref, wv_ref, k_ref, vt_ref):
    ckv = ckv_ref[...]
    kn = _dot(ckv, wk_ref[...])
    kr = kr_ref[...]
    for hd in range(C_HEADS):
        lo = hd * C_QK
        k_ref[:, lo:lo + LANES] = kn[:, hd * LANES:(hd + 1) * LANES].astype(k_ref.dtype)
        k_ref[:, lo + LANES:lo + C_QK] = kr
    _store_vt(_dot(ckv, wv_ref[...]), vt_ref)


def _mla_kv(ckv, kr, wk, wv, bsz, seq, tm=512):
    m, k = ckv.shape
    nt = seq // tm
    return pl.pallas_call(
        _mla_kv_kernel,
        grid=(m // tm,),
        in_specs=[pl.BlockSpec((tm, k), lambda i: (i, 0)),
                  pl.BlockSpec((tm, LANES), lambda i: (i, 0)),
                  pl.BlockSpec(wk.shape, lambda i: (0, 0)),
                  pl.BlockSpec(wv.shape, lambda i: (0, 0))],
        out_specs=[pl.BlockSpec((tm, C_HEADS * C_QK), lambda i: (i, 0)),
                   pl.BlockSpec((None, C_HEADS, VT_ROWS, tm), lambda i: (i // nt, 0, 0, i % nt))],
        out_shape=[jax.ShapeDtypeStruct((m, C_HEADS * C_QK), BF16),
                   jax.ShapeDtypeStruct((bsz, C_HEADS, VT_ROWS, seq), BF16)],
        compiler_params=_cparams("parallel"),
        name="mla_kv",
    )(ckv, kr, wk, wv)


def _proj_vt_kernel(h_ref, w_ref, vt_ref):
    _store_vt(_dot(h_ref[...], w_ref[...]), vt_ref)


def _proj_vt(h, w, heads, bsz, seq, tm=512):
    m, k = h.shape
    nt = seq // tm
    return pl.pallas_call(
        _proj_vt_kernel,
        grid=(m // tm,),
        in_specs=[pl.BlockSpec((tm, k), lambda i: (i, 0)), pl.BlockSpec(w.shape, lambda i: (0, 0))],
        out_specs=pl.BlockSpec((None, heads, VT_ROWS, tm), lambda i: (i // nt, 0, 0, i % nt)),
        out_shape=jax.ShapeDtypeStruct((bsz, heads, VT_ROWS, seq), BF16),
        compiler_params=_cparams("parallel"),
        name="proj_vt",
    )(h, w)


def _attend_chunks(qT, k_ref, vt_ref, scratch, *, tk, n_full, mask_main, mask_tail,
                   unroll=4, tail_steps=2):
    m_sc, acc_sc, sa_sc, sb_sc, pa_sc, pb_sc, ala_sc, alb_sc = scratch
    s_bufs = (sa_sc, sb_sc)
    p_bufs = (pa_sc, pb_sc)
    al_bufs = (ala_sc, alb_sc)
    last_chunk = k_ref.shape[0] // tk - 1
    m_sc[...] = jnp.full(m_sc.shape, MASKED, F32)
    acc_sc[...] = jnp.zeros(acc_sc.shape, F32)
    for p_ref, al_ref in zip(p_bufs, al_bufs):
        p_ref[...] = jnp.zeros(p_ref.shape, p_ref.dtype)
        al_ref[...] = jnp.ones(al_ref.shape, F32)

    def rows(c):
        return pl.ds(pl.multiple_of(jnp.clip(c, 0, last_chunk) * tk, tk), tk)

    def scores(c):
        return _dot(k_ref[rows(c), :], qT)

    def flush(c, slot):
        acc_sc[...] = (al_bufs[slot][...] * acc_sc[...]
                       + _dot(vt_ref[:, rows(c)], p_bufs[slot][...]))

    def softmax(sT, slot):
        m_old = m_sc[...]
        m_new = jnp.maximum(m_old, jnp.max(sT, axis=0, keepdims=True))
        al_bufs[slot][...] = jnp.exp2(m_old - m_new)
        p_bufs[slot][...] = jnp.exp2(sT - m_new).astype(p_bufs[slot].dtype)
        m_sc[...] = m_new

    def step(tau, slot, mask, prefetch):
        flush(tau - 2, slot)
        if prefetch:
            s_bufs[1 - slot][...] = scores(tau + 1)
        sT = s_bufs[slot][...]
        softmax(sT if mask is None else mask(sT, tau), slot)

    def steps(tau0, count, mask, prefetch_last=True):
        for j in range(count):
            step(tau0 + j, j % 2, mask, prefetch_last or j < count - 1)

    sa_sc[...] = scores(0)
    trips = n_full // unroll
    lax.fori_loop(0, trips, lambda u, c: (steps(unroll * u, unroll, mask_main), c)[1], 0)
    done = unroll * trips
    pairs = (n_full - done) // 2
    lax.fori_loop(0, pairs, lambda u, c: (steps(done + 2 * u, 2, mask_main), c)[1], 0)
    tau = done + 2 * pairs
    steps(tau, tail_steps, mask_tail, prefetch_last=False)
    flush(tau + tail_steps - 2, 0)
    flush(tau + tail_steps - 1, 1)
    acc = acc_sc[...]
    return acc[:V_DIM, :] / acc[V_DIM:V_DIM + 1, :]


def _attend_scratch(tq, tk):
    return [pltpu.VMEM((1, tq), F32), pltpu.VMEM((VT_ROWS, tq), F32),
            pltpu.VMEM((tk, tq), F32), pltpu.VMEM((tk, tq), F32),
            pltpu.VMEM((tk, tq), BF16), pltpu.VMEM((tk, tq), BF16),
            pltpu.VMEM((1, tq), F32), pltpu.VMEM((1, tq), F32)]


def _transpose_q(q_ref):
    return q_ref[...].astype(F32).T.astype(BF16)


def _flash_kernel(q_ref, k_ref, vt_ref, o_ref, *scratch, tq, tk):
    i = pl.program_id(2)

    def causal(sT, c):
        key = lax.broadcasted_iota(jnp.int32, sT.shape, 0) + c * tk
        qry = lax.broadcasted_iota(jnp.int32, sT.shape, 1) + i * tq
        return jnp.where(key <= qry, sT, MASKED)

    per_tile = tq // tk
    oT = _attend_chunks(_transpose_q(q_ref), k_ref, vt_ref, scratch, tk=tk, n_full=i * per_tile,
                        mask_main=None, mask_tail=causal, unroll=8, tail_steps=per_tile)
    o_ref[...] = oT.T.astype(o_ref.dtype)


def _flash_attention(q, k, vt, heads, qk_w, tq=1024, tk=256):
    bsz, seq, _ = q.shape
    assert tq % (2 * tk) == 0 and seq % tq == 0
    return pl.pallas_call(
        functools.partial(_flash_kernel, tq=tq, tk=tk),
        grid=(bsz, heads, seq // tq),
        in_specs=[pl.BlockSpec((None, tq, qk_w), lambda b, h, i: (b, i, h)),
                  pl.BlockSpec((None, seq, qk_w), lambda b, h, i: (b, 0, h)),
                  pl.BlockSpec((None, None, VT_ROWS, seq), lambda b, h, i: (b, h, 0, 0))],
        out_specs=pl.BlockSpec((None, tq, V_DIM), lambda b, h, i: (b, i, h)),
        out_shape=jax.ShapeDtypeStruct((bsz, seq, heads * V_DIM), BF16),
        scratch_shapes=_attend_scratch(tq, tk),
        compiler_params=_cparams("parallel", "parallel", "arbitrary"),
        name="mla_flash",
    )(q, k, vt)


def _kmean_kernel(k_ref, o_ref):
    k = k_ref[...].astype(F32)
    o_ref[...] = jnp.mean(k.reshape(SUBLANES, MOBA_BLOCK, k.shape[-1]), axis=1)


def _kmean(qk):
    bsz, seq, _ = qk.shape
    rows = SUBLANES * MOBA_BLOCK
    return pl.pallas_call(
        _kmean_kernel,
        grid=(bsz, seq // rows),
        in_specs=[pl.BlockSpec((None, rows, A_W), lambda b, i: (b, i, KA_BLK * LANES // A_W))],
        out_specs=pl.BlockSpec((None, SUBLANES, A_W), lambda b, i: (b, i, 0)),
        out_shape=jax.ShapeDtypeStruct((bsz, seq // MOBA_BLOCK, A_W), F32),
        compiler_params=_cparams("parallel", "parallel"),
        name="moba_kmean",
    )(qk)


def _block_attention(q, k, v, visible=None):
    s = _dot_nt(q, k)
    if visible is not None:
        s = jnp.where(visible, s, MASKED)
    m = jnp.max(s, axis=-1, keepdims=True)
    p = jnp.exp2(s - m).astype(BF16)
    v_ones = jnp.concatenate([v, jnp.ones((v.shape[0], LANES), v.dtype)], axis=-1)
    acc = _dot(p, v_ones)
    den = acc[:, V_DIM:]
    return acc[:, :V_DIM] / den, m + jnp.log2(den)


def _moba_gate_kernel(q_ref, km_ref, ids_ref, cnt_ref, qf_ref):
    t = MOBA_BLOCK
    i = pl.program_id(1)
    nb = km_ref.shape[0]
    blk = lax.broadcasted_iota(jnp.int32, (nb, t), 0)
    neg_inf = jnp.float32(-jnp.inf)
    not_after = (lax.broadcasted_iota(jnp.int32, (t, t), 0)
                 <= lax.broadcasted_iota(jnp.int32, (t, t), 1))
    upper = jnp.where(not_after, 1.0, 0.0).astype(BF16)
    ones = jnp.ones((SUBLANES, t), BF16)
    for hd in range(A_HEADS):
        sl = slice(hd * HEAD_DIM, (hd + 1) * HEAD_DIM)
        q = q_ref[:, sl].astype(F32)
        qf_ref[hd] = q
        qT = q.T.astype(BF16)
        km = km_ref[:, sl]
        km_hi = km.astype(BF16)
        km_lo = (km - km_hi.astype(F32)).astype(BF16)
        g = jnp.where(blk < i, _dot(km_hi, qT) + _dot(km_lo, qT), neg_inf)
        picks, ranks, counts = [], [], []
        for _ in range(MOBA_TOPK):
            mx = jnp.max(g, axis=0, keepdims=True)
            is_max = (g == mx) & (mx > neg_inf)
            first = jnp.min(jnp.where(is_max, blk, nb), axis=0, keepdims=True)
            pick = blk == first
            g = jnp.where(pick, neg_inf, g)
            onehot = jnp.where(pick, 1.0, 0.0).astype(BF16)
            before = _dot(onehot, upper)
            rank = jnp.sum(jnp.where(pick, before - 1.0, 0.0), axis=0, keepdims=True)
            picks.append(first)
            ranks.append(rank.astype(jnp.int32))
            counts.append(_dot_nt(ones, onehot)[0:1, :])
        pad_i = jnp.zeros((SUBLANES - 2 * MOBA_TOPK, t), jnp.int32)
        ids_ref[hd] = jnp.concatenate(picks + ranks + [pad_i], axis=0)
        pad_f = jnp.zeros((SUBLANES - MOBA_TOPK, nb), F32)
        cnt_ref[hd] = jnp.concatenate(counts + [pad_f], axis=0)


def _moba_gate(qk, kmean):
    bsz, seq, _ = qk.shape
    t = MOBA_BLOCK
    nb = seq // t
    return pl.pallas_call(
        _moba_gate_kernel,
        grid=(bsz, nb),
        in_specs=[pl.BlockSpec((None, t, A_W), lambda b, i: (b, i, QA_BLK * LANES // A_W)),
                  pl.BlockSpec((None, nb, A_W), lambda b, i: (b, 0, 0))],
        out_specs=[pl.BlockSpec((None, A_HEADS, SUBLANES, t), lambda b, i: (b, 0, 0, i)),
                   pl.BlockSpec((None, A_HEADS, None, SUBLANES, nb), lambda b, i: (b, 0, i, 0, 0)),
                   pl.BlockSpec((None, A_HEADS, t, HEAD_DIM), lambda b, i: (b, 0, i, 0))],
        out_shape=[jax.ShapeDtypeStruct((bsz, A_HEADS, SUBLANES, seq), jnp.int32),
                   jax.ShapeDtypeStruct((bsz, A_HEADS, nb, SUBLANES, nb), F32),
                   jax.ShapeDtypeStruct((bsz, A_HEADS, seq, HEAD_DIM), F32)],
        compiler_params=_cparams("parallel", "parallel"),
        name="moba_gate",
    )(qk, kmean)


def _moba_routes(ids, cnt, seq):
    bsz, heads = ids.shape[:2]
    bh, t = bsz * heads, MOBA_BLOCK
    nb = seq // t
    tiles = _moba_tiles(seq)
    picks = ids[:, :, 0:MOBA_TOPK, :].reshape(bh, MOBA_TOPK, nb, t)
    ranks = ids[:, :, MOBA_TOPK:2 * MOBA_TOPK, :].reshape(bh, MOBA_TOPK, nb, t)
    per_tile = cnt[:, :, :, 0:MOBA_TOPK, :].astype(jnp.int32).reshape(bh, nb * MOBA_TOPK, nb)
    before = jnp.cumsum(per_tile, axis=1) - per_tile
    total = jnp.sum(per_tile, axis=1)
    padded = -(-total // t) * t
    ends = jnp.cumsum(padded, axis=1)
    base = before + (ends - padded)[:, None, :]
    base = base.reshape(bh, nb, MOBA_TOPK, nb).transpose(0, 2, 1, 3)
    onehot = picks[..., None] == jnp.arange(nb)
    pos = jnp.sum(jnp.where(onehot, base[:, :, :, None, :], 0), axis=-1) + ranks
    pos = jnp.where(picks < nb, pos, (tiles - 1) * t)
    pos = pos + (jnp.arange(bh, dtype=jnp.int32) * (tiles * t))[:, None, None, None]
    pos = pos.reshape(bh, MOBA_TOPK, seq).transpose(1, 0, 2).reshape(MOBA_TOPK, bh * seq)
    tile_start = jnp.arange(tiles, dtype=jnp.int32) * t
    tile_blk = jnp.sum(tile_start[None, :, None] >= ends[:, None, :], axis=-1)
    tile_blk = jnp.where(tile_start[None, :] < ends[:, -1:], tile_blk, -1)
    return pos.astype(jnp.int32), tile_blk.astype(jnp.int32)


def _moba_tiles(seq):
    nb = seq // MOBA_BLOCK
    return -(-(MOBA_TOPK * nb + nb + 1) // GROUP_STEP) * GROUP_STEP


def _sc_mesh():
    return plsc.VectorSubcoreMesh(core_axis_name="core", subcore_axis_name="subcore")


def _sc_scatter_rows(x, idx, rows):
    slots, n = idx.shape
    d = x.shape[1]

    @pl.kernel(out_type=jax.ShapeDtypeStruct((rows, d), x.dtype), mesh=_sc_mesh())
    def scatter(x_hbm, i_hbm, o_hbm):
        def body(x_vmem, i_vmem):
            pltpu.sync_copy(x_vmem, o_hbm.at[i_vmem.at[0]])

        pltpu.emit_pipeline(
            body, grid=(slots, n // SC_WINDOW),
            in_specs=[pl.BlockSpec((SC_WINDOW, d), lambda s, i: (i, 0)),
                      pl.BlockSpec((1, SC_WINDOW), lambda s, i: (s, i))],
            out_specs=[],
            core_axis_name=("core", "subcore"),
            dimension_semantics=(pltpu.PARALLEL, pltpu.PARALLEL),
        )(x_hbm, i_hbm)

    return scatter(x, idx)


def _sc_gather_rows(x, idx):
    n = idx.shape[0]
    d = x.shape[1]

    @pl.kernel(out_type=jax.ShapeDtypeStruct((n, d), x.dtype), mesh=_sc_mesh())
    def gather(x_hbm, i_hbm, o_hbm):
        def body(i_vmem, o_vmem):
            pltpu.sync_copy(x_hbm.at[i_vmem.at[0]], o_vmem)

        pltpu.emit_pipeline(
            body, grid=(n // SC_WINDOW,),
            in_specs=[pl.BlockSpec((1, SC_WINDOW), lambda i: (0, i))],
            out_specs=[pl.BlockSpec((SC_WINDOW, d), lambda i: (i, 0))],
            core_axis_name=("core", "subcore"),
            dimension_semantics=(pltpu.PARALLEL,),
        )(i_hbm, o_hbm)

    return gather(x, idx.reshape(1, n))


def _moba_group_kernel(tb_ref, q_ref, *refs):
    t = MOBA_BLOCK
    k_refs, v_refs = refs[:GROUP_STEP], refs[GROUP_STEP:2 * GROUP_STEP]
    o_ref, lse_ref = refs[2 * GROUP_STEP:]
    g, step = pl.program_id(0), pl.program_id(1)
    for u in range(GROUP_STEP):
        used = tb_ref[g, step * GROUP_STEP + u] >= 0
        rows = slice(u * t, (u + 1) * t)
        o, lse = _block_attention(q_ref[rows, :].astype(BF16), k_refs[u][...], v_refs[u][...])
        o_ref[rows, :] = jnp.where(used, o, 0.0)
        lse_ref[rows, :] = jnp.where(used, lse, MASKED)


def _moba_group_attention(q_grouped, tile_blk, qk, v):
    bh, rows, _ = q_grouped.shape
    t = MOBA_BLOCK
    tiles = rows // t
    heads = A_HEADS

    def block_of(u, first_col):
        return lambda g, s, tb: (g // heads, jnp.maximum(tb[g, s * GROUP_STEP + u], 0),
                                 first_col + g % heads)

    row_tile = pl.BlockSpec((None, GROUP_STEP * t, HEAD_DIM), lambda g, s, tb: (g, s, 0))
    key_value = lambda first_col: [pl.BlockSpec((None, t, HEAD_DIM), block_of(u, first_col))
                                   for u in range(GROUP_STEP)]
    grid_spec = pltpu.PrefetchScalarGridSpec(
        num_scalar_prefetch=1,
        grid=(bh, tiles // GROUP_STEP),
        in_specs=[row_tile] + key_value(KA_BLK) + key_value(0),
        out_specs=[row_tile, row_tile],
    )
    return pl.pallas_call(
        _moba_group_kernel,
        grid_spec=grid_spec,
        out_shape=[jax.ShapeDtypeStruct(q_grouped.shape, F32)] * 2,
        compiler_params=_cparams("parallel", "parallel"),
        name="moba_group",
    )(tile_blk, q_grouped, *([qk] * GROUP_STEP), *([v] * GROUP_STEP))


def _moba_merge_kernel(q_ref, k_ref, v_ref, po_ref, pl_ref, o_ref):
    t = MOBA_BLOCK
    causal = (lax.broadcasted_iota(jnp.int32, (t, t), 1)
              <= lax.broadcasted_iota(jnp.int32, (t, t), 0))
    for hd in range(A_HEADS):
        sl = slice(hd * HEAD_DIM, (hd + 1) * HEAD_DIM)
        o_own, lse_own = _block_attention(q_ref[:, sl], k_ref[:, sl], v_ref[:, sl], causal)
        outs = [o_own] + [po_ref[s, hd] for s in range(MOBA_TOPK)]
        lses = [lse_own] + [pl_ref[s, hd] for s in range(MOBA_TOPK)]
        top = functools.reduce(jnp.maximum, lses)
        weights = [jnp.exp2(l - top) for l in lses]
        num = sum(w * o for w, o in zip(weights, outs))
        o_ref[:, sl] = (num / sum(weights)).astype(o_ref.dtype)


def _moba_merge(qk, v, part_o, part_lse):
    bsz, seq, _ = qk.shape
    t = MOBA_BLOCK
    part = pl.BlockSpec((MOBA_TOPK, None, A_HEADS, t, HEAD_DIM), lambda b, i: (0, b, 0, i, 0))
    return pl.pallas_call(
        _moba_merge_kernel,
        grid=(bsz, seq // t),
        in_specs=[pl.BlockSpec((None, t, A_W), lambda b, i: (b, i, QA_BLK * LANES // A_W)),
                  pl.BlockSpec((None, t, A_W), lambda b, i: (b, i, KA_BLK * LANES // A_W)),
                  pl.BlockSpec((None, t, A_W), lambda b, i: (b, i, 0)),
                  part, part],
        out_specs=pl.BlockSpec((None, t, A_W), lambda b, i: (b, i, 0)),
        out_shape=jax.ShapeDtypeStruct((bsz, seq, A_W), BF16),
        compiler_params=_cparams("parallel", "parallel"),
        name="moba_merge",
    )(qk, qk, v, part_o, part_lse)


def _moba_attention(qk, v, kmean):
    bsz, seq, _ = qk.shape
    bh = bsz * A_HEADS
    rows = _moba_tiles(seq) * MOBA_BLOCK
    ids, cnt, q_f32 = _moba_gate(qk, kmean)
    pos, tile_blk = _moba_routes(ids, cnt, seq)
    q_grouped = _sc_scatter_rows(q_f32.reshape(bh * seq, HEAD_DIM), pos, bh * rows)
    o_g, lse_g = _moba_group_attention(q_grouped.reshape(bh, rows, HEAD_DIM), tile_blk, qk, v)
    flat = pos.reshape(-1)
    back = lambda a: _sc_gather_rows(a.reshape(bh * rows, HEAD_DIM), flat).reshape(
        MOBA_TOPK, bsz, A_HEADS, seq, HEAD_DIM)
    return _moba_merge(qk, v, back(o_g), back(lse_g))


def _proj_dilated_kernel(h_ref, w_ref, c_ref, s_ref, q_ref, k_ref, v_ref, sc, *, d):
    acc = _dot(h_ref[...], w_ref[...])
    c, s = c_ref[...], s_ref[...]
    q_scale = HEAD_DIM ** -0.5 * LOG2E
    for j in range(acc.shape[1] // LANES):
        blk = acc[:, j * LANES:(j + 1) * LANES]
        if j < B_HEADS:
            blk = _rope128(blk, c, s) * q_scale
        elif j < 2 * B_HEADS:
            blk = _rope128(blk, c, s)
        sc[j] = blk
    rows = acc.shape[0] // d
    for r in range(d):
        for j in range(acc.shape[1] // LANES):
            dst = (q_ref, k_ref, v_ref)[j // B_HEADS]
            col = (j % B_HEADS) * LANES
            dst[r, :, col:col + LANES] = sc[j, pl.ds(r, rows, stride=d), :].astype(dst.dtype)


def _proj_dilated(h, w, rope_h, d, bsz, seq, tm=512):
    m, k = h.shape
    nt = seq // tm
    pos = pl.BlockSpec((tm, LANES), lambda i: (i % nt, 0))
    out = pl.BlockSpec((None, d, tm // d, B_W), lambda i: (i // nt, 0, i % nt, 0))
    return pl.pallas_call(
        functools.partial(_proj_dilated_kernel, d=d),
        grid=(m // tm,),
        in_specs=[pl.BlockSpec((tm, k), lambda i: (i, 0)), pl.BlockSpec(w.shape, lambda i: (0, 0)),
                  pos, pos],
        out_specs=[out] * 3,
        out_shape=[jax.ShapeDtypeStruct((bsz, d, seq // d, B_W), BF16)] * 3,
        scratch_shapes=[pltpu.VMEM((w.shape[1] // LANES, tm, LANES), F32)],
        compiler_params=_cparams("parallel"),
        name=f"proj_dilated_d{d}",
    )(h, w, *rope_h)


def _dilated_kernel(q_ref, kc_ref, kp_ref, vc_ref, vp_ref, o_ref, lse_ref, *, span):
    t, tp = q_ref.shape[0], kp_ref.shape[0]
    i = pl.program_id(2)
    dist = (lax.broadcasted_iota(jnp.int32, (t, t), 1)
            - lax.broadcasted_iota(jnp.int32, (t, t), 0))
    bias_c = jnp.where((dist >= 0) & (dist <= span), 0.0, MASKED)
    dist_p = (lax.broadcasted_iota(jnp.int32, (tp, t), 1) + tp
              - lax.broadcasted_iota(jnp.int32, (tp, t), 0))
    bias_p = jnp.where((dist_p <= span) & (i > 0), 0.0, MASKED)
    ones_c = jnp.ones((BF16_ROWS, t), BF16)
    ones_p = jnp.ones((BF16_ROWS, tp), BF16)

    def transposed(ref, sl):
        return ref[:, sl].astype(F32).T.astype(BF16)

    for j in range(B_HEADS):
        sl = slice(j * LANES, (j + 1) * LANES)
        qT = transposed(q_ref, sl)
        s_c = _dot(kc_ref[:, sl], qT) + bias_c
        s_p = _dot(kp_ref[:, sl], qT) + bias_p
        m = jnp.maximum(jnp.max(s_c, axis=0, keepdims=True), jnp.max(s_p, axis=0, keepdims=True))
        p_c = jnp.exp2(s_c - m).astype(BF16)
        p_p = jnp.exp2(s_p - m).astype(BF16)
        vt_c = jnp.concatenate([transposed(vc_ref, sl), ones_c], axis=0)
        vt_p = jnp.concatenate([transposed(vp_ref, sl), ones_p], axis=0)
        acc = _dot(vt_c, p_c) + _dot(vt_p, p_p)
        den = acc[V_DIM:V_DIM + 1, :]
        o_ref[:, sl] = (acc[:V_DIM, :] / den).T
        lse = m + jnp.log2(den)
        lse_ref[:, sl] = jnp.broadcast_to(lse, (LANES, t)).T


def _dilated_attention(q, k, v, span, t=512):
    bsz, d, length, _ = q.shape
    t = min(t, length)
    tp = B_QBLOCK
    assert span <= tp and t % tp == 0
    cur = pl.BlockSpec((None, None, t, B_W), lambda b, r, i: (b, r, i, 0))
    prev = pl.BlockSpec((None, None, tp, B_W),
                        lambda b, r, i: (b, r, jnp.maximum(i * (t // tp) - 1, 0), 0))
    return pl.pallas_call(
        functools.partial(_dilated_kernel, span=span),
        grid=(bsz, d, length // t),
        in_specs=[cur, cur, prev, cur, prev],
        out_specs=[cur, cur],
        out_shape=[jax.ShapeDtypeStruct(q.shape, F32)] * 2,
        compiler_params=_cparams("parallel", "parallel", "parallel"),
        name=f"dilated_d{d}",
    )(q, k, k, v, v)


def _natural_rows(ref, sc):
    d, rows = ref.shape[0], ref.shape[1]
    if d == 1:
        return ref[0]
    for r in range(d):
        for j in range(B_HEADS):
            sc[j, pl.ds(r, rows, stride=d), :] = ref[r, :, j * LANES:(j + 1) * LANES]
    return jnp.concatenate([sc[j] for j in range(B_HEADS)], axis=-1)


def _mixer_tail_kernel(x_ref, oa_ref, o0_ref, o1_ref, o2_ref, l0_ref, l1_ref, l2_ref, oc_ref,
                       g_ref, wpa_ref, wpb_ref, wpc_ref, wo_ref, y_ref, *scratch):
    o0, o1, o2, l0, l1, l2 = [
        _natural_rows(ref, sc)
        for ref, sc in zip((o0_ref, o1_ref, o2_ref, l0_ref, l1_ref, l2_ref), scratch)]
    mx = jnp.maximum(jnp.maximum(l0, l1), l2)
    e0, e1, e2 = jnp.exp2(l0 - mx), jnp.exp2(l1 - mx), jnp.exp2(l2 - mx)
    ob = (e0 * o0 + e1 * o1 + e2 * o2) / (e0 + e1 + e2)
    pa = _dot(oa_ref[...], wpa_ref[...])
    pb = _dot(ob.astype(BF16), wpb_ref[...])
    pc = _dot(oc_ref[...], wpc_ref[...])
    d = D_MODEL
    merged = (g_ref[:, 0:d].astype(F32) * pa + g_ref[:, d:2 * d].astype(F32) * pb
              + g_ref[:, 2 * d:3 * d].astype(F32) * pc)
    y_ref[...] = x_ref[...] + _dot(merged.astype(BF16), wo_ref[...])


def _mixer_tail(x, out_a, o_groups, lse_groups, out_c, gates, w_pa, w_pb, w_pc, w_o, seq, tm=256):
    m, d = x.shape
    nt = seq // tm
    row = lambda width: pl.BlockSpec((tm, width), lambda i: (i, 0))
    residue = lambda g: pl.BlockSpec((None, g.shape[1], tm // g.shape[1], B_W),
                                     lambda i: (i // nt, 0, i % nt, 0))
    weights = [_resident(w) for w in (w_pa, w_pb, w_pc, w_o)]
    groups = list(o_groups) + list(lse_groups)
    return pl.pallas_call(
        _mixer_tail_kernel,
        grid=(m // tm,),
        in_specs=([row(d), row(A_W)] + [residue(g) for g in groups]
                  + [row(C_W), row(3 * d)] + weights),
        out_specs=row(d),
        out_shape=jax.ShapeDtypeStruct((m, d), F32),
        scratch_shapes=[pltpu.VMEM((B_HEADS, tm, LANES), F32) for _ in groups],
        compiler_params=_cparams("parallel"),
        name="mixer_tail",
    )(x, out_a, *groups, out_c, gates, w_pa, w_pb, w_pc, w_o)


def _mem_kv_kernel(mem_ref, g_ref, wk_ref, wv_ref, k_ref, v_ref):
    memn = _rms(mem_ref[...], g_ref[...]).astype(BF16)
    k_ref[...] = _dot(memn, wk_ref[...]).astype(k_ref.dtype)
    v_ref[...] = _dot(memn, wv_ref[...]).astype(v_ref.dtype)


def _mem_kv(mem, g, wk, wv):
    bsz, n, d = mem.shape
    out = pl.BlockSpec((None, n, X_W), lambda b: (b, 0, 0))
    return pl.pallas_call(
        _mem_kv_kernel,
        grid=(bsz,),
        in_specs=[pl.BlockSpec((None, n, d), lambda b: (b, 0, 0)),
                  pl.BlockSpec((1, d), lambda b: (0, 0)),
                  pl.BlockSpec(wk.shape, lambda b: (0, 0)),
                  pl.BlockSpec(wv.shape, lambda b: (0, 0))],
        out_specs=[out, out],
        out_shape=[jax.ShapeDtypeStruct((bsz, n, X_W), BF16)] * 2,
        compiler_params=_cparams("parallel"),
        name="mem_kv",
    )(mem, g.reshape(1, d), wk, wv)


def _mem_attn_kernel(x_ref, g_ref, wq_ref, k_ref, v_ref, wo_ref, y_ref):
    x = x_ref[...]
    h = _rms(x, g_ref[...]).astype(BF16)
    q = (_dot(h, wq_ref[...]) * HEAD_DIM ** -0.5).astype(BF16)
    heads = []
    for hd in range(X_HEADS):
        sl = slice(hd * HEAD_DIM, (hd + 1) * HEAD_DIM)
        s = _dot_nt(q[:, sl], k_ref[:, sl])
        p = jnp.exp(s - jnp.max(s, axis=-1, keepdims=True))
        o = _dot(p.astype(BF16), v_ref[:, sl]) / jnp.sum(p, axis=-1, keepdims=True)
        heads.append(o.astype(BF16))
    y_ref[...] = x + _dot(jnp.concatenate(heads, axis=-1), wo_ref[...])


def _mem_attention(x, g, wq, kmem, vmem, wo, seq, tm=512):
    m, d = x.shape
    nt = seq // tm
    n = kmem.shape[1]
    kv = pl.BlockSpec((None, n, X_W), lambda i: (i // nt, 0, 0))
    return pl.pallas_call(
        _mem_attn_kernel,
        grid=(m // tm,),
        in_specs=[pl.BlockSpec((tm, d), lambda i: (i, 0)),
                  pl.BlockSpec((1, d), lambda i: (0, 0)),
                  pl.BlockSpec(wq.shape, lambda i: (0, 0)), kv, kv,
                  pl.BlockSpec(wo.shape, lambda i: (0, 0))],
        out_specs=pl.BlockSpec((tm, d), lambda i: (i, 0)),
        out_shape=jax.ShapeDtypeStruct((m, d), F32),
        compiler_params=_cparams("parallel"),
        name="mem_attention",
    )(x, g.reshape(1, d), wq, kmem, vmem, wo)


def _ffn_kernel(x_ref, halo_ref, g_ref, wg_ref, wv_ref, cwg_ref, cwv_ref, cbg_ref, cbv_ref,
                wd_ref, y_ref, h_sc, acc_sc, *, tiles_per_seq):
    i = pl.program_id(0)
    f = pl.program_id(1)
    tm = x_ref.shape[0]

    @pl.when(f == 0)
    def _():
        g = g_ref[...]
        keep = (i % tiles_per_seq != 0).astype(F32)
        h_sc[0:HALO, :] = (_rms(halo_ref[...], g) * keep).astype(h_sc.dtype)
        h_sc[HALO:, :] = _rms(x_ref[...], g).astype(h_sc.dtype)
        acc_sc[...] = jnp.zeros_like(acc_sc)

    h = h_sc[...]

    def conv(w_ref, cw_ref, cb_ref):
        u = _dot(h, w_ref[...])
        c = cb_ref[...]
        for tap in range(CONV_W):
            lo = HALO - (CONV_W - 1) + tap
            c = c + cw_ref[tap:tap + 1, :] * u[lo:lo + tm, :]
        return c

    act = jax.nn.silu(conv(wg_ref, cwg_ref, cbg_ref)) * conv(wv_ref, cwv_ref, cbv_ref)
    acc_sc[...] += _dot(act.astype(BF16), wd_ref[...])

    @pl.when(f == pl.num_programs(1) - 1)
    def _():
        y_ref[...] = x_ref[...] + acc_sc[...]


def _conv_ffn(x, g, w_up, conv_w, conv_b, w_down, seq, tm=512):
    m, d = x.shape
    tf = FFN_TF
    nf = D_FF_PAD // tf
    halo_blocks = tm // HALO
    return pl.pallas_call(
        functools.partial(_ffn_kernel, tiles_per_seq=seq // tm),
        grid=(m // tm, nf),
        in_specs=[pl.BlockSpec((tm, d), lambda i, f: (i, 0)),
                  pl.BlockSpec((HALO, d), lambda i, f: (jnp.maximum(i * halo_blocks - 1, 0), 0)),
                  pl.BlockSpec((1, d), lambda i, f: (0, 0)),
                  pl.BlockSpec((d, tf), lambda i, f: (0, f)),
                  pl.BlockSpec((d, tf), lambda i, f: (0, f + nf)),
                  pl.BlockSpec((CONV_W, tf), lambda i, f: (0, f)),
                  pl.BlockSpec((CONV_W, tf), lambda i, f: (0, f + nf)),
                  pl.BlockSpec((1, tf), lambda i, f: (0, f)),
                  pl.BlockSpec((1, tf), lambda i, f: (0, f + nf)),
                  pl.BlockSpec((tf, d), lambda i, f: (f, 0))],
        out_specs=pl.BlockSpec((tm, d), lambda i, f: (i, 0)),
        out_shape=jax.ShapeDtypeStruct((m, d), F32),
        scratch_shapes=[pltpu.VMEM((HALO + tm, d), BF16), pltpu.VMEM((tm, d), F32)],
        compiler_params=_cparams("parallel", "arbitrary"),
        name="conv_ffn",
    )(x, x, g.reshape(1, d), w_up, w_up, conv_w, conv_w, conv_b, conv_b, w_down)


def _rope_tables(seq):
    def angles(dim):
        inv_freq = jnp.exp(jnp.arange(0, dim, 2, dtype=F32) * (-math.log(ROPE_THETA) / dim))
        ang = jnp.arange(seq, dtype=F32)[:, None] * inv_freq[None, :]
        return jnp.cos(ang), jnp.sin(ang)

    cos_h, sin_h = angles(HEAD_DIM)
    rope_h = (jnp.concatenate([cos_h, cos_h], axis=-1), jnp.concatenate([-sin_h, sin_h], axis=-1))
    cos_r, sin_r = angles(ROPE_DIM)
    z = jnp.zeros_like(cos_r)
    rope_r = (jnp.concatenate([cos_r, cos_r, z, z], axis=-1),
              jnp.concatenate([-sin_r, z, z, z], axis=-1),
              jnp.concatenate([z, sin_r, z, z], axis=-1))
    return rope_h, rope_r


def _split_in(w_in):
    return [w_in[:, IN_OFFSETS[k]:IN_OFFSETS[k + 1]] for k in range(len(IN_WIDTHS))]


def _pad_cols(w, width):
    return jnp.pad(w, ((0, 0), (0, width - w.shape[1])))


def _layer_params(w_in, w_uq, w_ukv, w_up, conv_w, conv_b, w_down):
    qa, ka, va, qb, kb, vb, cq, ckv, kr, gates = _split_in(w_in)
    w_qk = jnp.concatenate([qa, ka], axis=1).astype(BF16)
    group_cols = lambda w, g: w[:, g * B_W:(g + 1) * B_W]
    w_b = [jnp.concatenate([group_cols(qb, g), group_cols(kb, g), group_cols(vb, g)],
                           axis=1).astype(BF16) for g in range(len(B_GROUPS))]
    w_down_in = jnp.concatenate([cq, ckv, _pad_cols(kr, LANES)], axis=1).astype(BF16)
    uq = w_uq.reshape(Q_LORA, C_HEADS, NOPE_DIM + ROPE_DIM)
    uq = jnp.pad(uq, ((0, 0), (0, 0), (0, C_QK - NOPE_DIM - ROPE_DIM)))
    ukv = w_ukv.reshape(KV_LORA, C_HEADS, NOPE_DIM + V_DIM)
    pad_ff = lambda w: jnp.pad(w, ((0, 0), (0, D_FF_PAD - D_FF)))
    two_halves = lambda w: jnp.concatenate([pad_ff(w[:, :D_FF]), pad_ff(w[:, D_FF:])], axis=1)
    return dict(
        w_qk=w_qk, w_va=va.astype(BF16), w_b=w_b, w_gates=gates.astype(BF16),
        w_down_in=w_down_in,
        w_uq=uq.reshape(Q_LORA, C_HEADS * C_QK).astype(BF16),
        w_uk=ukv[:, :, :NOPE_DIM].reshape(KV_LORA, C_HEADS * NOPE_DIM).astype(BF16),
        w_uv=ukv[:, :, NOPE_DIM:].reshape(KV_LORA, C_W).astype(BF16),
        w_up=two_halves(w_up).astype(BF16),
        conv_w=two_halves(conv_w),
        conv_b=two_halves(conv_b.reshape(1, -1)),
        w_down=jnp.pad(w_down, ((0, D_FF_PAD - D_FF), (0, 0))).astype(BF16),
    )


def _qk_col_scale():
    q_scale = HEAD_DIM ** -0.5
    parts = [jnp.full((A_W,), q_scale * LOG2E, F32), jnp.ones((A_W,), F32)]
    return jnp.concatenate(parts).reshape(1, QK_W)


def _mixer(x, g_mix, p, g_cq, g_ckv, w_pa, w_pb, w_pc, w_o, rope_h, rope_r, bsz, seq):
    m = x.shape[0]
    h = _rmsnorm(x, g_mix, BF16)
    qk = _matmul(h, p["w_qk"], _mm_rope_kernel, BF16, 1024, 1024, seq=seq,
                 extras=(("col", _qk_col_scale()), ("pos", rope_h[0]), ("pos", rope_h[1])),
                 name="proj_qk_rope")
    v_a = _matmul(h, p["w_va"], _mm_plain_kernel, BF16, 1024, A_W, name="proj_va")
    gates = _matmul(h, p["w_gates"], _mm_sigmoid_kernel, BF16, 1024, 1024, name="proj_gates")
    cq, ckv, kr = _mla_down(h, p["w_down_in"], g_cq, g_ckv, rope_r, seq)
    q_c = _mla_q(cq, p["w_uq"], rope_r, seq)
    k_c, vt_c = _mla_kv(ckv, kr, p["w_uk"], p["w_uv"], bsz, seq)

    qk3 = qk.reshape(bsz, seq, QK_W)
    out_a = _moba_attention(qk3, v_a.reshape(bsz, seq, A_W), _kmean(qk3)).reshape(m, A_W)
    groups = []
    for (window, d), w_g in zip(B_GROUPS, p["w_b"]):
        q_g, k_g, v_g = _proj_dilated(h, w_g, rope_h, d, bsz, seq)
        groups.append(_dilated_attention(q_g, k_g, v_g, window // d))
    out_c = _flash_attention(q_c.reshape(bsz, seq, -1), k_c.reshape(bsz, seq, -1), vt_c,
                             C_HEADS, C_QK).reshape(m, C_W)
    return _mixer_tail(x, out_a, [g[0] for g in groups], [g[1] for g in groups], out_c, gates,
                       w_pa.astype(BF16), w_pb.astype(BF16), w_pc.astype(BF16), w_o.astype(BF16),
                       seq)


def kernel(x, mem, g_mix, w_in, g_cq, g_ckv, w_uq, w_ukv, w_pa, w_pb, w_pc, w_o, g_mem, g_memkv,
           w_xq, w_xk, w_xv, w_xo, g_ffn, w_up, conv_w, conv_b, w_down, g_final):
    bsz, seq, d = x.shape
    rope_h, rope_r = _rope_tables(seq)
    xf = x.reshape(bsz * seq, d)
    for l in range(DEPTH):
        p = _layer_params(w_in[l], w_uq[l], w_ukv[l], w_up[l], conv_w[l], conv_b[l], w_down[l])
        xf = _mixer(xf, g_mix[l], p, g_cq[l], g_ckv[l], w_pa[l], w_pb[l], w_pc[l], w_o[l],
                    rope_h, rope_r, bsz, seq)
        kmem, vmem = _mem_kv(mem, g_memkv[l], w_xk[l].astype(BF16), w_xv[l].astype(BF16))
        xf = _mem_attention(xf, g_mem[l], w_xq[l].astype(BF16), kmem, vmem,
                            w_xo[l].astype(BF16), seq)
        xf = _conv_ffn(xf, g_ffn[l], p["w_up"], p["conv_w"], p["conv_b"], p["w_down"], seq)
    return _rmsnorm(xf, g_final, F32).reshape(bsz, seq, d)
```

```python
import functools
import math

import jax
import jax.numpy as jnp
import numpy as np
from jax import lax
from jax.experimental import pallas as pl
from jax.experimental.pallas import tpu as pltpu
from jax.experimental.pallas import tpu_sc as plsc

F32 = jnp.float32
BF16 = jnp.bfloat16

LANES = 128
SUBLANES = 8
VMEM_LIMIT = 56 * 1024 * 1024

D_MODEL = 2048
DEPTH = 2
HEAD_DIM = 128
ROPE_THETA = 10000.0
EPS = 1e-6

A_HEADS = 4
MOBA_BLOCK = 256
MOBA_TOPK = 3

B_GROUPS = ((128, 1), (512, 4), (2048, 16))
B_HEADS = 4
B_QBLOCK = 128

C_HEADS = 8
Q_LORA = 1536
KV_LORA = 512
NOPE_DIM = 128
ROPE_DIM = 64
V_DIM = 128

X_HEADS = 4
D_FF = 5504
CONV_W = 3

A_W = A_HEADS * HEAD_DIM
B_QKV_W = len(B_GROUPS) * B_HEADS * HEAD_DIM
B_W = B_HEADS * HEAD_DIM
C_W = C_HEADS * V_DIM
X_W = X_HEADS * HEAD_DIM
IN_WIDTHS = (A_W, A_W, A_W, B_QKV_W, B_QKV_W, B_QKV_W, Q_LORA, KV_LORA, ROPE_DIM, 3 * D_MODEL)
IN_OFFSETS = tuple(int(o) for o in np.cumsum((0,) + IN_WIDTHS))

QK_W = 2 * A_W
QA_BLK, KA_BLK = 0, A_W // LANES

C_QK = 2 * LANES
MASKED = -1e30
LOG2E = math.log2(math.e)
BF16_ROWS = 16
VT_ROWS = V_DIM + BF16_ROWS
GROUP_STEP = 8
SC_WINDOW = 128

D_FF_PAD = 5632
FFN_TF = 512
HALO = SUBLANES


def _cparams(*sem):
    return pltpu.CompilerParams(dimension_semantics=sem, vmem_limit_bytes=VMEM_LIMIT)


def _resident(arr):
    zeros = (0,) * arr.ndim
    return pl.BlockSpec(arr.shape, lambda *_: zeros, pipeline_mode=pl.Buffered(1))


def _dot(a, b):
    return jnp.dot(a, b, preferred_element_type=F32)


def _dot_nt(a, b):
    return lax.dot_general(a, b, (((1,), (1,)), ((), ())), preferred_element_type=F32)


def _rms(x, g):
    return x * lax.rsqrt(jnp.mean(x * x, axis=-1, keepdims=True) + EPS) * g


def _rmsnorm_kernel(x_ref, g_ref, o_ref):
    o_ref[...] = _rms(x_ref[...], g_ref[...]).astype(o_ref.dtype)


def _rmsnorm(x, g, out_dtype, tm=512):
    m, d = x.shape
    return pl.pallas_call(
        _rmsnorm_kernel,
        grid=(m // tm,),
        in_specs=[pl.BlockSpec((tm, d), lambda i: (i, 0)),
                  pl.BlockSpec((1, d), lambda i: (0, 0))],
        out_specs=pl.BlockSpec((tm, d), lambda i: (i, 0)),
        out_shape=jax.ShapeDtypeStruct((m, d), out_dtype),
        compiler_params=_cparams("parallel"),
        name="rmsnorm",
    )(x, g.reshape(1, d))


def _rope128(x, c, s):
    return x * c + pltpu.roll(x, HEAD_DIM // 2, 1) * s


def _rope64(x, c, sa, sb):
    half = ROPE_DIM // 2
    return x * c + pltpu.roll(x, LANES - half, 1) * sa + pltpu.roll(x, half, 1) * sb


def _mm_plain_kernel(a_ref, w_ref, o_ref):
    o_ref[...] = _dot(a_ref[...], w_ref[...]).astype(o_ref.dtype)


def _mm_sigmoid_kernel(a_ref, w_ref, o_ref):
    o_ref[...] = jax.nn.sigmoid(_dot(a_ref[...], w_ref[...])).astype(o_ref.dtype)


def _mm_rope_kernel(a_ref, w_ref, cs_ref, c_ref, s_ref, o_ref):
    acc = _dot(a_ref[...], w_ref[...])
    c = c_ref[...]
    s = s_ref[...]
    for j in range(acc.shape[1] // LANES):
        sl = slice(j * LANES, (j + 1) * LANES)
        o_ref[:, sl] = (_rope128(acc[:, sl], c, s) * cs_ref[:, sl]).astype(o_ref.dtype)


def _matmul(a, w, kernel, out_dtype, tm, tn, seq=None, extras=(), name="matmul"):
    m, k = a.shape
    n = w.shape[1]
    in_specs = [pl.BlockSpec((tm, k), lambda i, j: (i, 0)),
                pl.BlockSpec((k, tn), lambda i, j: (0, j))]
    args = [a, w]
    for kind, arr in extras:
        if kind == "col":
            in_specs.append(pl.BlockSpec((1, tn), lambda i, j: (0, j)))
        else:
            nt = seq // tm
            in_specs.append(pl.BlockSpec((tm, LANES), lambda i, j: (i % nt, 0)))
        args.append(arr)
    return pl.pallas_call(
        kernel,
        grid=(m // tm, n // tn),
        in_specs=in_specs,
        out_specs=pl.BlockSpec((tm, tn), lambda i, j: (i, j)),
        out_shape=jax.ShapeDtypeStruct((m, n), out_dtype),
        compiler_params=_cparams("parallel", "parallel"),
        name=name,
    )(*args)


def _mla_down_kernel(h_ref, w_ref, gq_ref, gkv_ref, c_ref, sa_ref, sb_ref,
                     cq_ref, ckv_ref, kr_ref):
    acc = _dot(h_ref[...], w_ref[...])
    cq_ref[...] = _rms(acc[:, :Q_LORA], gq_ref[...]).astype(cq_ref.dtype)
    ckv_ref[...] = _rms(acc[:, Q_LORA:Q_LORA + KV_LORA], gkv_ref[...]).astype(ckv_ref.dtype)
    kr = acc[:, Q_LORA + KV_LORA:]
    kr_ref[...] = _rope64(kr, c_ref[...], sa_ref[...], sb_ref[...]).astype(kr_ref.dtype)


def _mla_down(h, w, g_cq, g_ckv, rope_r, seq, tm=512):
    m, k = h.shape
    n = w.shape[1]
    nt = seq // tm
    row = lambda width: pl.BlockSpec((tm, width), lambda i: (i, 0))
    full = lambda r, c: pl.BlockSpec((r, c), lambda i: (0, 0))
    pos = pl.BlockSpec((tm, LANES), lambda i: (i % nt, 0))
    return pl.pallas_call(
        _mla_down_kernel,
        grid=(m // tm,),
        in_specs=[row(k), full(k, n), full(1, Q_LORA), full(1, KV_LORA), pos, pos, pos],
        out_specs=[row(Q_LORA), row(KV_LORA), row(LANES)],
        out_shape=[jax.ShapeDtypeStruct((m, Q_LORA), BF16),
                   jax.ShapeDtypeStruct((m, KV_LORA), BF16),
                   jax.ShapeDtypeStruct((m, LANES), BF16)],
        compiler_params=_cparams("parallel"),
        name="mla_down",
    )(h, w, g_cq.reshape(1, -1), g_ckv.reshape(1, -1), *rope_r)


def _mla_q_kernel(cq_ref, w_ref, c_ref, sa_ref, sb_ref, q_ref, *, scale):
    acc = _dot(cq_ref[...], w_ref[...])
    c, sa, sb = c_ref[...], sa_ref[...], sb_ref[...]
    for hd in range(C_HEADS):
        lo = hd * C_QK
        q_ref[:, lo:lo + LANES] = (acc[:, lo:lo + LANES] * scale).astype(q_ref.dtype)
        rope = _rope64(acc[:, lo + LANES:lo + C_QK], c, sa, sb)
        q_ref[:, lo + LANES:lo + C_QK] = (rope * scale).astype(q_ref.dtype)


def _mla_q(cq, w, rope_r, seq, tm=512):
    m, k = cq.shape
    n = w.shape[1]
    nt = seq // tm
    pos = pl.BlockSpec((tm, LANES), lambda i: (i % nt, 0))
    return pl.pallas_call(
        functools.partial(_mla_q_kernel, scale=(NOPE_DIM + ROPE_DIM) ** -0.5 * LOG2E),
        grid=(m // tm,),
        in_specs=[pl.BlockSpec((tm, k), lambda i: (i, 0)),
                  pl.BlockSpec((k, n), lambda i: (0, 0)), pos, pos, pos],
        out_specs=pl.BlockSpec((tm, n), lambda i: (i, 0)),
        out_shape=jax.ShapeDtypeStruct((m, n), BF16),
        compiler_params=_cparams("parallel"),
        name="mla_q",
    )(cq, w, *rope_r)


def _store_vt(v, vt_ref):
    vt = v.T
    for hd in range(vt_ref.shape[0]):
        vt_ref[hd, 0:V_DIM, :] = vt[hd * V_DIM:(hd + 1) * V_DIM, :].astype(vt_ref.dtype)
        vt_ref[hd, V_DIM:VT_ROWS, :] = jnp.ones((VT_ROWS - V_DIM, vt.shape[1]), vt_ref.dtype)


def _mla_kv_kernel(ckv_ref, kr_ref, wk_ref, wv_ref, k_ref, vt_ref):
    ckv = ckv_ref[...]
    kn = _dot(ckv, wk_ref[...])
    kr = kr_ref[...]
    for hd in range(C_HEADS):
        lo = hd * C_QK
        k_ref[:, lo:lo + LANES] = kn[:, hd * LANES:(hd + 1) * LANES].astype(k_ref.dtype)
        k_ref[:, lo + LANES:lo + C_QK] = kr
    _store_vt(_dot(ckv, wv_ref[...]), vt_ref)


def _mla_kv(ckv, kr, wk, wv, bsz, seq, tm=512):
    m, k = ckv.shape
    nt = seq // tm
    return pl.pallas_call(
        _mla_kv_kernel,
        grid=(m // tm,),
        in_specs=[pl.BlockSpec((tm, k), lambda i: (i, 0)),
                  pl.BlockSpec((tm, LANES), lambda i: (i, 0)),
                  pl.BlockSpec(wk.shape, lambda i: (0, 0)),
                  pl.BlockSpec(wv.shape, lambda i: (0, 0))],
        out_specs=[pl.BlockSpec((tm, C_HEADS * C_QK), lambda i: (i, 0)),
                   pl.BlockSpec((None, C_HEADS, VT_ROWS, tm), lambda i: (i // nt, 0, 0, i % nt))],
        out_shape=[jax.ShapeDtypeStruct((m, C_HEADS * C_QK), BF16),
                   jax.ShapeDtypeStruct((bsz, C_HEADS, VT_ROWS, seq), BF16)],
        compiler_params=_cparams("parallel"),
        name="mla_kv",
    )(ckv, kr, wk, wv)


def _proj_vt_kernel(h_ref, w_ref, vt_ref):
    _store_vt(_dot(h_ref[...], w_ref[...]), vt_ref)


def _proj_vt(h, w, heads, bsz, seq, tm=512):
    m, k = h.shape
    nt = seq // tm
    return pl.pallas_call(
        _proj_vt_kernel,
        grid=(m // tm,),
        in_specs=[pl.BlockSpec((tm, k), lambda i: (i, 0)), pl.BlockSpec(w.shape, lambda i: (0, 0))],
        out_specs=pl.BlockSpec((None, heads, VT_ROWS, tm), lambda i: (i // nt, 0, 0, i % nt)),
        out_shape=jax.ShapeDtypeStruct((bsz, heads, VT_ROWS, seq), BF16),
        compiler_params=_cparams("parallel"),
        name="proj_vt",
    )(h, w)


def _attend_chunks(qT, k_ref, vt_ref, scratch, *, tk, n_full, mask_main, mask_tail,
                   unroll=4, tail_steps=2):
    m_sc, acc_sc, sa_sc, sb_sc, pa_sc, pb_sc, ala_sc, alb_sc = scratch
    s_bufs = (sa_sc, sb_sc)
    p_bufs = (pa_sc, pb_sc)
    al_bufs = (ala_sc, alb_sc)
    last_chunk = k_ref.shape[0] // tk - 1
    m_sc[...] = jnp.full(m_sc.shape, MASKED, F32)
    acc_sc[...] = jnp.zeros(acc_sc.shape, F32)
    for p_ref, al_ref in zip(p_bufs, al_bufs):
        p_ref[...] = jnp.zeros(p_ref.shape, p_ref.dtype)
        al_ref[...] = jnp.ones(al_ref.shape, F32)

    def rows(c):
        return pl.ds(pl.multiple_of(jnp.clip(c, 0, last_chunk) * tk, tk), tk)

    def scores(c):
        return _dot(k_ref[rows(c), :], qT)

    def flush(c, slot):
        acc_sc[...] = (al_bufs[slot][...] * acc_sc[...]
                       + _dot(vt_ref[:, rows(c)], p_bufs[slot][...]))

    def softmax(sT, slot):
        m_old = m_sc[...]
        m_new = jnp.maximum(m_old, jnp.max(sT, axis=0, keepdims=True))
        al_bufs[slot][...] = jnp.exp2(m_old - m_new)
        p_bufs[slot][...] = jnp.exp2(sT - m_new).astype(p_bufs[slot].dtype)
        m_sc[...] = m_new

    def step(tau, slot, mask, prefetch):
        flush(tau - 2, slot)
        if prefetch:
            s_bufs[1 - slot][...] = scores(tau + 1)
        sT = s_bufs[slot][...]
        softmax(sT if mask is None else mask(sT, tau), slot)

    def steps(tau0, count, mask, prefetch_last=True):
        for j in range(count):
            step(tau0 + j, j % 2, mask, prefetch_last or j < count - 1)

    sa_sc[...] = scores(0)
    trips = n_full // unroll
    lax.fori_loop(0, trips, lambda u, c: (steps(unroll * u, unroll, mask_main), c)[1], 0)
    done = unroll * trips
    pairs = (n_full - done) // 2
    lax.fori_loop(0, pairs, lambda u, c: (steps(done + 2 * u, 2, mask_main), c)[1], 0)
    tau = done + 2 * pairs
    steps(tau, tail_steps, mask_tail, prefetch_last=False)
    flush(tau + tail_steps - 2, 0)
    flush(tau + tail_steps - 1, 1)
    acc = acc_sc[...]
    return acc[:V_DIM, :] / acc[V_DIM:V_DIM + 1, :]


def _attend_scratch(tq, tk):
    return [pltpu.VMEM((1, tq), F32), pltpu.VMEM((VT_ROWS, tq), F32),
            pltpu.VMEM((tk, tq), F32), pltpu.VMEM((tk, tq), F32),
            pltpu.VMEM((tk, tq), BF16), pltpu.VMEM((tk, tq), BF16),
            pltpu.VMEM((1, tq), F32), pltpu.VMEM((1, tq), F32)]


def _transpose_q(q_ref):
    return q_ref[...].astype(F32).T.astype(BF16)


def _flash_kernel(q_ref, k_ref, vt_ref, o_ref, *scratch, tq, tk):
    i = pl.program_id(2)

    def causal(sT, c):
        key = lax.broadcasted_iota(jnp.int32, sT.shape, 0) + c * tk
        qry = lax.broadcasted_iota(jnp.int32, sT.shape, 1) + i * tq
        return jnp.where(key <= qry, sT, MASKED)

    per_tile = tq // tk
    oT = _attend_chunks(_transpose_q(q_ref), k_ref, vt_ref, scratch, tk=tk, n_full=i * per_tile,
                        mask_main=None, mask_tail=causal, unroll=8, tail_steps=per_tile)
    o_ref[...] = oT.T.astype(o_ref.dtype)


def _flash_attention(q, k, vt, heads, qk_w, tq=1024, tk=256):
    bsz, seq, _ = q.shape
    assert tq % (2 * tk) == 0 and seq % tq == 0
    return pl.pallas_call(
        functools.partial(_flash_kernel, tq=tq, tk=tk),
        grid=(bsz, heads, seq // tq),
        in_specs=[pl.BlockSpec((None, tq, qk_w), lambda b, h, i: (b, i, h)),
                  pl.BlockSpec((None, seq, qk_w), lambda b, h, i: (b, 0, h)),
                  pl.BlockSpec((None, None, VT_ROWS, seq), lambda b, h, i: (b, h, 0, 0))],
        out_specs=pl.BlockSpec((None, tq, V_DIM), lambda b, h, i: (b, i, h)),
        out_shape=jax.ShapeDtypeStruct((bsz, seq, heads * V_DIM), BF16),
        scratch_shapes=_attend_scratch(tq, tk),
        compiler_params=_cparams("parallel", "parallel", "arbitrary"),
        name="mla_flash",
    )(q, k, vt)


def _kmean_kernel(k_ref, o_ref):
    k = k_ref[...].astype(F32)
    o_ref[...] = jnp.mean(k.reshape(SUBLANES, MOBA_BLOCK, k.shape[-1]), axis=1)


def _kmean(qk):
    bsz, seq, _ = qk.shape
    rows = SUBLANES * MOBA_BLOCK
    return pl.pallas_call(
        _kmean_kernel,
        grid=(bsz, seq // rows),
        in_specs=[pl.BlockSpec((None, rows, A_W), lambda b, i: (b, i, KA_BLK * LANES // A_W))],
        out_specs=pl.BlockSpec((None, SUBLANES, A_W), lambda b, i: (b, i, 0)),
        out_shape=jax.ShapeDtypeStruct((bsz, seq // MOBA_BLOCK, A_W), F32),
        compiler_params=_cparams("parallel", "parallel"),
        name="moba_kmean",
    )(qk)


def _block_attention(q, k, v, visible=None):
    s = _dot_nt(q, k)
    if visible is not None:
        s = jnp.where(visible, s, MASKED)
    m = jnp.max(s, axis=-1, keepdims=True)
    p = jnp.exp2(s - m).astype(BF16)
    v_ones = jnp.concatenate([v, jnp.ones((v.shape[0], LANES), v.dtype)], axis=-1)
    acc = _dot(p, v_ones)
    den = acc[:, V_DIM:]
    return acc[:, :V_DIM] / den, m + jnp.log2(den)


def _moba_gate_kernel(q_ref, km_ref, ids_ref, cnt_ref, qf_ref):
    t = MOBA_BLOCK
    i = pl.program_id(1)
    nb = km_ref.shape[0]
    blk = lax.broadcasted_iota(jnp.int32, (nb, t), 0)
    neg_inf = jnp.float32(-jnp.inf)
    not_after = (lax.broadcasted_iota(jnp.int32, (t, t), 0)
                 <= lax.broadcasted_iota(jnp.int32, (t, t), 1))
    upper = jnp.where(not_after, 1.0, 0.0).astype(BF16)
    ones = jnp.ones((SUBLANES, t), BF16)
    for hd in range(A_HEADS):
        sl = slice(hd * HEAD_DIM, (hd + 1) * HEAD_DIM)
        q = q_ref[:, sl].astype(F32)
        qf_ref[hd] = q
        qT = q.T.astype(BF16)
        km = km_ref[:, sl]
        km_hi = km.astype(BF16)
        km_lo = (km - km_hi.astype(F32)).astype(BF16)
        g = jnp.where(blk < i, _dot(km_hi, qT) + _dot(km_lo, qT), neg_inf)
        picks, ranks, counts = [], [], []
        for _ in range(MOBA_TOPK):
            mx = jnp.max(g, axis=0, keepdims=True)
            is_max = (g == mx) & (mx > neg_inf)
            first = jnp.min(jnp.where(is_max, blk, nb), axis=0, keepdims=True)
            pick = blk == first
            g = jnp.where(pick, neg_inf, g)
            onehot = jnp.where(pick, 1.0, 0.0).astype(BF16)
            before = _dot(onehot, upper)
            rank = jnp.sum(jnp.where(pick, before - 1.0, 0.0), axis=0, keepdims=True)
            picks.append(first)
            ranks.append(rank.astype(jnp.int32))
            counts.append(_dot_nt(ones, onehot)[0:1, :])
        pad_i = jnp.zeros((SUBLANES - 2 * MOBA_TOPK, t), jnp.int32)
        ids_ref[hd] = jnp.concatenate(picks + ranks + [pad_i], axis=0)
        pad_f = jnp.zeros((SUBLANES - MOBA_TOPK, nb), F32)
        cnt_ref[hd] = jnp.concatenate(counts + [pad_f], axis=0)


def _moba_gate(qk, kmean):
    bsz, seq, _ = qk.shape
    t = MOBA_BLOCK
    nb = seq // t
    return pl.pallas_call(
        _moba_gate_kernel,
        grid=(bsz, nb),
        in_specs=[pl.BlockSpec((None, t, A_W), lambda b, i: (b, i, QA_BLK * LANES // A_W)),
                  pl.BlockSpec((None, nb, A_W), lambda b, i: (b, 0, 0))],
        out_specs=[pl.BlockSpec((None, A_HEADS, SUBLANES, t), lambda b, i: (b, 0, 0, i)),
                   pl.BlockSpec((None, A_HEADS, None, SUBLANES, nb), lambda b, i: (b, 0, i, 0, 0)),
                   pl.BlockSpec((None, A_HEADS, t, HEAD_DIM), lambda b, i: (b, 0, i, 0))],
        out_shape=[jax.ShapeDtypeStruct((bsz, A_HEADS, SUBLANES, seq), jnp.int32),
                   jax.ShapeDtypeStruct((bsz, A_HEADS, nb, SUBLANES, nb), F32),
                   jax.ShapeDtypeStruct((bsz, A_HEADS, seq, HEAD_DIM), F32)],
        compiler_params=_cparams("parallel", "parallel"),
        name="moba_gate",
    )(qk, kmean)


def _moba_routes(ids, cnt, seq):
    bsz, heads = ids.shape[:2]
    bh, t = bsz * heads, MOBA_BLOCK
    nb = seq // t
    tiles = _moba_tiles(seq)
    picks = ids[:, :, 0:MOBA_TOPK, :].reshape(bh, MOBA_TOPK, nb, t)
    ranks = ids[:, :, MOBA_TOPK:2 * MOBA_TOPK, :].reshape(bh, MOBA_TOPK, nb, t)
    per_tile = cnt[:, :, :, 0:MOBA_TOPK, :].astype(jnp.int32).reshape(bh, nb * MOBA_TOPK, nb)
    before = jnp.cumsum(per_tile, axis=1) - per_tile
    total = jnp.sum(per_tile, axis=1)
    padded = -(-total // t) * t
    ends = jnp.cumsum(padded, axis=1)
    base = before + (ends - padded)[:, None, :]
    base = base.reshape(bh, nb, MOBA_TOPK, nb).transpose(0, 2, 1, 3)
    onehot = picks[..., None] == jnp.arange(nb)
    pos = jnp.sum(jnp.where(onehot, base[:, :, :, None, :], 0), axis=-1) + ranks
    pos = jnp.where(picks < nb, pos, (tiles - 1) * t)
    pos = pos + (jnp.arange(bh, dtype=jnp.int32) * (tiles * t))[:, None, None, None]
    pos = pos.reshape(bh, MOBA_TOPK, seq).transpose(1, 0, 2).reshape(MOBA_TOPK, bh * seq)
    tile_start = jnp.arange(tiles, dtype=jnp.int32) * t
    tile_blk = jnp.sum(tile_start[None, :, None] >= ends[:, None, :], axis=-1)
    tile_blk = jnp.where(tile_start[None, :] < ends[:, -1:], tile_blk, -1)
    return pos.astype(jnp.int32), tile_blk.astype(jnp.int32)


def _moba_tiles(seq):
    nb = seq // MOBA_BLOCK
    return -(-(MOBA_TOPK * nb + nb + 1) // GROUP_STEP) * GROUP_STEP


def _sc_mesh():
    return plsc.VectorSubcoreMesh(core_axis_name="core", subcore_axis_name="subcore")


def _sc_scatter_rows(x, idx, rows):
    slots, n = idx.shape
    d = x.shape[1]

    @pl.kernel(out_type=jax.ShapeDtypeStruct((rows, d), x.dtype), mesh=_sc_mesh())
    def scatter(x_hbm, i_hbm, o_hbm):
        def body(x_vmem, i_vmem):
            pltpu.sync_copy(x_vmem, o_hbm.at[i_vmem.at[0]])

        pltpu.emit_pipeline(
            body, grid=(slots, n // SC_WINDOW),
            in_specs=[pl.BlockSpec((SC_WINDOW, d), lambda s, i: (i, 0)),
                      pl.BlockSpec((1, SC_WINDOW), lambda s, i: (s, i))],
            out_specs=[],
            core_axis_name=("core", "subcore"),
            dimension_semantics=(pltpu.PARALLEL, pltpu.PARALLEL),
        )(x_hbm, i_hbm)

    return scatter(x, idx)


def _sc_gather_rows(x, idx):
    n = idx.shape[0]
    d = x.shape[1]

    @pl.kernel(out_type=jax.ShapeDtypeStruct((n, d), x.dtype), mesh=_sc_mesh())
    def gather(x_hbm, i_hbm, o_hbm):
        def body(i_vmem, o_vmem):
            pltpu.sync_copy(x_hbm.at[i_vmem.at[0]], o_vmem)

        pltpu.emit_pipeline(
            body, grid=(n // SC_WINDOW,),
            in_specs=[pl.BlockSpec((1, SC_WINDOW), lambda i: (0, i))],
            out_specs=[pl.BlockSpec((SC_WINDOW, d), lambda i: (i, 0))],
            core_axis_name=("core", "subcore"),
            dimension_semantics=(pltpu.PARALLEL,),
        )(i_hbm, o_hbm)

    return gather(x, idx.reshape(1, n))


def _moba_group_kernel(tb_ref, q_ref, *refs):
    t = MOBA_BLOCK
    k_refs, v_refs = refs[:GROUP_STEP], refs[GROUP_STEP:2 * GROUP_STEP]
    o_ref, lse_ref = refs[2 * GROUP_STEP:]
    g, step = pl.program_id(0), pl.program_id(1)
    for u in range(GROUP_STEP):
        used = tb_ref[g, step * GROUP_STEP + u] >= 0
        rows = slice(u * t, (u + 1) * t)
        o, lse = _block_attention(q_ref[rows, :].astype(BF16), k_refs[u][...], v_refs[u][...])
        o_ref[rows, :] = jnp.where(used, o, 0.0)
        lse_ref[rows, :] = jnp.where(used, lse, MASKED)


def _moba_group_attention(q_grouped, tile_blk, qk, v):
    bh, rows, _ = q_grouped.shape
    t = MOBA_BLOCK
    tiles = rows // t
    heads = A_HEADS

    def block_of(u, first_col):
        return lambda g, s, tb: (g // heads, jnp.maximum(tb[g, s * GROUP_STEP + u], 0),
                                 first_col + g % heads)

    row_tile = pl.BlockSpec((None, GROUP_STEP * t, HEAD_DIM), lambda g, s, tb: (g, s, 0))
    key_value = lambda first_col: [pl.BlockSpec((None, t, HEAD_DIM), block_of(u, first_col))
                                   for u in range(GROUP_STEP)]
    grid_spec = pltpu.PrefetchScalarGridSpec(
        num_scalar_prefetch=1,
        grid=(bh, tiles // GROUP_STEP),
        in_specs=[row_tile] + key_value(KA_BLK) + key_value(0),
        out_specs=[row_tile, row_tile],
    )
    return pl.pallas_call(
        _moba_group_kernel,
        grid_spec=grid_spec,
        out_shape=[jax.ShapeDtypeStruct(q_grouped.shape, F32)] * 2,
        compiler_params=_cparams("parallel", "parallel"),
        name="moba_group",
    )(tile_blk, q_grouped, *([qk] * GROUP_STEP), *([v] * GROUP_STEP))


def _moba_merge_kernel(q_ref, k_ref, v_ref, po_ref, pl_ref, o_ref):
    t = MOBA_BLOCK
    causal = (lax.broadcasted_iota(jnp.int32, (t, t), 1)
              <= lax.broadcasted_iota(jnp.int32, (t, t), 0))
    for hd in range(A_HEADS):
        sl = slice(hd * HEAD_DIM, (hd + 1) * HEAD_DIM)
        o_own, lse_own = _block_attention(q_ref[:, sl], k_ref[:, sl], v_ref[:, sl], causal)
        outs = [o_own] + [po_ref[s, hd] for s in range(MOBA_TOPK)]
        lses = [lse_own] + [pl_ref[s, hd] for s in range(MOBA_TOPK)]
        top = functools.reduce(jnp.maximum, lses)
        weights = [jnp.exp2(l - top) for l in lses]
        num = sum(w * o for w, o in zip(weights, outs))
        o_ref[:, sl] = (num / sum(weights)).astype(o_ref.dtype)


def _moba_merge(qk, v, part_o, part_lse):
    bsz, seq, _ = qk.shape
    t = MOBA_BLOCK
    part = pl.BlockSpec((MOBA_TOPK, None, A_HEADS, t, HEAD_DIM), lambda b, i: (0, b, 0, i, 0))
    return pl.pallas_call(
        _moba_merge_kernel,
        grid=(bsz, seq // t),
        in_specs=[pl.BlockSpec((None, t, A_W), lambda b, i: (b, i, QA_BLK * LANES // A_W)),
                  pl.BlockSpec((None, t, A_W), lambda b, i: (b, i, KA_BLK * LANES // A_W)),
                  pl.BlockSpec((None, t, A_W), lambda b, i: (b, i, 0)),
                  part, part],
        out_specs=pl.BlockSpec((None, t, A_W), lambda b, i: (b, i, 0)),
        out_shape=jax.ShapeDtypeStruct((bsz, seq, A_W), BF16),
        compiler_params=_cparams("parallel", "parallel"),
        name="moba_merge",
    )(qk, qk, v, part_o, part_lse)


def _moba_attention(qk, v, kmean):
    bsz, seq, _ = qk.shape
    bh = bsz * A_HEADS
    rows = _moba_tiles(seq) * MOBA_BLOCK
    ids, cnt, q_f32 = _moba_gate(qk, kmean)
    pos, tile_blk = _moba_routes(ids, cnt, seq)
    q_grouped = _sc_scatter_rows(q_f32.reshape(bh * seq, HEAD_DIM), pos, bh * rows)
    o_g, lse_g = _moba_group_attention(q_grouped.reshape(bh, rows, HEAD_DIM), tile_blk, qk, v)
    flat = pos.reshape(-1)
    back = lambda a: _sc_gather_rows(a.reshape(bh * rows, HEAD_DIM), flat).reshape(
        MOBA_TOPK, bsz, A_HEADS, seq, HEAD_DIM)
    return _moba_merge(qk, v, back(o_g), back(lse_g))


def _proj_dilated_kernel(h_ref, w_ref, c_ref, s_ref, q_ref, k_ref, v_ref, sc, *, d):
    acc = _dot(h_ref[...], w_ref[...])
    c, s = c_ref[...], s_ref[...]
    q_scale = HEAD_DIM ** -0.5 * LOG2E
    for j in range(acc.shape[1] // LANES):
        blk = acc[:, j * LANES:(j + 1) * LANES]
        if j < B_HEADS:
            blk = _rope128(blk, c, s) * q_scale
        elif j < 2 * B_HEADS:
            blk = _rope128(blk, c, s)
        sc[j] = blk
    rows = acc.shape[0] // d
    for r in range(d):
        for j in range(acc.shape[1] // LANES):
            dst = (q_ref, k_ref, v_ref)[j // B_HEADS]
            col = (j % B_HEADS) * LANES
            dst[r, :, col:col + LANES] = sc[j, pl.ds(r, rows, stride=d), :].astype(dst.dtype)


def _proj_dilated(h, w, rope_h, d, bsz, seq, tm=512):
    m, k = h.shape
    nt = seq // tm
    pos = pl.BlockSpec((tm, LANES), lambda i: (i % nt, 0))
    out = pl.BlockSpec((None, d, tm // d, B_W), lambda i: (i // nt, 0, i % nt, 0))
    return pl.pallas_call(
        functools.partial(_proj_dilated_kernel, d=d),
        grid=(m // tm,),
        in_specs=[pl.BlockSpec((tm, k), lambda i: (i, 0)), pl.BlockSpec(w.shape, lambda i: (0, 0)),
                  pos, pos],
        out_specs=[out] * 3,
        out_shape=[jax.ShapeDtypeStruct((bsz, d, seq // d, B_W), BF16)] * 3,
        scratch_shapes=[pltpu.VMEM((w.shape[1] // LANES, tm, LANES), F32)],
        compiler_params=_cparams("parallel"),
        name=f"proj_dilated_d{d}",
    )(h, w, *rope_h)


def _dilated_kernel(q_ref, kc_ref, kp_ref, vc_ref, vp_ref, o_ref, lse_ref, *, span):
    t, tp = q_ref.shape[0], kp_ref.shape[0]
    i = pl.program_id(2)
    dist = (lax.broadcasted_iota(jnp.int32, (t, t), 1)
            - lax.broadcasted_iota(jnp.int32, (t, t), 0))
    bias_c = jnp.where((dist >= 0) & (dist <= span), 0.0, MASKED)
    dist_p = (lax.broadcasted_iota(jnp.int32, (tp, t), 1) + tp
              - lax.broadcasted_iota(jnp.int32, (tp, t), 0))
    bias_p = jnp.where((dist_p <= span) & (i > 0), 0.0, MASKED)
    ones_c = jnp.ones((BF16_ROWS, t), BF16)
    ones_p = jnp.ones((BF16_ROWS, tp), BF16)

    def transposed(ref, sl):
        return ref[:, sl].astype(F32).T.astype(BF16)

    for j in range(B_HEADS):
        sl = slice(j * LANES, (j + 1) * LANES)
        qT = transposed(q_ref, sl)
        s_c = _dot(kc_ref[:, sl], qT) + bias_c
        s_p = _dot(kp_ref[:, sl], qT) + bias_p
        m = jnp.maximum(jnp.max(s_c, axis=0, keepdims=True), jnp.max(s_p, axis=0, keepdims=True))
        p_c = jnp.exp2(s_c - m).astype(BF16)
        p_p = jnp.exp2(s_p - m).astype(BF16)
        vt_c = jnp.concatenate([transposed(vc_ref, sl), ones_c], axis=0)
        vt_p = jnp.concatenate([transposed(vp_ref, sl), ones_p], axis=0)
        acc = _dot(vt_c, p_c) + _dot(vt_p, p_p)
        den = acc[V_DIM:V_DIM + 1, :]
        o_ref[:, sl] = (acc[:V_DIM, :] / den).T
        lse = m + jnp.log2(den)
        lse_ref[:, sl] = jnp.broadcast_to(lse, (LANES, t)).T


def _dilated_attention(q, k, v, span, t=512):
    bsz, d, length, _ = q.shape
    t = min(t, length)
    tp = B_QBLOCK
    assert span <= tp and t % tp == 0
    cur = pl.BlockSpec((None, None, t, B_W), lambda b, r, i: (b, r, i, 0))
    prev = pl.BlockSpec((None, None, tp, B_W),
                        lambda b, r, i: (b, r, jnp.maximum(i * (t // tp) - 1, 0), 0))
    return pl.pallas_call(
        functools.partial(_dilated_kernel, span=span),
        grid=(bsz, d, length // t),
        in_specs=[cur, cur, prev, cur, prev],
        out_specs=[cur, cur],
        out_shape=[jax.ShapeDtypeStruct(q.shape, F32)] * 2,
        compiler_params=_cparams("parallel", "parallel", "parallel"),
        name=f"dilated_d{d}",
    )(q, k, k, v, v)


def _natural_rows(ref, sc):
    d, rows = ref.shape[0], ref.shape[1]
    if d == 1:
        return ref[0]
    for r in range(d):
        for j in range(B_HEADS):
            sc[j, pl.ds(r, rows, stride=d), :] = ref[r, :, j * LANES:(j + 1) * LANES]
    return jnp.concatenate([sc[j] for j in range(B_HEADS)], axis=-1)


def _mixer_tail_kernel(x_ref, oa_ref, o0_ref, o1_ref, o2_ref, l0_ref, l1_ref, l2_ref, oc_ref,
                       g_ref, wpa_ref, wpb_ref, wpc_ref, wo_ref, y_ref, *scratch):
    o0, o1, o2, l0, l1, l2 = [
        _natural_rows(ref, sc)
        for ref, sc in zip((o0_ref, o1_ref, o2_ref, l0_ref, l1_ref, l2_ref), scratch)]
    mx = jnp.maximum(jnp.maximum(l0, l1), l2)
    e0, e1, e2 = jnp.exp2(l0 - mx), jnp.exp2(l1 - mx), jnp.exp2(l2 - mx)
    ob = (e0 * o0 + e1 * o1 + e2 * o2) / (e0 + e1 + e2)
    pa = _dot(oa_ref[...], wpa_ref[...])
    pb = _dot(ob.astype(BF16), wpb_ref[...])
    pc = _dot(oc_ref[...], wpc_ref[...])
    d = D_MODEL
    merged = (g_ref[:, 0:d].astype(F32) * pa + g_ref[:, d:2 * d].astype(F32) * pb
              + g_ref[:, 2 * d:3 * d].astype(F32) * pc)
    y_ref[...] = x_ref[...] + _dot(merged.astype(BF16), wo_ref[...])


def _mixer_tail(x, out_a, o_groups, lse_groups, out_c, gates, w_pa, w_pb, w_pc, w_o, seq, tm=256):
    m, d = x.shape
    nt = seq // tm
    row = lambda width: pl.BlockSpec((tm, width), lambda i: (i, 0))
    residue = lambda g: pl.BlockSpec((None, g.shape[1], tm // g.shape[1], B_W),
                                     lambda i: (i // nt, 0, i % nt, 0))
    weights = [_resident(w) for w in (w_pa, w_pb, w_pc, w_o)]
    groups = list(o_groups) + list(lse_groups)
    return pl.pallas_call(
        _mixer_tail_kernel,
        grid=(m // tm,),
        in_specs=([row(d), row(A_W)] + [residue(g) for g in groups]
                  + [row(C_W), row(3 * d)] + weights),
        out_specs=row(d),
        out_shape=jax.ShapeDtypeStruct((m, d), F32),
        scratch_shapes=[pltpu.VMEM((B_HEADS, tm, LANES), F32) for _ in groups],
        compiler_params=_cparams("parallel"),
        name="mixer_tail",
    )(x, out_a, *groups, out_c, gates, w_pa, w_pb, w_pc, w_o)


def _mem_kv_kernel(mem_ref, g_ref, wk_ref, wv_ref, k_ref, v_ref):
    memn = _rms(mem_ref[...], g_ref[...]).astype(BF16)
    k_ref[...] = _dot(memn, wk_ref[...]).astype(k_ref.dtype)
    v_ref[...] = _dot(memn, wv_ref[...]).astype(v_ref.dtype)


def _mem_kv(mem, g, wk, wv):
    bsz, n, d = mem.shape
    out = pl.BlockSpec((None, n, X_W), lambda b: (b, 0, 0))
    return pl.pallas_call(
        _mem_kv_kernel,
        grid=(bsz,),
        in_specs=[pl.BlockSpec((None, n, d), lambda b: (b, 0, 0)),
                  pl.BlockSpec((1, d), lambda b: (0, 0)),
                  pl.BlockSpec(wk.shape, lambda b: (0, 0)),
                  pl.BlockSpec(wv.shape, lambda b: (0, 0))],
        out_specs=[out, out],
        out_shape=[jax.ShapeDtypeStruct((bsz, n, X_W), BF16)] * 2,
        compiler_params=_cparams("parallel"),
        name="mem_kv",
    )(mem, g.reshape(1, d), wk, wv)


def _mem_attn_kernel(x_ref, g_ref, wq_ref, k_ref, v_ref, wo_ref, y_ref):
    x = x_ref[...]
    h = _rms(x, g_ref[...]).astype(BF16)
    q = (_dot(h, wq_ref[...]) * HEAD_DIM ** -0.5).astype(BF16)
    heads = []
    for hd in range(X_HEADS):
        sl = slice(hd * HEAD_DIM, (hd + 1) * HEAD_DIM)
        s = _dot_nt(q[:, sl], k_ref[:, sl])
        p = jnp.exp(s - jnp.max(s, axis=-1, keepdims=True))
        o = _dot(p.astype(BF16), v_ref[:, sl]) / jnp.sum(p, axis=-1, keepdims=True)
        heads.append(o.astype(BF16))
    y_ref[...] = x + _dot(jnp.concatenate(heads, axis=-1), wo_ref[...])


def _mem_attention(x, g, wq, kmem, vmem, wo, seq, tm=512):
    m, d = x.shape
    nt = seq // tm
    n = kmem.shape[1]
    kv = pl.BlockSpec((None, n, X_W), lambda i: (i // nt, 0, 0))
    return pl.pallas_call(
        _mem_attn_kernel,
        grid=(m // tm,),
        in_specs=[pl.BlockSpec((tm, d), lambda i: (i, 0)),
                  pl.BlockSpec((1, d), lambda i: (0, 0)),
                  pl.BlockSpec(wq.shape, lambda i: (0, 0)), kv, kv,
                  pl.BlockSpec(wo.shape, lambda i: (0, 0))],
        out_specs=pl.BlockSpec((tm, d), lambda i: (i, 0)),
        out_shape=jax.ShapeDtypeStruct((m, d), F32),
        compiler_params=_cparams("parallel"),
        name="mem_attention",
    )(x, g.reshape(1, d), wq, kmem, vmem, wo)


def _ffn_up_kernel(x_ref, halo_ref, g_ref, wg_ref, wv_ref, cwg_ref, cwv_ref, cbg_ref, cbv_ref,
                   act_ref, h_sc, *, tiles_per_seq):
    i = pl.program_id(0)
    tm = x_ref.shape[0]

    @pl.when(pl.program_id(1) == 0)
    def _():
        g = g_ref[...]
        keep = (i % tiles_per_seq != 0).astype(F32)
        h_sc[0:HALO, :] = (_rms(halo_ref[...], g) * keep).astype(h_sc.dtype)
        h_sc[HALO:, :] = _rms(x_ref[...], g).astype(h_sc.dtype)

    h = h_sc[...]

    def conv(w_ref, cw_ref, cb_ref):
        u = _dot(h, w_ref[...])
        c = cb_ref[...]
        for tap in range(CONV_W):
            lo = HALO - (CONV_W - 1) + tap
            c = c + cw_ref[tap:tap + 1, :] * u[lo:lo + tm, :]
        return c

    act = jax.nn.silu(conv(wg_ref, cwg_ref, cbg_ref)) * conv(wv_ref, cwv_ref, cbv_ref)
    act_ref[...] = act.astype(act_ref.dtype)


def _ffn_down_kernel(a_ref, w_ref, x_ref, y_ref):
    y_ref[...] = x_ref[...] + _dot(a_ref[...], w_ref[...])


def _conv_ffn(x, g, w_up, conv_w, conv_b, w_down, seq, tm=1024):
    m, d = x.shape
    tf = FFN_TF
    nf = D_FF_PAD // tf
    halo_blocks = tm // HALO
    act = pl.pallas_call(
        functools.partial(_ffn_up_kernel, tiles_per_seq=seq // tm),
        grid=(m // tm, nf),
        in_specs=[pl.BlockSpec((tm, d), lambda i, f: (i, 0)),
                  pl.BlockSpec((HALO, d), lambda i, f: (jnp.maximum(i * halo_blocks - 1, 0), 0)),
                  pl.BlockSpec((1, d), lambda i, f: (0, 0)),
                  pl.BlockSpec((d, tf), lambda i, f: (0, f)),
                  pl.BlockSpec((d, tf), lambda i, f: (0, f + nf)),
                  pl.BlockSpec((CONV_W, tf), lambda i, f: (0, f)),
                  pl.BlockSpec((CONV_W, tf), lambda i, f: (0, f + nf)),
                  pl.BlockSpec((1, tf), lambda i, f: (0, f)),
                  pl.BlockSpec((1, tf), lambda i, f: (0, f + nf))],
        out_specs=pl.BlockSpec((tm, tf), lambda i, f: (i, f)),
        out_shape=jax.ShapeDtypeStruct((m, D_FF_PAD), BF16),
        scratch_shapes=[pltpu.VMEM((HALO + tm, d), BF16)],
        compiler_params=_cparams("parallel", "arbitrary"),
        name="ffn_up",
    )(x, x, g.reshape(1, d), w_up, w_up, conv_w, conv_w, conv_b, conv_b)
    tn = FFN_TF
    return pl.pallas_call(
        _ffn_down_kernel,
        grid=(m // tm, d // tn),
        in_specs=[pl.BlockSpec((tm, D_FF_PAD), lambda i, j: (i, 0)),
                  pl.BlockSpec((D_FF_PAD, tn), lambda i, j: (0, j)),
                  pl.BlockSpec((tm, tn), lambda i, j: (i, j))],
        out_specs=pl.BlockSpec((tm, tn), lambda i, j: (i, j)),
        out_shape=jax.ShapeDtypeStruct((m, d), F32),
        compiler_params=_cparams("parallel", "parallel"),
        name="ffn_down",
    )(act, w_down, x)


def _rope_tables(seq):
    def angles(dim):
        inv_freq = jnp.exp(jnp.arange(0, dim, 2, dtype=F32) * (-math.log(ROPE_THETA) / dim))
        ang = jnp.arange(seq, dtype=F32)[:, None] * inv_freq[None, :]
        return jnp.cos(ang), jnp.sin(ang)

    cos_h, sin_h = angles(HEAD_DIM)
    rope_h = (jnp.concatenate([cos_h, cos_h], axis=-1), jnp.concatenate([-sin_h, sin_h], axis=-1))
    cos_r, sin_r = angles(ROPE_DIM)
    z = jnp.zeros_like(cos_r)
    rope_r = (jnp.concatenate([cos_r, cos_r, z, z], axis=-1),
              jnp.concatenate([-sin_r, z, z, z], axis=-1),
              jnp.concatenate([z, sin_r, z, z], axis=-1))
    return rope_h, rope_r


def _split_in(w_in):
    return [w_in[:, IN_OFFSETS[k]:IN_OFFSETS[k + 1]] for k in range(len(IN_WIDTHS))]


def _pad_cols(w, width):
    return jnp.pad(w, ((0, 0), (0, width - w.shape[1])))


def _layer_params(w_in, w_uq, w_ukv, w_up, conv_w, conv_b, w_down):
    qa, ka, va, qb, kb, vb, cq, ckv, kr, gates = _split_in(w_in)
    w_qk = jnp.concatenate([qa, ka], axis=1).astype(BF16)
    group_cols = lambda w, g: w[:, g * B_W:(g + 1) * B_W]
    w_b = [jnp.concatenate([group_cols(qb, g), group_cols(kb, g), group_cols(vb, g)],
                           axis=1).astype(BF16) for g in range(len(B_GROUPS))]
    w_down_in = jnp.concatenate([cq, ckv, _pad_cols(kr, LANES)], axis=1).astype(BF16)
    uq = w_uq.reshape(Q_LORA, C_HEADS, NOPE_DIM + ROPE_DIM)
    uq = jnp.pad(uq, ((0, 0), (0, 0), (0, C_QK - NOPE_DIM - ROPE_DIM)))
    ukv = w_ukv.reshape(KV_LORA, C_HEADS, NOPE_DIM + V_DIM)
    pad_ff = lambda w: jnp.pad(w, ((0, 0), (0, D_FF_PAD - D_FF)))
    two_halves = lambda w: jnp.concatenate([pad_ff(w[:, :D_FF]), pad_ff(w[:, D_FF:])], axis=1)
    return dict(
        w_qk=w_qk, w_va=va.astype(BF16), w_b=w_b, w_gates=gates.astype(BF16),
        w_down_in=w_down_in,
        w_uq=uq.reshape(Q_LORA, C_HEADS * C_QK).astype(BF16),
        w_uk=ukv[:, :, :NOPE_DIM].reshape(KV_LORA, C_HEADS * NOPE_DIM).astype(BF16),
        w_uv=ukv[:, :, NOPE_DIM:].reshape(KV_LORA, C_W).astype(BF16),
        w_up=two_halves(w_up).astype(BF16),
        conv_w=two_halves(conv_w),
        conv_b=two_halves(conv_b.reshape(1, -1)),
        w_down=jnp.pad(w_down, ((0, D_FF_PAD - D_FF), (0, 0))).astype(BF16),
    )


def _qk_col_scale():
    q_scale = HEAD_DIM ** -0.5
    parts = [jnp.full((A_W,), q_scale * LOG2E, F32), jnp.ones((A_W,), F32)]
    return jnp.concatenate(parts).reshape(1, QK_W)


def _mixer(x, g_mix, p, g_cq, g_ckv, w_pa, w_pb, w_pc, w_o, rope_h, rope_r, bsz, seq):
    m = x.shape[0]
    h = _rmsnorm(x, g_mix, BF16)
    qk = _matmul(h, p["w_qk"], _mm_rope_kernel, BF16, 1024, 1024, seq=seq,
                 extras=(("col", _qk_col_scale()), ("pos", rope_h[0]), ("pos", rope_h[1])),
                 name="proj_qk_rope")
    v_a = _matmul(h, p["w_va"], _mm_plain_kernel, BF16, 1024, A_W, name="proj_va")
    gates = _matmul(h, p["w_gates"], _mm_sigmoid_kernel, BF16, 1024, 1024, name="proj_gates")
    cq, ckv, kr = _mla_down(h, p["w_down_in"], g_cq, g_ckv, rope_r, seq)
    q_c = _mla_q(cq, p["w_uq"], rope_r, seq)
    k_c, vt_c = _mla_kv(ckv, kr, p["w_uk"], p["w_uv"], bsz, seq)

    qk3 = qk.reshape(bsz, seq, QK_W)
    out_a = _moba_attention(qk3, v_a.reshape(bsz, seq, A_W), _kmean(qk3)).reshape(m, A_W)
    groups = []
    for (window, d), w_g in zip(B_GROUPS, p["w_b"]):
        q_g, k_g, v_g = _proj_dilated(h, w_g, rope_h, d, bsz, seq)
        groups.append(_dilated_attention(q_g, k_g, v_g, window // d))
    out_c = _flash_attention(q_c.reshape(bsz, seq, -1), k_c.reshape(bsz, seq, -1), vt_c,
                             C_HEADS, C_QK).reshape(m, C_W)
    return _mixer_tail(x, out_a, [g[0] for g in groups], [g[1] for g in groups], out_c, gates,
                       w_pa.astype(BF16), w_pb.astype(BF16), w_pc.astype(BF16), w_o.astype(BF16),
                       seq)


def kernel(x, mem, g_mix, w_in, g_cq, g_ckv, w_uq, w_ukv, w_pa, w_pb, w_pc, w_o, g_mem, g_memkv,
           w_xq, w_xk, w_xv, w_xo, g_ffn, w_up, conv_w, conv_b, w_down, g_final):
    bsz, seq, d = x.shape
    rope_h, rope_r = _rope_tables(seq)
    xf = x.reshape(bsz * seq, d)
    for l in range(DEPTH):
        p = _layer_params(w_in[l], w_uq[l], w_ukv[l], w_up[l], conv_w[l], conv_b[l], w_down[l])
        xf = _mixer(xf, g_mix[l], p, g_cq[l], g_ckv[l], w_pa[l], w_pb[l], w_pc[l], w_o[l],
                    rope_h, rope_r, bsz, seq)
        kmem, vmem = _mem_kv(mem, g_memkv[l], w_xk[l].astype(BF16), w_xv[l].astype(BF16))
        xf = _mem_attention(xf, g_mem[l], w_xq[l].astype(BF16), kmem, vmem,
                            w_xo[l].astype(BF16), seq)
        xf = _conv_ffn(xf, g_ffn[l], p["w_up"], p["conv_w"], p["conv_b"], p["w_down"], seq)
    return _rmsnorm(xf, g_final, F32).reshape(bsz, seq, d)
```

```python
import functools
import math

import jax
import jax.numpy as jnp
import numpy as np
from jax import lax
from jax.experimental import pallas as pl
from jax.experimental.pallas import tpu as pltpu
from jax.experimental.pallas import tpu_sc as plsc

F32 = jnp.float32
BF16 = jnp.bfloat16

LANES = 128
SUBLANES = 8
VMEM_LIMIT = 56 * 1024 * 1024

D_MODEL = 2048
DEPTH = 2
HEAD_DIM = 128
ROPE_THETA = 10000.0
EPS = 1e-6

A_HEADS = 4
MOBA_BLOCK = 256
MOBA_TOPK = 3

B_GROUPS = ((128, 1), (512, 4), (2048, 16))
B_HEADS = 4
B_QBLOCK = 128

C_HEADS = 8
Q_LORA = 1536
KV_LORA = 512
NOPE_DIM = 128
ROPE_DIM = 64
V_DIM = 128

X_HEADS = 4
D_FF = 5504
CONV_W = 3

A_W = A_HEADS * HEAD_DIM
B_QKV_W = len(B_GROUPS) * B_HEADS * HEAD_DIM
B_W = B_HEADS * HEAD_DIM
C_W = C_HEADS * V_DIM
X_W = X_HEADS * HEAD_DIM
IN_WIDTHS = (A_W, A_W, A_W, B_QKV_W, B_QKV_W, B_QKV_W, Q_LORA, KV_LORA, ROPE_DIM, 3 * D_MODEL)
IN_OFFSETS = tuple(int(o) for o in np.cumsum((0,) + IN_WIDTHS))

QK_W = 2 * A_W
QA_BLK, KA_BLK = 0, A_W // LANES

C_QK = 2 * LANES
MASKED = -1e30
LOG2E = math.log2(math.e)
BF16_ROWS = 16
VT_ROWS = V_DIM + BF16_ROWS
GROUP_STEP = 8
SC_WINDOW = 128

D_FF_PAD = 5632
FFN_TF = 512
HALO = SUBLANES


def _cparams(*sem):
    return pltpu.CompilerParams(dimension_semantics=sem, vmem_limit_bytes=VMEM_LIMIT)


def _resident(arr):
    zeros = (0,) * arr.ndim
    return pl.BlockSpec(arr.shape, lambda *_: zeros, pipeline_mode=pl.Buffered(1))


def _dot(a, b):
    return jnp.dot(a, b, preferred_element_type=F32)


def _dot_nt(a, b):
    return lax.dot_general(a, b, (((1,), (1,)), ((), ())), preferred_element_type=F32)


def _rms(x, g):
    return x * lax.rsqrt(jnp.mean(x * x, axis=-1, keepdims=True) + EPS) * g


def _rmsnorm_kernel(x_ref, g_ref, o_ref):
    o_ref[...] = _rms(x_ref[...], g_ref[...]).astype(o_ref.dtype)


def _rmsnorm(x, g, out_dtype, tm=512):
    m, d = x.shape
    return pl.pallas_call(
        _rmsnorm_kernel,
        grid=(m // tm,),
        in_specs=[pl.BlockSpec((tm, d), lambda i: (i, 0)),
                  pl.BlockSpec((1, d), lambda i: (0, 0))],
        out_specs=pl.BlockSpec((tm, d), lambda i: (i, 0)),
        out_shape=jax.ShapeDtypeStruct((m, d), out_dtype),
        compiler_params=_cparams("parallel"),
        name="rmsnorm",
    )(x, g.reshape(1, d))


def _rope128(x, c, s):
    return x * c + pltpu.roll(x, HEAD_DIM // 2, 1) * s


def _rope64(x, c, sa, sb):
    half = ROPE_DIM // 2
    return x * c + pltpu.roll(x, LANES - half, 1) * sa + pltpu.roll(x, half, 1) * sb


def _mm_plain_kernel(a_ref, w_ref, o_ref):
    o_ref[...] = _dot(a_ref[...], w_ref[...]).astype(o_ref.dtype)


def _mm_sigmoid_kernel(a_ref, w_ref, o_ref):
    o_ref[...] = jax.nn.sigmoid(_dot(a_ref[...], w_ref[...])).astype(o_ref.dtype)


def _mm_rope_kernel(a_ref, w_ref, cs_ref, c_ref, s_ref, o_ref):
    acc = _dot(a_ref[...], w_ref[...])
    c = c_ref[...]
    s = s_ref[...]
    for j in range(acc.shape[1] // LANES):
        sl = slice(j * LANES, (j + 1) * LANES)
        o_ref[:, sl] = (_rope128(acc[:, sl], c, s) * cs_ref[:, sl]).astype(o_ref.dtype)


def _matmul(a, w, kernel, out_dtype, tm, tn, seq=None, extras=(), name="matmul"):
    m, k = a.shape
    n = w.shape[1]
    in_specs = [pl.BlockSpec((tm, k), lambda i, j: (i, 0)),
                pl.BlockSpec((k, tn), lambda i, j: (0, j))]
    args = [a, w]
    for kind, arr in extras:
        if kind == "col":
            in_specs.append(pl.BlockSpec((1, tn), lambda i, j: (0, j)))
        else:
            nt = seq // tm
            in_specs.append(pl.BlockSpec((tm, LANES), lambda i, j: (i % nt, 0)))
        args.append(arr)
    return pl.pallas_call(
        kernel,
        grid=(m // tm, n // tn),
        in_specs=in_specs,
        out_specs=pl.BlockSpec((tm, tn), lambda i, j: (i, j)),
        out_shape=jax.ShapeDtypeStruct((m, n), out_dtype),
        compiler_params=_cparams("parallel", "parallel"),
        name=name,
    )(*args)


def _mla_down_kernel(h_ref, w_ref, gq_ref, gkv_ref, c_ref, sa_ref, sb_ref,
                     cq_ref, ckv_ref, kr_ref):
    acc = _dot(h_ref[...], w_ref[...])
    cq_ref[...] = _rms(acc[:, :Q_LORA], gq_ref[...]).astype(cq_ref.dtype)
    ckv_ref[...] = _rms(acc[:, Q_LORA:Q_LORA + KV_LORA], gkv_ref[...]).astype(ckv_ref.dtype)
    kr = acc[:, Q_LORA + KV_LORA:]
    kr_ref[...] = _rope64(kr, c_ref[...], sa_ref[...], sb_ref[...]).astype(kr_ref.dtype)


def _mla_down(h, w, g_cq, g_ckv, rope_r, seq, tm=512):
    m, k = h.shape
    n = w.shape[1]
    nt = seq // tm
    row = lambda width: pl.BlockSpec((tm, width), lambda i: (i, 0))
    full = lambda r, c: pl.BlockSpec((r, c), lambda i: (0, 0))
    pos = pl.BlockSpec((tm, LANES), lambda i: (i % nt, 0))
    return pl.pallas_call(
        _mla_down_kernel,
        grid=(m // tm,),
        in_specs=[row(k), full(k, n), full(1, Q_LORA), full(1, KV_LORA), pos, pos, pos],
        out_specs=[row(Q_LORA), row(KV_LORA), row(LANES)],
        out_shape=[jax.ShapeDtypeStruct((m, Q_LORA), BF16),
                   jax.ShapeDtypeStruct((m, KV_LORA), BF16),
                   jax.ShapeDtypeStruct((m, LANES), BF16)],
        compiler_params=_cparams("parallel"),
        name="mla_down",
    )(h, w, g_cq.reshape(1, -1), g_ckv.reshape(1, -1), *rope_r)


def _mla_q_kernel(cq_ref, w_ref, c_ref, sa_ref, sb_ref, q_ref, *, scale):
    acc = _dot(cq_ref[...], w_ref[...])
    c, sa, sb = c_ref[...], sa_ref[...], sb_ref[...]
    for hd in range(C_HEADS):
        lo = hd * C_QK
        q_ref[:, lo:lo + LANES] = (acc[:, lo:lo + LANES] * scale).astype(q_ref.dtype)
        rope = _rope64(acc[:, lo + LANES:lo + C_QK], c, sa, sb)
        q_ref[:, lo + LANES:lo + C_QK] = (rope * scale).astype(q_ref.dtype)


def _mla_q(cq, w, rope_r, seq, tm=512):
    m, k = cq.shape
    n = w.shape[1]
    nt = seq // tm
    pos = pl.BlockSpec((tm, LANES), lambda i: (i % nt, 0))
    return pl.pallas_call(
        functools.partial(_mla_q_kernel, scale=(NOPE_DIM + ROPE_DIM) ** -0.5 * LOG2E),
        grid=(m // tm,),
        in_specs=[pl.BlockSpec((tm, k), lambda i: (i, 0)),
                  pl.BlockSpec((k, n), lambda i: (0, 0)), pos, pos, pos],
        out_specs=pl.BlockSpec((tm, n), lambda i: (i, 0)),
        out_shape=jax.ShapeDtypeStruct((m, n), BF16),
        compiler_params=_cparams("parallel"),
        name="mla_q",
    )(cq, w, *rope_r)


def _store_vt(v, vt_ref):
    vt = v.T
    for hd in range(vt_ref.shape[0]):
        vt_ref[hd, 0:V_DIM, :] = vt[hd * V_DIM:(hd + 1) * V_DIM, :].astype(vt_ref.dtype)
        vt_ref[hd, V_DIM:VT_ROWS, :] = jnp.ones((VT_ROWS - V_DIM, vt.shape[1]), vt_ref.dtype)


def _mla_kv_kernel(ckv_ref, kr_ref, wk_ref, wv_ref, k_ref, vt_ref):
    ckv = ckv_ref[...]
    kn = _dot(ckv, wk_ref[...])
    kr = kr_ref[...]
    for hd in range(C_HEADS):
        lo = hd * C_QK
        k_ref[:, lo:lo + LANES] = kn[:, hd * LANES:(hd + 1) * LANES].astype(k_ref.dtype)
        k_ref[:, lo + LANES:lo + C_QK] = kr
    _store_vt(_dot(ckv, wv_ref[...]), vt_ref)


def _mla_kv(ckv, kr, wk, wv, bsz, seq, tm=512):
    m, k = ckv.shape
    nt = seq // tm
    return pl.pallas_call(
        _mla_kv_kernel,
        grid=(m // tm,),
        in_specs=[pl.BlockSpec((tm, k), lambda i: (i, 0)),
                  pl.BlockSpec((tm, LANES), lambda i: (i, 0)),
                  pl.BlockSpec(wk.shape, lambda i: (0, 0)),
                  pl.BlockSpec(wv.shape, lambda i: (0, 0))],
        out_specs=[pl.BlockSpec((tm, C_HEADS * C_QK), lambda i: (i, 0)),
                   pl.BlockSpec((None, C_HEADS, VT_ROWS, tm), lambda i: (i // nt, 0, 0, i % nt))],
        out_shape=[jax.ShapeDtypeStruct((m, C_HEADS * C_QK), BF16),
                   jax.ShapeDtypeStruct((bsz, C_HEADS, VT_ROWS, seq), BF16)],
        compiler_params=_cparams("parallel"),
        name="mla_kv",
    )(ckv, kr, wk, wv)


def _attend_chunks(qT, k_ref, vt_ref, scratch, *, tk, n_full, mask_main, mask_tail,
                   unroll=4, tail_steps=2):
    m_sc, acc_sc, sa_sc, sb_sc, pa_sc, pb_sc, ala_sc, alb_sc = scratch
    s_bufs = (sa_sc, sb_sc)
    p_bufs = (pa_sc, pb_sc)
    al_bufs = (ala_sc, alb_sc)
    last_chunk = k_ref.shape[0] // tk - 1
    m_sc[...] = jnp.full(m_sc.shape, MASKED, F32)
    acc_sc[...] = jnp.zeros(acc_sc.shape, F32)
    for p_ref, al_ref in zip(p_bufs, al_bufs):
        p_ref[...] = jnp.zeros(p_ref.shape, p_ref.dtype)
        al_ref[...] = jnp.ones(al_ref.shape, F32)

    def rows(c):
        return pl.ds(pl.multiple_of(jnp.clip(c, 0, last_chunk) * tk, tk), tk)

    def scores(c):
        return _dot(k_ref[rows(c), :], qT)

    def flush(c, slot):
        acc_sc[...] = (al_bufs[slot][...] * acc_sc[...]
                       + _dot(vt_ref[:, rows(c)], p_bufs[slot][...]))

    def softmax(sT, slot):
        m_old = m_sc[...]
        m_new = jnp.maximum(m_old, jnp.max(sT, axis=0, keepdims=True))
        al_bufs[slot][...] = jnp.exp2(m_old - m_new)
        p_bufs[slot][...] = jnp.exp2(sT - m_new).astype(p_bufs[slot].dtype)
        m_sc[...] = m_new

    def step(tau, slot, mask, prefetch):
        flush(tau - 2, slot)
        if prefetch:
            s_bufs[1 - slot][...] = scores(tau + 1)
        sT = s_bufs[slot][...]
        softmax(sT if mask is None else mask(sT, tau), slot)

    def steps(tau0, count, mask, prefetch_last=True):
        for j in range(count):
            step(tau0 + j, j % 2, mask, prefetch_last or j < count - 1)

    sa_sc[...] = scores(0)
    trips = n_full // unroll
    lax.fori_loop(0, trips, lambda u, c: (steps(unroll * u, unroll, mask_main), c)[1], 0)
    done = unroll * trips
    pairs = (n_full - done) // 2
    lax.fori_loop(0, pairs, lambda u, c: (steps(done + 2 * u, 2, mask_main), c)[1], 0)
    tau = done + 2 * pairs
    steps(tau, tail_steps, mask_tail, prefetch_last=False)
    flush(tau + tail_steps - 2, 0)
    flush(tau + tail_steps - 1, 1)
    acc = acc_sc[...]
    return acc[:V_DIM, :] / acc[V_DIM:V_DIM + 1, :]


def _attend_scratch(tq, tk):
    return [pltpu.VMEM((1, tq), F32), pltpu.VMEM((VT_ROWS, tq), F32),
            pltpu.VMEM((tk, tq), F32), pltpu.VMEM((tk, tq), F32),
            pltpu.VMEM((tk, tq), BF16), pltpu.VMEM((tk, tq), BF16),
            pltpu.VMEM((1, tq), F32), pltpu.VMEM((1, tq), F32)]


def _transpose_q(q_ref):
    return q_ref[...].astype(F32).T.astype(BF16)


def _flash_kernel(q_ref, k_ref, vt_ref, o_ref, *scratch, tq, tk):
    i = pl.program_id(2)

    def causal(sT, c):
        key = lax.broadcasted_iota(jnp.int32, sT.shape, 0) + c * tk
        qry = lax.broadcasted_iota(jnp.int32, sT.shape, 1) + i * tq
        return jnp.where(key <= qry, sT, MASKED)

    per_tile = tq // tk
    oT = _attend_chunks(_transpose_q(q_ref), k_ref, vt_ref, scratch, tk=tk, n_full=i * per_tile,
                        mask_main=None, mask_tail=causal, unroll=8, tail_steps=per_tile)
    o_ref[...] = oT.T.astype(o_ref.dtype)


def _flash_attention(q, k, vt, heads, qk_w, tq=1024, tk=256):
    bsz, seq, _ = q.shape
    assert tq % (2 * tk) == 0 and seq % tq == 0
    return pl.pallas_call(
        functools.partial(_flash_kernel, tq=tq, tk=tk),
        grid=(bsz, heads, seq // tq),
        in_specs=[pl.BlockSpec((None, tq, qk_w), lambda b, h, i: (b, i, h)),
                  pl.BlockSpec((None, seq, qk_w), lambda b, h, i: (b, 0, h)),
                  pl.BlockSpec((None, None, VT_ROWS, seq), lambda b, h, i: (b, h, 0, 0))],
        out_specs=pl.BlockSpec((None, tq, V_DIM), lambda b, h, i: (b, i, h)),
        out_shape=jax.ShapeDtypeStruct((bsz, seq, heads * V_DIM), BF16),
        scratch_shapes=_attend_scratch(tq, tk),
        compiler_params=_cparams("parallel", "parallel", "arbitrary"),
        name="mla_flash",
    )(q, k, vt)


def _kmean_kernel(k_ref, o_ref):
    k = k_ref[...].astype(F32)
    o_ref[...] = jnp.mean(k.reshape(SUBLANES, MOBA_BLOCK, k.shape[-1]), axis=1)


def _kmean(qk):
    bsz, seq, _ = qk.shape
    rows = SUBLANES * MOBA_BLOCK
    return pl.pallas_call(
        _kmean_kernel,
        grid=(bsz, seq // rows),
        in_specs=[pl.BlockSpec((None, rows, A_W), lambda b, i: (b, i, KA_BLK * LANES // A_W))],
        out_specs=pl.BlockSpec((None, SUBLANES, A_W), lambda b, i: (b, i, 0)),
        out_shape=jax.ShapeDtypeStruct((bsz, seq // MOBA_BLOCK, A_W), F32),
        compiler_params=_cparams("parallel", "parallel"),
        name="moba_kmean",
    )(qk)


def _block_attention(q, k, v, visible=None):
    s = _dot_nt(q, k)
    if visible is not None:
        s = jnp.where(visible, s, MASKED)
    m = jnp.max(s, axis=-1, keepdims=True)
    p = jnp.exp2(s - m).astype(BF16)
    v_ones = jnp.concatenate([v, jnp.ones((v.shape[0], LANES), v.dtype)], axis=-1)
    acc = _dot(p, v_ones)
    den = acc[:, V_DIM:]
    return acc[:, :V_DIM] / den, m + jnp.log2(den)


def _moba_gate_kernel(q_ref, km_ref, ids_ref, cnt_ref, qf_ref):
    t = MOBA_BLOCK
    i = pl.program_id(1)
    nb = km_ref.shape[0]
    blk = lax.broadcasted_iota(jnp.int32, (nb, t), 0)
    neg_inf = jnp.float32(-jnp.inf)
    not_after = (lax.broadcasted_iota(jnp.int32, (t, t), 0)
                 <= lax.broadcasted_iota(jnp.int32, (t, t), 1))
    upper = jnp.where(not_after, 1.0, 0.0).astype(BF16)
    ones = jnp.ones((SUBLANES, t), BF16)
    for hd in range(A_HEADS):
        sl = slice(hd * HEAD_DIM, (hd + 1) * HEAD_DIM)
        q = q_ref[:, sl].astype(F32)
        qf_ref[hd] = q
        qT = q.T.astype(BF16)
        km = km_ref[:, sl]
        km_hi = km.astype(BF16)
        km_lo = (km - km_hi.astype(F32)).astype(BF16)
        g = jnp.where(blk < i, _dot(km_hi, qT) + _dot(km_lo, qT), neg_inf)
        picks, ranks, counts = [], [], []
        for _ in range(MOBA_TOPK):
            mx = jnp.max(g, axis=0, keepdims=True)
            is_max = (g == mx) & (mx > neg_inf)
            first = jnp.min(jnp.where(is_max, blk, nb), axis=0, keepdims=True)
            pick = blk == first
            g = jnp.where(pick, neg_inf, g)
            onehot = jnp.where(pick, 1.0, 0.0).astype(BF16)
            before = _dot(onehot, upper)
            rank = jnp.sum(jnp.where(pick, before - 1.0, 0.0), axis=0, keepdims=True)
            picks.append(first)
            ranks.append(rank.astype(jnp.int32))
            counts.append(_dot_nt(ones, onehot)[0:1, :])
        pad_i = jnp.zeros((SUBLANES - 2 * MOBA_TOPK, t), jnp.int32)
        ids_ref[hd] = jnp.concatenate(picks + ranks + [pad_i], axis=0)
        pad_f = jnp.zeros((SUBLANES - MOBA_TOPK, nb), F32)
        cnt_ref[hd] = jnp.concatenate(counts + [pad_f], axis=0)


def _moba_gate(qk, kmean):
    bsz, seq, _ = qk.shape
    t = MOBA_BLOCK
    nb = seq // t
    return pl.pallas_call(
        _moba_gate_kernel,
        grid=(bsz, nb),
        in_specs=[pl.BlockSpec((None, t, A_W), lambda b, i: (b, i, QA_BLK * LANES // A_W)),
                  pl.BlockSpec((None, nb, A_W), lambda b, i: (b, 0, 0))],
        out_specs=[pl.BlockSpec((None, A_HEADS, SUBLANES, t), lambda b, i: (b, 0, 0, i)),
                   pl.BlockSpec((None, A_HEADS, None, SUBLANES, nb), lambda b, i: (b, 0, i, 0, 0)),
                   pl.BlockSpec((None, A_HEADS, t, HEAD_DIM), lambda b, i: (b, 0, i, 0))],
        out_shape=[jax.ShapeDtypeStruct((bsz, A_HEADS, SUBLANES, seq), jnp.int32),
                   jax.ShapeDtypeStruct((bsz, A_HEADS, nb, SUBLANES, nb), F32),
                   jax.ShapeDtypeStruct((bsz, A_HEADS, seq, HEAD_DIM), F32)],
        compiler_params=_cparams("parallel", "parallel"),
        name="moba_gate",
    )(qk, kmean)


def _moba_routes(ids, cnt, seq):
    bsz, heads = ids.shape[:2]
    bh, t = bsz * heads, MOBA_BLOCK
    nb = seq // t
    tiles = _moba_tiles(seq)
    picks = ids[:, :, 0:MOBA_TOPK, :].reshape(bh, MOBA_TOPK, nb, t)
    ranks = ids[:, :, MOBA_TOPK:2 * MOBA_TOPK, :].reshape(bh, MOBA_TOPK, nb, t)
    per_tile = cnt[:, :, :, 0:MOBA_TOPK, :].astype(jnp.int32).reshape(bh, nb * MOBA_TOPK, nb)
    before = jnp.cumsum(per_tile, axis=1) - per_tile
    total = jnp.sum(per_tile, axis=1)
    padded = -(-total // t) * t
    ends = jnp.cumsum(padded, axis=1)
    base = before + (ends - padded)[:, None, :]
    base = base.reshape(bh, nb, MOBA_TOPK, nb).transpose(0, 2, 1, 3)
    onehot = picks[..., None] == jnp.arange(nb)
    pos = jnp.sum(jnp.where(onehot, base[:, :, :, None, :], 0), axis=-1) + ranks
    pos = jnp.where(picks < nb, pos, (tiles - 1) * t)
    pos = pos + (jnp.arange(bh, dtype=jnp.int32) * (tiles * t))[:, None, None, None]
    pos = pos.reshape(bh, MOBA_TOPK, seq).transpose(1, 0, 2).reshape(MOBA_TOPK, bh * seq)
    tile_start = jnp.arange(tiles, dtype=jnp.int32) * t
    tile_blk = jnp.sum(tile_start[None, :, None] >= ends[:, None, :], axis=-1)
    tile_blk = jnp.where(tile_start[None, :] < ends[:, -1:], tile_blk, -1)
    return pos.astype(jnp.int32), tile_blk.astype(jnp.int32)


def _moba_tiles(seq):
    nb = seq // MOBA_BLOCK
    return -(-(MOBA_TOPK * nb + nb + 1) // GROUP_STEP) * GROUP_STEP


def _sc_mesh():
    return plsc.VectorSubcoreMesh(core_axis_name="core", subcore_axis_name="subcore")


def _sc_scatter_rows(x, idx, rows):
    slots, n = idx.shape
    d = x.shape[1]

    @pl.kernel(out_type=jax.ShapeDtypeStruct((rows, d), x.dtype), mesh=_sc_mesh())
    def scatter(x_hbm, i_hbm, o_hbm):
        def body(x_vmem, i_vmem):
            pltpu.sync_copy(x_vmem, o_hbm.at[i_vmem.at[0]])

        pltpu.emit_pipeline(
            body, grid=(slots, n // SC_WINDOW),
            in_specs=[pl.BlockSpec((SC_WINDOW, d), lambda s, i: (i, 0)),
                      pl.BlockSpec((1, SC_WINDOW), lambda s, i: (s, i))],
            out_specs=[],
            core_axis_name=("core", "subcore"),
            dimension_semantics=(pltpu.PARALLEL, pltpu.PARALLEL),
        )(x_hbm, i_hbm)

    return scatter(x, idx)


def _sc_gather_rows(x, idx):
    n = idx.shape[0]
    d = x.shape[1]

    @pl.kernel(out_type=jax.ShapeDtypeStruct((n, d), x.dtype), mesh=_sc_mesh())
    def gather(x_hbm, i_hbm, o_hbm):
        def body(i_vmem, o_vmem):
            pltpu.sync_copy(x_hbm.at[i_vmem.at[0]], o_vmem)

        pltpu.emit_pipeline(
            body, grid=(n // SC_WINDOW,),
            in_specs=[pl.BlockSpec((1, SC_WINDOW), lambda i: (0, i))],
            out_specs=[pl.BlockSpec((SC_WINDOW, d), lambda i: (i, 0))],
            core_axis_name=("core", "subcore"),
            dimension_semantics=(pltpu.PARALLEL,),
        )(i_hbm, o_hbm)

    return gather(x, idx.reshape(1, n))


def _moba_group_kernel(tb_ref, q_ref, *refs):
    t = MOBA_BLOCK
    k_refs, v_refs = refs[:GROUP_STEP], refs[GROUP_STEP:2 * GROUP_STEP]
    o_ref, lse_ref = refs[2 * GROUP_STEP:]
    g, step = pl.program_id(0), pl.program_id(1)
    first = step * GROUP_STEP

    @pl.when(tb_ref[g, first] < 0)
    def _():
        o_ref[...] = jnp.zeros(o_ref.shape, o_ref.dtype)
        lse_ref[...] = jnp.full(lse_ref.shape, MASKED, lse_ref.dtype)

    @pl.when(tb_ref[g, first] >= 0)
    def _():
        for u in range(GROUP_STEP):
            used = tb_ref[g, first + u] >= 0
            rows = slice(u * t, (u + 1) * t)
            o, lse = _block_attention(q_ref[rows, :].astype(BF16), k_refs[u][...], v_refs[u][...])
            o_ref[rows, :] = jnp.where(used, o, 0.0)
            lse_ref[rows, :] = jnp.where(used, lse, MASKED)


def _moba_group_attention(q_grouped, tile_blk, qk, v):
    bh, rows, _ = q_grouped.shape
    t = MOBA_BLOCK
    tiles = rows // t
    heads = A_HEADS

    def block_of(u, first_col):
        return lambda g, s, tb: (g // heads, jnp.maximum(tb[g, s * GROUP_STEP + u], 0),
                                 first_col + g % heads)

    row_tile = pl.BlockSpec((None, GROUP_STEP * t, HEAD_DIM), lambda g, s, tb: (g, s, 0))
    key_value = lambda first_col: [pl.BlockSpec((None, t, HEAD_DIM), block_of(u, first_col))
                                   for u in range(GROUP_STEP)]
    grid_spec = pltpu.PrefetchScalarGridSpec(
        num_scalar_prefetch=1,
        grid=(bh, tiles // GROUP_STEP),
        in_specs=[row_tile] + key_value(KA_BLK) + key_value(0),
        out_specs=[row_tile, row_tile],
    )
    return pl.pallas_call(
        _moba_group_kernel,
        grid_spec=grid_spec,
        out_shape=[jax.ShapeDtypeStruct(q_grouped.shape, F32)] * 2,
        compiler_params=_cparams("parallel", "parallel"),
        name="moba_group",
    )(tile_blk, q_grouped, *([qk] * GROUP_STEP), *([v] * GROUP_STEP))


def _moba_merge_kernel(q_ref, k_ref, v_ref, po_ref, pl_ref, o_ref):
    t = MOBA_BLOCK
    causal = (lax.broadcasted_iota(jnp.int32, (t, t), 1)
              <= lax.broadcasted_iota(jnp.int32, (t, t), 0))
    for hd in range(A_HEADS):
        sl = slice(hd * HEAD_DIM, (hd + 1) * HEAD_DIM)
        o_own, lse_own = _block_attention(q_ref[:, sl], k_ref[:, sl], v_ref[:, sl], causal)
        outs = [o_own] + [po_ref[s, hd] for s in range(MOBA_TOPK)]
        lses = [lse_own] + [pl_ref[s, hd] for s in range(MOBA_TOPK)]
        top = functools.reduce(jnp.maximum, lses)
        weights = [jnp.exp2(l - top) for l in lses]
        num = sum(w * o for w, o in zip(weights, outs))
        o_ref[:, sl] = (num / sum(weights)).astype(o_ref.dtype)


def _moba_merge(qk, v, part_o, part_lse):
    bsz, seq, _ = qk.shape
    t = MOBA_BLOCK
    part = pl.BlockSpec((MOBA_TOPK, None, A_HEADS, t, HEAD_DIM), lambda b, i: (0, b, 0, i, 0))
    return pl.pallas_call(
        _moba_merge_kernel,
        grid=(bsz, seq // t),
        in_specs=[pl.BlockSpec((None, t, A_W), lambda b, i: (b, i, QA_BLK * LANES // A_W)),
                  pl.BlockSpec((None, t, A_W), lambda b, i: (b, i, KA_BLK * LANES // A_W)),
                  pl.BlockSpec((None, t, A_W), lambda b, i: (b, i, 0)),
                  part, part],
        out_specs=pl.BlockSpec((None, t, A_W), lambda b, i: (b, i, 0)),
        out_shape=jax.ShapeDtypeStruct((bsz, seq, A_W), BF16),
        compiler_params=_cparams("parallel", "parallel"),
        name="moba_merge",
    )(qk, qk, v, part_o, part_lse)


def _moba_attention(qk, v, kmean):
    bsz, seq, _ = qk.shape
    bh = bsz * A_HEADS
    rows = _moba_tiles(seq) * MOBA_BLOCK
    ids, cnt, q_f32 = _moba_gate(qk, kmean)
    pos, tile_blk = _moba_routes(ids, cnt, seq)
    q_grouped = _sc_scatter_rows(q_f32.reshape(bh * seq, HEAD_DIM), pos, bh * rows)
    o_g, lse_g = _moba_group_attention(q_grouped.reshape(bh, rows, HEAD_DIM), tile_blk, qk, v)
    flat = pos.reshape(-1)
    back = lambda a: _sc_gather_rows(a.reshape(bh * rows, HEAD_DIM), flat).reshape(
        MOBA_TOPK, bsz, A_HEADS, seq, HEAD_DIM)
    return _moba_merge(qk, v, back(o_g), back(lse_g))


def _proj_dilated_kernel(h_ref, w_ref, c_ref, s_ref, q_ref, k_ref, v_ref, sc, *, d):
    acc = _dot(h_ref[...], w_ref[...])
    c, s = c_ref[...], s_ref[...]
    q_scale = HEAD_DIM ** -0.5 * LOG2E
    for j in range(acc.shape[1] // LANES):
        blk = acc[:, j * LANES:(j + 1) * LANES]
        if j < B_HEADS:
            blk = _rope128(blk, c, s) * q_scale
        elif j < 2 * B_HEADS:
            blk = _rope128(blk, c, s)
        sc[j] = blk
    rows = acc.shape[0] // d
    for r in range(d):
        for j in range(acc.shape[1] // LANES):
            dst = (q_ref, k_ref, v_ref)[j // B_HEADS]
            col = (j % B_HEADS) * LANES
            dst[r, :, col:col + LANES] = sc[j, pl.ds(r, rows, stride=d), :].astype(dst.dtype)


def _proj_dilated(h, w, rope_h, d, bsz, seq, tm=512):
    m, k = h.shape
    nt = seq // tm
    pos = pl.BlockSpec((tm, LANES), lambda i: (i % nt, 0))
    out = pl.BlockSpec((None, d, tm // d, B_W), lambda i: (i // nt, 0, i % nt, 0))
    return pl.pallas_call(
        functools.partial(_proj_dilated_kernel, d=d),
        grid=(m // tm,),
        in_specs=[pl.BlockSpec((tm, k), lambda i: (i, 0)), pl.BlockSpec(w.shape, lambda i: (0, 0)),
                  pos, pos],
        out_specs=[out] * 3,
        out_shape=[jax.ShapeDtypeStruct((bsz, d, seq // d, B_W), BF16)] * 3,
        scratch_shapes=[pltpu.VMEM((w.shape[1] // LANES, tm, LANES), F32)],
        compiler_params=_cparams("parallel"),
        name=f"proj_dilated_d{d}",
    )(h, w, *rope_h)


def _dilated_kernel(q_ref, kc_ref, kp_ref, vc_ref, vp_ref, o_ref, lse_ref, *, span):
    t, tp = q_ref.shape[0], kp_ref.shape[0]
    i = pl.program_id(2)
    shape = (2 * tp, tp)
    key_row = lax.broadcasted_iota(jnp.int32, shape, 0)
    dist = lax.broadcasted_iota(jnp.int32, shape, 1) + tp - key_row
    visible = (dist >= 0) & (dist <= span)
    bias = jnp.where(visible, 0.0, MASKED)
    bias_first = jnp.where(visible & ((key_row >= tp) | (i > 0)), 0.0, MASKED)
    ones = jnp.ones((BF16_ROWS, tp + t), BF16)

    def transposed(x):
        return x.astype(F32).T.astype(BF16)

    for j in range(B_HEADS):
        sl = slice(j * LANES, (j + 1) * LANES)
        qT = transposed(q_ref[:, sl])
        k_all = jnp.concatenate([kp_ref[:, sl], kc_ref[:, sl]], axis=0)
        vt_all = jnp.concatenate([transposed(vp_ref[:, sl]), transposed(vc_ref[:, sl])], axis=1)
        vt_all = jnp.concatenate([vt_all, ones], axis=0)
        outs, lses = [], []
        for u in range(t // tp):
            window = slice(u * tp, (u + 2) * tp)
            s = _dot(k_all[window, :], qT[:, u * tp:(u + 1) * tp])
            s = s + (bias_first if u == 0 else bias)
            m = jnp.max(s, axis=0, keepdims=True)
            p = jnp.exp2(s - m).astype(BF16)
            acc = _dot(vt_all[:, window], p)
            den = acc[V_DIM:V_DIM + 1, :]
            outs.append(acc[:V_DIM, :] / den)
            lses.append(m + jnp.log2(den))
        o_ref[:, sl] = jnp.concatenate(outs, axis=1).T
        lse = jnp.concatenate(lses, axis=1)
        lse_ref[:, sl] = jnp.broadcast_to(lse, (LANES, t)).T


def _dilated_attention(q, k, v, span, t=1024):
    bsz, d, length, _ = q.shape
    t = min(t, length)
    tp = B_QBLOCK
    assert span <= tp and t % tp == 0
    cur = pl.BlockSpec((None, None, t, B_W), lambda b, r, i: (b, r, i, 0))
    prev = pl.BlockSpec((None, None, tp, B_W),
                        lambda b, r, i: (b, r, jnp.maximum(i * (t // tp) - 1, 0), 0))
    return pl.pallas_call(
        functools.partial(_dilated_kernel, span=span),
        grid=(bsz, d, length // t),
        in_specs=[cur, cur, prev, cur, prev],
        out_specs=[cur, cur],
        out_shape=[jax.ShapeDtypeStruct(q.shape, F32)] * 2,
        compiler_params=_cparams("parallel", "parallel", "parallel"),
        name=f"dilated_d{d}",
    )(q, k, k, v, v)


def _natural_rows(ref, sc):
    d, rows = ref.shape[0], ref.shape[1]
    if d == 1:
        return ref[0]
    for r in range(d):
        for j in range(B_HEADS):
            sc[j, pl.ds(r, rows, stride=d), :] = ref[r, :, j * LANES:(j + 1) * LANES]
    return jnp.concatenate([sc[j] for j in range(B_HEADS)], axis=-1)


def _mixer_tail_kernel(x_ref, oa_ref, o0_ref, o1_ref, o2_ref, l0_ref, l1_ref, l2_ref, oc_ref,
                       g_ref, wpa_ref, wpb_ref, wpc_ref, wo_ref, y_ref, *scratch):
    o0, o1, o2, l0, l1, l2 = [
        _natural_rows(ref, sc)
        for ref, sc in zip((o0_ref, o1_ref, o2_ref, l0_ref, l1_ref, l2_ref), scratch)]
    mx = jnp.maximum(jnp.maximum(l0, l1), l2)
    e0, e1, e2 = jnp.exp2(l0 - mx), jnp.exp2(l1 - mx), jnp.exp2(l2 - mx)
    ob = (e0 * o0 + e1 * o1 + e2 * o2) / (e0 + e1 + e2)
    pa = _dot(oa_ref[...], wpa_ref[...])
    pb = _dot(ob.astype(BF16), wpb_ref[...])
    pc = _dot(oc_ref[...], wpc_ref[...])
    d = D_MODEL
    merged = (g_ref[:, 0:d].astype(F32) * pa + g_ref[:, d:2 * d].astype(F32) * pb
              + g_ref[:, 2 * d:3 * d].astype(F32) * pc)
    y_ref[...] = x_ref[...] + _dot(merged.astype(BF16), wo_ref[...])


def _mixer_tail(x, out_a, o_groups, lse_groups, out_c, gates, w_pa, w_pb, w_pc, w_o, seq, tm=256):
    m, d = x.shape
    nt = seq // tm
    row = lambda width: pl.BlockSpec((tm, width), lambda i: (i, 0))
    residue = lambda g: pl.BlockSpec((None, g.shape[1], tm // g.shape[1], B_W),
                                     lambda i: (i // nt, 0, i % nt, 0))
    weights = [_resident(w) for w in (w_pa, w_pb, w_pc, w_o)]
    groups = list(o_groups) + list(lse_groups)
    return pl.pallas_call(
        _mixer_tail_kernel,
        grid=(m // tm,),
        in_specs=([row(d), row(A_W)] + [residue(g) for g in groups]
                  + [row(C_W), row(3 * d)] + weights),
        out_specs=row(d),
        out_shape=jax.ShapeDtypeStruct((m, d), F32),
        scratch_shapes=[pltpu.VMEM((B_HEADS, tm, LANES), F32) for _ in groups],
        compiler_params=_cparams("parallel"),
        name="mixer_tail",
    )(x, out_a, *groups, out_c, gates, w_pa, w_pb, w_pc, w_o)


def _mem_kv_kernel(mem_ref, g_ref, wk_ref, wv_ref, k_ref, v_ref):
    memn = _rms(mem_ref[...], g_ref[...]).astype(BF16)
    k_ref[...] = _dot(memn, wk_ref[...]).astype(k_ref.dtype)
    v_ref[...] = _dot(memn, wv_ref[...]).astype(v_ref.dtype)


def _mem_kv(mem, g, wk, wv):
    bsz, n, d = mem.shape
    out = pl.BlockSpec((None, n, X_W), lambda b: (b, 0, 0))
    return pl.pallas_call(
        _mem_kv_kernel,
        grid=(bsz,),
        in_specs=[pl.BlockSpec((None, n, d), lambda b: (b, 0, 0)),
                  pl.BlockSpec((1, d), lambda b: (0, 0)),
                  pl.BlockSpec(wk.shape, lambda b: (0, 0)),
                  pl.BlockSpec(wv.shape, lambda b: (0, 0))],
        out_specs=[out, out],
        out_shape=[jax.ShapeDtypeStruct((bsz, n, X_W), BF16)] * 2,
        compiler_params=_cparams("parallel"),
        name="mem_kv",
    )(mem, g.reshape(1, d), wk, wv)


def _mem_attn_kernel(x_ref, g_ref, wq_ref, k_ref, v_ref, wo_ref, y_ref):
    x = x_ref[...]
    h = _rms(x, g_ref[...]).astype(BF16)
    q = (_dot(h, wq_ref[...]) * HEAD_DIM ** -0.5).astype(BF16)
    heads = []
    for hd in range(X_HEADS):
        sl = slice(hd * HEAD_DIM, (hd + 1) * HEAD_DIM)
        s = _dot_nt(q[:, sl], k_ref[:, sl])
        p = jnp.exp(s - jnp.max(s, axis=-1, keepdims=True))
        o = _dot(p.astype(BF16), v_ref[:, sl]) / jnp.sum(p, axis=-1, keepdims=True)
        heads.append(o.astype(BF16))
    y_ref[...] = x + _dot(jnp.concatenate(heads, axis=-1), wo_ref[...])


def _mem_attention(x, g, wq, kmem, vmem, wo, seq, tm=512):
    m, d = x.shape
    nt = seq // tm
    n = kmem.shape[1]
    kv = pl.BlockSpec((None, n, X_W), lambda i: (i // nt, 0, 0))
    return pl.pallas_call(
        _mem_attn_kernel,
        grid=(m // tm,),
        in_specs=[pl.BlockSpec((tm, d), lambda i: (i, 0)),
                  pl.BlockSpec((1, d), lambda i: (0, 0)),
                  pl.BlockSpec(wq.shape, lambda i: (0, 0)), kv, kv,
                  pl.BlockSpec(wo.shape, lambda i: (0, 0))],
        out_specs=pl.BlockSpec((tm, d), lambda i: (i, 0)),
        out_shape=jax.ShapeDtypeStruct((m, d), F32),
        compiler_params=_cparams("parallel"),
        name="mem_attention",
    )(x, g.reshape(1, d), wq, kmem, vmem, wo)


def _ffn_up_kernel(x_ref, halo_ref, g_ref, wg_ref, wv_ref, cwg_ref, cwv_ref, cbg_ref, cbv_ref,
                   act_ref, h_sc, *, tiles_per_seq):
    i = pl.program_id(0)
    tm = x_ref.shape[0]

    @pl.when(pl.program_id(1) == 0)
    def _():
        g = g_ref[...]
        keep = (i % tiles_per_seq != 0).astype(F32)
        h_sc[0:HALO, :] = (_rms(halo_ref[...], g) * keep).astype(h_sc.dtype)
        h_sc[HALO:, :] = _rms(x_ref[...], g).astype(h_sc.dtype)

    h = h_sc[...]

    def conv(w_ref, cw_ref, cb_ref):
        u = _dot(h, w_ref[...])
        c = cb_ref[...]
        for tap in range(CONV_W):
            lo = HALO - (CONV_W - 1) + tap
            c = c + cw_ref[tap:tap + 1, :] * u[lo:lo + tm, :]
        return c

    act = jax.nn.silu(conv(wg_ref, cwg_ref, cbg_ref)) * conv(wv_ref, cwv_ref, cbv_ref)
    act_ref[...] = act.astype(act_ref.dtype)


def _ffn_down_kernel(a_ref, w_ref, x_ref, y_ref):
    y_ref[...] = x_ref[...] + _dot(a_ref[...], w_ref[...])


def _conv_ffn(x, g, w_up, conv_w, conv_b, w_down, seq):
    m, d = x.shape
    act = _ffn_up(x, g, w_up, conv_w, conv_b, seq)
    tm, tn = 1024, FFN_TF
    return pl.pallas_call(
        _ffn_down_kernel,
        grid=(m // tm, d // tn),
        in_specs=[pl.BlockSpec((tm, D_FF_PAD), lambda i, j: (i, 0)),
                  pl.BlockSpec((D_FF_PAD, tn), lambda i, j: (0, j)),
                  pl.BlockSpec((tm, tn), lambda i, j: (i, j))],
        out_specs=pl.BlockSpec((tm, tn), lambda i, j: (i, j)),
        out_shape=jax.ShapeDtypeStruct((m, d), F32),
        compiler_params=_cparams("parallel", "parallel"),
        name="ffn_down",
    )(act, w_down, x)


def _ffn_up(x, g, w_up, conv_w, conv_b, seq, tm=1024, tf=FFN_TF):
    m, d = x.shape
    nf = D_FF_PAD // tf
    halo_blocks = tm // HALO
    return pl.pallas_call(
        functools.partial(_ffn_up_kernel, tiles_per_seq=seq // tm),
        grid=(m // tm, nf),
        in_specs=[pl.BlockSpec((tm, d), lambda i, f: (i, 0)),
                  pl.BlockSpec((HALO, d), lambda i, f: (jnp.maximum(i * halo_blocks - 1, 0), 0)),
                  pl.BlockSpec((1, d), lambda i, f: (0, 0)),
                  pl.BlockSpec((d, tf), lambda i, f: (0, f)),
                  pl.BlockSpec((d, tf), lambda i, f: (0, f + nf)),
                  pl.BlockSpec((CONV_W, tf), lambda i, f: (0, f)),
                  pl.BlockSpec((CONV_W, tf), lambda i, f: (0, f + nf)),
                  pl.BlockSpec((1, tf), lambda i, f: (0, f)),
                  pl.BlockSpec((1, tf), lambda i, f: (0, f + nf))],
        out_specs=pl.BlockSpec((tm, tf), lambda i, f: (i, f)),
        out_shape=jax.ShapeDtypeStruct((m, D_FF_PAD), BF16),
        scratch_shapes=[pltpu.VMEM((HALO + tm, d), BF16)],
        compiler_params=_cparams("parallel", "arbitrary"),
        name="ffn_up",
    )(x, x, g.reshape(1, d), w_up, w_up, conv_w, conv_w, conv_b, conv_b)


def _rope_tables(seq):
    def angles(dim):
        inv_freq = jnp.exp(jnp.arange(0, dim, 2, dtype=F32) * (-math.log(ROPE_THETA) / dim))
        ang = jnp.arange(seq, dtype=F32)[:, None] * inv_freq[None, :]
        return jnp.cos(ang), jnp.sin(ang)

    cos_h, sin_h = angles(HEAD_DIM)
    rope_h = (jnp.concatenate([cos_h, cos_h], axis=-1), jnp.concatenate([-sin_h, sin_h], axis=-1))
    cos_r, sin_r = angles(ROPE_DIM)
    z = jnp.zeros_like(cos_r)
    rope_r = (jnp.concatenate([cos_r, cos_r, z, z], axis=-1),
              jnp.concatenate([-sin_r, z, z, z], axis=-1),
              jnp.concatenate([z, sin_r, z, z], axis=-1))
    return rope_h, rope_r


def _split_in(w_in):
    return [w_in[:, IN_OFFSETS[k]:IN_OFFSETS[k + 1]] for k in range(len(IN_WIDTHS))]


def _pad_cols(w, width):
    return jnp.pad(w, ((0, 0), (0, width - w.shape[1])))


def _layer_params(w_in, w_uq, w_ukv, w_up, conv_w, conv_b, w_down):
    qa, ka, va, qb, kb, vb, cq, ckv, kr, gates = _split_in(w_in)
    w_qk = jnp.concatenate([qa, ka], axis=1).astype(BF16)
    group_cols = lambda w, g: w[:, g * B_W:(g + 1) * B_W]
    w_b = [jnp.concatenate([group_cols(qb, g), group_cols(kb, g), group_cols(vb, g)],
                           axis=1).astype(BF16) for g in range(len(B_GROUPS))]
    w_down_in = jnp.concatenate([cq, ckv, _pad_cols(kr, LANES)], axis=1).astype(BF16)
    uq = w_uq.reshape(Q_LORA, C_HEADS, NOPE_DIM + ROPE_DIM)
    uq = jnp.pad(uq, ((0, 0), (0, 0), (0, C_QK - NOPE_DIM - ROPE_DIM)))
    ukv = w_ukv.reshape(KV_LORA, C_HEADS, NOPE_DIM + V_DIM)
    pad_ff = lambda w: jnp.pad(w, ((0, 0), (0, D_FF_PAD - D_FF)))
    two_halves = lambda w: jnp.concatenate([pad_ff(w[:, :D_FF]), pad_ff(w[:, D_FF:])], axis=1)
    return dict(
        w_qk=w_qk, w_va=va.astype(BF16), w_b=w_b, w_gates=gates.astype(BF16),
        w_down_in=w_down_in,
        w_uq=uq.reshape(Q_LORA, C_HEADS * C_QK).astype(BF16),
        w_uk=ukv[:, :, :NOPE_DIM].reshape(KV_LORA, C_HEADS * NOPE_DIM).astype(BF16),
        w_uv=ukv[:, :, NOPE_DIM:].reshape(KV_LORA, C_W).astype(BF16),
        w_up=two_halves(w_up).astype(BF16),
        conv_w=two_halves(conv_w),
        conv_b=two_halves(conv_b.reshape(1, -1)),
        w_down=jnp.pad(w_down, ((0, D_FF_PAD - D_FF), (0, 0))).astype(BF16),
    )


def _qk_col_scale():
    q_scale = HEAD_DIM ** -0.5
    parts = [jnp.full((A_W,), q_scale * LOG2E, F32), jnp.ones((A_W,), F32)]
    return jnp.concatenate(parts).reshape(1, QK_W)


def _mixer(x, g_mix, p, g_cq, g_ckv, w_pa, w_pb, w_pc, w_o, rope_h, rope_r, bsz, seq):
    m = x.shape[0]
    h = _rmsnorm(x, g_mix, BF16)
    qk = _matmul(h, p["w_qk"], _mm_rope_kernel, BF16, 1024, 1024, seq=seq,
                 extras=(("col", _qk_col_scale()), ("pos", rope_h[0]), ("pos", rope_h[1])),
                 name="proj_qk_rope")
    v_a = _matmul(h, p["w_va"], _mm_plain_kernel, BF16, 1024, A_W, name="proj_va")
    gates = _matmul(h, p["w_gates"], _mm_sigmoid_kernel, BF16, 1024, 1024, name="proj_gates")
    cq, ckv, kr = _mla_down(h, p["w_down_in"], g_cq, g_ckv, rope_r, seq)
    q_c = _mla_q(cq, p["w_uq"], rope_r, seq)
    k_c, vt_c = _mla_kv(ckv, kr, p["w_uk"], p["w_uv"], bsz, seq)

    qk3 = qk.reshape(bsz, seq, QK_W)
    out_a = _moba_attention(qk3, v_a.reshape(bsz, seq, A_W), _kmean(qk3)).reshape(m, A_W)
    groups = []
    for (window, d), w_g in zip(B_GROUPS, p["w_b"]):
        q_g, k_g, v_g = _proj_dilated(h, w_g, rope_h, d, bsz, seq)
        groups.append(_dilated_attention(q_g, k_g, v_g, window // d))
    out_c = _flash_attention(q_c.reshape(bsz, seq, -1), k_c.reshape(bsz, seq, -1), vt_c,
                             C_HEADS, C_QK).reshape(m, C_W)
    return _mixer_tail(x, out_a, [g[0] for g in groups], [g[1] for g in groups], out_c, gates,
                       w_pa.astype(BF16), w_pb.astype(BF16), w_pc.astype(BF16), w_o.astype(BF16),
                       seq)


def kernel(x, mem, g_mix, w_in, g_cq, g_ckv, w_uq, w_ukv, w_pa, w_pb, w_pc, w_o, g_mem, g_memkv,
           w_xq, w_xk, w_xv, w_xo, g_ffn, w_up, conv_w, conv_b, w_down, g_final):
    bsz, seq, d = x.shape
    rope_h, rope_r = _rope_tables(seq)
    xf = x.reshape(bsz * seq, d)
    for l in range(DEPTH):
        p = _layer_params(w_in[l], w_uq[l], w_ukv[l], w_up[l], conv_w[l], conv_b[l], w_down[l])
        xf = _mixer(xf, g_mix[l], p, g_cq[l], g_ckv[l], w_pa[l], w_pb[l], w_pc[l], w_o[l],
                    rope_h, rope_r, bsz, seq)
        kmem, vmem = _mem_kv(mem, g_memkv[l], w_xk[l].astype(BF16), w_xv[l].astype(BF16))
        xf = _mem_attention(xf, g_mem[l], w_xq[l].astype(BF16), kmem, vmem,
                            w_xo[l].astype(BF16), seq)
        xf = _conv_ffn(xf, g_ffn[l], p["w_up"], p["conv_w"], p["conv_b"], p["w_down"], seq)
    return _rmsnorm(xf, g_final, F32).reshape(bsz, seq, d)
```

```python
import functools
import math

import jax
import jax.numpy as jnp
import numpy as np
from jax import lax
from jax.experimental import pallas as pl
from jax.experimental.pallas import tpu as pltpu
from jax.experimental.pallas import tpu_sc as plsc

F32 = jnp.float32
BF16 = jnp.bfloat16

LANES = 128
SUBLANES = 8
VMEM_LIMIT = 56 * 1024 * 1024

D_MODEL = 2048
DEPTH = 2
HEAD_DIM = 128
ROPE_THETA = 10000.0
EPS = 1e-6

A_HEADS = 4
MOBA_BLOCK = 256
MOBA_TOPK = 3

B_GROUPS = ((128, 1), (512, 4), (2048, 16))
B_HEADS = 4
B_QBLOCK = 128

C_HEADS = 8
Q_LORA = 1536
KV_LORA = 512
NOPE_DIM = 128
ROPE_DIM = 64
V_DIM = 128

X_HEADS = 4
D_FF = 5504
CONV_W = 3

A_W = A_HEADS * HEAD_DIM
B_QKV_W = len(B_GROUPS) * B_HEADS * HEAD_DIM
B_W = B_HEADS * HEAD_DIM
C_W = C_HEADS * V_DIM
X_W = X_HEADS * HEAD_DIM
IN_WIDTHS = (A_W, A_W, A_W, B_QKV_W, B_QKV_W, B_QKV_W, Q_LORA, KV_LORA, ROPE_DIM, 3 * D_MODEL)
IN_OFFSETS = tuple(int(o) for o in np.cumsum((0,) + IN_WIDTHS))

QK_W = 2 * A_W
QA_BLK, KA_BLK = 0, A_W // LANES

C_QK = 2 * LANES
MASKED = -1e30
LOG2E = math.log2(math.e)
BF16_ROWS = 16
VT_ROWS = V_DIM + BF16_ROWS
GROUP_STEP = 8
SC_WINDOW = 128

D_FF_PAD = 5632
FFN_TF = 512
HALO = SUBLANES


def _cparams(*sem):
    return pltpu.CompilerParams(dimension_semantics=sem, vmem_limit_bytes=VMEM_LIMIT)


def _resident(arr):
    zeros = (0,) * arr.ndim
    return pl.BlockSpec(arr.shape, lambda *_: zeros, pipeline_mode=pl.Buffered(1))


def _dot(a, b):
    return jnp.dot(a, b, preferred_element_type=F32)


def _dot_nt(a, b):
    return lax.dot_general(a, b, (((1,), (1,)), ((), ())), preferred_element_type=F32)


def _rms(x, g):
    return x * lax.rsqrt(jnp.mean(x * x, axis=-1, keepdims=True) + EPS) * g


def _rmsnorm_kernel(x_ref, g_ref, o_ref):
    o_ref[...] = _rms(x_ref[...], g_ref[...]).astype(o_ref.dtype)


def _rmsnorm(x, g, out_dtype, tm=512):
    m, d = x.shape
    return pl.pallas_call(
        _rmsnorm_kernel,
        grid=(m // tm,),
        in_specs=[pl.BlockSpec((tm, d), lambda i: (i, 0)),
                  pl.BlockSpec((1, d), lambda i: (0, 0))],
        out_specs=pl.BlockSpec((tm, d), lambda i: (i, 0)),
        out_shape=jax.ShapeDtypeStruct((m, d), out_dtype),
        compiler_params=_cparams("parallel"),
        name="rmsnorm",
    )(x, g.reshape(1, d))


def _rope128(x, c, s):
    return x * c + pltpu.roll(x, HEAD_DIM // 2, 1) * s


def _rope64(x, c, sa, sb):
    half = ROPE_DIM // 2
    return x * c + pltpu.roll(x, LANES - half, 1) * sa + pltpu.roll(x, half, 1) * sb


def _mm_plain_kernel(a_ref, w_ref, o_ref):
    o_ref[...] = _dot(a_ref[...], w_ref[...]).astype(o_ref.dtype)


def _mm_sigmoid_kernel(a_ref, w_ref, o_ref):
    o_ref[...] = jax.nn.sigmoid(_dot(a_ref[...], w_ref[...])).astype(o_ref.dtype)


def _mm_rope_kernel(a_ref, w_ref, cs_ref, c_ref, s_ref, o_ref):
    acc = _dot(a_ref[...], w_ref[...])
    c = c_ref[...]
    s = s_ref[...]
    for j in range(acc.shape[1] // LANES):
        sl = slice(j * LANES, (j + 1) * LANES)
        o_ref[:, sl] = (_rope128(acc[:, sl], c, s) * cs_ref[:, sl]).astype(o_ref.dtype)


def _matmul(a, w, kernel, out_dtype, tm, tn, seq=None, extras=(), name="matmul"):
    m, k = a.shape
    n = w.shape[1]
    in_specs = [pl.BlockSpec((tm, k), lambda i, j: (i, 0)),
                pl.BlockSpec((k, tn), lambda i, j: (0, j))]
    args = [a, w]
    for kind, arr in extras:
        if kind == "col":
            in_specs.append(pl.BlockSpec((1, tn), lambda i, j: (0, j)))
        else:
            nt = seq // tm
            in_specs.append(pl.BlockSpec((tm, LANES), lambda i, j: (i % nt, 0)))
        args.append(arr)
    return pl.pallas_call(
        kernel,
        grid=(m // tm, n // tn),
        in_specs=in_specs,
        out_specs=pl.BlockSpec((tm, tn), lambda i, j: (i, j)),
        out_shape=jax.ShapeDtypeStruct((m, n), out_dtype),
        compiler_params=_cparams("parallel", "parallel"),
        name=name,
    )(*args)


def _mla_down_kernel(h_ref, w_ref, gq_ref, gkv_ref, c_ref, sa_ref, sb_ref,
                     cq_ref, ckv_ref, kr_ref):
    acc = _dot(h_ref[...], w_ref[...])
    cq_ref[...] = _rms(acc[:, :Q_LORA], gq_ref[...]).astype(cq_ref.dtype)
    ckv_ref[...] = _rms(acc[:, Q_LORA:Q_LORA + KV_LORA], gkv_ref[...]).astype(ckv_ref.dtype)
    kr = acc[:, Q_LORA + KV_LORA:]
    kr_ref[...] = _rope64(kr, c_ref[...], sa_ref[...], sb_ref[...]).astype(kr_ref.dtype)


def _mla_down(h, w, g_cq, g_ckv, rope_r, seq, tm=512):
    m, k = h.shape
    n = w.shape[1]
    nt = seq // tm
    row = lambda width: pl.BlockSpec((tm, width), lambda i: (i, 0))
    full = lambda r, c: pl.BlockSpec((r, c), lambda i: (0, 0))
    pos = pl.BlockSpec((tm, LANES), lambda i: (i % nt, 0))
    return pl.pallas_call(
        _mla_down_kernel,
        grid=(m // tm,),
        in_specs=[row(k), full(k, n), full(1, Q_LORA), full(1, KV_LORA), pos, pos, pos],
        out_specs=[row(Q_LORA), row(KV_LORA), row(LANES)],
        out_shape=[jax.ShapeDtypeStruct((m, Q_LORA), BF16),
                   jax.ShapeDtypeStruct((m, KV_LORA), BF16),
                   jax.ShapeDtypeStruct((m, LANES), BF16)],
        compiler_params=_cparams("parallel"),
        name="mla_down",
    )(h, w, g_cq.reshape(1, -1), g_ckv.reshape(1, -1), *rope_r)


def _mla_q_kernel(cq_ref, w_ref, c_ref, sa_ref, sb_ref, q_ref, *, scale):
    acc = _dot(cq_ref[...], w_ref[...])
    c, sa, sb = c_ref[...], sa_ref[...], sb_ref[...]
    for hd in range(C_HEADS):
        lo = hd * C_QK
        q_ref[:, lo:lo + LANES] = (acc[:, lo:lo + LANES] * scale).astype(q_ref.dtype)
        rope = _rope64(acc[:, lo + LANES:lo + C_QK], c, sa, sb)
        q_ref[:, lo + LANES:lo + C_QK] = (rope * scale).astype(q_ref.dtype)


def _mla_q(cq, w, rope_r, seq, tm=512):
    m, k = cq.shape
    n = w.shape[1]
    nt = seq // tm
    pos = pl.BlockSpec((tm, LANES), lambda i: (i % nt, 0))
    return pl.pallas_call(
        functools.partial(_mla_q_kernel, scale=(NOPE_DIM + ROPE_DIM) ** -0.5 * LOG2E),
        grid=(m // tm,),
        in_specs=[pl.BlockSpec((tm, k), lambda i: (i, 0)),
                  pl.BlockSpec((k, n), lambda i: (0, 0)), pos, pos, pos],
        out_specs=pl.BlockSpec((tm, n), lambda i: (i, 0)),
        out_shape=jax.ShapeDtypeStruct((m, n), BF16),
        compiler_params=_cparams("parallel"),
        name="mla_q",
    )(cq, w, *rope_r)


def _store_vt(v, vt_ref):
    vt = v.T
    for hd in range(vt_ref.shape[0]):
        vt_ref[hd, 0:V_DIM, :] = vt[hd * V_DIM:(hd + 1) * V_DIM, :].astype(vt_ref.dtype)
        vt_ref[hd, V_DIM:VT_ROWS, :] = jnp.ones((VT_ROWS - V_DIM, vt.shape[1]), vt_ref.dtype)


def _mla_kv_kernel(ckv_ref, kr_ref, wk_ref, wv_ref, k_ref, vt_ref):
    ckv = ckv_ref[...]
    kn = _dot(ckv, wk_ref[...])
    kr = kr_ref[...]
    for hd in range(C_HEADS):
        lo = hd * C_QK
        k_ref[:, lo:lo + LANES] = kn[:, hd * LANES:(hd + 1) * LANES].astype(k_ref.dtype)
        k_ref[:, lo + LANES:lo + C_QK] = kr
    _store_vt(_dot(ckv, wv_ref[...]), vt_ref)


def _mla_kv(ckv, kr, wk, wv, bsz, seq, tm=512):
    m, k = ckv.shape
    nt = seq // tm
    return pl.pallas_call(
        _mla_kv_kernel,
        grid=(m // tm,),
        in_specs=[pl.BlockSpec((tm, k), lambda i: (i, 0)),
                  pl.BlockSpec((tm, LANES), lambda i: (i, 0)),
                  pl.BlockSpec(wk.shape, lambda i: (0, 0)),
                  pl.BlockSpec(wv.shape, lambda i: (0, 0))],
        out_specs=[pl.BlockSpec((tm, C_HEADS * C_QK), lambda i: (i, 0)),
                   pl.BlockSpec((None, C_HEADS, VT_ROWS, tm), lambda i: (i // nt, 0, 0, i % nt))],
        out_shape=[jax.ShapeDtypeStruct((m, C_HEADS * C_QK), BF16),
                   jax.ShapeDtypeStruct((bsz, C_HEADS, VT_ROWS, seq), BF16)],
        compiler_params=_cparams("parallel"),
        name="mla_kv",
    )(ckv, kr, wk, wv)


def _attend_chunks(qT, k_ref, vt_ref, scratch, *, tk, n_full, mask_tail, tail_steps, tail_col,
                   unroll):
    m_sc, acc_sc, sa_sc, sb_sc, pa_sc, pb_sc, ala_sc, alb_sc, mxa_sc, mxb_sc = scratch
    s_bufs = (sa_sc, sb_sc)
    p_bufs = (pa_sc, pb_sc)
    al_bufs = (ala_sc, alb_sc)
    mx_bufs = (mxa_sc, mxb_sc)
    last_chunk = k_ref.shape[0] // tk - 1
    m_sc[...] = jnp.full(m_sc.shape, MASKED, F32)
    acc_sc[...] = jnp.zeros(acc_sc.shape, F32)
    for p_ref, al_ref in zip(p_bufs, al_bufs):
        p_ref[...] = jnp.zeros(p_ref.shape, p_ref.dtype)
        al_ref[...] = jnp.ones(al_ref.shape, F32)

    def rows(c):
        return pl.ds(pl.multiple_of(jnp.clip(c, 0, last_chunk) * tk, tk), tk)

    def scores(c, slot, col=0):
        sT = _dot(k_ref[rows(c), :], qT[:, col:])
        s_bufs[slot][:, col:] = sT
        mx_bufs[slot][:, col:] = jnp.max(sT, axis=0, keepdims=True)

    def flush(c, slot, col=0):
        acc_sc[:, col:] = (al_bufs[slot][:, col:] * acc_sc[:, col:]
                           + _dot(vt_ref[:, rows(c)], p_bufs[slot][:, col:]))

    def softmax(sT, top, slot, col):
        m_old = m_sc[:, col:]
        m_new = jnp.maximum(m_old, top)
        al_bufs[slot][:, col:] = jnp.exp2(m_old - m_new)
        p_bufs[slot][:, col:] = jnp.exp2(sT - m_new).astype(p_bufs[slot].dtype)
        m_sc[:, col:] = m_new

    def step(tau, slot, mask, cols):
        col_flush, col, col_next = cols
        flush(tau - 2, slot, col_flush)
        if col_next is not None:
            scores(tau + 1, 1 - slot, col_next)
        sT = s_bufs[slot][:, col:]
        if mask is None:
            softmax(sT, mx_bufs[slot][:, col:], slot, col)
        else:
            sT = mask(sT, tau, col)
            softmax(sT, jnp.max(sT, axis=0, keepdims=True), slot, col)

    def full_steps(tau0, count):
        for j in range(count):
            step(tau0 + j, j % 2, None, (0, 0, 0))

    scores(0, 0)
    trips = n_full // unroll
    lax.fori_loop(0, trips, lambda u, c: (full_steps(unroll * u, unroll), c)[1], 0)
    done = unroll * trips
    pairs = (n_full - done) // 2
    lax.fori_loop(0, pairs, lambda u, c: (full_steps(done + 2 * u, 2), c)[1], 0)
    tau = done + 2 * pairs
    cols = [tail_col(j) for j in range(tail_steps)]
    for j in range(tail_steps):
        col_flush = cols[j - 2] if j >= 2 else 0
        col_next = cols[j + 1] if j + 1 < tail_steps else None
        step(tau + j, j % 2, mask_tail, (col_flush, cols[j], col_next))
    flush(tau + tail_steps - 2, 0, cols[-2])
    flush(tau + tail_steps - 1, 1, cols[-1])
    acc = acc_sc[...]
    return acc[:V_DIM, :] / acc[V_DIM:V_DIM + 1, :]


def _attend_scratch(tq, tk):
    return [pltpu.VMEM((1, tq), F32), pltpu.VMEM((VT_ROWS, tq), F32),
            pltpu.VMEM((tk, tq), F32), pltpu.VMEM((tk, tq), F32),
            pltpu.VMEM((tk, tq), BF16), pltpu.VMEM((tk, tq), BF16),
            pltpu.VMEM((1, tq), F32), pltpu.VMEM((1, tq), F32),
            pltpu.VMEM((1, tq), F32), pltpu.VMEM((1, tq), F32)]


def _transpose_q(q_ref):
    return q_ref[...].astype(F32).T.astype(BF16)


def _flash_kernel(q_ref, k_ref, vt_ref, o_ref, *scratch, tq, tk):
    i = pl.program_id(2)

    def causal(sT, c, col):
        key = lax.broadcasted_iota(jnp.int32, sT.shape, 0) + c * tk
        qry = lax.broadcasted_iota(jnp.int32, sT.shape, 1) + (i * tq + col)
        return jnp.where(key <= qry, sT, MASKED)

    per_tile = tq // tk
    oT = _attend_chunks(_transpose_q(q_ref), k_ref, vt_ref, scratch, tk=tk, n_full=i * per_tile,
                        mask_tail=causal, tail_steps=per_tile, tail_col=lambda j: j * tk,
                        unroll=8)
    o_ref[...] = oT.T.astype(o_ref.dtype)


def _flash_attention(q, k, vt, heads, qk_w, tq=1024, tk=256):
    bsz, seq, _ = q.shape
    assert tq % (2 * tk) == 0 and seq % tq == 0
    return pl.pallas_call(
        functools.partial(_flash_kernel, tq=tq, tk=tk),
        grid=(bsz, heads, seq // tq),
        in_specs=[pl.BlockSpec((None, tq, qk_w), lambda b, h, i: (b, i, h)),
                  pl.BlockSpec((None, seq, qk_w), lambda b, h, i: (b, 0, h)),
                  pl.BlockSpec((None, None, VT_ROWS, seq), lambda b, h, i: (b, h, 0, 0))],
        out_specs=pl.BlockSpec((None, tq, V_DIM), lambda b, h, i: (b, i, h)),
        out_shape=jax.ShapeDtypeStruct((bsz, seq, heads * V_DIM), BF16),
        scratch_shapes=_attend_scratch(tq, tk),
        compiler_params=_cparams("parallel", "parallel", "arbitrary"),
        name="mla_flash",
    )(q, k, vt)


def _kmean_kernel(k_ref, o_ref):
    k = k_ref[...].astype(F32)
    o_ref[...] = jnp.mean(k.reshape(SUBLANES, MOBA_BLOCK, k.shape[-1]), axis=1)


def _kmean(qk):
    bsz, seq, _ = qk.shape
    rows = SUBLANES * MOBA_BLOCK
    return pl.pallas_call(
        _kmean_kernel,
        grid=(bsz, seq // rows),
        in_specs=[pl.BlockSpec((None, rows, A_W), lambda b, i: (b, i, KA_BLK * LANES // A_W))],
        out_specs=pl.BlockSpec((None, SUBLANES, A_W), lambda b, i: (b, i, 0)),
        out_shape=jax.ShapeDtypeStruct((bsz, seq // MOBA_BLOCK, A_W), F32),
        compiler_params=_cparams("parallel", "parallel"),
        name="moba_kmean",
    )(qk)


def _block_attention(q, k, v, visible=None):
    s = _dot_nt(q, k)
    if visible is not None:
        s = jnp.where(visible, s, MASKED)
    m = jnp.max(s, axis=-1, keepdims=True)
    p = jnp.exp2(s - m).astype(BF16)
    v_ones = jnp.concatenate([v, jnp.ones((v.shape[0], LANES), v.dtype)], axis=-1)
    acc = _dot(p, v_ones)
    den = acc[:, V_DIM:]
    return acc[:, :V_DIM] / den, m + jnp.log2(den)


def _moba_gate_kernel(q_ref, km_ref, ids_ref, cnt_ref, qf_ref):
    t = MOBA_BLOCK
    i = pl.program_id(1)
    nb = km_ref.shape[0]
    blk = lax.broadcasted_iota(jnp.int32, (nb, t), 0)
    neg_inf = jnp.float32(-jnp.inf)
    not_after = (lax.broadcasted_iota(jnp.int32, (t, t), 0)
                 <= lax.broadcasted_iota(jnp.int32, (t, t), 1))
    upper = jnp.where(not_after, 1.0, 0.0).astype(BF16)
    ones = jnp.ones((SUBLANES, t), BF16)
    for hd in range(A_HEADS):
        sl = slice(hd * HEAD_DIM, (hd + 1) * HEAD_DIM)
        q = q_ref[:, sl].astype(F32)
        qf_ref[hd] = q
        qT = q.T.astype(BF16)
        km = km_ref[:, sl]
        km_hi = km.astype(BF16)
        km_lo = (km - km_hi.astype(F32)).astype(BF16)
        g = jnp.where(blk < i, _dot(km_hi, qT) + _dot(km_lo, qT), neg_inf)
        picks, ranks, counts = [], [], []
        for _ in range(MOBA_TOPK):
            mx = jnp.max(g, axis=0, keepdims=True)
            is_max = (g == mx) & (mx > neg_inf)
            first = jnp.min(jnp.where(is_max, blk, nb), axis=0, keepdims=True)
            pick = blk == first
            g = jnp.where(pick, neg_inf, g)
            onehot = jnp.where(pick, 1.0, 0.0).astype(BF16)
            before = _dot(onehot, upper)
            rank = jnp.sum(jnp.where(pick, before - 1.0, 0.0), axis=0, keepdims=True)
            picks.append(first)
            ranks.append(rank.astype(jnp.int32))
            counts.append(_dot_nt(ones, onehot)[0:1, :])
        pad_i = jnp.zeros((SUBLANES - 2 * MOBA_TOPK, t), jnp.int32)
        ids_ref[hd] = jnp.concatenate(picks + ranks + [pad_i], axis=0)
        pad_f = jnp.zeros((SUBLANES - MOBA_TOPK, nb), F32)
        cnt_ref[hd] = jnp.concatenate(counts + [pad_f], axis=0)


def _moba_gate(qk, kmean):
    bsz, seq, _ = qk.shape
    t = MOBA_BLOCK
    nb = seq // t
    return pl.pallas_call(
        _moba_gate_kernel,
        grid=(bsz, nb),
        in_specs=[pl.BlockSpec((None, t, A_W), lambda b, i: (b, i, QA_BLK * LANES // A_W)),
                  pl.BlockSpec((None, nb, A_W), lambda b, i: (b, 0, 0))],
        out_specs=[pl.BlockSpec((None, A_HEADS, SUBLANES, t), lambda b, i: (b, 0, 0, i)),
                   pl.BlockSpec((None, A_HEADS, None, SUBLANES, nb), lambda b, i: (b, 0, i, 0, 0)),
                   pl.BlockSpec((None, A_HEADS, t, HEAD_DIM), lambda b, i: (b, 0, i, 0))],
        out_shape=[jax.ShapeDtypeStruct((bsz, A_HEADS, SUBLANES, seq), jnp.int32),
                   jax.ShapeDtypeStruct((bsz, A_HEADS, nb, SUBLANES, nb), F32),
                   jax.ShapeDtypeStruct((bsz, A_HEADS, seq, HEAD_DIM), F32)],
        compiler_params=_cparams("parallel", "parallel"),
        name="moba_gate",
    )(qk, kmean)


def _moba_routes(ids, cnt, seq):
    bsz, heads = ids.shape[:2]
    bh, t = bsz * heads, MOBA_BLOCK
    nb = seq // t
    tiles = _moba_tiles(seq)
    picks = ids[:, :, 0:MOBA_TOPK, :].reshape(bh, MOBA_TOPK, nb, t)
    ranks = ids[:, :, MOBA_TOPK:2 * MOBA_TOPK, :].reshape(bh, MOBA_TOPK, nb, t)
    per_tile = cnt[:, :, :, 0:MOBA_TOPK, :].astype(jnp.int32).reshape(bh, nb * MOBA_TOPK, nb)
    before = jnp.cumsum(per_tile, axis=1) - per_tile
    total = jnp.sum(per_tile, axis=1)
    padded = -(-total // t) * t
    ends = jnp.cumsum(padded, axis=1)
    base = before + (ends - padded)[:, None, :]
    base = base.reshape(bh, nb, MOBA_TOPK, nb).transpose(0, 2, 1, 3)
    onehot = picks[..., None] == jnp.arange(nb)
    pos = jnp.sum(jnp.where(onehot, base[:, :, :, None, :], 0), axis=-1) + ranks
    pos = jnp.where(picks < nb, pos, (tiles - 1) * t)
    pos = pos + (jnp.arange(bh, dtype=jnp.int32) * (tiles * t))[:, None, None, None]
    pos = pos.reshape(bh, MOBA_TOPK, seq).transpose(1, 0, 2).reshape(MOBA_TOPK, bh * seq)
    tile_start = jnp.arange(tiles, dtype=jnp.int32) * t
    tile_blk = jnp.sum(tile_start[None, :, None] >= ends[:, None, :], axis=-1)
    tile_blk = jnp.where(tile_start[None, :] < ends[:, -1:], tile_blk, -1)
    return pos.astype(jnp.int32), tile_blk.astype(jnp.int32)


def _moba_tiles(seq):
    nb = seq // MOBA_BLOCK
    return -(-(MOBA_TOPK * nb + nb + 1) // GROUP_STEP) * GROUP_STEP


def _sc_mesh():
    return plsc.VectorSubcoreMesh(core_axis_name="core", subcore_axis_name="subcore")


def _sc_scatter_rows(x, idx, rows):
    slots, n = idx.shape
    d = x.shape[1]

    @pl.kernel(out_type=jax.ShapeDtypeStruct((rows, d), x.dtype), mesh=_sc_mesh())
    def scatter(x_hbm, i_hbm, o_hbm):
        def body(x_vmem, i_vmem):
            pltpu.sync_copy(x_vmem, o_hbm.at[i_vmem.at[0]])

        pltpu.emit_pipeline(
            body, grid=(slots, n // SC_WINDOW),
            in_specs=[pl.BlockSpec((SC_WINDOW, d), lambda s, i: (i, 0)),
                      pl.BlockSpec((1, SC_WINDOW), lambda s, i: (s, i))],
            out_specs=[],
            core_axis_name=("core", "subcore"),
            dimension_semantics=(pltpu.PARALLEL, pltpu.PARALLEL),
        )(x_hbm, i_hbm)

    return scatter(x, idx)


def _sc_gather_rows(x, idx):
    n = idx.shape[0]
    d = x.shape[1]

    @pl.kernel(out_type=jax.ShapeDtypeStruct((n, d), x.dtype), mesh=_sc_mesh())
    def gather(x_hbm, i_hbm, o_hbm):
        def body(i_vmem, o_vmem):
            pltpu.sync_copy(x_hbm.at[i_vmem.at[0]], o_vmem)

        pltpu.emit_pipeline(
            body, grid=(n // SC_WINDOW,),
            in_specs=[pl.BlockSpec((1, SC_WINDOW), lambda i: (0, i))],
            out_specs=[pl.BlockSpec((SC_WINDOW, d), lambda i: (i, 0))],
            core_axis_name=("core", "subcore"),
            dimension_semantics=(pltpu.PARALLEL,),
        )(i_hbm, o_hbm)

    return gather(x, idx.reshape(1, n))


def _moba_group_kernel(tb_ref, q_ref, *refs):
    t = MOBA_BLOCK
    k_refs, v_refs = refs[:GROUP_STEP], refs[GROUP_STEP:2 * GROUP_STEP]
    o_ref, lse_ref = refs[2 * GROUP_STEP:]
    g, step = pl.program_id(0), pl.program_id(1)
    first = step * GROUP_STEP

    @pl.when(tb_ref[g, first] < 0)
    def _():
        o_ref[...] = jnp.zeros(o_ref.shape, o_ref.dtype)
        lse_ref[...] = jnp.full(lse_ref.shape, MASKED, lse_ref.dtype)

    @pl.when(tb_ref[g, first] >= 0)
    def _():
        for u in range(GROUP_STEP):
            used = tb_ref[g, first + u] >= 0
            rows = slice(u * t, (u + 1) * t)
            o, lse = _block_attention(q_ref[rows, :].astype(BF16), k_refs[u][...], v_refs[u][...])
            o_ref[rows, :] = jnp.where(used, o, 0.0)
            lse_ref[rows, :] = jnp.where(used, lse, MASKED)


def _moba_group_attention(q_grouped, tile_blk, qk, v):
    bh, rows, _ = q_grouped.shape
    t = MOBA_BLOCK
    tiles = rows // t
    heads = A_HEADS

    def block_of(u, first_col):
        return lambda g, s, tb: (g // heads, jnp.maximum(tb[g, s * GROUP_STEP + u], 0),
                                 first_col + g % heads)

    row_tile = pl.BlockSpec((None, GROUP_STEP * t, HEAD_DIM), lambda g, s, tb: (g, s, 0))
    key_value = lambda first_col: [pl.BlockSpec((None, t, HEAD_DIM), block_of(u, first_col))
                                   for u in range(GROUP_STEP)]
    grid_spec = pltpu.PrefetchScalarGridSpec(
        num_scalar_prefetch=1,
        grid=(bh, tiles // GROUP_STEP),
        in_specs=[row_tile] + key_value(KA_BLK) + key_value(0),
        out_specs=[row_tile, row_tile],
    )
    return pl.pallas_call(
        _moba_group_kernel,
        grid_spec=grid_spec,
        out_shape=[jax.ShapeDtypeStruct(q_grouped.shape, F32)] * 2,
        compiler_params=_cparams("parallel", "parallel"),
        name="moba_group",
    )(tile_blk, q_grouped, *([qk] * GROUP_STEP), *([v] * GROUP_STEP))


def _moba_merge_kernel(q_ref, k_ref, v_ref, po_ref, pl_ref, o_ref):
    t = MOBA_BLOCK
    causal = (lax.broadcasted_iota(jnp.int32, (t, t), 1)
              <= lax.broadcasted_iota(jnp.int32, (t, t), 0))
    for hd in range(A_HEADS):
        sl = slice(hd * HEAD_DIM, (hd + 1) * HEAD_DIM)
        o_own, lse_own = _block_attention(q_ref[:, sl], k_ref[:, sl], v_ref[:, sl], causal)
        outs = [o_own] + [po_ref[s, hd] for s in range(MOBA_TOPK)]
        lses = [lse_own] + [pl_ref[s, hd] for s in range(MOBA_TOPK)]
        top = functools.reduce(jnp.maximum, lses)
        weights = [jnp.exp2(l - top) for l in lses]
        num = sum(w * o for w, o in zip(weights, outs))
        o_ref[:, sl] = (num / sum(weights)).astype(o_ref.dtype)


def _moba_merge(qk, v, part_o, part_lse):
    bsz, seq, _ = qk.shape
    t = MOBA_BLOCK
    part = pl.BlockSpec((MOBA_TOPK, None, A_HEADS, t, HEAD_DIM), lambda b, i: (0, b, 0, i, 0))
    return pl.pallas_call(
        _moba_merge_kernel,
        grid=(bsz, seq // t),
        in_specs=[pl.BlockSpec((None, t, A_W), lambda b, i: (b, i, QA_BLK * LANES // A_W)),
                  pl.BlockSpec((None, t, A_W), lambda b, i: (b, i, KA_BLK * LANES // A_W)),
                  pl.BlockSpec((None, t, A_W), lambda b, i: (b, i, 0)),
                  part, part],
        out_specs=pl.BlockSpec((None, t, A_W), lambda b, i: (b, i, 0)),
        out_shape=jax.ShapeDtypeStruct((bsz, seq, A_W), BF16),
        compiler_params=_cparams("parallel", "parallel"),
        name="moba_merge",
    )(qk, qk, v, part_o, part_lse)


def _moba_attention(qk, v, kmean):
    bsz, seq, _ = qk.shape
    bh = bsz * A_HEADS
    rows = _moba_tiles(seq) * MOBA_BLOCK
    ids, cnt, q_f32 = _moba_gate(qk, kmean)
    pos, tile_blk = _moba_routes(ids, cnt, seq)
    q_grouped = _sc_scatter_rows(q_f32.reshape(bh * seq, HEAD_DIM), pos, bh * rows)
    o_g, lse_g = _moba_group_attention(q_grouped.reshape(bh, rows, HEAD_DIM), tile_blk, qk, v)
    flat = pos.reshape(-1)
    back = lambda a: _sc_gather_rows(a.reshape(bh * rows, HEAD_DIM), flat).reshape(
        MOBA_TOPK, bsz, A_HEADS, seq, HEAD_DIM)
    return _moba_merge(qk, v, back(o_g), back(lse_g))


def _proj_dilated_kernel(h_ref, w_ref, c_ref, s_ref, q_ref, k_ref, v_ref, sc, *, d):
    acc = _dot(h_ref[...], w_ref[...])
    c, s = c_ref[...], s_ref[...]
    q_scale = HEAD_DIM ** -0.5 * LOG2E
    for j in range(acc.shape[1] // LANES):
        blk = acc[:, j * LANES:(j + 1) * LANES]
        if j < B_HEADS:
            blk = _rope128(blk, c, s) * q_scale
        elif j < 2 * B_HEADS:
            blk = _rope128(blk, c, s)
        sc[j] = blk
    rows = acc.shape[0] // d
    for r in range(d):
        for j in range(acc.shape[1] // LANES):
            dst = (q_ref, k_ref, v_ref)[j // B_HEADS]
            col = (j % B_HEADS) * LANES
            dst[r, :, col:col + LANES] = sc[j, pl.ds(r, rows, stride=d), :].astype(dst.dtype)


def _proj_dilated(h, w, rope_h, d, bsz, seq, tm=512):
    m, k = h.shape
    nt = seq // tm
    pos = pl.BlockSpec((tm, LANES), lambda i: (i % nt, 0))
    out = pl.BlockSpec((None, d, tm // d, B_W), lambda i: (i // nt, 0, i % nt, 0))
    return pl.pallas_call(
        functools.partial(_proj_dilated_kernel, d=d),
        grid=(m // tm,),
        in_specs=[pl.BlockSpec((tm, k), lambda i: (i, 0)), pl.BlockSpec(w.shape, lambda i: (0, 0)),
                  pos, pos],
        out_specs=[out] * 3,
        out_shape=[jax.ShapeDtypeStruct((bsz, d, seq // d, B_W), BF16)] * 3,
        scratch_shapes=[pltpu.VMEM((w.shape[1] // LANES, tm, LANES), F32)],
        compiler_params=_cparams("parallel"),
        name=f"proj_dilated_d{d}",
    )(h, w, *rope_h)


def _dilated_kernel(q_ref, kc_ref, kp_ref, vc_ref, vp_ref, o_ref, lse_ref, *, span):
    t, tp = q_ref.shape[0], kp_ref.shape[0]
    i = pl.program_id(2)
    shape = (2 * tp, tp)
    key_row = lax.broadcasted_iota(jnp.int32, shape, 0)
    dist = lax.broadcasted_iota(jnp.int32, shape, 1) + tp - key_row
    visible = (dist >= 0) & (dist <= span)
    bias = jnp.where(visible, 0.0, MASKED)
    bias_first = jnp.where(visible & ((key_row >= tp) | (i > 0)), 0.0, MASKED)
    ones = jnp.ones((BF16_ROWS, tp + t), BF16)

    def transposed(x):
        return x.astype(F32).T.astype(BF16)

    for j in range(B_HEADS):
        sl = slice(j * LANES, (j + 1) * LANES)
        qT = transposed(q_ref[:, sl])
        k_all = jnp.concatenate([kp_ref[:, sl], kc_ref[:, sl]], axis=0)
        vt_all = jnp.concatenate([transposed(vp_ref[:, sl]), transposed(vc_ref[:, sl])], axis=1)
        vt_all = jnp.concatenate([vt_all, ones], axis=0)
        outs, lses = [], []
        for u in range(t // tp):
            window = slice(u * tp, (u + 2) * tp)
            s = _dot(k_all[window, :], qT[:, u * tp:(u + 1) * tp])
            s = s + (bias_first if u == 0 else bias)
            m = jnp.max(s, axis=0, keepdims=True)
            p = jnp.exp2(s - m).astype(BF16)
            acc = _dot(vt_all[:, window], p)
            den = acc[V_DIM:V_DIM + 1, :]
            outs.append(acc[:V_DIM, :] / den)
            lses.append(m + jnp.log2(den))
        o_ref[:, sl] = jnp.concatenate(outs, axis=1).T
        lse = jnp.concatenate(lses, axis=1)
        lse_ref[:, sl] = jnp.broadcast_to(lse, (LANES, t)).T


def _dilated_attention(q, k, v, span, t=1024):
    bsz, d, length, _ = q.shape
    t = min(t, length)
    tp = B_QBLOCK
    assert span <= tp and t % tp == 0
    cur = pl.BlockSpec((None, None, t, B_W), lambda b, r, i: (b, r, i, 0))
    prev = pl.BlockSpec((None, None, tp, B_W),
                        lambda b, r, i: (b, r, jnp.maximum(i * (t // tp) - 1, 0), 0))
    return pl.pallas_call(
        functools.partial(_dilated_kernel, span=span),
        grid=(bsz, d, length // t),
        in_specs=[cur, cur, prev, cur, prev],
        out_specs=[cur, cur],
        out_shape=[jax.ShapeDtypeStruct(q.shape, F32)] * 2,
        compiler_params=_cparams("parallel", "parallel", "parallel"),
        name=f"dilated_d{d}",
    )(q, k, k, v, v)


def _natural_rows(ref, sc):
    d, rows = ref.shape[0], ref.shape[1]
    if d == 1:
        return ref[0]
    for r in range(d):
        for j in range(B_HEADS):
            sc[j, pl.ds(r, rows, stride=d), :] = ref[r, :, j * LANES:(j + 1) * LANES]
    return jnp.concatenate([sc[j] for j in range(B_HEADS)], axis=-1)


def _mixer_tail_kernel(x_ref, oa_ref, o0_ref, o1_ref, o2_ref, l0_ref, l1_ref, l2_ref, oc_ref,
                       g_ref, wpa_ref, wpb_ref, wpc_ref, wo_ref, y_ref, *scratch):
    o0, o1, o2, l0, l1, l2 = [
        _natural_rows(ref, sc)
        for ref, sc in zip((o0_ref, o1_ref, o2_ref, l0_ref, l1_ref, l2_ref), scratch)]
    mx = jnp.maximum(jnp.maximum(l0, l1), l2)
    e0, e1, e2 = jnp.exp2(l0 - mx), jnp.exp2(l1 - mx), jnp.exp2(l2 - mx)
    ob = (e0 * o0 + e1 * o1 + e2 * o2) / (e0 + e1 + e2)
    pa = _dot(oa_ref[...], wpa_ref[...])
    pb = _dot(ob.astype(BF16), wpb_ref[...])
    pc = _dot(oc_ref[...], wpc_ref[...])
    d = D_MODEL
    merged = (g_ref[:, 0:d].astype(F32) * pa + g_ref[:, d:2 * d].astype(F32) * pb
              + g_ref[:, 2 * d:3 * d].astype(F32) * pc)
    y_ref[...] = x_ref[...] + _dot(merged.astype(BF16), wo_ref[...])


def _mixer_tail(x, out_a, o_groups, lse_groups, out_c, gates, w_pa, w_pb, w_pc, w_o, seq, tm=256):
    m, d = x.shape
    nt = seq // tm
    row = lambda width: pl.BlockSpec((tm, width), lambda i: (i, 0))
    residue = lambda g: pl.BlockSpec((None, g.shape[1], tm // g.shape[1], B_W),
                                     lambda i: (i // nt, 0, i % nt, 0))
    weights = [_resident(w) for w in (w_pa, w_pb, w_pc, w_o)]
    groups = list(o_groups) + list(lse_groups)
    return pl.pallas_call(
        _mixer_tail_kernel,
        grid=(m // tm,),
        in_specs=([row(d), row(A_W)] + [residue(g) for g in groups]
                  + [row(C_W), row(3 * d)] + weights),
        out_specs=row(d),
        out_shape=jax.ShapeDtypeStruct((m, d), F32),
        scratch_shapes=[pltpu.VMEM((B_HEADS, tm, LANES), F32) for _ in groups],
        compiler_params=_cparams("parallel"),
        name="mixer_tail",
    )(x, out_a, *groups, out_c, gates, w_pa, w_pb, w_pc, w_o)


def _mem_kv_kernel(mem_ref, g_ref, wk_ref, wv_ref, k_ref, v_ref):
    memn = _rms(mem_ref[...], g_ref[...]).astype(BF16)
    k_ref[...] = _dot(memn, wk_ref[...]).astype(k_ref.dtype)
    v_ref[...] = _dot(memn, wv_ref[...]).astype(v_ref.dtype)


def _mem_kv(mem, g, wk, wv):
    bsz, n, d = mem.shape
    out = pl.BlockSpec((None, n, X_W), lambda b: (b, 0, 0))
    return pl.pallas_call(
        _mem_kv_kernel,
        grid=(bsz,),
        in_specs=[pl.BlockSpec((None, n, d), lambda b: (b, 0, 0)),
                  pl.BlockSpec((1, d), lambda b: (0, 0)),
                  pl.BlockSpec(wk.shape, lambda b: (0, 0)),
                  pl.BlockSpec(wv.shape, lambda b: (0, 0))],
        out_specs=[out, out],
        out_shape=[jax.ShapeDtypeStruct((bsz, n, X_W), BF16)] * 2,
        compiler_params=_cparams("parallel"),
        name="mem_kv",
    )(mem, g.reshape(1, d), wk, wv)


def _mem_attn_kernel(x_ref, g_ref, wq_ref, k_ref, v_ref, wo_ref, y_ref):
    x = x_ref[...]
    h = _rms(x, g_ref[...]).astype(BF16)
    q = (_dot(h, wq_ref[...]) * HEAD_DIM ** -0.5).astype(BF16)
    heads = []
    for hd in range(X_HEADS):
        sl = slice(hd * HEAD_DIM, (hd + 1) * HEAD_DIM)
        s = _dot_nt(q[:, sl], k_ref[:, sl])
        p = jnp.exp(s - jnp.max(s, axis=-1, keepdims=True))
        o = _dot(p.astype(BF16), v_ref[:, sl]) / jnp.sum(p, axis=-1, keepdims=True)
        heads.append(o.astype(BF16))
    y_ref[...] = x + _dot(jnp.concatenate(heads, axis=-1), wo_ref[...])


def _mem_attention(x, g, wq, kmem, vmem, wo, seq, tm=512):
    m, d = x.shape
    nt = seq // tm
    n = kmem.shape[1]
    kv = pl.BlockSpec((None, n, X_W), lambda i: (i // nt, 0, 0))
    return pl.pallas_call(
        _mem_attn_kernel,
        grid=(m // tm,),
        in_specs=[pl.BlockSpec((tm, d), lambda i: (i, 0)),
                  pl.BlockSpec((1, d), lambda i: (0, 0)),
                  pl.BlockSpec(wq.shape, lambda i: (0, 0)), kv, kv,
                  pl.BlockSpec(wo.shape, lambda i: (0, 0))],
        out_specs=pl.BlockSpec((tm, d), lambda i: (i, 0)),
        out_shape=jax.ShapeDtypeStruct((m, d), F32),
        compiler_params=_cparams("parallel"),
        name="mem_attention",
    )(x, g.reshape(1, d), wq, kmem, vmem, wo)


def _ffn_up_kernel(x_ref, halo_ref, g_ref, wg_ref, wv_ref, cwg_ref, cwv_ref, cbg_ref, cbv_ref,
                   act_ref, h_sc, *, tiles_per_seq):
    i = pl.program_id(0)
    tm = x_ref.shape[0]

    @pl.when(pl.program_id(1) == 0)
    def _():
        g = g_ref[...]
        keep = (i % tiles_per_seq != 0).astype(F32)
        h_sc[0:HALO, :] = (_rms(halo_ref[...], g) * keep).astype(h_sc.dtype)
        h_sc[HALO:, :] = _rms(x_ref[...], g).astype(h_sc.dtype)

    h = h_sc[...]

    def conv(w_ref, cw_ref, cb_ref):
        u = _dot(h, w_ref[...])
        c = cb_ref[...]
        for tap in range(CONV_W):
            lo = HALO - (CONV_W - 1) + tap
            c = c + cw_ref[tap:tap + 1, :] * u[lo:lo + tm, :]
        return c

    act = jax.nn.silu(conv(wg_ref, cwg_ref, cbg_ref)) * conv(wv_ref, cwv_ref, cbv_ref)
    act_ref[...] = act.astype(act_ref.dtype)


def _ffn_down_kernel(a_ref, w_ref, x_ref, y_ref):
    y_ref[...] = x_ref[...] + _dot(a_ref[...], w_ref[...])


def _conv_ffn(x, g, w_up, conv_w, conv_b, w_down, seq):
    m, d = x.shape
    act = _ffn_up(x, g, w_up, conv_w, conv_b, seq)
    tm, tn = 1024, FFN_TF
    return pl.pallas_call(
        _ffn_down_kernel,
        grid=(m // tm, d // tn),
        in_specs=[pl.BlockSpec((tm, D_FF_PAD), lambda i, j: (i, 0)),
                  pl.BlockSpec((D_FF_PAD, tn), lambda i, j: (0, j)),
                  pl.BlockSpec((tm, tn), lambda i, j: (i, j))],
        out_specs=pl.BlockSpec((tm, tn), lambda i, j: (i, j)),
        out_shape=jax.ShapeDtypeStruct((m, d), F32),
        compiler_params=_cparams("parallel", "parallel"),
        name="ffn_down",
    )(act, w_down, x)


def _ffn_up(x, g, w_up, conv_w, conv_b, seq, tm=1024, tf=FFN_TF):
    m, d = x.shape
    nf = D_FF_PAD // tf
    halo_blocks = tm // HALO
    return pl.pallas_call(
        functools.partial(_ffn_up_kernel, tiles_per_seq=seq // tm),
        grid=(m // tm, nf),
        in_specs=[pl.BlockSpec((tm, d), lambda i, f: (i, 0)),
                  pl.BlockSpec((HALO, d), lambda i, f: (jnp.maximum(i * halo_blocks - 1, 0), 0)),
                  pl.BlockSpec((1, d), lambda i, f: (0, 0)),
                  pl.BlockSpec((d, tf), lambda i, f: (0, f)),
                  pl.BlockSpec((d, tf), lambda i, f: (0, f + nf)),
                  pl.BlockSpec((CONV_W, tf), lambda i, f: (0, f)),
                  pl.BlockSpec((CONV_W, tf), lambda i, f: (0, f + nf)),
                  pl.BlockSpec((1, tf), lambda i, f: (0, f)),
                  pl.BlockSpec((1, tf), lambda i, f: (0, f + nf))],
        out_specs=pl.BlockSpec((tm, tf), lambda i, f: (i, f)),
        out_shape=jax.ShapeDtypeStruct((m, D_FF_PAD), BF16),
        scratch_shapes=[pltpu.VMEM((HALO + tm, d), BF16)],
        compiler_params=_cparams("parallel", "arbitrary"),
        name="ffn_up",
    )(x, x, g.reshape(1, d), w_up, w_up, conv_w, conv_w, conv_b, conv_b)


def _rope_tables(seq):
    def angles(dim):
        inv_freq = jnp.exp(jnp.arange(0, dim, 2, dtype=F32) * (-math.log(ROPE_THETA) / dim))
        ang = jnp.arange(seq, dtype=F32)[:, None] * inv_freq[None, :]
        return jnp.cos(ang), jnp.sin(ang)

    cos_h, sin_h = angles(HEAD_DIM)
    rope_h = (jnp.concatenate([cos_h, cos_h], axis=-1), jnp.concatenate([-sin_h, sin_h], axis=-1))
    cos_r, sin_r = angles(ROPE_DIM)
    z = jnp.zeros_like(cos_r)
    rope_r = (jnp.concatenate([cos_r, cos_r, z, z], axis=-1),
              jnp.concatenate([-sin_r, z, z, z], axis=-1),
              jnp.concatenate([z, sin_r, z, z], axis=-1))
    return rope_h, rope_r


def _split_in(w_in):
    return [w_in[:, IN_OFFSETS[k]:IN_OFFSETS[k + 1]] for k in range(len(IN_WIDTHS))]


def _pad_cols(w, width):
    return jnp.pad(w, ((0, 0), (0, width - w.shape[1])))


def _layer_params(w_in, w_uq, w_ukv, w_up, conv_w, conv_b, w_down):
    qa, ka, va, qb, kb, vb, cq, ckv, kr, gates = _split_in(w_in)
    w_qk = jnp.concatenate([qa, ka], axis=1).astype(BF16)
    group_cols = lambda w, g: w[:, g * B_W:(g + 1) * B_W]
    w_b = [jnp.concatenate([group_cols(qb, g), group_cols(kb, g), group_cols(vb, g)],
                           axis=1).astype(BF16) for g in range(len(B_GROUPS))]
    w_down_in = jnp.concatenate([cq, ckv, _pad_cols(kr, LANES)], axis=1).astype(BF16)
    uq = w_uq.reshape(Q_LORA, C_HEADS, NOPE_DIM + ROPE_DIM)
    uq = jnp.pad(uq, ((0, 0), (0, 0), (0, C_QK - NOPE_DIM - ROPE_DIM)))
    ukv = w_ukv.reshape(KV_LORA, C_HEADS, NOPE_DIM + V_DIM)
    pad_ff = lambda w: jnp.pad(w, ((0, 0), (0, D_FF_PAD - D_FF)))
    two_halves = lambda w: jnp.concatenate([pad_ff(w[:, :D_FF]), pad_ff(w[:, D_FF:])], axis=1)
    return dict(
        w_qk=w_qk, w_va=va.astype(BF16), w_b=w_b, w_gates=gates.astype(BF16),
        w_down_in=w_down_in,
        w_uq=uq.reshape(Q_LORA, C_HEADS * C_QK).astype(BF16),
        w_uk=ukv[:, :, :NOPE_DIM].reshape(KV_LORA, C_HEADS * NOPE_DIM).astype(BF16),
        w_uv=ukv[:, :, NOPE_DIM:].reshape(KV_LORA, C_W).astype(BF16),
        w_up=two_halves(w_up).astype(BF16),
        conv_w=two_halves(conv_w),
        conv_b=two_halves(conv_b.reshape(1, -1)),
        w_down=jnp.pad(w_down, ((0, D_FF_PAD - D_FF), (0, 0))).astype(BF16),
    )


def _qk_col_scale():
    q_scale = HEAD_DIM ** -0.5
    parts = [jnp.full((A_W,), q_scale * LOG2E, F32), jnp.ones((A_W,), F32)]
    return jnp.concatenate(parts).reshape(1, QK_W)


def _mixer(x, g_mix, p, g_cq, g_ckv, w_pa, w_pb, w_pc, w_o, rope_h, rope_r, bsz, seq):
    m = x.shape[0]
    h = _rmsnorm(x, g_mix, BF16)
    qk = _matmul(h, p["w_qk"], _mm_rope_kernel, BF16, 1024, 1024, seq=seq,
                 extras=(("col", _qk_col_scale()), ("pos", rope_h[0]), ("pos", rope_h[1])),
                 name="proj_qk_rope")
    v_a = _matmul(h, p["w_va"], _mm_plain_kernel, BF16, 1024, A_W, name="proj_va")
    gates = _matmul(h, p["w_gates"], _mm_sigmoid_kernel, BF16, 1024, 1024, name="proj_gates")
    cq, ckv, kr = _mla_down(h, p["w_down_in"], g_cq, g_ckv, rope_r, seq)
    q_c = _mla_q(cq, p["w_uq"], rope_r, seq)
    k_c, vt_c = _mla_kv(ckv, kr, p["w_uk"], p["w_uv"], bsz, seq)

    qk3 = qk.reshape(bsz, seq, QK_W)
    out_a = _moba_attention(qk3, v_a.reshape(bsz, seq, A_W), _kmean(qk3)).reshape(m, A_W)
    groups = []
    for (window, d), w_g in zip(B_GROUPS, p["w_b"]):
        q_g, k_g, v_g = _proj_dilated(h, w_g, rope_h, d, bsz, seq)
        groups.append(_dilated_attention(q_g, k_g, v_g, window // d))
    out_c = _flash_attention(q_c.reshape(bsz, seq, -1), k_c.reshape(bsz, seq, -1), vt_c,
                             C_HEADS, C_QK).reshape(m, C_W)
    return _mixer_tail(x, out_a, [g[0] for g in groups], [g[1] for g in groups], out_c, gates,
                       w_pa.astype(BF16), w_pb.astype(BF16), w_pc.astype(BF16), w_o.astype(BF16),
                       seq)


def kernel(x, mem, g_mix, w_in, g_cq, g_ckv, w_uq, w_ukv, w_pa, w_pb, w_pc, w_o, g_mem, g_memkv,
           w_xq, w_xk, w_xv, w_xo, g_ffn, w_up, conv_w, conv_b, w_down, g_final):
    bsz, seq, d = x.shape
    rope_h, rope_r = _rope_tables(seq)
    xf = x.reshape(bsz * seq, d)
    for l in range(DEPTH):
        p = _layer_params(w_in[l], w_uq[l], w_ukv[l], w_up[l], conv_w[l], conv_b[l], w_down[l])
        xf = _mixer(xf, g_mix[l], p, g_cq[l], g_ckv[l], w_pa[l], w_pb[l], w_pc[l], w_o[l],
                    rope_h, rope_r, bsz, seq)
        kmem, vmem = _mem_kv(mem, g_memkv[l], w_xk[l].astype(BF16), w_xv[l].astype(BF16))
        xf = _mem_attention(xf, g_mem[l], w_xq[l].astype(BF16), kmem, vmem,
                            w_xo[l].astype(BF16), seq)
        xf = _conv_ffn(xf, g_ffn[l], p["w_up"], p["conv_w"], p["conv_b"], p["w_down"], seq)
    return _rmsnorm(xf, g_final, F32).reshape(bsz, seq, d)
```

```python
import functools
import math

import jax
import jax.numpy as jnp
import numpy as np
from jax import lax
from jax.experimental import pallas as pl
from jax.experimental.pallas import tpu as pltpu
from jax.experimental.pallas import tpu_sc as plsc

F32 = jnp.float32
BF16 = jnp.bfloat16

LANES = 128
SUBLANES = 8
VMEM_LIMIT = 56 * 1024 * 1024

D_MODEL = 2048
DEPTH = 2
HEAD_DIM = 128
ROPE_THETA = 10000.0
EPS = 1e-6

A_HEADS = 4
MOBA_BLOCK = 256
MOBA_TOPK = 3

B_GROUPS = ((128, 1), (512, 4), (2048, 16))
B_HEADS = 4
B_QBLOCK = 128

C_HEADS = 8
Q_LORA = 1536
KV_LORA = 512
NOPE_DIM = 128
ROPE_DIM = 64
V_DIM = 128

X_HEADS = 4
D_FF = 5504
CONV_W = 3

A_W = A_HEADS * HEAD_DIM
B_QKV_W = len(B_GROUPS) * B_HEADS * HEAD_DIM
B_W = B_HEADS * HEAD_DIM
C_W = C_HEADS * V_DIM
X_W = X_HEADS * HEAD_DIM
IN_WIDTHS = (A_W, A_W, A_W, B_QKV_W, B_QKV_W, B_QKV_W, Q_LORA, KV_LORA, ROPE_DIM, 3 * D_MODEL)
IN_OFFSETS = tuple(int(o) for o in np.cumsum((0,) + IN_WIDTHS))

QK_W = 2 * A_W
QA_BLK, KA_BLK = 0, A_W // LANES

C_QK = 2 * LANES
MASKED = -1e30
LOG2E = math.log2(math.e)
BF16_ROWS = 16
VT_ROWS = V_DIM + BF16_ROWS
GROUP_STEP = 8
SC_WINDOW = 128

D_FF_PAD = 5632
FFN_TF = 512
HALO = SUBLANES


def _cparams(*sem):
    return pltpu.CompilerParams(dimension_semantics=sem, vmem_limit_bytes=VMEM_LIMIT)


def _resident(arr):
    zeros = (0,) * arr.ndim
    return pl.BlockSpec(arr.shape, lambda *_: zeros, pipeline_mode=pl.Buffered(1))


def _dot(a, b):
    return jnp.dot(a, b, preferred_element_type=F32)


def _dot_nt(a, b):
    return lax.dot_general(a, b, (((1,), (1,)), ((), ())), preferred_element_type=F32)


def _rms(x, g):
    return x * lax.rsqrt(jnp.mean(x * x, axis=-1, keepdims=True) + EPS) * g


def _rmsnorm_kernel(x_ref, g_ref, o_ref):
    o_ref[...] = _rms(x_ref[...], g_ref[...]).astype(o_ref.dtype)


def _rmsnorm(x, g, out_dtype, tm=512):
    m, d = x.shape
    return pl.pallas_call(
        _rmsnorm_kernel,
        grid=(m // tm,),
        in_specs=[pl.BlockSpec((tm, d), lambda i: (i, 0)),
                  pl.BlockSpec((1, d), lambda i: (0, 0))],
        out_specs=pl.BlockSpec((tm, d), lambda i: (i, 0)),
        out_shape=jax.ShapeDtypeStruct((m, d), out_dtype),
        compiler_params=_cparams("parallel"),
        name="rmsnorm",
    )(x, g.reshape(1, d))


def _rope128(x, c, s):
    return x * c + pltpu.roll(x, HEAD_DIM // 2, 1) * s


def _rope64(x, c, sa, sb):
    half = ROPE_DIM // 2
    return x * c + pltpu.roll(x, LANES - half, 1) * sa + pltpu.roll(x, half, 1) * sb


def _mm_plain_kernel(a_ref, w_ref, o_ref):
    o_ref[...] = _dot(a_ref[...], w_ref[...]).astype(o_ref.dtype)


def _mm_sigmoid_kernel(a_ref, w_ref, o_ref):
    o_ref[...] = jax.nn.sigmoid(_dot(a_ref[...], w_ref[...])).astype(o_ref.dtype)


def _mm_rope_kernel(a_ref, w_ref, cs_ref, c_ref, s_ref, o_ref):
    acc = _dot(a_ref[...], w_ref[...])
    c = c_ref[...]
    s = s_ref[...]
    for j in range(acc.shape[1] // LANES):
        sl = slice(j * LANES, (j + 1) * LANES)
        o_ref[:, sl] = (_rope128(acc[:, sl], c, s) * cs_ref[:, sl]).astype(o_ref.dtype)


def _matmul(a, w, kernel, out_dtype, tm, tn, seq=None, extras=(), name="matmul"):
    m, k = a.shape
    n = w.shape[1]
    in_specs = [pl.BlockSpec((tm, k), lambda i, j: (i, 0)),
                pl.BlockSpec((k, tn), lambda i, j: (0, j))]
    args = [a, w]
    for kind, arr in extras:
        if kind == "col":
            in_specs.append(pl.BlockSpec((1, tn), lambda i, j: (0, j)))
        else:
            nt = seq // tm
            in_specs.append(pl.BlockSpec((tm, LANES), lambda i, j: (i % nt, 0)))
        args.append(arr)
    return pl.pallas_call(
        kernel,
        grid=(m // tm, n // tn),
        in_specs=in_specs,
        out_specs=pl.BlockSpec((tm, tn), lambda i, j: (i, j)),
        out_shape=jax.ShapeDtypeStruct((m, n), out_dtype),
        compiler_params=_cparams("parallel", "parallel"),
        name=name,
    )(*args)


def _mla_down_kernel(h_ref, w_ref, gq_ref, gkv_ref, c_ref, sa_ref, sb_ref,
                     cq_ref, ckv_ref, kr_ref):
    acc = _dot(h_ref[...], w_ref[...])
    cq_ref[...] = _rms(acc[:, :Q_LORA], gq_ref[...]).astype(cq_ref.dtype)
    ckv_ref[...] = _rms(acc[:, Q_LORA:Q_LORA + KV_LORA], gkv_ref[...]).astype(ckv_ref.dtype)
    kr = acc[:, Q_LORA + KV_LORA:]
    kr_ref[...] = _rope64(kr, c_ref[...], sa_ref[...], sb_ref[...]).astype(kr_ref.dtype)


def _mla_down(h, w, g_cq, g_ckv, rope_r, seq, tm=512):
    m, k = h.shape
    n = w.shape[1]
    nt = seq // tm
    row = lambda width: pl.BlockSpec((tm, width), lambda i: (i, 0))
    full = lambda r, c: pl.BlockSpec((r, c), lambda i: (0, 0))
    pos = pl.BlockSpec((tm, LANES), lambda i: (i % nt, 0))
    return pl.pallas_call(
        _mla_down_kernel,
        grid=(m // tm,),
        in_specs=[row(k), full(k, n), full(1, Q_LORA), full(1, KV_LORA), pos, pos, pos],
        out_specs=[row(Q_LORA), row(KV_LORA), row(LANES)],
        out_shape=[jax.ShapeDtypeStruct((m, Q_LORA), BF16),
                   jax.ShapeDtypeStruct((m, KV_LORA), BF16),
                   jax.ShapeDtypeStruct((m, LANES), BF16)],
        compiler_params=_cparams("parallel"),
        name="mla_down",
    )(h, w, g_cq.reshape(1, -1), g_ckv.reshape(1, -1), *rope_r)


def _mla_q_kernel(cq_ref, w_ref, c_ref, sa_ref, sb_ref, q_ref, *, scale):
    acc = _dot(cq_ref[...], w_ref[...])
    c, sa, sb = c_ref[...], sa_ref[...], sb_ref[...]
    for hd in range(C_HEADS):
        lo = hd * C_QK
        q_ref[:, lo:lo + LANES] = (acc[:, lo:lo + LANES] * scale).astype(q_ref.dtype)
        rope = _rope64(acc[:, lo + LANES:lo + C_QK], c, sa, sb)
        q_ref[:, lo + LANES:lo + C_QK] = (rope * scale).astype(q_ref.dtype)


def _mla_q(cq, w, rope_r, seq, tm=512):
    m, k = cq.shape
    n = w.shape[1]
    nt = seq // tm
    pos = pl.BlockSpec((tm, LANES), lambda i: (i % nt, 0))
    return pl.pallas_call(
        functools.partial(_mla_q_kernel, scale=(NOPE_DIM + ROPE_DIM) ** -0.5 * LOG2E),
        grid=(m // tm,),
        in_specs=[pl.BlockSpec((tm, k), lambda i: (i, 0)),
                  pl.BlockSpec((k, n), lambda i: (0, 0)), pos, pos, pos],
        out_specs=pl.BlockSpec((tm, n), lambda i: (i, 0)),
        out_shape=jax.ShapeDtypeStruct((m, n), BF16),
        compiler_params=_cparams("parallel"),
        name="mla_q",
    )(cq, w, *rope_r)


def _store_vt(v, vt_ref):
    vt = v.T
    for hd in range(vt_ref.shape[0]):
        vt_ref[hd, 0:V_DIM, :] = vt[hd * V_DIM:(hd + 1) * V_DIM, :].astype(vt_ref.dtype)
        vt_ref[hd, V_DIM:VT_ROWS, :] = jnp.ones((VT_ROWS - V_DIM, vt.shape[1]), vt_ref.dtype)


def _mla_kv_kernel(ckv_ref, kr_ref, wk_ref, wv_ref, k_ref, vt_ref):
    ckv = ckv_ref[...]
    kn = _dot(ckv, wk_ref[...])
    kr = kr_ref[...]
    for hd in range(C_HEADS):
        lo = hd * C_QK
        k_ref[:, lo:lo + LANES] = kn[:, hd * LANES:(hd + 1) * LANES].astype(k_ref.dtype)
        k_ref[:, lo + LANES:lo + C_QK] = kr
    _store_vt(_dot(ckv, wv_ref[...]), vt_ref)


def _mla_kv(ckv, kr, wk, wv, bsz, seq, tm=512):
    m, k = ckv.shape
    nt = seq // tm
    return pl.pallas_call(
        _mla_kv_kernel,
        grid=(m // tm,),
        in_specs=[pl.BlockSpec((tm, k), lambda i: (i, 0)),
                  pl.BlockSpec((tm, LANES), lambda i: (i, 0)),
                  pl.BlockSpec(wk.shape, lambda i: (0, 0)),
                  pl.BlockSpec(wv.shape, lambda i: (0, 0))],
        out_specs=[pl.BlockSpec((tm, C_HEADS * C_QK), lambda i: (i, 0)),
                   pl.BlockSpec((None, C_HEADS, VT_ROWS, tm), lambda i: (i // nt, 0, 0, i % nt))],
        out_shape=[jax.ShapeDtypeStruct((m, C_HEADS * C_QK), BF16),
                   jax.ShapeDtypeStruct((bsz, C_HEADS, VT_ROWS, seq), BF16)],
        compiler_params=_cparams("parallel"),
        name="mla_kv",
    )(ckv, kr, wk, wv)


def _attend_chunks(qT, k_ref, vt_ref, scratch, *, tk, n_full, mask_tail, tail_steps, tail_col,
                   unroll):
    m_sc, acc_sc, sa_sc, sb_sc, pa_sc, pb_sc, ala_sc, alb_sc, mxa_sc, mxb_sc = scratch
    s_bufs = (sa_sc, sb_sc)
    p_bufs = (pa_sc, pb_sc)
    al_bufs = (ala_sc, alb_sc)
    mx_bufs = (mxa_sc, mxb_sc)
    last_chunk = k_ref.shape[0] // tk - 1
    m_sc[...] = jnp.full(m_sc.shape, MASKED, F32)
    acc_sc[...] = jnp.zeros(acc_sc.shape, F32)
    for p_ref, al_ref in zip(p_bufs, al_bufs):
        p_ref[...] = jnp.zeros(p_ref.shape, p_ref.dtype)
        al_ref[...] = jnp.ones(al_ref.shape, F32)

    def rows(c):
        return pl.ds(pl.multiple_of(jnp.clip(c, 0, last_chunk) * tk, tk), tk)

    def scores(c, slot, col=0):
        sT = _dot(k_ref[rows(c), :], qT[:, col:])
        s_bufs[slot][:, col:] = sT
        mx_bufs[slot][:, col:] = jnp.max(sT, axis=0, keepdims=True)

    def flush(c, slot, col=0):
        acc_sc[:, col:] = (al_bufs[slot][:, col:] * acc_sc[:, col:]
                           + _dot(vt_ref[:, rows(c)], p_bufs[slot][:, col:]))

    def softmax(sT, top, slot, col):
        m_old = m_sc[:, col:]
        m_new = jnp.maximum(m_old, top)
        al_bufs[slot][:, col:] = jnp.exp2(m_old - m_new)
        p_bufs[slot][:, col:] = jnp.exp2(sT - m_new).astype(p_bufs[slot].dtype)
        m_sc[:, col:] = m_new

    def step(tau, slot, mask, cols):
        col_flush, col, col_next = cols
        flush(tau - 2, slot, col_flush)
        if col_next is not None:
            scores(tau + 1, 1 - slot, col_next)
        sT = s_bufs[slot][:, col:]
        if mask is None:
            softmax(sT, mx_bufs[slot][:, col:], slot, col)
        else:
            sT = mask(sT, tau, col)
            softmax(sT, jnp.max(sT, axis=0, keepdims=True), slot, col)

    def full_steps(tau0, count):
        for j in range(count):
            step(tau0 + j, j % 2, None, (0, 0, 0))

    scores(0, 0)
    trips = n_full // unroll
    lax.fori_loop(0, trips, lambda u, c: (full_steps(unroll * u, unroll), c)[1], 0)
    done = unroll * trips
    pairs = (n_full - done) // 2
    lax.fori_loop(0, pairs, lambda u, c: (full_steps(done + 2 * u, 2), c)[1], 0)
    tau = done + 2 * pairs
    cols = [tail_col(j) for j in range(tail_steps)]
    for j in range(tail_steps):
        col_flush = cols[j - 2] if j >= 2 else 0
        col_next = cols[j + 1] if j + 1 < tail_steps else None
        step(tau + j, j % 2, mask_tail, (col_flush, cols[j], col_next))
    flush(tau + tail_steps - 2, 0, cols[-2])
    flush(tau + tail_steps - 1, 1, cols[-1])
    acc = acc_sc[...]
    return acc[:V_DIM, :] / acc[V_DIM:V_DIM + 1, :]


def _attend_scratch(tq, tk):
    return [pltpu.VMEM((1, tq), F32), pltpu.VMEM((VT_ROWS, tq), F32),
            pltpu.VMEM((tk, tq), F32), pltpu.VMEM((tk, tq), F32),
            pltpu.VMEM((tk, tq), BF16), pltpu.VMEM((tk, tq), BF16),
            pltpu.VMEM((1, tq), F32), pltpu.VMEM((1, tq), F32),
            pltpu.VMEM((1, tq), F32), pltpu.VMEM((1, tq), F32)]


def _transpose_q(q_ref):
    return q_ref[...].astype(F32).T.astype(BF16)


def _flash_kernel(q_ref, k_ref, vt_ref, o_ref, *scratch, tq, tk):
    i = pl.program_id(2)

    def causal(sT, c, col):
        key = lax.broadcasted_iota(jnp.int32, sT.shape, 0) + c * tk
        qry = lax.broadcasted_iota(jnp.int32, sT.shape, 1) + (i * tq + col)
        return jnp.where(key <= qry, sT, MASKED)

    per_tile = tq // tk
    oT = _attend_chunks(_transpose_q(q_ref), k_ref, vt_ref, scratch, tk=tk, n_full=i * per_tile,
                        mask_tail=causal, tail_steps=per_tile, tail_col=lambda j: j * tk,
                        unroll=8)
    o_ref[...] = oT.T.astype(o_ref.dtype)


def _flash_attention(q, k, vt, heads, qk_w, tq=1024, tk=256):
    bsz, seq, _ = q.shape
    assert tq % (2 * tk) == 0 and seq % tq == 0
    return pl.pallas_call(
        functools.partial(_flash_kernel, tq=tq, tk=tk),
        grid=(bsz, heads, seq // tq),
        in_specs=[pl.BlockSpec((None, tq, qk_w), lambda b, h, i: (b, i, h)),
                  pl.BlockSpec((None, seq, qk_w), lambda b, h, i: (b, 0, h)),
                  pl.BlockSpec((None, None, VT_ROWS, seq), lambda b, h, i: (b, h, 0, 0))],
        out_specs=pl.BlockSpec((None, tq, V_DIM), lambda b, h, i: (b, i, h)),
        out_shape=jax.ShapeDtypeStruct((bsz, seq, heads * V_DIM), BF16),
        scratch_shapes=_attend_scratch(tq, tk),
        compiler_params=_cparams("parallel", "parallel", "arbitrary"),
        name="mla_flash",
    )(q, k, vt)


def _kmean_kernel(k_ref, o_ref):
    k = k_ref[...].astype(F32)
    o_ref[...] = jnp.mean(k.reshape(SUBLANES, MOBA_BLOCK, k.shape[-1]), axis=1)


def _kmean(qk):
    bsz, seq, _ = qk.shape
    rows = SUBLANES * MOBA_BLOCK
    return pl.pallas_call(
        _kmean_kernel,
        grid=(bsz, seq // rows),
        in_specs=[pl.BlockSpec((None, rows, A_W), lambda b, i: (b, i, KA_BLK * LANES // A_W))],
        out_specs=pl.BlockSpec((None, SUBLANES, A_W), lambda b, i: (b, i, 0)),
        out_shape=jax.ShapeDtypeStruct((bsz, seq // MOBA_BLOCK, A_W), F32),
        compiler_params=_cparams("parallel", "parallel"),
        name="moba_kmean",
    )(qk)


def _block_attention(q, k, v, visible=None):
    s = _dot_nt(q, k)
    if visible is not None:
        s = jnp.where(visible, s, MASKED)
    m = jnp.max(s, axis=-1, keepdims=True)
    p = jnp.exp2(s - m).astype(BF16)
    v_ones = jnp.concatenate([v, jnp.ones((v.shape[0], LANES), v.dtype)], axis=-1)
    acc = _dot(p, v_ones)
    den = acc[:, V_DIM:]
    return acc[:, :V_DIM] / den, m + jnp.log2(den)


def _moba_gate_kernel(q_ref, km_ref, ids_ref, cnt_ref, qf_ref):
    t = MOBA_BLOCK
    i = pl.program_id(1)
    nb = km_ref.shape[0]
    blk = lax.broadcasted_iota(jnp.int32, (nb, t), 0)
    neg_inf = jnp.float32(-jnp.inf)
    not_after = (lax.broadcasted_iota(jnp.int32, (t, t), 0)
                 <= lax.broadcasted_iota(jnp.int32, (t, t), 1))
    upper = jnp.where(not_after, 1.0, 0.0).astype(BF16)
    ones = jnp.ones((SUBLANES, t), BF16)
    for hd in range(A_HEADS):
        sl = slice(hd * HEAD_DIM, (hd + 1) * HEAD_DIM)
        q = q_ref[:, sl].astype(F32)
        qf_ref[hd] = q
        qT = q.T.astype(BF16)
        km = km_ref[:, sl]
        km_hi = km.astype(BF16)
        km_lo = (km - km_hi.astype(F32)).astype(BF16)
        g = jnp.where(blk < i, _dot(km_hi, qT) + _dot(km_lo, qT), neg_inf)
        picks, ranks, counts = [], [], []
        for _ in range(MOBA_TOPK):
            mx = jnp.max(g, axis=0, keepdims=True)
            is_max = (g == mx) & (mx > neg_inf)
            first = jnp.min(jnp.where(is_max, blk, nb), axis=0, keepdims=True)
            pick = blk == first
            g = jnp.where(pick, neg_inf, g)
            onehot = jnp.where(pick, 1.0, 0.0).astype(BF16)
            before = _dot(onehot, upper)
            rank = jnp.sum(jnp.where(pick, before - 1.0, 0.0), axis=0, keepdims=True)
            picks.append(first)
            ranks.append(rank.astype(jnp.int32))
            counts.append(_dot_nt(ones, onehot)[0:1, :])
        pad_i = jnp.zeros((SUBLANES - 2 * MOBA_TOPK, t), jnp.int32)
        ids_ref[hd] = jnp.concatenate(picks + ranks + [pad_i], axis=0)
        pad_f = jnp.zeros((SUBLANES - MOBA_TOPK, nb), F32)
        cnt_ref[hd] = jnp.concatenate(counts + [pad_f], axis=0)


def _moba_gate(qk, kmean):
    bsz, seq, _ = qk.shape
    t = MOBA_BLOCK
    nb = seq // t
    return pl.pallas_call(
        _moba_gate_kernel,
        grid=(bsz, nb),
        in_specs=[pl.BlockSpec((None, t, A_W), lambda b, i: (b, i, QA_BLK * LANES // A_W)),
                  pl.BlockSpec((None, nb, A_W), lambda b, i: (b, 0, 0))],
        out_specs=[pl.BlockSpec((None, A_HEADS, SUBLANES, t), lambda b, i: (b, 0, 0, i)),
                   pl.BlockSpec((None, A_HEADS, None, SUBLANES, nb), lambda b, i: (b, 0, i, 0, 0)),
                   pl.BlockSpec((None, A_HEADS, t, HEAD_DIM), lambda b, i: (b, 0, i, 0))],
        out_shape=[jax.ShapeDtypeStruct((bsz, A_HEADS, SUBLANES, seq), jnp.int32),
                   jax.ShapeDtypeStruct((bsz, A_HEADS, nb, SUBLANES, nb), F32),
                   jax.ShapeDtypeStruct((bsz, A_HEADS, seq, HEAD_DIM), F32)],
        compiler_params=_cparams("parallel", "parallel"),
        name="moba_gate",
    )(qk, kmean)


def _moba_routes(ids, cnt, seq):
    bsz, heads = ids.shape[:2]
    bh, t = bsz * heads, MOBA_BLOCK
    nb = seq // t
    tiles = _moba_tiles(seq)
    picks = ids[:, :, 0:MOBA_TOPK, :].reshape(bh, MOBA_TOPK, nb, t)
    ranks = ids[:, :, MOBA_TOPK:2 * MOBA_TOPK, :].reshape(bh, MOBA_TOPK, nb, t)
    per_tile = cnt[:, :, :, 0:MOBA_TOPK, :].astype(jnp.int32).reshape(bh, nb * MOBA_TOPK, nb)
    before = jnp.cumsum(per_tile, axis=1) - per_tile
    total = jnp.sum(per_tile, axis=1)
    padded = -(-total // t) * t
    ends = jnp.cumsum(padded, axis=1)
    base = before + (ends - padded)[:, None, :]
    base = base.reshape(bh, nb, MOBA_TOPK, nb).transpose(0, 2, 1, 3)
    onehot = picks[..., None] == jnp.arange(nb)
    pos = jnp.sum(jnp.where(onehot, base[:, :, :, None, :], 0), axis=-1) + ranks
    pos = jnp.where(picks < nb, pos, (tiles - 1) * t)
    pos = pos + (jnp.arange(bh, dtype=jnp.int32) * (tiles * t))[:, None, None, None]
    pos = pos.reshape(bh, MOBA_TOPK, seq).transpose(1, 0, 2).reshape(MOBA_TOPK, bh * seq)
    tile_start = jnp.arange(tiles, dtype=jnp.int32) * t
    tile_blk = jnp.sum(tile_start[None, :, None] >= ends[:, None, :], axis=-1)
    tile_blk = jnp.where(tile_start[None, :] < ends[:, -1:], tile_blk, -1)
    return pos.astype(jnp.int32), tile_blk.astype(jnp.int32)


def _moba_tiles(seq):
    nb = seq // MOBA_BLOCK
    return -(-(MOBA_TOPK * nb + nb + 1) // GROUP_STEP) * GROUP_STEP


def _sc_mesh():
    return plsc.VectorSubcoreMesh(core_axis_name="core", subcore_axis_name="subcore")


def _sc_scatter_rows(x, idx, rows):
    slots, n = idx.shape
    d = x.shape[1]

    @pl.kernel(out_type=jax.ShapeDtypeStruct((rows, d), x.dtype), mesh=_sc_mesh())
    def scatter(x_hbm, i_hbm, o_hbm):
        def body(x_vmem, i_vmem):
            pltpu.sync_copy(x_vmem, o_hbm.at[i_vmem.at[0]])

        pltpu.emit_pipeline(
            body, grid=(slots, n // SC_WINDOW),
            in_specs=[pl.BlockSpec((SC_WINDOW, d), lambda s, i: (i, 0)),
                      pl.BlockSpec((1, SC_WINDOW), lambda s, i: (s, i))],
            out_specs=[],
            core_axis_name=("core", "subcore"),
            dimension_semantics=(pltpu.PARALLEL, pltpu.PARALLEL),
        )(x_hbm, i_hbm)

    return scatter(x, idx)


def _sc_gather_rows(x, idx):
    n = idx.shape[0]
    d = x.shape[1]

    @pl.kernel(out_type=jax.ShapeDtypeStruct((n, d), x.dtype), mesh=_sc_mesh())
    def gather(x_hbm, i_hbm, o_hbm):
        def body(i_vmem, o_vmem):
            pltpu.sync_copy(x_hbm.at[i_vmem.at[0]], o_vmem)

        pltpu.emit_pipeline(
            body, grid=(n // SC_WINDOW,),
            in_specs=[pl.BlockSpec((1, SC_WINDOW), lambda i: (0, i))],
            out_specs=[pl.BlockSpec((SC_WINDOW, d), lambda i: (i, 0))],
            core_axis_name=("core", "subcore"),
            dimension_semantics=(pltpu.PARALLEL,),
        )(i_hbm, o_hbm)

    return gather(x, idx.reshape(1, n))


def _moba_group_kernel(tb_ref, q_ref, *refs):
    t = MOBA_BLOCK
    k_refs, v_refs = refs[:GROUP_STEP], refs[GROUP_STEP:2 * GROUP_STEP]
    o_ref, lse_ref = refs[2 * GROUP_STEP:]
    g, step = pl.program_id(0), pl.program_id(1)
    first = step * GROUP_STEP

    @pl.when(tb_ref[g, first] < 0)
    def _():
        o_ref[...] = jnp.zeros(o_ref.shape, o_ref.dtype)
        lse_ref[...] = jnp.full(lse_ref.shape, MASKED, lse_ref.dtype)

    @pl.when(tb_ref[g, first] >= 0)
    def _():
        for u in range(GROUP_STEP):
            used = tb_ref[g, first + u] >= 0
            rows = slice(u * t, (u + 1) * t)
            o, lse = _block_attention(q_ref[rows, :].astype(BF16), k_refs[u][...], v_refs[u][...])
            o_ref[rows, :] = jnp.where(used, o, 0.0)
            lse_ref[rows, :] = jnp.where(used, lse, MASKED)


def _moba_group_attention(q_grouped, tile_blk, qk, v):
    bh, rows, _ = q_grouped.shape
    t = MOBA_BLOCK
    tiles = rows // t
    heads = A_HEADS

    def block_of(u, first_col):
        return lambda g, s, tb: (g // heads, jnp.maximum(tb[g, s * GROUP_STEP + u], 0),
                                 first_col + g % heads)

    row_tile = pl.BlockSpec((None, GROUP_STEP * t, HEAD_DIM), lambda g, s, tb: (g, s, 0))
    key_value = lambda first_col: [pl.BlockSpec((None, t, HEAD_DIM), block_of(u, first_col))
                                   for u in range(GROUP_STEP)]
    grid_spec = pltpu.PrefetchScalarGridSpec(
        num_scalar_prefetch=1,
        grid=(bh, tiles // GROUP_STEP),
        in_specs=[row_tile] + key_value(KA_BLK) + key_value(0),
        out_specs=[row_tile, row_tile],
    )
    return pl.pallas_call(
        _moba_group_kernel,
        grid_spec=grid_spec,
        out_shape=[jax.ShapeDtypeStruct(q_grouped.shape, F32)] * 2,
        compiler_params=_cparams("parallel", "parallel"),
        name="moba_group",
    )(tile_blk, q_grouped, *([qk] * GROUP_STEP), *([v] * GROUP_STEP))


def _moba_merge_kernel(q_ref, k_ref, v_ref, po_ref, pl_ref, o_ref):
    t = MOBA_BLOCK
    causal = (lax.broadcasted_iota(jnp.int32, (t, t), 1)
              <= lax.broadcasted_iota(jnp.int32, (t, t), 0))
    for hd in range(A_HEADS):
        sl = slice(hd * HEAD_DIM, (hd + 1) * HEAD_DIM)
        o_own, lse_own = _block_attention(q_ref[:, sl], k_ref[:, sl], v_ref[:, sl], causal)
        outs = [o_own] + [po_ref[s, hd] for s in range(MOBA_TOPK)]
        lses = [lse_own] + [pl_ref[s, hd] for s in range(MOBA_TOPK)]
        top = functools.reduce(jnp.maximum, lses)
        weights = [jnp.exp2(l - top) for l in lses]
        num = sum(w * o for w, o in zip(weights, outs))
        o_ref[:, sl] = (num / sum(weights)).astype(o_ref.dtype)


def _moba_merge(qk, v, part_o, part_lse):
    bsz, seq, _ = qk.shape
    t = MOBA_BLOCK
    part = pl.BlockSpec((MOBA_TOPK, None, A_HEADS, t, HEAD_DIM), lambda b, i: (0, b, 0, i, 0))
    return pl.pallas_call(
        _moba_merge_kernel,
        grid=(bsz, seq // t),
        in_specs=[pl.BlockSpec((None, t, A_W), lambda b, i: (b, i, QA_BLK * LANES // A_W)),
                  pl.BlockSpec((None, t, A_W), lambda b, i: (b, i, KA_BLK * LANES // A_W)),
                  pl.BlockSpec((None, t, A_W), lambda b, i: (b, i, 0)),
                  part, part],
        out_specs=pl.BlockSpec((None, t, A_W), lambda b, i: (b, i, 0)),
        out_shape=jax.ShapeDtypeStruct((bsz, seq, A_W), BF16),
        compiler_params=_cparams("parallel", "parallel"),
        name="moba_merge",
    )(qk, qk, v, part_o, part_lse)


def _moba_regroup(qk, kmean):
    bsz, seq, _ = qk.shape
    bh = bsz * A_HEADS
    rows = _moba_tiles(seq) * MOBA_BLOCK
    ids, cnt, q_f32 = _moba_gate(qk, kmean)
    pos, tile_blk = _moba_routes(ids, cnt, seq)
    q_grouped = _sc_scatter_rows(q_f32.reshape(bh * seq, HEAD_DIM), pos, bh * rows)
    return q_grouped.reshape(bh, rows, HEAD_DIM), pos, tile_blk


def _moba_picked_blocks(q_grouped, pos, tile_blk, qk, v):
    bsz, seq, _ = qk.shape
    bh, rows, _ = q_grouped.shape
    o_g, lse_g = _moba_group_attention(q_grouped, tile_blk, qk, v)
    flat = pos.reshape(-1)
    back = lambda a: _sc_gather_rows(a.reshape(bh * rows, HEAD_DIM), flat).reshape(
        MOBA_TOPK, bsz, A_HEADS, seq, HEAD_DIM)
    return back(o_g), back(lse_g)


def _proj_dilated_kernel(h_ref, w_ref, c_ref, s_ref, q_ref, k_ref, v_ref, sc, *, d):
    acc = _dot(h_ref[...], w_ref[...])
    c, s = c_ref[...], s_ref[...]
    q_scale = HEAD_DIM ** -0.5 * LOG2E
    for j in range(acc.shape[1] // LANES):
        blk = acc[:, j * LANES:(j + 1) * LANES]
        if j < B_HEADS:
            blk = _rope128(blk, c, s) * q_scale
        elif j < 2 * B_HEADS:
            blk = _rope128(blk, c, s)
        sc[j] = blk
    rows = acc.shape[0] // d
    for r in range(d):
        for j in range(acc.shape[1] // LANES):
            dst = (q_ref, k_ref, v_ref)[j // B_HEADS]
            col = (j % B_HEADS) * LANES
            dst[r, :, col:col + LANES] = sc[j, pl.ds(r, rows, stride=d), :].astype(dst.dtype)


def _proj_dilated(h, w, rope_h, d, bsz, seq, tm=512):
    m, k = h.shape
    nt = seq // tm
    pos = pl.BlockSpec((tm, LANES), lambda i: (i % nt, 0))
    out = pl.BlockSpec((None, d, tm // d, B_W), lambda i: (i // nt, 0, i % nt, 0))
    return pl.pallas_call(
        functools.partial(_proj_dilated_kernel, d=d),
        grid=(m // tm,),
        in_specs=[pl.BlockSpec((tm, k), lambda i: (i, 0)), pl.BlockSpec(w.shape, lambda i: (0, 0)),
                  pos, pos],
        out_specs=[out] * 3,
        out_shape=[jax.ShapeDtypeStruct((bsz, d, seq // d, B_W), BF16)] * 3,
        scratch_shapes=[pltpu.VMEM((w.shape[1] // LANES, tm, LANES), F32)],
        compiler_params=_cparams("parallel"),
        name=f"proj_dilated_d{d}",
    )(h, w, *rope_h)


def _dilated_kernel(q_ref, kc_ref, kp_ref, vc_ref, vp_ref, o_ref, lse_ref, *, span):
    t, tp = q_ref.shape[0], kp_ref.shape[0]
    i = pl.program_id(2)
    shape = (2 * tp, tp)
    key_row = lax.broadcasted_iota(jnp.int32, shape, 0)
    dist = lax.broadcasted_iota(jnp.int32, shape, 1) + tp - key_row
    visible = (dist >= 0) & (dist <= span)
    bias = jnp.where(visible, 0.0, MASKED)
    bias_first = jnp.where(visible & ((key_row >= tp) | (i > 0)), 0.0, MASKED)
    ones = jnp.ones((BF16_ROWS, tp + t), BF16)

    def transposed(x):
        return x.astype(F32).T.astype(BF16)

    for j in range(B_HEADS):
        sl = slice(j * LANES, (j + 1) * LANES)
        qT = transposed(q_ref[:, sl])
        k_all = jnp.concatenate([kp_ref[:, sl], kc_ref[:, sl]], axis=0)
        vt_all = jnp.concatenate([transposed(vp_ref[:, sl]), transposed(vc_ref[:, sl])], axis=1)
        vt_all = jnp.concatenate([vt_all, ones], axis=0)
        outs, lses = [], []
        for u in range(t // tp):
            window = slice(u * tp, (u + 2) * tp)
            s = _dot(k_all[window, :], qT[:, u * tp:(u + 1) * tp])
            s = s + (bias_first if u == 0 else bias)
            m = jnp.max(s, axis=0, keepdims=True)
            p = jnp.exp2(s - m).astype(BF16)
            acc = _dot(vt_all[:, window], p)
            den = acc[V_DIM:V_DIM + 1, :]
            outs.append(acc[:V_DIM, :] / den)
            lses.append(m + jnp.log2(den))
        o_ref[:, sl] = jnp.concatenate(outs, axis=1).T
        lse = jnp.concatenate(lses, axis=1)
        lse_ref[:, sl] = jnp.broadcast_to(lse, (LANES, t)).T


def _dilated_attention(q, k, v, span, t=1024):
    bsz, d, length, _ = q.shape
    t = min(t, length)
    tp = B_QBLOCK
    assert span <= tp and t % tp == 0
    cur = pl.BlockSpec((None, None, t, B_W), lambda b, r, i: (b, r, i, 0))
    prev = pl.BlockSpec((None, None, tp, B_W),
                        lambda b, r, i: (b, r, jnp.maximum(i * (t // tp) - 1, 0), 0))
    return pl.pallas_call(
        functools.partial(_dilated_kernel, span=span),
        grid=(bsz, d, length // t),
        in_specs=[cur, cur, prev, cur, prev],
        out_specs=[cur, cur],
        out_shape=[jax.ShapeDtypeStruct(q.shape, F32)] * 2,
        compiler_params=_cparams("parallel", "parallel", "parallel"),
        name=f"dilated_d{d}",
    )(q, k, k, v, v)


def _natural_rows(ref, sc):
    d, rows = ref.shape[0], ref.shape[1]
    if d == 1:
        return ref[0]
    for r in range(d):
        for j in range(B_HEADS):
            sc[j, pl.ds(r, rows, stride=d), :] = ref[r, :, j * LANES:(j + 1) * LANES]
    return jnp.concatenate([sc[j] for j in range(B_HEADS)], axis=-1)


def _mixer_tail_kernel(x_ref, oa_ref, o0_ref, o1_ref, o2_ref, l0_ref, l1_ref, l2_ref, oc_ref,
                       g_ref, wpa_ref, wpb_ref, wpc_ref, wo_ref, y_ref, *scratch):
    o0, o1, o2, l0, l1, l2 = [
        _natural_rows(ref, sc)
        for ref, sc in zip((o0_ref, o1_ref, o2_ref, l0_ref, l1_ref, l2_ref), scratch)]
    mx = jnp.maximum(jnp.maximum(l0, l1), l2)
    e0, e1, e2 = jnp.exp2(l0 - mx), jnp.exp2(l1 - mx), jnp.exp2(l2 - mx)
    ob = (e0 * o0 + e1 * o1 + e2 * o2) / (e0 + e1 + e2)
    pa = _dot(oa_ref[...], wpa_ref[...])
    pb = _dot(ob.astype(BF16), wpb_ref[...])
    pc = _dot(oc_ref[...], wpc_ref[...])
    d = D_MODEL
    merged = (g_ref[:, 0:d].astype(F32) * pa + g_ref[:, d:2 * d].astype(F32) * pb
              + g_ref[:, 2 * d:3 * d].astype(F32) * pc)
    y_ref[...] = x_ref[...] + _dot(merged.astype(BF16), wo_ref[...])


def _mixer_tail(x, out_a, o_groups, lse_groups, out_c, gates, w_pa, w_pb, w_pc, w_o, seq, tm=256):
    m, d = x.shape
    nt = seq // tm
    row = lambda width: pl.BlockSpec((tm, width), lambda i: (i, 0))
    residue = lambda g: pl.BlockSpec((None, g.shape[1], tm // g.shape[1], B_W),
                                     lambda i: (i // nt, 0, i % nt, 0))
    weights = [_resident(w) for w in (w_pa, w_pb, w_pc, w_o)]
    groups = list(o_groups) + list(lse_groups)
    return pl.pallas_call(
        _mixer_tail_kernel,
        grid=(m // tm,),
        in_specs=([row(d), row(A_W)] + [residue(g) for g in groups]
                  + [row(C_W), row(3 * d)] + weights),
        out_specs=row(d),
        out_shape=jax.ShapeDtypeStruct((m, d), F32),
        scratch_shapes=[pltpu.VMEM((B_HEADS, tm, LANES), F32) for _ in groups],
        compiler_params=_cparams("parallel"),
        name="mixer_tail",
    )(x, out_a, *groups, out_c, gates, w_pa, w_pb, w_pc, w_o)


def _mem_kv_kernel(mem_ref, g_ref, wk_ref, wv_ref, k_ref, v_ref):
    memn = _rms(mem_ref[...], g_ref[...]).astype(BF16)
    k_ref[...] = _dot(memn, wk_ref[...]).astype(k_ref.dtype)
    v_ref[...] = _dot(memn, wv_ref[...]).astype(v_ref.dtype)


def _mem_kv(mem, g, wk, wv):
    bsz, n, d = mem.shape
    out = pl.BlockSpec((None, n, X_W), lambda b: (b, 0, 0))
    return pl.pallas_call(
        _mem_kv_kernel,
        grid=(bsz,),
        in_specs=[pl.BlockSpec((None, n, d), lambda b: (b, 0, 0)),
                  pl.BlockSpec((1, d), lambda b: (0, 0)),
                  pl.BlockSpec(wk.shape, lambda b: (0, 0)),
                  pl.BlockSpec(wv.shape, lambda b: (0, 0))],
        out_specs=[out, out],
        out_shape=[jax.ShapeDtypeStruct((bsz, n, X_W), BF16)] * 2,
        compiler_params=_cparams("parallel"),
        name="mem_kv",
    )(mem, g.reshape(1, d), wk, wv)


def _mem_attn_kernel(x_ref, g_ref, wq_ref, k_ref, v_ref, wo_ref, y_ref):
    x = x_ref[...]
    h = _rms(x, g_ref[...]).astype(BF16)
    q = (_dot(h, wq_ref[...]) * HEAD_DIM ** -0.5).astype(BF16)
    heads = []
    for hd in range(X_HEADS):
        sl = slice(hd * HEAD_DIM, (hd + 1) * HEAD_DIM)
        s = _dot_nt(q[:, sl], k_ref[:, sl])
        p = jnp.exp(s - jnp.max(s, axis=-1, keepdims=True))
        o = _dot(p.astype(BF16), v_ref[:, sl]) / jnp.sum(p, axis=-1, keepdims=True)
        heads.append(o.astype(BF16))
    y_ref[...] = x + _dot(jnp.concatenate(heads, axis=-1), wo_ref[...])


def _mem_attention(x, g, wq, kmem, vmem, wo, seq, tm=512):
    m, d = x.shape
    nt = seq // tm
    n = kmem.shape[1]
    kv = pl.BlockSpec((None, n, X_W), lambda i: (i // nt, 0, 0))
    return pl.pallas_call(
        _mem_attn_kernel,
        grid=(m // tm,),
        in_specs=[pl.BlockSpec((tm, d), lambda i: (i, 0)),
                  pl.BlockSpec((1, d), lambda i: (0, 0)),
                  pl.BlockSpec(wq.shape, lambda i: (0, 0)), kv, kv,
                  pl.BlockSpec(wo.shape, lambda i: (0, 0))],
        out_specs=pl.BlockSpec((tm, d), lambda i: (i, 0)),
        out_shape=jax.ShapeDtypeStruct((m, d), F32),
        compiler_params=_cparams("parallel"),
        name="mem_attention",
    )(x, g.reshape(1, d), wq, kmem, vmem, wo)


def _ffn_up_kernel(x_ref, halo_ref, g_ref, wg_ref, wv_ref, cwg_ref, cwv_ref, cbg_ref, cbv_ref,
                   act_ref, h_sc, *, tiles_per_seq):
    i = pl.program_id(0)
    tm = x_ref.shape[0]

    @pl.when(pl.program_id(1) == 0)
    def _():
        g = g_ref[...]
        keep = (i % tiles_per_seq != 0).astype(F32)
        h_sc[0:HALO, :] = (_rms(halo_ref[...], g) * keep).astype(h_sc.dtype)
        h_sc[HALO:, :] = _rms(x_ref[...], g).astype(h_sc.dtype)

    h = h_sc[...]

    def conv(w_ref, cw_ref, cb_ref):
        u = _dot(h, w_ref[...])
        c = cb_ref[...]
        for tap in range(CONV_W):
            lo = HALO - (CONV_W - 1) + tap
            c = c + cw_ref[tap:tap + 1, :] * u[lo:lo + tm, :]
        return c

    act = jax.nn.silu(conv(wg_ref, cwg_ref, cbg_ref)) * conv(wv_ref, cwv_ref, cbv_ref)
    act_ref[...] = act.astype(act_ref.dtype)


def _ffn_down_kernel(a_ref, w_ref, x_ref, y_ref):
    y_ref[...] = x_ref[...] + _dot(a_ref[...], w_ref[...])


def _conv_ffn(x, g, w_up, conv_w, conv_b, w_down, layer, seq):
    m, d = x.shape
    act = _ffn_up(x, g, w_up, conv_w, conv_b, seq)
    tm, tn = 1024, FFN_TF
    return pl.pallas_call(
        _ffn_down_kernel,
        grid=(m // tm, d // tn),
        in_specs=[pl.BlockSpec((tm, D_FF), lambda i, j: (i, 0)),
                  pl.BlockSpec((None, D_FF, tn), lambda i, j: (layer, 0, j)),
                  pl.BlockSpec((tm, tn), lambda i, j: (i, j))],
        out_specs=pl.BlockSpec((tm, tn), lambda i, j: (i, j)),
        out_shape=jax.ShapeDtypeStruct((m, d), F32),
        compiler_params=_cparams("parallel", "parallel"),
        name="ffn_down",
    )(act, w_down, x)


def _ffn_up(x, g, w_up, conv_w, conv_b, seq, tm=1024, tf=FFN_TF):
    m, d = x.shape
    nf = D_FF_PAD // tf
    halo_blocks = tm // HALO
    return pl.pallas_call(
        functools.partial(_ffn_up_kernel, tiles_per_seq=seq // tm),
        grid=(m // tm, nf),
        in_specs=[pl.BlockSpec((tm, d), lambda i, f: (i, 0)),
                  pl.BlockSpec((HALO, d), lambda i, f: (jnp.maximum(i * halo_blocks - 1, 0), 0)),
                  pl.BlockSpec((1, d), lambda i, f: (0, 0)),
                  pl.BlockSpec((d, tf), lambda i, f: (0, f)),
                  pl.BlockSpec((d, tf), lambda i, f: (0, f + nf)),
                  pl.BlockSpec((CONV_W, tf), lambda i, f: (0, f)),
                  pl.BlockSpec((CONV_W, tf), lambda i, f: (0, f + nf)),
                  pl.BlockSpec((1, tf), lambda i, f: (0, f)),
                  pl.BlockSpec((1, tf), lambda i, f: (0, f + nf))],
        out_specs=pl.BlockSpec((tm, tf), lambda i, f: (i, f)),
        out_shape=jax.ShapeDtypeStruct((m, D_FF_PAD), BF16),
        scratch_shapes=[pltpu.VMEM((HALO + tm, d), BF16)],
        compiler_params=_cparams("parallel", "arbitrary"),
        name="ffn_up",
    )(x, x, g.reshape(1, d), w_up, w_up, conv_w, conv_w, conv_b, conv_b)


def _rope_tables(seq):
    def angles(dim):
        inv_freq = jnp.exp(jnp.arange(0, dim, 2, dtype=F32) * (-math.log(ROPE_THETA) / dim))
        ang = jnp.arange(seq, dtype=F32)[:, None] * inv_freq[None, :]
        return jnp.cos(ang), jnp.sin(ang)

    cos_h, sin_h = angles(HEAD_DIM)
    rope_h = (jnp.concatenate([cos_h, cos_h], axis=-1), jnp.concatenate([-sin_h, sin_h], axis=-1))
    cos_r, sin_r = angles(ROPE_DIM)
    z = jnp.zeros_like(cos_r)
    rope_r = (jnp.concatenate([cos_r, cos_r, z, z], axis=-1),
              jnp.concatenate([-sin_r, z, z, z], axis=-1),
              jnp.concatenate([z, sin_r, z, z], axis=-1))
    return rope_h, rope_r


def _split_in(w_in):
    return [w_in[:, IN_OFFSETS[k]:IN_OFFSETS[k + 1]] for k in range(len(IN_WIDTHS))]


def _pad_cols(w, width):
    return jnp.pad(w, ((0, 0), (0, width - w.shape[1])))


def _layer_params(w_in, w_uq, w_ukv, w_up, conv_w, conv_b):
    qa, ka, va, qb, kb, vb, cq, ckv, kr, gates = _split_in(w_in)
    w_qk = jnp.concatenate([qa, ka], axis=1).astype(BF16)
    group_cols = lambda w, g: w[:, g * B_W:(g + 1) * B_W]
    w_b = [jnp.concatenate([group_cols(qb, g), group_cols(kb, g), group_cols(vb, g)],
                           axis=1).astype(BF16) for g in range(len(B_GROUPS))]
    w_down_in = jnp.concatenate([cq, ckv, _pad_cols(kr, LANES)], axis=1).astype(BF16)
    uq = w_uq.reshape(Q_LORA, C_HEADS, NOPE_DIM + ROPE_DIM)
    uq = jnp.pad(uq, ((0, 0), (0, 0), (0, C_QK - NOPE_DIM - ROPE_DIM)))
    ukv = w_ukv.reshape(KV_LORA, C_HEADS, NOPE_DIM + V_DIM)
    pad_ff = lambda w: jnp.pad(w, ((0, 0), (0, D_FF_PAD - D_FF)))
    two_halves = lambda w: jnp.concatenate([pad_ff(w[:, :D_FF]), pad_ff(w[:, D_FF:])], axis=1)
    return dict(
        w_qk=w_qk, w_va=va.astype(BF16), w_b=w_b, w_gates=gates.astype(BF16),
        w_down_in=w_down_in,
        w_uq=uq.reshape(Q_LORA, C_HEADS * C_QK).astype(BF16),
        w_uk=ukv[:, :, :NOPE_DIM].reshape(KV_LORA, C_HEADS * NOPE_DIM).astype(BF16),
        w_uv=ukv[:, :, NOPE_DIM:].reshape(KV_LORA, C_W).astype(BF16),
        w_up=two_halves(w_up).astype(BF16),
        conv_w=two_halves(conv_w),
        conv_b=two_halves(conv_b.reshape(1, -1)),
    )


def _qk_col_scale():
    q_scale = HEAD_DIM ** -0.5
    parts = [jnp.full((A_W,), q_scale * LOG2E, F32), jnp.ones((A_W,), F32)]
    return jnp.concatenate(parts).reshape(1, QK_W)


def _mixer(x, g_mix, p, g_cq, g_ckv, w_pa, w_pb, w_pc, w_o, rope_h, rope_r, bsz, seq):
    m = x.shape[0]
    h = _rmsnorm(x, g_mix, BF16)
    qk = _matmul(h, p["w_qk"], _mm_rope_kernel, BF16, 1024, 1024, seq=seq,
                 extras=(("col", _qk_col_scale()), ("pos", rope_h[0]), ("pos", rope_h[1])),
                 name="proj_qk_rope")
    qk3 = qk.reshape(bsz, seq, QK_W)
    q_grouped, pos, tile_blk = _moba_regroup(qk3, _kmean(qk3))
    v_a = _matmul(h, p["w_va"], _mm_plain_kernel, BF16, 1024, A_W, name="proj_va")
    gates = _matmul(h, p["w_gates"], _mm_sigmoid_kernel, BF16, 1024, 1024, name="proj_gates")
    v_a3 = v_a.reshape(bsz, seq, A_W)
    part_o, part_lse = _moba_picked_blocks(q_grouped, pos, tile_blk, qk3, v_a3)
    cq, ckv, kr = _mla_down(h, p["w_down_in"], g_cq, g_ckv, rope_r, seq)
    q_c = _mla_q(cq, p["w_uq"], rope_r, seq)
    k_c, vt_c = _mla_kv(ckv, kr, p["w_uk"], p["w_uv"], bsz, seq)
    groups = []
    for (window, d), w_g in zip(B_GROUPS, p["w_b"]):
        q_g, k_g, v_g = _proj_dilated(h, w_g, rope_h, d, bsz, seq)
        groups.append(_dilated_attention(q_g, k_g, v_g, window // d))
    out_c = _flash_attention(q_c.reshape(bsz, seq, -1), k_c.reshape(bsz, seq, -1), vt_c,
                             C_HEADS, C_QK).reshape(m, C_W)
    out_a = _moba_merge(qk3, v_a3, part_o, part_lse).reshape(m, A_W)
    return _mixer_tail(x, out_a, [g[0] for g in groups], [g[1] for g in groups], out_c, gates,
                       w_pa.astype(BF16), w_pb.astype(BF16), w_pc.astype(BF16), w_o.astype(BF16),
                       seq)


def kernel(x, mem, g_mix, w_in, g_cq, g_ckv, w_uq, w_ukv, w_pa, w_pb, w_pc, w_o, g_mem, g_memkv,
           w_xq, w_xk, w_xv, w_xo, g_ffn, w_up, conv_w, conv_b, w_down, g_final):
    bsz, seq, d = x.shape
    rope_h, rope_r = _rope_tables(seq)
    xf = x.reshape(bsz * seq, d)
    w_down = w_down.astype(BF16)
    for l in range(DEPTH):
        p = _layer_params(w_in[l], w_uq[l], w_ukv[l], w_up[l], conv_w[l], conv_b[l])
        xf = _mixer(xf, g_mix[l], p, g_cq[l], g_ckv[l], w_pa[l], w_pb[l], w_pc[l], w_o[l],
                    rope_h, rope_r, bsz, seq)
        kmem, vmem = _mem_kv(mem, g_memkv[l], w_xk[l].astype(BF16), w_xv[l].astype(BF16))
        xf = _mem_attention(xf, g_mem[l], w_xq[l].astype(BF16), kmem, vmem,
                            w_xo[l].astype(BF16), seq)
        xf = _conv_ffn(xf, g_ffn[l], p["w_up"], p["conv_w"], p["conv_b"], w_down, l, seq)
    return _rmsnorm(xf, g_final, F32).reshape(bsz, seq, d)
```

```python
import functools
import math

import jax
import jax.numpy as jnp
import numpy as np
from jax import lax
from jax.experimental import pallas as pl
from jax.experimental.pallas import tpu as pltpu
from jax.experimental.pallas import tpu_sc as plsc

F32 = jnp.float32
BF16 = jnp.bfloat16

LANES = 128
SUBLANES = 8
VMEM_LIMIT = 56 * 1024 * 1024

D_MODEL = 2048
DEPTH = 2
HEAD_DIM = 128
ROPE_THETA = 10000.0
EPS = 1e-6

A_HEADS = 4
MOBA_BLOCK = 256
MOBA_TOPK = 3

B_GROUPS = ((128, 1), (512, 4), (2048, 16))
B_HEADS = 4
B_QBLOCK = 128

C_HEADS = 8
Q_LORA = 1536
KV_LORA = 512
NOPE_DIM = 128
ROPE_DIM = 64
V_DIM = 128

X_HEADS = 4
D_FF = 5504
CONV_W = 3

A_W = A_HEADS * HEAD_DIM
B_QKV_W = len(B_GROUPS) * B_HEADS * HEAD_DIM
B_W = B_HEADS * HEAD_DIM
C_W = C_HEADS * V_DIM
X_W = X_HEADS * HEAD_DIM
IN_WIDTHS = (A_W, A_W, A_W, B_QKV_W, B_QKV_W, B_QKV_W, Q_LORA, KV_LORA, ROPE_DIM, 3 * D_MODEL)
IN_OFFSETS = tuple(int(o) for o in np.cumsum((0,) + IN_WIDTHS))

QK_W = 2 * A_W
QA_BLK, KA_BLK = 0, A_W // LANES

C_QK = 2 * LANES
MASKED = -1e30
LOG2E = math.log2(math.e)
BF16_ROWS = 16
VT_ROWS = V_DIM + BF16_ROWS
GROUP_STEP = 8
SC_WINDOW = 128

D_FF_PAD = 5632
FFN_TF = 512
HALO = SUBLANES


def _cparams(*sem):
    return pltpu.CompilerParams(dimension_semantics=sem, vmem_limit_bytes=VMEM_LIMIT)


def _resident(arr):
    zeros = (0,) * arr.ndim
    return pl.BlockSpec(arr.shape, lambda *_: zeros, pipeline_mode=pl.Buffered(1))


def _dot(a, b):
    return jnp.dot(a, b, preferred_element_type=F32)


def _dot_nt(a, b):
    return lax.dot_general(a, b, (((1,), (1,)), ((), ())), preferred_element_type=F32)


def _rms(x, g):
    return x * lax.rsqrt(jnp.mean(x * x, axis=-1, keepdims=True) + EPS) * g


def _rmsnorm_kernel(x_ref, g_ref, o_ref):
    o_ref[...] = _rms(x_ref[...], g_ref[...]).astype(o_ref.dtype)


def _rmsnorm(x, g, out_dtype, tm=512):
    m, d = x.shape
    return pl.pallas_call(
        _rmsnorm_kernel,
        grid=(m // tm,),
        in_specs=[pl.BlockSpec((tm, d), lambda i: (i, 0)),
                  pl.BlockSpec((1, d), lambda i: (0, 0))],
        out_specs=pl.BlockSpec((tm, d), lambda i: (i, 0)),
        out_shape=jax.ShapeDtypeStruct((m, d), out_dtype),
        compiler_params=_cparams("parallel"),
        name="rmsnorm",
    )(x, g.reshape(1, d))


def _rope128(x, c, s):
    return x * c + pltpu.roll(x, HEAD_DIM // 2, 1) * s


def _rope64(x, c, sa, sb):
    half = ROPE_DIM // 2
    return x * c + pltpu.roll(x, LANES - half, 1) * sa + pltpu.roll(x, half, 1) * sb


def _mm_plain_kernel(a_ref, w_ref, o_ref):
    o_ref[...] = _dot(a_ref[...], w_ref[...]).astype(o_ref.dtype)


def _mm_sigmoid_kernel(a_ref, w_ref, o_ref):
    o_ref[...] = jax.nn.sigmoid(_dot(a_ref[...], w_ref[...])).astype(o_ref.dtype)


def _mm_rope_kernel(a_ref, w_ref, cs_ref, c_ref, s_ref, o_ref):
    acc = _dot(a_ref[...], w_ref[...])
    c = c_ref[...]
    s = s_ref[...]
    for j in range(acc.shape[1] // LANES):
        sl = slice(j * LANES, (j + 1) * LANES)
        o_ref[:, sl] = (_rope128(acc[:, sl], c, s) * cs_ref[:, sl]).astype(o_ref.dtype)


def _matmul(a, w, kernel, out_dtype, tm, tn, seq=None, extras=(), name="matmul"):
    m, k = a.shape
    n = w.shape[1]
    in_specs = [pl.BlockSpec((tm, k), lambda i, j: (i, 0)),
                pl.BlockSpec((k, tn), lambda i, j: (0, j))]
    args = [a, w]
    for kind, arr in extras:
        if kind == "col":
            in_specs.append(pl.BlockSpec((1, tn), lambda i, j: (0, j)))
        else:
            nt = seq // tm
            in_specs.append(pl.BlockSpec((tm, LANES), lambda i, j: (i % nt, 0)))
        args.append(arr)
    return pl.pallas_call(
        kernel,
        grid=(m // tm, n // tn),
        in_specs=in_specs,
        out_specs=pl.BlockSpec((tm, tn), lambda i, j: (i, j)),
        out_shape=jax.ShapeDtypeStruct((m, n), out_dtype),
        compiler_params=_cparams("parallel", "parallel"),
        name=name,
    )(*args)


def _mla_down_kernel(h_ref, w_ref, gq_ref, gkv_ref, c_ref, sa_ref, sb_ref,
                     cq_ref, ckv_ref, kr_ref):
    acc = _dot(h_ref[...], w_ref[...])
    cq_ref[...] = _rms(acc[:, :Q_LORA], gq_ref[...]).astype(cq_ref.dtype)
    ckv_ref[...] = _rms(acc[:, Q_LORA:Q_LORA + KV_LORA], gkv_ref[...]).astype(ckv_ref.dtype)
    kr = acc[:, Q_LORA + KV_LORA:]
    kr_ref[...] = _rope64(kr, c_ref[...], sa_ref[...], sb_ref[...]).astype(kr_ref.dtype)


def _mla_down(h, w, g_cq, g_ckv, rope_r, seq, tm=512):
    m, k = h.shape
    n = w.shape[1]
    nt = seq // tm
    row = lambda width: pl.BlockSpec((tm, width), lambda i: (i, 0))
    full = lambda r, c: pl.BlockSpec((r, c), lambda i: (0, 0))
    pos = pl.BlockSpec((tm, LANES), lambda i: (i % nt, 0))
    return pl.pallas_call(
        _mla_down_kernel,
        grid=(m // tm,),
        in_specs=[row(k), full(k, n), full(1, Q_LORA), full(1, KV_LORA), pos, pos, pos],
        out_specs=[row(Q_LORA), row(KV_LORA), row(LANES)],
        out_shape=[jax.ShapeDtypeStruct((m, Q_LORA), BF16),
                   jax.ShapeDtypeStruct((m, KV_LORA), BF16),
                   jax.ShapeDtypeStruct((m, LANES), BF16)],
        compiler_params=_cparams("parallel"),
        name="mla_down",
    )(h, w, g_cq.reshape(1, -1), g_ckv.reshape(1, -1), *rope_r)


def _mla_q_kernel(cq_ref, w_ref, c_ref, sa_ref, sb_ref, q_ref, *, scale):
    acc = _dot(cq_ref[...], w_ref[...])
    c, sa, sb = c_ref[...], sa_ref[...], sb_ref[...]
    for hd in range(C_HEADS):
        lo = hd * C_QK
        q_ref[:, lo:lo + LANES] = (acc[:, lo:lo + LANES] * scale).astype(q_ref.dtype)
        rope = _rope64(acc[:, lo + LANES:lo + C_QK], c, sa, sb)
        q_ref[:, lo + LANES:lo + C_QK] = (rope * scale).astype(q_ref.dtype)


def _mla_q(cq, w, rope_r, seq, tm=512):
    m, k = cq.shape
    n = w.shape[1]
    nt = seq // tm
    pos = pl.BlockSpec((tm, LANES), lambda i: (i % nt, 0))
    return pl.pallas_call(
        functools.partial(_mla_q_kernel, scale=(NOPE_DIM + ROPE_DIM) ** -0.5 * LOG2E),
        grid=(m // tm,),
        in_specs=[pl.BlockSpec((tm, k), lambda i: (i, 0)),
                  pl.BlockSpec((k, n), lambda i: (0, 0)), pos, pos, pos],
        out_specs=pl.BlockSpec((tm, n), lambda i: (i, 0)),
        out_shape=jax.ShapeDtypeStruct((m, n), BF16),
        compiler_params=_cparams("parallel"),
        name="mla_q",
    )(cq, w, *rope_r)


def _store_vt(v, vt_ref):
    vt = v.T
    for hd in range(vt_ref.shape[0]):
        vt_ref[hd, 0:V_DIM, :] = vt[hd * V_DIM:(hd + 1) * V_DIM, :].astype(vt_ref.dtype)
        vt_ref[hd, V_DIM:VT_ROWS, :] = jnp.ones((VT_ROWS - V_DIM, vt.shape[1]), vt_ref.dtype)


def _mla_kv_kernel(ckv_ref, kr_ref, wk_ref, wv_ref, k_ref, vt_ref):
    ckv = ckv_ref[...]
    kn = _dot(ckv, wk_ref[...])
    kr = kr_ref[...]
    for hd in range(C_HEADS):
        lo = hd * C_QK
        k_ref[:, lo:lo + LANES] = kn[:, hd * LANES:(hd + 1) * LANES].astype(k_ref.dtype)
        k_ref[:, lo + LANES:lo + C_QK] = kr
    _store_vt(_dot(ckv, wv_ref[...]), vt_ref)


def _mla_kv(ckv, kr, wk, wv, bsz, seq, tm=512):
    m, k = ckv.shape
    nt = seq // tm
    return pl.pallas_call(
        _mla_kv_kernel,
        grid=(m // tm,),
        in_specs=[pl.BlockSpec((tm, k), lambda i: (i, 0)),
                  pl.BlockSpec((tm, LANES), lambda i: (i, 0)),
                  pl.BlockSpec(wk.shape, lambda i: (0, 0)),
                  pl.BlockSpec(wv.shape, lambda i: (0, 0))],
        out_specs=[pl.BlockSpec((tm, C_HEADS * C_QK), lambda i: (i, 0)),
                   pl.BlockSpec((None, C_HEADS, VT_ROWS, tm), lambda i: (i // nt, 0, 0, i % nt))],
        out_shape=[jax.ShapeDtypeStruct((m, C_HEADS * C_QK), BF16),
                   jax.ShapeDtypeStruct((bsz, C_HEADS, VT_ROWS, seq), BF16)],
        compiler_params=_cparams("parallel"),
        name="mla_kv",
    )(ckv, kr, wk, wv)


def _attend_chunks(qT, k_ref, vt_ref, scratch, *, tk, n_full, mask_tail, tail_steps, tail_col,
                   unroll):
    m_sc, acc_sc, sa_sc, sb_sc, pa_sc, pb_sc, ala_sc, alb_sc, mxa_sc, mxb_sc = scratch
    s_bufs = (sa_sc, sb_sc)
    p_bufs = (pa_sc, pb_sc)
    al_bufs = (ala_sc, alb_sc)
    mx_bufs = (mxa_sc, mxb_sc)
    last_chunk = k_ref.shape[0] // tk - 1
    m_sc[...] = jnp.full(m_sc.shape, MASKED, F32)
    acc_sc[...] = jnp.zeros(acc_sc.shape, F32)
    for p_ref, al_ref in zip(p_bufs, al_bufs):
        p_ref[...] = jnp.zeros(p_ref.shape, p_ref.dtype)
        al_ref[...] = jnp.ones(al_ref.shape, F32)

    def rows(c):
        return pl.ds(pl.multiple_of(jnp.clip(c, 0, last_chunk) * tk, tk), tk)

    def scores(c, slot, col=0):
        sT = _dot(k_ref[rows(c), :], qT[:, col:])
        s_bufs[slot][:, col:] = sT
        mx_bufs[slot][:, col:] = jnp.max(sT, axis=0, keepdims=True)

    def flush(c, slot, col=0):
        acc_sc[:, col:] = (al_bufs[slot][:, col:] * acc_sc[:, col:]
                           + _dot(vt_ref[:, rows(c)], p_bufs[slot][:, col:]))

    def softmax(sT, top, slot, col):
        m_old = m_sc[:, col:]
        m_new = jnp.maximum(m_old, top)
        al_bufs[slot][:, col:] = jnp.exp2(m_old - m_new)
        p_bufs[slot][:, col:] = jnp.exp2(sT - m_new).astype(p_bufs[slot].dtype)
        m_sc[:, col:] = m_new

    def step(tau, slot, mask, cols):
        col_flush, col, col_next = cols
        flush(tau - 2, slot, col_flush)
        if col_next is not None:
            scores(tau + 1, 1 - slot, col_next)
        sT = s_bufs[slot][:, col:]
        if mask is None:
            softmax(sT, mx_bufs[slot][:, col:], slot, col)
        else:
            sT = mask(sT, tau, col)
            softmax(sT, jnp.max(sT, axis=0, keepdims=True), slot, col)

    def full_steps(tau0, count):
        for j in range(count):
            step(tau0 + j, j % 2, None, (0, 0, 0))

    scores(0, 0)
    trips = n_full // unroll
    lax.fori_loop(0, trips, lambda u, c: (full_steps(unroll * u, unroll), c)[1], 0)
    done = unroll * trips
    pairs = (n_full - done) // 2
    lax.fori_loop(0, pairs, lambda u, c: (full_steps(done + 2 * u, 2), c)[1], 0)
    tau = done + 2 * pairs
    cols = [tail_col(j) for j in range(tail_steps)]
    for j in range(tail_steps):
        col_flush = cols[j - 2] if j >= 2 else 0
        col_next = cols[j + 1] if j + 1 < tail_steps else None
        step(tau + j, j % 2, mask_tail, (col_flush, cols[j], col_next))
    flush(tau + tail_steps - 2, 0, cols[-2])
    flush(tau + tail_steps - 1, 1, cols[-1])
    acc = acc_sc[...]
    return acc[:V_DIM, :] / acc[V_DIM:V_DIM + 1, :]


def _attend_scratch(tq, tk):
    return [pltpu.VMEM((1, tq), F32), pltpu.VMEM((VT_ROWS, tq), F32),
            pltpu.VMEM((tk, tq), F32), pltpu.VMEM((tk, tq), F32),
            pltpu.VMEM((tk, tq), BF16), pltpu.VMEM((tk, tq), BF16),
            pltpu.VMEM((1, tq), F32), pltpu.VMEM((1, tq), F32),
            pltpu.VMEM((1, tq), F32), pltpu.VMEM((1, tq), F32)]


def _transpose_q(q_ref):
    return q_ref[...].astype(F32).T.astype(BF16)


def _flash_kernel(q_ref, k_ref, vt_ref, o_ref, *scratch, tq, tk):
    i = pl.program_id(2)

    def causal(sT, c, col):
        key = lax.broadcasted_iota(jnp.int32, sT.shape, 0) + c * tk
        qry = lax.broadcasted_iota(jnp.int32, sT.shape, 1) + (i * tq + col)
        return jnp.where(key <= qry, sT, MASKED)

    per_tile = tq // tk
    oT = _attend_chunks(_transpose_q(q_ref), k_ref, vt_ref, scratch, tk=tk, n_full=i * per_tile,
                        mask_tail=causal, tail_steps=per_tile, tail_col=lambda j: j * tk,
                        unroll=8)
    o_ref[...] = oT.T.astype(o_ref.dtype)


def _flash_attention(q, k, vt, heads, qk_w, tq=1024, tk=256):
    bsz, seq, _ = q.shape
    assert tq % (2 * tk) == 0 and seq % tq == 0
    return pl.pallas_call(
        functools.partial(_flash_kernel, tq=tq, tk=tk),
        grid=(bsz, heads, seq // tq),
        in_specs=[pl.BlockSpec((None, tq, qk_w), lambda b, h, i: (b, i, h)),
                  pl.BlockSpec((None, seq, qk_w), lambda b, h, i: (b, 0, h)),
                  pl.BlockSpec((None, None, VT_ROWS, seq), lambda b, h, i: (b, h, 0, 0))],
        out_specs=pl.BlockSpec((None, tq, V_DIM), lambda b, h, i: (b, i, h)),
        out_shape=jax.ShapeDtypeStruct((bsz, seq, heads * V_DIM), BF16),
        scratch_shapes=_attend_scratch(tq, tk),
        compiler_params=_cparams("parallel", "parallel", "arbitrary"),
        name="mla_flash",
    )(q, k, vt)


def _kmean_kernel(k_ref, o_ref):
    k = k_ref[...].astype(F32)
    o_ref[...] = jnp.mean(k.reshape(SUBLANES, MOBA_BLOCK, k.shape[-1]), axis=1)


def _kmean(qk):
    bsz, seq, _ = qk.shape
    rows = SUBLANES * MOBA_BLOCK
    return pl.pallas_call(
        _kmean_kernel,
        grid=(bsz, seq // rows),
        in_specs=[pl.BlockSpec((None, rows, A_W), lambda b, i: (b, i, KA_BLK * LANES // A_W))],
        out_specs=pl.BlockSpec((None, SUBLANES, A_W), lambda b, i: (b, i, 0)),
        out_shape=jax.ShapeDtypeStruct((bsz, seq // MOBA_BLOCK, A_W), F32),
        compiler_params=_cparams("parallel", "parallel"),
        name="moba_kmean",
    )(qk)


def _block_attention(q, k, v, visible=None):
    s = _dot_nt(q, k)
    if visible is not None:
        s = jnp.where(visible, s, MASKED)
    m = jnp.max(s, axis=-1, keepdims=True)
    p = jnp.exp2(s - m).astype(BF16)
    v_ones = jnp.concatenate([v, jnp.ones((v.shape[0], LANES), v.dtype)], axis=-1)
    acc = _dot(p, v_ones)
    den = acc[:, V_DIM:]
    return acc[:, :V_DIM] / den, m + jnp.log2(den)


def _moba_gate_kernel(q_ref, km_ref, ids_ref, cnt_ref, qf_ref):
    t = MOBA_BLOCK
    i = pl.program_id(1)
    nb = km_ref.shape[0]
    blk = lax.broadcasted_iota(jnp.int32, (nb, t), 0)
    neg_inf = jnp.float32(-jnp.inf)
    not_after = (lax.broadcasted_iota(jnp.int32, (t, t), 0)
                 <= lax.broadcasted_iota(jnp.int32, (t, t), 1))
    upper = jnp.where(not_after, 1.0, 0.0).astype(BF16)
    ones = jnp.ones((SUBLANES, t), BF16)
    for hd in range(A_HEADS):
        sl = slice(hd * HEAD_DIM, (hd + 1) * HEAD_DIM)
        q = q_ref[:, sl].astype(F32)
        qf_ref[hd] = q
        qT = q.T.astype(BF16)
        km = km_ref[:, sl]
        km_hi = km.astype(BF16)
        km_lo = (km - km_hi.astype(F32)).astype(BF16)
        g = jnp.where(blk < i, _dot(km_hi, qT) + _dot(km_lo, qT), neg_inf)
        picks, ranks, counts = [], [], []
        for _ in range(MOBA_TOPK):
            mx = jnp.max(g, axis=0, keepdims=True)
            is_max = (g == mx) & (mx > neg_inf)
            first = jnp.min(jnp.where(is_max, blk, nb), axis=0, keepdims=True)
            pick = blk == first
            g = jnp.where(pick, neg_inf, g)
            onehot = jnp.where(pick, 1.0, 0.0).astype(BF16)
            before = _dot(onehot, upper)
            rank = jnp.sum(jnp.where(pick, before - 1.0, 0.0), axis=0, keepdims=True)
            picks.append(first)
            ranks.append(rank.astype(jnp.int32))
            counts.append(_dot_nt(ones, onehot)[0:1, :])
        pad_i = jnp.zeros((SUBLANES - 2 * MOBA_TOPK, t), jnp.int32)
        ids_ref[hd] = jnp.concatenate(picks + ranks + [pad_i], axis=0)
        pad_f = jnp.zeros((SUBLANES - MOBA_TOPK, nb), F32)
        cnt_ref[hd] = jnp.concatenate(counts + [pad_f], axis=0)


def _moba_gate(qk, kmean):
    bsz, seq, _ = qk.shape
    t = MOBA_BLOCK
    nb = seq // t
    return pl.pallas_call(
        _moba_gate_kernel,
        grid=(bsz, nb),
        in_specs=[pl.BlockSpec((None, t, A_W), lambda b, i: (b, i, QA_BLK * LANES // A_W)),
                  pl.BlockSpec((None, nb, A_W), lambda b, i: (b, 0, 0))],
        out_specs=[pl.BlockSpec((None, A_HEADS, SUBLANES, t), lambda b, i: (b, 0, 0, i)),
                   pl.BlockSpec((None, A_HEADS, None, SUBLANES, nb), lambda b, i: (b, 0, i, 0, 0)),
                   pl.BlockSpec((None, A_HEADS, t, HEAD_DIM), lambda b, i: (b, 0, i, 0))],
        out_shape=[jax.ShapeDtypeStruct((bsz, A_HEADS, SUBLANES, seq), jnp.int32),
                   jax.ShapeDtypeStruct((bsz, A_HEADS, nb, SUBLANES, nb), F32),
                   jax.ShapeDtypeStruct((bsz, A_HEADS, seq, HEAD_DIM), F32)],
        compiler_params=_cparams("parallel", "parallel"),
        name="moba_gate",
    )(qk, kmean)


def _moba_routes(ids, cnt, seq):
    bsz, heads = ids.shape[:2]
    bh, t = bsz * heads, MOBA_BLOCK
    nb = seq // t
    tiles = _moba_tiles(seq)
    picks = ids[:, :, 0:MOBA_TOPK, :].reshape(bh, MOBA_TOPK, nb, t)
    ranks = ids[:, :, MOBA_TOPK:2 * MOBA_TOPK, :].reshape(bh, MOBA_TOPK, nb, t)
    per_tile = cnt[:, :, :, 0:MOBA_TOPK, :].astype(jnp.int32).reshape(bh, nb * MOBA_TOPK, nb)
    before = jnp.cumsum(per_tile, axis=1) - per_tile
    total = jnp.sum(per_tile, axis=1)
    padded = -(-total // t) * t
    ends = jnp.cumsum(padded, axis=1)
    base = before + (ends - padded)[:, None, :]
    base = base.reshape(bh, nb, MOBA_TOPK, nb).transpose(0, 2, 1, 3)
    onehot = picks[..., None] == jnp.arange(nb)
    pos = jnp.sum(jnp.where(onehot, base[:, :, :, None, :], 0), axis=-1) + ranks
    pos = jnp.where(picks < nb, pos, (tiles - 1) * t)
    pos = pos + (jnp.arange(bh, dtype=jnp.int32) * (tiles * t))[:, None, None, None]
    pos = pos.reshape(bh, MOBA_TOPK, seq).transpose(1, 0, 2).reshape(MOBA_TOPK, bh * seq)
    tile_start = jnp.arange(tiles, dtype=jnp.int32) * t
    tile_blk = jnp.sum(tile_start[None, :, None] >= ends[:, None, :], axis=-1)
    tile_blk = jnp.where(tile_start[None, :] < ends[:, -1:], tile_blk, -1)
    return pos.astype(jnp.int32), tile_blk.astype(jnp.int32)


def _moba_tiles(seq):
    nb = seq // MOBA_BLOCK
    return -(-(MOBA_TOPK * nb + nb + 1) // GROUP_STEP) * GROUP_STEP


def _sc_mesh():
    return plsc.VectorSubcoreMesh(core_axis_name="core", subcore_axis_name="subcore")


def _sc_scatter_rows(x, idx, rows):
    slots, n = idx.shape
    d = x.shape[1]

    @pl.kernel(out_type=jax.ShapeDtypeStruct((rows, d), x.dtype), mesh=_sc_mesh())
    def scatter(x_hbm, i_hbm, o_hbm):
        def body(x_vmem, i_vmem):
            pltpu.sync_copy(x_vmem, o_hbm.at[i_vmem.at[0]])

        pltpu.emit_pipeline(
            body, grid=(slots, n // SC_WINDOW),
            in_specs=[pl.BlockSpec((SC_WINDOW, d), lambda s, i: (i, 0)),
                      pl.BlockSpec((1, SC_WINDOW), lambda s, i: (s, i))],
            out_specs=[],
            core_axis_name=("core", "subcore"),
            dimension_semantics=(pltpu.PARALLEL, pltpu.PARALLEL),
        )(x_hbm, i_hbm)

    return scatter(x, idx)


def _sc_gather_rows(x, idx):
    n = idx.shape[0]
    d = x.shape[1]

    @pl.kernel(out_type=jax.ShapeDtypeStruct((n, d), x.dtype), mesh=_sc_mesh())
    def gather(x_hbm, i_hbm, o_hbm):
        def body(i_vmem, o_vmem):
            pltpu.sync_copy(x_hbm.at[i_vmem.at[0]], o_vmem)

        pltpu.emit_pipeline(
            body, grid=(n // SC_WINDOW,),
            in_specs=[pl.BlockSpec((1, SC_WINDOW), lambda i: (0, i))],
            out_specs=[pl.BlockSpec((SC_WINDOW, d), lambda i: (i, 0))],
            core_axis_name=("core", "subcore"),
            dimension_semantics=(pltpu.PARALLEL,),
        )(i_hbm, o_hbm)

    return gather(x, idx.reshape(1, n))


def _moba_group_kernel(tb_ref, q_ref, *refs):
    t = MOBA_BLOCK
    k_refs, v_refs = refs[:GROUP_STEP], refs[GROUP_STEP:2 * GROUP_STEP]
    o_ref, lse_ref = refs[-2:]
    g, step = pl.program_id(0), pl.program_id(1)
    first = step * GROUP_STEP

    @pl.when(tb_ref[g, first] < 0)
    def _():
        o_ref[...] = jnp.zeros(o_ref.shape, o_ref.dtype)
        lse_ref[...] = jnp.full(lse_ref.shape, MASKED, lse_ref.dtype)

    @pl.when(tb_ref[g, first] >= 0)
    def _():
        for u in range(GROUP_STEP):
            used = tb_ref[g, first + u] >= 0
            rows = slice(u * t, (u + 1) * t)
            o, lse = _block_attention(q_ref[rows, :].astype(BF16), k_refs[u][...], v_refs[u][...])
            o_ref[rows, :] = jnp.where(used, o, 0.0)
            lse_ref[rows, :] = jnp.where(used, lse, MASKED)


def _moba_group_attention(q_grouped, tile_blk, qk, v, after):
    bh, rows, _ = q_grouped.shape
    t = MOBA_BLOCK
    tiles = rows // t
    heads = A_HEADS

    def block_of(u, first_col):
        return lambda g, s, tb: (g // heads, jnp.maximum(tb[g, s * GROUP_STEP + u], 0),
                                 first_col + g % heads)

    row_tile = pl.BlockSpec((None, GROUP_STEP * t, HEAD_DIM), lambda g, s, tb: (g, s, 0))
    key_value = lambda first_col: [pl.BlockSpec((None, t, HEAD_DIM), block_of(u, first_col))
                                   for u in range(GROUP_STEP)]
    grid_spec = pltpu.PrefetchScalarGridSpec(
        num_scalar_prefetch=1,
        grid=(bh, tiles // GROUP_STEP),
        in_specs=([row_tile] + key_value(KA_BLK) + key_value(0)
                  + [pl.BlockSpec(memory_space=pl.ANY)]),
        out_specs=[row_tile, row_tile],
    )
    return pl.pallas_call(
        _moba_group_kernel,
        grid_spec=grid_spec,
        out_shape=[jax.ShapeDtypeStruct(q_grouped.shape, F32)] * 2,
        compiler_params=_cparams("parallel", "parallel"),
        name="moba_group",
    )(tile_blk, q_grouped, *([qk] * GROUP_STEP), *([v] * GROUP_STEP), after)


def _moba_merge_kernel(q_ref, k_ref, v_ref, po_ref, pl_ref, o_ref):
    t = MOBA_BLOCK
    causal = (lax.broadcasted_iota(jnp.int32, (t, t), 1)
              <= lax.broadcasted_iota(jnp.int32, (t, t), 0))
    for hd in range(A_HEADS):
        sl = slice(hd * HEAD_DIM, (hd + 1) * HEAD_DIM)
        o_own, lse_own = _block_attention(q_ref[:, sl], k_ref[:, sl], v_ref[:, sl], causal)
        outs = [o_own] + [po_ref[s, hd] for s in range(MOBA_TOPK)]
        lses = [lse_own] + [pl_ref[s, hd] for s in range(MOBA_TOPK)]
        top = functools.reduce(jnp.maximum, lses)
        weights = [jnp.exp2(l - top) for l in lses]
        num = sum(w * o for w, o in zip(weights, outs))
        o_ref[:, sl] = (num / sum(weights)).astype(o_ref.dtype)


def _moba_merge(qk, v, part_o, part_lse):
    bsz, seq, _ = qk.shape
    t = MOBA_BLOCK
    part = pl.BlockSpec((MOBA_TOPK, None, A_HEADS, t, HEAD_DIM), lambda b, i: (0, b, 0, i, 0))
    return pl.pallas_call(
        _moba_merge_kernel,
        grid=(bsz, seq // t),
        in_specs=[pl.BlockSpec((None, t, A_W), lambda b, i: (b, i, QA_BLK * LANES // A_W)),
                  pl.BlockSpec((None, t, A_W), lambda b, i: (b, i, KA_BLK * LANES // A_W)),
                  pl.BlockSpec((None, t, A_W), lambda b, i: (b, i, 0)),
                  part, part],
        out_specs=pl.BlockSpec((None, t, A_W), lambda b, i: (b, i, 0)),
        out_shape=jax.ShapeDtypeStruct((bsz, seq, A_W), BF16),
        compiler_params=_cparams("parallel", "parallel"),
        name="moba_merge",
    )(qk, qk, v, part_o, part_lse)


def _moba_regroup(qk, kmean):
    bsz, seq, _ = qk.shape
    bh = bsz * A_HEADS
    rows = _moba_tiles(seq) * MOBA_BLOCK
    ids, cnt, q_f32 = _moba_gate(qk, kmean)
    pos, tile_blk = _moba_routes(ids, cnt, seq)
    q_grouped = _sc_scatter_rows(q_f32.reshape(bh * seq, HEAD_DIM), pos, bh * rows)
    return q_grouped.reshape(bh, rows, HEAD_DIM), pos, tile_blk


def _moba_picked_blocks(q_grouped, pos, tile_blk, qk, v, after):
    bsz, seq, _ = qk.shape
    bh, rows, _ = q_grouped.shape
    o_g, lse_g = _moba_group_attention(q_grouped, tile_blk, qk, v, after)
    flat = pos.reshape(-1)
    back = lambda a: _sc_gather_rows(a.reshape(bh * rows, HEAD_DIM), flat).reshape(
        MOBA_TOPK, bsz, A_HEADS, seq, HEAD_DIM)
    return back(o_g), back(lse_g)


def _proj_dilated_kernel(h_ref, w_ref, c_ref, s_ref, q_ref, k_ref, v_ref, sc, *, d):
    acc = _dot(h_ref[...], w_ref[...])
    c, s = c_ref[...], s_ref[...]
    q_scale = HEAD_DIM ** -0.5 * LOG2E
    for j in range(acc.shape[1] // LANES):
        blk = acc[:, j * LANES:(j + 1) * LANES]
        if j < B_HEADS:
            blk = _rope128(blk, c, s) * q_scale
        elif j < 2 * B_HEADS:
            blk = _rope128(blk, c, s)
        sc[j] = blk
    rows = acc.shape[0] // d
    for r in range(d):
        for j in range(acc.shape[1] // LANES):
            dst = (q_ref, k_ref, v_ref)[j // B_HEADS]
            col = (j % B_HEADS) * LANES
            dst[r, :, col:col + LANES] = sc[j, pl.ds(r, rows, stride=d), :].astype(dst.dtype)


def _proj_dilated(h, w, rope_h, d, bsz, seq, tm=512):
    m, k = h.shape
    nt = seq // tm
    pos = pl.BlockSpec((tm, LANES), lambda i: (i % nt, 0))
    out = pl.BlockSpec((None, d, tm // d, B_W), lambda i: (i // nt, 0, i % nt, 0))
    return pl.pallas_call(
        functools.partial(_proj_dilated_kernel, d=d),
        grid=(m // tm,),
        in_specs=[pl.BlockSpec((tm, k), lambda i: (i, 0)), pl.BlockSpec(w.shape, lambda i: (0, 0)),
                  pos, pos],
        out_specs=[out] * 3,
        out_shape=[jax.ShapeDtypeStruct((bsz, d, seq // d, B_W), BF16)] * 3,
        scratch_shapes=[pltpu.VMEM((w.shape[1] // LANES, tm, LANES), F32)],
        compiler_params=_cparams("parallel"),
        name=f"proj_dilated_d{d}",
    )(h, w, *rope_h)


def _dilated_kernel(q_ref, kc_ref, kp_ref, vc_ref, vp_ref, o_ref, lse_ref, *, span):
    t, tp = q_ref.shape[0], kp_ref.shape[0]
    i = pl.program_id(2)
    shape = (2 * tp, tp)
    key_row = lax.broadcasted_iota(jnp.int32, shape, 0)
    dist = lax.broadcasted_iota(jnp.int32, shape, 1) + tp - key_row
    visible = (dist >= 0) & (dist <= span)
    bias = jnp.where(visible, 0.0, MASKED)
    bias_first = jnp.where(visible & ((key_row >= tp) | (i > 0)), 0.0, MASKED)
    ones = jnp.ones((BF16_ROWS, tp + t), BF16)

    def transposed(x):
        return x.astype(F32).T.astype(BF16)

    for j in range(B_HEADS):
        sl = slice(j * LANES, (j + 1) * LANES)
        qT = transposed(q_ref[:, sl])
        k_all = jnp.concatenate([kp_ref[:, sl], kc_ref[:, sl]], axis=0)
        vt_all = jnp.concatenate([transposed(vp_ref[:, sl]), transposed(vc_ref[:, sl])], axis=1)
        vt_all = jnp.concatenate([vt_all, ones], axis=0)
        outs, lses = [], []
        for u in range(t // tp):
            window = slice(u * tp, (u + 2) * tp)
            s = _dot(k_all[window, :], qT[:, u * tp:(u + 1) * tp])
            s = s + (bias_first if u == 0 else bias)
            m = jnp.max(s, axis=0, keepdims=True)
            p = jnp.exp2(s - m).astype(BF16)
            acc = _dot(vt_all[:, window], p)
            den = acc[V_DIM:V_DIM + 1, :]
            outs.append(acc[:V_DIM, :] / den)
            lses.append(m + jnp.log2(den))
        o_ref[:, sl] = jnp.concatenate(outs, axis=1).T
        lse = jnp.concatenate(lses, axis=1)
        lse_ref[:, sl] = jnp.broadcast_to(lse, (LANES, t)).T


def _dilated_attention(q, k, v, span, t=1024):
    bsz, d, length, _ = q.shape
    t = min(t, length)
    tp = B_QBLOCK
    assert span <= tp and t % tp == 0
    cur = pl.BlockSpec((None, None, t, B_W), lambda b, r, i: (b, r, i, 0))
    prev = pl.BlockSpec((None, None, tp, B_W),
                        lambda b, r, i: (b, r, jnp.maximum(i * (t // tp) - 1, 0), 0))
    return pl.pallas_call(
        functools.partial(_dilated_kernel, span=span),
        grid=(bsz, d, length // t),
        in_specs=[cur, cur, prev, cur, prev],
        out_specs=[cur, cur],
        out_shape=[jax.ShapeDtypeStruct(q.shape, F32)] * 2,
        compiler_params=_cparams("parallel", "parallel", "parallel"),
        name=f"dilated_d{d}",
    )(q, k, k, v, v)


def _natural_rows(ref, sc):
    d, rows = ref.shape[0], ref.shape[1]
    if d == 1:
        return ref[0]
    for r in range(d):
        for j in range(B_HEADS):
            sc[j, pl.ds(r, rows, stride=d), :] = ref[r, :, j * LANES:(j + 1) * LANES]
    return jnp.concatenate([sc[j] for j in range(B_HEADS)], axis=-1)


def _mixer_tail_kernel(x_ref, oa_ref, o0_ref, o1_ref, o2_ref, l0_ref, l1_ref, l2_ref, oc_ref,
                       g_ref, wpa_ref, wpb_ref, wpc_ref, wo_ref, y_ref, *scratch):
    o0, o1, o2, l0, l1, l2 = [
        _natural_rows(ref, sc)
        for ref, sc in zip((o0_ref, o1_ref, o2_ref, l0_ref, l1_ref, l2_ref), scratch)]
    mx = jnp.maximum(jnp.maximum(l0, l1), l2)
    e0, e1, e2 = jnp.exp2(l0 - mx), jnp.exp2(l1 - mx), jnp.exp2(l2 - mx)
    ob = (e0 * o0 + e1 * o1 + e2 * o2) / (e0 + e1 + e2)
    pa = _dot(oa_ref[...], wpa_ref[...])
    pb = _dot(ob.astype(BF16), wpb_ref[...])
    pc = _dot(oc_ref[...], wpc_ref[...])
    d = D_MODEL
    merged = (g_ref[:, 0:d].astype(F32) * pa + g_ref[:, d:2 * d].astype(F32) * pb
              + g_ref[:, 2 * d:3 * d].astype(F32) * pc)
    y_ref[...] = x_ref[...] + _dot(merged.astype(BF16), wo_ref[...])


def _mixer_tail(x, out_a, o_groups, lse_groups, out_c, gates, w_pa, w_pb, w_pc, w_o, seq, tm=256):
    m, d = x.shape
    nt = seq // tm
    row = lambda width: pl.BlockSpec((tm, width), lambda i: (i, 0))
    residue = lambda g: pl.BlockSpec((None, g.shape[1], tm // g.shape[1], B_W),
                                     lambda i: (i // nt, 0, i % nt, 0))
    weights = [_resident(w) for w in (w_pa, w_pb, w_pc, w_o)]
    groups = list(o_groups) + list(lse_groups)
    return pl.pallas_call(
        _mixer_tail_kernel,
        grid=(m // tm,),
        in_specs=([row(d), row(A_W)] + [residue(g) for g in groups]
                  + [row(C_W), row(3 * d)] + weights),
        out_specs=row(d),
        out_shape=jax.ShapeDtypeStruct((m, d), F32),
        scratch_shapes=[pltpu.VMEM((B_HEADS, tm, LANES), F32) for _ in groups],
        compiler_params=_cparams("parallel"),
        name="mixer_tail",
    )(x, out_a, *groups, out_c, gates, w_pa, w_pb, w_pc, w_o)


def _mem_kv_kernel(mem_ref, g_ref, wk_ref, wv_ref, k_ref, v_ref):
    memn = _rms(mem_ref[...], g_ref[...]).astype(BF16)
    k_ref[...] = _dot(memn, wk_ref[...]).astype(k_ref.dtype)
    v_ref[...] = _dot(memn, wv_ref[...]).astype(v_ref.dtype)


def _mem_kv(mem, g, wk, wv):
    bsz, n, d = mem.shape
    out = pl.BlockSpec((None, n, X_W), lambda b: (b, 0, 0))
    return pl.pallas_call(
        _mem_kv_kernel,
        grid=(bsz,),
        in_specs=[pl.BlockSpec((None, n, d), lambda b: (b, 0, 0)),
                  pl.BlockSpec((1, d), lambda b: (0, 0)),
                  pl.BlockSpec(wk.shape, lambda b: (0, 0)),
                  pl.BlockSpec(wv.shape, lambda b: (0, 0))],
        out_specs=[out, out],
        out_shape=[jax.ShapeDtypeStruct((bsz, n, X_W), BF16)] * 2,
        compiler_params=_cparams("parallel"),
        name="mem_kv",
    )(mem, g.reshape(1, d), wk, wv)


def _mem_attn_kernel(x_ref, g_ref, wq_ref, k_ref, v_ref, wo_ref, y_ref):
    x = x_ref[...]
    h = _rms(x, g_ref[...]).astype(BF16)
    q = (_dot(h, wq_ref[...]) * HEAD_DIM ** -0.5).astype(BF16)
    heads = []
    for hd in range(X_HEADS):
        sl = slice(hd * HEAD_DIM, (hd + 1) * HEAD_DIM)
        s = _dot_nt(q[:, sl], k_ref[:, sl])
        p = jnp.exp(s - jnp.max(s, axis=-1, keepdims=True))
        o = _dot(p.astype(BF16), v_ref[:, sl]) / jnp.sum(p, axis=-1, keepdims=True)
        heads.append(o.astype(BF16))
    y_ref[...] = x + _dot(jnp.concatenate(heads, axis=-1), wo_ref[...])


def _mem_attention(x, g, wq, kmem, vmem, wo, seq, tm=512):
    m, d = x.shape
    nt = seq // tm
    n = kmem.shape[1]
    kv = pl.BlockSpec((None, n, X_W), lambda i: (i // nt, 0, 0))
    return pl.pallas_call(
        _mem_attn_kernel,
        grid=(m // tm,),
        in_specs=[pl.BlockSpec((tm, d), lambda i: (i, 0)),
                  pl.BlockSpec((1, d), lambda i: (0, 0)),
                  pl.BlockSpec(wq.shape, lambda i: (0, 0)), kv, kv,
                  pl.BlockSpec(wo.shape, lambda i: (0, 0))],
        out_specs=pl.BlockSpec((tm, d), lambda i: (i, 0)),
        out_shape=jax.ShapeDtypeStruct((m, d), F32),
        compiler_params=_cparams("parallel"),
        name="mem_attention",
    )(x, g.reshape(1, d), wq, kmem, vmem, wo)


def _ffn_up_kernel(x_ref, halo_ref, g_ref, wg_ref, wv_ref, cwg_ref, cwv_ref, cbg_ref, cbv_ref,
                   act_ref, h_sc, *, tiles_per_seq):
    i = pl.program_id(0)
    tm = x_ref.shape[0]

    @pl.when(pl.program_id(1) == 0)
    def _():
        g = g_ref[...]
        keep = (i % tiles_per_seq != 0).astype(F32)
        h_sc[0:HALO, :] = (_rms(halo_ref[...], g) * keep).astype(h_sc.dtype)
        h_sc[HALO:, :] = _rms(x_ref[...], g).astype(h_sc.dtype)

    h = h_sc[...]

    def conv(w_ref, cw_ref, cb_ref):
        u = _dot(h, w_ref[...])
        c = cb_ref[...]
        for tap in range(CONV_W):
            lo = HALO - (CONV_W - 1) + tap
            c = c + cw_ref[tap:tap + 1, :] * u[lo:lo + tm, :]
        return c

    act = jax.nn.silu(conv(wg_ref, cwg_ref, cbg_ref)) * conv(wv_ref, cwv_ref, cbv_ref)
    act_ref[...] = act.astype(act_ref.dtype)


def _ffn_down_kernel(a_ref, w_ref, x_ref, y_ref):
    y_ref[...] = x_ref[...] + _dot(a_ref[...], w_ref[...])


def _conv_ffn(x, g, w_up, conv_w, conv_b, w_down, layer, seq):
    m, d = x.shape
    act = _ffn_up(x, g, w_up, conv_w, conv_b, layer, seq)
    tm, tn = 1024, FFN_TF
    return pl.pallas_call(
        _ffn_down_kernel,
        grid=(m // tm, d // tn),
        in_specs=[pl.BlockSpec((tm, D_FF), lambda i, j: (i, 0)),
                  pl.BlockSpec((None, D_FF, tn), lambda i, j: (layer, 0, j)),
                  pl.BlockSpec((tm, tn), lambda i, j: (i, j))],
        out_specs=pl.BlockSpec((tm, tn), lambda i, j: (i, j)),
        out_shape=jax.ShapeDtypeStruct((m, d), F32),
        compiler_params=_cparams("parallel", "parallel"),
        name="ffn_down",
    )(act, w_down, x)


def _ffn_up(x, g, w_up, conv_w, conv_b, layer, seq, tm=1024, tf=FFN_TF):
    m, d = x.shape
    nf = D_FF_PAD // tf
    halo_blocks = tm // HALO
    return pl.pallas_call(
        functools.partial(_ffn_up_kernel, tiles_per_seq=seq // tm),
        grid=(m // tm, nf),
        in_specs=[pl.BlockSpec((tm, d), lambda i, f: (i, 0)),
                  pl.BlockSpec((HALO, d), lambda i, f: (jnp.maximum(i * halo_blocks - 1, 0), 0)),
                  pl.BlockSpec((1, d), lambda i, f: (0, 0)),
                  pl.BlockSpec((None, d, tf), lambda i, f: (layer, 0, f)),
                  pl.BlockSpec((None, d, tf), lambda i, f: (layer, 0, f + nf)),
                  pl.BlockSpec((CONV_W, tf), lambda i, f: (0, f)),
                  pl.BlockSpec((CONV_W, tf), lambda i, f: (0, f + nf)),
                  pl.BlockSpec((1, tf), lambda i, f: (0, f)),
                  pl.BlockSpec((1, tf), lambda i, f: (0, f + nf))],
        out_specs=pl.BlockSpec((tm, tf), lambda i, f: (i, f)),
        out_shape=jax.ShapeDtypeStruct((m, D_FF_PAD), BF16),
        scratch_shapes=[pltpu.VMEM((HALO + tm, d), BF16)],
        compiler_params=_cparams("parallel", "arbitrary"),
        name="ffn_up",
    )(x, x, g.reshape(1, d), w_up, w_up, conv_w, conv_w, conv_b, conv_b)


def _rope_tables(seq):
    def angles(dim):
        inv_freq = jnp.exp(jnp.arange(0, dim, 2, dtype=F32) * (-math.log(ROPE_THETA) / dim))
        ang = jnp.arange(seq, dtype=F32)[:, None] * inv_freq[None, :]
        return jnp.cos(ang), jnp.sin(ang)

    cos_h, sin_h = angles(HEAD_DIM)
    rope_h = (jnp.concatenate([cos_h, cos_h], axis=-1), jnp.concatenate([-sin_h, sin_h], axis=-1))
    cos_r, sin_r = angles(ROPE_DIM)
    z = jnp.zeros_like(cos_r)
    rope_r = (jnp.concatenate([cos_r, cos_r, z, z], axis=-1),
              jnp.concatenate([-sin_r, z, z, z], axis=-1),
              jnp.concatenate([z, sin_r, z, z], axis=-1))
    return rope_h, rope_r


def _split_in(w_in):
    return [w_in[:, IN_OFFSETS[k]:IN_OFFSETS[k + 1]] for k in range(len(IN_WIDTHS))]


def _pad_cols(w, width):
    return jnp.pad(w, ((0, 0), (0, width - w.shape[1])))


def _layer_params(w_in, w_uq, w_ukv, conv_w, conv_b):
    qa, ka, va, qb, kb, vb, cq, ckv, kr, gates = _split_in(w_in)
    w_qk = jnp.concatenate([qa, ka], axis=1).astype(BF16)
    group_cols = lambda w, g: w[:, g * B_W:(g + 1) * B_W]
    w_b = [jnp.concatenate([group_cols(qb, g), group_cols(kb, g), group_cols(vb, g)],
                           axis=1).astype(BF16) for g in range(len(B_GROUPS))]
    w_down_in = jnp.concatenate([cq, ckv, _pad_cols(kr, LANES)], axis=1).astype(BF16)
    uq = w_uq.reshape(Q_LORA, C_HEADS, NOPE_DIM + ROPE_DIM)
    uq = jnp.pad(uq, ((0, 0), (0, 0), (0, C_QK - NOPE_DIM - ROPE_DIM)))
    ukv = w_ukv.reshape(KV_LORA, C_HEADS, NOPE_DIM + V_DIM)
    return dict(
        w_qk=w_qk, w_va=va.astype(BF16), w_b=w_b, w_gates=gates.astype(BF16),
        w_down_in=w_down_in,
        w_uq=uq.reshape(Q_LORA, C_HEADS * C_QK).astype(BF16),
        w_uk=ukv[:, :, :NOPE_DIM].reshape(KV_LORA, C_HEADS * NOPE_DIM).astype(BF16),
        w_uv=ukv[:, :, NOPE_DIM:].reshape(KV_LORA, C_W).astype(BF16),
        conv_w=_pad_ff_halves(conv_w),
        conv_b=_pad_ff_halves(conv_b.reshape(1, -1)),
    )


def _pad_ff_halves(w):
    halves = w.reshape(w.shape[:-1] + (2, D_FF))
    pad = [(0, 0)] * (halves.ndim - 1) + [(0, D_FF_PAD - D_FF)]
    return jnp.pad(halves, pad).reshape(w.shape[:-1] + (2 * D_FF_PAD,))


def _qk_col_scale():
    q_scale = HEAD_DIM ** -0.5
    parts = [jnp.full((A_W,), q_scale * LOG2E, F32), jnp.ones((A_W,), F32)]
    return jnp.concatenate(parts).reshape(1, QK_W)


def _mixer(x, g_mix, p, g_cq, g_ckv, w_pa, w_pb, w_pc, w_o, rope_h, rope_r, bsz, seq):
    m = x.shape[0]
    h = _rmsnorm(x, g_mix, BF16)
    qk = _matmul(h, p["w_qk"], _mm_rope_kernel, BF16, 1024, 1024, seq=seq,
                 extras=(("col", _qk_col_scale()), ("pos", rope_h[0]), ("pos", rope_h[1])),
                 name="proj_qk_rope")
    qk3 = qk.reshape(bsz, seq, QK_W)
    q_grouped, pos, tile_blk = _moba_regroup(qk3, _kmean(qk3))
    v_a = _matmul(h, p["w_va"], _mm_plain_kernel, BF16, 1024, A_W, name="proj_va")
    gates = _matmul(h, p["w_gates"], _mm_sigmoid_kernel, BF16, 1024, 1024, name="proj_gates")
    v_a3 = v_a.reshape(bsz, seq, A_W)
    part_o, part_lse = _moba_picked_blocks(q_grouped, pos, tile_blk, qk3, v_a3, after=gates)
    cq, ckv, kr = _mla_down(h, p["w_down_in"], g_cq, g_ckv, rope_r, seq)
    q_c = _mla_q(cq, p["w_uq"], rope_r, seq)
    k_c, vt_c = _mla_kv(ckv, kr, p["w_uk"], p["w_uv"], bsz, seq)
    groups = []
    for (window, d), w_g in zip(B_GROUPS, p["w_b"]):
        q_g, k_g, v_g = _proj_dilated(h, w_g, rope_h, d, bsz, seq)
        groups.append(_dilated_attention(q_g, k_g, v_g, window // d))
    out_c = _flash_attention(q_c.reshape(bsz, seq, -1), k_c.reshape(bsz, seq, -1), vt_c,
                             C_HEADS, C_QK).reshape(m, C_W)
    out_a = _moba_merge(qk3, v_a3, part_o, part_lse).reshape(m, A_W)
    return _mixer_tail(x, out_a, [g[0] for g in groups], [g[1] for g in groups], out_c, gates,
                       w_pa.astype(BF16), w_pb.astype(BF16), w_pc.astype(BF16), w_o.astype(BF16),
                       seq)


def kernel(x, mem, g_mix, w_in, g_cq, g_ckv, w_uq, w_ukv, w_pa, w_pb, w_pc, w_o, g_mem, g_memkv,
           w_xq, w_xk, w_xv, w_xo, g_ffn, w_up, conv_w, conv_b, w_down, g_final):
    bsz, seq, d = x.shape
    rope_h, rope_r = _rope_tables(seq)
    xf = x.reshape(bsz * seq, d)
    w_down = w_down.astype(BF16)
    w_up = _pad_ff_halves(w_up.astype(BF16))
    for l in range(DEPTH):
        p = _layer_params(w_in[l], w_uq[l], w_ukv[l], conv_w[l], conv_b[l])
        xf = _mixer(xf, g_mix[l], p, g_cq[l], g_ckv[l], w_pa[l], w_pb[l], w_pc[l], w_o[l],
                    rope_h, rope_r, bsz, seq)
        kmem, vmem = _mem_kv(mem, g_memkv[l], w_xk[l].astype(BF16), w_xv[l].astype(BF16))
        xf = _mem_attention(xf, g_mem[l], w_xq[l].astype(BF16), kmem, vmem,
                            w_xo[l].astype(BF16), seq)
        xf = _conv_ffn(xf, g_ffn[l], w_up, p["conv_w"], p["conv_b"], w_down, l, seq)
    return _rmsnorm(xf, g_final, F32).reshape(bsz, seq, d)
```

```python
import functools
import math

import jax
import jax.numpy as jnp
import numpy as np
from jax import lax
from jax.experimental import pallas as pl
from jax.experimental.pallas import tpu as pltpu
from jax.experimental.pallas import tpu_sc as plsc

F32 = jnp.float32
BF16 = jnp.bfloat16

LANES = 128
SUBLANES = 8
VMEM_LIMIT = 56 * 1024 * 1024

D_MODEL = 2048
DEPTH = 2
HEAD_DIM = 128
ROPE_THETA = 10000.0
EPS = 1e-6

A_HEADS = 4
MOBA_BLOCK = 256
MOBA_TOPK = 3

B_GROUPS = ((128, 1), (512, 4), (2048, 16))
B_HEADS = 4
B_QBLOCK = 128

C_HEADS = 8
Q_LORA = 1536
KV_LORA = 512
NOPE_DIM = 128
ROPE_DIM = 64
V_DIM = 128

X_HEADS = 4
D_FF = 5504
CONV_W = 3

A_W = A_HEADS * HEAD_DIM
B_QKV_W = len(B_GROUPS) * B_HEADS * HEAD_DIM
B_W = B_HEADS * HEAD_DIM
C_W = C_HEADS * V_DIM
X_W = X_HEADS * HEAD_DIM
IN_WIDTHS = (A_W, A_W, A_W, B_QKV_W, B_QKV_W, B_QKV_W, Q_LORA, KV_LORA, ROPE_DIM, 3 * D_MODEL)
IN_OFFSETS = tuple(int(o) for o in np.cumsum((0,) + IN_WIDTHS))

QK_W = 2 * A_W
QA_BLK, KA_BLK = 0, A_W // LANES

C_QK = 2 * LANES
MASKED = -1e30
LOG2E = math.log2(math.e)
BF16_ROWS = 16
VT_ROWS = V_DIM + BF16_ROWS
GROUP_STEP = 8
SC_WINDOW = 128

D_FF_PAD = 5632
FFN_TF = 512
HALO = SUBLANES


def _cparams(*sem):
    return pltpu.CompilerParams(dimension_semantics=sem, vmem_limit_bytes=VMEM_LIMIT)


def _resident(arr):
    zeros = (0,) * arr.ndim
    return pl.BlockSpec(arr.shape, lambda *_: zeros, pipeline_mode=pl.Buffered(1))


def _dot(a, b):
    return jnp.dot(a, b, preferred_element_type=F32)


def _dot_nt(a, b):
    return lax.dot_general(a, b, (((1,), (1,)), ((), ())), preferred_element_type=F32)


def _rms(x, g):
    return x * lax.rsqrt(jnp.mean(x * x, axis=-1, keepdims=True) + EPS) * g


def _rmsnorm_kernel(x_ref, g_ref, o_ref):
    o_ref[...] = _rms(x_ref[...], g_ref[...]).astype(o_ref.dtype)


def _rmsnorm(x, g, out_dtype, tm=512):
    m, d = x.shape
    return pl.pallas_call(
        _rmsnorm_kernel,
        grid=(m // tm,),
        in_specs=[pl.BlockSpec((tm, d), lambda i: (i, 0)),
                  pl.BlockSpec((1, d), lambda i: (0, 0))],
        out_specs=pl.BlockSpec((tm, d), lambda i: (i, 0)),
        out_shape=jax.ShapeDtypeStruct((m, d), out_dtype),
        compiler_params=_cparams("parallel"),
        name="rmsnorm",
    )(x, g.reshape(1, d))


def _rope128(x, c, s):
    return x * c + pltpu.roll(x, HEAD_DIM // 2, 1) * s


def _rope64(x, c, sa, sb):
    half = ROPE_DIM // 2
    return x * c + pltpu.roll(x, LANES - half, 1) * sa + pltpu.roll(x, half, 1) * sb


def _mm_plain_kernel(a_ref, w_ref, o_ref):
    o_ref[...] = _dot(a_ref[...], w_ref[...]).astype(o_ref.dtype)


def _mm_sigmoid_kernel(a_ref, w_ref, o_ref):
    o_ref[...] = jax.nn.sigmoid(_dot(a_ref[...], w_ref[...])).astype(o_ref.dtype)


def _mm_rope_kernel(a_ref, w_ref, cs_ref, c_ref, s_ref, o_ref):
    acc = _dot(a_ref[...], w_ref[...])
    c = c_ref[...]
    s = s_ref[...]
    for j in range(acc.shape[1] // LANES):
        sl = slice(j * LANES, (j + 1) * LANES)
        o_ref[:, sl] = (_rope128(acc[:, sl], c, s) * cs_ref[:, sl]).astype(o_ref.dtype)


def _matmul(a, w, kernel, out_dtype, tm, tn, seq=None, extras=(), name="matmul"):
    m, k = a.shape
    n = w.shape[1]
    in_specs = [pl.BlockSpec((tm, k), lambda i, j: (i, 0)),
                pl.BlockSpec((k, tn), lambda i, j: (0, j))]
    args = [a, w]
    for kind, arr in extras:
        if kind == "col":
            in_specs.append(pl.BlockSpec((1, tn), lambda i, j: (0, j)))
        else:
            nt = seq // tm
            in_specs.append(pl.BlockSpec((tm, LANES), lambda i, j: (i % nt, 0)))
        args.append(arr)
    return pl.pallas_call(
        kernel,
        grid=(m // tm, n // tn),
        in_specs=in_specs,
        out_specs=pl.BlockSpec((tm, tn), lambda i, j: (i, j)),
        out_shape=jax.ShapeDtypeStruct((m, n), out_dtype),
        compiler_params=_cparams("parallel", "parallel"),
        name=name,
    )(*args)


def _mla_down_kernel(h_ref, w_ref, gq_ref, gkv_ref, c_ref, sa_ref, sb_ref,
                     cq_ref, ckv_ref, kr_ref):
    acc = _dot(h_ref[...], w_ref[...])
    cq_ref[...] = _rms(acc[:, :Q_LORA], gq_ref[...]).astype(cq_ref.dtype)
    ckv_ref[...] = _rms(acc[:, Q_LORA:Q_LORA + KV_LORA], gkv_ref[...]).astype(ckv_ref.dtype)
    kr = acc[:, Q_LORA + KV_LORA:]
    kr_ref[...] = _rope64(kr, c_ref[...], sa_ref[...], sb_ref[...]).astype(kr_ref.dtype)


def _mla_down(h, w, g_cq, g_ckv, rope_r, seq, tm=512):
    m, k = h.shape
    n = w.shape[1]
    nt = seq // tm
    row = lambda width: pl.BlockSpec((tm, width), lambda i: (i, 0))
    full = lambda r, c: pl.BlockSpec((r, c), lambda i: (0, 0))
    pos = pl.BlockSpec((tm, LANES), lambda i: (i % nt, 0))
    return pl.pallas_call(
        _mla_down_kernel,
        grid=(m // tm,),
        in_specs=[row(k), full(k, n), full(1, Q_LORA), full(1, KV_LORA), pos, pos, pos],
        out_specs=[row(Q_LORA), row(KV_LORA), row(LANES)],
        out_shape=[jax.ShapeDtypeStruct((m, Q_LORA), BF16),
                   jax.ShapeDtypeStruct((m, KV_LORA), BF16),
                   jax.ShapeDtypeStruct((m, LANES), BF16)],
        compiler_params=_cparams("parallel"),
        name="mla_down",
    )(h, w, g_cq.reshape(1, -1), g_ckv.reshape(1, -1), *rope_r)


def _mla_q_kernel(cq_ref, w_ref, c_ref, sa_ref, sb_ref, q_ref, *, scale):
    acc = _dot(cq_ref[...], w_ref[...])
    c, sa, sb = c_ref[...], sa_ref[...], sb_ref[...]
    for hd in range(C_HEADS):
        lo = hd * C_QK
        q_ref[:, lo:lo + LANES] = (acc[:, lo:lo + LANES] * scale).astype(q_ref.dtype)
        rope = _rope64(acc[:, lo + LANES:lo + C_QK], c, sa, sb)
        q_ref[:, lo + LANES:lo + C_QK] = (rope * scale).astype(q_ref.dtype)


def _mla_q(cq, w, rope_r, seq, tm=512):
    m, k = cq.shape
    n = w.shape[1]
    nt = seq // tm
    pos = pl.BlockSpec((tm, LANES), lambda i: (i % nt, 0))
    return pl.pallas_call(
        functools.partial(_mla_q_kernel, scale=(NOPE_DIM + ROPE_DIM) ** -0.5 * LOG2E),
        grid=(m // tm,),
        in_specs=[pl.BlockSpec((tm, k), lambda i: (i, 0)),
                  pl.BlockSpec((k, n), lambda i: (0, 0)), pos, pos, pos],
        out_specs=pl.BlockSpec((tm, n), lambda i: (i, 0)),
        out_shape=jax.ShapeDtypeStruct((m, n), BF16),
        compiler_params=_cparams("parallel"),
        name="mla_q",
    )(cq, w, *rope_r)


def _store_vt(v, vt_ref):
    vt = v.T
    for hd in range(vt_ref.shape[0]):
        vt_ref[hd, 0:V_DIM, :] = vt[hd * V_DIM:(hd + 1) * V_DIM, :].astype(vt_ref.dtype)
        vt_ref[hd, V_DIM:VT_ROWS, :] = jnp.ones((VT_ROWS - V_DIM, vt.shape[1]), vt_ref.dtype)


def _mla_kv_kernel(ckv_ref, kr_ref, wk_ref, wv_ref, k_ref, vt_ref):
    ckv = ckv_ref[...]
    kn = _dot(ckv, wk_ref[...])
    kr = kr_ref[...]
    for hd in range(C_HEADS):
        lo = hd * C_QK
        k_ref[:, lo:lo + LANES] = kn[:, hd * LANES:(hd + 1) * LANES].astype(k_ref.dtype)
        k_ref[:, lo + LANES:lo + C_QK] = kr
    _store_vt(_dot(ckv, wv_ref[...]), vt_ref)


def _mla_kv(ckv, kr, wk, wv, bsz, seq, tm=512):
    m, k = ckv.shape
    nt = seq // tm
    return pl.pallas_call(
        _mla_kv_kernel,
        grid=(m // tm,),
        in_specs=[pl.BlockSpec((tm, k), lambda i: (i, 0)),
                  pl.BlockSpec((tm, LANES), lambda i: (i, 0)),
                  pl.BlockSpec(wk.shape, lambda i: (0, 0)),
                  pl.BlockSpec(wv.shape, lambda i: (0, 0))],
        out_specs=[pl.BlockSpec((tm, C_HEADS * C_QK), lambda i: (i, 0)),
                   pl.BlockSpec((None, C_HEADS, VT_ROWS, tm), lambda i: (i // nt, 0, 0, i % nt))],
        out_shape=[jax.ShapeDtypeStruct((m, C_HEADS * C_QK), BF16),
                   jax.ShapeDtypeStruct((bsz, C_HEADS, VT_ROWS, seq), BF16)],
        compiler_params=_cparams("parallel"),
        name="mla_kv",
    )(ckv, kr, wk, wv)


def _attend_chunks(qT, k_ref, vt_ref, scratch, *, tk, n_full, mask_tail, tail_steps, tail_col,
                   unroll):
    m_sc, acc_sc, sa_sc, sb_sc, pa_sc, pb_sc, ala_sc, alb_sc, mxa_sc, mxb_sc = scratch
    s_bufs = (sa_sc, sb_sc)
    p_bufs = (pa_sc, pb_sc)
    al_bufs = (ala_sc, alb_sc)
    mx_bufs = (mxa_sc, mxb_sc)
    last_chunk = k_ref.shape[0] // tk - 1
    m_sc[...] = jnp.full(m_sc.shape, MASKED, F32)
    acc_sc[...] = jnp.zeros(acc_sc.shape, F32)
    for p_ref, al_ref in zip(p_bufs, al_bufs):
        p_ref[...] = jnp.zeros(p_ref.shape, p_ref.dtype)
        al_ref[...] = jnp.ones(al_ref.shape, F32)

    def rows(c):
        return pl.ds(pl.multiple_of(jnp.clip(c, 0, last_chunk) * tk, tk), tk)

    def scores(c, slot, col=0):
        sT = _dot(k_ref[rows(c), :], qT[:, col:])
        s_bufs[slot][:, col:] = sT
        mx_bufs[slot][:, col:] = jnp.max(sT, axis=0, keepdims=True)

    def flush(c, slot, col=0):
        acc_sc[:, col:] = (al_bufs[slot][:, col:] * acc_sc[:, col:]
                           + _dot(vt_ref[:, rows(c)], p_bufs[slot][:, col:]))

    def softmax(sT, top, slot, col):
        m_old = m_sc[:, col:]
        m_new = jnp.maximum(m_old, top)
        al_bufs[slot][:, col:] = jnp.exp2(m_old - m_new)
        p_bufs[slot][:, col:] = jnp.exp2(sT - m_new).astype(p_bufs[slot].dtype)
        m_sc[:, col:] = m_new

    def step(tau, slot, mask, cols):
        col_flush, col, col_next = cols
        flush(tau - 2, slot, col_flush)
        if col_next is not None:
            scores(tau + 1, 1 - slot, col_next)
        sT = s_bufs[slot][:, col:]
        if mask is None:
            softmax(sT, mx_bufs[slot][:, col:], slot, col)
        else:
            sT = mask(sT, tau, col)
            softmax(sT, jnp.max(sT, axis=0, keepdims=True), slot, col)

    def full_steps(tau0, count):
        for j in range(count):
            step(tau0 + j, j % 2, None, (0, 0, 0))

    scores(0, 0)
    trips = n_full // unroll
    lax.fori_loop(0, trips, lambda u, c: (full_steps(unroll * u, unroll), c)[1], 0)
    done = unroll * trips
    pairs = (n_full - done) // 2
    lax.fori_loop(0, pairs, lambda u, c: (full_steps(done + 2 * u, 2), c)[1], 0)
    tau = done + 2 * pairs
    cols = [tail_col(j) for j in range(tail_steps)]
    for j in range(tail_steps):
        col_flush = cols[j - 2] if j >= 2 else 0
        col_next = cols[j + 1] if j + 1 < tail_steps else None
        step(tau + j, j % 2, mask_tail, (col_flush, cols[j], col_next))
    flush(tau + tail_steps - 2, 0, cols[-2])
    flush(tau + tail_steps - 1, 1, cols[-1])
    acc = acc_sc[...]
    return acc[:V_DIM, :] / acc[V_DIM:V_DIM + 1, :]


def _attend_scratch(tq, tk):
    return [pltpu.VMEM((1, tq), F32), pltpu.VMEM((VT_ROWS, tq), F32),
            pltpu.VMEM((tk, tq), F32), pltpu.VMEM((tk, tq), F32),
            pltpu.VMEM((tk, tq), BF16), pltpu.VMEM((tk, tq), BF16),
            pltpu.VMEM((1, tq), F32), pltpu.VMEM((1, tq), F32),
            pltpu.VMEM((1, tq), F32), pltpu.VMEM((1, tq), F32)]


def _transpose_q(q_ref):
    return q_ref[...].astype(F32).T.astype(BF16)


def _flash_kernel(q_ref, k_ref, vt_ref, o_ref, *scratch, tq, tk):
    i = pl.program_id(2)

    def causal(sT, c, col):
        key = lax.broadcasted_iota(jnp.int32, sT.shape, 0) + c * tk
        qry = lax.broadcasted_iota(jnp.int32, sT.shape, 1) + (i * tq + col)
        return jnp.where(key <= qry, sT, MASKED)

    per_tile = tq // tk
    oT = _attend_chunks(_transpose_q(q_ref), k_ref, vt_ref, scratch, tk=tk, n_full=i * per_tile,
                        mask_tail=causal, tail_steps=per_tile, tail_col=lambda j: j * tk,
                        unroll=8)
    o_ref[...] = oT.T.astype(o_ref.dtype)


def _flash_attention(q, k, vt, heads, qk_w, tq=1024, tk=256):
    bsz, seq, _ = q.shape
    assert tq % (2 * tk) == 0 and seq % tq == 0
    return pl.pallas_call(
        functools.partial(_flash_kernel, tq=tq, tk=tk),
        grid=(bsz, heads, seq // tq),
        in_specs=[pl.BlockSpec((None, tq, qk_w), lambda b, h, i: (b, i, h)),
                  pl.BlockSpec((None, seq, qk_w), lambda b, h, i: (b, 0, h)),
                  pl.BlockSpec((None, None, VT_ROWS, seq), lambda b, h, i: (b, h, 0, 0))],
        out_specs=pl.BlockSpec((None, tq, V_DIM), lambda b, h, i: (b, i, h)),
        out_shape=jax.ShapeDtypeStruct((bsz, seq, heads * V_DIM), BF16),
        scratch_shapes=_attend_scratch(tq, tk),
        compiler_params=_cparams("parallel", "parallel", "arbitrary"),
        name="mla_flash",
    )(q, k, vt)


def _kmean_kernel(k_ref, o_ref):
    k = k_ref[...].astype(F32)
    o_ref[...] = jnp.mean(k.reshape(SUBLANES, MOBA_BLOCK, k.shape[-1]), axis=1)


def _kmean(qk):
    bsz, seq, _ = qk.shape
    rows = SUBLANES * MOBA_BLOCK
    return pl.pallas_call(
        _kmean_kernel,
        grid=(bsz, seq // rows),
        in_specs=[pl.BlockSpec((None, rows, A_W), lambda b, i: (b, i, KA_BLK * LANES // A_W))],
        out_specs=pl.BlockSpec((None, SUBLANES, A_W), lambda b, i: (b, i, 0)),
        out_shape=jax.ShapeDtypeStruct((bsz, seq // MOBA_BLOCK, A_W), F32),
        compiler_params=_cparams("parallel", "parallel"),
        name="moba_kmean",
    )(qk)


def _block_attention(q, k, v, visible=None):
    s = _dot_nt(q, k)
    if visible is not None:
        s = jnp.where(visible, s, MASKED)
    m = jnp.max(s, axis=-1, keepdims=True)
    p = jnp.exp2(s - m).astype(BF16)
    v_ones = jnp.concatenate([v, jnp.ones((v.shape[0], LANES), v.dtype)], axis=-1)
    acc = _dot(p, v_ones)
    den = acc[:, V_DIM:]
    return acc[:, :V_DIM] / den, m + jnp.log2(den)


def _moba_gate_kernel(q_ref, km_ref, ids_ref, cnt_ref, qf_ref):
    t = MOBA_BLOCK
    i = pl.program_id(1)
    nb = km_ref.shape[0]
    blk = lax.broadcasted_iota(jnp.int32, (nb, t), 0)
    neg_inf = jnp.float32(-jnp.inf)
    not_after = (lax.broadcasted_iota(jnp.int32, (t, t), 0)
                 <= lax.broadcasted_iota(jnp.int32, (t, t), 1))
    upper = jnp.where(not_after, 1.0, 0.0).astype(BF16)
    ones = jnp.ones((SUBLANES, t), BF16)
    for hd in range(A_HEADS):
        sl = slice(hd * HEAD_DIM, (hd + 1) * HEAD_DIM)
        q = q_ref[:, sl].astype(F32)
        qf_ref[hd] = q
        qT = q.T.astype(BF16)
        km = km_ref[:, sl]
        km_hi = km.astype(BF16)
        km_lo = (km - km_hi.astype(F32)).astype(BF16)
        g = jnp.where(blk < i, _dot(km_hi, qT) + _dot(km_lo, qT), neg_inf)
        picks, ranks, counts = [], [], []
        for _ in range(MOBA_TOPK):
            mx = jnp.max(g, axis=0, keepdims=True)
            is_max = (g == mx) & (mx > neg_inf)
            first = jnp.min(jnp.where(is_max, blk, nb), axis=0, keepdims=True)
            pick = blk == first
            g = jnp.where(pick, neg_inf, g)
            onehot = jnp.where(pick, 1.0, 0.0).astype(BF16)
            before = _dot(onehot, upper)
            rank = jnp.sum(jnp.where(pick, before - 1.0, 0.0), axis=0, keepdims=True)
            picks.append(first)
            ranks.append(rank.astype(jnp.int32))
            counts.append(_dot_nt(ones, onehot)[0:1, :])
        pad_i = jnp.zeros((SUBLANES - 2 * MOBA_TOPK, t), jnp.int32)
        ids_ref[hd] = jnp.concatenate(picks + ranks + [pad_i], axis=0)
        pad_f = jnp.zeros((SUBLANES - MOBA_TOPK, nb), F32)
        cnt_ref[hd] = jnp.concatenate(counts + [pad_f], axis=0)


def _moba_gate(qk, kmean):
    bsz, seq, _ = qk.shape
    t = MOBA_BLOCK
    nb = seq // t
    return pl.pallas_call(
        _moba_gate_kernel,
        grid=(bsz, nb),
        in_specs=[pl.BlockSpec((None, t, A_W), lambda b, i: (b, i, QA_BLK * LANES // A_W)),
                  pl.BlockSpec((None, nb, A_W), lambda b, i: (b, 0, 0))],
        out_specs=[pl.BlockSpec((None, A_HEADS, SUBLANES, t), lambda b, i: (b, 0, 0, i)),
                   pl.BlockSpec((None, A_HEADS, None, SUBLANES, nb), lambda b, i: (b, 0, i, 0, 0)),
                   pl.BlockSpec((None, A_HEADS, t, HEAD_DIM), lambda b, i: (b, 0, i, 0))],
        out_shape=[jax.ShapeDtypeStruct((bsz, A_HEADS, SUBLANES, seq), jnp.int32),
                   jax.ShapeDtypeStruct((bsz, A_HEADS, nb, SUBLANES, nb), F32),
                   jax.ShapeDtypeStruct((bsz, A_HEADS, seq, HEAD_DIM), F32)],
        compiler_params=_cparams("parallel", "parallel"),
        name="moba_gate",
    )(qk, kmean)


def _moba_routes(ids, cnt, seq):
    bsz, heads = ids.shape[:2]
    bh, t = bsz * heads, MOBA_BLOCK
    nb = seq // t
    tiles = _moba_tiles(seq)
    picks = ids[:, :, 0:MOBA_TOPK, :].reshape(bh, MOBA_TOPK, nb, t)
    ranks = ids[:, :, MOBA_TOPK:2 * MOBA_TOPK, :].reshape(bh, MOBA_TOPK, nb, t)
    per_tile = cnt[:, :, :, 0:MOBA_TOPK, :].astype(jnp.int32).reshape(bh, nb * MOBA_TOPK, nb)
    before = jnp.cumsum(per_tile, axis=1) - per_tile
    total = jnp.sum(per_tile, axis=1)
    padded = -(-total // t) * t
    ends = jnp.cumsum(padded, axis=1)
    base = before + (ends - padded)[:, None, :]
    base = base.reshape(bh, nb, MOBA_TOPK, nb).transpose(0, 2, 1, 3)
    onehot = picks[..., None] == jnp.arange(nb)
    pos = jnp.sum(jnp.where(onehot, base[:, :, :, None, :], 0), axis=-1) + ranks
    pos = jnp.where(picks < nb, pos, (tiles - 1) * t)
    pos = pos + (jnp.arange(bh, dtype=jnp.int32) * (tiles * t))[:, None, None, None]
    pos = pos.reshape(bh, MOBA_TOPK, seq).transpose(1, 0, 2).reshape(MOBA_TOPK, bh * seq)
    tile_start = jnp.arange(tiles, dtype=jnp.int32) * t
    tile_blk = jnp.sum(tile_start[None, :, None] >= ends[:, None, :], axis=-1)
    tile_blk = jnp.where(tile_start[None, :] < ends[:, -1:], tile_blk, -1)
    return pos.astype(jnp.int32), tile_blk.astype(jnp.int32)


def _moba_tiles(seq):
    nb = seq // MOBA_BLOCK
    return -(-(MOBA_TOPK * nb + nb + 1) // GROUP_STEP) * GROUP_STEP


def _sc_mesh():
    return plsc.VectorSubcoreMesh(core_axis_name="core", subcore_axis_name="subcore")


def _sc_scatter_rows(x, idx, rows):
    slots, n = idx.shape
    d = x.shape[1]

    @pl.kernel(out_type=jax.ShapeDtypeStruct((rows, d), x.dtype), mesh=_sc_mesh())
    def scatter(x_hbm, i_hbm, o_hbm):
        def body(x_vmem, i_vmem):
            pltpu.sync_copy(x_vmem, o_hbm.at[i_vmem.at[0]])

        pltpu.emit_pipeline(
            body, grid=(slots, n // SC_WINDOW),
            in_specs=[pl.BlockSpec((SC_WINDOW, d), lambda s, i: (i, 0)),
                      pl.BlockSpec((1, SC_WINDOW), lambda s, i: (s, i))],
            out_specs=[],
            core_axis_name=("core", "subcore"),
            dimension_semantics=(pltpu.PARALLEL, pltpu.PARALLEL),
        )(x_hbm, i_hbm)

    return scatter(x, idx)


def _sc_gather_rows(x, idx):
    n = idx.shape[0]
    d = x.shape[1]

    @pl.kernel(out_type=jax.ShapeDtypeStruct((n, d), x.dtype), mesh=_sc_mesh())
    def gather(x_hbm, i_hbm, o_hbm):
        def body(i_vmem, o_vmem):
            pltpu.sync_copy(x_hbm.at[i_vmem.at[0]], o_vmem)

        pltpu.emit_pipeline(
            body, grid=(n // SC_WINDOW,),
            in_specs=[pl.BlockSpec((1, SC_WINDOW), lambda i: (0, i))],
            out_specs=[pl.BlockSpec((SC_WINDOW, d), lambda i: (i, 0))],
            core_axis_name=("core", "subcore"),
            dimension_semantics=(pltpu.PARALLEL,),
        )(i_hbm, o_hbm)

    return gather(x, idx.reshape(1, n))


def _moba_group_kernel(tb_ref, q_ref, *refs):
    t = MOBA_BLOCK
    k_refs, v_refs = refs[:GROUP_STEP], refs[GROUP_STEP:2 * GROUP_STEP]
    o_ref, lse_ref = refs[-2:]
    g, step = pl.program_id(0), pl.program_id(1)
    first = step * GROUP_STEP

    @pl.when(tb_ref[g, first] < 0)
    def _():
        o_ref[...] = jnp.zeros(o_ref.shape, o_ref.dtype)
        lse_ref[...] = jnp.full(lse_ref.shape, MASKED, lse_ref.dtype)

    @pl.when(tb_ref[g, first] >= 0)
    def _():
        for u in range(GROUP_STEP):
            used = tb_ref[g, first + u] >= 0
            rows = slice(u * t, (u + 1) * t)
            o, lse = _block_attention(q_ref[rows, :].astype(BF16), k_refs[u][...], v_refs[u][...])
            o_ref[rows, :] = jnp.where(used, o, 0.0)
            lse_ref[rows, :] = jnp.where(used, lse, MASKED)


def _moba_group_attention(q_grouped, tile_blk, qk, v, after):
    bh, rows, _ = q_grouped.shape
    t = MOBA_BLOCK
    tiles = rows // t
    heads = A_HEADS

    def block_of(u, first_col):
        return lambda g, s, tb: (g // heads, jnp.maximum(tb[g, s * GROUP_STEP + u], 0),
                                 first_col + g % heads)

    row_tile = pl.BlockSpec((None, GROUP_STEP * t, HEAD_DIM), lambda g, s, tb: (g, s, 0))
    key_value = lambda first_col: [pl.BlockSpec((None, t, HEAD_DIM), block_of(u, first_col))
                                   for u in range(GROUP_STEP)]
    grid_spec = pltpu.PrefetchScalarGridSpec(
        num_scalar_prefetch=1,
        grid=(bh, tiles // GROUP_STEP),
        in_specs=([row_tile] + key_value(KA_BLK) + key_value(0)
                  + [pl.BlockSpec(memory_space=pl.ANY)]),
        out_specs=[row_tile, row_tile],
    )
    return pl.pallas_call(
        _moba_group_kernel,
        grid_spec=grid_spec,
        out_shape=[jax.ShapeDtypeStruct(q_grouped.shape, F32)] * 2,
        compiler_params=_cparams("parallel", "parallel"),
        name="moba_group",
    )(tile_blk, q_grouped, *([qk] * GROUP_STEP), *([v] * GROUP_STEP), after)


def _moba_merge_kernel(q_ref, k_ref, v_ref, po_ref, pl_ref, o_ref):
    t = MOBA_BLOCK
    causal = (lax.broadcasted_iota(jnp.int32, (t, t), 1)
              <= lax.broadcasted_iota(jnp.int32, (t, t), 0))
    for hd in range(A_HEADS):
        sl = slice(hd * HEAD_DIM, (hd + 1) * HEAD_DIM)
        o_own, lse_own = _block_attention(q_ref[:, sl], k_ref[:, sl], v_ref[:, sl], causal)
        outs = [o_own] + [po_ref[s, hd] for s in range(MOBA_TOPK)]
        lses = [lse_own] + [pl_ref[s, hd] for s in range(MOBA_TOPK)]
        top = functools.reduce(jnp.maximum, lses)
        weights = [jnp.exp2(l - top) for l in lses]
        num = sum(w * o for w, o in zip(weights, outs))
        o_ref[:, sl] = (num / sum(weights)).astype(o_ref.dtype)


def _moba_merge(qk, v, part_o, part_lse):
    bsz, seq, _ = qk.shape
    t = MOBA_BLOCK
    part = pl.BlockSpec((MOBA_TOPK, None, A_HEADS, t, HEAD_DIM), lambda b, i: (0, b, 0, i, 0))
    return pl.pallas_call(
        _moba_merge_kernel,
        grid=(bsz, seq // t),
        in_specs=[pl.BlockSpec((None, t, A_W), lambda b, i: (b, i, QA_BLK * LANES // A_W)),
                  pl.BlockSpec((None, t, A_W), lambda b, i: (b, i, KA_BLK * LANES // A_W)),
                  pl.BlockSpec((None, t, A_W), lambda b, i: (b, i, 0)),
                  part, part],
        out_specs=pl.BlockSpec((None, t, A_W), lambda b, i: (b, i, 0)),
        out_shape=jax.ShapeDtypeStruct((bsz, seq, A_W), BF16),
        compiler_params=_cparams("parallel", "parallel"),
        name="moba_merge",
    )(qk, qk, v, part_o, part_lse)


def _moba_regroup(qk, kmean):
    bsz, seq, _ = qk.shape
    bh = bsz * A_HEADS
    rows = _moba_tiles(seq) * MOBA_BLOCK
    ids, cnt, q_f32 = _moba_gate(qk, kmean)
    pos, tile_blk = _moba_routes(ids, cnt, seq)
    q_grouped = _sc_scatter_rows(q_f32.reshape(bh * seq, HEAD_DIM), pos, bh * rows)
    return q_grouped.reshape(bh, rows, HEAD_DIM), pos, tile_blk


def _moba_picked_blocks(q_grouped, pos, tile_blk, qk, v, after):
    bsz, seq, _ = qk.shape
    bh, rows, _ = q_grouped.shape
    o_g, lse_g = _moba_group_attention(q_grouped, tile_blk, qk, v, after)
    flat = pos.reshape(-1)
    back = lambda a: _sc_gather_rows(a.reshape(bh * rows, HEAD_DIM), flat).reshape(
        MOBA_TOPK, bsz, A_HEADS, seq, HEAD_DIM)
    return back(o_g), back(lse_g)


def _proj_dilated_kernel(h_ref, w_ref, c_ref, s_ref, q_ref, k_ref, v_ref, sc, *, d):
    acc = _dot(h_ref[...], w_ref[...])
    c, s = c_ref[...], s_ref[...]
    q_scale = HEAD_DIM ** -0.5 * LOG2E
    for j in range(acc.shape[1] // LANES):
        blk = acc[:, j * LANES:(j + 1) * LANES]
        if j < B_HEADS:
            blk = _rope128(blk, c, s) * q_scale
        elif j < 2 * B_HEADS:
            blk = _rope128(blk, c, s)
        sc[j] = blk
    rows = acc.shape[0] // d
    for r in range(d):
        for j in range(acc.shape[1] // LANES):
            dst = (q_ref, k_ref, v_ref)[j // B_HEADS]
            col = (j % B_HEADS) * LANES
            dst[r, :, col:col + LANES] = sc[j, pl.ds(r, rows, stride=d), :].astype(dst.dtype)


def _proj_dilated(h, w, rope_h, d, bsz, seq, tm=512):
    m, k = h.shape
    nt = seq // tm
    pos = pl.BlockSpec((tm, LANES), lambda i: (i % nt, 0))
    out = pl.BlockSpec((None, d, tm // d, B_W), lambda i: (i // nt, 0, i % nt, 0))
    return pl.pallas_call(
        functools.partial(_proj_dilated_kernel, d=d),
        grid=(m // tm,),
        in_specs=[pl.BlockSpec((tm, k), lambda i: (i, 0)), pl.BlockSpec(w.shape, lambda i: (0, 0)),
                  pos, pos],
        out_specs=[out] * 3,
        out_shape=[jax.ShapeDtypeStruct((bsz, d, seq // d, B_W), BF16)] * 3,
        scratch_shapes=[pltpu.VMEM((w.shape[1] // LANES, tm, LANES), F32)],
        compiler_params=_cparams("parallel"),
        name=f"proj_dilated_d{d}",
    )(h, w, *rope_h)


def _dilated_kernel(q_ref, kc_ref, kp_ref, vc_ref, vp_ref, o_ref, lse_ref, *, span):
    t, tp = q_ref.shape[0], kp_ref.shape[0]
    i = pl.program_id(2)
    shape = (2 * tp, tp)
    key_row = lax.broadcasted_iota(jnp.int32, shape, 0)
    dist = lax.broadcasted_iota(jnp.int32, shape, 1) + tp - key_row
    visible = (dist >= 0) & (dist <= span)
    bias = jnp.where(visible, 0.0, MASKED)
    bias_first = jnp.where(visible & ((key_row >= tp) | (i > 0)), 0.0, MASKED)
    ones = jnp.ones((BF16_ROWS, tp + t), BF16)

    def transposed(x):
        return x.astype(F32).T.astype(BF16)

    for j in range(B_HEADS):
        sl = slice(j * LANES, (j + 1) * LANES)
        qT = transposed(q_ref[:, sl])
        k_all = jnp.concatenate([kp_ref[:, sl], kc_ref[:, sl]], axis=0)
        vt_all = jnp.concatenate([transposed(vp_ref[:, sl]), transposed(vc_ref[:, sl])], axis=1)
        vt_all = jnp.concatenate([vt_all, ones], axis=0)
        outs, lses = [], []
        for u in range(t // tp):
            window = slice(u * tp, (u + 2) * tp)
            s = _dot(k_all[window, :], qT[:, u * tp:(u + 1) * tp])
            s = s + (bias_first if u == 0 else bias)
            m = jnp.max(s, axis=0, keepdims=True)
            p = jnp.exp2(s - m).astype(BF16)
            acc = _dot(vt_all[:, window], p)
            den = acc[V_DIM:V_DIM + 1, :]
            outs.append(acc[:V_DIM, :] / den)
            lses.append(m + jnp.log2(den))
        o_ref[:, sl] = jnp.concatenate(outs, axis=1).T
        lse = jnp.concatenate(lses, axis=1)
        lse_ref[:, sl] = jnp.broadcast_to(lse, (LANES, t)).T


def _dilated_attention(q, k, v, span, t=1024):
    bsz, d, length, _ = q.shape
    t = min(t, length)
    tp = B_QBLOCK
    assert span <= tp and t % tp == 0
    cur = pl.BlockSpec((None, None, t, B_W), lambda b, r, i: (b, r, i, 0))
    prev = pl.BlockSpec((None, None, tp, B_W),
                        lambda b, r, i: (b, r, jnp.maximum(i * (t // tp) - 1, 0), 0))
    return pl.pallas_call(
        functools.partial(_dilated_kernel, span=span),
        grid=(bsz, d, length // t),
        in_specs=[cur, cur, prev, cur, prev],
        out_specs=[cur, cur],
        out_shape=[jax.ShapeDtypeStruct(q.shape, F32)] * 2,
        compiler_params=_cparams("parallel", "parallel", "parallel"),
        name=f"dilated_d{d}",
    )(q, k, k, v, v)


def _natural_rows(ref, sc):
    d, rows = ref.shape[0], ref.shape[1]
    if d == 1:
        return ref[0]
    for r in range(d):
        for j in range(B_HEADS):
            sc[j, pl.ds(r, rows, stride=d), :] = ref[r, :, j * LANES:(j + 1) * LANES]
    return jnp.concatenate([sc[j] for j in range(B_HEADS)], axis=-1)


def _mixer_tail_kernel(x_ref, oa_ref, o0_ref, o1_ref, o2_ref, l0_ref, l1_ref, l2_ref, oc_ref,
                       g_ref, wpa_ref, wpb_ref, wpc_ref, wo_ref, y_ref, *scratch):
    o0, o1, o2, l0, l1, l2 = [
        _natural_rows(ref, sc)
        for ref, sc in zip((o0_ref, o1_ref, o2_ref, l0_ref, l1_ref, l2_ref), scratch)]
    mx = jnp.maximum(jnp.maximum(l0, l1), l2)
    e0, e1, e2 = jnp.exp2(l0 - mx), jnp.exp2(l1 - mx), jnp.exp2(l2 - mx)
    ob = (e0 * o0 + e1 * o1 + e2 * o2) / (e0 + e1 + e2)
    pa = _dot(oa_ref[...], wpa_ref[...])
    pb = _dot(ob.astype(BF16), wpb_ref[...])
    pc = _dot(oc_ref[...], wpc_ref[...])
    d = D_MODEL
    merged = (g_ref[:, 0:d].astype(F32) * pa + g_ref[:, d:2 * d].astype(F32) * pb
              + g_ref[:, 2 * d:3 * d].astype(F32) * pc)
    y_ref[...] = x_ref[...] + _dot(merged.astype(BF16), wo_ref[...])


def _mixer_tail(x, out_a, o_groups, lse_groups, out_c, gates, w_pa, w_pb, w_pc, w_o, seq, tm=256):
    m, d = x.shape
    nt = seq // tm
    row = lambda width: pl.BlockSpec((tm, width), lambda i: (i, 0))
    residue = lambda g: pl.BlockSpec((None, g.shape[1], tm // g.shape[1], B_W),
                                     lambda i: (i // nt, 0, i % nt, 0))
    weights = [_resident(w) for w in (w_pa, w_pb, w_pc, w_o)]
    groups = list(o_groups) + list(lse_groups)
    return pl.pallas_call(
        _mixer_tail_kernel,
        grid=(m // tm,),
        in_specs=([row(d), row(A_W)] + [residue(g) for g in groups]
                  + [row(C_W), row(3 * d)] + weights),
        out_specs=row(d),
        out_shape=jax.ShapeDtypeStruct((m, d), F32),
        scratch_shapes=[pltpu.VMEM((B_HEADS, tm, LANES), F32) for _ in groups],
        compiler_params=_cparams("parallel"),
        name="mixer_tail",
    )(x, out_a, *groups, out_c, gates, w_pa, w_pb, w_pc, w_o)


def _mem_kv_kernel(mem_ref, g_ref, wk_ref, wv_ref, k_ref, v_ref):
    memn = _rms(mem_ref[...], g_ref[...]).astype(BF16)
    k_ref[...] = _dot(memn, wk_ref[...]).astype(k_ref.dtype)
    v_ref[...] = _dot(memn, wv_ref[...]).astype(v_ref.dtype)


def _mem_kv(mem, g, wk, wv):
    bsz, n, d = mem.shape
    out = pl.BlockSpec((None, n, X_W), lambda b: (b, 0, 0))
    return pl.pallas_call(
        _mem_kv_kernel,
        grid=(bsz,),
        in_specs=[pl.BlockSpec((None, n, d), lambda b: (b, 0, 0)),
                  pl.BlockSpec((1, d), lambda b: (0, 0)),
                  pl.BlockSpec(wk.shape, lambda b: (0, 0)),
                  pl.BlockSpec(wv.shape, lambda b: (0, 0))],
        out_specs=[out, out],
        out_shape=[jax.ShapeDtypeStruct((bsz, n, X_W), BF16)] * 2,
        compiler_params=_cparams("parallel"),
        name="mem_kv",
    )(mem, g.reshape(1, d), wk, wv)


def _mem_attn_kernel(x_ref, g_ref, wq_ref, k_ref, v_ref, wo_ref, y_ref):
    x = x_ref[...]
    h = _rms(x, g_ref[...]).astype(BF16)
    q = (_dot(h, wq_ref[...]) * HEAD_DIM ** -0.5).astype(BF16)
    heads = []
    for hd in range(X_HEADS):
        sl = slice(hd * HEAD_DIM, (hd + 1) * HEAD_DIM)
        s = _dot_nt(q[:, sl], k_ref[:, sl])
        p = jnp.exp(s - jnp.max(s, axis=-1, keepdims=True))
        o = _dot(p.astype(BF16), v_ref[:, sl]) / jnp.sum(p, axis=-1, keepdims=True)
        heads.append(o.astype(BF16))
    y_ref[...] = x + _dot(jnp.concatenate(heads, axis=-1), wo_ref[...])


def _mem_attention(x, g, wq, kmem, vmem, wo, seq, tm=512):
    m, d = x.shape
    nt = seq // tm
    n = kmem.shape[1]
    kv = pl.BlockSpec((None, n, X_W), lambda i: (i // nt, 0, 0))
    return pl.pallas_call(
        _mem_attn_kernel,
        grid=(m // tm,),
        in_specs=[pl.BlockSpec((tm, d), lambda i: (i, 0)),
                  pl.BlockSpec((1, d), lambda i: (0, 0)),
                  pl.BlockSpec(wq.shape, lambda i: (0, 0)), kv, kv,
                  pl.BlockSpec(wo.shape, lambda i: (0, 0))],
        out_specs=pl.BlockSpec((tm, d), lambda i: (i, 0)),
        out_shape=jax.ShapeDtypeStruct((m, d), F32),
        compiler_params=_cparams("parallel"),
        name="mem_attention",
    )(x, g.reshape(1, d), wq, kmem, vmem, wo)


def _ffn_up_kernel(x_ref, halo_ref, g_ref, wg_ref, wv_ref, cwg_ref, cwv_ref, cbg_ref, cbv_ref,
                   act_ref, h_sc, *, tiles_per_seq):
    i = pl.program_id(0)
    tm = x_ref.shape[0]

    @pl.when(pl.program_id(1) == 0)
    def _():
        g = g_ref[...]
        keep = (i % tiles_per_seq != 0).astype(F32)
        h_sc[0:HALO, :] = (_rms(halo_ref[...], g) * keep).astype(h_sc.dtype)
        h_sc[HALO:, :] = _rms(x_ref[...], g).astype(h_sc.dtype)

    h = h_sc[...]

    def conv(w_ref, cw_ref, cb_ref):
        u = _dot(h, w_ref[...])
        c = cb_ref[...]
        for tap in range(CONV_W):
            lo = HALO - (CONV_W - 1) + tap
            c = c + cw_ref[tap:tap + 1, :] * u[lo:lo + tm, :]
        return c

    act = jax.nn.silu(conv(wg_ref, cwg_ref, cbg_ref)) * conv(wv_ref, cwv_ref, cbv_ref)
    act_ref[...] = act.astype(act_ref.dtype)


def _ffn_down_kernel(a_ref, w_ref, x_ref, y_ref):
    y_ref[...] = x_ref[...] + _dot(a_ref[...], w_ref[...])


def _conv_ffn(x, g, w_up, conv_w, conv_b, w_down, layer, seq):
    m, d = x.shape
    act = _ffn_up(x, g, w_up, conv_w, conv_b, layer, seq)
    tm, tn = 1024, FFN_TF
    return pl.pallas_call(
        _ffn_down_kernel,
        grid=(m // tm, d // tn),
        in_specs=[pl.BlockSpec((tm, D_FF), lambda i, j: (i, 0)),
                  pl.BlockSpec((None, D_FF, tn), lambda i, j: (layer, 0, j)),
                  pl.BlockSpec((tm, tn), lambda i, j: (i, j))],
        out_specs=pl.BlockSpec((tm, tn), lambda i, j: (i, j)),
        out_shape=jax.ShapeDtypeStruct((m, d), F32),
        compiler_params=_cparams("parallel", "parallel"),
        name="ffn_down",
    )(act, w_down, x)


def _ffn_up(x, g, w_up, conv_w, conv_b, layer, seq, tm=1024, tf=FFN_TF):
    m, d = x.shape
    nf = D_FF_PAD // tf
    halo_blocks = tm // HALO
    return pl.pallas_call(
        functools.partial(_ffn_up_kernel, tiles_per_seq=seq // tm),
        grid=(m // tm, nf),
        in_specs=[pl.BlockSpec((tm, d), lambda i, f: (i, 0)),
                  pl.BlockSpec((HALO, d), lambda i, f: (jnp.maximum(i * halo_blocks - 1, 0), 0)),
                  pl.BlockSpec((1, d), lambda i, f: (0, 0)),
                  pl.BlockSpec((None, d, tf), lambda i, f: (layer, 0, f)),
                  pl.BlockSpec((None, d, tf), lambda i, f: (layer, 0, f + nf)),
                  pl.BlockSpec((CONV_W, tf), lambda i, f: (0, f)),
                  pl.BlockSpec((CONV_W, tf), lambda i, f: (0, f + nf)),
                  pl.BlockSpec((1, tf), lambda i, f: (0, f)),
                  pl.BlockSpec((1, tf), lambda i, f: (0, f + nf))],
        out_specs=pl.BlockSpec((tm, tf), lambda i, f: (i, f)),
        out_shape=jax.ShapeDtypeStruct((m, D_FF_PAD), BF16),
        scratch_shapes=[pltpu.VMEM((HALO + tm, d), BF16)],
        compiler_params=_cparams("parallel", "arbitrary"),
        name="ffn_up",
    )(x, x, g.reshape(1, d), w_up, w_up, conv_w, conv_w, conv_b, conv_b)


def _rope_tables(seq):
    def angles(dim):
        inv_freq = jnp.exp(jnp.arange(0, dim, 2, dtype=F32) * (-math.log(ROPE_THETA) / dim))
        ang = jnp.arange(seq, dtype=F32)[:, None] * inv_freq[None, :]
        return jnp.cos(ang), jnp.sin(ang)

    cos_h, sin_h = angles(HEAD_DIM)
    rope_h = (jnp.concatenate([cos_h, cos_h], axis=-1), jnp.concatenate([-sin_h, sin_h], axis=-1))
    cos_r, sin_r = angles(ROPE_DIM)
    z = jnp.zeros_like(cos_r)
    rope_r = (jnp.concatenate([cos_r, cos_r, z, z], axis=-1),
              jnp.concatenate([-sin_r, z, z, z], axis=-1),
              jnp.concatenate([z, sin_r, z, z], axis=-1))
    return rope_h, rope_r


def _split_in(w_in):
    return [w_in[:, IN_OFFSETS[k]:IN_OFFSETS[k + 1]] for k in range(len(IN_WIDTHS))]


def _pad_cols(w, width):
    return jnp.pad(w, ((0, 0), (0, width - w.shape[1])))


def _layer_params(w_in, w_uq, w_ukv, conv_w, conv_b):
    qa, ka, va, qb, kb, vb, cq, ckv, kr, gates = _split_in(w_in)
    w_qk = jnp.concatenate([qa, ka], axis=1).astype(BF16)
    group_cols = lambda w, g: w[:, g * B_W:(g + 1) * B_W]
    w_b = [jnp.concatenate([group_cols(qb, g), group_cols(kb, g), group_cols(vb, g)],
                           axis=1).astype(BF16) for g in range(len(B_GROUPS))]
    w_down_in = jnp.concatenate([cq, ckv, _pad_cols(kr, LANES)], axis=1).astype(BF16)
    uq = w_uq.reshape(Q_LORA, C_HEADS, NOPE_DIM + ROPE_DIM)
    uq = jnp.pad(uq, ((0, 0), (0, 0), (0, C_QK - NOPE_DIM - ROPE_DIM)))
    ukv = w_ukv.reshape(KV_LORA, C_HEADS, NOPE_DIM + V_DIM)
    return dict(
        w_qk=w_qk, w_va=va.astype(BF16), w_b=w_b, w_gates=gates.astype(BF16),
        w_down_in=w_down_in,
        w_uq=uq.reshape(Q_LORA, C_HEADS * C_QK).astype(BF16),
        w_uk=ukv[:, :, :NOPE_DIM].reshape(KV_LORA, C_HEADS * NOPE_DIM).astype(BF16),
        w_uv=ukv[:, :, NOPE_DIM:].reshape(KV_LORA, C_W).astype(BF16),
        conv_w=_pad_ff_halves(conv_w),
        conv_b=_pad_ff_halves(conv_b.reshape(1, -1)),
    )


def _pad_ff_halves(w):
    pad = [(0, 0)] * (w.ndim - 1) + [(0, D_FF_PAD - D_FF)]
    return jnp.concatenate([jnp.pad(w[..., :D_FF], pad), jnp.pad(w[..., D_FF:], pad)], axis=-1)


def _qk_col_scale():
    q_scale = HEAD_DIM ** -0.5
    parts = [jnp.full((A_W,), q_scale * LOG2E, F32), jnp.ones((A_W,), F32)]
    return jnp.concatenate(parts).reshape(1, QK_W)


def _mixer(x, g_mix, p, g_cq, g_ckv, w_pa, w_pb, w_pc, w_o, rope_h, rope_r, bsz, seq):
    m = x.shape[0]
    h = _rmsnorm(x, g_mix, BF16)
    qk = _matmul(h, p["w_qk"], _mm_rope_kernel, BF16, 1024, 1024, seq=seq,
                 extras=(("col", _qk_col_scale()), ("pos", rope_h[0]), ("pos", rope_h[1])),
                 name="proj_qk_rope")
    qk3 = qk.reshape(bsz, seq, QK_W)
    q_grouped, pos, tile_blk = _moba_regroup(qk3, _kmean(qk3))
    v_a = _matmul(h, p["w_va"], _mm_plain_kernel, BF16, 1024, A_W, name="proj_va")
    gates = _matmul(h, p["w_gates"], _mm_sigmoid_kernel, BF16, 1024, 1024, name="proj_gates")
    v_a3 = v_a.reshape(bsz, seq, A_W)
    part_o, part_lse = _moba_picked_blocks(q_grouped, pos, tile_blk, qk3, v_a3, after=gates)
    cq, ckv, kr = _mla_down(h, p["w_down_in"], g_cq, g_ckv, rope_r, seq)
    q_c = _mla_q(cq, p["w_uq"], rope_r, seq)
    k_c, vt_c = _mla_kv(ckv, kr, p["w_uk"], p["w_uv"], bsz, seq)
    groups = []
    for (window, d), w_g in zip(B_GROUPS, p["w_b"]):
        q_g, k_g, v_g = _proj_dilated(h, w_g, rope_h, d, bsz, seq)
        groups.append(_dilated_attention(q_g, k_g, v_g, window // d))
    out_c = _flash_attention(q_c.reshape(bsz, seq, -1), k_c.reshape(bsz, seq, -1), vt_c,
                             C_HEADS, C_QK).reshape(m, C_W)
    out_a = _moba_merge(qk3, v_a3, part_o, part_lse).reshape(m, A_W)
    return _mixer_tail(x, out_a, [g[0] for g in groups], [g[1] for g in groups], out_c, gates,
                       w_pa.astype(BF16), w_pb.astype(BF16), w_pc.astype(BF16), w_o.astype(BF16),
                       seq)


def kernel(x, mem, g_mix, w_in, g_cq, g_ckv, w_uq, w_ukv, w_pa, w_pb, w_pc, w_o, g_mem, g_memkv,
           w_xq, w_xk, w_xv, w_xo, g_ffn, w_up, conv_w, conv_b, w_down, g_final):
    bsz, seq, d = x.shape
    rope_h, rope_r = _rope_tables(seq)
    xf = x.reshape(bsz * seq, d)
    w_down = w_down.astype(BF16)
    w_up = _pad_ff_halves(w_up.astype(BF16))
    for l in range(DEPTH):
        p = _layer_params(w_in[l], w_uq[l], w_ukv[l], conv_w[l], conv_b[l])
        xf = _mixer(xf, g_mix[l], p, g_cq[l], g_ckv[l], w_pa[l], w_pb[l], w_pc[l], w_o[l],
                    rope_h, rope_r, bsz, seq)
        kmem, vmem = _mem_kv(mem, g_memkv[l], w_xk[l].astype(BF16), w_xv[l].astype(BF16))
        xf = _mem_attention(xf, g_mem[l], w_xq[l].astype(BF16), kmem, vmem,
                            w_xo[l].astype(BF16), seq)
        xf = _conv_ffn(xf, g_ffn[l], w_up, p["conv_w"], p["conv_b"], w_down, l, seq)
    return _rmsnorm(xf, g_final, F32).reshape(bsz, seq, d)
```

```python
import functools
import math

import jax
import jax.numpy as jnp
import numpy as np
from jax import lax
from jax.experimental import pallas as pl
from jax.experimental.pallas import tpu as pltpu
from jax.experimental.pallas import tpu_sc as plsc

F32 = jnp.float32
BF16 = jnp.bfloat16

LANES = 128
SUBLANES = 8
V7X_VMEM_BYTES = 64 * 1024 * 1024
VMEM_LIMIT = V7X_VMEM_BYTES * 7 // 8

D_MODEL = 2048
DEPTH = 2
HEAD_DIM = 128
ROPE_THETA = 10000.0
EPS = 1e-6

A_HEADS = 4
MOBA_BLOCK = 256
MOBA_TOPK = 3

B_GROUPS = ((128, 1), (512, 4), (2048, 16))
B_HEADS = 4
B_QBLOCK = 128

C_HEADS = 8
Q_LORA = 1536
KV_LORA = 512
NOPE_DIM = 128
ROPE_DIM = 64
V_DIM = 128

X_HEADS = 4
D_FF = 5504
CONV_W = 3

A_W = A_HEADS * HEAD_DIM
B_QKV_W = len(B_GROUPS) * B_HEADS * HEAD_DIM
B_W = B_HEADS * HEAD_DIM
C_W = C_HEADS * V_DIM
X_W = X_HEADS * HEAD_DIM
IN_WIDTHS = (A_W, A_W, A_W, B_QKV_W, B_QKV_W, B_QKV_W, Q_LORA, KV_LORA, ROPE_DIM, 3 * D_MODEL)
IN_OFFSETS = tuple(int(o) for o in np.cumsum((0,) + IN_WIDTHS))

QK_W = 2 * A_W
QA_BLK, KA_BLK = 0, A_W // LANES

C_QK = 2 * LANES
MASKED = -1e30
LOG2E = math.log2(math.e)
BF16_ROWS = 16
VT_ROWS = V_DIM + BF16_ROWS
GROUP_STEP = 8
SC_WINDOW = 128

FFN_TF = 512
D_FF_PAD = -(-D_FF // FFN_TF) * FFN_TF
HALO = SUBLANES


def _cparams(*sem):
    return pltpu.CompilerParams(dimension_semantics=sem, vmem_limit_bytes=VMEM_LIMIT)


def _resident(arr):
    zeros = (0,) * arr.ndim
    return pl.BlockSpec(arr.shape, lambda *_: zeros, pipeline_mode=pl.Buffered(1))


def _dot(a, b):
    return jnp.dot(a, b, preferred_element_type=F32)


def _dot_nt(a, b):
    return lax.dot_general(a, b, (((1,), (1,)), ((), ())), preferred_element_type=F32)


def _rms(x, g):
    return x * lax.rsqrt(jnp.mean(x * x, axis=-1, keepdims=True) + EPS) * g


def _rmsnorm_kernel(x_ref, g_ref, o_ref):
    o_ref[...] = _rms(x_ref[...], g_ref[...]).astype(o_ref.dtype)


def _rmsnorm(x, g, out_dtype, tm=512):
    m, d = x.shape
    return pl.pallas_call(
        _rmsnorm_kernel,
        grid=(m // tm,),
        in_specs=[pl.BlockSpec((tm, d), lambda i: (i, 0)),
                  pl.BlockSpec((1, d), lambda i: (0, 0))],
        out_specs=pl.BlockSpec((tm, d), lambda i: (i, 0)),
        out_shape=jax.ShapeDtypeStruct((m, d), out_dtype),
        compiler_params=_cparams("parallel"),
        name="rmsnorm",
    )(x, g.reshape(1, d))


def _rope128(x, c, s):
    return x * c + pltpu.roll(x, HEAD_DIM // 2, 1) * s


def _rope64(x, c, sa, sb):
    half = ROPE_DIM // 2
    return x * c + pltpu.roll(x, LANES - half, 1) * sa + pltpu.roll(x, half, 1) * sb


def _mm_plain_kernel(a_ref, w_ref, o_ref):
    o_ref[...] = _dot(a_ref[...], w_ref[...]).astype(o_ref.dtype)


def _mm_sigmoid_kernel(a_ref, w_ref, o_ref):
    o_ref[...] = jax.nn.sigmoid(_dot(a_ref[...], w_ref[...])).astype(o_ref.dtype)


def _mm_rope_kernel(a_ref, w_ref, cs_ref, c_ref, s_ref, o_ref):
    acc = _dot(a_ref[...], w_ref[...])
    c = c_ref[...]
    s = s_ref[...]
    for j in range(acc.shape[1] // LANES):
        sl = slice(j * LANES, (j + 1) * LANES)
        o_ref[:, sl] = (_rope128(acc[:, sl], c, s) * cs_ref[:, sl]).astype(o_ref.dtype)


def _matmul(a, w, kernel, out_dtype, tm, tn, seq=None, extras=(), name="matmul"):
    m, k = a.shape
    n = w.shape[1]
    in_specs = [pl.BlockSpec((tm, k), lambda i, j: (i, 0)),
                pl.BlockSpec((k, tn), lambda i, j: (0, j))]
    args = [a, w]
    for kind, arr in extras:
        if kind == "col":
            in_specs.append(pl.BlockSpec((1, tn), lambda i, j: (0, j)))
        else:
            nt = seq // tm
            in_specs.append(pl.BlockSpec((tm, LANES), lambda i, j: (i % nt, 0)))
        args.append(arr)
    return pl.pallas_call(
        kernel,
        grid=(m // tm, n // tn),
        in_specs=in_specs,
        out_specs=pl.BlockSpec((tm, tn), lambda i, j: (i, j)),
        out_shape=jax.ShapeDtypeStruct((m, n), out_dtype),
        compiler_params=_cparams("parallel", "parallel"),
        name=name,
    )(*args)


def _mla_down_kernel(h_ref, w_ref, gq_ref, gkv_ref, c_ref, sa_ref, sb_ref,
                     cq_ref, ckv_ref, kr_ref):
    acc = _dot(h_ref[...], w_ref[...])
    cq_ref[...] = _rms(acc[:, :Q_LORA], gq_ref[...]).astype(cq_ref.dtype)
    ckv_ref[...] = _rms(acc[:, Q_LORA:Q_LORA + KV_LORA], gkv_ref[...]).astype(ckv_ref.dtype)
    kr = acc[:, Q_LORA + KV_LORA:]
    kr_ref[...] = _rope64(kr, c_ref[...], sa_ref[...], sb_ref[...]).astype(kr_ref.dtype)


def _mla_down(h, w, g_cq, g_ckv, rope_r, seq, tm=512):
    m, k = h.shape
    n = w.shape[1]
    nt = seq // tm
    row = lambda width: pl.BlockSpec((tm, width), lambda i: (i, 0))
    full = lambda r, c: pl.BlockSpec((r, c), lambda i: (0, 0))
    pos = pl.BlockSpec((tm, LANES), lambda i: (i % nt, 0))
    return pl.pallas_call(
        _mla_down_kernel,
        grid=(m // tm,),
        in_specs=[row(k), full(k, n), full(1, Q_LORA), full(1, KV_LORA), pos, pos, pos],
        out_specs=[row(Q_LORA), row(KV_LORA), row(LANES)],
        out_shape=[jax.ShapeDtypeStruct((m, Q_LORA), BF16),
                   jax.ShapeDtypeStruct((m, KV_LORA), BF16),
                   jax.ShapeDtypeStruct((m, LANES), BF16)],
        compiler_params=_cparams("parallel"),
        name="mla_down",
    )(h, w, g_cq.reshape(1, -1), g_ckv.reshape(1, -1), *rope_r)


def _mla_q_kernel(cq_ref, w_ref, c_ref, sa_ref, sb_ref, q_ref, *, scale):
    acc = _dot(cq_ref[...], w_ref[...])
    c, sa, sb = c_ref[...], sa_ref[...], sb_ref[...]
    for hd in range(C_HEADS):
        lo = hd * C_QK
        q_ref[:, lo:lo + LANES] = (acc[:, lo:lo + LANES] * scale).astype(q_ref.dtype)
        rope = _rope64(acc[:, lo + LANES:lo + C_QK], c, sa, sb)
        q_ref[:, lo + LANES:lo + C_QK] = (rope * scale).astype(q_ref.dtype)


def _mla_q(cq, w, rope_r, seq, tm=512):
    m, k = cq.shape
    n = w.shape[1]
    nt = seq // tm
    pos = pl.BlockSpec((tm, LANES), lambda i: (i % nt, 0))
    return pl.pallas_call(
        functools.partial(_mla_q_kernel, scale=(NOPE_DIM + ROPE_DIM) ** -0.5 * LOG2E),
        grid=(m // tm,),
        in_specs=[pl.BlockSpec((tm, k), lambda i: (i, 0)),
                  pl.BlockSpec((k, n), lambda i: (0, 0)), pos, pos, pos],
        out_specs=pl.BlockSpec((tm, n), lambda i: (i, 0)),
        out_shape=jax.ShapeDtypeStruct((m, n), BF16),
        compiler_params=_cparams("parallel"),
        name="mla_q",
    )(cq, w, *rope_r)


def _store_vt(v, vt_ref):
    vt = v.T
    for hd in range(vt_ref.shape[0]):
        vt_ref[hd, 0:V_DIM, :] = vt[hd * V_DIM:(hd + 1) * V_DIM, :].astype(vt_ref.dtype)
        vt_ref[hd, V_DIM:VT_ROWS, :] = jnp.ones((VT_ROWS - V_DIM, vt.shape[1]), vt_ref.dtype)


def _mla_kv_kernel(ckv_ref, kr_ref, wk_ref, wv_ref, k_ref, vt_ref):
    ckv = ckv_ref[...]
    kn = _dot(ckv, wk_ref[...])
    kr = kr_ref[...]
    for hd in range(C_HEADS):
        lo = hd * C_QK
        k_ref[:, lo:lo + LANES] = kn[:, hd * LANES:(hd + 1) * LANES].astype(k_ref.dtype)
        k_ref[:, lo + LANES:lo + C_QK] = kr
    _store_vt(_dot(ckv, wv_ref[...]), vt_ref)


def _mla_kv(ckv, kr, wk, wv, bsz, seq, tm=512):
    m, k = ckv.shape
    nt = seq // tm
    return pl.pallas_call(
        _mla_kv_kernel,
        grid=(m // tm,),
        in_specs=[pl.BlockSpec((tm, k), lambda i: (i, 0)),
                  pl.BlockSpec((tm, LANES), lambda i: (i, 0)),
                  pl.BlockSpec(wk.shape, lambda i: (0, 0)),
                  pl.BlockSpec(wv.shape, lambda i: (0, 0))],
        out_specs=[pl.BlockSpec((tm, C_HEADS * C_QK), lambda i: (i, 0)),
                   pl.BlockSpec((None, C_HEADS, VT_ROWS, tm), lambda i: (i // nt, 0, 0, i % nt))],
        out_shape=[jax.ShapeDtypeStruct((m, C_HEADS * C_QK), BF16),
                   jax.ShapeDtypeStruct((bsz, C_HEADS, VT_ROWS, seq), BF16)],
        compiler_params=_cparams("parallel"),
        name="mla_kv",
    )(ckv, kr, wk, wv)


def _attend_chunks(qT, k_ref, vt_ref, scratch, *, tk, n_full, mask_tail, tail_steps, tail_col,
                   unroll):
    m_sc, acc_sc, sa_sc, sb_sc, pa_sc, pb_sc, ala_sc, alb_sc, mxa_sc, mxb_sc = scratch
    s_bufs = (sa_sc, sb_sc)
    p_bufs = (pa_sc, pb_sc)
    al_bufs = (ala_sc, alb_sc)
    mx_bufs = (mxa_sc, mxb_sc)
    last_chunk = k_ref.shape[0] // tk - 1
    m_sc[...] = jnp.full(m_sc.shape, MASKED, F32)
    acc_sc[...] = jnp.zeros(acc_sc.shape, F32)
    for p_ref, al_ref in zip(p_bufs, al_bufs):
        p_ref[...] = jnp.zeros(p_ref.shape, p_ref.dtype)
        al_ref[...] = jnp.ones(al_ref.shape, F32)

    def rows(c):
        return pl.ds(pl.multiple_of(jnp.clip(c, 0, last_chunk) * tk, tk), tk)

    def scores(c, slot, col=0):
        sT = _dot(k_ref[rows(c), :], qT[:, col:])
        s_bufs[slot][:, col:] = sT
        mx_bufs[slot][:, col:] = jnp.max(sT, axis=0, keepdims=True)

    def flush(c, slot, col=0):
        acc_sc[:, col:] = (al_bufs[slot][:, col:] * acc_sc[:, col:]
                           + _dot(vt_ref[:, rows(c)], p_bufs[slot][:, col:]))

    def softmax(sT, top, slot, col):
        m_old = m_sc[:, col:]
        m_new = jnp.maximum(m_old, top)
        al_bufs[slot][:, col:] = jnp.exp2(m_old - m_new)
        p_bufs[slot][:, col:] = jnp.exp2(sT - m_new).astype(p_bufs[slot].dtype)
        m_sc[:, col:] = m_new

    def step(tau, slot, mask, cols):
        col_flush, col, col_next = cols
        flush(tau - 2, slot, col_flush)
        if col_next is not None:
            scores(tau + 1, 1 - slot, col_next)
        sT = s_bufs[slot][:, col:]
        if mask is None:
            softmax(sT, mx_bufs[slot][:, col:], slot, col)
        else:
            sT = mask(sT, tau, col)
            softmax(sT, jnp.max(sT, axis=0, keepdims=True), slot, col)

    def full_steps(tau0, count):
        for j in range(count):
            step(tau0 + j, j % 2, None, (0, 0, 0))

    scores(0, 0)
    trips = n_full // unroll
    lax.fori_loop(0, trips, lambda u, c: (full_steps(unroll * u, unroll), c)[1], 0)
    done = unroll * trips
    pairs = (n_full - done) // 2
    lax.fori_loop(0, pairs, lambda u, c: (full_steps(done + 2 * u, 2), c)[1], 0)
    tau = done + 2 * pairs
    cols = [tail_col(j) for j in range(tail_steps)]
    for j in range(tail_steps):
        col_flush = cols[j - 2] if j >= 2 else 0
        col_next = cols[j + 1] if j + 1 < tail_steps else None
        step(tau + j, j % 2, mask_tail, (col_flush, cols[j], col_next))
    flush(tau + tail_steps - 2, 0, cols[-2])
    flush(tau + tail_steps - 1, 1, cols[-1])
    acc = acc_sc[...]
    return acc[:V_DIM, :] / acc[V_DIM:V_DIM + 1, :]


def _attend_scratch(tq, tk):
    return [pltpu.VMEM((1, tq), F32), pltpu.VMEM((VT_ROWS, tq), F32),
            pltpu.VMEM((tk, tq), F32), pltpu.VMEM((tk, tq), F32),
            pltpu.VMEM((tk, tq), BF16), pltpu.VMEM((tk, tq), BF16),
            pltpu.VMEM((1, tq), F32), pltpu.VMEM((1, tq), F32),
            pltpu.VMEM((1, tq), F32), pltpu.VMEM((1, tq), F32)]


def _transpose_q(q_ref):
    return q_ref[...].astype(F32).T.astype(BF16)


def _flash_kernel(q_ref, k_ref, vt_ref, o_ref, *scratch, tq, tk):
    i = pl.program_id(2)

    def causal(sT, c, col):
        key = lax.broadcasted_iota(jnp.int32, sT.shape, 0) + c * tk
        qry = lax.broadcasted_iota(jnp.int32, sT.shape, 1) + (i * tq + col)
        return jnp.where(key <= qry, sT, MASKED)

    per_tile = tq // tk
    oT = _attend_chunks(_transpose_q(q_ref), k_ref, vt_ref, scratch, tk=tk, n_full=i * per_tile,
                        mask_tail=causal, tail_steps=per_tile, tail_col=lambda j: j * tk,
                        unroll=8)
    o_ref[...] = oT.T.astype(o_ref.dtype)


def _flash_attention(q, k, vt, heads, qk_w, tq=1024, tk=256):
    bsz, seq, _ = q.shape
    assert tq % (2 * tk) == 0 and seq % tq == 0
    return pl.pallas_call(
        functools.partial(_flash_kernel, tq=tq, tk=tk),
        grid=(bsz, heads, seq // tq),
        in_specs=[pl.BlockSpec((None, tq, qk_w), lambda b, h, i: (b, i, h)),
                  pl.BlockSpec((None, seq, qk_w), lambda b, h, i: (b, 0, h)),
                  pl.BlockSpec((None, None, VT_ROWS, seq), lambda b, h, i: (b, h, 0, 0))],
        out_specs=pl.BlockSpec((None, tq, V_DIM), lambda b, h, i: (b, i, h)),
        out_shape=jax.ShapeDtypeStruct((bsz, seq, heads * V_DIM), BF16),
        scratch_shapes=_attend_scratch(tq, tk),
        compiler_params=_cparams("parallel", "parallel", "arbitrary"),
        name="mla_flash",
    )(q, k, vt)


def _kmean_kernel(k_ref, o_ref):
    k = k_ref[...].astype(F32)
    o_ref[...] = jnp.mean(k.reshape(SUBLANES, MOBA_BLOCK, k.shape[-1]), axis=1)


def _kmean(qk):
    bsz, seq, _ = qk.shape
    rows = SUBLANES * MOBA_BLOCK
    return pl.pallas_call(
        _kmean_kernel,
        grid=(bsz, seq // rows),
        in_specs=[pl.BlockSpec((None, rows, A_W), lambda b, i: (b, i, KA_BLK * LANES // A_W))],
        out_specs=pl.BlockSpec((None, SUBLANES, A_W), lambda b, i: (b, i, 0)),
        out_shape=jax.ShapeDtypeStruct((bsz, seq // MOBA_BLOCK, A_W), F32),
        compiler_params=_cparams("parallel", "parallel"),
        name="moba_kmean",
    )(qk)


def _block_attention(q, k, v, visible=None):
    s = _dot_nt(q, k)
    if visible is not None:
        s = jnp.where(visible, s, MASKED)
    m = jnp.max(s, axis=-1, keepdims=True)
    p = jnp.exp2(s - m).astype(BF16)
    v_ones = jnp.concatenate([v, jnp.ones((v.shape[0], LANES), v.dtype)], axis=-1)
    acc = _dot(p, v_ones)
    den = acc[:, V_DIM:]
    return acc[:, :V_DIM] / den, m + jnp.log2(den)


def _moba_gate_kernel(q_ref, km_ref, ids_ref, cnt_ref, qf_ref):
    t = MOBA_BLOCK
    i = pl.program_id(1)
    nb = km_ref.shape[0]
    blk = lax.broadcasted_iota(jnp.int32, (nb, t), 0)
    neg_inf = jnp.float32(-jnp.inf)
    not_after = (lax.broadcasted_iota(jnp.int32, (t, t), 0)
                 <= lax.broadcasted_iota(jnp.int32, (t, t), 1))
    upper = jnp.where(not_after, 1.0, 0.0).astype(BF16)
    ones = jnp.ones((SUBLANES, t), BF16)
    for hd in range(A_HEADS):
        sl = slice(hd * HEAD_DIM, (hd + 1) * HEAD_DIM)
        q = q_ref[:, sl].astype(F32)
        qf_ref[hd] = q
        qT = q.T.astype(BF16)
        km = km_ref[:, sl]
        km_hi = km.astype(BF16)
        km_lo = (km - km_hi.astype(F32)).astype(BF16)
        g = jnp.where(blk < i, _dot(km_hi, qT) + _dot(km_lo, qT), neg_inf)
        picks, ranks, counts = [], [], []
        for _ in range(MOBA_TOPK):
            mx = jnp.max(g, axis=0, keepdims=True)
            is_max = (g == mx) & (mx > neg_inf)
            first = jnp.min(jnp.where(is_max, blk, nb), axis=0, keepdims=True)
            pick = blk == first
            g = jnp.where(pick, neg_inf, g)
            onehot = jnp.where(pick, 1.0, 0.0).astype(BF16)
            before = _dot(onehot, upper)
            rank = jnp.sum(jnp.where(pick, before - 1.0, 0.0), axis=0, keepdims=True)
            picks.append(first)
            ranks.append(rank.astype(jnp.int32))
            counts.append(_dot_nt(ones, onehot)[0:1, :])
        pad_i = jnp.zeros((SUBLANES - 2 * MOBA_TOPK, t), jnp.int32)
        ids_ref[hd] = jnp.concatenate(picks + ranks + [pad_i], axis=0)
        pad_f = jnp.zeros((SUBLANES - MOBA_TOPK, nb), F32)
        cnt_ref[hd] = jnp.concatenate(counts + [pad_f], axis=0)


def _moba_gate(qk, kmean):
    bsz, seq, _ = qk.shape
    t = MOBA_BLOCK
    nb = seq // t
    return pl.pallas_call(
        _moba_gate_kernel,
        grid=(bsz, nb),
        in_specs=[pl.BlockSpec((None, t, A_W), lambda b, i: (b, i, QA_BLK * LANES // A_W)),
                  pl.BlockSpec((None, nb, A_W), lambda b, i: (b, 0, 0))],
        out_specs=[pl.BlockSpec((None, A_HEADS, SUBLANES, t), lambda b, i: (b, 0, 0, i)),
                   pl.BlockSpec((None, A_HEADS, None, SUBLANES, nb), lambda b, i: (b, 0, i, 0, 0)),
                   pl.BlockSpec((None, A_HEADS, t, HEAD_DIM), lambda b, i: (b, 0, i, 0))],
        out_shape=[jax.ShapeDtypeStruct((bsz, A_HEADS, SUBLANES, seq), jnp.int32),
                   jax.ShapeDtypeStruct((bsz, A_HEADS, nb, SUBLANES, nb), F32),
                   jax.ShapeDtypeStruct((bsz, A_HEADS, seq, HEAD_DIM), F32)],
        compiler_params=_cparams("parallel", "parallel"),
        name="moba_gate",
    )(qk, kmean)


def _moba_routes(ids, cnt, seq):
    bsz, heads = ids.shape[:2]
    bh, t = bsz * heads, MOBA_BLOCK
    nb = seq // t
    tiles = _moba_tiles(seq)
    picks = ids[:, :, 0:MOBA_TOPK, :].reshape(bh, MOBA_TOPK, nb, t)
    ranks = ids[:, :, MOBA_TOPK:2 * MOBA_TOPK, :].reshape(bh, MOBA_TOPK, nb, t)
    per_tile = cnt[:, :, :, 0:MOBA_TOPK, :].astype(jnp.int32).reshape(bh, nb * MOBA_TOPK, nb)
    before = jnp.cumsum(per_tile, axis=1) - per_tile
    total = jnp.sum(per_tile, axis=1)
    padded = -(-total // t) * t
    ends = jnp.cumsum(padded, axis=1)
    base = before + (ends - padded)[:, None, :]
    base = base.reshape(bh, nb, MOBA_TOPK, nb).transpose(0, 2, 1, 3)
    onehot = picks[..., None] == jnp.arange(nb)
    pos = jnp.sum(jnp.where(onehot, base[:, :, :, None, :], 0), axis=-1) + ranks
    pos = jnp.where(picks < nb, pos, (tiles - 1) * t)
    pos = pos + (jnp.arange(bh, dtype=jnp.int32) * (tiles * t))[:, None, None, None]
    pos = pos.reshape(bh, MOBA_TOPK, seq).transpose(1, 0, 2).reshape(MOBA_TOPK, bh * seq)
    tile_start = jnp.arange(tiles, dtype=jnp.int32) * t
    tile_blk = jnp.sum(tile_start[None, :, None] >= ends[:, None, :], axis=-1)
    tile_blk = jnp.where(tile_start[None, :] < ends[:, -1:], tile_blk, -1)
    return pos.astype(jnp.int32), tile_blk.astype(jnp.int32)


def _moba_tiles(seq):
    nb = seq // MOBA_BLOCK
    return -(-(MOBA_TOPK * nb + nb + 1) // GROUP_STEP) * GROUP_STEP


def _sc_mesh():
    return plsc.VectorSubcoreMesh(core_axis_name="core", subcore_axis_name="subcore")


def _sc_scatter_rows(x, idx, rows):
    slots, n = idx.shape
    d = x.shape[1]

    @pl.kernel(out_type=jax.ShapeDtypeStruct((rows, d), x.dtype), mesh=_sc_mesh())
    def scatter(x_hbm, i_hbm, o_hbm):
        def body(x_vmem, i_vmem):
            pltpu.sync_copy(x_vmem, o_hbm.at[i_vmem.at[0]])

        pltpu.emit_pipeline(
            body, grid=(slots, n // SC_WINDOW),
            in_specs=[pl.BlockSpec((SC_WINDOW, d), lambda s, i: (i, 0)),
                      pl.BlockSpec((1, SC_WINDOW), lambda s, i: (s, i))],
            out_specs=[],
            core_axis_name=("core", "subcore"),
            dimension_semantics=(pltpu.PARALLEL, pltpu.PARALLEL),
        )(x_hbm, i_hbm)

    return scatter(x, idx)


def _sc_gather_rows(x, idx):
    n = idx.shape[0]
    d = x.shape[1]

    @pl.kernel(out_type=jax.ShapeDtypeStruct((n, d), x.dtype), mesh=_sc_mesh())
    def gather(x_hbm, i_hbm, o_hbm):
        def body(i_vmem, o_vmem):
            pltpu.sync_copy(x_hbm.at[i_vmem.at[0]], o_vmem)

        pltpu.emit_pipeline(
            body, grid=(n // SC_WINDOW,),
            in_specs=[pl.BlockSpec((1, SC_WINDOW), lambda i: (0, i))],
            out_specs=[pl.BlockSpec((SC_WINDOW, d), lambda i: (i, 0))],
            core_axis_name=("core", "subcore"),
            dimension_semantics=(pltpu.PARALLEL,),
        )(i_hbm, o_hbm)

    return gather(x, idx.reshape(1, n))


def _moba_group_kernel(tb_ref, q_ref, *refs):
    t = MOBA_BLOCK
    k_refs, v_refs = refs[:GROUP_STEP], refs[GROUP_STEP:2 * GROUP_STEP]
    o_ref, lse_ref = refs[-2:]
    g, step = pl.program_id(0), pl.program_id(1)
    first = step * GROUP_STEP

    @pl.when(tb_ref[g, first] < 0)
    def _():
        o_ref[...] = jnp.zeros(o_ref.shape, o_ref.dtype)
        lse_ref[...] = jnp.full(lse_ref.shape, MASKED, lse_ref.dtype)

    @pl.when(tb_ref[g, first] >= 0)
    def _():
        for u in range(GROUP_STEP):
            used = tb_ref[g, first + u] >= 0
            rows = slice(u * t, (u + 1) * t)
            o, lse = _block_attention(q_ref[rows, :].astype(BF16), k_refs[u][...], v_refs[u][...])
            o_ref[rows, :] = jnp.where(used, o, 0.0)
            lse_ref[rows, :] = jnp.where(used, lse, MASKED)


def _moba_group_attention(q_grouped, tile_blk, qk, v, after):
    bh, rows, _ = q_grouped.shape
    t = MOBA_BLOCK
    tiles = rows // t
    heads = A_HEADS

    def block_of(u, first_col):
        return lambda g, s, tb: (g // heads, jnp.maximum(tb[g, s * GROUP_STEP + u], 0),
                                 first_col + g % heads)

    row_tile = pl.BlockSpec((None, GROUP_STEP * t, HEAD_DIM), lambda g, s, tb: (g, s, 0))
    key_value = lambda first_col: [pl.BlockSpec((None, t, HEAD_DIM), block_of(u, first_col))
                                   for u in range(GROUP_STEP)]
    grid_spec = pltpu.PrefetchScalarGridSpec(
        num_scalar_prefetch=1,
        grid=(bh, tiles // GROUP_STEP),
        in_specs=([row_tile] + key_value(KA_BLK) + key_value(0)
                  + [pl.BlockSpec(memory_space=pl.ANY)]),
        out_specs=[row_tile, row_tile],
    )
    return pl.pallas_call(
        _moba_group_kernel,
        grid_spec=grid_spec,
        out_shape=[jax.ShapeDtypeStruct(q_grouped.shape, F32)] * 2,
        compiler_params=_cparams("parallel", "parallel"),
        name="moba_group",
    )(tile_blk, q_grouped, *([qk] * GROUP_STEP), *([v] * GROUP_STEP), after)


def _moba_merge_kernel(q_ref, k_ref, v_ref, po_ref, pl_ref, o_ref):
    t = MOBA_BLOCK
    causal = (lax.broadcasted_iota(jnp.int32, (t, t), 1)
              <= lax.broadcasted_iota(jnp.int32, (t, t), 0))
    for hd in range(A_HEADS):
        sl = slice(hd * HEAD_DIM, (hd + 1) * HEAD_DIM)
        o_own, lse_own = _block_attention(q_ref[:, sl], k_ref[:, sl], v_ref[:, sl], causal)
        outs = [o_own] + [po_ref[s, hd] for s in range(MOBA_TOPK)]
        lses = [lse_own] + [pl_ref[s, hd] for s in range(MOBA_TOPK)]
        top = functools.reduce(jnp.maximum, lses)
        weights = [jnp.exp2(l - top) for l in lses]
        num = sum(w * o for w, o in zip(weights, outs))
        o_ref[:, sl] = (num / sum(weights)).astype(o_ref.dtype)


def _moba_merge(qk, v, part_o, part_lse):
    bsz, seq, _ = qk.shape
    t = MOBA_BLOCK
    part = pl.BlockSpec((MOBA_TOPK, None, A_HEADS, t, HEAD_DIM), lambda b, i: (0, b, 0, i, 0))
    return pl.pallas_call(
        _moba_merge_kernel,
        grid=(bsz, seq // t),
        in_specs=[pl.BlockSpec((None, t, A_W), lambda b, i: (b, i, QA_BLK * LANES // A_W)),
                  pl.BlockSpec((None, t, A_W), lambda b, i: (b, i, KA_BLK * LANES // A_W)),
                  pl.BlockSpec((None, t, A_W), lambda b, i: (b, i, 0)),
                  part, part],
        out_specs=pl.BlockSpec((None, t, A_W), lambda b, i: (b, i, 0)),
        out_shape=jax.ShapeDtypeStruct((bsz, seq, A_W), BF16),
        compiler_params=_cparams("parallel", "parallel"),
        name="moba_merge",
    )(qk, qk, v, part_o, part_lse)


def _moba_regroup(qk, kmean):
    bsz, seq, _ = qk.shape
    bh = bsz * A_HEADS
    rows = _moba_tiles(seq) * MOBA_BLOCK
    ids, cnt, q_f32 = _moba_gate(qk, kmean)
    pos, tile_blk = _moba_routes(ids, cnt, seq)
    q_grouped = _sc_scatter_rows(q_f32.reshape(bh * seq, HEAD_DIM), pos, bh * rows)
    return q_grouped.reshape(bh, rows, HEAD_DIM), pos, tile_blk


def _moba_picked_blocks(q_grouped, pos, tile_blk, qk, v, after):
    bsz, seq, _ = qk.shape
    bh, rows, _ = q_grouped.shape
    o_g, lse_g = _moba_group_attention(q_grouped, tile_blk, qk, v, after)
    flat = pos.reshape(-1)
    back = lambda a: _sc_gather_rows(a.reshape(bh * rows, HEAD_DIM), flat).reshape(
        MOBA_TOPK, bsz, A_HEADS, seq, HEAD_DIM)
    return back(o_g), back(lse_g)


def _proj_dilated_kernel(h_ref, w_ref, c_ref, s_ref, q_ref, k_ref, v_ref, sc, *, d):
    acc = _dot(h_ref[...], w_ref[...])
    c, s = c_ref[...], s_ref[...]
    q_scale = HEAD_DIM ** -0.5 * LOG2E
    for j in range(acc.shape[1] // LANES):
        blk = acc[:, j * LANES:(j + 1) * LANES]
        if j < B_HEADS:
            blk = _rope128(blk, c, s) * q_scale
        elif j < 2 * B_HEADS:
            blk = _rope128(blk, c, s)
        sc[j] = blk
    rows = acc.shape[0] // d
    for r in range(d):
        for j in range(acc.shape[1] // LANES):
            dst = (q_ref, k_ref, v_ref)[j // B_HEADS]
            col = (j % B_HEADS) * LANES
            dst[r, :, col:col + LANES] = sc[j, pl.ds(r, rows, stride=d), :].astype(dst.dtype)


def _proj_dilated(h, w, rope_h, d, bsz, seq, tm=512):
    m, k = h.shape
    nt = seq // tm
    pos = pl.BlockSpec((tm, LANES), lambda i: (i % nt, 0))
    out = pl.BlockSpec((None, d, tm // d, B_W), lambda i: (i // nt, 0, i % nt, 0))
    return pl.pallas_call(
        functools.partial(_proj_dilated_kernel, d=d),
        grid=(m // tm,),
        in_specs=[pl.BlockSpec((tm, k), lambda i: (i, 0)), pl.BlockSpec(w.shape, lambda i: (0, 0)),
                  pos, pos],
        out_specs=[out] * 3,
        out_shape=[jax.ShapeDtypeStruct((bsz, d, seq // d, B_W), BF16)] * 3,
        scratch_shapes=[pltpu.VMEM((w.shape[1] // LANES, tm, LANES), F32)],
        compiler_params=_cparams("parallel"),
        name=f"proj_dilated_d{d}",
    )(h, w, *rope_h)


def _dilated_kernel(q_ref, kc_ref, kp_ref, vc_ref, vp_ref, o_ref, lse_ref, *, span):
    t, tp = q_ref.shape[0], kp_ref.shape[0]
    i = pl.program_id(2)
    shape = (2 * tp, tp)
    key_row = lax.broadcasted_iota(jnp.int32, shape, 0)
    dist = lax.broadcasted_iota(jnp.int32, shape, 1) + tp - key_row
    visible = (dist >= 0) & (dist <= span)
    bias = jnp.where(visible, 0.0, MASKED)
    bias_first = jnp.where(visible & ((key_row >= tp) | (i > 0)), 0.0, MASKED)
    ones = jnp.ones((BF16_ROWS, tp + t), BF16)

    def transposed(x):
        return x.astype(F32).T.astype(BF16)

    for j in range(B_HEADS):
        sl = slice(j * LANES, (j + 1) * LANES)
        qT = transposed(q_ref[:, sl])
        k_all = jnp.concatenate([kp_ref[:, sl], kc_ref[:, sl]], axis=0)
        vt_all = jnp.concatenate([transposed(vp_ref[:, sl]), transposed(vc_ref[:, sl])], axis=1)
        vt_all = jnp.concatenate([vt_all, ones], axis=0)
        outs, lses = [], []
        for u in range(t // tp):
            window = slice(u * tp, (u + 2) * tp)
            s = _dot(k_all[window, :], qT[:, u * tp:(u + 1) * tp])
            s = s + (bias_first if u == 0 else bias)
            m = jnp.max(s, axis=0, keepdims=True)
            p = jnp.exp2(s - m).astype(BF16)
            acc = _dot(vt_all[:, window], p)
            den = acc[V_DIM:V_DIM + 1, :]
            outs.append(acc[:V_DIM, :] / den)
            lses.append(m + jnp.log2(den))
        o_ref[:, sl] = jnp.concatenate(outs, axis=1).T
        lse = jnp.concatenate(lses, axis=1)
        lse_ref[:, sl] = jnp.broadcast_to(lse, (LANES, t)).T


def _dilated_attention(q, k, v, span, t=1024):
    bsz, d, length, _ = q.shape
    t = min(t, length)
    tp = B_QBLOCK
    assert span <= tp and t % tp == 0
    cur = pl.BlockSpec((None, None, t, B_W), lambda b, r, i: (b, r, i, 0))
    prev = pl.BlockSpec((None, None, tp, B_W),
                        lambda b, r, i: (b, r, jnp.maximum(i * (t // tp) - 1, 0), 0))
    return pl.pallas_call(
        functools.partial(_dilated_kernel, span=span),
        grid=(bsz, d, length // t),
        in_specs=[cur, cur, prev, cur, prev],
        out_specs=[cur, cur],
        out_shape=[jax.ShapeDtypeStruct(q.shape, F32)] * 2,
        compiler_params=_cparams("parallel", "parallel", "parallel"),
        name=f"dilated_d{d}",
    )(q, k, k, v, v)


def _natural_rows(ref, sc):
    d, rows = ref.shape[0], ref.shape[1]
    if d == 1:
        return ref[0]
    for r in range(d):
        for j in range(B_HEADS):
            sc[j, pl.ds(r, rows, stride=d), :] = ref[r, :, j * LANES:(j + 1) * LANES]
    return jnp.concatenate([sc[j] for j in range(B_HEADS)], axis=-1)


def _mixer_tail_kernel(x_ref, oa_ref, o0_ref, o1_ref, o2_ref, l0_ref, l1_ref, l2_ref, oc_ref,
                       g_ref, wpa_ref, wpb_ref, wpc_ref, wo_ref, y_ref, *scratch):
    o0, o1, o2, l0, l1, l2 = [
        _natural_rows(ref, sc)
        for ref, sc in zip((o0_ref, o1_ref, o2_ref, l0_ref, l1_ref, l2_ref), scratch)]
    mx = jnp.maximum(jnp.maximum(l0, l1), l2)
    e0, e1, e2 = jnp.exp2(l0 - mx), jnp.exp2(l1 - mx), jnp.exp2(l2 - mx)
    ob = (e0 * o0 + e1 * o1 + e2 * o2) / (e0 + e1 + e2)
    pa = _dot(oa_ref[...], wpa_ref[...])
    pb = _dot(ob.astype(BF16), wpb_ref[...])
    pc = _dot(oc_ref[...], wpc_ref[...])
    d = D_MODEL
    merged = (g_ref[:, 0:d].astype(F32) * pa + g_ref[:, d:2 * d].astype(F32) * pb
              + g_ref[:, 2 * d:3 * d].astype(F32) * pc)
    y_ref[...] = x_ref[...] + _dot(merged.astype(BF16), wo_ref[...])


def _mixer_tail(x, out_a, o_groups, lse_groups, out_c, gates, w_pa, w_pb, w_pc, w_o, seq, tm=256):
    m, d = x.shape
    nt = seq // tm
    row = lambda width: pl.BlockSpec((tm, width), lambda i: (i, 0))
    residue = lambda g: pl.BlockSpec((None, g.shape[1], tm // g.shape[1], B_W),
                                     lambda i: (i // nt, 0, i % nt, 0))
    weights = [_resident(w) for w in (w_pa, w_pb, w_pc, w_o)]
    groups = list(o_groups) + list(lse_groups)
    return pl.pallas_call(
        _mixer_tail_kernel,
        grid=(m // tm,),
        in_specs=([row(d), row(A_W)] + [residue(g) for g in groups]
                  + [row(C_W), row(3 * d)] + weights),
        out_specs=row(d),
        out_shape=jax.ShapeDtypeStruct((m, d), F32),
        scratch_shapes=[pltpu.VMEM((B_HEADS, tm, LANES), F32) for _ in groups],
        compiler_params=_cparams("parallel"),
        name="mixer_tail",
    )(x, out_a, *groups, out_c, gates, w_pa, w_pb, w_pc, w_o)


def _mem_kv_kernel(mem_ref, g_ref, wk_ref, wv_ref, k_ref, v_ref):
    memn = _rms(mem_ref[...], g_ref[...]).astype(BF16)
    k_ref[...] = _dot(memn, wk_ref[...]).astype(k_ref.dtype)
    v_ref[...] = _dot(memn, wv_ref[...]).astype(v_ref.dtype)


def _mem_kv(mem, g, wk, wv):
    bsz, n, d = mem.shape
    out = pl.BlockSpec((None, n, X_W), lambda b: (b, 0, 0))
    return pl.pallas_call(
        _mem_kv_kernel,
        grid=(bsz,),
        in_specs=[pl.BlockSpec((None, n, d), lambda b: (b, 0, 0)),
                  pl.BlockSpec((1, d), lambda b: (0, 0)),
                  pl.BlockSpec(wk.shape, lambda b: (0, 0)),
                  pl.BlockSpec(wv.shape, lambda b: (0, 0))],
        out_specs=[out, out],
        out_shape=[jax.ShapeDtypeStruct((bsz, n, X_W), BF16)] * 2,
        compiler_params=_cparams("parallel"),
        name="mem_kv",
    )(mem, g.reshape(1, d), wk, wv)


def _mem_attn_kernel(x_ref, g_ref, wq_ref, k_ref, v_ref, wo_ref, y_ref):
    x = x_ref[...]
    h = _rms(x, g_ref[...]).astype(BF16)
    q = (_dot(h, wq_ref[...]) * HEAD_DIM ** -0.5).astype(BF16)
    heads = []
    for hd in range(X_HEADS):
        sl = slice(hd * HEAD_DIM, (hd + 1) * HEAD_DIM)
        s = _dot_nt(q[:, sl], k_ref[:, sl])
        p = jnp.exp(s - jnp.max(s, axis=-1, keepdims=True))
        o = _dot(p.astype(BF16), v_ref[:, sl]) / jnp.sum(p, axis=-1, keepdims=True)
        heads.append(o.astype(BF16))
    y_ref[...] = x + _dot(jnp.concatenate(heads, axis=-1), wo_ref[...])


def _mem_attention(x, g, wq, kmem, vmem, wo, seq, tm=512):
    m, d = x.shape
    nt = seq // tm
    n = kmem.shape[1]
    kv = pl.BlockSpec((None, n, X_W), lambda i: (i // nt, 0, 0))
    return pl.pallas_call(
        _mem_attn_kernel,
        grid=(m // tm,),
        in_specs=[pl.BlockSpec((tm, d), lambda i: (i, 0)),
                  pl.BlockSpec((1, d), lambda i: (0, 0)),
                  pl.BlockSpec(wq.shape, lambda i: (0, 0)), kv, kv,
                  pl.BlockSpec(wo.shape, lambda i: (0, 0))],
        out_specs=pl.BlockSpec((tm, d), lambda i: (i, 0)),
        out_shape=jax.ShapeDtypeStruct((m, d), F32),
        compiler_params=_cparams("parallel"),
        name="mem_attention",
    )(x, g.reshape(1, d), wq, kmem, vmem, wo)


def _ffn_up_kernel(x_ref, halo_ref, g_ref, wg_ref, wv_ref, cwg_ref, cwv_ref, cbg_ref, cbv_ref,
                   act_ref, h_sc, *, tiles_per_seq):
    i = pl.program_id(0)
    tm = x_ref.shape[0]

    @pl.when(pl.program_id(1) == 0)
    def _():
        g = g_ref[...]
        keep = (i % tiles_per_seq != 0).astype(F32)
        h_sc[0:HALO, :] = (_rms(halo_ref[...], g) * keep).astype(h_sc.dtype)
        h_sc[HALO:, :] = _rms(x_ref[...], g).astype(h_sc.dtype)

    h = h_sc[...]

    def conv(w_ref, cw_ref, cb_ref):
        u = _dot(h, w_ref[...])
        c = cb_ref[...]
        for tap in range(CONV_W):
            lo = HALO - (CONV_W - 1) + tap
            c = c + cw_ref[tap:tap + 1, :] * u[lo:lo + tm, :]
        return c

    act = jax.nn.silu(conv(wg_ref, cwg_ref, cbg_ref)) * conv(wv_ref, cwv_ref, cbv_ref)
    act_ref[...] = act.astype(act_ref.dtype)


def _ffn_down_kernel(a_ref, w_ref, x_ref, y_ref):
    y_ref[...] = x_ref[...] + _dot(a_ref[...], w_ref[...])


def _conv_ffn(x, g, w_up, conv_w, conv_b, w_down, layer, seq):
    m, d = x.shape
    act = _ffn_up(x, g, w_up, conv_w, conv_b, layer, seq)
    tm, tn = 1024, FFN_TF
    return pl.pallas_call(
        _ffn_down_kernel,
        grid=(m // tm, d // tn),
        in_specs=[pl.BlockSpec((tm, D_FF), lambda i, j: (i, 0)),
                  pl.BlockSpec((None, D_FF, tn), lambda i, j: (layer, 0, j)),
                  pl.BlockSpec((tm, tn), lambda i, j: (i, j))],
        out_specs=pl.BlockSpec((tm, tn), lambda i, j: (i, j)),
        out_shape=jax.ShapeDtypeStruct((m, d), F32),
        compiler_params=_cparams("parallel", "parallel"),
        name="ffn_down",
    )(act, w_down, x)


def _ffn_up(x, g, w_up, conv_w, conv_b, layer, seq, tm=1024, tf=FFN_TF):
    m, d = x.shape
    nf = D_FF_PAD // tf
    halo_blocks = tm // HALO
    return pl.pallas_call(
        functools.partial(_ffn_up_kernel, tiles_per_seq=seq // tm),
        grid=(m // tm, nf),
        in_specs=[pl.BlockSpec((tm, d), lambda i, f: (i, 0)),
                  pl.BlockSpec((HALO, d), lambda i, f: (jnp.maximum(i * halo_blocks - 1, 0), 0)),
                  pl.BlockSpec((1, d), lambda i, f: (0, 0)),
                  pl.BlockSpec((None, d, tf), lambda i, f: (layer, 0, f)),
                  pl.BlockSpec((None, d, tf), lambda i, f: (layer, 0, f + nf)),
                  pl.BlockSpec((CONV_W, tf), lambda i, f: (0, f)),
                  pl.BlockSpec((CONV_W, tf), lambda i, f: (0, f + nf)),
                  pl.BlockSpec((1, tf), lambda i, f: (0, f)),
                  pl.BlockSpec((1, tf), lambda i, f: (0, f + nf))],
        out_specs=pl.BlockSpec((tm, tf), lambda i, f: (i, f)),
        out_shape=jax.ShapeDtypeStruct((m, D_FF_PAD), BF16),
        scratch_shapes=[pltpu.VMEM((HALO + tm, d), BF16)],
        compiler_params=_cparams("parallel", "arbitrary"),
        name="ffn_up",
    )(x, x, g.reshape(1, d), w_up, w_up, conv_w, conv_w, conv_b, conv_b)


def _rope_tables(seq):
    def angles(dim):
        inv_freq = jnp.exp(jnp.arange(0, dim, 2, dtype=F32) * (-math.log(ROPE_THETA) / dim))
        ang = jnp.arange(seq, dtype=F32)[:, None] * inv_freq[None, :]
        return jnp.cos(ang), jnp.sin(ang)

    cos_h, sin_h = angles(HEAD_DIM)
    rope_h = (jnp.concatenate([cos_h, cos_h], axis=-1), jnp.concatenate([-sin_h, sin_h], axis=-1))
    cos_r, sin_r = angles(ROPE_DIM)
    z = jnp.zeros_like(cos_r)
    rope_r = (jnp.concatenate([cos_r, cos_r, z, z], axis=-1),
              jnp.concatenate([-sin_r, z, z, z], axis=-1),
              jnp.concatenate([z, sin_r, z, z], axis=-1))
    return rope_h, rope_r


def _split_in(w_in):
    return [w_in[:, IN_OFFSETS[k]:IN_OFFSETS[k + 1]] for k in range(len(IN_WIDTHS))]


def _pad_cols(w, width):
    return jnp.pad(w, ((0, 0), (0, width - w.shape[1])))


def _layer_params(w_in, w_uq, w_ukv, conv_w, conv_b):
    qa, ka, va, qb, kb, vb, cq, ckv, kr, gates = _split_in(w_in)
    w_qk = jnp.concatenate([qa, ka], axis=1).astype(BF16)
    group_cols = lambda w, g: w[:, g * B_W:(g + 1) * B_W]
    w_b = [jnp.concatenate([group_cols(qb, g), group_cols(kb, g), group_cols(vb, g)],
                           axis=1).astype(BF16) for g in range(len(B_GROUPS))]
    w_down_in = jnp.concatenate([cq, ckv, _pad_cols(kr, LANES)], axis=1).astype(BF16)
    uq = w_uq.reshape(Q_LORA, C_HEADS, NOPE_DIM + ROPE_DIM)
    uq = jnp.pad(uq, ((0, 0), (0, 0), (0, C_QK - NOPE_DIM - ROPE_DIM)))
    ukv = w_ukv.reshape(KV_LORA, C_HEADS, NOPE_DIM + V_DIM)
    return dict(
        w_qk=w_qk, w_va=va.astype(BF16), w_b=w_b, w_gates=gates.astype(BF16),
        w_down_in=w_down_in,
        w_uq=uq.reshape(Q_LORA, C_HEADS * C_QK).astype(BF16),
        w_uk=ukv[:, :, :NOPE_DIM].reshape(KV_LORA, C_HEADS * NOPE_DIM).astype(BF16),
        w_uv=ukv[:, :, NOPE_DIM:].reshape(KV_LORA, C_W).astype(BF16),
        conv_w=_pad_ff_halves(conv_w),
        conv_b=_pad_ff_halves(conv_b.reshape(1, -1)),
    )


def _pad_ff_halves(w):
    pad = [(0, 0)] * (w.ndim - 1) + [(0, D_FF_PAD - D_FF)]
    return jnp.concatenate([jnp.pad(w[..., :D_FF], pad), jnp.pad(w[..., D_FF:], pad)], axis=-1)


def _qk_col_scale():
    q_scale = HEAD_DIM ** -0.5
    parts = [jnp.full((A_W,), q_scale * LOG2E, F32), jnp.ones((A_W,), F32)]
    return jnp.concatenate(parts).reshape(1, QK_W)


def _mixer(x, g_mix, p, g_cq, g_ckv, w_pa, w_pb, w_pc, w_o, rope_h, rope_r, bsz, seq):
    m = x.shape[0]
    h = _rmsnorm(x, g_mix, BF16)
    qk = _matmul(h, p["w_qk"], _mm_rope_kernel, BF16, 1024, 1024, seq=seq,
                 extras=(("col", _qk_col_scale()), ("pos", rope_h[0]), ("pos", rope_h[1])),
                 name="proj_qk_rope")
    qk3 = qk.reshape(bsz, seq, QK_W)
    q_grouped, pos, tile_blk = _moba_regroup(qk3, _kmean(qk3))
    v_a = _matmul(h, p["w_va"], _mm_plain_kernel, BF16, 1024, A_W, name="proj_va")
    gates = _matmul(h, p["w_gates"], _mm_sigmoid_kernel, BF16, 1024, 1024, name="proj_gates")
    v_a3 = v_a.reshape(bsz, seq, A_W)
    part_o, part_lse = _moba_picked_blocks(q_grouped, pos, tile_blk, qk3, v_a3, after=gates)
    cq, ckv, kr = _mla_down(h, p["w_down_in"], g_cq, g_ckv, rope_r, seq)
    q_c = _mla_q(cq, p["w_uq"], rope_r, seq)
    k_c, vt_c = _mla_kv(ckv, kr, p["w_uk"], p["w_uv"], bsz, seq)
    groups = []
    for (window, d), w_g in zip(B_GROUPS, p["w_b"]):
        q_g, k_g, v_g = _proj_dilated(h, w_g, rope_h, d, bsz, seq)
        groups.append(_dilated_attention(q_g, k_g, v_g, window // d))
    out_c = _flash_attention(q_c.reshape(bsz, seq, -1), k_c.reshape(bsz, seq, -1), vt_c,
                             C_HEADS, C_QK).reshape(m, C_W)
    out_a = _moba_merge(qk3, v_a3, part_o, part_lse).reshape(m, A_W)
    return _mixer_tail(x, out_a, [g[0] for g in groups], [g[1] for g in groups], out_c, gates,
                       w_pa.astype(BF16), w_pb.astype(BF16), w_pc.astype(BF16), w_o.astype(BF16),
                       seq)


def kernel(x, mem, g_mix, w_in, g_cq, g_ckv, w_uq, w_ukv, w_pa, w_pb, w_pc, w_o, g_mem, g_memkv,
           w_xq, w_xk, w_xv, w_xo, g_ffn, w_up, conv_w, conv_b, w_down, g_final):
    bsz, seq, d = x.shape
    rope_h, rope_r = _rope_tables(seq)
    xf = x.reshape(bsz * seq, d)
    w_down = w_down.astype(BF16)
    w_up = _pad_ff_halves(w_up.astype(BF16))
    for l in range(DEPTH):
        p = _layer_params(w_in[l], w_uq[l], w_ukv[l], conv_w[l], conv_b[l])
        xf = _mixer(xf, g_mix[l], p, g_cq[l], g_ckv[l], w_pa[l], w_pb[l], w_pc[l], w_o[l],
                    rope_h, rope_r, bsz, seq)
        kmem, vmem = _mem_kv(mem, g_memkv[l], w_xk[l].astype(BF16), w_xv[l].astype(BF16))
        xf = _mem_attention(xf, g_mem[l], w_xq[l].astype(BF16), kmem, vmem,
                            w_xo[l].astype(BF16), seq)
        xf = _conv_ffn(xf, g_ffn[l], w_up, p["conv_w"], p["conv_b"], w_down, l, seq)
    return _rmsnorm(xf, g_final, F32).reshape(bsz, seq, d)
```

```python
import functools
import math

import jax
import jax.numpy as jnp
import numpy as np
from jax import lax
from jax.experimental import pallas as pl
from jax.experimental.pallas import tpu as pltpu
from jax.experimental.pallas import tpu_sc as plsc

F32 = jnp.float32
BF16 = jnp.bfloat16

LANES = 128
SUBLANES = 8
V7X_VMEM_BYTES = 64 * 1024 * 1024
VMEM_LIMIT = V7X_VMEM_BYTES * 7 // 8

D_MODEL = 2048
DEPTH = 2
HEAD_DIM = 128
ROPE_THETA = 10000.0
EPS = 1e-6

A_HEADS = 4
MOBA_BLOCK = 256
MOBA_TOPK = 3

B_GROUPS = ((128, 1), (512, 4), (2048, 16))
B_HEADS = 4
B_QBLOCK = 128

C_HEADS = 8
Q_LORA = 1536
KV_LORA = 512
NOPE_DIM = 128
ROPE_DIM = 64
V_DIM = 128

X_HEADS = 4
D_FF = 5504
CONV_W = 3

A_W = A_HEADS * HEAD_DIM
B_QKV_W = len(B_GROUPS) * B_HEADS * HEAD_DIM
B_W = B_HEADS * HEAD_DIM
C_W = C_HEADS * V_DIM
X_W = X_HEADS * HEAD_DIM
IN_WIDTHS = (A_W, A_W, A_W, B_QKV_W, B_QKV_W, B_QKV_W, Q_LORA, KV_LORA, ROPE_DIM, 3 * D_MODEL)
IN_OFFSETS = tuple(int(o) for o in np.cumsum((0,) + IN_WIDTHS))

QK_W = 2 * A_W
QA_BLK, KA_BLK = 0, A_W // LANES

C_QK = 2 * LANES
MASKED = -1e30
LOG2E = math.log2(math.e)
BF16_ROWS = 16
VT_ROWS = V_DIM + BF16_ROWS
GROUP_STEP = 8
SC_WINDOW = 128

FFN_TF = 512
D_FF_PAD = -(-D_FF // FFN_TF) * FFN_TF
HALO = SUBLANES


def _cparams(*sem):
    return pltpu.CompilerParams(dimension_semantics=sem, vmem_limit_bytes=VMEM_LIMIT)


def _resident(arr):
    zeros = (0,) * arr.ndim
    return pl.BlockSpec(arr.shape, lambda *_: zeros, pipeline_mode=pl.Buffered(1))


def _dot(a, b):
    return jnp.dot(a, b, preferred_element_type=F32)


def _dot_nt(a, b):
    return lax.dot_general(a, b, (((1,), (1,)), ((), ())), preferred_element_type=F32)


def _rms(x, g):
    return x * lax.rsqrt(jnp.mean(x * x, axis=-1, keepdims=True) + EPS) * g


def _rmsnorm_kernel(x_ref, g_ref, o_ref):
    o_ref[...] = _rms(x_ref[...], g_ref[...]).astype(o_ref.dtype)


def _rmsnorm(x, g, out_dtype, tm=512):
    m, d = x.shape
    return pl.pallas_call(
        _rmsnorm_kernel,
        grid=(m // tm,),
        in_specs=[pl.BlockSpec((tm, d), lambda i: (i, 0)),
                  pl.BlockSpec((1, d), lambda i: (0, 0))],
        out_specs=pl.BlockSpec((tm, d), lambda i: (i, 0)),
        out_shape=jax.ShapeDtypeStruct((m, d), out_dtype),
        compiler_params=_cparams("parallel"),
        name="rmsnorm",
    )(x, g.reshape(1, d))


def _rope128(x, c, s):
    return x * c + pltpu.roll(x, HEAD_DIM // 2, 1) * s


def _rope64(x, c, sa, sb):
    half = ROPE_DIM // 2
    return x * c + pltpu.roll(x, LANES - half, 1) * sa + pltpu.roll(x, half, 1) * sb


def _mm_plain_kernel(a_ref, w_ref, o_ref):
    o_ref[...] = _dot(a_ref[...], w_ref[...]).astype(o_ref.dtype)


def _mm_sigmoid_kernel(a_ref, w_ref, o_ref):
    o_ref[...] = jax.nn.sigmoid(_dot(a_ref[...], w_ref[...])).astype(o_ref.dtype)


def _mm_rope_kernel(a_ref, w_ref, cs_ref, c_ref, s_ref, o_ref):
    acc = _dot(a_ref[...], w_ref[...])
    c = c_ref[...]
    s = s_ref[...]
    for j in range(acc.shape[1] // LANES):
        sl = slice(j * LANES, (j + 1) * LANES)
        o_ref[:, sl] = (_rope128(acc[:, sl], c, s) * cs_ref[:, sl]).astype(o_ref.dtype)


def _matmul(a, w, kernel, out_dtype, tm, tn, seq=None, extras=(), name="matmul"):
    m, k = a.shape
    n = w.shape[1]
    in_specs = [pl.BlockSpec((tm, k), lambda i, j: (i, 0)),
                pl.BlockSpec((k, tn), lambda i, j: (0, j))]
    args = [a, w]
    for kind, arr in extras:
        if kind == "col":
            in_specs.append(pl.BlockSpec((1, tn), lambda i, j: (0, j)))
        else:
            nt = seq // tm
            in_specs.append(pl.BlockSpec((tm, LANES), lambda i, j: (i % nt, 0)))
        args.append(arr)
    return pl.pallas_call(
        kernel,
        grid=(m // tm, n // tn),
        in_specs=in_specs,
        out_specs=pl.BlockSpec((tm, tn), lambda i, j: (i, j)),
        out_shape=jax.ShapeDtypeStruct((m, n), out_dtype),
        compiler_params=_cparams("parallel", "parallel"),
        name=name,
    )(*args)


def _mla_down_kernel(h_ref, w_ref, gq_ref, gkv_ref, c_ref, sa_ref, sb_ref,
                     cq_ref, ckv_ref, kr_ref):
    acc = _dot(h_ref[...], w_ref[...])
    cq_ref[...] = _rms(acc[:, :Q_LORA], gq_ref[...]).astype(cq_ref.dtype)
    ckv_ref[...] = _rms(acc[:, Q_LORA:Q_LORA + KV_LORA], gkv_ref[...]).astype(ckv_ref.dtype)
    kr = acc[:, Q_LORA + KV_LORA:]
    kr_ref[...] = _rope64(kr, c_ref[...], sa_ref[...], sb_ref[...]).astype(kr_ref.dtype)


def _mla_down(h, w, g_cq, g_ckv, rope_r, seq, tm=512):
    m, k = h.shape
    n = w.shape[1]
    nt = seq // tm
    row = lambda width: pl.BlockSpec((tm, width), lambda i: (i, 0))
    full = lambda r, c: pl.BlockSpec((r, c), lambda i: (0, 0))
    pos = pl.BlockSpec((tm, LANES), lambda i: (i % nt, 0))
    return pl.pallas_call(
        _mla_down_kernel,
        grid=(m // tm,),
        in_specs=[row(k), full(k, n), full(1, Q_LORA), full(1, KV_LORA), pos, pos, pos],
        out_specs=[row(Q_LORA), row(KV_LORA), row(LANES)],
        out_shape=[jax.ShapeDtypeStruct((m, Q_LORA), BF16),
                   jax.ShapeDtypeStruct((m, KV_LORA), BF16),
                   jax.ShapeDtypeStruct((m, LANES), BF16)],
        compiler_params=_cparams("parallel"),
        name="mla_down",
    )(h, w, g_cq.reshape(1, -1), g_ckv.reshape(1, -1), *rope_r)


def _mla_q_kernel(cq_ref, w_ref, c_ref, sa_ref, sb_ref, q_ref, *, scale):
    acc = _dot(cq_ref[...], w_ref[...])
    c, sa, sb = c_ref[...], sa_ref[...], sb_ref[...]
    for hd in range(C_HEADS):
        lo = hd * C_QK
        q_ref[:, lo:lo + LANES] = (acc[:, lo:lo + LANES] * scale).astype(q_ref.dtype)
        rope = _rope64(acc[:, lo + LANES:lo + C_QK], c, sa, sb)
        q_ref[:, lo + LANES:lo + C_QK] = (rope * scale).astype(q_ref.dtype)


def _mla_q(cq, w, rope_r, seq, tm=512):
    m, k = cq.shape
    n = w.shape[1]
    nt = seq // tm
    pos = pl.BlockSpec((tm, LANES), lambda i: (i % nt, 0))
    return pl.pallas_call(
        functools.partial(_mla_q_kernel, scale=(NOPE_DIM + ROPE_DIM) ** -0.5 * LOG2E),
        grid=(m // tm,),
        in_specs=[pl.BlockSpec((tm, k), lambda i: (i, 0)),
                  pl.BlockSpec((k, n), lambda i: (0, 0)), pos, pos, pos],
        out_specs=pl.BlockSpec((tm, n), lambda i: (i, 0)),
        out_shape=jax.ShapeDtypeStruct((m, n), BF16),
        compiler_params=_cparams("parallel"),
        name="mla_q",
    )(cq, w, *rope_r)


def _store_vt(v, vt_ref):
    vt = v.T
    for hd in range(vt_ref.shape[0]):
        vt_ref[hd, 0:V_DIM, :] = vt[hd * V_DIM:(hd + 1) * V_DIM, :].astype(vt_ref.dtype)
        vt_ref[hd, V_DIM:VT_ROWS, :] = jnp.ones((VT_ROWS - V_DIM, vt.shape[1]), vt_ref.dtype)


def _mla_kv_kernel(ckv_ref, kr_ref, wk_ref, wv_ref, k_ref, vt_ref):
    ckv = ckv_ref[...]
    kn = _dot(ckv, wk_ref[...])
    kr = kr_ref[...]
    for hd in range(C_HEADS):
        lo = hd * C_QK
        k_ref[:, lo:lo + LANES] = kn[:, hd * LANES:(hd + 1) * LANES].astype(k_ref.dtype)
        k_ref[:, lo + LANES:lo + C_QK] = kr
    _store_vt(_dot(ckv, wv_ref[...]), vt_ref)


def _mla_kv(ckv, kr, wk, wv, bsz, seq, tm=512):
    m, k = ckv.shape
    nt = seq // tm
    return pl.pallas_call(
        _mla_kv_kernel,
        grid=(m // tm,),
        in_specs=[pl.BlockSpec((tm, k), lambda i: (i, 0)),
                  pl.BlockSpec((tm, LANES), lambda i: (i, 0)),
                  pl.BlockSpec(wk.shape, lambda i: (0, 0)),
                  pl.BlockSpec(wv.shape, lambda i: (0, 0))],
        out_specs=[pl.BlockSpec((tm, C_HEADS * C_QK), lambda i: (i, 0)),
                   pl.BlockSpec((None, C_HEADS, VT_ROWS, tm), lambda i: (i // nt, 0, 0, i % nt))],
        out_shape=[jax.ShapeDtypeStruct((m, C_HEADS * C_QK), BF16),
                   jax.ShapeDtypeStruct((bsz, C_HEADS, VT_ROWS, seq), BF16)],
        compiler_params=_cparams("parallel"),
        name="mla_kv",
    )(ckv, kr, wk, wv)


def _attend_chunks(qT, k_ref, vt_ref, scratch, *, tk, n_full, mask_tail, tail_steps, tail_col,
                   unroll):
    m_sc, acc_sc, sa_sc, sb_sc, pa_sc, pb_sc, ala_sc, alb_sc, mxa_sc, mxb_sc = scratch
    s_bufs = (sa_sc, sb_sc)
    p_bufs = (pa_sc, pb_sc)
    al_bufs = (ala_sc, alb_sc)
    mx_bufs = (mxa_sc, mxb_sc)
    last_chunk = k_ref.shape[0] // tk - 1
    m_sc[...] = jnp.full(m_sc.shape, MASKED, F32)
    acc_sc[...] = jnp.zeros(acc_sc.shape, F32)
    for p_ref, al_ref in zip(p_bufs, al_bufs):
        p_ref[...] = jnp.zeros(p_ref.shape, p_ref.dtype)
        al_ref[...] = jnp.ones(al_ref.shape, F32)

    def rows(c):
        return pl.ds(pl.multiple_of(jnp.clip(c, 0, last_chunk) * tk, tk), tk)

    def scores(c, slot, col=0):
        sT = _dot(k_ref[rows(c), :], qT[:, col:])
        s_bufs[slot][:, col:] = sT
        mx_bufs[slot][:, col:] = jnp.max(sT, axis=0, keepdims=True)

    def flush(c, slot, col=0):
        acc_sc[:, col:] = (al_bufs[slot][:, col:] * acc_sc[:, col:]
                           + _dot(vt_ref[:, rows(c)], p_bufs[slot][:, col:]))

    def softmax(sT, top, slot, col):
        m_old = m_sc[:, col:]
        m_new = jnp.maximum(m_old, top)
        al_bufs[slot][:, col:] = jnp.exp2(m_old - m_new)
        p_bufs[slot][:, col:] = jnp.exp2(sT - m_new).astype(p_bufs[slot].dtype)
        m_sc[:, col:] = m_new

    def step(tau, slot, mask, cols):
        col_flush, col, col_next = cols
        flush(tau - 2, slot, col_flush)
        if col_next is not None:
            scores(tau + 1, 1 - slot, col_next)
        sT = s_bufs[slot][:, col:]
        if mask is None:
            softmax(sT, mx_bufs[slot][:, col:], slot, col)
        else:
            sT = mask(sT, tau, col)
            softmax(sT, jnp.max(sT, axis=0, keepdims=True), slot, col)

    def full_steps(tau0, count):
        for j in range(count):
            step(tau0 + j, j % 2, None, (0, 0, 0))

    scores(0, 0)
    trips = n_full // unroll
    lax.fori_loop(0, trips, lambda u, c: (full_steps(unroll * u, unroll), c)[1], 0)
    done = unroll * trips
    pairs = (n_full - done) // 2
    lax.fori_loop(0, pairs, lambda u, c: (full_steps(done + 2 * u, 2), c)[1], 0)
    tau = done + 2 * pairs
    cols = [tail_col(j) for j in range(tail_steps)]
    for j in range(tail_steps):
        col_flush = cols[j - 2] if j >= 2 else 0
        col_next = cols[j + 1] if j + 1 < tail_steps else None
        step(tau + j, j % 2, mask_tail, (col_flush, cols[j], col_next))
    flush(tau + tail_steps - 2, 0, cols[-2])
    flush(tau + tail_steps - 1, 1, cols[-1])
    acc = acc_sc[...]
    return acc[:V_DIM, :] / acc[V_DIM:V_DIM + 1, :]


def _attend_scratch(tq, tk):
    return [pltpu.VMEM((1, tq), F32), pltpu.VMEM((VT_ROWS, tq), F32),
            pltpu.VMEM((tk, tq), F32), pltpu.VMEM((tk, tq), F32),
            pltpu.VMEM((tk, tq), BF16), pltpu.VMEM((tk, tq), BF16),
            pltpu.VMEM((1, tq), F32), pltpu.VMEM((1, tq), F32),
            pltpu.VMEM((1, tq), F32), pltpu.VMEM((1, tq), F32)]


def _transpose_q(q_ref):
    return q_ref[...].astype(F32).T.astype(BF16)


def _flash_kernel(q_ref, k_ref, vt_ref, o_ref, *scratch, tq, tk):
    i = pl.program_id(2)

    def causal(sT, c, col):
        key = lax.broadcasted_iota(jnp.int32, sT.shape, 0) + c * tk
        qry = lax.broadcasted_iota(jnp.int32, sT.shape, 1) + (i * tq + col)
        return jnp.where(key <= qry, sT, MASKED)

    per_tile = tq // tk
    oT = _attend_chunks(_transpose_q(q_ref), k_ref, vt_ref, scratch, tk=tk, n_full=i * per_tile,
                        mask_tail=causal, tail_steps=per_tile, tail_col=lambda j: j * tk,
                        unroll=8)
    o_ref[...] = oT.T.astype(o_ref.dtype)


def _flash_attention(q, k, vt, heads, qk_w, tq=1024, tk=256):
    bsz, seq, _ = q.shape
    assert tq % (2 * tk) == 0 and seq % tq == 0
    return pl.pallas_call(
        functools.partial(_flash_kernel, tq=tq, tk=tk),
        grid=(bsz, heads, seq // tq),
        in_specs=[pl.BlockSpec((None, tq, qk_w), lambda b, h, i: (b, i, h)),
                  pl.BlockSpec((None, seq, qk_w), lambda b, h, i: (b, 0, h)),
                  pl.BlockSpec((None, None, VT_ROWS, seq), lambda b, h, i: (b, h, 0, 0))],
        out_specs=pl.BlockSpec((None, tq, V_DIM), lambda b, h, i: (b, i, h)),
        out_shape=jax.ShapeDtypeStruct((bsz, seq, heads * V_DIM), BF16),
        scratch_shapes=_attend_scratch(tq, tk),
        compiler_params=_cparams("parallel", "parallel", "arbitrary"),
        name="mla_flash",
    )(q, k, vt)


def _kmean_kernel(k_ref, o_ref):
    k = k_ref[...].astype(F32)
    o_ref[...] = jnp.mean(k.reshape(SUBLANES, MOBA_BLOCK, k.shape[-1]), axis=1)


def _kmean(qk):
    bsz, seq, _ = qk.shape
    rows = SUBLANES * MOBA_BLOCK
    return pl.pallas_call(
        _kmean_kernel,
        grid=(bsz, seq // rows),
        in_specs=[pl.BlockSpec((None, rows, A_W), lambda b, i: (b, i, KA_BLK * LANES // A_W))],
        out_specs=pl.BlockSpec((None, SUBLANES, A_W), lambda b, i: (b, i, 0)),
        out_shape=jax.ShapeDtypeStruct((bsz, seq // MOBA_BLOCK, A_W), F32),
        compiler_params=_cparams("parallel", "parallel"),
        name="moba_kmean",
    )(qk)


def _block_attention(q, k, v, visible=None):
    s = _dot_nt(q, k)
    if visible is not None:
        s = jnp.where(visible, s, MASKED)
    m = jnp.max(s, axis=-1, keepdims=True)
    p = jnp.exp2(s - m).astype(BF16)
    v_ones = jnp.concatenate([v, jnp.ones((v.shape[0], LANES), v.dtype)], axis=-1)
    acc = _dot(p, v_ones)
    den = acc[:, V_DIM:]
    return acc[:, :V_DIM] / den, m + jnp.log2(den)


def _moba_gate_kernel(q_ref, km_ref, ids_ref, cnt_ref, qf_ref):
    t = MOBA_BLOCK
    i = pl.program_id(1)
    nb = km_ref.shape[0]
    blk = lax.broadcasted_iota(jnp.int32, (nb, t), 0)
    neg_inf = jnp.float32(-jnp.inf)
    not_after = (lax.broadcasted_iota(jnp.int32, (t, t), 0)
                 <= lax.broadcasted_iota(jnp.int32, (t, t), 1))
    upper = jnp.where(not_after, 1.0, 0.0).astype(BF16)
    ones = jnp.ones((SUBLANES, t), BF16)
    for hd in range(A_HEADS):
        sl = slice(hd * HEAD_DIM, (hd + 1) * HEAD_DIM)
        q = q_ref[:, sl].astype(F32)
        qf_ref[hd] = q
        qT = q.T.astype(BF16)
        km = km_ref[:, sl]
        km_hi = km.astype(BF16)
        km_lo = (km - km_hi.astype(F32)).astype(BF16)
        g = jnp.where(blk < i, _dot(km_hi, qT) + _dot(km_lo, qT), neg_inf)
        picks, ranks, counts = [], [], []
        for _ in range(MOBA_TOPK):
            mx = jnp.max(g, axis=0, keepdims=True)
            is_max = (g == mx) & (mx > neg_inf)
            first = jnp.min(jnp.where(is_max, blk, nb), axis=0, keepdims=True)
            pick = blk == first
            g = jnp.where(pick, neg_inf, g)
            onehot = jnp.where(pick, 1.0, 0.0).astype(BF16)
            before = _dot(onehot, upper)
            rank = jnp.sum(jnp.where(pick, before - 1.0, 0.0), axis=0, keepdims=True)
            picks.append(first)
            ranks.append(rank.astype(jnp.int32))
            counts.append(_dot_nt(ones, onehot)[0:1, :])
        pad_i = jnp.zeros((SUBLANES - 2 * MOBA_TOPK, t), jnp.int32)
        ids_ref[hd] = jnp.concatenate(picks + ranks + [pad_i], axis=0)
        pad_f = jnp.zeros((SUBLANES - MOBA_TOPK, nb), F32)
        cnt_ref[hd] = jnp.concatenate(counts + [pad_f], axis=0)


def _moba_gate(qk, kmean):
    bsz, seq, _ = qk.shape
    t = MOBA_BLOCK
    nb = seq // t
    return pl.pallas_call(
        _moba_gate_kernel,
        grid=(bsz, nb),
        in_specs=[pl.BlockSpec((None, t, A_W), lambda b, i: (b, i, QA_BLK * LANES // A_W)),
                  pl.BlockSpec((None, nb, A_W), lambda b, i: (b, 0, 0))],
        out_specs=[pl.BlockSpec((None, A_HEADS, SUBLANES, t), lambda b, i: (b, 0, 0, i)),
                   pl.BlockSpec((None, A_HEADS, None, SUBLANES, nb), lambda b, i: (b, 0, i, 0, 0)),
                   pl.BlockSpec((None, A_HEADS, t, HEAD_DIM), lambda b, i: (b, 0, i, 0))],
        out_shape=[jax.ShapeDtypeStruct((bsz, A_HEADS, SUBLANES, seq), jnp.int32),
                   jax.ShapeDtypeStruct((bsz, A_HEADS, nb, SUBLANES, nb), F32),
                   jax.ShapeDtypeStruct((bsz, A_HEADS, seq, HEAD_DIM), F32)],
        compiler_params=_cparams("parallel", "parallel"),
        name="moba_gate",
    )(qk, kmean)


def _moba_routes(ids, cnt, seq):
    bsz, heads = ids.shape[:2]
    bh, t = bsz * heads, MOBA_BLOCK
    nb = seq // t
    tiles = _moba_tiles(seq)
    picks = ids[:, :, 0:MOBA_TOPK, :].reshape(bh, MOBA_TOPK, nb, t)
    ranks = ids[:, :, MOBA_TOPK:2 * MOBA_TOPK, :].reshape(bh, MOBA_TOPK, nb, t)
    per_tile = cnt[:, :, :, 0:MOBA_TOPK, :].astype(jnp.int32).reshape(bh, nb * MOBA_TOPK, nb)
    before = jnp.cumsum(per_tile, axis=1) - per_tile
    total = jnp.sum(per_tile, axis=1)
    padded = -(-total // t) * t
    ends = jnp.cumsum(padded, axis=1)
    base = before + (ends - padded)[:, None, :]
    base = base.reshape(bh, nb, MOBA_TOPK, nb).transpose(0, 2, 1, 3)
    onehot = picks[..., None] == jnp.arange(nb)
    pos = jnp.sum(jnp.where(onehot, base[:, :, :, None, :], 0), axis=-1) + ranks
    pos = jnp.where(picks < nb, pos, (tiles - 1) * t)
    pos = pos + (jnp.arange(bh, dtype=jnp.int32) * (tiles * t))[:, None, None, None]
    pos = pos.reshape(bh, MOBA_TOPK, seq).transpose(1, 0, 2).reshape(MOBA_TOPK, bh * seq)
    tile_start = jnp.arange(tiles, dtype=jnp.int32) * t
    tile_blk = jnp.sum(tile_start[None, :, None] >= ends[:, None, :], axis=-1)
    tile_blk = jnp.where(tile_start[None, :] < ends[:, -1:], tile_blk, -1)
    return pos.astype(jnp.int32), tile_blk.astype(jnp.int32)


def _moba_tiles(seq):
    nb = seq // MOBA_BLOCK
    return -(-(MOBA_TOPK * nb + nb + 1) // GROUP_STEP) * GROUP_STEP


def _sc_mesh():
    return plsc.VectorSubcoreMesh(core_axis_name="core", subcore_axis_name="subcore")


def _sc_scatter_rows(x, idx, rows):
    slots, n = idx.shape
    d = x.shape[1]

    @pl.kernel(out_type=jax.ShapeDtypeStruct((rows, d), x.dtype), mesh=_sc_mesh())
    def scatter(x_hbm, i_hbm, o_hbm):
        def body(x_vmem, i_vmem):
            pltpu.sync_copy(x_vmem, o_hbm.at[i_vmem.at[0]])

        pltpu.emit_pipeline(
            body, grid=(slots, n // SC_WINDOW),
            in_specs=[pl.BlockSpec((SC_WINDOW, d), lambda s, i: (i, 0)),
                      pl.BlockSpec((1, SC_WINDOW), lambda s, i: (s, i))],
            out_specs=[],
            core_axis_name=("core", "subcore"),
            dimension_semantics=(pltpu.PARALLEL, pltpu.PARALLEL),
        )(x_hbm, i_hbm)

    return scatter(x, idx)


def _sc_gather_rows(x, idx):
    n = idx.shape[0]
    d = x.shape[1]

    @pl.kernel(out_type=jax.ShapeDtypeStruct((n, d), x.dtype), mesh=_sc_mesh())
    def gather(x_hbm, i_hbm, o_hbm):
        def body(i_vmem, o_vmem):
            pltpu.sync_copy(x_hbm.at[i_vmem.at[0]], o_vmem)

        pltpu.emit_pipeline(
            body, grid=(n // SC_WINDOW,),
            in_specs=[pl.BlockSpec((1, SC_WINDOW), lambda i: (0, i))],
            out_specs=[pl.BlockSpec((SC_WINDOW, d), lambda i: (i, 0))],
            core_axis_name=("core", "subcore"),
            dimension_semantics=(pltpu.PARALLEL,),
        )(i_hbm, o_hbm)

    return gather(x, idx.reshape(1, n))


def _moba_group_kernel(tb_ref, q_ref, *refs):
    t = MOBA_BLOCK
    k_refs, v_refs = refs[:GROUP_STEP], refs[GROUP_STEP:2 * GROUP_STEP]
    o_ref, lse_ref = refs[-2:]
    g, step = pl.program_id(0), pl.program_id(1)
    first = step * GROUP_STEP

    @pl.when(tb_ref[g, first] < 0)
    def _():
        o_ref[...] = jnp.zeros(o_ref.shape, o_ref.dtype)
        lse_ref[...] = jnp.full(lse_ref.shape, MASKED, lse_ref.dtype)

    @pl.when(tb_ref[g, first] >= 0)
    def _():
        for u in range(GROUP_STEP):
            used = tb_ref[g, first + u] >= 0
            rows = slice(u * t, (u + 1) * t)
            o, lse = _block_attention(q_ref[rows, :].astype(BF16), k_refs[u][...], v_refs[u][...])
            o_ref[rows, :] = jnp.where(used, o, 0.0)
            lse_ref[rows, :] = jnp.where(used, lse, MASKED)


def _moba_group_attention(q_grouped, tile_blk, qk, v, after):
    bh, rows, _ = q_grouped.shape
    t = MOBA_BLOCK
    tiles = rows // t
    heads = A_HEADS

    def block_of(u, first_col):
        return lambda g, s, tb: (g // heads, jnp.maximum(tb[g, s * GROUP_STEP + u], 0),
                                 first_col + g % heads)

    row_tile = pl.BlockSpec((None, GROUP_STEP * t, HEAD_DIM), lambda g, s, tb: (g, s, 0))
    key_value = lambda first_col: [pl.BlockSpec((None, t, HEAD_DIM), block_of(u, first_col))
                                   for u in range(GROUP_STEP)]
    grid_spec = pltpu.PrefetchScalarGridSpec(
        num_scalar_prefetch=1,
        grid=(bh, tiles // GROUP_STEP),
        in_specs=([row_tile] + key_value(KA_BLK) + key_value(0)
                  + [pl.BlockSpec(memory_space=pl.ANY)]),
        out_specs=[row_tile, row_tile],
    )
    return pl.pallas_call(
        _moba_group_kernel,
        grid_spec=grid_spec,
        out_shape=[jax.ShapeDtypeStruct(q_grouped.shape, F32)] * 2,
        compiler_params=_cparams("parallel", "parallel"),
        name="moba_group",
    )(tile_blk, q_grouped, *([qk] * GROUP_STEP), *([v] * GROUP_STEP), after)


def _moba_merge_kernel(q_ref, k_ref, v_ref, po_ref, pl_ref, o_ref):
    t = MOBA_BLOCK
    causal = (lax.broadcasted_iota(jnp.int32, (t, t), 1)
              <= lax.broadcasted_iota(jnp.int32, (t, t), 0))
    for hd in range(A_HEADS):
        sl = slice(hd * HEAD_DIM, (hd + 1) * HEAD_DIM)
        o_own, lse_own = _block_attention(q_ref[:, sl], k_ref[:, sl], v_ref[:, sl], causal)
        outs = [o_own] + [po_ref[s, hd] for s in range(MOBA_TOPK)]
        lses = [lse_own] + [pl_ref[s, hd] for s in range(MOBA_TOPK)]
        top = functools.reduce(jnp.maximum, lses)
        weights = [jnp.exp2(l - top) for l in lses]
        num = sum(w * o for w, o in zip(weights, outs))
        o_ref[:, sl] = (num / sum(weights)).astype(o_ref.dtype)


def _moba_merge(qk, v, part_o, part_lse):
    bsz, seq, _ = qk.shape
    t = MOBA_BLOCK
    part = pl.BlockSpec((MOBA_TOPK, None, A_HEADS, t, HEAD_DIM), lambda b, i: (0, b, 0, i, 0))
    return pl.pallas_call(
        _moba_merge_kernel,
        grid=(bsz, seq // t),
        in_specs=[pl.BlockSpec((None, t, A_W), lambda b, i: (b, i, QA_BLK * LANES // A_W)),
                  pl.BlockSpec((None, t, A_W), lambda b, i: (b, i, KA_BLK * LANES // A_W)),
                  pl.BlockSpec((None, t, A_W), lambda b, i: (b, i, 0)),
                  part, part],
        out_specs=pl.BlockSpec((None, t, A_W), lambda b, i: (b, i, 0)),
        out_shape=jax.ShapeDtypeStruct((bsz, seq, A_W), BF16),
        compiler_params=_cparams("parallel", "parallel"),
        name="moba_merge",
    )(qk, qk, v, part_o, part_lse)


def _moba_regroup(qk, kmean):
    bsz, seq, _ = qk.shape
    bh = bsz * A_HEADS
    rows = _moba_tiles(seq) * MOBA_BLOCK
    ids, cnt, q_f32 = _moba_gate(qk, kmean)
    pos, tile_blk = _moba_routes(ids, cnt, seq)
    q_grouped = _sc_scatter_rows(q_f32.reshape(bh * seq, HEAD_DIM), pos, bh * rows)
    return q_grouped.reshape(bh, rows, HEAD_DIM), pos, tile_blk


def _moba_picked_blocks(q_grouped, pos, tile_blk, qk, v, after):
    bsz, seq, _ = qk.shape
    bh, rows, _ = q_grouped.shape
    o_g, lse_g = _moba_group_attention(q_grouped, tile_blk, qk, v, after)
    flat = pos.reshape(-1)
    back = lambda a: _sc_gather_rows(a.reshape(bh * rows, HEAD_DIM), flat).reshape(
        MOBA_TOPK, bsz, A_HEADS, seq, HEAD_DIM)
    return back(o_g), back(lse_g)


def _proj_dilated_kernel(h_ref, w_ref, c_ref, s_ref, q_ref, k_ref, v_ref, sc, *, d):
    acc = _dot(h_ref[...], w_ref[...])
    c, s = c_ref[...], s_ref[...]
    q_scale = HEAD_DIM ** -0.5 * LOG2E
    for j in range(acc.shape[1] // LANES):
        blk = acc[:, j * LANES:(j + 1) * LANES]
        if j < B_HEADS:
            blk = _rope128(blk, c, s) * q_scale
        elif j < 2 * B_HEADS:
            blk = _rope128(blk, c, s)
        sc[j] = blk
    rows = acc.shape[0] // d
    for r in range(d):
        for j in range(acc.shape[1] // LANES):
            dst = (q_ref, k_ref, v_ref)[j // B_HEADS]
            col = (j % B_HEADS) * LANES
            dst[r, :, col:col + LANES] = sc[j, pl.ds(r, rows, stride=d), :].astype(dst.dtype)


def _proj_dilated(h, w, rope_h, d, bsz, seq, tm=512):
    m, k = h.shape
    nt = seq // tm
    pos = pl.BlockSpec((tm, LANES), lambda i: (i % nt, 0))
    out = pl.BlockSpec((None, d, tm // d, B_W), lambda i: (i // nt, 0, i % nt, 0))
    return pl.pallas_call(
        functools.partial(_proj_dilated_kernel, d=d),
        grid=(m // tm,),
        in_specs=[pl.BlockSpec((tm, k), lambda i: (i, 0)), pl.BlockSpec(w.shape, lambda i: (0, 0)),
                  pos, pos],
        out_specs=[out] * 3,
        out_shape=[jax.ShapeDtypeStruct((bsz, d, seq // d, B_W), BF16)] * 3,
        scratch_shapes=[pltpu.VMEM((w.shape[1] // LANES, tm, LANES), F32)],
        compiler_params=_cparams("parallel"),
        name=f"proj_dilated_d{d}",
    )(h, w, *rope_h)


def _dilated_kernel(q_ref, kc_ref, kp_ref, vc_ref, vp_ref, o_ref, lse_ref, *, span):
    t, tp = q_ref.shape[0], kp_ref.shape[0]
    i = pl.program_id(2)
    shape = (2 * tp, tp)
    key_row = lax.broadcasted_iota(jnp.int32, shape, 0)
    dist = lax.broadcasted_iota(jnp.int32, shape, 1) + tp - key_row
    visible = (dist >= 0) & (dist <= span)
    bias = jnp.where(visible, 0.0, MASKED)
    bias_first = jnp.where(visible & ((key_row >= tp) | (i > 0)), 0.0, MASKED)
    ones = jnp.ones((BF16_ROWS, tp + t), BF16)

    def transposed(x):
        return x.astype(F32).T.astype(BF16)

    for j in range(B_HEADS):
        sl = slice(j * LANES, (j + 1) * LANES)
        qT = transposed(q_ref[:, sl])
        k_all = jnp.concatenate([kp_ref[:, sl], kc_ref[:, sl]], axis=0)
        vt_all = jnp.concatenate([transposed(vp_ref[:, sl]), transposed(vc_ref[:, sl])], axis=1)
        vt_all = jnp.concatenate([vt_all, ones], axis=0)
        outs, lses = [], []
        for u in range(t // tp):
            window = slice(u * tp, (u + 2) * tp)
            s = _dot(k_all[window, :], qT[:, u * tp:(u + 1) * tp])
            s = s + (bias_first if u == 0 else bias)
            m = jnp.max(s, axis=0, keepdims=True)
            p = jnp.exp2(s - m).astype(BF16)
            acc = _dot(vt_all[:, window], p)
            den = acc[V_DIM:V_DIM + 1, :]
            outs.append(acc[:V_DIM, :] / den)
            lses.append(m + jnp.log2(den))
        o_ref[:, sl] = jnp.concatenate(outs, axis=1).T
        lse = jnp.concatenate(lses, axis=1)
        lse_ref[:, sl] = jnp.broadcast_to(lse, (LANES, t)).T


def _dilated_attention(q, k, v, span, t=1024):
    bsz, d, length, _ = q.shape
    t = min(t, length)
    tp = B_QBLOCK
    assert span <= tp and t % tp == 0
    cur = pl.BlockSpec((None, None, t, B_W), lambda b, r, i: (b, r, i, 0))
    prev = pl.BlockSpec((None, None, tp, B_W),
                        lambda b, r, i: (b, r, jnp.maximum(i * (t // tp) - 1, 0), 0))
    return pl.pallas_call(
        functools.partial(_dilated_kernel, span=span),
        grid=(bsz, d, length // t),
        in_specs=[cur, cur, prev, cur, prev],
        out_specs=[cur, cur],
        out_shape=[jax.ShapeDtypeStruct(q.shape, F32)] * 2,
        compiler_params=_cparams("parallel", "parallel", "parallel"),
        name=f"dilated_d{d}",
    )(q, k, k, v, v)


def _natural_rows(ref, sc):
    d, rows = ref.shape[0], ref.shape[1]
    if d == 1:
        return ref[0]
    for r in range(d):
        for j in range(B_HEADS):
            sc[j, pl.ds(r, rows, stride=d), :] = ref[r, :, j * LANES:(j + 1) * LANES]
    return jnp.concatenate([sc[j] for j in range(B_HEADS)], axis=-1)


def _mixer_tail_kernel(x_ref, oa_ref, o0_ref, o1_ref, o2_ref, l0_ref, l1_ref, l2_ref, oc_ref,
                       g_ref, wpa_ref, wpb_ref, wpc_ref, wo_ref, y_ref, *scratch):
    o0, o1, o2, l0, l1, l2 = [
        _natural_rows(ref, sc)
        for ref, sc in zip((o0_ref, o1_ref, o2_ref, l0_ref, l1_ref, l2_ref), scratch)]
    mx = jnp.maximum(jnp.maximum(l0, l1), l2)
    e0, e1, e2 = jnp.exp2(l0 - mx), jnp.exp2(l1 - mx), jnp.exp2(l2 - mx)
    ob = (e0 * o0 + e1 * o1 + e2 * o2) / (e0 + e1 + e2)
    pa = _dot(oa_ref[...], wpa_ref[...])
    pb = _dot(ob.astype(BF16), wpb_ref[...])
    pc = _dot(oc_ref[...], wpc_ref[...])
    d = D_MODEL
    merged = (g_ref[:, 0:d].astype(F32) * pa + g_ref[:, d:2 * d].astype(F32) * pb
              + g_ref[:, 2 * d:3 * d].astype(F32) * pc)
    y_ref[...] = x_ref[...] + _dot(merged.astype(BF16), wo_ref[...])


def _mixer_tail(x, out_a, o_groups, lse_groups, out_c, gates, w_pa, w_pb, w_pc, w_o, seq, tm=256):
    m, d = x.shape
    nt = seq // tm
    row = lambda width: pl.BlockSpec((tm, width), lambda i: (i, 0))
    residue = lambda g: pl.BlockSpec((None, g.shape[1], tm // g.shape[1], B_W),
                                     lambda i: (i // nt, 0, i % nt, 0))
    weights = [_resident(w) for w in (w_pa, w_pb, w_pc, w_o)]
    groups = list(o_groups) + list(lse_groups)
    return pl.pallas_call(
        _mixer_tail_kernel,
        grid=(m // tm,),
        in_specs=([row(d), row(A_W)] + [residue(g) for g in groups]
                  + [row(C_W), row(3 * d)] + weights),
        out_specs=row(d),
        out_shape=jax.ShapeDtypeStruct((m, d), F32),
        scratch_shapes=[pltpu.VMEM((B_HEADS, tm, LANES), F32) for _ in groups],
        compiler_params=_cparams("parallel"),
        name="mixer_tail",
    )(x, out_a, *groups, out_c, gates, w_pa, w_pb, w_pc, w_o)


def _mem_kv_kernel(mem_ref, g_ref, wk_ref, wv_ref, k_ref, v_ref):
    memn = _rms(mem_ref[...], g_ref[...]).astype(BF16)
    k_ref[...] = _dot(memn, wk_ref[...]).astype(k_ref.dtype)
    v_ref[...] = _dot(memn, wv_ref[...]).astype(v_ref.dtype)


def _mem_kv(mem, g, wk, wv):
    bsz, n, d = mem.shape
    out = pl.BlockSpec((None, n, X_W), lambda b: (b, 0, 0))
    return pl.pallas_call(
        _mem_kv_kernel,
        grid=(bsz,),
        in_specs=[pl.BlockSpec((None, n, d), lambda b: (b, 0, 0)),
                  pl.BlockSpec((1, d), lambda b: (0, 0)),
                  pl.BlockSpec(wk.shape, lambda b: (0, 0)),
                  pl.BlockSpec(wv.shape, lambda b: (0, 0))],
        out_specs=[out, out],
        out_shape=[jax.ShapeDtypeStruct((bsz, n, X_W), BF16)] * 2,
        compiler_params=_cparams("parallel"),
        name="mem_kv",
    )(mem, g.reshape(1, d), wk, wv)


def _mem_attn_kernel(x_ref, g_ref, wq_ref, k_ref, v_ref, wo_ref, y_ref):
    x = x_ref[...]
    h = _rms(x, g_ref[...]).astype(BF16)
    q = (_dot(h, wq_ref[...]) * HEAD_DIM ** -0.5).astype(BF16)
    heads = []
    for hd in range(X_HEADS):
        sl = slice(hd * HEAD_DIM, (hd + 1) * HEAD_DIM)
        s = _dot_nt(q[:, sl], k_ref[:, sl])
        p = jnp.exp(s - jnp.max(s, axis=-1, keepdims=True))
        o = _dot(p.astype(BF16), v_ref[:, sl]) / jnp.sum(p, axis=-1, keepdims=True)
        heads.append(o.astype(BF16))
    y_ref[...] = x + _dot(jnp.concatenate(heads, axis=-1), wo_ref[...])


def _mem_attention(x, g, wq, kmem, vmem, wo, seq, tm=512):
    m, d = x.shape
    nt = seq // tm
    n = kmem.shape[1]
    kv = pl.BlockSpec((None, n, X_W), lambda i: (i // nt, 0, 0))
    return pl.pallas_call(
        _mem_attn_kernel,
        grid=(m // tm,),
        in_specs=[pl.BlockSpec((tm, d), lambda i: (i, 0)),
                  pl.BlockSpec((1, d), lambda i: (0, 0)),
                  pl.BlockSpec(wq.shape, lambda i: (0, 0)), kv, kv,
                  pl.BlockSpec(wo.shape, lambda i: (0, 0))],
        out_specs=pl.BlockSpec((tm, d), lambda i: (i, 0)),
        out_shape=jax.ShapeDtypeStruct((m, d), F32),
        compiler_params=_cparams("parallel"),
        name="mem_attention",
    )(x, g.reshape(1, d), wq, kmem, vmem, wo)


def _ffn_up_kernel(x_ref, halo_ref, g_ref, wg_ref, wv_ref, cwg_ref, cwv_ref, cbg_ref, cbv_ref,
                   act_ref, h_sc, *, tiles_per_seq):
    i = pl.program_id(0)
    tm = x_ref.shape[0]

    @pl.when(pl.program_id(1) == 0)
    def _():
        g = g_ref[...]
        keep = (i % tiles_per_seq != 0).astype(F32)
        h_sc[0:HALO, :] = (_rms(halo_ref[...], g) * keep).astype(h_sc.dtype)
        h_sc[HALO:, :] = _rms(x_ref[...], g).astype(h_sc.dtype)

    h = h_sc[...]

    def conv(w_ref, cw_ref, cb_ref):
        u = _dot(h, w_ref[...])
        c = cb_ref[...]
        for tap in range(CONV_W):
            lo = HALO - (CONV_W - 1) + tap
            c = c + cw_ref[tap:tap + 1, :] * u[lo:lo + tm, :]
        return c

    act = jax.nn.silu(conv(wg_ref, cwg_ref, cbg_ref)) * conv(wv_ref, cwv_ref, cbv_ref)
    act_ref[...] = act.astype(act_ref.dtype)


def _ffn_down_kernel(a_ref, w_ref, x_ref, y_ref):
    y_ref[...] = x_ref[...] + _dot(a_ref[...], w_ref[...])


def _conv_ffn(x, g, w_up, conv_w, conv_b, w_down, layer, seq):
    m, d = x.shape
    act = _ffn_up(x, g, w_up, conv_w, conv_b, layer, seq)
    tm, tn = 1024, FFN_TF
    return pl.pallas_call(
        _ffn_down_kernel,
        grid=(m // tm, d // tn),
        in_specs=[pl.BlockSpec((tm, D_FF), lambda i, j: (i, 0)),
                  pl.BlockSpec((None, D_FF, tn), lambda i, j: (layer, 0, j)),
                  pl.BlockSpec((tm, tn), lambda i, j: (i, j))],
        out_specs=pl.BlockSpec((tm, tn), lambda i, j: (i, j)),
        out_shape=jax.ShapeDtypeStruct((m, d), F32),
        compiler_params=_cparams("parallel", "parallel"),
        name="ffn_down",
    )(act, w_down, x)


def _ffn_up(x, g, w_up, conv_w, conv_b, layer, seq, tm=1024, tf=FFN_TF):
    m, d = x.shape
    nf = D_FF_PAD // tf
    halo_blocks = tm // HALO
    return pl.pallas_call(
        functools.partial(_ffn_up_kernel, tiles_per_seq=seq // tm),
        grid=(m // tm, nf),
        in_specs=[pl.BlockSpec((tm, d), lambda i, f: (i, 0)),
                  pl.BlockSpec((HALO, d), lambda i, f: (jnp.maximum(i * halo_blocks - 1, 0), 0)),
                  pl.BlockSpec((1, d), lambda i, f: (0, 0)),
                  pl.BlockSpec((None, d, tf), lambda i, f: (layer, 0, f)),
                  pl.BlockSpec((None, d, tf), lambda i, f: (layer, 0, f + nf)),
                  pl.BlockSpec((CONV_W, tf), lambda i, f: (0, f)),
                  pl.BlockSpec((CONV_W, tf), lambda i, f: (0, f + nf)),
                  pl.BlockSpec((1, tf), lambda i, f: (0, f)),
                  pl.BlockSpec((1, tf), lambda i, f: (0, f + nf))],
        out_specs=pl.BlockSpec((tm, tf), lambda i, f: (i, f)),
        out_shape=jax.ShapeDtypeStruct((m, D_FF_PAD), BF16),
        scratch_shapes=[pltpu.VMEM((HALO + tm, d), BF16)],
        compiler_params=_cparams("parallel", "arbitrary"),
        name="ffn_up",
    )(x, x, g.reshape(1, d), w_up, w_up, conv_w, conv_w, conv_b, conv_b)


def _rope_tables(seq):
    def angles(dim):
        inv_freq = jnp.exp(jnp.arange(0, dim, 2, dtype=F32) * (-math.log(ROPE_THETA) / dim))
        ang = jnp.arange(seq, dtype=F32)[:, None] * inv_freq[None, :]
        return jnp.cos(ang), jnp.sin(ang)

    cos_h, sin_h = angles(HEAD_DIM)
    rope_h = (jnp.concatenate([cos_h, cos_h], axis=-1), jnp.concatenate([-sin_h, sin_h], axis=-1))
    cos_r, sin_r = angles(ROPE_DIM)
    z = jnp.zeros_like(cos_r)
    rope_r = (jnp.concatenate([cos_r, cos_r, z, z], axis=-1),
              jnp.concatenate([-sin_r, z, z, z], axis=-1),
              jnp.concatenate([z, sin_r, z, z], axis=-1))
    return rope_h, rope_r


def _split_in(w_in):
    return [w_in[:, IN_OFFSETS[k]:IN_OFFSETS[k + 1]] for k in range(len(IN_WIDTHS))]


def _pad_cols(w, width):
    return jnp.pad(w, ((0, 0), (0, width - w.shape[1])))


def _layer_params(w_in, w_uq, w_ukv, conv_w, conv_b):
    qa, ka, va, qb, kb, vb, cq, ckv, kr, gates = _split_in(w_in)
    w_qk = jnp.concatenate([qa, ka], axis=1).astype(BF16)
    group_cols = lambda w, g: w[:, g * B_W:(g + 1) * B_W]
    w_b = [jnp.concatenate([group_cols(qb, g), group_cols(kb, g), group_cols(vb, g)],
                           axis=1).astype(BF16) for g in range(len(B_GROUPS))]
    w_down_in = jnp.concatenate([cq, ckv, _pad_cols(kr, LANES)], axis=1).astype(BF16)
    uq = w_uq.reshape(Q_LORA, C_HEADS, NOPE_DIM + ROPE_DIM)
    uq = jnp.pad(uq, ((0, 0), (0, 0), (0, C_QK - NOPE_DIM - ROPE_DIM)))
    ukv = w_ukv.reshape(KV_LORA, C_HEADS, NOPE_DIM + V_DIM)
    return dict(
        w_qk=w_qk, w_va=va.astype(BF16), w_b=w_b, w_gates=gates.astype(BF16),
        w_down_in=w_down_in,
        w_uq=uq.reshape(Q_LORA, C_HEADS * C_QK).astype(BF16),
        w_uk=ukv[:, :, :NOPE_DIM].reshape(KV_LORA, C_HEADS * NOPE_DIM).astype(BF16),
        w_uv=ukv[:, :, NOPE_DIM:].reshape(KV_LORA, C_W).astype(BF16),
        conv_w=_pad_ff_halves(conv_w),
        conv_b=_pad_ff_halves(conv_b.reshape(1, -1)),
    )


def _pad_ff_halves(w):
    pad = [(0, 0)] * (w.ndim - 1) + [(0, D_FF_PAD - D_FF)]
    return jnp.concatenate([jnp.pad(w[..., :D_FF], pad), jnp.pad(w[..., D_FF:], pad)], axis=-1)


def _qk_col_scale():
    q_scale = HEAD_DIM ** -0.5
    parts = [jnp.full((A_W,), q_scale * LOG2E, F32), jnp.ones((A_W,), F32)]
    return jnp.concatenate(parts).reshape(1, QK_W)


def _mixer(x, g_mix, p, g_cq, g_ckv, w_pa, w_pb, w_pc, w_o, rope_h, rope_r, bsz, seq):
    m = x.shape[0]
    h = _rmsnorm(x, g_mix, BF16)
    qk = _matmul(h, p["w_qk"], _mm_rope_kernel, BF16, 1024, 1024, seq=seq,
                 extras=(("col", _qk_col_scale()), ("pos", rope_h[0]), ("pos", rope_h[1])),
                 name="proj_qk_rope")
    qk3 = qk.reshape(bsz, seq, QK_W)
    q_grouped, pos, tile_blk = _moba_regroup(qk3, _kmean(qk3))
    v_a = _matmul(h, p["w_va"], _mm_plain_kernel, BF16, 1024, A_W, name="proj_va")
    gates = _matmul(h, p["w_gates"], _mm_sigmoid_kernel, BF16, 1024, 1024, name="proj_gates")
    v_a3 = v_a.reshape(bsz, seq, A_W)
    part_o, part_lse = _moba_picked_blocks(q_grouped, pos, tile_blk, qk3, v_a3, after=gates)
    cq, ckv, kr = _mla_down(h, p["w_down_in"], g_cq, g_ckv, rope_r, seq)
    q_c = _mla_q(cq, p["w_uq"], rope_r, seq)
    k_c, vt_c = _mla_kv(ckv, kr, p["w_uk"], p["w_uv"], bsz, seq)
    groups = []
    for (window, d), w_g in zip(B_GROUPS, p["w_b"]):
        q_g, k_g, v_g = _proj_dilated(h, w_g, rope_h, d, bsz, seq)
        groups.append(_dilated_attention(q_g, k_g, v_g, window // d))
    out_c = _flash_attention(q_c.reshape(bsz, seq, -1), k_c.reshape(bsz, seq, -1), vt_c,
                             C_HEADS, C_QK).reshape(m, C_W)
    out_a = _moba_merge(qk3, v_a3, part_o, part_lse).reshape(m, A_W)
    return _mixer_tail(x, out_a, [g[0] for g in groups], [g[1] for g in groups], out_c, gates,
                       w_pa.astype(BF16), w_pb.astype(BF16), w_pc.astype(BF16), w_o.astype(BF16),
                       seq)


def kernel(x, mem, g_mix, w_in, g_cq, g_ckv, w_uq, w_ukv, w_pa, w_pb, w_pc, w_o, g_mem, g_memkv,
           w_xq, w_xk, w_xv, w_xo, g_ffn, w_up, conv_w, conv_b, w_down, g_final):
    bsz, seq, d = x.shape
    rope_h, rope_r = _rope_tables(seq)
    xf = x.reshape(bsz * seq, d)
    w_down = w_down.astype(BF16)
    w_up = _pad_ff_halves(w_up).astype(BF16)
    for l in range(DEPTH):
        p = _layer_params(w_in[l], w_uq[l], w_ukv[l], conv_w[l], conv_b[l])
        xf = _mixer(xf, g_mix[l], p, g_cq[l], g_ckv[l], w_pa[l], w_pb[l], w_pc[l], w_o[l],
                    rope_h, rope_r, bsz, seq)
        kmem, vmem = _mem_kv(mem, g_memkv[l], w_xk[l].astype(BF16), w_xv[l].astype(BF16))
        xf = _mem_attention(xf, g_mem[l], w_xq[l].astype(BF16), kmem, vmem,
                            w_xo[l].astype(BF16), seq)
        xf = _conv_ffn(xf, g_ffn[l], w_up, p["conv_w"], p["conv_b"], w_down, l, seq)
    return _rmsnorm(xf, g_final, F32).reshape(bsz, seq, d)
```

```python
import functools
import math

import jax
import jax.numpy as jnp
import numpy as np
from jax import lax
from jax.experimental import pallas as pl
from jax.experimental.pallas import tpu as pltpu
from jax.experimental.pallas import tpu_sc as plsc

F32 = jnp.float32
BF16 = jnp.bfloat16

LANES = 128
SUBLANES = 8
V7X_VMEM_BYTES = 64 * 1024 * 1024
VMEM_LIMIT = V7X_VMEM_BYTES * 7 // 8

D_MODEL = 2048
DEPTH = 2
HEAD_DIM = 128
ROPE_THETA = 10000.0
EPS = 1e-6

A_HEADS = 4
MOBA_BLOCK = 256
MOBA_TOPK = 3

B_GROUPS = ((128, 1), (512, 4), (2048, 16))
B_HEADS = 4
B_QBLOCK = 128

C_HEADS = 8
Q_LORA = 1536
KV_LORA = 512
NOPE_DIM = 128
ROPE_DIM = 64
V_DIM = 128

X_HEADS = 4
D_FF = 5504
CONV_W = 3

A_W = A_HEADS * HEAD_DIM
B_QKV_W = len(B_GROUPS) * B_HEADS * HEAD_DIM
B_W = B_HEADS * HEAD_DIM
C_W = C_HEADS * V_DIM
X_W = X_HEADS * HEAD_DIM
IN_WIDTHS = (A_W, A_W, A_W, B_QKV_W, B_QKV_W, B_QKV_W, Q_LORA, KV_LORA, ROPE_DIM, 3 * D_MODEL)
IN_OFFSETS = tuple(int(o) for o in np.cumsum((0,) + IN_WIDTHS))

QK_W = 2 * A_W
QA_BLK, KA_BLK = 0, A_W // LANES

C_QK = 2 * LANES
MASKED = -1e30
LOG2E = math.log2(math.e)
BF16_ROWS = 16
VT_ROWS = V_DIM + BF16_ROWS
GROUP_STEP = 16
SC_WINDOW = 128

FFN_TF = 512
D_FF_PAD = -(-D_FF // FFN_TF) * FFN_TF
HALO = SUBLANES


def _cparams(*sem):
    return pltpu.CompilerParams(dimension_semantics=sem, vmem_limit_bytes=VMEM_LIMIT)


def _resident(arr):
    zeros = (0,) * arr.ndim
    return pl.BlockSpec(arr.shape, lambda *_: zeros, pipeline_mode=pl.Buffered(1))


def _dot(a, b):
    return jnp.dot(a, b, preferred_element_type=F32)


def _dot_nt(a, b):
    return lax.dot_general(a, b, (((1,), (1,)), ((), ())), preferred_element_type=F32)


def _rms(x, g):
    return x * lax.rsqrt(jnp.mean(x * x, axis=-1, keepdims=True) + EPS) * g


def _rmsnorm_kernel(x_ref, g_ref, o_ref):
    o_ref[...] = _rms(x_ref[...], g_ref[...]).astype(o_ref.dtype)


def _rmsnorm(x, g, out_dtype, tm=512):
    m, d = x.shape
    return pl.pallas_call(
        _rmsnorm_kernel,
        grid=(m // tm,),
        in_specs=[pl.BlockSpec((tm, d), lambda i: (i, 0)),
                  pl.BlockSpec((1, d), lambda i: (0, 0))],
        out_specs=pl.BlockSpec((tm, d), lambda i: (i, 0)),
        out_shape=jax.ShapeDtypeStruct((m, d), out_dtype),
        compiler_params=_cparams("parallel"),
        name="rmsnorm",
    )(x, g.reshape(1, d))


def _rope128(x, c, s):
    return x * c + pltpu.roll(x, HEAD_DIM // 2, 1) * s


def _rope64(x, c, sa, sb):
    half = ROPE_DIM // 2
    return x * c + pltpu.roll(x, LANES - half, 1) * sa + pltpu.roll(x, half, 1) * sb


def _mm_plain_kernel(a_ref, w_ref, o_ref):
    o_ref[...] = _dot(a_ref[...], w_ref[...]).astype(o_ref.dtype)


def _mm_sigmoid_kernel(a_ref, w_ref, o_ref):
    o_ref[...] = jax.nn.sigmoid(_dot(a_ref[...], w_ref[...])).astype(o_ref.dtype)


def _mm_rope_kernel(a_ref, w_ref, cs_ref, c_ref, s_ref, o_ref):
    acc = _dot(a_ref[...], w_ref[...])
    c = c_ref[...]
    s = s_ref[...]
    for j in range(acc.shape[1] // LANES):
        sl = slice(j * LANES, (j + 1) * LANES)
        o_ref[:, sl] = (_rope128(acc[:, sl], c, s) * cs_ref[:, sl]).astype(o_ref.dtype)


def _matmul(a, w, kernel, out_dtype, tm, tn, seq=None, extras=(), name="matmul"):
    m, k = a.shape
    n = w.shape[1]
    in_specs = [pl.BlockSpec((tm, k), lambda i, j: (i, 0)),
                pl.BlockSpec((k, tn), lambda i, j: (0, j))]
    args = [a, w]
    for kind, arr in extras:
        if kind == "col":
            in_specs.append(pl.BlockSpec((1, tn), lambda i, j: (0, j)))
        else:
            nt = seq // tm
            in_specs.append(pl.BlockSpec((tm, LANES), lambda i, j: (i % nt, 0)))
        args.append(arr)
    return pl.pallas_call(
        kernel,
        grid=(m // tm, n // tn),
        in_specs=in_specs,
        out_specs=pl.BlockSpec((tm, tn), lambda i, j: (i, j)),
        out_shape=jax.ShapeDtypeStruct((m, n), out_dtype),
        compiler_params=_cparams("parallel", "parallel"),
        name=name,
    )(*args)


def _mla_down_kernel(h_ref, w_ref, gq_ref, gkv_ref, c_ref, sa_ref, sb_ref,
                     cq_ref, ckv_ref, kr_ref):
    acc = _dot(h_ref[...], w_ref[...])
    cq_ref[...] = _rms(acc[:, :Q_LORA], gq_ref[...]).astype(cq_ref.dtype)
    ckv_ref[...] = _rms(acc[:, Q_LORA:Q_LORA + KV_LORA], gkv_ref[...]).astype(ckv_ref.dtype)
    kr = acc[:, Q_LORA + KV_LORA:]
    kr_ref[...] = _rope64(kr, c_ref[...], sa_ref[...], sb_ref[...]).astype(kr_ref.dtype)


def _mla_down(h, w, g_cq, g_ckv, rope_r, seq, tm=512):
    m, k = h.shape
    n = w.shape[1]
    nt = seq // tm
    row = lambda width: pl.BlockSpec((tm, width), lambda i: (i, 0))
    full = lambda r, c: pl.BlockSpec((r, c), lambda i: (0, 0))
    pos = pl.BlockSpec((tm, LANES), lambda i: (i % nt, 0))
    return pl.pallas_call(
        _mla_down_kernel,
        grid=(m // tm,),
        in_specs=[row(k), full(k, n), full(1, Q_LORA), full(1, KV_LORA), pos, pos, pos],
        out_specs=[row(Q_LORA), row(KV_LORA), row(LANES)],
        out_shape=[jax.ShapeDtypeStruct((m, Q_LORA), BF16),
                   jax.ShapeDtypeStruct((m, KV_LORA), BF16),
                   jax.ShapeDtypeStruct((m, LANES), BF16)],
        compiler_params=_cparams("parallel"),
        name="mla_down",
    )(h, w, g_cq.reshape(1, -1), g_ckv.reshape(1, -1), *rope_r)


def _mla_q_kernel(cq_ref, w_ref, c_ref, sa_ref, sb_ref, q_ref, *, scale):
    acc = _dot(cq_ref[...], w_ref[...])
    c, sa, sb = c_ref[...], sa_ref[...], sb_ref[...]
    for hd in range(C_HEADS):
        lo = hd * C_QK
        q_ref[:, lo:lo + LANES] = (acc[:, lo:lo + LANES] * scale).astype(q_ref.dtype)
        rope = _rope64(acc[:, lo + LANES:lo + C_QK], c, sa, sb)
        q_ref[:, lo + LANES:lo + C_QK] = (rope * scale).astype(q_ref.dtype)


def _mla_q(cq, w, rope_r, seq, tm=512):
    m, k = cq.shape
    n = w.shape[1]
    nt = seq // tm
    pos = pl.BlockSpec((tm, LANES), lambda i: (i % nt, 0))
    return pl.pallas_call(
        functools.partial(_mla_q_kernel, scale=(NOPE_DIM + ROPE_DIM) ** -0.5 * LOG2E),
        grid=(m // tm,),
        in_specs=[pl.BlockSpec((tm, k), lambda i: (i, 0)),
                  pl.BlockSpec((k, n), lambda i: (0, 0)), pos, pos, pos],
        out_specs=pl.BlockSpec((tm, n), lambda i: (i, 0)),
        out_shape=jax.ShapeDtypeStruct((m, n), BF16),
        compiler_params=_cparams("parallel"),
        name="mla_q",
    )(cq, w, *rope_r)


def _store_vt(v, vt_ref):
    vt = v.T
    for hd in range(vt_ref.shape[0]):
        vt_ref[hd, 0:V_DIM, :] = vt[hd * V_DIM:(hd + 1) * V_DIM, :].astype(vt_ref.dtype)
        vt_ref[hd, V_DIM:VT_ROWS, :] = jnp.ones((VT_ROWS - V_DIM, vt.shape[1]), vt_ref.dtype)


def _mla_kv_kernel(ckv_ref, kr_ref, wk_ref, wv_ref, k_ref, vt_ref):
    ckv = ckv_ref[...]
    kn = _dot(ckv, wk_ref[...])
    kr = kr_ref[...]
    for hd in range(C_HEADS):
        lo = hd * C_QK
        k_ref[:, lo:lo + LANES] = kn[:, hd * LANES:(hd + 1) * LANES].astype(k_ref.dtype)
        k_ref[:, lo + LANES:lo + C_QK] = kr
    _store_vt(_dot(ckv, wv_ref[...]), vt_ref)


def _mla_kv(ckv, kr, wk, wv, bsz, seq, tm=512):
    m, k = ckv.shape
    nt = seq // tm
    return pl.pallas_call(
        _mla_kv_kernel,
        grid=(m // tm,),
        in_specs=[pl.BlockSpec((tm, k), lambda i: (i, 0)),
                  pl.BlockSpec((tm, LANES), lambda i: (i, 0)),
                  pl.BlockSpec(wk.shape, lambda i: (0, 0)),
                  pl.BlockSpec(wv.shape, lambda i: (0, 0))],
        out_specs=[pl.BlockSpec((tm, C_HEADS * C_QK), lambda i: (i, 0)),
                   pl.BlockSpec((None, C_HEADS, VT_ROWS, tm), lambda i: (i // nt, 0, 0, i % nt))],
        out_shape=[jax.ShapeDtypeStruct((m, C_HEADS * C_QK), BF16),
                   jax.ShapeDtypeStruct((bsz, C_HEADS, VT_ROWS, seq), BF16)],
        compiler_params=_cparams("parallel"),
        name="mla_kv",
    )(ckv, kr, wk, wv)


def _attend_chunks(qT, k_ref, vt_ref, scratch, *, tk, n_full, mask_tail, tail_steps, tail_col,
                   unroll):
    m_sc, acc_sc, sa_sc, sb_sc, pa_sc, pb_sc, ala_sc, alb_sc, mxa_sc, mxb_sc = scratch
    s_bufs = (sa_sc, sb_sc)
    p_bufs = (pa_sc, pb_sc)
    al_bufs = (ala_sc, alb_sc)
    mx_bufs = (mxa_sc, mxb_sc)
    last_chunk = k_ref.shape[0] // tk - 1
    m_sc[...] = jnp.full(m_sc.shape, MASKED, F32)
    acc_sc[...] = jnp.zeros(acc_sc.shape, F32)
    for p_ref, al_ref in zip(p_bufs, al_bufs):
        p_ref[...] = jnp.zeros(p_ref.shape, p_ref.dtype)
        al_ref[...] = jnp.ones(al_ref.shape, F32)

    def rows(c):
        return pl.ds(pl.multiple_of(jnp.clip(c, 0, last_chunk) * tk, tk), tk)

    def scores(c, slot, col=0):
        sT = _dot(k_ref[rows(c), :], qT[:, col:])
        s_bufs[slot][:, col:] = sT
        mx_bufs[slot][:, col:] = jnp.max(sT, axis=0, keepdims=True)

    def flush(c, slot, col=0):
        acc_sc[:, col:] = (al_bufs[slot][:, col:] * acc_sc[:, col:]
                           + _dot(vt_ref[:, rows(c)], p_bufs[slot][:, col:]))

    def softmax(sT, top, slot, col):
        m_old = m_sc[:, col:]
        m_new = jnp.maximum(m_old, top)
        al_bufs[slot][:, col:] = jnp.exp2(m_old - m_new)
        p_bufs[slot][:, col:] = jnp.exp2(sT - m_new).astype(p_bufs[slot].dtype)
        m_sc[:, col:] = m_new

    def step(tau, slot, mask, cols):
        col_flush, col, col_next = cols
        flush(tau - 2, slot, col_flush)
        if col_next is not None:
            scores(tau + 1, 1 - slot, col_next)
        sT = s_bufs[slot][:, col:]
        if mask is None:
            softmax(sT, mx_bufs[slot][:, col:], slot, col)
        else:
            sT = mask(sT, tau, col)
            softmax(sT, jnp.max(sT, axis=0, keepdims=True), slot, col)

    def full_steps(tau0, count):
        for j in range(count):
            step(tau0 + j, j % 2, None, (0, 0, 0))

    scores(0, 0)
    trips = n_full // unroll
    lax.fori_loop(0, trips, lambda u, c: (full_steps(unroll * u, unroll), c)[1], 0)
    done = unroll * trips
    pairs = (n_full - done) // 2
    lax.fori_loop(0, pairs, lambda u, c: (full_steps(done + 2 * u, 2), c)[1], 0)
    tau = done + 2 * pairs
    cols = [tail_col(j) for j in range(tail_steps)]
    for j in range(tail_steps):
        col_flush = cols[j - 2] if j >= 2 else 0
        col_next = cols[j + 1] if j + 1 < tail_steps else None
        step(tau + j, j % 2, mask_tail, (col_flush, cols[j], col_next))
    flush(tau + tail_steps - 2, 0, cols[-2])
    flush(tau + tail_steps - 1, 1, cols[-1])
    acc = acc_sc[...]
    return acc[:V_DIM, :] / acc[V_DIM:V_DIM + 1, :]


def _attend_scratch(tq, tk):
    return [pltpu.VMEM((1, tq), F32), pltpu.VMEM((VT_ROWS, tq), F32),
            pltpu.VMEM((tk, tq), F32), pltpu.VMEM((tk, tq), F32),
            pltpu.VMEM((tk, tq), BF16), pltpu.VMEM((tk, tq), BF16),
            pltpu.VMEM((1, tq), F32), pltpu.VMEM((1, tq), F32),
            pltpu.VMEM((1, tq), F32), pltpu.VMEM((1, tq), F32)]


def _transpose_q(q_ref):
    return q_ref[...].astype(F32).T.astype(BF16)


def _flash_kernel(q_ref, k_ref, vt_ref, o_ref, *scratch, tq, tk):
    i = pl.program_id(2)

    def causal(sT, c, col):
        key = lax.broadcasted_iota(jnp.int32, sT.shape, 0) + c * tk
        qry = lax.broadcasted_iota(jnp.int32, sT.shape, 1) + (i * tq + col)
        return jnp.where(key <= qry, sT, MASKED)

    per_tile = tq // tk
    oT = _attend_chunks(_transpose_q(q_ref), k_ref, vt_ref, scratch, tk=tk, n_full=i * per_tile,
                        mask_tail=causal, tail_steps=per_tile, tail_col=lambda j: j * tk,
                        unroll=8)
    o_ref[...] = oT.T.astype(o_ref.dtype)


def _flash_attention(q, k, vt, heads, qk_w, tq=1024, tk=256):
    bsz, seq, _ = q.shape
    assert tq % (2 * tk) == 0 and seq % tq == 0
    return pl.pallas_call(
        functools.partial(_flash_kernel, tq=tq, tk=tk),
        grid=(bsz, heads, seq // tq),
        in_specs=[pl.BlockSpec((None, tq, qk_w), lambda b, h, i: (b, i, h)),
                  pl.BlockSpec((None, seq, qk_w), lambda b, h, i: (b, 0, h)),
                  pl.BlockSpec((None, None, VT_ROWS, seq), lambda b, h, i: (b, h, 0, 0))],
        out_specs=pl.BlockSpec((None, tq, V_DIM), lambda b, h, i: (b, i, h)),
        out_shape=jax.ShapeDtypeStruct((bsz, seq, heads * V_DIM), BF16),
        scratch_shapes=_attend_scratch(tq, tk),
        compiler_params=_cparams("parallel", "parallel", "arbitrary"),
        name="mla_flash",
    )(q, k, vt)


def _kmean_kernel(k_ref, o_ref):
    k = k_ref[...].astype(F32)
    o_ref[...] = jnp.mean(k.reshape(SUBLANES, MOBA_BLOCK, k.shape[-1]), axis=1)


def _kmean(qk):
    bsz, seq, _ = qk.shape
    rows = SUBLANES * MOBA_BLOCK
    return pl.pallas_call(
        _kmean_kernel,
        grid=(bsz, seq // rows),
        in_specs=[pl.BlockSpec((None, rows, A_W), lambda b, i: (b, i, KA_BLK * LANES // A_W))],
        out_specs=pl.BlockSpec((None, SUBLANES, A_W), lambda b, i: (b, i, 0)),
        out_shape=jax.ShapeDtypeStruct((bsz, seq // MOBA_BLOCK, A_W), F32),
        compiler_params=_cparams("parallel", "parallel"),
        name="moba_kmean",
    )(qk)


def _block_attention(q, k, v, visible=None):
    s = _dot_nt(q, k)
    if visible is not None:
        s = jnp.where(visible, s, MASKED)
    m = jnp.max(s, axis=-1, keepdims=True)
    p = jnp.exp2(s - m).astype(BF16)
    v_ones = jnp.concatenate([v, jnp.ones((v.shape[0], LANES), v.dtype)], axis=-1)
    acc = _dot(p, v_ones)
    den = acc[:, V_DIM:]
    return acc[:, :V_DIM] / den, m + jnp.log2(den)


def _moba_gate_kernel(q_ref, km_ref, ids_ref, cnt_ref, qf_ref):
    t = MOBA_BLOCK
    i = pl.program_id(1)
    nb = km_ref.shape[0]
    blk = lax.broadcasted_iota(jnp.int32, (nb, t), 0)
    neg_inf = jnp.float32(-jnp.inf)
    not_after = (lax.broadcasted_iota(jnp.int32, (t, t), 0)
                 <= lax.broadcasted_iota(jnp.int32, (t, t), 1))
    upper = jnp.where(not_after, 1.0, 0.0).astype(BF16)
    ones = jnp.ones((SUBLANES, t), BF16)
    for hd in range(A_HEADS):
        sl = slice(hd * HEAD_DIM, (hd + 1) * HEAD_DIM)
        q = q_ref[:, sl].astype(F32)
        qf_ref[hd] = q
        qT = q.T.astype(BF16)
        km = km_ref[:, sl]
        km_hi = km.astype(BF16)
        km_lo = (km - km_hi.astype(F32)).astype(BF16)
        g = jnp.where(blk < i, _dot(km_hi, qT) + _dot(km_lo, qT), neg_inf)
        picks, ranks, counts = [], [], []
        for _ in range(MOBA_TOPK):
            mx = jnp.max(g, axis=0, keepdims=True)
            is_max = (g == mx) & (mx > neg_inf)
            first = jnp.min(jnp.where(is_max, blk, nb), axis=0, keepdims=True)
            pick = blk == first
            g = jnp.where(pick, neg_inf, g)
            onehot = jnp.where(pick, 1.0, 0.0).astype(BF16)
            before = _dot(onehot, upper)
            rank = jnp.sum(jnp.where(pick, before - 1.0, 0.0), axis=0, keepdims=True)
            picks.append(first)
            ranks.append(rank.astype(jnp.int32))
            counts.append(_dot_nt(ones, onehot)[0:1, :])
        pad_i = jnp.zeros((SUBLANES - 2 * MOBA_TOPK, t), jnp.int32)
        ids_ref[hd] = jnp.concatenate(picks + ranks + [pad_i], axis=0)
        pad_f = jnp.zeros((SUBLANES - MOBA_TOPK, nb), F32)
        cnt_ref[hd] = jnp.concatenate(counts + [pad_f], axis=0)


def _moba_gate(qk, kmean):
    bsz, seq, _ = qk.shape
    t = MOBA_BLOCK
    nb = seq // t
    return pl.pallas_call(
        _moba_gate_kernel,
        grid=(bsz, nb),
        in_specs=[pl.BlockSpec((None, t, A_W), lambda b, i: (b, i, QA_BLK * LANES // A_W)),
                  pl.BlockSpec((None, nb, A_W), lambda b, i: (b, 0, 0))],
        out_specs=[pl.BlockSpec((None, A_HEADS, SUBLANES, t), lambda b, i: (b, 0, 0, i)),
                   pl.BlockSpec((None, A_HEADS, None, SUBLANES, nb), lambda b, i: (b, 0, i, 0, 0)),
                   pl.BlockSpec((None, A_HEADS, t, HEAD_DIM), lambda b, i: (b, 0, i, 0))],
        out_shape=[jax.ShapeDtypeStruct((bsz, A_HEADS, SUBLANES, seq), jnp.int32),
                   jax.ShapeDtypeStruct((bsz, A_HEADS, nb, SUBLANES, nb), F32),
                   jax.ShapeDtypeStruct((bsz, A_HEADS, seq, HEAD_DIM), F32)],
        compiler_params=_cparams("parallel", "parallel"),
        name="moba_gate",
    )(qk, kmean)


def _moba_routes(ids, cnt, seq):
    bsz, heads = ids.shape[:2]
    bh, t = bsz * heads, MOBA_BLOCK
    nb = seq // t
    tiles = _moba_tiles(seq)
    picks = ids[:, :, 0:MOBA_TOPK, :].reshape(bh, MOBA_TOPK, nb, t)
    ranks = ids[:, :, MOBA_TOPK:2 * MOBA_TOPK, :].reshape(bh, MOBA_TOPK, nb, t)
    per_tile = cnt[:, :, :, 0:MOBA_TOPK, :].astype(jnp.int32).reshape(bh, nb * MOBA_TOPK, nb)
    before = jnp.cumsum(per_tile, axis=1) - per_tile
    total = jnp.sum(per_tile, axis=1)
    padded = -(-total // t) * t
    ends = jnp.cumsum(padded, axis=1)
    base = before + (ends - padded)[:, None, :]
    base = base.reshape(bh, nb, MOBA_TOPK, nb).transpose(0, 2, 1, 3)
    onehot = picks[..., None] == jnp.arange(nb)
    pos = jnp.sum(jnp.where(onehot, base[:, :, :, None, :], 0), axis=-1) + ranks
    pos = jnp.where(picks < nb, pos, (tiles - 1) * t)
    pos = pos + (jnp.arange(bh, dtype=jnp.int32) * (tiles * t))[:, None, None, None]
    pos = pos.reshape(bh, MOBA_TOPK, seq).transpose(1, 0, 2).reshape(MOBA_TOPK, bh * seq)
    tile_start = jnp.arange(tiles, dtype=jnp.int32) * t
    tile_blk = jnp.sum(tile_start[None, :, None] >= ends[:, None, :], axis=-1)
    tile_blk = jnp.where(tile_start[None, :] < ends[:, -1:], tile_blk, -1)
    return pos.astype(jnp.int32), tile_blk.astype(jnp.int32)


def _moba_tiles(seq):
    nb = seq // MOBA_BLOCK
    return -(-(MOBA_TOPK * nb + nb + 1) // GROUP_STEP) * GROUP_STEP


def _sc_mesh():
    return plsc.VectorSubcoreMesh(core_axis_name="core", subcore_axis_name="subcore")


def _sc_scatter_rows(x, idx, rows):
    slots, n = idx.shape
    d = x.shape[1]

    @pl.kernel(out_type=jax.ShapeDtypeStruct((rows, d), x.dtype), mesh=_sc_mesh())
    def scatter(x_hbm, i_hbm, o_hbm):
        def body(x_vmem, i_vmem):
            pltpu.sync_copy(x_vmem, o_hbm.at[i_vmem.at[0]])

        pltpu.emit_pipeline(
            body, grid=(slots, n // SC_WINDOW),
            in_specs=[pl.BlockSpec((SC_WINDOW, d), lambda s, i: (i, 0)),
                      pl.BlockSpec((1, SC_WINDOW), lambda s, i: (s, i))],
            out_specs=[],
            core_axis_name=("core", "subcore"),
            dimension_semantics=(pltpu.PARALLEL, pltpu.PARALLEL),
        )(x_hbm, i_hbm)

    return scatter(x, idx)


def _sc_gather_rows(x, idx):
    n = idx.shape[0]
    d = x.shape[1]

    @pl.kernel(out_type=jax.ShapeDtypeStruct((n, d), x.dtype), mesh=_sc_mesh())
    def gather(x_hbm, i_hbm, o_hbm):
        def body(i_vmem, o_vmem):
            pltpu.sync_copy(x_hbm.at[i_vmem.at[0]], o_vmem)

        pltpu.emit_pipeline(
            body, grid=(n // SC_WINDOW,),
            in_specs=[pl.BlockSpec((1, SC_WINDOW), lambda i: (0, i))],
            out_specs=[pl.BlockSpec((SC_WINDOW, d), lambda i: (i, 0))],
            core_axis_name=("core", "subcore"),
            dimension_semantics=(pltpu.PARALLEL,),
        )(i_hbm, o_hbm)

    return gather(x, idx.reshape(1, n))


def _moba_group_kernel(tb_ref, q_ref, *refs):
    t = MOBA_BLOCK
    k_refs, v_refs = refs[:GROUP_STEP], refs[GROUP_STEP:2 * GROUP_STEP]
    o_ref, lse_ref = refs[-2:]
    g, step = pl.program_id(0), pl.program_id(1)
    first = step * GROUP_STEP

    @pl.when(tb_ref[g, first] < 0)
    def _():
        o_ref[...] = jnp.zeros(o_ref.shape, o_ref.dtype)
        lse_ref[...] = jnp.full(lse_ref.shape, MASKED, lse_ref.dtype)

    @pl.when(tb_ref[g, first] >= 0)
    def _():
        for u in range(GROUP_STEP):
            used = tb_ref[g, first + u] >= 0
            rows = slice(u * t, (u + 1) * t)
            o, lse = _block_attention(q_ref[rows, :].astype(BF16), k_refs[u][...], v_refs[u][...])
            o_ref[rows, :] = jnp.where(used, o, 0.0)
            lse_ref[rows, :] = jnp.where(used, lse, MASKED)


def _moba_group_attention(q_grouped, tile_blk, qk, v, after):
    bh, rows, _ = q_grouped.shape
    t = MOBA_BLOCK
    tiles = rows // t
    heads = A_HEADS

    def block_of(u, first_col):
        return lambda g, s, tb: (g // heads, jnp.maximum(tb[g, s * GROUP_STEP + u], 0),
                                 first_col + g % heads)

    row_tile = pl.BlockSpec((None, GROUP_STEP * t, HEAD_DIM), lambda g, s, tb: (g, s, 0))
    key_value = lambda first_col: [pl.BlockSpec((None, t, HEAD_DIM), block_of(u, first_col))
                                   for u in range(GROUP_STEP)]
    grid_spec = pltpu.PrefetchScalarGridSpec(
        num_scalar_prefetch=1,
        grid=(bh, tiles // GROUP_STEP),
        in_specs=([row_tile] + key_value(KA_BLK) + key_value(0)
                  + [pl.BlockSpec(memory_space=pl.ANY)]),
        out_specs=[row_tile, row_tile],
    )
    return pl.pallas_call(
        _moba_group_kernel,
        grid_spec=grid_spec,
        out_shape=[jax.ShapeDtypeStruct(q_grouped.shape, F32)] * 2,
        compiler_params=_cparams("parallel", "parallel"),
        name="moba_group",
    )(tile_blk, q_grouped, *([qk] * GROUP_STEP), *([v] * GROUP_STEP), after)


def _moba_merge_kernel(q_ref, k_ref, v_ref, po_ref, pl_ref, o_ref):
    t = MOBA_BLOCK
    causal = (lax.broadcasted_iota(jnp.int32, (t, t), 1)
              <= lax.broadcasted_iota(jnp.int32, (t, t), 0))
    for hd in range(A_HEADS):
        sl = slice(hd * HEAD_DIM, (hd + 1) * HEAD_DIM)
        o_own, lse_own = _block_attention(q_ref[:, sl], k_ref[:, sl], v_ref[:, sl], causal)
        outs = [o_own] + [po_ref[s, hd] for s in range(MOBA_TOPK)]
        lses = [lse_own] + [pl_ref[s, hd] for s in range(MOBA_TOPK)]
        top = functools.reduce(jnp.maximum, lses)
        weights = [jnp.exp2(l - top) for l in lses]
        num = sum(w * o for w, o in zip(weights, outs))
        o_ref[:, sl] = (num / sum(weights)).astype(o_ref.dtype)


def _moba_merge(qk, v, part_o, part_lse):
    bsz, seq, _ = qk.shape
    t = MOBA_BLOCK
    part = pl.BlockSpec((MOBA_TOPK, None, A_HEADS, t, HEAD_DIM), lambda b, i: (0, b, 0, i, 0))
    return pl.pallas_call(
        _moba_merge_kernel,
        grid=(bsz, seq // t),
        in_specs=[pl.BlockSpec((None, t, A_W), lambda b, i: (b, i, QA_BLK * LANES // A_W)),
                  pl.BlockSpec((None, t, A_W), lambda b, i: (b, i, KA_BLK * LANES // A_W)),
                  pl.BlockSpec((None, t, A_W), lambda b, i: (b, i, 0)),
                  part, part],
        out_specs=pl.BlockSpec((None, t, A_W), lambda b, i: (b, i, 0)),
        out_shape=jax.ShapeDtypeStruct((bsz, seq, A_W), BF16),
        compiler_params=_cparams("parallel", "parallel"),
        name="moba_merge",
    )(qk, qk, v, part_o, part_lse)


def _moba_regroup(qk, kmean):
    bsz, seq, _ = qk.shape
    bh = bsz * A_HEADS
    rows = _moba_tiles(seq) * MOBA_BLOCK
    ids, cnt, q_f32 = _moba_gate(qk, kmean)
    pos, tile_blk = _moba_routes(ids, cnt, seq)
    q_grouped = _sc_scatter_rows(q_f32.reshape(bh * seq, HEAD_DIM), pos, bh * rows)
    return q_grouped.reshape(bh, rows, HEAD_DIM), pos, tile_blk


def _moba_picked_blocks(q_grouped, pos, tile_blk, qk, v, after):
    bsz, seq, _ = qk.shape
    bh, rows, _ = q_grouped.shape
    o_g, lse_g = _moba_group_attention(q_grouped, tile_blk, qk, v, after)
    flat = pos.reshape(-1)
    back = lambda a: _sc_gather_rows(a.reshape(bh * rows, HEAD_DIM), flat).reshape(
        MOBA_TOPK, bsz, A_HEADS, seq, HEAD_DIM)
    return back(o_g), back(lse_g)


def _proj_dilated_kernel(h_ref, w_ref, c_ref, s_ref, q_ref, k_ref, v_ref, sc, *, d):
    acc = _dot(h_ref[...], w_ref[...])
    c, s = c_ref[...], s_ref[...]
    q_scale = HEAD_DIM ** -0.5 * LOG2E
    for j in range(acc.shape[1] // LANES):
        blk = acc[:, j * LANES:(j + 1) * LANES]
        if j < B_HEADS:
            blk = _rope128(blk, c, s) * q_scale
        elif j < 2 * B_HEADS:
            blk = _rope128(blk, c, s)
        sc[j] = blk
    rows = acc.shape[0] // d
    for r in range(d):
        for j in range(acc.shape[1] // LANES):
            dst = (q_ref, k_ref, v_ref)[j // B_HEADS]
            col = (j % B_HEADS) * LANES
            dst[r, :, col:col + LANES] = sc[j, pl.ds(r, rows, stride=d), :].astype(dst.dtype)


def _proj_dilated(h, w, rope_h, d, bsz, seq, tm=512):
    m, k = h.shape
    nt = seq // tm
    pos = pl.BlockSpec((tm, LANES), lambda i: (i % nt, 0))
    out = pl.BlockSpec((None, d, tm // d, B_W), lambda i: (i // nt, 0, i % nt, 0))
    return pl.pallas_call(
        functools.partial(_proj_dilated_kernel, d=d),
        grid=(m // tm,),
        in_specs=[pl.BlockSpec((tm, k), lambda i: (i, 0)), pl.BlockSpec(w.shape, lambda i: (0, 0)),
                  pos, pos],
        out_specs=[out] * 3,
        out_shape=[jax.ShapeDtypeStruct((bsz, d, seq // d, B_W), BF16)] * 3,
        scratch_shapes=[pltpu.VMEM((w.shape[1] // LANES, tm, LANES), F32)],
        compiler_params=_cparams("parallel"),
        name=f"proj_dilated_d{d}",
    )(h, w, *rope_h)


def _dilated_kernel(q_ref, kc_ref, kp_ref, vc_ref, vp_ref, o_ref, lse_ref, *, span):
    t, tp = q_ref.shape[0], kp_ref.shape[0]
    i = pl.program_id(2)
    shape = (2 * tp, tp)
    key_row = lax.broadcasted_iota(jnp.int32, shape, 0)
    dist = lax.broadcasted_iota(jnp.int32, shape, 1) + tp - key_row
    visible = (dist >= 0) & (dist <= span)
    bias = jnp.where(visible, 0.0, MASKED)
    bias_first = jnp.where(visible & ((key_row >= tp) | (i > 0)), 0.0, MASKED)
    ones = jnp.ones((BF16_ROWS, tp + t), BF16)

    def transposed(x):
        return x.astype(F32).T.astype(BF16)

    for j in range(B_HEADS):
        sl = slice(j * LANES, (j + 1) * LANES)
        qT = transposed(q_ref[:, sl])
        k_all = jnp.concatenate([kp_ref[:, sl], kc_ref[:, sl]], axis=0)
        vt_all = jnp.concatenate([transposed(vp_ref[:, sl]), transposed(vc_ref[:, sl])], axis=1)
        vt_all = jnp.concatenate([vt_all, ones], axis=0)
        outs, lses = [], []
        for u in range(t // tp):
            window = slice(u * tp, (u + 2) * tp)
            s = _dot(k_all[window, :], qT[:, u * tp:(u + 1) * tp])
            s = s + (bias_first if u == 0 else bias)
            m = jnp.max(s, axis=0, keepdims=True)
            p = jnp.exp2(s - m).astype(BF16)
            acc = _dot(vt_all[:, window], p)
            den = acc[V_DIM:V_DIM + 1, :]
            outs.append(acc[:V_DIM, :] / den)
            lses.append(m + jnp.log2(den))
        o_ref[:, sl] = jnp.concatenate(outs, axis=1).T
        lse = jnp.concatenate(lses, axis=1)
        lse_ref[:, sl] = jnp.broadcast_to(lse, (LANES, t)).T


def _dilated_attention(q, k, v, span, t=1024):
    bsz, d, length, _ = q.shape
    t = min(t, length)
    tp = B_QBLOCK
    assert span <= tp and t % tp == 0
    cur = pl.BlockSpec((None, None, t, B_W), lambda b, r, i: (b, r, i, 0))
    prev = pl.BlockSpec((None, None, tp, B_W),
                        lambda b, r, i: (b, r, jnp.maximum(i * (t // tp) - 1, 0), 0))
    return pl.pallas_call(
        functools.partial(_dilated_kernel, span=span),
        grid=(bsz, d, length // t),
        in_specs=[cur, cur, prev, cur, prev],
        out_specs=[cur, cur],
        out_shape=[jax.ShapeDtypeStruct(q.shape, F32)] * 2,
        compiler_params=_cparams("parallel", "parallel", "parallel"),
        name=f"dilated_d{d}",
    )(q, k, k, v, v)


def _natural_rows(ref, sc):
    d, rows = ref.shape[0], ref.shape[1]
    if d == 1:
        return ref[0]
    for r in range(d):
        for j in range(B_HEADS):
            sc[j, pl.ds(r, rows, stride=d), :] = ref[r, :, j * LANES:(j + 1) * LANES]
    return jnp.concatenate([sc[j] for j in range(B_HEADS)], axis=-1)


def _mixer_tail_kernel(x_ref, oa_ref, o0_ref, o1_ref, o2_ref, l0_ref, l1_ref, l2_ref, oc_ref,
                       g_ref, wpa_ref, wpb_ref, wpc_ref, wo_ref, y_ref, *scratch):
    o0, o1, o2, l0, l1, l2 = [
        _natural_rows(ref, sc)
        for ref, sc in zip((o0_ref, o1_ref, o2_ref, l0_ref, l1_ref, l2_ref), scratch)]
    mx = jnp.maximum(jnp.maximum(l0, l1), l2)
    e0, e1, e2 = jnp.exp2(l0 - mx), jnp.exp2(l1 - mx), jnp.exp2(l2 - mx)
    ob = (e0 * o0 + e1 * o1 + e2 * o2) / (e0 + e1 + e2)
    pa = _dot(oa_ref[...], wpa_ref[...])
    pb = _dot(ob.astype(BF16), wpb_ref[...])
    pc = _dot(oc_ref[...], wpc_ref[...])
    d = D_MODEL
    merged = (g_ref[:, 0:d].astype(F32) * pa + g_ref[:, d:2 * d].astype(F32) * pb
              + g_ref[:, 2 * d:3 * d].astype(F32) * pc)
    y_ref[...] = x_ref[...] + _dot(merged.astype(BF16), wo_ref[...])


def _mixer_tail(x, out_a, o_groups, lse_groups, out_c, gates, w_pa, w_pb, w_pc, w_o, seq, tm=256):
    m, d = x.shape
    nt = seq // tm
    row = lambda width: pl.BlockSpec((tm, width), lambda i: (i, 0))
    residue = lambda g: pl.BlockSpec((None, g.shape[1], tm // g.shape[1], B_W),
                                     lambda i: (i // nt, 0, i % nt, 0))
    weights = [_resident(w) for w in (w_pa, w_pb, w_pc, w_o)]
    groups = list(o_groups) + list(lse_groups)
    return pl.pallas_call(
        _mixer_tail_kernel,
        grid=(m // tm,),
        in_specs=([row(d), row(A_W)] + [residue(g) for g in groups]
                  + [row(C_W), row(3 * d)] + weights),
        out_specs=row(d),
        out_shape=jax.ShapeDtypeStruct((m, d), F32),
        scratch_shapes=[pltpu.VMEM((B_HEADS, tm, LANES), F32) for _ in groups],
        compiler_params=_cparams("parallel"),
        name="mixer_tail",
    )(x, out_a, *groups, out_c, gates, w_pa, w_pb, w_pc, w_o)


def _mem_kv_kernel(mem_ref, g_ref, wk_ref, wv_ref, k_ref, v_ref):
    memn = _rms(mem_ref[...], g_ref[...]).astype(BF16)
    k_ref[...] = _dot(memn, wk_ref[...]).astype(k_ref.dtype)
    v_ref[...] = _dot(memn, wv_ref[...]).astype(v_ref.dtype)


def _mem_kv(mem, g, wk, wv):
    bsz, n, d = mem.shape
    out = pl.BlockSpec((None, n, X_W), lambda b: (b, 0, 0))
    return pl.pallas_call(
        _mem_kv_kernel,
        grid=(bsz,),
        in_specs=[pl.BlockSpec((None, n, d), lambda b: (b, 0, 0)),
                  pl.BlockSpec((1, d), lambda b: (0, 0)),
                  pl.BlockSpec(wk.shape, lambda b: (0, 0)),
                  pl.BlockSpec(wv.shape, lambda b: (0, 0))],
        out_specs=[out, out],
        out_shape=[jax.ShapeDtypeStruct((bsz, n, X_W), BF16)] * 2,
        compiler_params=_cparams("parallel"),
        name="mem_kv",
    )(mem, g.reshape(1, d), wk, wv)


def _mem_attn_kernel(x_ref, g_ref, wq_ref, k_ref, v_ref, wo_ref, y_ref):
    x = x_ref[...]
    h = _rms(x, g_ref[...]).astype(BF16)
    q = (_dot(h, wq_ref[...]) * HEAD_DIM ** -0.5).astype(BF16)
    heads = []
    for hd in range(X_HEADS):
        sl = slice(hd * HEAD_DIM, (hd + 1) * HEAD_DIM)
        s = _dot_nt(q[:, sl], k_ref[:, sl])
        p = jnp.exp(s - jnp.max(s, axis=-1, keepdims=True))
        o = _dot(p.astype(BF16), v_ref[:, sl]) / jnp.sum(p, axis=-1, keepdims=True)
        heads.append(o.astype(BF16))
    y_ref[...] = x + _dot(jnp.concatenate(heads, axis=-1), wo_ref[...])


def _mem_attention(x, g, wq, kmem, vmem, wo, seq, tm=512):
    m, d = x.shape
    nt = seq // tm
    n = kmem.shape[1]
    kv = pl.BlockSpec((None, n, X_W), lambda i: (i // nt, 0, 0))
    return pl.pallas_call(
        _mem_attn_kernel,
        grid=(m // tm,),
        in_specs=[pl.BlockSpec((tm, d), lambda i: (i, 0)),
                  pl.BlockSpec((1, d), lambda i: (0, 0)),
                  pl.BlockSpec(wq.shape, lambda i: (0, 0)), kv, kv,
                  pl.BlockSpec(wo.shape, lambda i: (0, 0))],
        out_specs=pl.BlockSpec((tm, d), lambda i: (i, 0)),
        out_shape=jax.ShapeDtypeStruct((m, d), F32),
        compiler_params=_cparams("parallel"),
        name="mem_attention",
    )(x, g.reshape(1, d), wq, kmem, vmem, wo)


def _ffn_up_kernel(x_ref, halo_ref, g_ref, wg_ref, wv_ref, cwg_ref, cwv_ref, cbg_ref, cbv_ref,
                   act_ref, h_sc, *, tiles_per_seq):
    i = pl.program_id(0)
    tm = x_ref.shape[0]

    @pl.when(pl.program_id(1) == 0)
    def _():
        g = g_ref[...]
        keep = (i % tiles_per_seq != 0).astype(F32)
        h_sc[0:HALO, :] = (_rms(halo_ref[...], g) * keep).astype(h_sc.dtype)
        h_sc[HALO:, :] = _rms(x_ref[...], g).astype(h_sc.dtype)

    h = h_sc[...]

    def conv(w_ref, cw_ref, cb_ref):
        u = _dot(h, w_ref[...])
        c = cb_ref[...]
        for tap in range(CONV_W):
            lo = HALO - (CONV_W - 1) + tap
            c = c + cw_ref[tap:tap + 1, :] * u[lo:lo + tm, :]
        return c

    act = jax.nn.silu(conv(wg_ref, cwg_ref, cbg_ref)) * conv(wv_ref, cwv_ref, cbv_ref)
    act_ref[...] = act.astype(act_ref.dtype)


def _ffn_down_kernel(a_ref, w_ref, x_ref, y_ref):
    y_ref[...] = x_ref[...] + _dot(a_ref[...], w_ref[...])


def _conv_ffn(x, g, w_up, conv_w, conv_b, w_down, layer, seq):
    m, d = x.shape
    act = _ffn_up(x, g, w_up, conv_w, conv_b, layer, seq)
    tm, tn = 1024, FFN_TF
    return pl.pallas_call(
        _ffn_down_kernel,
        grid=(m // tm, d // tn),
        in_specs=[pl.BlockSpec((tm, D_FF), lambda i, j: (i, 0)),
                  pl.BlockSpec((None, D_FF, tn), lambda i, j: (layer, 0, j)),
                  pl.BlockSpec((tm, tn), lambda i, j: (i, j))],
        out_specs=pl.BlockSpec((tm, tn), lambda i, j: (i, j)),
        out_shape=jax.ShapeDtypeStruct((m, d), F32),
        compiler_params=_cparams("parallel", "parallel"),
        name="ffn_down",
    )(act, w_down, x)


def _ffn_up(x, g, w_up, conv_w, conv_b, layer, seq, tm=1024, tf=FFN_TF):
    m, d = x.shape
    nf = D_FF_PAD // tf
    halo_blocks = tm // HALO
    return pl.pallas_call(
        functools.partial(_ffn_up_kernel, tiles_per_seq=seq // tm),
        grid=(m // tm, nf),
        in_specs=[pl.BlockSpec((tm, d), lambda i, f: (i, 0)),
                  pl.BlockSpec((HALO, d), lambda i, f: (jnp.maximum(i * halo_blocks - 1, 0), 0)),
                  pl.BlockSpec((1, d), lambda i, f: (0, 0)),
                  pl.BlockSpec((None, d, tf), lambda i, f: (layer, 0, f)),
                  pl.BlockSpec((None, d, tf), lambda i, f: (layer, 0, f + nf)),
                  pl.BlockSpec((CONV_W, tf), lambda i, f: (0, f)),
                  pl.BlockSpec((CONV_W, tf), lambda i, f: (0, f + nf)),
                  pl.BlockSpec((1, tf), lambda i, f: (0, f)),
                  pl.BlockSpec((1, tf), lambda i, f: (0, f + nf))],
        out_specs=pl.BlockSpec((tm, tf), lambda i, f: (i, f)),
        out_shape=jax.ShapeDtypeStruct((m, D_FF_PAD), BF16),
        scratch_shapes=[pltpu.VMEM((HALO + tm, d), BF16)],
        compiler_params=_cparams("parallel", "arbitrary"),
        name="ffn_up",
    )(x, x, g.reshape(1, d), w_up, w_up, conv_w, conv_w, conv_b, conv_b)


def _rope_tables(seq):
    def angles(dim):
        inv_freq = jnp.exp(jnp.arange(0, dim, 2, dtype=F32) * (-math.log(ROPE_THETA) / dim))
        ang = jnp.arange(seq, dtype=F32)[:, None] * inv_freq[None, :]
        return jnp.cos(ang), jnp.sin(ang)

    cos_h, sin_h = angles(HEAD_DIM)
    rope_h = (jnp.concatenate([cos_h, cos_h], axis=-1), jnp.concatenate([-sin_h, sin_h], axis=-1))
    cos_r, sin_r = angles(ROPE_DIM)
    z = jnp.zeros_like(cos_r)
    rope_r = (jnp.concatenate([cos_r, cos_r, z, z], axis=-1),
              jnp.concatenate([-sin_r, z, z, z], axis=-1),
              jnp.concatenate([z, sin_r, z, z], axis=-1))
    return rope_h, rope_r


def _split_in(w_in):
    return [w_in[:, IN_OFFSETS[k]:IN_OFFSETS[k + 1]] for k in range(len(IN_WIDTHS))]


def _pad_cols(w, width):
    return jnp.pad(w, ((0, 0), (0, width - w.shape[1])))


def _layer_params(w_in, w_uq, w_ukv, conv_w, conv_b):
    qa, ka, va, qb, kb, vb, cq, ckv, kr, gates = _split_in(w_in)
    w_qk = jnp.concatenate([qa, ka], axis=1).astype(BF16)
    group_cols = lambda w, g: w[:, g * B_W:(g + 1) * B_W]
    w_b = [jnp.concatenate([group_cols(qb, g), group_cols(kb, g), group_cols(vb, g)],
                           axis=1).astype(BF16) for g in range(len(B_GROUPS))]
    w_down_in = jnp.concatenate([cq, ckv, _pad_cols(kr, LANES)], axis=1).astype(BF16)
    uq = w_uq.reshape(Q_LORA, C_HEADS, NOPE_DIM + ROPE_DIM)
    uq = jnp.pad(uq, ((0, 0), (0, 0), (0, C_QK - NOPE_DIM - ROPE_DIM)))
    ukv = w_ukv.reshape(KV_LORA, C_HEADS, NOPE_DIM + V_DIM)
    return dict(
        w_qk=w_qk, w_va=va.astype(BF16), w_b=w_b, w_gates=gates.astype(BF16),
        w_down_in=w_down_in,
        w_uq=uq.reshape(Q_LORA, C_HEADS * C_QK).astype(BF16),
        w_uk=ukv[:, :, :NOPE_DIM].reshape(KV_LORA, C_HEADS * NOPE_DIM).astype(BF16),
        w_uv=ukv[:, :, NOPE_DIM:].reshape(KV_LORA, C_W).astype(BF16),
        conv_w=_pad_ff_halves(conv_w),
        conv_b=_pad_ff_halves(conv_b.reshape(1, -1)),
    )


def _pad_ff_halves(w):
    pad = [(0, 0)] * (w.ndim - 1) + [(0, D_FF_PAD - D_FF)]
    return jnp.concatenate([jnp.pad(w[..., :D_FF], pad), jnp.pad(w[..., D_FF:], pad)], axis=-1)


def _qk_col_scale():
    q_scale = HEAD_DIM ** -0.5
    parts = [jnp.full((A_W,), q_scale * LOG2E, F32), jnp.ones((A_W,), F32)]
    return jnp.concatenate(parts).reshape(1, QK_W)


def _mixer(x, g_mix, p, g_cq, g_ckv, w_pa, w_pb, w_pc, w_o, rope_h, rope_r, bsz, seq):
    m = x.shape[0]
    h = _rmsnorm(x, g_mix, BF16)
    qk = _matmul(h, p["w_qk"], _mm_rope_kernel, BF16, 1024, 1024, seq=seq,
                 extras=(("col", _qk_col_scale()), ("pos", rope_h[0]), ("pos", rope_h[1])),
                 name="proj_qk_rope")
    qk3 = qk.reshape(bsz, seq, QK_W)
    q_grouped, pos, tile_blk = _moba_regroup(qk3, _kmean(qk3))
    v_a = _matmul(h, p["w_va"], _mm_plain_kernel, BF16, 1024, A_W, name="proj_va")
    gates = _matmul(h, p["w_gates"], _mm_sigmoid_kernel, BF16, 1024, 1024, name="proj_gates")
    v_a3 = v_a.reshape(bsz, seq, A_W)
    part_o, part_lse = _moba_picked_blocks(q_grouped, pos, tile_blk, qk3, v_a3, after=gates)
    cq, ckv, kr = _mla_down(h, p["w_down_in"], g_cq, g_ckv, rope_r, seq)
    q_c = _mla_q(cq, p["w_uq"], rope_r, seq)
    k_c, vt_c = _mla_kv(ckv, kr, p["w_uk"], p["w_uv"], bsz, seq)
    groups = []
    for (window, d), w_g in zip(B_GROUPS, p["w_b"]):
        q_g, k_g, v_g = _proj_dilated(h, w_g, rope_h, d, bsz, seq)
        groups.append(_dilated_attention(q_g, k_g, v_g, window // d))
    out_c = _flash_attention(q_c.reshape(bsz, seq, -1), k_c.reshape(bsz, seq, -1), vt_c,
                             C_HEADS, C_QK).reshape(m, C_W)
    out_a = _moba_merge(qk3, v_a3, part_o, part_lse).reshape(m, A_W)
    return _mixer_tail(x, out_a, [g[0] for g in groups], [g[1] for g in groups], out_c, gates,
                       w_pa.astype(BF16), w_pb.astype(BF16), w_pc.astype(BF16), w_o.astype(BF16),
                       seq)


def kernel(x, mem, g_mix, w_in, g_cq, g_ckv, w_uq, w_ukv, w_pa, w_pb, w_pc, w_o, g_mem, g_memkv,
           w_xq, w_xk, w_xv, w_xo, g_ffn, w_up, conv_w, conv_b, w_down, g_final):
    bsz, seq, d = x.shape
    rope_h, rope_r = _rope_tables(seq)
    xf = x.reshape(bsz * seq, d)
    w_down = w_down.astype(BF16)
    w_up = _pad_ff_halves(w_up.astype(BF16))
    for l in range(DEPTH):
        p = _layer_params(w_in[l], w_uq[l], w_ukv[l], conv_w[l], conv_b[l])
        xf = _mixer(xf, g_mix[l], p, g_cq[l], g_ckv[l], w_pa[l], w_pb[l], w_pc[l], w_o[l],
                    rope_h, rope_r, bsz, seq)
        kmem, vmem = _mem_kv(mem, g_memkv[l], w_xk[l].astype(BF16), w_xv[l].astype(BF16))
        xf = _mem_attention(xf, g_mem[l], w_xq[l].astype(BF16), kmem, vmem,
                            w_xo[l].astype(BF16), seq)
        xf = _conv_ffn(xf, g_ffn[l], w_up, p["conv_w"], p["conv_b"], w_down, l, seq)
    return _rmsnorm(xf, g_final, F32).reshape(bsz, seq, d)
```

```python
import functools
import math

import jax
import jax.numpy as jnp
import numpy as np
from jax import lax
from jax.experimental import pallas as pl
from jax.experimental.pallas import tpu as pltpu
from jax.experimental.pallas import tpu_sc as plsc

F32 = jnp.float32
BF16 = jnp.bfloat16

LANES = 128
SUBLANES = 8
V7X_VMEM_BYTES = 64 * 1024 * 1024
VMEM_LIMIT = V7X_VMEM_BYTES * 7 // 8

D_MODEL = 2048
DEPTH = 2
HEAD_DIM = 128
ROPE_THETA = 10000.0
EPS = 1e-6

A_HEADS = 4
MOBA_BLOCK = 256
MOBA_TOPK = 3

B_GROUPS = ((128, 1), (512, 4), (2048, 16))
B_HEADS = 4
B_QBLOCK = 128

C_HEADS = 8
Q_LORA = 1536
KV_LORA = 512
NOPE_DIM = 128
ROPE_DIM = 64
V_DIM = 128

X_HEADS = 4
D_FF = 5504
CONV_W = 3

A_W = A_HEADS * HEAD_DIM
B_QKV_W = len(B_GROUPS) * B_HEADS * HEAD_DIM
B_W = B_HEADS * HEAD_DIM
C_W = C_HEADS * V_DIM
X_W = X_HEADS * HEAD_DIM
IN_WIDTHS = (A_W, A_W, A_W, B_QKV_W, B_QKV_W, B_QKV_W, Q_LORA, KV_LORA, ROPE_DIM, 3 * D_MODEL)
IN_OFFSETS = tuple(int(o) for o in np.cumsum((0,) + IN_WIDTHS))

QK_W = 2 * A_W
QA_BLK, KA_BLK = 0, A_W // LANES

C_QK = 2 * LANES
MASKED = -1e30
LOG2E = math.log2(math.e)
BF16_ROWS = 16
VT_ROWS = V_DIM + BF16_ROWS
GROUP_STEP = 16
SC_WINDOW = 128

FFN_TF = 512
D_FF_PAD = -(-D_FF // FFN_TF) * FFN_TF
HALO = SUBLANES


def _cparams(*sem):
    return pltpu.CompilerParams(dimension_semantics=sem, vmem_limit_bytes=VMEM_LIMIT)


def _resident(arr):
    zeros = (0,) * arr.ndim
    return pl.BlockSpec(arr.shape, lambda *_: zeros, pipeline_mode=pl.Buffered(1))


def _dot(a, b):
    return jnp.dot(a, b, preferred_element_type=F32)


def _dot_nt(a, b):
    return lax.dot_general(a, b, (((1,), (1,)), ((), ())), preferred_element_type=F32)


def _rms(x, g):
    return x * lax.rsqrt(jnp.mean(x * x, axis=-1, keepdims=True) + EPS) * g


def _rmsnorm_kernel(x_ref, g_ref, o_ref):
    o_ref[...] = _rms(x_ref[...], g_ref[...]).astype(o_ref.dtype)


def _rmsnorm(x, g, out_dtype, tm=512):
    m, d = x.shape
    return pl.pallas_call(
        _rmsnorm_kernel,
        grid=(m // tm,),
        in_specs=[pl.BlockSpec((tm, d), lambda i: (i, 0)),
                  pl.BlockSpec((1, d), lambda i: (0, 0))],
        out_specs=pl.BlockSpec((tm, d), lambda i: (i, 0)),
        out_shape=jax.ShapeDtypeStruct((m, d), out_dtype),
        compiler_params=_cparams("parallel"),
        name="rmsnorm",
    )(x, g.reshape(1, d))


def _rope128(x, c, s):
    return x * c + pltpu.roll(x, HEAD_DIM // 2, 1) * s


def _rope64(x, c, sa, sb):
    half = ROPE_DIM // 2
    return x * c + pltpu.roll(x, LANES - half, 1) * sa + pltpu.roll(x, half, 1) * sb


def _mm_plain_kernel(a_ref, w_ref, o_ref):
    o_ref[...] = _dot(a_ref[...], w_ref[...]).astype(o_ref.dtype)


def _mm_sigmoid_kernel(a_ref, w_ref, o_ref):
    o_ref[...] = jax.nn.sigmoid(_dot(a_ref[...], w_ref[...])).astype(o_ref.dtype)


def _mm_rope_kernel(a_ref, w_ref, cs_ref, c_ref, s_ref, o_ref):
    acc = _dot(a_ref[...], w_ref[...])
    c = c_ref[...]
    s = s_ref[...]
    for j in range(acc.shape[1] // LANES):
        sl = slice(j * LANES, (j + 1) * LANES)
        o_ref[:, sl] = (_rope128(acc[:, sl], c, s) * cs_ref[:, sl]).astype(o_ref.dtype)


def _matmul(a, w, kernel, out_dtype, tm, tn, seq=None, extras=(), name="matmul"):
    m, k = a.shape
    n = w.shape[1]
    in_specs = [pl.BlockSpec((tm, k), lambda i, j: (i, 0)),
                pl.BlockSpec((k, tn), lambda i, j: (0, j))]
    args = [a, w]
    for kind, arr in extras:
        if kind == "col":
            in_specs.append(pl.BlockSpec((1, tn), lambda i, j: (0, j)))
        else:
            nt = seq // tm
            in_specs.append(pl.BlockSpec((tm, LANES), lambda i, j: (i % nt, 0)))
        args.append(arr)
    return pl.pallas_call(
        kernel,
        grid=(m // tm, n // tn),
        in_specs=in_specs,
        out_specs=pl.BlockSpec((tm, tn), lambda i, j: (i, j)),
        out_shape=jax.ShapeDtypeStruct((m, n), out_dtype),
        compiler_params=_cparams("parallel", "parallel"),
        name=name,
    )(*args)


def _mla_down_kernel(h_ref, w_ref, gq_ref, gkv_ref, c_ref, sa_ref, sb_ref,
                     cq_ref, ckv_ref, kr_ref):
    acc = _dot(h_ref[...], w_ref[...])
    cq_ref[...] = _rms(acc[:, :Q_LORA], gq_ref[...]).astype(cq_ref.dtype)
    ckv_ref[...] = _rms(acc[:, Q_LORA:Q_LORA + KV_LORA], gkv_ref[...]).astype(ckv_ref.dtype)
    kr = acc[:, Q_LORA + KV_LORA:]
    kr_ref[...] = _rope64(kr, c_ref[...], sa_ref[...], sb_ref[...]).astype(kr_ref.dtype)


def _mla_down(h, w, g_cq, g_ckv, rope_r, seq, tm=512):
    m, k = h.shape
    n = w.shape[1]
    nt = seq // tm
    row = lambda width: pl.BlockSpec((tm, width), lambda i: (i, 0))
    full = lambda r, c: pl.BlockSpec((r, c), lambda i: (0, 0))
    pos = pl.BlockSpec((tm, LANES), lambda i: (i % nt, 0))
    return pl.pallas_call(
        _mla_down_kernel,
        grid=(m // tm,),
        in_specs=[row(k), full(k, n), full(1, Q_LORA), full(1, KV_LORA), pos, pos, pos],
        out_specs=[row(Q_LORA), row(KV_LORA), row(LANES)],
        out_shape=[jax.ShapeDtypeStruct((m, Q_LORA), BF16),
                   jax.ShapeDtypeStruct((m, KV_LORA), BF16),
                   jax.ShapeDtypeStruct((m, LANES), BF16)],
        compiler_params=_cparams("parallel"),
        name="mla_down",
    )(h, w, g_cq.reshape(1, -1), g_ckv.reshape(1, -1), *rope_r)


def _mla_q_kernel(cq_ref, w_ref, c_ref, sa_ref, sb_ref, q_ref, *, scale):
    acc = _dot(cq_ref[...], w_ref[...])
    c, sa, sb = c_ref[...], sa_ref[...], sb_ref[...]
    for hd in range(C_HEADS):
        lo = hd * C_QK
        q_ref[:, lo:lo + LANES] = (acc[:, lo:lo + LANES] * scale).astype(q_ref.dtype)
        rope = _rope64(acc[:, lo + LANES:lo + C_QK], c, sa, sb)
        q_ref[:, lo + LANES:lo + C_QK] = (rope * scale).astype(q_ref.dtype)


def _mla_q(cq, w, rope_r, seq, tm=512):
    m, k = cq.shape
    n = w.shape[1]
    nt = seq // tm
    pos = pl.BlockSpec((tm, LANES), lambda i: (i % nt, 0))
    return pl.pallas_call(
        functools.partial(_mla_q_kernel, scale=(NOPE_DIM + ROPE_DIM) ** -0.5 * LOG2E),
        grid=(m // tm,),
        in_specs=[pl.BlockSpec((tm, k), lambda i: (i, 0)),
                  pl.BlockSpec((k, n), lambda i: (0, 0)), pos, pos, pos],
        out_specs=pl.BlockSpec((tm, n), lambda i: (i, 0)),
        out_shape=jax.ShapeDtypeStruct((m, n), BF16),
        compiler_params=_cparams("parallel"),
        name="mla_q",
    )(cq, w, *rope_r)


def _store_vt(v, vt_ref):
    vt = v.T
    for hd in range(vt_ref.shape[0]):
        vt_ref[hd, 0:V_DIM, :] = vt[hd * V_DIM:(hd + 1) * V_DIM, :].astype(vt_ref.dtype)
        vt_ref[hd, V_DIM:VT_ROWS, :] = jnp.ones((VT_ROWS - V_DIM, vt.shape[1]), vt_ref.dtype)


def _mla_kv_kernel(ckv_ref, kr_ref, wk_ref, wv_ref, k_ref, vt_ref):
    ckv = ckv_ref[...]
    kn = _dot(ckv, wk_ref[...])
    kr = kr_ref[...]
    for hd in range(C_HEADS):
        lo = hd * C_QK
        k_ref[:, lo:lo + LANES] = kn[:, hd * LANES:(hd + 1) * LANES].astype(k_ref.dtype)
        k_ref[:, lo + LANES:lo + C_QK] = kr
    _store_vt(_dot(ckv, wv_ref[...]), vt_ref)


def _mla_kv(ckv, kr, wk, wv, bsz, seq, tm=512):
    m, k = ckv.shape
    nt = seq // tm
    return pl.pallas_call(
        _mla_kv_kernel,
        grid=(m // tm,),
        in_specs=[pl.BlockSpec((tm, k), lambda i: (i, 0)),
                  pl.BlockSpec((tm, LANES), lambda i: (i, 0)),
                  pl.BlockSpec(wk.shape, lambda i: (0, 0)),
                  pl.BlockSpec(wv.shape, lambda i: (0, 0))],
        out_specs=[pl.BlockSpec((tm, C_HEADS * C_QK), lambda i: (i, 0)),
                   pl.BlockSpec((None, C_HEADS, VT_ROWS, tm), lambda i: (i // nt, 0, 0, i % nt))],
        out_shape=[jax.ShapeDtypeStruct((m, C_HEADS * C_QK), BF16),
                   jax.ShapeDtypeStruct((bsz, C_HEADS, VT_ROWS, seq), BF16)],
        compiler_params=_cparams("parallel"),
        name="mla_kv",
    )(ckv, kr, wk, wv)


def _attend_chunks(qT, k_ref, vt_ref, scratch, *, tk, n_full, mask_tail, tail_steps, tail_col,
                   unroll):
    m_sc, acc_sc, sa_sc, sb_sc, pa_sc, pb_sc, ala_sc, alb_sc, mxa_sc, mxb_sc = scratch
    s_bufs = (sa_sc, sb_sc)
    p_bufs = (pa_sc, pb_sc)
    al_bufs = (ala_sc, alb_sc)
    mx_bufs = (mxa_sc, mxb_sc)
    last_chunk = k_ref.shape[0] // tk - 1
    m_sc[...] = jnp.full(m_sc.shape, MASKED, F32)
    acc_sc[...] = jnp.zeros(acc_sc.shape, F32)
    for p_ref, al_ref in zip(p_bufs, al_bufs):
        p_ref[...] = jnp.zeros(p_ref.shape, p_ref.dtype)
        al_ref[...] = jnp.ones(al_ref.shape, F32)

    def rows(c):
        return pl.ds(pl.multiple_of(jnp.clip(c, 0, last_chunk) * tk, tk), tk)

    def scores(c, slot, col=0):
        sT = _dot(k_ref[rows(c), :], qT[:, col:])
        s_bufs[slot][:, col:] = sT
        mx_bufs[slot][:, col:] = jnp.max(sT, axis=0, keepdims=True)

    def flush(c, slot, col=0):
        acc_sc[:, col:] = (al_bufs[slot][:, col:] * acc_sc[:, col:]
                           + _dot(vt_ref[:, rows(c)], p_bufs[slot][:, col:]))

    def softmax(sT, top, slot, col):
        m_old = m_sc[:, col:]
        m_new = jnp.maximum(m_old, top)
        al_bufs[slot][:, col:] = jnp.exp2(m_old - m_new)
        p_bufs[slot][:, col:] = jnp.exp2(sT - m_new).astype(p_bufs[slot].dtype)
        m_sc[:, col:] = m_new

    def step(tau, slot, mask, cols):
        col_flush, col, col_next = cols
        flush(tau - 2, slot, col_flush)
        if col_next is not None:
            scores(tau + 1, 1 - slot, col_next)
        sT = s_bufs[slot][:, col:]
        if mask is None:
            softmax(sT, mx_bufs[slot][:, col:], slot, col)
        else:
            sT = mask(sT, tau, col)
            softmax(sT, jnp.max(sT, axis=0, keepdims=True), slot, col)

    def full_steps(tau0, count):
        for j in range(count):
            step(tau0 + j, j % 2, None, (0, 0, 0))

    scores(0, 0)
    trips = n_full // unroll
    lax.fori_loop(0, trips, lambda u, c: (full_steps(unroll * u, unroll), c)[1], 0)
    done = unroll * trips
    pairs = (n_full - done) // 2
    lax.fori_loop(0, pairs, lambda u, c: (full_steps(done + 2 * u, 2), c)[1], 0)
    tau = done + 2 * pairs
    cols = [tail_col(j) for j in range(tail_steps)]
    for j in range(tail_steps):
        col_flush = cols[j - 2] if j >= 2 else 0
        col_next = cols[j + 1] if j + 1 < tail_steps else None
        step(tau + j, j % 2, mask_tail, (col_flush, cols[j], col_next))
    flush(tau + tail_steps - 2, 0, cols[-2])
    flush(tau + tail_steps - 1, 1, cols[-1])
    acc = acc_sc[...]
    return acc[:V_DIM, :] / acc[V_DIM:V_DIM + 1, :]


def _attend_scratch(tq, tk):
    return [pltpu.VMEM((1, tq), F32), pltpu.VMEM((VT_ROWS, tq), F32),
            pltpu.VMEM((tk, tq), F32), pltpu.VMEM((tk, tq), F32),
            pltpu.VMEM((tk, tq), BF16), pltpu.VMEM((tk, tq), BF16),
            pltpu.VMEM((1, tq), F32), pltpu.VMEM((1, tq), F32),
            pltpu.VMEM((1, tq), F32), pltpu.VMEM((1, tq), F32)]


def _transpose_q(q_ref):
    return q_ref[...].astype(F32).T.astype(BF16)


def _flash_kernel(q_ref, k_ref, vt_ref, o_ref, *scratch, tq, tk):
    i = pl.program_id(2)

    def causal(sT, c, col):
        key = lax.broadcasted_iota(jnp.int32, sT.shape, 0) + c * tk
        qry = lax.broadcasted_iota(jnp.int32, sT.shape, 1) + (i * tq + col)
        return jnp.where(key <= qry, sT, MASKED)

    per_tile = tq // tk
    oT = _attend_chunks(_transpose_q(q_ref), k_ref, vt_ref, scratch, tk=tk, n_full=i * per_tile,
                        mask_tail=causal, tail_steps=per_tile, tail_col=lambda j: j * tk,
                        unroll=4)
    o_ref[...] = oT.T.astype(o_ref.dtype)


def _flash_attention(q, k, vt, heads, qk_w, tq=1024, tk=512):
    bsz, seq, _ = q.shape
    assert tq % (2 * tk) == 0 and seq % tq == 0
    return pl.pallas_call(
        functools.partial(_flash_kernel, tq=tq, tk=tk),
        grid=(bsz, heads, seq // tq),
        in_specs=[pl.BlockSpec((None, tq, qk_w), lambda b, h, i: (b, i, h)),
                  pl.BlockSpec((None, seq, qk_w), lambda b, h, i: (b, 0, h)),
                  pl.BlockSpec((None, None, VT_ROWS, seq), lambda b, h, i: (b, h, 0, 0))],
        out_specs=pl.BlockSpec((None, tq, V_DIM), lambda b, h, i: (b, i, h)),
        out_shape=jax.ShapeDtypeStruct((bsz, seq, heads * V_DIM), BF16),
        scratch_shapes=_attend_scratch(tq, tk),
        compiler_params=_cparams("parallel", "parallel", "arbitrary"),
        name="mla_flash",
    )(q, k, vt)


def _kmean_kernel(k_ref, o_ref):
    k = k_ref[...].astype(F32)
    o_ref[...] = jnp.mean(k.reshape(SUBLANES, MOBA_BLOCK, k.shape[-1]), axis=1)


def _kmean(qk):
    bsz, seq, _ = qk.shape
    rows = SUBLANES * MOBA_BLOCK
    return pl.pallas_call(
        _kmean_kernel,
        grid=(bsz, seq // rows),
        in_specs=[pl.BlockSpec((None, rows, A_W), lambda b, i: (b, i, KA_BLK * LANES // A_W))],
        out_specs=pl.BlockSpec((None, SUBLANES, A_W), lambda b, i: (b, i, 0)),
        out_shape=jax.ShapeDtypeStruct((bsz, seq // MOBA_BLOCK, A_W), F32),
        compiler_params=_cparams("parallel", "parallel"),
        name="moba_kmean",
    )(qk)


def _block_attention(q, k, v, visible=None):
    s = _dot_nt(q, k)
    if visible is not None:
        s = jnp.where(visible, s, MASKED)
    m = jnp.max(s, axis=-1, keepdims=True)
    p = jnp.exp2(s - m).astype(BF16)
    v_ones = jnp.concatenate([v, jnp.ones((v.shape[0], LANES), v.dtype)], axis=-1)
    acc = _dot(p, v_ones)
    den = acc[:, V_DIM:]
    return acc[:, :V_DIM] / den, m + jnp.log2(den)


def _moba_gate_kernel(q_ref, km_ref, ids_ref, cnt_ref, qf_ref):
    t = MOBA_BLOCK
    i = pl.program_id(1)
    nb = km_ref.shape[0]
    blk = lax.broadcasted_iota(jnp.int32, (nb, t), 0)
    neg_inf = jnp.float32(-jnp.inf)
    not_after = (lax.broadcasted_iota(jnp.int32, (t, t), 0)
                 <= lax.broadcasted_iota(jnp.int32, (t, t), 1))
    upper = jnp.where(not_after, 1.0, 0.0).astype(BF16)
    ones = jnp.ones((SUBLANES, t), BF16)
    for hd in range(A_HEADS):
        sl = slice(hd * HEAD_DIM, (hd + 1) * HEAD_DIM)
        q = q_ref[:, sl].astype(F32)
        qf_ref[hd] = q
        qT = q.T.astype(BF16)
        km = km_ref[:, sl]
        km_hi = km.astype(BF16)
        km_lo = (km - km_hi.astype(F32)).astype(BF16)
        g = jnp.where(blk < i, _dot(km_hi, qT) + _dot(km_lo, qT), neg_inf)
        picks, ranks, counts = [], [], []
        for _ in range(MOBA_TOPK):
            mx = jnp.max(g, axis=0, keepdims=True)
            is_max = (g == mx) & (mx > neg_inf)
            first = jnp.min(jnp.where(is_max, blk, nb), axis=0, keepdims=True)
            pick = blk == first
            g = jnp.where(pick, neg_inf, g)
            onehot = jnp.where(pick, 1.0, 0.0).astype(BF16)
            before = _dot(onehot, upper)
            rank = jnp.sum(jnp.where(pick, before - 1.0, 0.0), axis=0, keepdims=True)
            picks.append(first)
            ranks.append(rank.astype(jnp.int32))
            counts.append(_dot_nt(ones, onehot)[0:1, :])
        pad_i = jnp.zeros((SUBLANES - 2 * MOBA_TOPK, t), jnp.int32)
        ids_ref[hd] = jnp.concatenate(picks + ranks + [pad_i], axis=0)
        pad_f = jnp.zeros((SUBLANES - MOBA_TOPK, nb), F32)
        cnt_ref[hd] = jnp.concatenate(counts + [pad_f], axis=0)


def _moba_gate(qk, kmean):
    bsz, seq, _ = qk.shape
    t = MOBA_BLOCK
    nb = seq // t
    return pl.pallas_call(
        _moba_gate_kernel,
        grid=(bsz, nb),
        in_specs=[pl.BlockSpec((None, t, A_W), lambda b, i: (b, i, QA_BLK * LANES // A_W)),
                  pl.BlockSpec((None, nb, A_W), lambda b, i: (b, 0, 0))],
        out_specs=[pl.BlockSpec((None, A_HEADS, SUBLANES, t), lambda b, i: (b, 0, 0, i)),
                   pl.BlockSpec((None, A_HEADS, None, SUBLANES, nb), lambda b, i: (b, 0, i, 0, 0)),
                   pl.BlockSpec((None, A_HEADS, t, HEAD_DIM), lambda b, i: (b, 0, i, 0))],
        out_shape=[jax.ShapeDtypeStruct((bsz, A_HEADS, SUBLANES, seq), jnp.int32),
                   jax.ShapeDtypeStruct((bsz, A_HEADS, nb, SUBLANES, nb), F32),
                   jax.ShapeDtypeStruct((bsz, A_HEADS, seq, HEAD_DIM), F32)],
        compiler_params=_cparams("parallel", "parallel"),
        name="moba_gate",
    )(qk, kmean)


def _moba_routes(ids, cnt, seq):
    bsz, heads = ids.shape[:2]
    bh, t = bsz * heads, MOBA_BLOCK
    nb = seq // t
    tiles = _moba_tiles(seq)
    picks = ids[:, :, 0:MOBA_TOPK, :].reshape(bh, MOBA_TOPK, nb, t)
    ranks = ids[:, :, MOBA_TOPK:2 * MOBA_TOPK, :].reshape(bh, MOBA_TOPK, nb, t)
    per_tile = cnt[:, :, :, 0:MOBA_TOPK, :].astype(jnp.int32).reshape(bh, nb * MOBA_TOPK, nb)
    before = jnp.cumsum(per_tile, axis=1) - per_tile
    total = jnp.sum(per_tile, axis=1)
    padded = -(-total // t) * t
    ends = jnp.cumsum(padded, axis=1)
    base = before + (ends - padded)[:, None, :]
    base = base.reshape(bh, nb, MOBA_TOPK, nb).transpose(0, 2, 1, 3)
    onehot = picks[..., None] == jnp.arange(nb)
    pos = jnp.sum(jnp.where(onehot, base[:, :, :, None, :], 0), axis=-1) + ranks
    pos = jnp.where(picks < nb, pos, (tiles - 1) * t)
    pos = pos + (jnp.arange(bh, dtype=jnp.int32) * (tiles * t))[:, None, None, None]
    pos = pos.reshape(bh, MOBA_TOPK, seq).transpose(1, 0, 2).reshape(MOBA_TOPK, bh * seq)
    tile_start = jnp.arange(tiles, dtype=jnp.int32) * t
    tile_blk = jnp.sum(tile_start[None, :, None] >= ends[:, None, :], axis=-1)
    tile_blk = jnp.where(tile_start[None, :] < ends[:, -1:], tile_blk, -1)
    return pos.astype(jnp.int32), tile_blk.astype(jnp.int32)


def _moba_tiles(seq):
    nb = seq // MOBA_BLOCK
    return -(-(MOBA_TOPK * nb + nb + 1) // GROUP_STEP) * GROUP_STEP


def _sc_mesh():
    return plsc.VectorSubcoreMesh(core_axis_name="core", subcore_axis_name="subcore")


def _sc_scatter_rows(x, idx, rows):
    slots, n = idx.shape
    d = x.shape[1]

    @pl.kernel(out_type=jax.ShapeDtypeStruct((rows, d), x.dtype), mesh=_sc_mesh())
    def scatter(x_hbm, i_hbm, o_hbm):
        def body(x_vmem, i_vmem):
            pltpu.sync_copy(x_vmem, o_hbm.at[i_vmem.at[0]])

        pltpu.emit_pipeline(
            body, grid=(slots, n // SC_WINDOW),
            in_specs=[pl.BlockSpec((SC_WINDOW, d), lambda s, i: (i, 0)),
                      pl.BlockSpec((1, SC_WINDOW), lambda s, i: (s, i))],
            out_specs=[],
            core_axis_name=("core", "subcore"),
            dimension_semantics=(pltpu.PARALLEL, pltpu.PARALLEL),
        )(x_hbm, i_hbm)

    return scatter(x, idx)


def _sc_gather_rows(x, idx):
    n = idx.shape[0]
    d = x.shape[1]

    @pl.kernel(out_type=jax.ShapeDtypeStruct((n, d), x.dtype), mesh=_sc_mesh())
    def gather(x_hbm, i_hbm, o_hbm):
        def body(i_vmem, o_vmem):
            pltpu.sync_copy(x_hbm.at[i_vmem.at[0]], o_vmem)

        pltpu.emit_pipeline(
            body, grid=(n // SC_WINDOW,),
            in_specs=[pl.BlockSpec((1, SC_WINDOW), lambda i: (0, i))],
            out_specs=[pl.BlockSpec((SC_WINDOW, d), lambda i: (i, 0))],
            core_axis_name=("core", "subcore"),
            dimension_semantics=(pltpu.PARALLEL,),
        )(i_hbm, o_hbm)

    return gather(x, idx.reshape(1, n))


def _moba_group_kernel(tb_ref, q_ref, *refs):
    t = MOBA_BLOCK
    k_refs, v_refs = refs[:GROUP_STEP], refs[GROUP_STEP:2 * GROUP_STEP]
    o_ref, lse_ref = refs[-2:]
    g, step = pl.program_id(0), pl.program_id(1)
    first = step * GROUP_STEP

    @pl.when(tb_ref[g, first] < 0)
    def _():
        o_ref[...] = jnp.zeros(o_ref.shape, o_ref.dtype)
        lse_ref[...] = jnp.full(lse_ref.shape, MASKED, lse_ref.dtype)

    @pl.when(tb_ref[g, first] >= 0)
    def _():
        for u in range(GROUP_STEP):
            used = tb_ref[g, first + u] >= 0
            rows = slice(u * t, (u + 1) * t)
            o, lse = _block_attention(q_ref[rows, :].astype(BF16), k_refs[u][...], v_refs[u][...])
            o_ref[rows, :] = jnp.where(used, o, 0.0)
            lse_ref[rows, :] = jnp.where(used, lse, MASKED)


def _moba_group_attention(q_grouped, tile_blk, qk, v, after):
    bh, rows, _ = q_grouped.shape
    t = MOBA_BLOCK
    tiles = rows // t
    heads = A_HEADS

    def block_of(u, first_col):
        return lambda g, s, tb: (g // heads, jnp.maximum(tb[g, s * GROUP_STEP + u], 0),
                                 first_col + g % heads)

    row_tile = pl.BlockSpec((None, GROUP_STEP * t, HEAD_DIM), lambda g, s, tb: (g, s, 0))
    key_value = lambda first_col: [pl.BlockSpec((None, t, HEAD_DIM), block_of(u, first_col))
                                   for u in range(GROUP_STEP)]
    grid_spec = pltpu.PrefetchScalarGridSpec(
        num_scalar_prefetch=1,
        grid=(bh, tiles // GROUP_STEP),
        in_specs=([row_tile] + key_value(KA_BLK) + key_value(0)
                  + [pl.BlockSpec(memory_space=pl.ANY)]),
        out_specs=[row_tile, row_tile],
    )
    return pl.pallas_call(
        _moba_group_kernel,
        grid_spec=grid_spec,
        out_shape=[jax.ShapeDtypeStruct(q_grouped.shape, F32)] * 2,
        compiler_params=_cparams("parallel", "parallel"),
        name="moba_group",
    )(tile_blk, q_grouped, *([qk] * GROUP_STEP), *([v] * GROUP_STEP), after)


def _moba_merge_kernel(q_ref, k_ref, v_ref, po_ref, pl_ref, o_ref):
    t = MOBA_BLOCK
    causal = (lax.broadcasted_iota(jnp.int32, (t, t), 1)
              <= lax.broadcasted_iota(jnp.int32, (t, t), 0))
    for hd in range(A_HEADS):
        sl = slice(hd * HEAD_DIM, (hd + 1) * HEAD_DIM)
        o_own, lse_own = _block_attention(q_ref[:, sl], k_ref[:, sl], v_ref[:, sl], causal)
        outs = [o_own] + [po_ref[s, hd] for s in range(MOBA_TOPK)]
        lses = [lse_own] + [pl_ref[s, hd] for s in range(MOBA_TOPK)]
        top = functools.reduce(jnp.maximum, lses)
        weights = [jnp.exp2(l - top) for l in lses]
        num = sum(w * o for w, o in zip(weights, outs))
        o_ref[:, sl] = (num / sum(weights)).astype(o_ref.dtype)


def _moba_merge(qk, v, part_o, part_lse):
    bsz, seq, _ = qk.shape
    t = MOBA_BLOCK
    part = pl.BlockSpec((MOBA_TOPK, None, A_HEADS, t, HEAD_DIM), lambda b, i: (0, b, 0, i, 0))
    return pl.pallas_call(
        _moba_merge_kernel,
        grid=(bsz, seq // t),
        in_specs=[pl.BlockSpec((None, t, A_W), lambda b, i: (b, i, QA_BLK * LANES // A_W)),
                  pl.BlockSpec((None, t, A_W), lambda b, i: (b, i, KA_BLK * LANES // A_W)),
                  pl.BlockSpec((None, t, A_W), lambda b, i: (b, i, 0)),
                  part, part],
        out_specs=pl.BlockSpec((None, t, A_W), lambda b, i: (b, i, 0)),
        out_shape=jax.ShapeDtypeStruct((bsz, seq, A_W), BF16),
        compiler_params=_cparams("parallel", "parallel"),
        name="moba_merge",
    )(qk, qk, v, part_o, part_lse)


def _moba_regroup(qk, kmean):
    bsz, seq, _ = qk.shape
    bh = bsz * A_HEADS
    rows = _moba_tiles(seq) * MOBA_BLOCK
    ids, cnt, q_f32 = _moba_gate(qk, kmean)
    pos, tile_blk = _moba_routes(ids, cnt, seq)
    q_grouped = _sc_scatter_rows(q_f32.reshape(bh * seq, HEAD_DIM), pos, bh * rows)
    return q_grouped.reshape(bh, rows, HEAD_DIM), pos, tile_blk


def _moba_picked_blocks(q_grouped, pos, tile_blk, qk, v, after):
    bsz, seq, _ = qk.shape
    bh, rows, _ = q_grouped.shape
    o_g, lse_g = _moba_group_attention(q_grouped, tile_blk, qk, v, after)
    flat = pos.reshape(-1)
    back = lambda a: _sc_gather_rows(a.reshape(bh * rows, HEAD_DIM), flat).reshape(
        MOBA_TOPK, bsz, A_HEADS, seq, HEAD_DIM)
    return back(o_g), back(lse_g)


def _proj_dilated_kernel(h_ref, w_ref, c_ref, s_ref, q_ref, k_ref, v_ref, sc, *, d):
    acc = _dot(h_ref[...], w_ref[...])
    c, s = c_ref[...], s_ref[...]
    q_scale = HEAD_DIM ** -0.5 * LOG2E
    for j in range(acc.shape[1] // LANES):
        blk = acc[:, j * LANES:(j + 1) * LANES]
        if j < B_HEADS:
            blk = _rope128(blk, c, s) * q_scale
        elif j < 2 * B_HEADS:
            blk = _rope128(blk, c, s)
        sc[j] = blk
    rows = acc.shape[0] // d
    for r in range(d):
        for j in range(acc.shape[1] // LANES):
            dst = (q_ref, k_ref, v_ref)[j // B_HEADS]
            col = (j % B_HEADS) * LANES
            dst[r, :, col:col + LANES] = sc[j, pl.ds(r, rows, stride=d), :].astype(dst.dtype)


def _proj_dilated(h, w, rope_h, d, bsz, seq, tm=512):
    m, k = h.shape
    nt = seq // tm
    pos = pl.BlockSpec((tm, LANES), lambda i: (i % nt, 0))
    out = pl.BlockSpec((None, d, tm // d, B_W), lambda i: (i // nt, 0, i % nt, 0))
    return pl.pallas_call(
        functools.partial(_proj_dilated_kernel, d=d),
        grid=(m // tm,),
        in_specs=[pl.BlockSpec((tm, k), lambda i: (i, 0)), pl.BlockSpec(w.shape, lambda i: (0, 0)),
                  pos, pos],
        out_specs=[out] * 3,
        out_shape=[jax.ShapeDtypeStruct((bsz, d, seq // d, B_W), BF16)] * 3,
        scratch_shapes=[pltpu.VMEM((w.shape[1] // LANES, tm, LANES), F32)],
        compiler_params=_cparams("parallel"),
        name=f"proj_dilated_d{d}",
    )(h, w, *rope_h)


def _dilated_kernel(q_ref, kc_ref, kp_ref, vc_ref, vp_ref, o_ref, lse_ref, *, span):
    t, tp = q_ref.shape[0], kp_ref.shape[0]
    i = pl.program_id(2)
    shape = (2 * tp, tp)
    key_row = lax.broadcasted_iota(jnp.int32, shape, 0)
    dist = lax.broadcasted_iota(jnp.int32, shape, 1) + tp - key_row
    visible = (dist >= 0) & (dist <= span)
    bias = jnp.where(visible, 0.0, MASKED)
    bias_first = jnp.where(visible & ((key_row >= tp) | (i > 0)), 0.0, MASKED)
    ones = jnp.ones((BF16_ROWS, tp + t), BF16)

    def transposed(x):
        return x.astype(F32).T.astype(BF16)

    for j in range(B_HEADS):
        sl = slice(j * LANES, (j + 1) * LANES)
        qT = transposed(q_ref[:, sl])
        k_all = jnp.concatenate([kp_ref[:, sl], kc_ref[:, sl]], axis=0)
        vt_all = jnp.concatenate([transposed(vp_ref[:, sl]), transposed(vc_ref[:, sl])], axis=1)
        vt_all = jnp.concatenate([vt_all, ones], axis=0)
        outs, lses = [], []
        for u in range(t // tp):
            window = slice(u * tp, (u + 2) * tp)
            s = _dot(k_all[window, :], qT[:, u * tp:(u + 1) * tp])
            s = s + (bias_first if u == 0 else bias)
            m = jnp.max(s, axis=0, keepdims=True)
            p = jnp.exp2(s - m).astype(BF16)
            acc = _dot(vt_all[:, window], p)
            den = acc[V_DIM:V_DIM + 1, :]
            outs.append(acc[:V_DIM, :] / den)
            lses.append(m + jnp.log2(den))
        o_ref[:, sl] = jnp.concatenate(outs, axis=1).T
        lse = jnp.concatenate(lses, axis=1)
        lse_ref[:, sl] = jnp.broadcast_to(lse, (LANES, t)).T


def _dilated_attention(q, k, v, span, t=1024):
    bsz, d, length, _ = q.shape
    t = min(t, length)
    tp = B_QBLOCK
    assert span <= tp and t % tp == 0
    cur = pl.BlockSpec((None, None, t, B_W), lambda b, r, i: (b, r, i, 0))
    prev = pl.BlockSpec((None, None, tp, B_W),
                        lambda b, r, i: (b, r, jnp.maximum(i * (t // tp) - 1, 0), 0))
    return pl.pallas_call(
        functools.partial(_dilated_kernel, span=span),
        grid=(bsz, d, length // t),
        in_specs=[cur, cur, prev, cur, prev],
        out_specs=[cur, cur],
        out_shape=[jax.ShapeDtypeStruct(q.shape, F32)] * 2,
        compiler_params=_cparams("parallel", "parallel", "parallel"),
        name=f"dilated_d{d}",
    )(q, k, k, v, v)


def _natural_rows(ref, sc):
    d, rows = ref.shape[0], ref.shape[1]
    if d == 1:
        return ref[0]
    for r in range(d):
        for j in range(B_HEADS):
            sc[j, pl.ds(r, rows, stride=d), :] = ref[r, :, j * LANES:(j + 1) * LANES]
    return jnp.concatenate([sc[j] for j in range(B_HEADS)], axis=-1)


def _mixer_tail_kernel(x_ref, oa_ref, o0_ref, o1_ref, o2_ref, l0_ref, l1_ref, l2_ref, oc_ref,
                       g_ref, wpa_ref, wpb_ref, wpc_ref, wo_ref, y_ref, *scratch):
    o0, o1, o2, l0, l1, l2 = [
        _natural_rows(ref, sc)
        for ref, sc in zip((o0_ref, o1_ref, o2_ref, l0_ref, l1_ref, l2_ref), scratch)]
    mx = jnp.maximum(jnp.maximum(l0, l1), l2)
    e0, e1, e2 = jnp.exp2(l0 - mx), jnp.exp2(l1 - mx), jnp.exp2(l2 - mx)
    ob = (e0 * o0 + e1 * o1 + e2 * o2) / (e0 + e1 + e2)
    pa = _dot(oa_ref[...], wpa_ref[...])
    pb = _dot(ob.astype(BF16), wpb_ref[...])
    pc = _dot(oc_ref[...], wpc_ref[...])
    d = D_MODEL
    merged = (g_ref[:, 0:d].astype(F32) * pa + g_ref[:, d:2 * d].astype(F32) * pb
              + g_ref[:, 2 * d:3 * d].astype(F32) * pc)
    y_ref[...] = x_ref[...] + _dot(merged.astype(BF16), wo_ref[...])


def _mixer_tail(x, out_a, o_groups, lse_groups, out_c, gates, w_pa, w_pb, w_pc, w_o, seq, tm=256):
    m, d = x.shape
    nt = seq // tm
    row = lambda width: pl.BlockSpec((tm, width), lambda i: (i, 0))
    residue = lambda g: pl.BlockSpec((None, g.shape[1], tm // g.shape[1], B_W),
                                     lambda i: (i // nt, 0, i % nt, 0))
    weights = [_resident(w) for w in (w_pa, w_pb, w_pc, w_o)]
    groups = list(o_groups) + list(lse_groups)
    return pl.pallas_call(
        _mixer_tail_kernel,
        grid=(m // tm,),
        in_specs=([row(d), row(A_W)] + [residue(g) for g in groups]
                  + [row(C_W), row(3 * d)] + weights),
        out_specs=row(d),
        out_shape=jax.ShapeDtypeStruct((m, d), F32),
        scratch_shapes=[pltpu.VMEM((B_HEADS, tm, LANES), F32) for _ in groups],
        compiler_params=_cparams("parallel"),
        name="mixer_tail",
    )(x, out_a, *groups, out_c, gates, w_pa, w_pb, w_pc, w_o)


def _mem_kv_kernel(mem_ref, g_ref, wk_ref, wv_ref, k_ref, v_ref):
    memn = _rms(mem_ref[...], g_ref[...]).astype(BF16)
    k_ref[...] = _dot(memn, wk_ref[...]).astype(k_ref.dtype)
    v_ref[...] = _dot(memn, wv_ref[...]).astype(v_ref.dtype)


def _mem_kv(mem, g, wk, wv):
    bsz, n, d = mem.shape
    out = pl.BlockSpec((None, n, X_W), lambda b: (b, 0, 0))
    return pl.pallas_call(
        _mem_kv_kernel,
        grid=(bsz,),
        in_specs=[pl.BlockSpec((None, n, d), lambda b: (b, 0, 0)),
                  pl.BlockSpec((1, d), lambda b: (0, 0)),
                  pl.BlockSpec(wk.shape, lambda b: (0, 0)),
                  pl.BlockSpec(wv.shape, lambda b: (0, 0))],
        out_specs=[out, out],
        out_shape=[jax.ShapeDtypeStruct((bsz, n, X_W), BF16)] * 2,
        compiler_params=_cparams("parallel"),
        name="mem_kv",
    )(mem, g.reshape(1, d), wk, wv)


def _mem_attn_kernel(x_ref, g_ref, wq_ref, k_ref, v_ref, wo_ref, y_ref):
    x = x_ref[...]
    h = _rms(x, g_ref[...]).astype(BF16)
    q = (_dot(h, wq_ref[...]) * HEAD_DIM ** -0.5).astype(BF16)
    heads = []
    for hd in range(X_HEADS):
        sl = slice(hd * HEAD_DIM, (hd + 1) * HEAD_DIM)
        s = _dot_nt(q[:, sl], k_ref[:, sl])
        p = jnp.exp(s - jnp.max(s, axis=-1, keepdims=True))
        o = _dot(p.astype(BF16), v_ref[:, sl]) / jnp.sum(p, axis=-1, keepdims=True)
        heads.append(o.astype(BF16))
    y_ref[...] = x + _dot(jnp.concatenate(heads, axis=-1), wo_ref[...])


def _mem_attention(x, g, wq, kmem, vmem, wo, seq, tm=512):
    m, d = x.shape
    nt = seq // tm
    n = kmem.shape[1]
    kv = pl.BlockSpec((None, n, X_W), lambda i: (i // nt, 0, 0))
    return pl.pallas_call(
        _mem_attn_kernel,
        grid=(m // tm,),
        in_specs=[pl.BlockSpec((tm, d), lambda i: (i, 0)),
                  pl.BlockSpec((1, d), lambda i: (0, 0)),
                  pl.BlockSpec(wq.shape, lambda i: (0, 0)), kv, kv,
                  pl.BlockSpec(wo.shape, lambda i: (0, 0))],
        out_specs=pl.BlockSpec((tm, d), lambda i: (i, 0)),
        out_shape=jax.ShapeDtypeStruct((m, d), F32),
        compiler_params=_cparams("parallel"),
        name="mem_attention",
    )(x, g.reshape(1, d), wq, kmem, vmem, wo)


def _ffn_up_kernel(x_ref, halo_ref, g_ref, wg_ref, wv_ref, cwg_ref, cwv_ref, cbg_ref, cbv_ref,
                   act_ref, h_sc, *, tiles_per_seq):
    i = pl.program_id(0)
    tm = x_ref.shape[0]

    @pl.when(pl.program_id(1) == 0)
    def _():
        g = g_ref[...]
        keep = (i % tiles_per_seq != 0).astype(F32)
        h_sc[0:HALO, :] = (_rms(halo_ref[...], g) * keep).astype(h_sc.dtype)
        h_sc[HALO:, :] = _rms(x_ref[...], g).astype(h_sc.dtype)

    h = h_sc[...]

    def conv(w_ref, cw_ref, cb_ref):
        u = _dot(h, w_ref[...])
        c = cb_ref[...]
        for tap in range(CONV_W):
            lo = HALO - (CONV_W - 1) + tap
            c = c + cw_ref[tap:tap + 1, :] * u[lo:lo + tm, :]
        return c

    act = jax.nn.silu(conv(wg_ref, cwg_ref, cbg_ref)) * conv(wv_ref, cwv_ref, cbv_ref)
    act_ref[...] = act.astype(act_ref.dtype)


def _ffn_down_kernel(a_ref, w_ref, x_ref, y_ref):
    y_ref[...] = x_ref[...] + _dot(a_ref[...], w_ref[...])


def _conv_ffn(x, g, w_up, conv_w, conv_b, w_down, layer, seq):
    m, d = x.shape
    act = _ffn_up(x, g, w_up, conv_w, conv_b, layer, seq)
    tm, tn = 1024, FFN_TF
    return pl.pallas_call(
        _ffn_down_kernel,
        grid=(m // tm, d // tn),
        in_specs=[pl.BlockSpec((tm, D_FF), lambda i, j: (i, 0)),
                  pl.BlockSpec((None, D_FF, tn), lambda i, j: (layer, 0, j)),
                  pl.BlockSpec((tm, tn), lambda i, j: (i, j))],
        out_specs=pl.BlockSpec((tm, tn), lambda i, j: (i, j)),
        out_shape=jax.ShapeDtypeStruct((m, d), F32),
        compiler_params=_cparams("parallel", "parallel"),
        name="ffn_down",
    )(act, w_down, x)


def _ffn_up(x, g, w_up, conv_w, conv_b, layer, seq, tm=1024, tf=FFN_TF):
    m, d = x.shape
    nf = D_FF_PAD // tf
    halo_blocks = tm // HALO
    return pl.pallas_call(
        functools.partial(_ffn_up_kernel, tiles_per_seq=seq // tm),
        grid=(m // tm, nf),
        in_specs=[pl.BlockSpec((tm, d), lambda i, f: (i, 0)),
                  pl.BlockSpec((HALO, d), lambda i, f: (jnp.maximum(i * halo_blocks - 1, 0), 0)),
                  pl.BlockSpec((1, d), lambda i, f: (0, 0)),
                  pl.BlockSpec((None, d, tf), lambda i, f: (layer, 0, f)),
                  pl.BlockSpec((None, d, tf), lambda i, f: (layer, 0, f + nf)),
                  pl.BlockSpec((CONV_W, tf), lambda i, f: (0, f)),
                  pl.BlockSpec((CONV_W, tf), lambda i, f: (0, f + nf)),
                  pl.BlockSpec((1, tf), lambda i, f: (0, f)),
                  pl.BlockSpec((1, tf), lambda i, f: (0, f + nf))],
        out_specs=pl.BlockSpec((tm, tf), lambda i, f: (i, f)),
        out_shape=jax.ShapeDtypeStruct((m, D_FF_PAD), BF16),
        scratch_shapes=[pltpu.VMEM((HALO + tm, d), BF16)],
        compiler_params=_cparams("parallel", "arbitrary"),
        name="ffn_up",
    )(x, x, g.reshape(1, d), w_up, w_up, conv_w, conv_w, conv_b, conv_b)


def _rope_tables(seq):
    def angles(dim):
        inv_freq = jnp.exp(jnp.arange(0, dim, 2, dtype=F32) * (-math.log(ROPE_THETA) / dim))
        ang = jnp.arange(seq, dtype=F32)[:, None] * inv_freq[None, :]
        return jnp.cos(ang), jnp.sin(ang)

    cos_h, sin_h = angles(HEAD_DIM)
    rope_h = (jnp.concatenate([cos_h, cos_h], axis=-1), jnp.concatenate([-sin_h, sin_h], axis=-1))
    cos_r, sin_r = angles(ROPE_DIM)
    z = jnp.zeros_like(cos_r)
    rope_r = (jnp.concatenate([cos_r, cos_r, z, z], axis=-1),
              jnp.concatenate([-sin_r, z, z, z], axis=-1),
              jnp.concatenate([z, sin_r, z, z], axis=-1))
    return rope_h, rope_r


def _split_in(w_in):
    return [w_in[:, IN_OFFSETS[k]:IN_OFFSETS[k + 1]] for k in range(len(IN_WIDTHS))]


def _pad_cols(w, width):
    return jnp.pad(w, ((0, 0), (0, width - w.shape[1])))


def _layer_params(w_in, w_uq, w_ukv, conv_w, conv_b):
    qa, ka, va, qb, kb, vb, cq, ckv, kr, gates = _split_in(w_in)
    w_qk = jnp.concatenate([qa, ka], axis=1).astype(BF16)
    group_cols = lambda w, g: w[:, g * B_W:(g + 1) * B_W]
    w_b = [jnp.concatenate([group_cols(qb, g), group_cols(kb, g), group_cols(vb, g)],
                           axis=1).astype(BF16) for g in range(len(B_GROUPS))]
    w_down_in = jnp.concatenate([cq, ckv, _pad_cols(kr, LANES)], axis=1).astype(BF16)
    uq = w_uq.reshape(Q_LORA, C_HEADS, NOPE_DIM + ROPE_DIM)
    uq = jnp.pad(uq, ((0, 0), (0, 0), (0, C_QK - NOPE_DIM - ROPE_DIM)))
    ukv = w_ukv.reshape(KV_LORA, C_HEADS, NOPE_DIM + V_DIM)
    return dict(
        w_qk=w_qk, w_va=va.astype(BF16), w_b=w_b, w_gates=gates.astype(BF16),
        w_down_in=w_down_in,
        w_uq=uq.reshape(Q_LORA, C_HEADS * C_QK).astype(BF16),
        w_uk=ukv[:, :, :NOPE_DIM].reshape(KV_LORA, C_HEADS * NOPE_DIM).astype(BF16),
        w_uv=ukv[:, :, NOPE_DIM:].reshape(KV_LORA, C_W).astype(BF16),
        conv_w=_pad_ff_halves(conv_w),
        conv_b=_pad_ff_halves(conv_b.reshape(1, -1)),
    )


def _pad_ff_halves(w):
    pad = [(0, 0)] * (w.ndim - 1) + [(0, D_FF_PAD - D_FF)]
    return jnp.concatenate([jnp.pad(w[..., :D_FF], pad), jnp.pad(w[..., D_FF:], pad)], axis=-1)


def _qk_col_scale():
    q_scale = HEAD_DIM ** -0.5
    parts = [jnp.full((A_W,), q_scale * LOG2E, F32), jnp.ones((A_W,), F32)]
    return jnp.concatenate(parts).reshape(1, QK_W)


def _mixer(x, g_mix, p, g_cq, g_ckv, w_pa, w_pb, w_pc, w_o, rope_h, rope_r, bsz, seq):
    m = x.shape[0]
    h = _rmsnorm(x, g_mix, BF16)
    qk = _matmul(h, p["w_qk"], _mm_rope_kernel, BF16, 1024, 1024, seq=seq,
                 extras=(("col", _qk_col_scale()), ("pos", rope_h[0]), ("pos", rope_h[1])),
                 name="proj_qk_rope")
    qk3 = qk.reshape(bsz, seq, QK_W)
    q_grouped, pos, tile_blk = _moba_regroup(qk3, _kmean(qk3))
    v_a = _matmul(h, p["w_va"], _mm_plain_kernel, BF16, 1024, A_W, name="proj_va")
    gates = _matmul(h, p["w_gates"], _mm_sigmoid_kernel, BF16, 1024, 1024, name="proj_gates")
    v_a3 = v_a.reshape(bsz, seq, A_W)
    part_o, part_lse = _moba_picked_blocks(q_grouped, pos, tile_blk, qk3, v_a3, after=gates)
    cq, ckv, kr = _mla_down(h, p["w_down_in"], g_cq, g_ckv, rope_r, seq)
    q_c = _mla_q(cq, p["w_uq"], rope_r, seq)
    k_c, vt_c = _mla_kv(ckv, kr, p["w_uk"], p["w_uv"], bsz, seq)
    groups = []
    for (window, d), w_g in zip(B_GROUPS, p["w_b"]):
        q_g, k_g, v_g = _proj_dilated(h, w_g, rope_h, d, bsz, seq)
        groups.append(_dilated_attention(q_g, k_g, v_g, window // d))
    out_c = _flash_attention(q_c.reshape(bsz, seq, -1), k_c.reshape(bsz, seq, -1), vt_c,
                             C_HEADS, C_QK).reshape(m, C_W)
    out_a = _moba_merge(qk3, v_a3, part_o, part_lse).reshape(m, A_W)
    return _mixer_tail(x, out_a, [g[0] for g in groups], [g[1] for g in groups], out_c, gates,
                       w_pa.astype(BF16), w_pb.astype(BF16), w_pc.astype(BF16), w_o.astype(BF16),
                       seq)


def kernel(x, mem, g_mix, w_in, g_cq, g_ckv, w_uq, w_ukv, w_pa, w_pb, w_pc, w_o, g_mem, g_memkv,
           w_xq, w_xk, w_xv, w_xo, g_ffn, w_up, conv_w, conv_b, w_down, g_final):
    bsz, seq, d = x.shape
    rope_h, rope_r = _rope_tables(seq)
    xf = x.reshape(bsz * seq, d)
    w_down = w_down.astype(BF16)
    w_up = _pad_ff_halves(w_up.astype(BF16))
    for l in range(DEPTH):
        p = _layer_params(w_in[l], w_uq[l], w_ukv[l], conv_w[l], conv_b[l])
        xf = _mixer(xf, g_mix[l], p, g_cq[l], g_ckv[l], w_pa[l], w_pb[l], w_pc[l], w_o[l],
                    rope_h, rope_r, bsz, seq)
        kmem, vmem = _mem_kv(mem, g_memkv[l], w_xk[l].astype(BF16), w_xv[l].astype(BF16))
        xf = _mem_attention(xf, g_mem[l], w_xq[l].astype(BF16), kmem, vmem,
                            w_xo[l].astype(BF16), seq)
        xf = _conv_ffn(xf, g_ffn[l], w_up, p["conv_w"], p["conv_b"], w_down, l, seq)
    return _rmsnorm(xf, g_final, F32).reshape(bsz, seq, d)
```

```python
import functools
import math

import jax
import jax.numpy as jnp
import numpy as np
from jax import lax
from jax.experimental import pallas as pl
from jax.experimental.pallas import tpu as pltpu
from jax.experimental.pallas import tpu_sc as plsc

F32 = jnp.float32
BF16 = jnp.bfloat16

LANES = 128
SUBLANES = 8
V7X_VMEM_BYTES = 64 * 1024 * 1024
VMEM_LIMIT = V7X_VMEM_BYTES * 7 // 8

D_MODEL = 2048
DEPTH = 2
HEAD_DIM = 128
ROPE_THETA = 10000.0
EPS = 1e-6

A_HEADS = 4
MOBA_BLOCK = 256
MOBA_TOPK = 3

B_GROUPS = ((128, 1), (512, 4), (2048, 16))
B_HEADS = 4
B_QBLOCK = 128

C_HEADS = 8
Q_LORA = 1536
KV_LORA = 512
NOPE_DIM = 128
ROPE_DIM = 64
V_DIM = 128

X_HEADS = 4
D_FF = 5504
CONV_W = 3

A_W = A_HEADS * HEAD_DIM
B_QKV_W = len(B_GROUPS) * B_HEADS * HEAD_DIM
B_W = B_HEADS * HEAD_DIM
C_W = C_HEADS * V_DIM
X_W = X_HEADS * HEAD_DIM
IN_WIDTHS = (A_W, A_W, A_W, B_QKV_W, B_QKV_W, B_QKV_W, Q_LORA, KV_LORA, ROPE_DIM, 3 * D_MODEL)
IN_OFFSETS = tuple(int(o) for o in np.cumsum((0,) + IN_WIDTHS))

QK_W = 2 * A_W
QA_BLK, KA_BLK = 0, A_W // LANES

C_QK = 2 * LANES
MASKED = -1e30
LOG2E = math.log2(math.e)
BF16_ROWS = 16
VT_ROWS = V_DIM + BF16_ROWS
GROUP_STEP = 16
SC_WINDOW = 128

FFN_TF = 512
D_FF_PAD = -(-D_FF // FFN_TF) * FFN_TF
HALO = SUBLANES


def _cparams(*sem):
    return pltpu.CompilerParams(dimension_semantics=sem, vmem_limit_bytes=VMEM_LIMIT)


def _resident(arr):
    zeros = (0,) * arr.ndim
    return pl.BlockSpec(arr.shape, lambda *_: zeros, pipeline_mode=pl.Buffered(1))


def _dot(a, b):
    return jnp.dot(a, b, preferred_element_type=F32)


def _dot_nt(a, b):
    return lax.dot_general(a, b, (((1,), (1,)), ((), ())), preferred_element_type=F32)


def _rms(x, g):
    return x * lax.rsqrt(jnp.mean(x * x, axis=-1, keepdims=True) + EPS) * g


def _rmsnorm_kernel(x_ref, g_ref, o_ref):
    o_ref[...] = _rms(x_ref[...], g_ref[...]).astype(o_ref.dtype)


def _rmsnorm(x, g, out_dtype, tm=512):
    m, d = x.shape
    return pl.pallas_call(
        _rmsnorm_kernel,
        grid=(m // tm,),
        in_specs=[pl.BlockSpec((tm, d), lambda i: (i, 0)),
                  pl.BlockSpec((1, d), lambda i: (0, 0))],
        out_specs=pl.BlockSpec((tm, d), lambda i: (i, 0)),
        out_shape=jax.ShapeDtypeStruct((m, d), out_dtype),
        compiler_params=_cparams("parallel"),
        name="rmsnorm",
    )(x, g.reshape(1, d))


def _rope128(x, c, s):
    return x * c + pltpu.roll(x, HEAD_DIM // 2, 1) * s


def _rope64(x, c, sa, sb):
    half = ROPE_DIM // 2
    return x * c + pltpu.roll(x, LANES - half, 1) * sa + pltpu.roll(x, half, 1) * sb


def _mm_plain_kernel(a_ref, w_ref, o_ref):
    o_ref[...] = _dot(a_ref[...], w_ref[...]).astype(o_ref.dtype)


def _mm_sigmoid_kernel(a_ref, w_ref, o_ref):
    o_ref[...] = jax.nn.sigmoid(_dot(a_ref[...], w_ref[...])).astype(o_ref.dtype)


def _mm_rope_kernel(a_ref, w_ref, cs_ref, c_ref, s_ref, o_ref):
    acc = _dot(a_ref[...], w_ref[...])
    c = c_ref[...]
    s = s_ref[...]
    for j in range(acc.shape[1] // LANES):
        sl = slice(j * LANES, (j + 1) * LANES)
        o_ref[:, sl] = (_rope128(acc[:, sl], c, s) * cs_ref[:, sl]).astype(o_ref.dtype)


def _matmul(a, w, kernel, out_dtype, tm, tn, seq=None, extras=(), name="matmul"):
    m, k = a.shape
    n = w.shape[1]
    in_specs = [pl.BlockSpec((tm, k), lambda i, j: (i, 0)),
                pl.BlockSpec((k, tn), lambda i, j: (0, j))]
    args = [a, w]
    for kind, arr in extras:
        if kind == "col":
            in_specs.append(pl.BlockSpec((1, tn), lambda i, j: (0, j)))
        else:
            nt = seq // tm
            in_specs.append(pl.BlockSpec((tm, LANES), lambda i, j: (i % nt, 0)))
        args.append(arr)
    return pl.pallas_call(
        kernel,
        grid=(m // tm, n // tn),
        in_specs=in_specs,
        out_specs=pl.BlockSpec((tm, tn), lambda i, j: (i, j)),
        out_shape=jax.ShapeDtypeStruct((m, n), out_dtype),
        compiler_params=_cparams("parallel", "parallel"),
        name=name,
    )(*args)


def _mla_down_kernel(h_ref, w_ref, gq_ref, gkv_ref, c_ref, sa_ref, sb_ref,
                     cq_ref, ckv_ref, kr_ref):
    acc = _dot(h_ref[...], w_ref[...])
    cq_ref[...] = _rms(acc[:, :Q_LORA], gq_ref[...]).astype(cq_ref.dtype)
    ckv_ref[...] = _rms(acc[:, Q_LORA:Q_LORA + KV_LORA], gkv_ref[...]).astype(ckv_ref.dtype)
    kr = acc[:, Q_LORA + KV_LORA:]
    kr_ref[...] = _rope64(kr, c_ref[...], sa_ref[...], sb_ref[...]).astype(kr_ref.dtype)


def _mla_down(h, w, g_cq, g_ckv, rope_r, seq, tm=512):
    m, k = h.shape
    n = w.shape[1]
    nt = seq // tm
    row = lambda width: pl.BlockSpec((tm, width), lambda i: (i, 0))
    full = lambda r, c: pl.BlockSpec((r, c), lambda i: (0, 0))
    pos = pl.BlockSpec((tm, LANES), lambda i: (i % nt, 0))
    return pl.pallas_call(
        _mla_down_kernel,
        grid=(m // tm,),
        in_specs=[row(k), full(k, n), full(1, Q_LORA), full(1, KV_LORA), pos, pos, pos],
        out_specs=[row(Q_LORA), row(KV_LORA), row(LANES)],
        out_shape=[jax.ShapeDtypeStruct((m, Q_LORA), BF16),
                   jax.ShapeDtypeStruct((m, KV_LORA), BF16),
                   jax.ShapeDtypeStruct((m, LANES), BF16)],
        compiler_params=_cparams("parallel"),
        name="mla_down",
    )(h, w, g_cq.reshape(1, -1), g_ckv.reshape(1, -1), *rope_r)


def _mla_q_kernel(cq_ref, w_ref, c_ref, sa_ref, sb_ref, q_ref, *, scale):
    acc = _dot(cq_ref[...], w_ref[...])
    c, sa, sb = c_ref[...], sa_ref[...], sb_ref[...]
    for hd in range(C_HEADS):
        lo = hd * C_QK
        q_ref[:, lo:lo + LANES] = (acc[:, lo:lo + LANES] * scale).astype(q_ref.dtype)
        rope = _rope64(acc[:, lo + LANES:lo + C_QK], c, sa, sb)
        q_ref[:, lo + LANES:lo + C_QK] = (rope * scale).astype(q_ref.dtype)


def _mla_q(cq, w, rope_r, seq, tm=512):
    m, k = cq.shape
    n = w.shape[1]
    nt = seq // tm
    pos = pl.BlockSpec((tm, LANES), lambda i: (i % nt, 0))
    return pl.pallas_call(
        functools.partial(_mla_q_kernel, scale=(NOPE_DIM + ROPE_DIM) ** -0.5 * LOG2E),
        grid=(m // tm,),
        in_specs=[pl.BlockSpec((tm, k), lambda i: (i, 0)),
                  pl.BlockSpec((k, n), lambda i: (0, 0)), pos, pos, pos],
        out_specs=pl.BlockSpec((tm, n), lambda i: (i, 0)),
        out_shape=jax.ShapeDtypeStruct((m, n), BF16),
        compiler_params=_cparams("parallel"),
        name="mla_q",
    )(cq, w, *rope_r)


def _store_vt(v, vt_ref):
    vt = v.T
    for hd in range(vt_ref.shape[0]):
        vt_ref[hd, 0:V_DIM, :] = vt[hd * V_DIM:(hd + 1) * V_DIM, :].astype(vt_ref.dtype)
        vt_ref[hd, V_DIM:VT_ROWS, :] = jnp.ones((VT_ROWS - V_DIM, vt.shape[1]), vt_ref.dtype)


def _mla_kv_kernel(ckv_ref, kr_ref, wk_ref, wv_ref, k_ref, vt_ref):
    ckv = ckv_ref[...]
    kn = _dot(ckv, wk_ref[...])
    kr = kr_ref[...]
    for hd in range(C_HEADS):
        lo = hd * C_QK
        k_ref[:, lo:lo + LANES] = kn[:, hd * LANES:(hd + 1) * LANES].astype(k_ref.dtype)
        k_ref[:, lo + LANES:lo + C_QK] = kr
    _store_vt(_dot(ckv, wv_ref[...]), vt_ref)


def _mla_kv(ckv, kr, wk, wv, bsz, seq, tm=512):
    m, k = ckv.shape
    nt = seq // tm
    return pl.pallas_call(
        _mla_kv_kernel,
        grid=(m // tm,),
        in_specs=[pl.BlockSpec((tm, k), lambda i: (i, 0)),
                  pl.BlockSpec((tm, LANES), lambda i: (i, 0)),
                  pl.BlockSpec(wk.shape, lambda i: (0, 0)),
                  pl.BlockSpec(wv.shape, lambda i: (0, 0))],
        out_specs=[pl.BlockSpec((tm, C_HEADS * C_QK), lambda i: (i, 0)),
                   pl.BlockSpec((None, C_HEADS, VT_ROWS, tm), lambda i: (i // nt, 0, 0, i % nt))],
        out_shape=[jax.ShapeDtypeStruct((m, C_HEADS * C_QK), BF16),
                   jax.ShapeDtypeStruct((bsz, C_HEADS, VT_ROWS, seq), BF16)],
        compiler_params=_cparams("parallel"),
        name="mla_kv",
    )(ckv, kr, wk, wv)


def _attend_chunks(qT, k_ref, vt_ref, scratch, *, tk, n_full, mask_tail, tail_steps, tail_col,
                   unroll):
    m_sc, acc_sc, sa_sc, sb_sc, pa_sc, pb_sc, ala_sc, alb_sc, mxa_sc, mxb_sc = scratch
    s_bufs = (sa_sc, sb_sc)
    p_bufs = (pa_sc, pb_sc)
    al_bufs = (ala_sc, alb_sc)
    mx_bufs = (mxa_sc, mxb_sc)
    last_chunk = k_ref.shape[0] // tk - 1
    m_sc[...] = jnp.full(m_sc.shape, MASKED, F32)
    acc_sc[...] = jnp.zeros(acc_sc.shape, F32)
    for p_ref, al_ref in zip(p_bufs, al_bufs):
        p_ref[...] = jnp.zeros(p_ref.shape, p_ref.dtype)
        al_ref[...] = jnp.ones(al_ref.shape, F32)

    def rows(c):
        return pl.ds(pl.multiple_of(jnp.clip(c, 0, last_chunk) * tk, tk), tk)

    def scores(c, slot, col=0):
        sT = _dot(k_ref[rows(c), :], qT[:, col:])
        s_bufs[slot][:, col:] = sT
        mx_bufs[slot][:, col:] = jnp.max(sT, axis=0, keepdims=True)

    def flush(c, slot, col=0):
        acc_sc[:, col:] = (al_bufs[slot][:, col:] * acc_sc[:, col:]
                           + _dot(vt_ref[:, rows(c)], p_bufs[slot][:, col:]))

    def softmax(sT, top, slot, col):
        m_old = m_sc[:, col:]
        m_new = jnp.maximum(m_old, top)
        al_bufs[slot][:, col:] = jnp.exp2(m_old - m_new)
        p_bufs[slot][:, col:] = jnp.exp2(sT - m_new).astype(p_bufs[slot].dtype)
        m_sc[:, col:] = m_new

    def step(tau, slot, mask, cols):
        col_flush, col, col_next = cols
        flush(tau - 2, slot, col_flush)
        if col_next is not None:
            scores(tau + 1, 1 - slot, col_next)
        sT = s_bufs[slot][:, col:]
        if mask is None:
            softmax(sT, mx_bufs[slot][:, col:], slot, col)
        else:
            sT = mask(sT, tau, col)
            softmax(sT, jnp.max(sT, axis=0, keepdims=True), slot, col)

    def full_steps(tau0, count):
        for j in range(count):
            step(tau0 + j, j % 2, None, (0, 0, 0))

    scores(0, 0)
    trips = n_full // unroll
    lax.fori_loop(0, trips, lambda u, c: (full_steps(unroll * u, unroll), c)[1], 0)
    done = unroll * trips
    pairs = (n_full - done) // 2
    lax.fori_loop(0, pairs, lambda u, c: (full_steps(done + 2 * u, 2), c)[1], 0)
    tau = done + 2 * pairs
    cols = [tail_col(j) for j in range(tail_steps)]
    for j in range(tail_steps):
        col_flush = cols[j - 2] if j >= 2 else 0
        col_next = cols[j + 1] if j + 1 < tail_steps else None
        step(tau + j, j % 2, mask_tail, (col_flush, cols[j], col_next))
    flush(tau + tail_steps - 2, 0, cols[-2])
    flush(tau + tail_steps - 1, 1, cols[-1])
    acc = acc_sc[...]
    return acc[:V_DIM, :] / acc[V_DIM:V_DIM + 1, :]


def _attend_scratch(tq, tk):
    return [pltpu.VMEM((1, tq), F32), pltpu.VMEM((VT_ROWS, tq), F32),
            pltpu.VMEM((tk, tq), F32), pltpu.VMEM((tk, tq), F32),
            pltpu.VMEM((tk, tq), BF16), pltpu.VMEM((tk, tq), BF16),
            pltpu.VMEM((1, tq), F32), pltpu.VMEM((1, tq), F32),
            pltpu.VMEM((1, tq), F32), pltpu.VMEM((1, tq), F32)]


def _transpose_q(q_ref):
    return q_ref[...].astype(F32).T.astype(BF16)


def _flash_kernel(q_ref, k_ref, vt_ref, o_ref, *scratch, tq, tk):
    i = pl.program_id(2)

    def causal(sT, c, col):
        key = lax.broadcasted_iota(jnp.int32, sT.shape, 0) + c * tk
        qry = lax.broadcasted_iota(jnp.int32, sT.shape, 1) + (i * tq + col)
        return jnp.where(key <= qry, sT, MASKED)

    per_tile = tq // tk
    oT = _attend_chunks(_transpose_q(q_ref), k_ref, vt_ref, scratch, tk=tk, n_full=i * per_tile,
                        mask_tail=causal, tail_steps=per_tile, tail_col=lambda j: j * tk,
                        unroll=2)
    o_ref[...] = oT.T.astype(o_ref.dtype)


def _flash_attention(q, k, vt, heads, qk_w, tq=2048, tk=1024):
    bsz, seq, _ = q.shape
    tq, tk = min(tq, seq), min(tk, seq // 2)
    assert tq % (2 * tk) == 0 and seq % tq == 0
    once = pl.Buffered(1)
    return pl.pallas_call(
        functools.partial(_flash_kernel, tq=tq, tk=tk),
        grid=(bsz, heads, seq // tq),
        in_specs=[pl.BlockSpec((None, tq, qk_w), lambda b, h, i: (b, i, h)),
                  pl.BlockSpec((None, seq, qk_w), lambda b, h, i: (b, 0, h), pipeline_mode=once),
                  pl.BlockSpec((None, None, VT_ROWS, seq), lambda b, h, i: (b, h, 0, 0),
                               pipeline_mode=once)],
        out_specs=pl.BlockSpec((None, tq, V_DIM), lambda b, h, i: (b, i, h)),
        out_shape=jax.ShapeDtypeStruct((bsz, seq, heads * V_DIM), BF16),
        scratch_shapes=_attend_scratch(tq, tk),
        compiler_params=_cparams("parallel", "parallel", "arbitrary"),
        name="mla_flash",
    )(q, k, vt)


def _kmean_kernel(k_ref, o_ref):
    k = k_ref[...].astype(F32)
    o_ref[...] = jnp.mean(k.reshape(SUBLANES, MOBA_BLOCK, k.shape[-1]), axis=1)


def _kmean(qk):
    bsz, seq, _ = qk.shape
    rows = SUBLANES * MOBA_BLOCK
    return pl.pallas_call(
        _kmean_kernel,
        grid=(bsz, seq // rows),
        in_specs=[pl.BlockSpec((None, rows, A_W), lambda b, i: (b, i, KA_BLK * LANES // A_W))],
        out_specs=pl.BlockSpec((None, SUBLANES, A_W), lambda b, i: (b, i, 0)),
        out_shape=jax.ShapeDtypeStruct((bsz, seq // MOBA_BLOCK, A_W), F32),
        compiler_params=_cparams("parallel", "parallel"),
        name="moba_kmean",
    )(qk)


def _block_attention(q, k, v, visible=None):
    s = _dot_nt(q, k)
    if visible is not None:
        s = jnp.where(visible, s, MASKED)
    m = jnp.max(s, axis=-1, keepdims=True)
    p = jnp.exp2(s - m).astype(BF16)
    v_ones = jnp.concatenate([v, jnp.ones((v.shape[0], LANES), v.dtype)], axis=-1)
    acc = _dot(p, v_ones)
    den = acc[:, V_DIM:]
    return acc[:, :V_DIM] / den, m + jnp.log2(den)


def _moba_gate_kernel(q_ref, km_ref, ids_ref, cnt_ref, qf_ref):
    t = MOBA_BLOCK
    i = pl.program_id(1)
    nb = km_ref.shape[0]
    blk = lax.broadcasted_iota(jnp.int32, (nb, t), 0)
    neg_inf = jnp.float32(-jnp.inf)
    not_after = (lax.broadcasted_iota(jnp.int32, (t, t), 0)
                 <= lax.broadcasted_iota(jnp.int32, (t, t), 1))
    upper = jnp.where(not_after, 1.0, 0.0).astype(BF16)
    ones = jnp.ones((SUBLANES, t), BF16)
    for hd in range(A_HEADS):
        sl = slice(hd * HEAD_DIM, (hd + 1) * HEAD_DIM)
        q = q_ref[:, sl].astype(F32)
        qf_ref[hd] = q
        qT = q.T.astype(BF16)
        km = km_ref[:, sl]
        km_hi = km.astype(BF16)
        km_lo = (km - km_hi.astype(F32)).astype(BF16)
        g = jnp.where(blk < i, _dot(km_hi, qT) + _dot(km_lo, qT), neg_inf)
        picks, ranks, counts = [], [], []
        for _ in range(MOBA_TOPK):
            mx = jnp.max(g, axis=0, keepdims=True)
            is_max = (g == mx) & (mx > neg_inf)
            first = jnp.min(jnp.where(is_max, blk, nb), axis=0, keepdims=True)
            pick = blk == first
            g = jnp.where(pick, neg_inf, g)
            onehot = jnp.where(pick, 1.0, 0.0).astype(BF16)
            before = _dot(onehot, upper)
            rank = jnp.sum(jnp.where(pick, before - 1.0, 0.0), axis=0, keepdims=True)
            picks.append(first)
            ranks.append(rank.astype(jnp.int32))
            counts.append(_dot_nt(ones, onehot)[0:1, :])
        pad_i = jnp.zeros((SUBLANES - 2 * MOBA_TOPK, t), jnp.int32)
        ids_ref[hd] = jnp.concatenate(picks + ranks + [pad_i], axis=0)
        pad_f = jnp.zeros((SUBLANES - MOBA_TOPK, nb), F32)
        cnt_ref[hd] = jnp.concatenate(counts + [pad_f], axis=0)


def _moba_gate(qk, kmean):
    bsz, seq, _ = qk.shape
    t = MOBA_BLOCK
    nb = seq // t
    return pl.pallas_call(
        _moba_gate_kernel,
        grid=(bsz, nb),
        in_specs=[pl.BlockSpec((None, t, A_W), lambda b, i: (b, i, QA_BLK * LANES // A_W)),
                  pl.BlockSpec((None, nb, A_W), lambda b, i: (b, 0, 0))],
        out_specs=[pl.BlockSpec((None, A_HEADS, SUBLANES, t), lambda b, i: (b, 0, 0, i)),
                   pl.BlockSpec((None, A_HEADS, None, SUBLANES, nb), lambda b, i: (b, 0, i, 0, 0)),
                   pl.BlockSpec((None, A_HEADS, t, HEAD_DIM), lambda b, i: (b, 0, i, 0))],
        out_shape=[jax.ShapeDtypeStruct((bsz, A_HEADS, SUBLANES, seq), jnp.int32),
                   jax.ShapeDtypeStruct((bsz, A_HEADS, nb, SUBLANES, nb), F32),
                   jax.ShapeDtypeStruct((bsz, A_HEADS, seq, HEAD_DIM), F32)],
        compiler_params=_cparams("parallel", "parallel"),
        name="moba_gate",
    )(qk, kmean)


def _moba_routes(ids, cnt, seq):
    bsz, heads = ids.shape[:2]
    bh, t = bsz * heads, MOBA_BLOCK
    nb = seq // t
    tiles = _moba_tiles(seq)
    picks = ids[:, :, 0:MOBA_TOPK, :].reshape(bh, MOBA_TOPK, nb, t)
    ranks = ids[:, :, MOBA_TOPK:2 * MOBA_TOPK, :].reshape(bh, MOBA_TOPK, nb, t)
    per_tile = cnt[:, :, :, 0:MOBA_TOPK, :].astype(jnp.int32).reshape(bh, nb * MOBA_TOPK, nb)
    before = jnp.cumsum(per_tile, axis=1) - per_tile
    total = jnp.sum(per_tile, axis=1)
    padded = -(-total // t) * t
    ends = jnp.cumsum(padded, axis=1)
    base = before + (ends - padded)[:, None, :]
    base = base.reshape(bh, nb, MOBA_TOPK, nb).transpose(0, 2, 1, 3)
    onehot = picks[..., None] == jnp.arange(nb)
    pos = jnp.sum(jnp.where(onehot, base[:, :, :, None, :], 0), axis=-1) + ranks
    pos = jnp.where(picks < nb, pos, (tiles - 1) * t)
    pos = pos + (jnp.arange(bh, dtype=jnp.int32) * (tiles * t))[:, None, None, None]
    pos = pos.reshape(bh, MOBA_TOPK, seq).transpose(1, 0, 2).reshape(MOBA_TOPK, bh * seq)
    tile_start = jnp.arange(tiles, dtype=jnp.int32) * t
    tile_blk = jnp.sum(tile_start[None, :, None] >= ends[:, None, :], axis=-1)
    tile_blk = jnp.where(tile_start[None, :] < ends[:, -1:], tile_blk, -1)
    return pos.astype(jnp.int32), tile_blk.astype(jnp.int32)


def _moba_tiles(seq):
    nb = seq // MOBA_BLOCK
    return -(-(MOBA_TOPK * nb + nb + 1) // GROUP_STEP) * GROUP_STEP


def _sc_mesh():
    return plsc.VectorSubcoreMesh(core_axis_name="core", subcore_axis_name="subcore")


def _sc_scatter_rows(x, idx, rows):
    slots, n = idx.shape
    d = x.shape[1]

    @pl.kernel(out_type=jax.ShapeDtypeStruct((rows, d), x.dtype), mesh=_sc_mesh())
    def scatter(x_hbm, i_hbm, o_hbm):
        def body(x_vmem, i_vmem):
            pltpu.sync_copy(x_vmem, o_hbm.at[i_vmem.at[0]])

        pltpu.emit_pipeline(
            body, grid=(slots, n // SC_WINDOW),
            in_specs=[pl.BlockSpec((SC_WINDOW, d), lambda s, i: (i, 0)),
                      pl.BlockSpec((1, SC_WINDOW), lambda s, i: (s, i))],
            out_specs=[],
            core_axis_name=("core", "subcore"),
            dimension_semantics=(pltpu.PARALLEL, pltpu.PARALLEL),
        )(x_hbm, i_hbm)

    return scatter(x, idx)


def _sc_gather_rows(x, idx):
    n = idx.shape[0]
    d = x.shape[1]

    @pl.kernel(out_type=jax.ShapeDtypeStruct((n, d), x.dtype), mesh=_sc_mesh())
    def gather(x_hbm, i_hbm, o_hbm):
        def body(i_vmem, o_vmem):
            pltpu.sync_copy(x_hbm.at[i_vmem.at[0]], o_vmem)

        pltpu.emit_pipeline(
            body, grid=(n // SC_WINDOW,),
            in_specs=[pl.BlockSpec((1, SC_WINDOW), lambda i: (0, i))],
            out_specs=[pl.BlockSpec((SC_WINDOW, d), lambda i: (i, 0))],
            core_axis_name=("core", "subcore"),
            dimension_semantics=(pltpu.PARALLEL,),
        )(i_hbm, o_hbm)

    return gather(x, idx.reshape(1, n))


def _moba_group_kernel(tb_ref, q_ref, *refs):
    t = MOBA_BLOCK
    k_refs, v_refs = refs[:GROUP_STEP], refs[GROUP_STEP:2 * GROUP_STEP]
    o_ref, lse_ref = refs[-2:]
    g, step = pl.program_id(0), pl.program_id(1)
    first = step * GROUP_STEP

    @pl.when(tb_ref[g, first] < 0)
    def _():
        o_ref[...] = jnp.zeros(o_ref.shape, o_ref.dtype)
        lse_ref[...] = jnp.full(lse_ref.shape, MASKED, lse_ref.dtype)

    @pl.when(tb_ref[g, first] >= 0)
    def _():
        for u in range(GROUP_STEP):
            used = tb_ref[g, first + u] >= 0
            rows = slice(u * t, (u + 1) * t)
            o, lse = _block_attention(q_ref[rows, :].astype(BF16), k_refs[u][...], v_refs[u][...])
            o_ref[rows, :] = jnp.where(used, o, 0.0)
            lse_ref[rows, :] = jnp.where(used, lse, MASKED)


def _moba_group_attention(q_grouped, tile_blk, qk, v, after):
    bh, rows, _ = q_grouped.shape
    t = MOBA_BLOCK
    tiles = rows // t
    heads = A_HEADS

    def block_of(u, first_col):
        return lambda g, s, tb: (g // heads, jnp.maximum(tb[g, s * GROUP_STEP + u], 0),
                                 first_col + g % heads)

    row_tile = pl.BlockSpec((None, GROUP_STEP * t, HEAD_DIM), lambda g, s, tb: (g, s, 0))
    key_value = lambda first_col: [pl.BlockSpec((None, t, HEAD_DIM), block_of(u, first_col))
                                   for u in range(GROUP_STEP)]
    grid_spec = pltpu.PrefetchScalarGridSpec(
        num_scalar_prefetch=1,
        grid=(bh, tiles // GROUP_STEP),
        in_specs=([row_tile] + key_value(KA_BLK) + key_value(0)
                  + [pl.BlockSpec(memory_space=pl.ANY)]),
        out_specs=[row_tile, row_tile],
    )
    return pl.pallas_call(
        _moba_group_kernel,
        grid_spec=grid_spec,
        out_shape=[jax.ShapeDtypeStruct(q_grouped.shape, F32)] * 2,
        compiler_params=_cparams("parallel", "parallel"),
        name="moba_group",
    )(tile_blk, q_grouped, *([qk] * GROUP_STEP), *([v] * GROUP_STEP), after)


def _moba_merge_kernel(q_ref, k_ref, v_ref, po_ref, pl_ref, o_ref):
    t = MOBA_BLOCK
    causal = (lax.broadcasted_iota(jnp.int32, (t, t), 1)
              <= lax.broadcasted_iota(jnp.int32, (t, t), 0))
    for hd in range(A_HEADS):
        sl = slice(hd * HEAD_DIM, (hd + 1) * HEAD_DIM)
        o_own, lse_own = _block_attention(q_ref[:, sl], k_ref[:, sl], v_ref[:, sl], causal)
        outs = [o_own] + [po_ref[s, hd] for s in range(MOBA_TOPK)]
        lses = [lse_own] + [pl_ref[s, hd] for s in range(MOBA_TOPK)]
        top = functools.reduce(jnp.maximum, lses)
        weights = [jnp.exp2(l - top) for l in lses]
        num = sum(w * o for w, o in zip(weights, outs))
        o_ref[:, sl] = (num / sum(weights)).astype(o_ref.dtype)


def _moba_merge(qk, v, part_o, part_lse):
    bsz, seq, _ = qk.shape
    t = MOBA_BLOCK
    part = pl.BlockSpec((MOBA_TOPK, None, A_HEADS, t, HEAD_DIM), lambda b, i: (0, b, 0, i, 0))
    return pl.pallas_call(
        _moba_merge_kernel,
        grid=(bsz, seq // t),
        in_specs=[pl.BlockSpec((None, t, A_W), lambda b, i: (b, i, QA_BLK * LANES // A_W)),
                  pl.BlockSpec((None, t, A_W), lambda b, i: (b, i, KA_BLK * LANES // A_W)),
                  pl.BlockSpec((None, t, A_W), lambda b, i: (b, i, 0)),
                  part, part],
        out_specs=pl.BlockSpec((None, t, A_W), lambda b, i: (b, i, 0)),
        out_shape=jax.ShapeDtypeStruct((bsz, seq, A_W), BF16),
        compiler_params=_cparams("parallel", "parallel"),
        name="moba_merge",
    )(qk, qk, v, part_o, part_lse)


def _moba_regroup(qk, kmean):
    bsz, seq, _ = qk.shape
    bh = bsz * A_HEADS
    rows = _moba_tiles(seq) * MOBA_BLOCK
    ids, cnt, q_f32 = _moba_gate(qk, kmean)
    pos, tile_blk = _moba_routes(ids, cnt, seq)
    q_grouped = _sc_scatter_rows(q_f32.reshape(bh * seq, HEAD_DIM), pos, bh * rows)
    return q_grouped.reshape(bh, rows, HEAD_DIM), pos, tile_blk


def _moba_picked_blocks(q_grouped, pos, tile_blk, qk, v, after):
    bsz, seq, _ = qk.shape
    bh, rows, _ = q_grouped.shape
    o_g, lse_g = _moba_group_attention(q_grouped, tile_blk, qk, v, after)
    flat = pos.reshape(-1)
    back = lambda a: _sc_gather_rows(a.reshape(bh * rows, HEAD_DIM), flat).reshape(
        MOBA_TOPK, bsz, A_HEADS, seq, HEAD_DIM)
    return back(o_g), back(lse_g)


def _proj_dilated_kernel(h_ref, w_ref, c_ref, s_ref, q_ref, k_ref, v_ref, sc, *, d):
    acc = _dot(h_ref[...], w_ref[...])
    c, s = c_ref[...], s_ref[...]
    q_scale = HEAD_DIM ** -0.5 * LOG2E
    for j in range(acc.shape[1] // LANES):
        blk = acc[:, j * LANES:(j + 1) * LANES]
        if j < B_HEADS:
            blk = _rope128(blk, c, s) * q_scale
        elif j < 2 * B_HEADS:
            blk = _rope128(blk, c, s)
        sc[j] = blk
    rows = acc.shape[0] // d
    for r in range(d):
        for j in range(acc.shape[1] // LANES):
            dst = (q_ref, k_ref, v_ref)[j // B_HEADS]
            col = (j % B_HEADS) * LANES
            dst[r, :, col:col + LANES] = sc[j, pl.ds(r, rows, stride=d), :].astype(dst.dtype)


def _proj_dilated(h, w, rope_h, d, bsz, seq, tm=512):
    m, k = h.shape
    nt = seq // tm
    pos = pl.BlockSpec((tm, LANES), lambda i: (i % nt, 0))
    out = pl.BlockSpec((None, d, tm // d, B_W), lambda i: (i // nt, 0, i % nt, 0))
    return pl.pallas_call(
        functools.partial(_proj_dilated_kernel, d=d),
        grid=(m // tm,),
        in_specs=[pl.BlockSpec((tm, k), lambda i: (i, 0)), pl.BlockSpec(w.shape, lambda i: (0, 0)),
                  pos, pos],
        out_specs=[out] * 3,
        out_shape=[jax.ShapeDtypeStruct((bsz, d, seq // d, B_W), BF16)] * 3,
        scratch_shapes=[pltpu.VMEM((w.shape[1] // LANES, tm, LANES), F32)],
        compiler_params=_cparams("parallel"),
        name=f"proj_dilated_d{d}",
    )(h, w, *rope_h)


def _dilated_kernel(q_ref, kc_ref, kp_ref, vc_ref, vp_ref, o_ref, lse_ref, *, span):
    t, tp = q_ref.shape[0], kp_ref.shape[0]
    i = pl.program_id(2)
    shape = (2 * tp, tp)
    key_row = lax.broadcasted_iota(jnp.int32, shape, 0)
    dist = lax.broadcasted_iota(jnp.int32, shape, 1) + tp - key_row
    visible = (dist >= 0) & (dist <= span)
    bias = jnp.where(visible, 0.0, MASKED)
    bias_first = jnp.where(visible & ((key_row >= tp) | (i > 0)), 0.0, MASKED)
    ones = jnp.ones((BF16_ROWS, tp + t), BF16)

    def transposed(x):
        return x.astype(F32).T.astype(BF16)

    for j in range(B_HEADS):
        sl = slice(j * LANES, (j + 1) * LANES)
        qT = transposed(q_ref[:, sl])
        k_all = jnp.concatenate([kp_ref[:, sl], kc_ref[:, sl]], axis=0)
        vt_all = jnp.concatenate([transposed(vp_ref[:, sl]), transposed(vc_ref[:, sl])], axis=1)
        vt_all = jnp.concatenate([vt_all, ones], axis=0)
        outs, lses = [], []
        for u in range(t // tp):
            window = slice(u * tp, (u + 2) * tp)
            s = _dot(k_all[window, :], qT[:, u * tp:(u + 1) * tp])
            s = s + (bias_first if u == 0 else bias)
            m = jnp.max(s, axis=0, keepdims=True)
            p = jnp.exp2(s - m).astype(BF16)
            acc = _dot(vt_all[:, window], p)
            den = acc[V_DIM:V_DIM + 1, :]
            outs.append(acc[:V_DIM, :] / den)
            lses.append(m + jnp.log2(den))
        o_ref[:, sl] = jnp.concatenate(outs, axis=1).T
        lse = jnp.concatenate(lses, axis=1)
        lse_ref[:, sl] = jnp.broadcast_to(lse, (LANES, t)).T


def _dilated_attention(q, k, v, span, t=1024):
    bsz, d, length, _ = q.shape
    t = min(t, length)
    tp = B_QBLOCK
    assert span <= tp and t % tp == 0
    cur = pl.BlockSpec((None, None, t, B_W), lambda b, r, i: (b, r, i, 0))
    prev = pl.BlockSpec((None, None, tp, B_W),
                        lambda b, r, i: (b, r, jnp.maximum(i * (t // tp) - 1, 0), 0))
    return pl.pallas_call(
        functools.partial(_dilated_kernel, span=span),
        grid=(bsz, d, length // t),
        in_specs=[cur, cur, prev, cur, prev],
        out_specs=[cur, cur],
        out_shape=[jax.ShapeDtypeStruct(q.shape, F32)] * 2,
        compiler_params=_cparams("parallel", "parallel", "parallel"),
        name=f"dilated_d{d}",
    )(q, k, k, v, v)


def _natural_rows(ref, sc):
    d, rows = ref.shape[0], ref.shape[1]
    if d == 1:
        return ref[0]
    for r in range(d):
        for j in range(B_HEADS):
            sc[j, pl.ds(r, rows, stride=d), :] = ref[r, :, j * LANES:(j + 1) * LANES]
    return jnp.concatenate([sc[j] for j in range(B_HEADS)], axis=-1)


def _mixer_tail_kernel(x_ref, oa_ref, o0_ref, o1_ref, o2_ref, l0_ref, l1_ref, l2_ref, oc_ref,
                       g_ref, wpa_ref, wpb_ref, wpc_ref, wo_ref, y_ref, *scratch):
    o0, o1, o2, l0, l1, l2 = [
        _natural_rows(ref, sc)
        for ref, sc in zip((o0_ref, o1_ref, o2_ref, l0_ref, l1_ref, l2_ref), scratch)]
    mx = jnp.maximum(jnp.maximum(l0, l1), l2)
    e0, e1, e2 = jnp.exp2(l0 - mx), jnp.exp2(l1 - mx), jnp.exp2(l2 - mx)
    ob = (e0 * o0 + e1 * o1 + e2 * o2) / (e0 + e1 + e2)
    pa = _dot(oa_ref[...], wpa_ref[...])
    pb = _dot(ob.astype(BF16), wpb_ref[...])
    pc = _dot(oc_ref[...], wpc_ref[...])
    d = D_MODEL
    merged = (g_ref[:, 0:d].astype(F32) * pa + g_ref[:, d:2 * d].astype(F32) * pb
              + g_ref[:, 2 * d:3 * d].astype(F32) * pc)
    y_ref[...] = x_ref[...] + _dot(merged.astype(BF16), wo_ref[...])


def _mixer_tail(x, out_a, o_groups, lse_groups, out_c, gates, w_pa, w_pb, w_pc, w_o, seq, tm=256):
    m, d = x.shape
    nt = seq // tm
    row = lambda width: pl.BlockSpec((tm, width), lambda i: (i, 0))
    residue = lambda g: pl.BlockSpec((None, g.shape[1], tm // g.shape[1], B_W),
                                     lambda i: (i // nt, 0, i % nt, 0))
    weights = [_resident(w) for w in (w_pa, w_pb, w_pc, w_o)]
    groups = list(o_groups) + list(lse_groups)
    return pl.pallas_call(
        _mixer_tail_kernel,
        grid=(m // tm,),
        in_specs=([row(d), row(A_W)] + [residue(g) for g in groups]
                  + [row(C_W), row(3 * d)] + weights),
        out_specs=row(d),
        out_shape=jax.ShapeDtypeStruct((m, d), F32),
        scratch_shapes=[pltpu.VMEM((B_HEADS, tm, LANES), F32) for _ in groups],
        compiler_params=_cparams("parallel"),
        name="mixer_tail",
    )(x, out_a, *groups, out_c, gates, w_pa, w_pb, w_pc, w_o)


def _mem_kv_kernel(mem_ref, g_ref, wk_ref, wv_ref, k_ref, v_ref):
    memn = _rms(mem_ref[...], g_ref[...]).astype(BF16)
    k_ref[...] = _dot(memn, wk_ref[...]).astype(k_ref.dtype)
    v_ref[...] = _dot(memn, wv_ref[...]).astype(v_ref.dtype)


def _mem_kv(mem, g, wk, wv):
    bsz, n, d = mem.shape
    out = pl.BlockSpec((None, n, X_W), lambda b: (b, 0, 0))
    return pl.pallas_call(
        _mem_kv_kernel,
        grid=(bsz,),
        in_specs=[pl.BlockSpec((None, n, d), lambda b: (b, 0, 0)),
                  pl.BlockSpec((1, d), lambda b: (0, 0)),
                  pl.BlockSpec(wk.shape, lambda b: (0, 0)),
                  pl.BlockSpec(wv.shape, lambda b: (0, 0))],
        out_specs=[out, out],
        out_shape=[jax.ShapeDtypeStruct((bsz, n, X_W), BF16)] * 2,
        compiler_params=_cparams("parallel"),
        name="mem_kv",
    )(mem, g.reshape(1, d), wk, wv)


def _mem_attn_kernel(x_ref, g_ref, wq_ref, k_ref, v_ref, wo_ref, y_ref):
    x = x_ref[...]
    h = _rms(x, g_ref[...]).astype(BF16)
    q = (_dot(h, wq_ref[...]) * HEAD_DIM ** -0.5).astype(BF16)
    heads = []
    for hd in range(X_HEADS):
        sl = slice(hd * HEAD_DIM, (hd + 1) * HEAD_DIM)
        s = _dot_nt(q[:, sl], k_ref[:, sl])
        p = jnp.exp(s - jnp.max(s, axis=-1, keepdims=True))
        o = _dot(p.astype(BF16), v_ref[:, sl]) / jnp.sum(p, axis=-1, keepdims=True)
        heads.append(o.astype(BF16))
    y_ref[...] = x + _dot(jnp.concatenate(heads, axis=-1), wo_ref[...])


def _mem_attention(x, g, wq, kmem, vmem, wo, seq, tm=512):
    m, d = x.shape
    nt = seq // tm
    n = kmem.shape[1]
    kv = pl.BlockSpec((None, n, X_W), lambda i: (i // nt, 0, 0))
    return pl.pallas_call(
        _mem_attn_kernel,
        grid=(m // tm,),
        in_specs=[pl.BlockSpec((tm, d), lambda i: (i, 0)),
                  pl.BlockSpec((1, d), lambda i: (0, 0)),
                  pl.BlockSpec(wq.shape, lambda i: (0, 0)), kv, kv,
                  pl.BlockSpec(wo.shape, lambda i: (0, 0))],
        out_specs=pl.BlockSpec((tm, d), lambda i: (i, 0)),
        out_shape=jax.ShapeDtypeStruct((m, d), F32),
        compiler_params=_cparams("parallel"),
        name="mem_attention",
    )(x, g.reshape(1, d), wq, kmem, vmem, wo)


def _ffn_up_kernel(x_ref, halo_ref, g_ref, wg_ref, wv_ref, cwg_ref, cwv_ref, cbg_ref, cbv_ref,
                   act_ref, h_sc, *, tiles_per_seq):
    i = pl.program_id(0)
    tm = x_ref.shape[0]

    @pl.when(pl.program_id(1) == 0)
    def _():
        g = g_ref[...]
        keep = (i % tiles_per_seq != 0).astype(F32)
        h_sc[0:HALO, :] = (_rms(halo_ref[...], g) * keep).astype(h_sc.dtype)
        h_sc[HALO:, :] = _rms(x_ref[...], g).astype(h_sc.dtype)

    h = h_sc[...]

    def conv(w_ref, cw_ref, cb_ref):
        u = _dot(h, w_ref[...])
        c = cb_ref[...]
        for tap in range(CONV_W):
            lo = HALO - (CONV_W - 1) + tap
            c = c + cw_ref[tap:tap + 1, :] * u[lo:lo + tm, :]
        return c

    act = jax.nn.silu(conv(wg_ref, cwg_ref, cbg_ref)) * conv(wv_ref, cwv_ref, cbv_ref)
    act_ref[...] = act.astype(act_ref.dtype)


def _ffn_down_kernel(a_ref, w_ref, x_ref, y_ref):
    y_ref[...] = x_ref[...] + _dot(a_ref[...], w_ref[...])


def _conv_ffn(x, g, w_up, conv_w, conv_b, w_down, layer, seq):
    m, d = x.shape
    act = _ffn_up(x, g, w_up, conv_w, conv_b, layer, seq)
    tm, tn = 1024, FFN_TF
    return pl.pallas_call(
        _ffn_down_kernel,
        grid=(m // tm, d // tn),
        in_specs=[pl.BlockSpec((tm, D_FF), lambda i, j: (i, 0)),
                  pl.BlockSpec((None, D_FF, tn), lambda i, j: (layer, 0, j)),
                  pl.BlockSpec((tm, tn), lambda i, j: (i, j))],
        out_specs=pl.BlockSpec((tm, tn), lambda i, j: (i, j)),
        out_shape=jax.ShapeDtypeStruct((m, d), F32),
        compiler_params=_cparams("parallel", "parallel"),
        name="ffn_down",
    )(act, w_down, x)


def _ffn_up(x, g, w_up, conv_w, conv_b, layer, seq, tm=1024, tf=FFN_TF):
    m, d = x.shape
    nf = D_FF_PAD // tf
    halo_blocks = tm // HALO
    return pl.pallas_call(
        functools.partial(_ffn_up_kernel, tiles_per_seq=seq // tm),
        grid=(m // tm, nf),
        in_specs=[pl.BlockSpec((tm, d), lambda i, f: (i, 0)),
                  pl.BlockSpec((HALO, d), lambda i, f: (jnp.maximum(i * halo_blocks - 1, 0), 0)),
                  pl.BlockSpec((1, d), lambda i, f: (0, 0)),
                  pl.BlockSpec((None, d, tf), lambda i, f: (layer, 0, f)),
                  pl.BlockSpec((None, d, tf), lambda i, f: (layer, 0, f + nf)),
                  pl.BlockSpec((CONV_W, tf), lambda i, f: (0, f)),
                  pl.BlockSpec((CONV_W, tf), lambda i, f: (0, f + nf)),
                  pl.BlockSpec((1, tf), lambda i, f: (0, f)),
                  pl.BlockSpec((1, tf), lambda i, f: (0, f + nf))],
        out_specs=pl.BlockSpec((tm, tf), lambda i, f: (i, f)),
        out_shape=jax.ShapeDtypeStruct((m, D_FF_PAD), BF16),
        scratch_shapes=[pltpu.VMEM((HALO + tm, d), BF16)],
        compiler_params=_cparams("parallel", "arbitrary"),
        name="ffn_up",
    )(x, x, g.reshape(1, d), w_up, w_up, conv_w, conv_w, conv_b, conv_b)


def _rope_tables(seq):
    def angles(dim):
        inv_freq = jnp.exp(jnp.arange(0, dim, 2, dtype=F32) * (-math.log(ROPE_THETA) / dim))
        ang = jnp.arange(seq, dtype=F32)[:, None] * inv_freq[None, :]
        return jnp.cos(ang), jnp.sin(ang)

    cos_h, sin_h = angles(HEAD_DIM)
    rope_h = (jnp.concatenate([cos_h, cos_h], axis=-1), jnp.concatenate([-sin_h, sin_h], axis=-1))
    cos_r, sin_r = angles(ROPE_DIM)
    z = jnp.zeros_like(cos_r)
    rope_r = (jnp.concatenate([cos_r, cos_r, z, z], axis=-1),
              jnp.concatenate([-sin_r, z, z, z], axis=-1),
              jnp.concatenate([z, sin_r, z, z], axis=-1))
    return rope_h, rope_r


def _split_in(w_in):
    return [w_in[:, IN_OFFSETS[k]:IN_OFFSETS[k + 1]] for k in range(len(IN_WIDTHS))]


def _pad_cols(w, width):
    return jnp.pad(w, ((0, 0), (0, width - w.shape[1])))


def _layer_params(w_in, w_uq, w_ukv, conv_w, conv_b):
    qa, ka, va, qb, kb, vb, cq, ckv, kr, gates = _split_in(w_in)
    w_qk = jnp.concatenate([qa, ka], axis=1).astype(BF16)
    group_cols = lambda w, g: w[:, g * B_W:(g + 1) * B_W]
    w_b = [jnp.concatenate([group_cols(qb, g), group_cols(kb, g), group_cols(vb, g)],
                           axis=1).astype(BF16) for g in range(len(B_GROUPS))]
    w_down_in = jnp.concatenate([cq, ckv, _pad_cols(kr, LANES)], axis=1).astype(BF16)
    uq = w_uq.reshape(Q_LORA, C_HEADS, NOPE_DIM + ROPE_DIM)
    uq = jnp.pad(uq, ((0, 0), (0, 0), (0, C_QK - NOPE_DIM - ROPE_DIM)))
    ukv = w_ukv.reshape(KV_LORA, C_HEADS, NOPE_DIM + V_DIM)
    return dict(
        w_qk=w_qk, w_va=va.astype(BF16), w_b=w_b, w_gates=gates.astype(BF16),
        w_down_in=w_down_in,
        w_uq=uq.reshape(Q_LORA, C_HEADS * C_QK).astype(BF16),
        w_uk=ukv[:, :, :NOPE_DIM].reshape(KV_LORA, C_HEADS * NOPE_DIM).astype(BF16),
        w_uv=ukv[:, :, NOPE_DIM:].reshape(KV_LORA, C_W).astype(BF16),
        conv_w=_pad_ff_halves(conv_w),
        conv_b=_pad_ff_halves(conv_b.reshape(1, -1)),
    )


def _pad_ff_halves(w):
    pad = [(0, 0)] * (w.ndim - 1) + [(0, D_FF_PAD - D_FF)]
    return jnp.concatenate([jnp.pad(w[..., :D_FF], pad), jnp.pad(w[..., D_FF:], pad)], axis=-1)


def _qk_col_scale():
    q_scale = HEAD_DIM ** -0.5
    parts = [jnp.full((A_W,), q_scale * LOG2E, F32), jnp.ones((A_W,), F32)]
    return jnp.concatenate(parts).reshape(1, QK_W)


def _mixer(x, g_mix, p, g_cq, g_ckv, w_pa, w_pb, w_pc, w_o, rope_h, rope_r, bsz, seq):
    m = x.shape[0]
    h = _rmsnorm(x, g_mix, BF16)
    qk = _matmul(h, p["w_qk"], _mm_rope_kernel, BF16, 1024, 1024, seq=seq,
                 extras=(("col", _qk_col_scale()), ("pos", rope_h[0]), ("pos", rope_h[1])),
                 name="proj_qk_rope")
    qk3 = qk.reshape(bsz, seq, QK_W)
    q_grouped, pos, tile_blk = _moba_regroup(qk3, _kmean(qk3))
    v_a = _matmul(h, p["w_va"], _mm_plain_kernel, BF16, 1024, A_W, name="proj_va")
    gates = _matmul(h, p["w_gates"], _mm_sigmoid_kernel, BF16, 1024, 1024, name="proj_gates")
    v_a3 = v_a.reshape(bsz, seq, A_W)
    part_o, part_lse = _moba_picked_blocks(q_grouped, pos, tile_blk, qk3, v_a3, after=gates)
    cq, ckv, kr = _mla_down(h, p["w_down_in"], g_cq, g_ckv, rope_r, seq)
    q_c = _mla_q(cq, p["w_uq"], rope_r, seq)
    k_c, vt_c = _mla_kv(ckv, kr, p["w_uk"], p["w_uv"], bsz, seq)
    groups = []
    for (window, d), w_g in zip(B_GROUPS, p["w_b"]):
        q_g, k_g, v_g = _proj_dilated(h, w_g, rope_h, d, bsz, seq)
        groups.append(_dilated_attention(q_g, k_g, v_g, window // d))
    out_c = _flash_attention(q_c.reshape(bsz, seq, -1), k_c.reshape(bsz, seq, -1), vt_c,
                             C_HEADS, C_QK).reshape(m, C_W)
    out_a = _moba_merge(qk3, v_a3, part_o, part_lse).reshape(m, A_W)
    return _mixer_tail(x, out_a, [g[0] for g in groups], [g[1] for g in groups], out_c, gates,
                       w_pa.astype(BF16), w_pb.astype(BF16), w_pc.astype(BF16), w_o.astype(BF16),
                       seq)


def kernel(x, mem, g_mix, w_in, g_cq, g_ckv, w_uq, w_ukv, w_pa, w_pb, w_pc, w_o, g_mem, g_memkv,
           w_xq, w_xk, w_xv, w_xo, g_ffn, w_up, conv_w, conv_b, w_down, g_final):
    bsz, seq, d = x.shape
    rope_h, rope_r = _rope_tables(seq)
    xf = x.reshape(bsz * seq, d)
    w_down = w_down.astype(BF16)
    w_up = _pad_ff_halves(w_up.astype(BF16))
    for l in range(DEPTH):
        p = _layer_params(w_in[l], w_uq[l], w_ukv[l], conv_w[l], conv_b[l])
        xf = _mixer(xf, g_mix[l], p, g_cq[l], g_ckv[l], w_pa[l], w_pb[l], w_pc[l], w_o[l],
                    rope_h, rope_r, bsz, seq)
        kmem, vmem = _mem_kv(mem, g_memkv[l], w_xk[l].astype(BF16), w_xv[l].astype(BF16))
        xf = _mem_attention(xf, g_mem[l], w_xq[l].astype(BF16), kmem, vmem,
                            w_xo[l].astype(BF16), seq)
        xf = _conv_ffn(xf, g_ffn[l], w_up, p["conv_w"], p["conv_b"], w_down, l, seq)
    return _rmsnorm(xf, g_final, F32).reshape(bsz, seq, d)
```

```python
import functools
import math

import jax
import jax.numpy as jnp
import numpy as np
from jax import lax
from jax.experimental import pallas as pl
from jax.experimental.pallas import tpu as pltpu
from jax.experimental.pallas import tpu_sc as plsc

F32 = jnp.float32
BF16 = jnp.bfloat16

LANES = 128
SUBLANES = 8
V7X_VMEM_BYTES = 64 * 1024 * 1024
VMEM_LIMIT = V7X_VMEM_BYTES * 7 // 8

D_MODEL = 2048
DEPTH = 2
HEAD_DIM = 128
ROPE_THETA = 10000.0
EPS = 1e-6

A_HEADS = 4
MOBA_BLOCK = 256
MOBA_TOPK = 3

B_GROUPS = ((128, 1), (512, 4), (2048, 16))
B_HEADS = 4
B_QBLOCK = 128

C_HEADS = 8
Q_LORA = 1536
KV_LORA = 512
NOPE_DIM = 128
ROPE_DIM = 64
V_DIM = 128

X_HEADS = 4
D_FF = 5504
CONV_W = 3

A_W = A_HEADS * HEAD_DIM
B_QKV_W = len(B_GROUPS) * B_HEADS * HEAD_DIM
B_W = B_HEADS * HEAD_DIM
C_W = C_HEADS * V_DIM
X_W = X_HEADS * HEAD_DIM
IN_WIDTHS = (A_W, A_W, A_W, B_QKV_W, B_QKV_W, B_QKV_W, Q_LORA, KV_LORA, ROPE_DIM, 3 * D_MODEL)
IN_OFFSETS = tuple(int(o) for o in np.cumsum((0,) + IN_WIDTHS))

QK_W = 2 * A_W
QA_BLK, KA_BLK = 0, A_W // LANES

C_QK = 2 * LANES
MASKED = -1e30
LOG2E = math.log2(math.e)
BF16_ROWS = 16
VT_ROWS = V_DIM + BF16_ROWS
GROUP_STEP = 16
SC_WINDOW = 128

FFN_TF = 512
D_FF_PAD = -(-D_FF // FFN_TF) * FFN_TF
HALO = SUBLANES


def _cparams(*sem):
    return pltpu.CompilerParams(dimension_semantics=sem, vmem_limit_bytes=VMEM_LIMIT)


def _resident(arr):
    zeros = (0,) * arr.ndim
    return pl.BlockSpec(arr.shape, lambda *_: zeros, pipeline_mode=pl.Buffered(1))


def _dot(a, b):
    return jnp.dot(a, b, preferred_element_type=F32)


def _dot_nt(a, b):
    return lax.dot_general(a, b, (((1,), (1,)), ((), ())), preferred_element_type=F32)


def _rms(x, g):
    return x * lax.rsqrt(jnp.mean(x * x, axis=-1, keepdims=True) + EPS) * g


def _rmsnorm_kernel(x_ref, g_ref, o_ref):
    o_ref[...] = _rms(x_ref[...], g_ref[...]).astype(o_ref.dtype)


def _rmsnorm(x, g, out_dtype, tm=512):
    m, d = x.shape
    return pl.pallas_call(
        _rmsnorm_kernel,
        grid=(m // tm,),
        in_specs=[pl.BlockSpec((tm, d), lambda i: (i, 0)),
                  pl.BlockSpec((1, d), lambda i: (0, 0))],
        out_specs=pl.BlockSpec((tm, d), lambda i: (i, 0)),
        out_shape=jax.ShapeDtypeStruct((m, d), out_dtype),
        compiler_params=_cparams("parallel"),
        name="rmsnorm",
    )(x, g.reshape(1, d))


def _rope128(x, c, s):
    return x * c + pltpu.roll(x, HEAD_DIM // 2, 1) * s


def _rope64(x, c, sa, sb):
    half = ROPE_DIM // 2
    return x * c + pltpu.roll(x, LANES - half, 1) * sa + pltpu.roll(x, half, 1) * sb


def _mm_plain_kernel(a_ref, w_ref, o_ref):
    o_ref[...] = _dot(a_ref[...], w_ref[...]).astype(o_ref.dtype)


def _mm_sigmoid_kernel(a_ref, w_ref, o_ref):
    o_ref[...] = jax.nn.sigmoid(_dot(a_ref[...], w_ref[...])).astype(o_ref.dtype)


def _mm_rope_kernel(a_ref, w_ref, cs_ref, c_ref, s_ref, o_ref):
    acc = _dot(a_ref[...], w_ref[...])
    c = c_ref[...]
    s = s_ref[...]
    for j in range(acc.shape[1] // LANES):
        sl = slice(j * LANES, (j + 1) * LANES)
        o_ref[:, sl] = (_rope128(acc[:, sl], c, s) * cs_ref[:, sl]).astype(o_ref.dtype)


def _matmul(a, w, kernel, out_dtype, tm, tn, seq=None, extras=(), name="matmul"):
    m, k = a.shape
    n = w.shape[1]
    in_specs = [pl.BlockSpec((tm, k), lambda i, j: (i, 0)),
                pl.BlockSpec((k, tn), lambda i, j: (0, j))]
    args = [a, w]
    for kind, arr in extras:
        if kind == "col":
            in_specs.append(pl.BlockSpec((1, tn), lambda i, j: (0, j)))
        else:
            nt = seq // tm
            in_specs.append(pl.BlockSpec((tm, LANES), lambda i, j: (i % nt, 0)))
        args.append(arr)
    return pl.pallas_call(
        kernel,
        grid=(m // tm, n // tn),
        in_specs=in_specs,
        out_specs=pl.BlockSpec((tm, tn), lambda i, j: (i, j)),
        out_shape=jax.ShapeDtypeStruct((m, n), out_dtype),
        compiler_params=_cparams("parallel", "parallel"),
        name=name,
    )(*args)


def _mla_down_kernel(h_ref, w_ref, gq_ref, gkv_ref, c_ref, sa_ref, sb_ref,
                     cq_ref, ckv_ref, kr_ref):
    acc = _dot(h_ref[...], w_ref[...])
    cq_ref[...] = _rms(acc[:, :Q_LORA], gq_ref[...]).astype(cq_ref.dtype)
    ckv_ref[...] = _rms(acc[:, Q_LORA:Q_LORA + KV_LORA], gkv_ref[...]).astype(ckv_ref.dtype)
    kr = acc[:, Q_LORA + KV_LORA:]
    kr_ref[...] = _rope64(kr, c_ref[...], sa_ref[...], sb_ref[...]).astype(kr_ref.dtype)


def _mla_down(h, w, g_cq, g_ckv, rope_r, seq, tm=512):
    m, k = h.shape
    n = w.shape[1]
    nt = seq // tm
    row = lambda width: pl.BlockSpec((tm, width), lambda i: (i, 0))
    full = lambda r, c: pl.BlockSpec((r, c), lambda i: (0, 0))
    pos = pl.BlockSpec((tm, LANES), lambda i: (i % nt, 0))
    return pl.pallas_call(
        _mla_down_kernel,
        grid=(m // tm,),
        in_specs=[row(k), full(k, n), full(1, Q_LORA), full(1, KV_LORA), pos, pos, pos],
        out_specs=[row(Q_LORA), row(KV_LORA), row(LANES)],
        out_shape=[jax.ShapeDtypeStruct((m, Q_LORA), BF16),
                   jax.ShapeDtypeStruct((m, KV_LORA), BF16),
                   jax.ShapeDtypeStruct((m, LANES), BF16)],
        compiler_params=_cparams("parallel"),
        name="mla_down",
    )(h, w, g_cq.reshape(1, -1), g_ckv.reshape(1, -1), *rope_r)


def _mla_q_kernel(cq_ref, w_ref, c_ref, sa_ref, sb_ref, q_ref, *, scale):
    acc = _dot(cq_ref[...], w_ref[...])
    c, sa, sb = c_ref[...], sa_ref[...], sb_ref[...]
    for hd in range(C_HEADS):
        lo = hd * C_QK
        q_ref[:, lo:lo + LANES] = (acc[:, lo:lo + LANES] * scale).astype(q_ref.dtype)
        rope = _rope64(acc[:, lo + LANES:lo + C_QK], c, sa, sb)
        q_ref[:, lo + LANES:lo + C_QK] = (rope * scale).astype(q_ref.dtype)


def _mla_q(cq, w, rope_r, seq, tm=512):
    m, k = cq.shape
    n = w.shape[1]
    nt = seq // tm
    pos = pl.BlockSpec((tm, LANES), lambda i: (i % nt, 0))
    return pl.pallas_call(
        functools.partial(_mla_q_kernel, scale=(NOPE_DIM + ROPE_DIM) ** -0.5 * LOG2E),
        grid=(m // tm,),
        in_specs=[pl.BlockSpec((tm, k), lambda i: (i, 0)),
                  pl.BlockSpec((k, n), lambda i: (0, 0)), pos, pos, pos],
        out_specs=pl.BlockSpec((tm, n), lambda i: (i, 0)),
        out_shape=jax.ShapeDtypeStruct((m, n), BF16),
        compiler_params=_cparams("parallel"),
        name="mla_q",
    )(cq, w, *rope_r)


def _store_vt(v, vt_ref):
    vt = v.T
    for hd in range(vt_ref.shape[0]):
        vt_ref[hd, 0:V_DIM, :] = vt[hd * V_DIM:(hd + 1) * V_DIM, :].astype(vt_ref.dtype)
        vt_ref[hd, V_DIM:VT_ROWS, :] = jnp.ones((VT_ROWS - V_DIM, vt.shape[1]), vt_ref.dtype)


def _mla_kv_kernel(ckv_ref, kr_ref, wk_ref, wv_ref, k_ref, vt_ref):
    ckv = ckv_ref[...]
    kn = _dot(ckv, wk_ref[...])
    kr = kr_ref[...]
    for hd in range(C_HEADS):
        lo = hd * C_QK
        k_ref[:, lo:lo + LANES] = kn[:, hd * LANES:(hd + 1) * LANES].astype(k_ref.dtype)
        k_ref[:, lo + LANES:lo + C_QK] = kr
    _store_vt(_dot(ckv, wv_ref[...]), vt_ref)


def _mla_kv(ckv, kr, wk, wv, bsz, seq, tm=512):
    m, k = ckv.shape
    nt = seq // tm
    return pl.pallas_call(
        _mla_kv_kernel,
        grid=(m // tm,),
        in_specs=[pl.BlockSpec((tm, k), lambda i: (i, 0)),
                  pl.BlockSpec((tm, LANES), lambda i: (i, 0)),
                  pl.BlockSpec(wk.shape, lambda i: (0, 0)),
                  pl.BlockSpec(wv.shape, lambda i: (0, 0))],
        out_specs=[pl.BlockSpec((tm, C_HEADS * C_QK), lambda i: (i, 0)),
                   pl.BlockSpec((None, C_HEADS, VT_ROWS, tm), lambda i: (i // nt, 0, 0, i % nt))],
        out_shape=[jax.ShapeDtypeStruct((m, C_HEADS * C_QK), BF16),
                   jax.ShapeDtypeStruct((bsz, C_HEADS, VT_ROWS, seq), BF16)],
        compiler_params=_cparams("parallel"),
        name="mla_kv",
    )(ckv, kr, wk, wv)


def _attend_chunks(qT, k_ref, vt_ref, scratch, *, tk, n_pairs, mask_tail, tail_steps, tail_col):
    m_sc, acc_sc, sa_sc, sb_sc, pa_sc, pb_sc, ala_sc, alb_sc, mxa_sc, mxb_sc = scratch
    s_bufs = (sa_sc, sb_sc)
    p_bufs = (pa_sc, pb_sc)
    al_bufs = (ala_sc, alb_sc)
    mx_bufs = (mxa_sc, mxb_sc)
    last_chunk = k_ref.shape[0] // tk - 1
    m_sc[...] = jnp.full(m_sc.shape, MASKED, F32)
    acc_sc[...] = jnp.zeros(acc_sc.shape, F32)

    def rows(c):
        return pl.ds(pl.multiple_of(jnp.minimum(c, last_chunk) * tk, tk), tk)

    def scores(c, slot, col=0):
        sT = _dot(k_ref[rows(c), :], qT[:, col:])
        s_bufs[slot][:, col:] = sT
        mx_bufs[slot][:, col:] = jnp.max(sT, axis=0, keepdims=True)

    def flush(c, slot, col=0):
        acc_sc[:, col:] = (al_bufs[slot][:, col:] * acc_sc[:, col:]
                           + _dot(vt_ref[:, rows(c)], p_bufs[slot][:, col:]))

    def softmax(sT, top, slot, col):
        m_old = m_sc[:, col:]
        m_new = jnp.maximum(m_old, top)
        al_bufs[slot][:, col:] = jnp.exp2(m_old - m_new)
        p_bufs[slot][:, col:] = jnp.exp2(sT - m_new).astype(p_bufs[slot].dtype)
        m_sc[:, col:] = m_new

    def step(tau, slot, mask, cols):
        col_flush, col, col_next = cols
        if col_flush is not None:
            flush(tau - 2, slot, col_flush)
        if col_next is not None:
            scores(tau + 1, 1 - slot, col_next)
        sT = s_bufs[slot][:, col:]
        if mask is None:
            softmax(sT, mx_bufs[slot][:, col:], slot, col)
        else:
            sT = mask(sT, tau, col)
            softmax(sT, jnp.max(sT, axis=0, keepdims=True), slot, col)

    def full_pair(u, first=False):
        for slot in range(2):
            step(2 * u + slot, slot, None, (None if first else 0, 0, 0))

    def tail(tau, first):
        cols = [tail_col(j) for j in range(tail_steps)]
        for j in range(tail_steps):
            col_flush = cols[j - 2] if j >= 2 else (None if first else 0)
            col_next = cols[j + 1] if j + 1 < tail_steps else None
            step(tau + j, j % 2, mask_tail, (col_flush, cols[j], col_next))
        flush(tau + tail_steps - 2, 0, cols[-2])
        flush(tau + tail_steps - 1, 1, cols[-1])

    scores(0, 0)

    @pl.when(n_pairs == 0)
    def _():
        tail(0, first=True)

    @pl.when(n_pairs > 0)
    def _():
        full_pair(0, first=True)
        lax.fori_loop(1, n_pairs, lambda u, c: (full_pair(u), c)[1], 0)
        tail(2 * n_pairs, first=False)

    acc = acc_sc[...]
    return acc[:V_DIM, :] / acc[V_DIM:V_DIM + 1, :]


def _attend_scratch(tq, tk):
    return [pltpu.VMEM((1, tq), F32), pltpu.VMEM((VT_ROWS, tq), F32),
            pltpu.VMEM((tk, tq), F32), pltpu.VMEM((tk, tq), F32),
            pltpu.VMEM((tk, tq), BF16), pltpu.VMEM((tk, tq), BF16),
            pltpu.VMEM((1, tq), F32), pltpu.VMEM((1, tq), F32),
            pltpu.VMEM((1, tq), F32), pltpu.VMEM((1, tq), F32)]


def _transpose_q(q_ref):
    return q_ref[...].astype(F32).T.astype(BF16)


def _flash_kernel(q_ref, k_ref, vt_ref, o_ref, *scratch, tq, tk):
    i = pl.program_id(2)

    def causal(sT, c, col):
        key = lax.broadcasted_iota(jnp.int32, sT.shape, 0) + c * tk
        qry = lax.broadcasted_iota(jnp.int32, sT.shape, 1) + (i * tq + col)
        return jnp.where(key <= qry, sT, MASKED)

    per_tile = tq // tk
    oT = _attend_chunks(_transpose_q(q_ref), k_ref, vt_ref, scratch, tk=tk,
                        n_pairs=i * (per_tile // 2), mask_tail=causal, tail_steps=per_tile,
                        tail_col=lambda j: j * tk)
    o_ref[...] = oT.T.astype(o_ref.dtype)


def _flash_attention(q, k, vt, heads, qk_w, tq=2048, tk=1024):
    bsz, seq, _ = q.shape
    tq, tk = min(tq, seq), min(tk, seq // 2)
    assert tq % (2 * tk) == 0 and seq % tq == 0
    once = pl.Buffered(1)
    return pl.pallas_call(
        functools.partial(_flash_kernel, tq=tq, tk=tk),
        grid=(bsz, heads, seq // tq),
        in_specs=[pl.BlockSpec((None, tq, qk_w), lambda b, h, i: (b, i, h)),
                  pl.BlockSpec((None, seq, qk_w), lambda b, h, i: (b, 0, h), pipeline_mode=once),
                  pl.BlockSpec((None, None, VT_ROWS, seq), lambda b, h, i: (b, h, 0, 0),
                               pipeline_mode=once)],
        out_specs=pl.BlockSpec((None, tq, V_DIM), lambda b, h, i: (b, i, h)),
        out_shape=jax.ShapeDtypeStruct((bsz, seq, heads * V_DIM), BF16),
        scratch_shapes=_attend_scratch(tq, tk),
        compiler_params=_cparams("parallel", "parallel", "arbitrary"),
        name="mla_flash",
    )(q, k, vt)


def _kmean_kernel(k_ref, o_ref):
    k = k_ref[...].astype(F32)
    o_ref[...] = jnp.mean(k.reshape(SUBLANES, MOBA_BLOCK, k.shape[-1]), axis=1)


def _kmean(qk):
    bsz, seq, _ = qk.shape
    rows = SUBLANES * MOBA_BLOCK
    return pl.pallas_call(
        _kmean_kernel,
        grid=(bsz, seq // rows),
        in_specs=[pl.BlockSpec((None, rows, A_W), lambda b, i: (b, i, KA_BLK * LANES // A_W))],
        out_specs=pl.BlockSpec((None, SUBLANES, A_W), lambda b, i: (b, i, 0)),
        out_shape=jax.ShapeDtypeStruct((bsz, seq // MOBA_BLOCK, A_W), F32),
        compiler_params=_cparams("parallel", "parallel"),
        name="moba_kmean",
    )(qk)


def _block_attention(q, k, v, visible=None):
    s = _dot_nt(q, k)
    if visible is not None:
        s = jnp.where(visible, s, MASKED)
    m = jnp.max(s, axis=-1, keepdims=True)
    p = jnp.exp2(s - m).astype(BF16)
    v_ones = jnp.concatenate([v, jnp.ones((v.shape[0], LANES), v.dtype)], axis=-1)
    acc = _dot(p, v_ones)
    den = acc[:, V_DIM:]
    return acc[:, :V_DIM] / den, m + jnp.log2(den)


def _moba_gate_kernel(q_ref, km_ref, ids_ref, cnt_ref, qf_ref):
    t = MOBA_BLOCK
    i = pl.program_id(1)
    nb = km_ref.shape[0]
    blk = lax.broadcasted_iota(jnp.int32, (nb, t), 0)
    neg_inf = jnp.float32(-jnp.inf)
    not_after = (lax.broadcasted_iota(jnp.int32, (t, t), 0)
                 <= lax.broadcasted_iota(jnp.int32, (t, t), 1))
    upper = jnp.where(not_after, 1.0, 0.0).astype(BF16)
    ones = jnp.ones((SUBLANES, t), BF16)
    for hd in range(A_HEADS):
        sl = slice(hd * HEAD_DIM, (hd + 1) * HEAD_DIM)
        q = q_ref[:, sl].astype(F32)
        qf_ref[hd] = q
        qT = q.T.astype(BF16)
        km = km_ref[:, sl]
        km_hi = km.astype(BF16)
        km_lo = (km - km_hi.astype(F32)).astype(BF16)
        g = jnp.where(blk < i, _dot(km_hi, qT) + _dot(km_lo, qT), neg_inf)
        picks, ranks, counts = [], [], []
        for _ in range(MOBA_TOPK):
            mx = jnp.max(g, axis=0, keepdims=True)
            is_max = (g == mx) & (mx > neg_inf)
            first = jnp.min(jnp.where(is_max, blk, nb), axis=0, keepdims=True)
            pick = blk == first
            g = jnp.where(pick, neg_inf, g)
            onehot = jnp.where(pick, 1.0, 0.0).astype(BF16)
            before = _dot(onehot, upper)
            rank = jnp.sum(jnp.where(pick, before - 1.0, 0.0), axis=0, keepdims=True)
            picks.append(first)
            ranks.append(rank.astype(jnp.int32))
            counts.append(_dot_nt(ones, onehot)[0:1, :])
        pad_i = jnp.zeros((SUBLANES - 2 * MOBA_TOPK, t), jnp.int32)
        ids_ref[hd] = jnp.concatenate(picks + ranks + [pad_i], axis=0)
        pad_f = jnp.zeros((SUBLANES - MOBA_TOPK, nb), F32)
        cnt_ref[hd] = jnp.concatenate(counts + [pad_f], axis=0)


def _moba_gate(qk, kmean):
    bsz, seq, _ = qk.shape
    t = MOBA_BLOCK
    nb = seq // t
    return pl.pallas_call(
        _moba_gate_kernel,
        grid=(bsz, nb),
        in_specs=[pl.BlockSpec((None, t, A_W), lambda b, i: (b, i, QA_BLK * LANES // A_W)),
                  pl.BlockSpec((None, nb, A_W), lambda b, i: (b, 0, 0))],
        out_specs=[pl.BlockSpec((None, A_HEADS, SUBLANES, t), lambda b, i: (b, 0, 0, i)),
                   pl.BlockSpec((None, A_HEADS, None, SUBLANES, nb), lambda b, i: (b, 0, i, 0, 0)),
                   pl.BlockSpec((None, A_HEADS, t, HEAD_DIM), lambda b, i: (b, 0, i, 0))],
        out_shape=[jax.ShapeDtypeStruct((bsz, A_HEADS, SUBLANES, seq), jnp.int32),
                   jax.ShapeDtypeStruct((bsz, A_HEADS, nb, SUBLANES, nb), F32),
                   jax.ShapeDtypeStruct((bsz, A_HEADS, seq, HEAD_DIM), F32)],
        compiler_params=_cparams("parallel", "parallel"),
        name="moba_gate",
    )(qk, kmean)


def _moba_routes(ids, cnt, seq):
    bsz, heads = ids.shape[:2]
    bh, t = bsz * heads, MOBA_BLOCK
    nb = seq // t
    tiles = _moba_tiles(seq)
    picks = ids[:, :, 0:MOBA_TOPK, :].reshape(bh, MOBA_TOPK, nb, t)
    ranks = ids[:, :, MOBA_TOPK:2 * MOBA_TOPK, :].reshape(bh, MOBA_TOPK, nb, t)
    per_tile = cnt[:, :, :, 0:MOBA_TOPK, :].astype(jnp.int32).reshape(bh, nb * MOBA_TOPK, nb)
    before = jnp.cumsum(per_tile, axis=1) - per_tile
    total = jnp.sum(per_tile, axis=1)
    padded = -(-total // t) * t
    ends = jnp.cumsum(padded, axis=1)
    base = before + (ends - padded)[:, None, :]
    base = base.reshape(bh, nb, MOBA_TOPK, nb).transpose(0, 2, 1, 3)
    onehot = picks[..., None] == jnp.arange(nb)
    pos = jnp.sum(jnp.where(onehot, base[:, :, :, None, :], 0), axis=-1) + ranks
    pos = jnp.where(picks < nb, pos, (tiles - 1) * t)
    pos = pos + (jnp.arange(bh, dtype=jnp.int32) * (tiles * t))[:, None, None, None]
    pos = pos.reshape(bh, MOBA_TOPK, seq).transpose(1, 0, 2).reshape(MOBA_TOPK, bh * seq)
    tile_start = jnp.arange(tiles, dtype=jnp.int32) * t
    tile_blk = jnp.sum(tile_start[None, :, None] >= ends[:, None, :], axis=-1)
    tile_blk = jnp.where(tile_start[None, :] < ends[:, -1:], tile_blk, -1)
    return pos.astype(jnp.int32), tile_blk.astype(jnp.int32)


def _moba_tiles(seq):
    nb = seq // MOBA_BLOCK
    return -(-(MOBA_TOPK * nb + nb + 1) // GROUP_STEP) * GROUP_STEP


def _sc_mesh():
    return plsc.VectorSubcoreMesh(core_axis_name="core", subcore_axis_name="subcore")


def _sc_scatter_rows(x, idx, rows):
    slots, n = idx.shape
    d = x.shape[1]

    @pl.kernel(out_type=jax.ShapeDtypeStruct((rows, d), x.dtype), mesh=_sc_mesh())
    def scatter(x_hbm, i_hbm, o_hbm):
        def body(x_vmem, i_vmem):
            pltpu.sync_copy(x_vmem, o_hbm.at[i_vmem.at[0]])

        pltpu.emit_pipeline(
            body, grid=(slots, n // SC_WINDOW),
            in_specs=[pl.BlockSpec((SC_WINDOW, d), lambda s, i: (i, 0)),
                      pl.BlockSpec((1, SC_WINDOW), lambda s, i: (s, i))],
            out_specs=[],
            core_axis_name=("core", "subcore"),
            dimension_semantics=(pltpu.PARALLEL, pltpu.PARALLEL),
        )(x_hbm, i_hbm)

    return scatter(x, idx)


def _sc_gather_rows(x, idx):
    n = idx.shape[0]
    d = x.shape[1]

    @pl.kernel(out_type=jax.ShapeDtypeStruct((n, d), x.dtype), mesh=_sc_mesh())
    def gather(x_hbm, i_hbm, o_hbm):
        def body(i_vmem, o_vmem):
            pltpu.sync_copy(x_hbm.at[i_vmem.at[0]], o_vmem)

        pltpu.emit_pipeline(
            body, grid=(n // SC_WINDOW,),
            in_specs=[pl.BlockSpec((1, SC_WINDOW), lambda i: (0, i))],
            out_specs=[pl.BlockSpec((SC_WINDOW, d), lambda i: (i, 0))],
            core_axis_name=("core", "subcore"),
            dimension_semantics=(pltpu.PARALLEL,),
        )(i_hbm, o_hbm)

    return gather(x, idx.reshape(1, n))


def _moba_group_kernel(tb_ref, q_ref, *refs):
    t = MOBA_BLOCK
    k_refs, v_refs = refs[:GROUP_STEP], refs[GROUP_STEP:2 * GROUP_STEP]
    o_ref, lse_ref = refs[-2:]
    g, step = pl.program_id(0), pl.program_id(1)
    first = step * GROUP_STEP

    @pl.when(tb_ref[g, first] < 0)
    def _():
        o_ref[...] = jnp.zeros(o_ref.shape, o_ref.dtype)
        lse_ref[...] = jnp.full(lse_ref.shape, MASKED, lse_ref.dtype)

    @pl.when(tb_ref[g, first] >= 0)
    def _():
        for u in range(GROUP_STEP):
            used = tb_ref[g, first + u] >= 0
            rows = slice(u * t, (u + 1) * t)
            o, lse = _block_attention(q_ref[rows, :].astype(BF16), k_refs[u][...], v_refs[u][...])
            o_ref[rows, :] = jnp.where(used, o, 0.0)
            lse_ref[rows, :] = jnp.where(used, lse, MASKED)


def _moba_group_attention(q_grouped, tile_blk, qk, v, after):
    bh, rows, _ = q_grouped.shape
    t = MOBA_BLOCK
    tiles = rows // t
    heads = A_HEADS

    def block_of(u, first_col):
        return lambda g, s, tb: (g // heads, jnp.maximum(tb[g, s * GROUP_STEP + u], 0),
                                 first_col + g % heads)

    row_tile = pl.BlockSpec((None, GROUP_STEP * t, HEAD_DIM), lambda g, s, tb: (g, s, 0))
    key_value = lambda first_col: [pl.BlockSpec((None, t, HEAD_DIM), block_of(u, first_col))
                                   for u in range(GROUP_STEP)]
    grid_spec = pltpu.PrefetchScalarGridSpec(
        num_scalar_prefetch=1,
        grid=(bh, tiles // GROUP_STEP),
        in_specs=([row_tile] + key_value(KA_BLK) + key_value(0)
                  + [pl.BlockSpec(memory_space=pl.ANY)]),
        out_specs=[row_tile, row_tile],
    )
    return pl.pallas_call(
        _moba_group_kernel,
        grid_spec=grid_spec,
        out_shape=[jax.ShapeDtypeStruct(q_grouped.shape, F32)] * 2,
        compiler_params=_cparams("parallel", "parallel"),
        name="moba_group",
    )(tile_blk, q_grouped, *([qk] * GROUP_STEP), *([v] * GROUP_STEP), after)


def _moba_merge_kernel(q_ref, k_ref, v_ref, po_ref, pl_ref, o_ref):
    t = MOBA_BLOCK
    causal = (lax.broadcasted_iota(jnp.int32, (t, t), 1)
              <= lax.broadcasted_iota(jnp.int32, (t, t), 0))
    for hd in range(A_HEADS):
        sl = slice(hd * HEAD_DIM, (hd + 1) * HEAD_DIM)
        o_own, lse_own = _block_attention(q_ref[:, sl], k_ref[:, sl], v_ref[:, sl], causal)
        outs = [o_own] + [po_ref[s, hd] for s in range(MOBA_TOPK)]
        lses = [lse_own] + [pl_ref[s, hd] for s in range(MOBA_TOPK)]
        top = functools.reduce(jnp.maximum, lses)
        weights = [jnp.exp2(l - top) for l in lses]
        num = sum(w * o for w, o in zip(weights, outs))
        o_ref[:, sl] = (num / sum(weights)).astype(o_ref.dtype)


def _moba_merge(qk, v, part_o, part_lse):
    bsz, seq, _ = qk.shape
    t = MOBA_BLOCK
    part = pl.BlockSpec((MOBA_TOPK, None, A_HEADS, t, HEAD_DIM), lambda b, i: (0, b, 0, i, 0))
    return pl.pallas_call(
        _moba_merge_kernel,
        grid=(bsz, seq // t),
        in_specs=[pl.BlockSpec((None, t, A_W), lambda b, i: (b, i, QA_BLK * LANES // A_W)),
                  pl.BlockSpec((None, t, A_W), lambda b, i: (b, i, KA_BLK * LANES // A_W)),
                  pl.BlockSpec((None, t, A_W), lambda b, i: (b, i, 0)),
                  part, part],
        out_specs=pl.BlockSpec((None, t, A_W), lambda b, i: (b, i, 0)),
        out_shape=jax.ShapeDtypeStruct((bsz, seq, A_W), BF16),
        compiler_params=_cparams("parallel", "parallel"),
        name="moba_merge",
    )(qk, qk, v, part_o, part_lse)


def _moba_regroup(qk, kmean):
    bsz, seq, _ = qk.shape
    bh = bsz * A_HEADS
    rows = _moba_tiles(seq) * MOBA_BLOCK
    ids, cnt, q_f32 = _moba_gate(qk, kmean)
    pos, tile_blk = _moba_routes(ids, cnt, seq)
    q_grouped = _sc_scatter_rows(q_f32.reshape(bh * seq, HEAD_DIM), pos, bh * rows)
    return q_grouped.reshape(bh, rows, HEAD_DIM), pos, tile_blk


def _moba_picked_blocks(q_grouped, pos, tile_blk, qk, v, after):
    bsz, seq, _ = qk.shape
    bh, rows, _ = q_grouped.shape
    o_g, lse_g = _moba_group_attention(q_grouped, tile_blk, qk, v, after)
    flat = pos.reshape(-1)
    back = lambda a: _sc_gather_rows(a.reshape(bh * rows, HEAD_DIM), flat).reshape(
        MOBA_TOPK, bsz, A_HEADS, seq, HEAD_DIM)
    return back(o_g), back(lse_g)


def _proj_dilated_kernel(h_ref, w_ref, c_ref, s_ref, q_ref, k_ref, v_ref, sc, *, d):
    acc = _dot(h_ref[...], w_ref[...])
    c, s = c_ref[...], s_ref[...]
    q_scale = HEAD_DIM ** -0.5 * LOG2E
    for j in range(acc.shape[1] // LANES):
        blk = acc[:, j * LANES:(j + 1) * LANES]
        if j < B_HEADS:
            blk = _rope128(blk, c, s) * q_scale
        elif j < 2 * B_HEADS:
            blk = _rope128(blk, c, s)
        sc[j] = blk
    rows = acc.shape[0] // d
    for r in range(d):
        for j in range(acc.shape[1] // LANES):
            dst = (q_ref, k_ref, v_ref)[j // B_HEADS]
            col = (j % B_HEADS) * LANES
            dst[r, :, col:col + LANES] = sc[j, pl.ds(r, rows, stride=d), :].astype(dst.dtype)


def _proj_dilated(h, w, rope_h, d, bsz, seq, tm=512):
    m, k = h.shape
    nt = seq // tm
    pos = pl.BlockSpec((tm, LANES), lambda i: (i % nt, 0))
    out = pl.BlockSpec((None, d, tm // d, B_W), lambda i: (i // nt, 0, i % nt, 0))
    return pl.pallas_call(
        functools.partial(_proj_dilated_kernel, d=d),
        grid=(m // tm,),
        in_specs=[pl.BlockSpec((tm, k), lambda i: (i, 0)), pl.BlockSpec(w.shape, lambda i: (0, 0)),
                  pos, pos],
        out_specs=[out] * 3,
        out_shape=[jax.ShapeDtypeStruct((bsz, d, seq // d, B_W), BF16)] * 3,
        scratch_shapes=[pltpu.VMEM((w.shape[1] // LANES, tm, LANES), F32)],
        compiler_params=_cparams("parallel"),
        name=f"proj_dilated_d{d}",
    )(h, w, *rope_h)


def _dilated_kernel(q_ref, kc_ref, kp_ref, vc_ref, vp_ref, o_ref, lse_ref, *, span):
    t, tp = q_ref.shape[0], kp_ref.shape[0]
    i = pl.program_id(2)
    shape = (2 * tp, tp)
    key_row = lax.broadcasted_iota(jnp.int32, shape, 0)
    dist = lax.broadcasted_iota(jnp.int32, shape, 1) + tp - key_row
    visible = (dist >= 0) & (dist <= span)
    bias = jnp.where(visible, 0.0, MASKED)
    bias_first = jnp.where(visible & ((key_row >= tp) | (i > 0)), 0.0, MASKED)
    ones = jnp.ones((BF16_ROWS, tp + t), BF16)

    def transposed(x):
        return x.astype(F32).T.astype(BF16)

    for j in range(B_HEADS):
        sl = slice(j * LANES, (j + 1) * LANES)
        qT = transposed(q_ref[:, sl])
        k_all = jnp.concatenate([kp_ref[:, sl], kc_ref[:, sl]], axis=0)
        vt_all = jnp.concatenate([transposed(vp_ref[:, sl]), transposed(vc_ref[:, sl])], axis=1)
        vt_all = jnp.concatenate([vt_all, ones], axis=0)
        outs, lses = [], []
        for u in range(t // tp):
            window = slice(u * tp, (u + 2) * tp)
            s = _dot(k_all[window, :], qT[:, u * tp:(u + 1) * tp])
            s = s + (bias_first if u == 0 else bias)
            m = jnp.max(s, axis=0, keepdims=True)
            p = jnp.exp2(s - m).astype(BF16)
            acc = _dot(vt_all[:, window], p)
            den = acc[V_DIM:V_DIM + 1, :]
            outs.append(acc[:V_DIM, :] / den)
            lses.append(m + jnp.log2(den))
        o_ref[:, sl] = jnp.concatenate(outs, axis=1).T
        lse = jnp.concatenate(lses, axis=1)
        lse_ref[:, sl] = jnp.broadcast_to(lse, (LANES, t)).T


def _dilated_attention(q, k, v, span, t=1024):
    bsz, d, length, _ = q.shape
    t = min(t, length)
    tp = B_QBLOCK
    assert span <= tp and t % tp == 0
    cur = pl.BlockSpec((None, None, t, B_W), lambda b, r, i: (b, r, i, 0))
    prev = pl.BlockSpec((None, None, tp, B_W),
                        lambda b, r, i: (b, r, jnp.maximum(i * (t // tp) - 1, 0), 0))
    return pl.pallas_call(
        functools.partial(_dilated_kernel, span=span),
        grid=(bsz, d, length // t),
        in_specs=[cur, cur, prev, cur, prev],
        out_specs=[cur, cur],
        out_shape=[jax.ShapeDtypeStruct(q.shape, F32)] * 2,
        compiler_params=_cparams("parallel", "parallel", "parallel"),
        name=f"dilated_d{d}",
    )(q, k, k, v, v)


def _natural_rows(ref, sc):
    d, rows = ref.shape[0], ref.shape[1]
    if d == 1:
        return ref[0]
    for r in range(d):
        for j in range(B_HEADS):
            sc[j, pl.ds(r, rows, stride=d), :] = ref[r, :, j * LANES:(j + 1) * LANES]
    return jnp.concatenate([sc[j] for j in range(B_HEADS)], axis=-1)


def _mixer_tail_kernel(x_ref, oa_ref, o0_ref, o1_ref, o2_ref, l0_ref, l1_ref, l2_ref, oc_ref,
                       g_ref, wpa_ref, wpb_ref, wpc_ref, wo_ref, y_ref, *scratch):
    o0, o1, o2, l0, l1, l2 = [
        _natural_rows(ref, sc)
        for ref, sc in zip((o0_ref, o1_ref, o2_ref, l0_ref, l1_ref, l2_ref), scratch)]
    mx = jnp.maximum(jnp.maximum(l0, l1), l2)
    e0, e1, e2 = jnp.exp2(l0 - mx), jnp.exp2(l1 - mx), jnp.exp2(l2 - mx)
    ob = (e0 * o0 + e1 * o1 + e2 * o2) / (e0 + e1 + e2)
    pa = _dot(oa_ref[...], wpa_ref[...])
    pb = _dot(ob.astype(BF16), wpb_ref[...])
    pc = _dot(oc_ref[...], wpc_ref[...])
    d = D_MODEL
    merged = (g_ref[:, 0:d].astype(F32) * pa + g_ref[:, d:2 * d].astype(F32) * pb
              + g_ref[:, 2 * d:3 * d].astype(F32) * pc)
    y_ref[...] = x_ref[...] + _dot(merged.astype(BF16), wo_ref[...])


def _mixer_tail(x, out_a, o_groups, lse_groups, out_c, gates, w_pa, w_pb, w_pc, w_o, seq, tm=256):
    m, d = x.shape
    nt = seq // tm
    row = lambda width: pl.BlockSpec((tm, width), lambda i: (i, 0))
    residue = lambda g: pl.BlockSpec((None, g.shape[1], tm // g.shape[1], B_W),
                                     lambda i: (i // nt, 0, i % nt, 0))
    weights = [_resident(w) for w in (w_pa, w_pb, w_pc, w_o)]
    groups = list(o_groups) + list(lse_groups)
    return pl.pallas_call(
        _mixer_tail_kernel,
        grid=(m // tm,),
        in_specs=([row(d), row(A_W)] + [residue(g) for g in groups]
                  + [row(C_W), row(3 * d)] + weights),
        out_specs=row(d),
        out_shape=jax.ShapeDtypeStruct((m, d), F32),
        scratch_shapes=[pltpu.VMEM((B_HEADS, tm, LANES), F32) for _ in groups],
        compiler_params=_cparams("parallel"),
        name="mixer_tail",
    )(x, out_a, *groups, out_c, gates, w_pa, w_pb, w_pc, w_o)


def _mem_kv_kernel(mem_ref, g_ref, wk_ref, wv_ref, k_ref, v_ref):
    memn = _rms(mem_ref[...], g_ref[...]).astype(BF16)
    k_ref[...] = _dot(memn, wk_ref[...]).astype(k_ref.dtype)
    v_ref[...] = _dot(memn, wv_ref[...]).astype(v_ref.dtype)


def _mem_kv(mem, g, wk, wv):
    bsz, n, d = mem.shape
    out = pl.BlockSpec((None, n, X_W), lambda b: (b, 0, 0))
    return pl.pallas_call(
        _mem_kv_kernel,
        grid=(bsz,),
        in_specs=[pl.BlockSpec((None, n, d), lambda b: (b, 0, 0)),
                  pl.BlockSpec((1, d), lambda b: (0, 0)),
                  pl.BlockSpec(wk.shape, lambda b: (0, 0)),
                  pl.BlockSpec(wv.shape, lambda b: (0, 0))],
        out_specs=[out, out],
        out_shape=[jax.ShapeDtypeStruct((bsz, n, X_W), BF16)] * 2,
        compiler_params=_cparams("parallel"),
        name="mem_kv",
    )(mem, g.reshape(1, d), wk, wv)


def _mem_attn_kernel(x_ref, g_ref, wq_ref, k_ref, v_ref, wo_ref, y_ref):
    x = x_ref[...]
    h = _rms(x, g_ref[...]).astype(BF16)
    q = (_dot(h, wq_ref[...]) * HEAD_DIM ** -0.5).astype(BF16)
    heads = []
    for hd in range(X_HEADS):
        sl = slice(hd * HEAD_DIM, (hd + 1) * HEAD_DIM)
        s = _dot_nt(q[:, sl], k_ref[:, sl])
        p = jnp.exp(s - jnp.max(s, axis=-1, keepdims=True))
        o = _dot(p.astype(BF16), v_ref[:, sl]) / jnp.sum(p, axis=-1, keepdims=True)
        heads.append(o.astype(BF16))
    y_ref[...] = x + _dot(jnp.concatenate(heads, axis=-1), wo_ref[...])


def _mem_attention(x, g, wq, kmem, vmem, wo, seq, tm=512):
    m, d = x.shape
    nt = seq // tm
    n = kmem.shape[1]
    kv = pl.BlockSpec((None, n, X_W), lambda i: (i // nt, 0, 0))
    return pl.pallas_call(
        _mem_attn_kernel,
        grid=(m // tm,),
        in_specs=[pl.BlockSpec((tm, d), lambda i: (i, 0)),
                  pl.BlockSpec((1, d), lambda i: (0, 0)),
                  pl.BlockSpec(wq.shape, lambda i: (0, 0)), kv, kv,
                  pl.BlockSpec(wo.shape, lambda i: (0, 0))],
        out_specs=pl.BlockSpec((tm, d), lambda i: (i, 0)),
        out_shape=jax.ShapeDtypeStruct((m, d), F32),
        compiler_params=_cparams("parallel"),
        name="mem_attention",
    )(x, g.reshape(1, d), wq, kmem, vmem, wo)


def _ffn_up_kernel(x_ref, halo_ref, g_ref, wg_ref, wv_ref, cwg_ref, cwv_ref, cbg_ref, cbv_ref,
                   act_ref, h_sc, *, tiles_per_seq):
    i = pl.program_id(0)
    tm = x_ref.shape[0]

    @pl.when(pl.program_id(1) == 0)
    def _():
        g = g_ref[...]
        keep = (i % tiles_per_seq != 0).astype(F32)
        h_sc[0:HALO, :] = (_rms(halo_ref[...], g) * keep).astype(h_sc.dtype)
        h_sc[HALO:, :] = _rms(x_ref[...], g).astype(h_sc.dtype)

    h = h_sc[...]

    def conv(w_ref, cw_ref, cb_ref):
        u = _dot(h, w_ref[...])
        c = cb_ref[...]
        for tap in range(CONV_W):
            lo = HALO - (CONV_W - 1) + tap
            c = c + cw_ref[tap:tap + 1, :] * u[lo:lo + tm, :]
        return c

    act = jax.nn.silu(conv(wg_ref, cwg_ref, cbg_ref)) * conv(wv_ref, cwv_ref, cbv_ref)
    act_ref[...] = act.astype(act_ref.dtype)


def _ffn_down_kernel(a_ref, w_ref, x_ref, y_ref):
    y_ref[...] = x_ref[...] + _dot(a_ref[...], w_ref[...])


def _conv_ffn(x, g, w_up, conv_w, conv_b, w_down, layer, seq):
    m, d = x.shape
    act = _ffn_up(x, g, w_up, conv_w, conv_b, layer, seq)
    tm, tn = 1024, FFN_TF
    return pl.pallas_call(
        _ffn_down_kernel,
        grid=(m // tm, d // tn),
        in_specs=[pl.BlockSpec((tm, D_FF), lambda i, j: (i, 0)),
                  pl.BlockSpec((None, D_FF, tn), lambda i, j: (layer, 0, j)),
                  pl.BlockSpec((tm, tn), lambda i, j: (i, j))],
        out_specs=pl.BlockSpec((tm, tn), lambda i, j: (i, j)),
        out_shape=jax.ShapeDtypeStruct((m, d), F32),
        compiler_params=_cparams("parallel", "parallel"),
        name="ffn_down",
    )(act, w_down, x)


def _ffn_up(x, g, w_up, conv_w, conv_b, layer, seq, tm=1024, tf=FFN_TF):
    m, d = x.shape
    nf = D_FF_PAD // tf
    halo_blocks = tm // HALO
    return pl.pallas_call(
        functools.partial(_ffn_up_kernel, tiles_per_seq=seq // tm),
        grid=(m // tm, nf),
        in_specs=[pl.BlockSpec((tm, d), lambda i, f: (i, 0)),
                  pl.BlockSpec((HALO, d), lambda i, f: (jnp.maximum(i * halo_blocks - 1, 0), 0)),
                  pl.BlockSpec((1, d), lambda i, f: (0, 0)),
                  pl.BlockSpec((None, d, tf), lambda i, f: (layer, 0, f)),
                  pl.BlockSpec((None, d, tf), lambda i, f: (layer, 0, f + nf)),
                  pl.BlockSpec((CONV_W, tf), lambda i, f: (0, f)),
                  pl.BlockSpec((CONV_W, tf), lambda i, f: (0, f + nf)),
                  pl.BlockSpec((1, tf), lambda i, f: (0, f)),
                  pl.BlockSpec((1, tf), lambda i, f: (0, f + nf))],
        out_specs=pl.BlockSpec((tm, tf), lambda i, f: (i, f)),
        out_shape=jax.ShapeDtypeStruct((m, D_FF_PAD), BF16),
        scratch_shapes=[pltpu.VMEM((HALO + tm, d), BF16)],
        compiler_params=_cparams("parallel", "arbitrary"),
        name="ffn_up",
    )(x, x, g.reshape(1, d), w_up, w_up, conv_w, conv_w, conv_b, conv_b)


def _rope_tables(seq):
    def angles(dim):
        inv_freq = jnp.exp(jnp.arange(0, dim, 2, dtype=F32) * (-math.log(ROPE_THETA) / dim))
        ang = jnp.arange(seq, dtype=F32)[:, None] * inv_freq[None, :]
        return jnp.cos(ang), jnp.sin(ang)

    cos_h, sin_h = angles(HEAD_DIM)
    rope_h = (jnp.concatenate([cos_h, cos_h], axis=-1), jnp.concatenate([-sin_h, sin_h], axis=-1))
    cos_r, sin_r = angles(ROPE_DIM)
    z = jnp.zeros_like(cos_r)
    rope_r = (jnp.concatenate([cos_r, cos_r, z, z], axis=-1),
              jnp.concatenate([-sin_r, z, z, z], axis=-1),
              jnp.concatenate([z, sin_r, z, z], axis=-1))
    return rope_h, rope_r


def _split_in(w_in):
    return [w_in[:, IN_OFFSETS[k]:IN_OFFSETS[k + 1]] for k in range(len(IN_WIDTHS))]


def _pad_cols(w, width):
    return jnp.pad(w, ((0, 0), (0, width - w.shape[1])))


def _layer_params(w_in, w_uq, w_ukv, conv_w, conv_b):
    qa, ka, va, qb, kb, vb, cq, ckv, kr, gates = _split_in(w_in)
    w_qk = jnp.concatenate([qa, ka], axis=1).astype(BF16)
    group_cols = lambda w, g: w[:, g * B_W:(g + 1) * B_W]
    w_b = [jnp.concatenate([group_cols(qb, g), group_cols(kb, g), group_cols(vb, g)],
                           axis=1).astype(BF16) for g in range(len(B_GROUPS))]
    w_down_in = jnp.concatenate([cq, ckv, _pad_cols(kr, LANES)], axis=1).astype(BF16)
    uq = w_uq.reshape(Q_LORA, C_HEADS, NOPE_DIM + ROPE_DIM)
    uq = jnp.pad(uq, ((0, 0), (0, 0), (0, C_QK - NOPE_DIM - ROPE_DIM)))
    ukv = w_ukv.reshape(KV_LORA, C_HEADS, NOPE_DIM + V_DIM)
    return dict(
        w_qk=w_qk, w_va=va.astype(BF16), w_b=w_b, w_gates=gates.astype(BF16),
        w_down_in=w_down_in,
        w_uq=uq.reshape(Q_LORA, C_HEADS * C_QK).astype(BF16),
        w_uk=ukv[:, :, :NOPE_DIM].reshape(KV_LORA, C_HEADS * NOPE_DIM).astype(BF16),
        w_uv=ukv[:, :, NOPE_DIM:].reshape(KV_LORA, C_W).astype(BF16),
        conv_w=_pad_ff_halves(conv_w),
        conv_b=_pad_ff_halves(conv_b.reshape(1, -1)),
    )


def _pad_ff_halves(w):
    pad = [(0, 0)] * (w.ndim - 1) + [(0, D_FF_PAD - D_FF)]
    return jnp.concatenate([jnp.pad(w[..., :D_FF], pad), jnp.pad(w[..., D_FF:], pad)], axis=-1)


def _qk_col_scale():
    q_scale = HEAD_DIM ** -0.5
    parts = [jnp.full((A_W,), q_scale * LOG2E, F32), jnp.ones((A_W,), F32)]
    return jnp.concatenate(parts).reshape(1, QK_W)


def _mixer(x, g_mix, p, g_cq, g_ckv, w_pa, w_pb, w_pc, w_o, rope_h, rope_r, bsz, seq):
    m = x.shape[0]
    h = _rmsnorm(x, g_mix, BF16)
    qk = _matmul(h, p["w_qk"], _mm_rope_kernel, BF16, 1024, 1024, seq=seq,
                 extras=(("col", _qk_col_scale()), ("pos", rope_h[0]), ("pos", rope_h[1])),
                 name="proj_qk_rope")
    qk3 = qk.reshape(bsz, seq, QK_W)
    q_grouped, pos, tile_blk = _moba_regroup(qk3, _kmean(qk3))
    v_a = _matmul(h, p["w_va"], _mm_plain_kernel, BF16, 1024, A_W, name="proj_va")
    gates = _matmul(h, p["w_gates"], _mm_sigmoid_kernel, BF16, 1024, 1024, name="proj_gates")
    v_a3 = v_a.reshape(bsz, seq, A_W)
    part_o, part_lse = _moba_picked_blocks(q_grouped, pos, tile_blk, qk3, v_a3, after=gates)
    cq, ckv, kr = _mla_down(h, p["w_down_in"], g_cq, g_ckv, rope_r, seq)
    q_c = _mla_q(cq, p["w_uq"], rope_r, seq)
    k_c, vt_c = _mla_kv(ckv, kr, p["w_uk"], p["w_uv"], bsz, seq)
    groups = []
    for (window, d), w_g in zip(B_GROUPS, p["w_b"]):
        q_g, k_g, v_g = _proj_dilated(h, w_g, rope_h, d, bsz, seq)
        groups.append(_dilated_attention(q_g, k_g, v_g, window // d))
    out_c = _flash_attention(q_c.reshape(bsz, seq, -1), k_c.reshape(bsz, seq, -1), vt_c,
                             C_HEADS, C_QK).reshape(m, C_W)
    out_a = _moba_merge(qk3, v_a3, part_o, part_lse).reshape(m, A_W)
    return _mixer_tail(x, out_a, [g[0] for g in groups], [g[1] for g in groups], out_c, gates,
                       w_pa.astype(BF16), w_pb.astype(BF16), w_pc.astype(BF16), w_o.astype(BF16),
                       seq)


def kernel(x, mem, g_mix, w_in, g_cq, g_ckv, w_uq, w_ukv, w_pa, w_pb, w_pc, w_o, g_mem, g_memkv,
           w_xq, w_xk, w_xv, w_xo, g_ffn, w_up, conv_w, conv_b, w_down, g_final):
    bsz, seq, d = x.shape
    rope_h, rope_r = _rope_tables(seq)
    xf = x.reshape(bsz * seq, d)
    w_down = w_down.astype(BF16)
    w_up = _pad_ff_halves(w_up.astype(BF16))
    for l in range(DEPTH):
        p = _layer_params(w_in[l], w_uq[l], w_ukv[l], conv_w[l], conv_b[l])
        xf = _mixer(xf, g_mix[l], p, g_cq[l], g_ckv[l], w_pa[l], w_pb[l], w_pc[l], w_o[l],
                    rope_h, rope_r, bsz, seq)
        kmem, vmem = _mem_kv(mem, g_memkv[l], w_xk[l].astype(BF16), w_xv[l].astype(BF16))
        xf = _mem_attention(xf, g_mem[l], w_xq[l].astype(BF16), kmem, vmem,
                            w_xo[l].astype(BF16), seq)
        xf = _conv_ffn(xf, g_ffn[l], w_up, p["conv_w"], p["conv_b"], w_down, l, seq)
    return _rmsnorm(xf, g_final, F32).reshape(bsz, seq, d)
```

```python
import functools
import math

import jax
import jax.numpy as jnp
import numpy as np
from jax import lax
from jax.experimental import pallas as pl
from jax.experimental.pallas import tpu as pltpu
from jax.experimental.pallas import tpu_sc as plsc

F32 = jnp.float32
BF16 = jnp.bfloat16

LANES = 128
SUBLANES = 8
V7X_VMEM_BYTES = 64 * 1024 * 1024
VMEM_LIMIT = V7X_VMEM_BYTES * 7 // 8

D_MODEL = 2048
DEPTH = 2
HEAD_DIM = 128
ROPE_THETA = 10000.0
EPS = 1e-6

A_HEADS = 4
MOBA_BLOCK = 256
MOBA_TOPK = 3

B_GROUPS = ((128, 1), (512, 4), (2048, 16))
B_HEADS = 4
B_QBLOCK = 128

C_HEADS = 8
Q_LORA = 1536
KV_LORA = 512
NOPE_DIM = 128
ROPE_DIM = 64
V_DIM = 128

X_HEADS = 4
D_FF = 5504
CONV_W = 3

A_W = A_HEADS * HEAD_DIM
B_QKV_W = len(B_GROUPS) * B_HEADS * HEAD_DIM
B_W = B_HEADS * HEAD_DIM
C_W = C_HEADS * V_DIM
X_W = X_HEADS * HEAD_DIM
IN_WIDTHS = (A_W, A_W, A_W, B_QKV_W, B_QKV_W, B_QKV_W, Q_LORA, KV_LORA, ROPE_DIM, 3 * D_MODEL)
IN_OFFSETS = tuple(int(o) for o in np.cumsum((0,) + IN_WIDTHS))

QK_W = 2 * A_W
QA_BLK, KA_BLK = 0, A_W // LANES

C_QK = 2 * LANES
MASKED = -1e30
LOG2E = math.log2(math.e)
BF16_ROWS = 16
VT_ROWS = V_DIM + BF16_ROWS
GROUP_STEP = 16
SC_WINDOW = 128

FFN_TF = 512
D_FF_PAD = -(-D_FF // FFN_TF) * FFN_TF
HALO = SUBLANES


def _cparams(*sem):
    return pltpu.CompilerParams(dimension_semantics=sem, vmem_limit_bytes=VMEM_LIMIT)


def _resident(arr):
    zeros = (0,) * arr.ndim
    return pl.BlockSpec(arr.shape, lambda *_: zeros, pipeline_mode=pl.Buffered(1))


def _dot(a, b):
    return jnp.dot(a, b, preferred_element_type=F32)


def _dot_nt(a, b):
    return lax.dot_general(a, b, (((1,), (1,)), ((), ())), preferred_element_type=F32)


def _rms(x, g):
    return x * lax.rsqrt(jnp.mean(x * x, axis=-1, keepdims=True) + EPS) * g


def _rmsnorm_kernel(x_ref, g_ref, o_ref):
    o_ref[...] = _rms(x_ref[...], g_ref[...]).astype(o_ref.dtype)


def _rmsnorm(x, g, out_dtype, tm=512):
    m, d = x.shape
    return pl.pallas_call(
        _rmsnorm_kernel,
        grid=(m // tm,),
        in_specs=[pl.BlockSpec((tm, d), lambda i: (i, 0)),
                  pl.BlockSpec((1, d), lambda i: (0, 0))],
        out_specs=pl.BlockSpec((tm, d), lambda i: (i, 0)),
        out_shape=jax.ShapeDtypeStruct((m, d), out_dtype),
        compiler_params=_cparams("parallel"),
        name="rmsnorm",
    )(x, g.reshape(1, d))


def _rope128(x, c, s):
    return x * c + pltpu.roll(x, HEAD_DIM // 2, 1) * s


def _rope64(x, c, sa, sb):
    half = ROPE_DIM // 2
    return x * c + pltpu.roll(x, LANES - half, 1) * sa + pltpu.roll(x, half, 1) * sb


def _mm_plain_kernel(a_ref, w_ref, o_ref):
    o_ref[...] = _dot(a_ref[...], w_ref[...]).astype(o_ref.dtype)


def _mm_sigmoid_kernel(a_ref, w_ref, o_ref):
    o_ref[...] = jax.nn.sigmoid(_dot(a_ref[...], w_ref[...])).astype(o_ref.dtype)


def _mm_rope_kernel(a_ref, w_ref, cs_ref, c_ref, s_ref, o_ref):
    acc = _dot(a_ref[...], w_ref[...])
    c = c_ref[...]
    s = s_ref[...]
    for j in range(acc.shape[1] // LANES):
        sl = slice(j * LANES, (j + 1) * LANES)
        o_ref[:, sl] = (_rope128(acc[:, sl], c, s) * cs_ref[:, sl]).astype(o_ref.dtype)


def _matmul(a, w, kernel, out_dtype, tm, tn, seq=None, extras=(), name="matmul"):
    m, k = a.shape
    n = w.shape[1]
    in_specs = [pl.BlockSpec((tm, k), lambda i, j: (i, 0)),
                pl.BlockSpec((k, tn), lambda i, j: (0, j))]
    args = [a, w]
    for kind, arr in extras:
        if kind == "col":
            in_specs.append(pl.BlockSpec((1, tn), lambda i, j: (0, j)))
        else:
            nt = seq // tm
            in_specs.append(pl.BlockSpec((tm, LANES), lambda i, j: (i % nt, 0)))
        args.append(arr)
    return pl.pallas_call(
        kernel,
        grid=(m // tm, n // tn),
        in_specs=in_specs,
        out_specs=pl.BlockSpec((tm, tn), lambda i, j: (i, j)),
        out_shape=jax.ShapeDtypeStruct((m, n), out_dtype),
        compiler_params=_cparams("parallel", "parallel"),
        name=name,
    )(*args)


def _mla_down_kernel(h_ref, w_ref, gq_ref, gkv_ref, c_ref, sa_ref, sb_ref,
                     cq_ref, ckv_ref, kr_ref):
    acc = _dot(h_ref[...], w_ref[...])
    cq_ref[...] = _rms(acc[:, :Q_LORA], gq_ref[...]).astype(cq_ref.dtype)
    ckv_ref[...] = _rms(acc[:, Q_LORA:Q_LORA + KV_LORA], gkv_ref[...]).astype(ckv_ref.dtype)
    kr = acc[:, Q_LORA + KV_LORA:]
    kr_ref[...] = _rope64(kr, c_ref[...], sa_ref[...], sb_ref[...]).astype(kr_ref.dtype)


def _mla_down(h, w, g_cq, g_ckv, rope_r, seq, tm=1024):
    m, k = h.shape
    n = w.shape[1]
    nt = seq // tm
    row = lambda width: pl.BlockSpec((tm, width), lambda i: (i, 0))
    full = lambda r, c: pl.BlockSpec((r, c), lambda i: (0, 0))
    pos = pl.BlockSpec((tm, LANES), lambda i: (i % nt, 0))
    return pl.pallas_call(
        _mla_down_kernel,
        grid=(m // tm,),
        in_specs=[row(k), full(k, n), full(1, Q_LORA), full(1, KV_LORA), pos, pos, pos],
        out_specs=[row(Q_LORA), row(KV_LORA), row(LANES)],
        out_shape=[jax.ShapeDtypeStruct((m, Q_LORA), BF16),
                   jax.ShapeDtypeStruct((m, KV_LORA), BF16),
                   jax.ShapeDtypeStruct((m, LANES), BF16)],
        compiler_params=_cparams("parallel"),
        name="mla_down",
    )(h, w, g_cq.reshape(1, -1), g_ckv.reshape(1, -1), *rope_r)


def _mla_q_kernel(cq_ref, w_ref, c_ref, sa_ref, sb_ref, q_ref, *, scale):
    acc = _dot(cq_ref[...], w_ref[...])
    c, sa, sb = c_ref[...], sa_ref[...], sb_ref[...]
    for hd in range(C_HEADS):
        lo = hd * C_QK
        q_ref[:, lo:lo + LANES] = (acc[:, lo:lo + LANES] * scale).astype(q_ref.dtype)
        rope = _rope64(acc[:, lo + LANES:lo + C_QK], c, sa, sb)
        q_ref[:, lo + LANES:lo + C_QK] = (rope * scale).astype(q_ref.dtype)


def _mla_q(cq, w, rope_r, seq, tm=1024):
    m, k = cq.shape
    n = w.shape[1]
    nt = seq // tm
    pos = pl.BlockSpec((tm, LANES), lambda i: (i % nt, 0))
    return pl.pallas_call(
        functools.partial(_mla_q_kernel, scale=(NOPE_DIM + ROPE_DIM) ** -0.5 * LOG2E),
        grid=(m // tm,),
        in_specs=[pl.BlockSpec((tm, k), lambda i: (i, 0)),
                  pl.BlockSpec((k, n), lambda i: (0, 0)), pos, pos, pos],
        out_specs=pl.BlockSpec((tm, n), lambda i: (i, 0)),
        out_shape=jax.ShapeDtypeStruct((m, n), BF16),
        compiler_params=_cparams("parallel"),
        name="mla_q",
    )(cq, w, *rope_r)


def _store_vt(v, vt_ref):
    vt = v.T
    for hd in range(vt_ref.shape[0]):
        vt_ref[hd, 0:V_DIM, :] = vt[hd * V_DIM:(hd + 1) * V_DIM, :].astype(vt_ref.dtype)
        vt_ref[hd, V_DIM:VT_ROWS, :] = jnp.ones((VT_ROWS - V_DIM, vt.shape[1]), vt_ref.dtype)


def _mla_kv_kernel(ckv_ref, kr_ref, wk_ref, wv_ref, k_ref, vt_ref):
    ckv = ckv_ref[...]
    kn = _dot(ckv, wk_ref[...])
    kr = kr_ref[...]
    for hd in range(C_HEADS):
        lo = hd * C_QK
        k_ref[:, lo:lo + LANES] = kn[:, hd * LANES:(hd + 1) * LANES].astype(k_ref.dtype)
        k_ref[:, lo + LANES:lo + C_QK] = kr
    _store_vt(_dot(ckv, wv_ref[...]), vt_ref)


def _mla_kv(ckv, kr, wk, wv, bsz, seq, tm=1024):
    m, k = ckv.shape
    nt = seq // tm
    return pl.pallas_call(
        _mla_kv_kernel,
        grid=(m // tm,),
        in_specs=[pl.BlockSpec((tm, k), lambda i: (i, 0)),
                  pl.BlockSpec((tm, LANES), lambda i: (i, 0)),
                  pl.BlockSpec(wk.shape, lambda i: (0, 0)),
                  pl.BlockSpec(wv.shape, lambda i: (0, 0))],
        out_specs=[pl.BlockSpec((tm, C_HEADS * C_QK), lambda i: (i, 0)),
                   pl.BlockSpec((None, C_HEADS, VT_ROWS, tm), lambda i: (i // nt, 0, 0, i % nt))],
        out_shape=[jax.ShapeDtypeStruct((m, C_HEADS * C_QK), BF16),
                   jax.ShapeDtypeStruct((bsz, C_HEADS, VT_ROWS, seq), BF16)],
        compiler_params=_cparams("parallel"),
        name="mla_kv",
    )(ckv, kr, wk, wv)


def _attend_chunks(qT, k_ref, vt_ref, scratch, *, tk, n_full, mask_tail, tail_steps, tail_col,
                   unroll):
    m_sc, acc_sc, sa_sc, sb_sc, pa_sc, pb_sc, ala_sc, alb_sc, mxa_sc, mxb_sc = scratch
    s_bufs = (sa_sc, sb_sc)
    p_bufs = (pa_sc, pb_sc)
    al_bufs = (ala_sc, alb_sc)
    mx_bufs = (mxa_sc, mxb_sc)
    last_chunk = k_ref.shape[0] // tk - 1
    m_sc[...] = jnp.full(m_sc.shape, MASKED, F32)
    acc_sc[...] = jnp.zeros(acc_sc.shape, F32)
    for p_ref, al_ref in zip(p_bufs, al_bufs):
        p_ref[...] = jnp.zeros(p_ref.shape, p_ref.dtype)
        al_ref[...] = jnp.ones(al_ref.shape, F32)

    def rows(c):
        return pl.ds(pl.multiple_of(jnp.clip(c, 0, last_chunk) * tk, tk), tk)

    def scores(c, slot, col=0):
        sT = _dot(k_ref[rows(c), :], qT[:, col:])
        s_bufs[slot][:, col:] = sT
        mx_bufs[slot][:, col:] = jnp.max(sT, axis=0, keepdims=True)

    def flush(c, slot, col=0):
        acc_sc[:, col:] = (al_bufs[slot][:, col:] * acc_sc[:, col:]
                           + _dot(vt_ref[:, rows(c)], p_bufs[slot][:, col:]))

    def softmax(sT, top, slot, col):
        m_old = m_sc[:, col:]
        m_new = jnp.maximum(m_old, top)
        al_bufs[slot][:, col:] = jnp.exp2(m_old - m_new)
        p_bufs[slot][:, col:] = jnp.exp2(sT - m_new).astype(p_bufs[slot].dtype)
        m_sc[:, col:] = m_new

    def step(tau, slot, mask, cols):
        col_flush, col, col_next = cols
        flush(tau - 2, slot, col_flush)
        if col_next is not None:
            scores(tau + 1, 1 - slot, col_next)
        sT = s_bufs[slot][:, col:]
        if mask is None:
            softmax(sT, mx_bufs[slot][:, col:], slot, col)
        else:
            sT = mask(sT, tau, col)
            softmax(sT, jnp.max(sT, axis=0, keepdims=True), slot, col)

    def full_steps(tau0, count):
        for j in range(count):
            step(tau0 + j, j % 2, None, (0, 0, 0))

    scores(0, 0)
    trips = n_full // unroll
    lax.fori_loop(0, trips, lambda u, c: (full_steps(unroll * u, unroll), c)[1], 0)
    done = unroll * trips
    pairs = (n_full - done) // 2
    lax.fori_loop(0, pairs, lambda u, c: (full_steps(done + 2 * u, 2), c)[1], 0)
    tau = done + 2 * pairs
    cols = [tail_col(j) for j in range(tail_steps)]
    for j in range(tail_steps):
        col_flush = cols[j - 2] if j >= 2 else 0
        col_next = cols[j + 1] if j + 1 < tail_steps else None
        step(tau + j, j % 2, mask_tail, (col_flush, cols[j], col_next))
    flush(tau + tail_steps - 2, 0, cols[-2])
    flush(tau + tail_steps - 1, 1, cols[-1])
    acc = acc_sc[...]
    return acc[:V_DIM, :] / acc[V_DIM:V_DIM + 1, :]


def _attend_scratch(tq, tk):
    return [pltpu.VMEM((1, tq), F32), pltpu.VMEM((VT_ROWS, tq), F32),
            pltpu.VMEM((tk, tq), F32), pltpu.VMEM((tk, tq), F32),
            pltpu.VMEM((tk, tq), BF16), pltpu.VMEM((tk, tq), BF16),
            pltpu.VMEM((1, tq), F32), pltpu.VMEM((1, tq), F32),
            pltpu.VMEM((1, tq), F32), pltpu.VMEM((1, tq), F32)]


def _transpose_q(q_ref):
    return q_ref[...].astype(F32).T.astype(BF16)


def _flash_kernel(q_ref, k_ref, vt_ref, o_ref, *scratch, tq, tk):
    i = pl.program_id(2)

    def causal(sT, c, col):
        key = lax.broadcasted_iota(jnp.int32, sT.shape, 0) + c * tk
        qry = lax.broadcasted_iota(jnp.int32, sT.shape, 1) + (i * tq + col)
        return jnp.where(key <= qry, sT, MASKED)

    per_tile = tq // tk
    oT = _attend_chunks(_transpose_q(q_ref), k_ref, vt_ref, scratch, tk=tk, n_full=i * per_tile,
                        mask_tail=causal, tail_steps=per_tile, tail_col=lambda j: j * tk,
                        unroll=2)
    o_ref[...] = oT.T.astype(o_ref.dtype)


def _flash_attention(q, k, vt, heads, qk_w, tq=2048, tk=1024):
    bsz, seq, _ = q.shape
    tq, tk = min(tq, seq), min(tk, seq // 2)
    assert tq % (2 * tk) == 0 and seq % tq == 0
    once = pl.Buffered(1)
    return pl.pallas_call(
        functools.partial(_flash_kernel, tq=tq, tk=tk),
        grid=(bsz, heads, seq // tq),
        in_specs=[pl.BlockSpec((None, tq, qk_w), lambda b, h, i: (b, i, h)),
                  pl.BlockSpec((None, seq, qk_w), lambda b, h, i: (b, 0, h), pipeline_mode=once),
                  pl.BlockSpec((None, None, VT_ROWS, seq), lambda b, h, i: (b, h, 0, 0),
                               pipeline_mode=once)],
        out_specs=pl.BlockSpec((None, tq, V_DIM), lambda b, h, i: (b, i, h)),
        out_shape=jax.ShapeDtypeStruct((bsz, seq, heads * V_DIM), BF16),
        scratch_shapes=_attend_scratch(tq, tk),
        compiler_params=_cparams("parallel", "parallel", "arbitrary"),
        name="mla_flash",
    )(q, k, vt)


def _kmean_kernel(k_ref, o_ref):
    k = k_ref[...].astype(F32)
    o_ref[...] = jnp.mean(k.reshape(SUBLANES, MOBA_BLOCK, k.shape[-1]), axis=1)


def _kmean(qk):
    bsz, seq, _ = qk.shape
    rows = SUBLANES * MOBA_BLOCK
    return pl.pallas_call(
        _kmean_kernel,
        grid=(bsz, seq // rows),
        in_specs=[pl.BlockSpec((None, rows, A_W), lambda b, i: (b, i, KA_BLK * LANES // A_W))],
        out_specs=pl.BlockSpec((None, SUBLANES, A_W), lambda b, i: (b, i, 0)),
        out_shape=jax.ShapeDtypeStruct((bsz, seq // MOBA_BLOCK, A_W), F32),
        compiler_params=_cparams("parallel", "parallel"),
        name="moba_kmean",
    )(qk)


def _block_attention(q, k, v, visible=None):
    s = _dot_nt(q, k)
    if visible is not None:
        s = jnp.where(visible, s, MASKED)
    m = jnp.max(s, axis=-1, keepdims=True)
    p = jnp.exp2(s - m).astype(BF16)
    v_ones = jnp.concatenate([v, jnp.ones((v.shape[0], LANES), v.dtype)], axis=-1)
    acc = _dot(p, v_ones)
    den = acc[:, V_DIM:]
    return acc[:, :V_DIM] / den, m + jnp.log2(den)


def _moba_gate_kernel(q_ref, km_ref, ids_ref, cnt_ref, qf_ref):
    t = MOBA_BLOCK
    i = pl.program_id(1)
    nb = km_ref.shape[0]
    blk = lax.broadcasted_iota(jnp.int32, (nb, t), 0)
    neg_inf = jnp.float32(-jnp.inf)
    not_after = (lax.broadcasted_iota(jnp.int32, (t, t), 0)
                 <= lax.broadcasted_iota(jnp.int32, (t, t), 1))
    upper = jnp.where(not_after, 1.0, 0.0).astype(BF16)
    ones = jnp.ones((SUBLANES, t), BF16)
    for hd in range(A_HEADS):
        sl = slice(hd * HEAD_DIM, (hd + 1) * HEAD_DIM)
        q = q_ref[:, sl].astype(F32)
        qf_ref[hd] = q
        qT = q.T.astype(BF16)
        km = km_ref[:, sl]
        km_hi = km.astype(BF16)
        km_lo = (km - km_hi.astype(F32)).astype(BF16)
        g = jnp.where(blk < i, _dot(km_hi, qT) + _dot(km_lo, qT), neg_inf)
        picks, ranks, counts = [], [], []
        for _ in range(MOBA_TOPK):
            mx = jnp.max(g, axis=0, keepdims=True)
            is_max = (g == mx) & (mx > neg_inf)
            first = jnp.min(jnp.where(is_max, blk, nb), axis=0, keepdims=True)
            pick = blk == first
            g = jnp.where(pick, neg_inf, g)
            onehot = jnp.where(pick, 1.0, 0.0).astype(BF16)
            before = _dot(onehot, upper)
            rank = jnp.sum(jnp.where(pick, before - 1.0, 0.0), axis=0, keepdims=True)
            picks.append(first)
            ranks.append(rank.astype(jnp.int32))
            counts.append(_dot_nt(ones, onehot)[0:1, :])
        pad_i = jnp.zeros((SUBLANES - 2 * MOBA_TOPK, t), jnp.int32)
        ids_ref[hd] = jnp.concatenate(picks + ranks + [pad_i], axis=0)
        pad_f = jnp.zeros((SUBLANES - MOBA_TOPK, nb), F32)
        cnt_ref[hd] = jnp.concatenate(counts + [pad_f], axis=0)


def _moba_gate(qk, kmean):
    bsz, seq, _ = qk.shape
    t = MOBA_BLOCK
    nb = seq // t
    return pl.pallas_call(
        _moba_gate_kernel,
        grid=(bsz, nb),
        in_specs=[pl.BlockSpec((None, t, A_W), lambda b, i: (b, i, QA_BLK * LANES // A_W)),
                  pl.BlockSpec((None, nb, A_W), lambda b, i: (b, 0, 0))],
        out_specs=[pl.BlockSpec((None, A_HEADS, SUBLANES, t), lambda b, i: (b, 0, 0, i)),
                   pl.BlockSpec((None, A_HEADS, None, SUBLANES, nb), lambda b, i: (b, 0, i, 0, 0)),
                   pl.BlockSpec((None, A_HEADS, t, HEAD_DIM), lambda b, i: (b, 0, i, 0))],
        out_shape=[jax.ShapeDtypeStruct((bsz, A_HEADS, SUBLANES, seq), jnp.int32),
                   jax.ShapeDtypeStruct((bsz, A_HEADS, nb, SUBLANES, nb), F32),
                   jax.ShapeDtypeStruct((bsz, A_HEADS, seq, HEAD_DIM), F32)],
        compiler_params=_cparams("parallel", "parallel"),
        name="moba_gate",
    )(qk, kmean)


def _moba_routes(ids, cnt, seq):
    bsz, heads = ids.shape[:2]
    bh, t = bsz * heads, MOBA_BLOCK
    nb = seq // t
    tiles = _moba_tiles(seq)
    picks = ids[:, :, 0:MOBA_TOPK, :].reshape(bh, MOBA_TOPK, nb, t)
    ranks = ids[:, :, MOBA_TOPK:2 * MOBA_TOPK, :].reshape(bh, MOBA_TOPK, nb, t)
    per_tile = cnt[:, :, :, 0:MOBA_TOPK, :].astype(jnp.int32).reshape(bh, nb * MOBA_TOPK, nb)
    before = jnp.cumsum(per_tile, axis=1) - per_tile
    total = jnp.sum(per_tile, axis=1)
    padded = -(-total // t) * t
    ends = jnp.cumsum(padded, axis=1)
    base = before + (ends - padded)[:, None, :]
    base = base.reshape(bh, nb, MOBA_TOPK, nb).transpose(0, 2, 1, 3)
    onehot = picks[..., None] == jnp.arange(nb)
    pos = jnp.sum(jnp.where(onehot, base[:, :, :, None, :], 0), axis=-1) + ranks
    pos = jnp.where(picks < nb, pos, (tiles - 1) * t)
    pos = pos + (jnp.arange(bh, dtype=jnp.int32) * (tiles * t))[:, None, None, None]
    pos = pos.reshape(bh, MOBA_TOPK, seq).transpose(1, 0, 2).reshape(MOBA_TOPK, bh * seq)
    tile_start = jnp.arange(tiles, dtype=jnp.int32) * t
    tile_blk = jnp.sum(tile_start[None, :, None] >= ends[:, None, :], axis=-1)
    tile_blk = jnp.where(tile_start[None, :] < ends[:, -1:], tile_blk, -1)
    return pos.astype(jnp.int32), tile_blk.astype(jnp.int32)


def _moba_tiles(seq):
    nb = seq // MOBA_BLOCK
    return -(-(MOBA_TOPK * nb + nb + 1) // GROUP_STEP) * GROUP_STEP


def _sc_mesh():
    return plsc.VectorSubcoreMesh(core_axis_name="core", subcore_axis_name="subcore")


def _sc_scatter_rows(x, idx, rows):
    slots, n = idx.shape
    d = x.shape[1]

    @pl.kernel(out_type=jax.ShapeDtypeStruct((rows, d), x.dtype), mesh=_sc_mesh())
    def scatter(x_hbm, i_hbm, o_hbm):
        def body(x_vmem, i_vmem):
            pltpu.sync_copy(x_vmem, o_hbm.at[i_vmem.at[0]])

        pltpu.emit_pipeline(
            body, grid=(slots, n // SC_WINDOW),
            in_specs=[pl.BlockSpec((SC_WINDOW, d), lambda s, i: (i, 0)),
                      pl.BlockSpec((1, SC_WINDOW), lambda s, i: (s, i))],
            out_specs=[],
            core_axis_name=("core", "subcore"),
            dimension_semantics=(pltpu.PARALLEL, pltpu.PARALLEL),
        )(x_hbm, i_hbm)

    return scatter(x, idx)


def _sc_gather_rows(x, idx):
    n = idx.shape[0]
    d = x.shape[1]

    @pl.kernel(out_type=jax.ShapeDtypeStruct((n, d), x.dtype), mesh=_sc_mesh())
    def gather(x_hbm, i_hbm, o_hbm):
        def body(i_vmem, o_vmem):
            pltpu.sync_copy(x_hbm.at[i_vmem.at[0]], o_vmem)

        pltpu.emit_pipeline(
            body, grid=(n // SC_WINDOW,),
            in_specs=[pl.BlockSpec((1, SC_WINDOW), lambda i: (0, i))],
            out_specs=[pl.BlockSpec((SC_WINDOW, d), lambda i: (i, 0))],
            core_axis_name=("core", "subcore"),
            dimension_semantics=(pltpu.PARALLEL,),
        )(i_hbm, o_hbm)

    return gather(x, idx.reshape(1, n))


def _moba_group_kernel(tb_ref, q_ref, *refs):
    t = MOBA_BLOCK
    k_refs, v_refs = refs[:GROUP_STEP], refs[GROUP_STEP:2 * GROUP_STEP]
    o_ref, lse_ref = refs[-2:]
    g, step = pl.program_id(0), pl.program_id(1)
    first = step * GROUP_STEP

    @pl.when(tb_ref[g, first] < 0)
    def _():
        o_ref[...] = jnp.zeros(o_ref.shape, o_ref.dtype)
        lse_ref[...] = jnp.full(lse_ref.shape, MASKED, lse_ref.dtype)

    @pl.when(tb_ref[g, first] >= 0)
    def _():
        for u in range(GROUP_STEP):
            used = tb_ref[g, first + u] >= 0
            rows = slice(u * t, (u + 1) * t)
            o, lse = _block_attention(q_ref[rows, :].astype(BF16), k_refs[u][...], v_refs[u][...])
            o_ref[rows, :] = jnp.where(used, o, 0.0)
            lse_ref[rows, :] = jnp.where(used, lse, MASKED)


def _moba_group_attention(q_grouped, tile_blk, qk, v, after):
    bh, rows, _ = q_grouped.shape
    t = MOBA_BLOCK
    tiles = rows // t
    heads = A_HEADS

    def block_of(u, first_col):
        return lambda g, s, tb: (g // heads, jnp.maximum(tb[g, s * GROUP_STEP + u], 0),
                                 first_col + g % heads)

    row_tile = pl.BlockSpec((None, GROUP_STEP * t, HEAD_DIM), lambda g, s, tb: (g, s, 0))
    key_value = lambda first_col: [pl.BlockSpec((None, t, HEAD_DIM), block_of(u, first_col))
                                   for u in range(GROUP_STEP)]
    grid_spec = pltpu.PrefetchScalarGridSpec(
        num_scalar_prefetch=1,
        grid=(bh, tiles // GROUP_STEP),
        in_specs=([row_tile] + key_value(KA_BLK) + key_value(0)
                  + [pl.BlockSpec(memory_space=pl.ANY)]),
        out_specs=[row_tile, row_tile],
    )
    return pl.pallas_call(
        _moba_group_kernel,
        grid_spec=grid_spec,
        out_shape=[jax.ShapeDtypeStruct(q_grouped.shape, F32)] * 2,
        compiler_params=_cparams("parallel", "parallel"),
        name="moba_group",
    )(tile_blk, q_grouped, *([qk] * GROUP_STEP), *([v] * GROUP_STEP), after)


def _moba_merge_kernel(q_ref, k_ref, v_ref, po_ref, pl_ref, o_ref):
    t = MOBA_BLOCK
    causal = (lax.broadcasted_iota(jnp.int32, (t, t), 1)
              <= lax.broadcasted_iota(jnp.int32, (t, t), 0))
    for hd in range(A_HEADS):
        sl = slice(hd * HEAD_DIM, (hd + 1) * HEAD_DIM)
        o_own, lse_own = _block_attention(q_ref[:, sl], k_ref[:, sl], v_ref[:, sl], causal)
        outs = [o_own] + [po_ref[s, hd] for s in range(MOBA_TOPK)]
        lses = [lse_own] + [pl_ref[s, hd] for s in range(MOBA_TOPK)]
        top = functools.reduce(jnp.maximum, lses)
        weights = [jnp.exp2(l - top) for l in lses]
        num = sum(w * o for w, o in zip(weights, outs))
        o_ref[:, sl] = (num / sum(weights)).astype(o_ref.dtype)


def _moba_merge(qk, v, part_o, part_lse):
    bsz, seq, _ = qk.shape
    t = MOBA_BLOCK
    part = pl.BlockSpec((MOBA_TOPK, None, A_HEADS, t, HEAD_DIM), lambda b, i: (0, b, 0, i, 0))
    return pl.pallas_call(
        _moba_merge_kernel,
        grid=(bsz, seq // t),
        in_specs=[pl.BlockSpec((None, t, A_W), lambda b, i: (b, i, QA_BLK * LANES // A_W)),
                  pl.BlockSpec((None, t, A_W), lambda b, i: (b, i, KA_BLK * LANES // A_W)),
                  pl.BlockSpec((None, t, A_W), lambda b, i: (b, i, 0)),
                  part, part],
        out_specs=pl.BlockSpec((None, t, A_W), lambda b, i: (b, i, 0)),
        out_shape=jax.ShapeDtypeStruct((bsz, seq, A_W), BF16),
        compiler_params=_cparams("parallel", "parallel"),
        name="moba_merge",
    )(qk, qk, v, part_o, part_lse)


def _moba_regroup(qk, kmean):
    bsz, seq, _ = qk.shape
    bh = bsz * A_HEADS
    rows = _moba_tiles(seq) * MOBA_BLOCK
    ids, cnt, q_f32 = _moba_gate(qk, kmean)
    pos, tile_blk = _moba_routes(ids, cnt, seq)
    q_grouped = _sc_scatter_rows(q_f32.reshape(bh * seq, HEAD_DIM), pos, bh * rows)
    return q_grouped.reshape(bh, rows, HEAD_DIM), pos, tile_blk


def _moba_picked_blocks(q_grouped, pos, tile_blk, qk, v, after):
    bsz, seq, _ = qk.shape
    bh, rows, _ = q_grouped.shape
    o_g, lse_g = _moba_group_attention(q_grouped, tile_blk, qk, v, after)
    flat = pos.reshape(-1)
    back = lambda a: _sc_gather_rows(a.reshape(bh * rows, HEAD_DIM), flat).reshape(
        MOBA_TOPK, bsz, A_HEADS, seq, HEAD_DIM)
    return back(o_g), back(lse_g)


def _proj_dilated_kernel(h_ref, w_ref, c_ref, s_ref, q_ref, k_ref, v_ref, sc, *, d):
    acc = _dot(h_ref[...], w_ref[...])
    c, s = c_ref[...], s_ref[...]
    q_scale = HEAD_DIM ** -0.5 * LOG2E
    for j in range(acc.shape[1] // LANES):
        blk = acc[:, j * LANES:(j + 1) * LANES]
        if j < B_HEADS:
            blk = _rope128(blk, c, s) * q_scale
        elif j < 2 * B_HEADS:
            blk = _rope128(blk, c, s)
        sc[j] = blk
    rows = acc.shape[0] // d
    for r in range(d):
        for j in range(acc.shape[1] // LANES):
            dst = (q_ref, k_ref, v_ref)[j // B_HEADS]
            col = (j % B_HEADS) * LANES
            dst[r, :, col:col + LANES] = sc[j, pl.ds(r, rows, stride=d), :].astype(dst.dtype)


def _proj_dilated(h, w, rope_h, d, bsz, seq, tm=1024):
    m, k = h.shape
    nt = seq // tm
    pos = pl.BlockSpec((tm, LANES), lambda i: (i % nt, 0))
    out = pl.BlockSpec((None, d, tm // d, B_W), lambda i: (i // nt, 0, i % nt, 0))
    return pl.pallas_call(
        functools.partial(_proj_dilated_kernel, d=d),
        grid=(m // tm,),
        in_specs=[pl.BlockSpec((tm, k), lambda i: (i, 0)), pl.BlockSpec(w.shape, lambda i: (0, 0)),
                  pos, pos],
        out_specs=[out] * 3,
        out_shape=[jax.ShapeDtypeStruct((bsz, d, seq // d, B_W), BF16)] * 3,
        scratch_shapes=[pltpu.VMEM((w.shape[1] // LANES, tm, LANES), F32)],
        compiler_params=_cparams("parallel"),
        name=f"proj_dilated_d{d}",
    )(h, w, *rope_h)


def _dilated_kernel(q_ref, kc_ref, kp_ref, vc_ref, vp_ref, o_ref, lse_ref, *, span):
    t, tp = q_ref.shape[0], kp_ref.shape[0]
    i = pl.program_id(2)
    shape = (2 * tp, tp)
    key_row = lax.broadcasted_iota(jnp.int32, shape, 0)
    dist = lax.broadcasted_iota(jnp.int32, shape, 1) + tp - key_row
    visible = (dist >= 0) & (dist <= span)
    bias = jnp.where(visible, 0.0, MASKED)
    bias_first = jnp.where(visible & ((key_row >= tp) | (i > 0)), 0.0, MASKED)
    ones = jnp.ones((BF16_ROWS, tp + t), BF16)

    def transposed(x):
        return x.astype(F32).T.astype(BF16)

    for j in range(B_HEADS):
        sl = slice(j * LANES, (j + 1) * LANES)
        qT = transposed(q_ref[:, sl])
        k_all = jnp.concatenate([kp_ref[:, sl], kc_ref[:, sl]], axis=0)
        vt_all = jnp.concatenate([transposed(vp_ref[:, sl]), transposed(vc_ref[:, sl])], axis=1)
        vt_all = jnp.concatenate([vt_all, ones], axis=0)
        outs, lses = [], []
        for u in range(t // tp):
            window = slice(u * tp, (u + 2) * tp)
            s = _dot(k_all[window, :], qT[:, u * tp:(u + 1) * tp])
            s = s + (bias_first if u == 0 else bias)
            m = jnp.max(s, axis=0, keepdims=True)
            p = jnp.exp2(s - m).astype(BF16)
            acc = _dot(vt_all[:, window], p)
            den = acc[V_DIM:V_DIM + 1, :]
            outs.append(acc[:V_DIM, :] / den)
            lses.append(m + jnp.log2(den))
        o_ref[:, sl] = jnp.concatenate(outs, axis=1).T
        lse = jnp.concatenate(lses, axis=1)
        lse_ref[:, sl] = jnp.broadcast_to(lse, (LANES, t)).T


def _dilated_attention(q, k, v, span, t=1024):
    bsz, d, length, _ = q.shape
    t = min(t, length)
    tp = B_QBLOCK
    assert span <= tp and t % tp == 0
    cur = pl.BlockSpec((None, None, t, B_W), lambda b, r, i: (b, r, i, 0))
    prev = pl.BlockSpec((None, None, tp, B_W),
                        lambda b, r, i: (b, r, jnp.maximum(i * (t // tp) - 1, 0), 0))
    return pl.pallas_call(
        functools.partial(_dilated_kernel, span=span),
        grid=(bsz, d, length // t),
        in_specs=[cur, cur, prev, cur, prev],
        out_specs=[cur, cur],
        out_shape=[jax.ShapeDtypeStruct(q.shape, F32)] * 2,
        compiler_params=_cparams("parallel", "parallel", "parallel"),
        name=f"dilated_d{d}",
    )(q, k, k, v, v)


def _natural_rows(ref, sc):
    d, rows = ref.shape[0], ref.shape[1]
    if d == 1:
        return ref[0]
    for r in range(d):
        for j in range(B_HEADS):
            sc[j, pl.ds(r, rows, stride=d), :] = ref[r, :, j * LANES:(j + 1) * LANES]
    return jnp.concatenate([sc[j] for j in range(B_HEADS)], axis=-1)


def _mixer_tail_kernel(x_ref, oa_ref, o0_ref, o1_ref, o2_ref, l0_ref, l1_ref, l2_ref, oc_ref,
                       g_ref, wpa_ref, wpb_ref, wpc_ref, wo_ref, y_ref, *scratch):
    o0, o1, o2, l0, l1, l2 = [
        _natural_rows(ref, sc)
        for ref, sc in zip((o0_ref, o1_ref, o2_ref, l0_ref, l1_ref, l2_ref), scratch)]
    mx = jnp.maximum(jnp.maximum(l0, l1), l2)
    e0, e1, e2 = jnp.exp2(l0 - mx), jnp.exp2(l1 - mx), jnp.exp2(l2 - mx)
    ob = (e0 * o0 + e1 * o1 + e2 * o2) / (e0 + e1 + e2)
    pa = _dot(oa_ref[...], wpa_ref[...])
    pb = _dot(ob.astype(BF16), wpb_ref[...])
    pc = _dot(oc_ref[...], wpc_ref[...])
    d = D_MODEL
    merged = (g_ref[:, 0:d].astype(F32) * pa + g_ref[:, d:2 * d].astype(F32) * pb
              + g_ref[:, 2 * d:3 * d].astype(F32) * pc)
    y_ref[...] = x_ref[...] + _dot(merged.astype(BF16), wo_ref[...])


def _mixer_tail(x, out_a, o_groups, lse_groups, out_c, gates, w_pa, w_pb, w_pc, w_o, seq, tm=256):
    m, d = x.shape
    nt = seq // tm
    row = lambda width: pl.BlockSpec((tm, width), lambda i: (i, 0))
    residue = lambda g: pl.BlockSpec((None, g.shape[1], tm // g.shape[1], B_W),
                                     lambda i: (i // nt, 0, i % nt, 0))
    weights = [_resident(w) for w in (w_pa, w_pb, w_pc, w_o)]
    groups = list(o_groups) + list(lse_groups)
    return pl.pallas_call(
        _mixer_tail_kernel,
        grid=(m // tm,),
        in_specs=([row(d), row(A_W)] + [residue(g) for g in groups]
                  + [row(C_W), row(3 * d)] + weights),
        out_specs=row(d),
        out_shape=jax.ShapeDtypeStruct((m, d), F32),
        scratch_shapes=[pltpu.VMEM((B_HEADS, tm, LANES), F32) for _ in groups],
        compiler_params=_cparams("parallel"),
        name="mixer_tail",
    )(x, out_a, *groups, out_c, gates, w_pa, w_pb, w_pc, w_o)


def _mem_kv_kernel(mem_ref, g_ref, wk_ref, wv_ref, k_ref, v_ref):
    memn = _rms(mem_ref[...], g_ref[...]).astype(BF16)
    k_ref[...] = _dot(memn, wk_ref[...]).astype(k_ref.dtype)
    v_ref[...] = _dot(memn, wv_ref[...]).astype(v_ref.dtype)


def _mem_kv(mem, g, wk, wv):
    bsz, n, d = mem.shape
    out = pl.BlockSpec((None, n, X_W), lambda b: (b, 0, 0))
    return pl.pallas_call(
        _mem_kv_kernel,
        grid=(bsz,),
        in_specs=[pl.BlockSpec((None, n, d), lambda b: (b, 0, 0)),
                  pl.BlockSpec((1, d), lambda b: (0, 0)),
                  pl.BlockSpec(wk.shape, lambda b: (0, 0)),
                  pl.BlockSpec(wv.shape, lambda b: (0, 0))],
        out_specs=[out, out],
        out_shape=[jax.ShapeDtypeStruct((bsz, n, X_W), BF16)] * 2,
        compiler_params=_cparams("parallel"),
        name="mem_kv",
    )(mem, g.reshape(1, d), wk, wv)


def _mem_attn_kernel(x_ref, g_ref, wq_ref, k_ref, v_ref, wo_ref, y_ref):
    x = x_ref[...]
    h = _rms(x, g_ref[...]).astype(BF16)
    q = (_dot(h, wq_ref[...]) * HEAD_DIM ** -0.5).astype(BF16)
    heads = []
    for hd in range(X_HEADS):
        sl = slice(hd * HEAD_DIM, (hd + 1) * HEAD_DIM)
        s = _dot_nt(q[:, sl], k_ref[:, sl])
        p = jnp.exp(s - jnp.max(s, axis=-1, keepdims=True))
        o = _dot(p.astype(BF16), v_ref[:, sl]) / jnp.sum(p, axis=-1, keepdims=True)
        heads.append(o.astype(BF16))
    y_ref[...] = x + _dot(jnp.concatenate(heads, axis=-1), wo_ref[...])


def _mem_attention(x, g, wq, kmem, vmem, wo, seq, tm=1024):
    m, d = x.shape
    nt = seq // tm
    n = kmem.shape[1]
    kv = pl.BlockSpec((None, n, X_W), lambda i: (i // nt, 0, 0))
    return pl.pallas_call(
        _mem_attn_kernel,
        grid=(m // tm,),
        in_specs=[pl.BlockSpec((tm, d), lambda i: (i, 0)),
                  pl.BlockSpec((1, d), lambda i: (0, 0)),
                  pl.BlockSpec(wq.shape, lambda i: (0, 0)), kv, kv,
                  pl.BlockSpec(wo.shape, lambda i: (0, 0))],
        out_specs=pl.BlockSpec((tm, d), lambda i: (i, 0)),
        out_shape=jax.ShapeDtypeStruct((m, d), F32),
        compiler_params=_cparams("parallel"),
        name="mem_attention",
    )(x, g.reshape(1, d), wq, kmem, vmem, wo)


def _ffn_up_kernel(x_ref, halo_ref, g_ref, wg_ref, wv_ref, cwg_ref, cwv_ref, cbg_ref, cbv_ref,
                   act_ref, h_sc, *, tiles_per_seq):
    i = pl.program_id(0)
    tm = x_ref.shape[0]

    @pl.when(pl.program_id(1) == 0)
    def _():
        g = g_ref[...]
        keep = (i % tiles_per_seq != 0).astype(F32)
        h_sc[0:HALO, :] = (_rms(halo_ref[...], g) * keep).astype(h_sc.dtype)
        h_sc[HALO:, :] = _rms(x_ref[...], g).astype(h_sc.dtype)

    h = h_sc[...]

    def conv(w_ref, cw_ref, cb_ref):
        u = _dot(h, w_ref[...])
        c = cb_ref[...]
        for tap in range(CONV_W):
            lo = HALO - (CONV_W - 1) + tap
            c = c + cw_ref[tap:tap + 1, :] * u[lo:lo + tm, :]
        return c

    act = jax.nn.silu(conv(wg_ref, cwg_ref, cbg_ref)) * conv(wv_ref, cwv_ref, cbv_ref)
    act_ref[...] = act.astype(act_ref.dtype)


def _ffn_down_kernel(a_ref, w_ref, x_ref, y_ref):
    y_ref[...] = x_ref[...] + _dot(a_ref[...], w_ref[...])


def _conv_ffn(x, g, w_up, conv_w, conv_b, w_down, layer, seq):
    m, d = x.shape
    act = _ffn_up(x, g, w_up, conv_w, conv_b, layer, seq)
    tm, tn = 1024, FFN_TF
    return pl.pallas_call(
        _ffn_down_kernel,
        grid=(m // tm, d // tn),
        in_specs=[pl.BlockSpec((tm, D_FF), lambda i, j: (i, 0)),
                  pl.BlockSpec((None, D_FF, tn), lambda i, j: (layer, 0, j)),
                  pl.BlockSpec((tm, tn), lambda i, j: (i, j))],
        out_specs=pl.BlockSpec((tm, tn), lambda i, j: (i, j)),
        out_shape=jax.ShapeDtypeStruct((m, d), F32),
        compiler_params=_cparams("parallel", "parallel"),
        name="ffn_down",
    )(act, w_down, x)


def _ffn_up(x, g, w_up, conv_w, conv_b, layer, seq, tm=1024, tf=FFN_TF):
    m, d = x.shape
    nf = D_FF_PAD // tf
    halo_blocks = tm // HALO
    return pl.pallas_call(
        functools.partial(_ffn_up_kernel, tiles_per_seq=seq // tm),
        grid=(m // tm, nf),
        in_specs=[pl.BlockSpec((tm, d), lambda i, f: (i, 0)),
                  pl.BlockSpec((HALO, d), lambda i, f: (jnp.maximum(i * halo_blocks - 1, 0), 0)),
                  pl.BlockSpec((1, d), lambda i, f: (0, 0)),
                  pl.BlockSpec((None, d, tf), lambda i, f: (layer, 0, f)),
                  pl.BlockSpec((None, d, tf), lambda i, f: (layer, 0, f + nf)),
                  pl.BlockSpec((CONV_W, tf), lambda i, f: (0, f)),
                  pl.BlockSpec((CONV_W, tf), lambda i, f: (0, f + nf)),
                  pl.BlockSpec((1, tf), lambda i, f: (0, f)),
                  pl.BlockSpec((1, tf), lambda i, f: (0, f + nf))],
        out_specs=pl.BlockSpec((tm, tf), lambda i, f: (i, f)),
        out_shape=jax.ShapeDtypeStruct((m, D_FF_PAD), BF16),
        scratch_shapes=[pltpu.VMEM((HALO + tm, d), BF16)],
        compiler_params=_cparams("parallel", "arbitrary"),
        name="ffn_up",
    )(x, x, g.reshape(1, d), w_up, w_up, conv_w, conv_w, conv_b, conv_b)


def _rope_tables(seq):
    def angles(dim):
        inv_freq = jnp.exp(jnp.arange(0, dim, 2, dtype=F32) * (-math.log(ROPE_THETA) / dim))
        ang = jnp.arange(seq, dtype=F32)[:, None] * inv_freq[None, :]
        return jnp.cos(ang), jnp.sin(ang)

    cos_h, sin_h = angles(HEAD_DIM)
    rope_h = (jnp.concatenate([cos_h, cos_h], axis=-1), jnp.concatenate([-sin_h, sin_h], axis=-1))
    cos_r, sin_r = angles(ROPE_DIM)
    z = jnp.zeros_like(cos_r)
    rope_r = (jnp.concatenate([cos_r, cos_r, z, z], axis=-1),
              jnp.concatenate([-sin_r, z, z, z], axis=-1),
              jnp.concatenate([z, sin_r, z, z], axis=-1))
    return rope_h, rope_r


def _split_in(w_in):
    return [w_in[:, IN_OFFSETS[k]:IN_OFFSETS[k + 1]] for k in range(len(IN_WIDTHS))]


def _pad_cols(w, width):
    return jnp.pad(w, ((0, 0), (0, width - w.shape[1])))


def _layer_params(w_in, w_uq, w_ukv, conv_w, conv_b):
    qa, ka, va, qb, kb, vb, cq, ckv, kr, gates = _split_in(w_in)
    w_qk = jnp.concatenate([qa, ka], axis=1).astype(BF16)
    group_cols = lambda w, g: w[:, g * B_W:(g + 1) * B_W]
    w_b = [jnp.concatenate([group_cols(qb, g), group_cols(kb, g), group_cols(vb, g)],
                           axis=1).astype(BF16) for g in range(len(B_GROUPS))]
    w_down_in = jnp.concatenate([cq, ckv, _pad_cols(kr, LANES)], axis=1).astype(BF16)
    uq = w_uq.reshape(Q_LORA, C_HEADS, NOPE_DIM + ROPE_DIM)
    uq = jnp.pad(uq, ((0, 0), (0, 0), (0, C_QK - NOPE_DIM - ROPE_DIM)))
    ukv = w_ukv.reshape(KV_LORA, C_HEADS, NOPE_DIM + V_DIM)
    return dict(
        w_qk=w_qk, w_va=va.astype(BF16), w_b=w_b, w_gates=gates.astype(BF16),
        w_down_in=w_down_in,
        w_uq=uq.reshape(Q_LORA, C_HEADS * C_QK).astype(BF16),
        w_uk=ukv[:, :, :NOPE_DIM].reshape(KV_LORA, C_HEADS * NOPE_DIM).astype(BF16),
        w_uv=ukv[:, :, NOPE_DIM:].reshape(KV_LORA, C_W).astype(BF16),
        conv_w=_pad_ff_halves(conv_w),
        conv_b=_pad_ff_halves(conv_b.reshape(1, -1)),
    )


def _pad_ff_halves(w):
    pad = [(0, 0)] * (w.ndim - 1) + [(0, D_FF_PAD - D_FF)]
    return jnp.concatenate([jnp.pad(w[..., :D_FF], pad), jnp.pad(w[..., D_FF:], pad)], axis=-1)


def _qk_col_scale():
    q_scale = HEAD_DIM ** -0.5
    parts = [jnp.full((A_W,), q_scale * LOG2E, F32), jnp.ones((A_W,), F32)]
    return jnp.concatenate(parts).reshape(1, QK_W)


def _mixer(x, g_mix, p, g_cq, g_ckv, w_pa, w_pb, w_pc, w_o, rope_h, rope_r, bsz, seq):
    m = x.shape[0]
    h = _rmsnorm(x, g_mix, BF16)
    qk = _matmul(h, p["w_qk"], _mm_rope_kernel, BF16, 1024, 1024, seq=seq,
                 extras=(("col", _qk_col_scale()), ("pos", rope_h[0]), ("pos", rope_h[1])),
                 name="proj_qk_rope")
    qk3 = qk.reshape(bsz, seq, QK_W)
    q_grouped, pos, tile_blk = _moba_regroup(qk3, _kmean(qk3))
    v_a = _matmul(h, p["w_va"], _mm_plain_kernel, BF16, 1024, A_W, name="proj_va")
    gates = _matmul(h, p["w_gates"], _mm_sigmoid_kernel, BF16, 1024, 1024, name="proj_gates")
    v_a3 = v_a.reshape(bsz, seq, A_W)
    part_o, part_lse = _moba_picked_blocks(q_grouped, pos, tile_blk, qk3, v_a3, after=gates)
    cq, ckv, kr = _mla_down(h, p["w_down_in"], g_cq, g_ckv, rope_r, seq)
    q_c = _mla_q(cq, p["w_uq"], rope_r, seq)
    k_c, vt_c = _mla_kv(ckv, kr, p["w_uk"], p["w_uv"], bsz, seq)
    groups = []
    for (window, d), w_g in zip(B_GROUPS, p["w_b"]):
        q_g, k_g, v_g = _proj_dilated(h, w_g, rope_h, d, bsz, seq)
        groups.append(_dilated_attention(q_g, k_g, v_g, window // d))
    out_c = _flash_attention(q_c.reshape(bsz, seq, -1), k_c.reshape(bsz, seq, -1), vt_c,
                             C_HEADS, C_QK).reshape(m, C_W)
    out_a = _moba_merge(qk3, v_a3, part_o, part_lse).reshape(m, A_W)
    return _mixer_tail(x, out_a, [g[0] for g in groups], [g[1] for g in groups], out_c, gates,
                       w_pa.astype(BF16), w_pb.astype(BF16), w_pc.astype(BF16), w_o.astype(BF16),
                       seq)


def kernel(x, mem, g_mix, w_in, g_cq, g_ckv, w_uq, w_ukv, w_pa, w_pb, w_pc, w_o, g_mem, g_memkv,
           w_xq, w_xk, w_xv, w_xo, g_ffn, w_up, conv_w, conv_b, w_down, g_final):
    bsz, seq, d = x.shape
    rope_h, rope_r = _rope_tables(seq)
    xf = x.reshape(bsz * seq, d)
    w_down = w_down.astype(BF16)
    w_up = _pad_ff_halves(w_up.astype(BF16))
    for l in range(DEPTH):
        p = _layer_params(w_in[l], w_uq[l], w_ukv[l], conv_w[l], conv_b[l])
        xf = _mixer(xf, g_mix[l], p, g_cq[l], g_ckv[l], w_pa[l], w_pb[l], w_pc[l], w_o[l],
                    rope_h, rope_r, bsz, seq)
        kmem, vmem = _mem_kv(mem, g_memkv[l], w_xk[l].astype(BF16), w_xv[l].astype(BF16))
        xf = _mem_attention(xf, g_mem[l], w_xq[l].astype(BF16), kmem, vmem,
                            w_xo[l].astype(BF16), seq)
        xf = _conv_ffn(xf, g_ffn[l], w_up, p["conv_w"], p["conv_b"], w_down, l, seq)
    return _rmsnorm(xf, g_final, F32).reshape(bsz, seq, d)
```

```python
import functools
import math

import jax
import jax.numpy as jnp
import numpy as np
from jax import lax
from jax.experimental import pallas as pl
from jax.experimental.pallas import tpu as pltpu
from jax.experimental.pallas import tpu_sc as plsc

F32 = jnp.float32
BF16 = jnp.bfloat16

LANES = 128
SUBLANES = 8
V7X_VMEM_BYTES = 64 * 1024 * 1024
VMEM_LIMIT = V7X_VMEM_BYTES * 7 // 8

D_MODEL = 2048
DEPTH = 2
HEAD_DIM = 128
ROPE_THETA = 10000.0
EPS = 1e-6

A_HEADS = 4
MOBA_BLOCK = 256
MOBA_TOPK = 3

B_GROUPS = ((128, 1), (512, 4), (2048, 16))
B_HEADS = 4
B_QBLOCK = 128

C_HEADS = 8
Q_LORA = 1536
KV_LORA = 512
NOPE_DIM = 128
ROPE_DIM = 64
V_DIM = 128

X_HEADS = 4
D_FF = 5504
CONV_W = 3

A_W = A_HEADS * HEAD_DIM
B_QKV_W = len(B_GROUPS) * B_HEADS * HEAD_DIM
B_W = B_HEADS * HEAD_DIM
C_W = C_HEADS * V_DIM
X_W = X_HEADS * HEAD_DIM
IN_WIDTHS = (A_W, A_W, A_W, B_QKV_W, B_QKV_W, B_QKV_W, Q_LORA, KV_LORA, ROPE_DIM, 3 * D_MODEL)
IN_OFFSETS = tuple(int(o) for o in np.cumsum((0,) + IN_WIDTHS))

QK_W = 2 * A_W
QA_BLK, KA_BLK = 0, A_W // LANES

C_QK = 2 * LANES
MASKED = -1e30
LOG2E = math.log2(math.e)
BF16_ROWS = 16
VT_ROWS = V_DIM + BF16_ROWS
GROUP_STEP = 16
SC_WINDOW = 128

FFN_TF = 512
D_FF_PAD = -(-D_FF // FFN_TF) * FFN_TF
HALO = SUBLANES


def _cparams(*sem):
    return pltpu.CompilerParams(dimension_semantics=sem, vmem_limit_bytes=VMEM_LIMIT)


def _resident(arr):
    zeros = (0,) * arr.ndim
    return pl.BlockSpec(arr.shape, lambda *_: zeros, pipeline_mode=pl.Buffered(1))


def _dot(a, b):
    return jnp.dot(a, b, preferred_element_type=F32)


def _dot_nt(a, b):
    return lax.dot_general(a, b, (((1,), (1,)), ((), ())), preferred_element_type=F32)


def _rms(x, g):
    return x * lax.rsqrt(jnp.mean(x * x, axis=-1, keepdims=True) + EPS) * g


def _rmsnorm_kernel(x_ref, g_ref, o_ref):
    o_ref[...] = _rms(x_ref[...], g_ref[...]).astype(o_ref.dtype)


def _rmsnorm(x, g, out_dtype, tm=1024):
    m, d = x.shape
    return pl.pallas_call(
        _rmsnorm_kernel,
        grid=(m // tm,),
        in_specs=[pl.BlockSpec((tm, d), lambda i: (i, 0)),
                  pl.BlockSpec((1, d), lambda i: (0, 0))],
        out_specs=pl.BlockSpec((tm, d), lambda i: (i, 0)),
        out_shape=jax.ShapeDtypeStruct((m, d), out_dtype),
        compiler_params=_cparams("parallel"),
        name="rmsnorm",
    )(x, g.reshape(1, d))


def _rope128(x, c, s):
    return x * c + pltpu.roll(x, HEAD_DIM // 2, 1) * s


def _rope64(x, c, sa, sb):
    half = ROPE_DIM // 2
    return x * c + pltpu.roll(x, LANES - half, 1) * sa + pltpu.roll(x, half, 1) * sb


def _mm_plain_kernel(a_ref, w_ref, o_ref):
    o_ref[...] = _dot(a_ref[...], w_ref[...]).astype(o_ref.dtype)


def _mm_sigmoid_kernel(a_ref, w_ref, o_ref):
    o_ref[...] = jax.nn.sigmoid(_dot(a_ref[...], w_ref[...])).astype(o_ref.dtype)


def _mm_rope_kernel(a_ref, w_ref, cs_ref, c_ref, s_ref, o_ref):
    acc = _dot(a_ref[...], w_ref[...])
    c = c_ref[...]
    s = s_ref[...]
    for j in range(acc.shape[1] // LANES):
        sl = slice(j * LANES, (j + 1) * LANES)
        o_ref[:, sl] = (_rope128(acc[:, sl], c, s) * cs_ref[:, sl]).astype(o_ref.dtype)


def _matmul(a, w, kernel, out_dtype, tm, tn, seq=None, extras=(), name="matmul"):
    m, k = a.shape
    n = w.shape[1]
    in_specs = [pl.BlockSpec((tm, k), lambda i, j: (i, 0)),
                pl.BlockSpec((k, tn), lambda i, j: (0, j))]
    args = [a, w]
    for kind, arr in extras:
        if kind == "col":
            in_specs.append(pl.BlockSpec((1, tn), lambda i, j: (0, j)))
        else:
            nt = seq // tm
            in_specs.append(pl.BlockSpec((tm, LANES), lambda i, j: (i % nt, 0)))
        args.append(arr)
    return pl.pallas_call(
        kernel,
        grid=(m // tm, n // tn),
        in_specs=in_specs,
        out_specs=pl.BlockSpec((tm, tn), lambda i, j: (i, j)),
        out_shape=jax.ShapeDtypeStruct((m, n), out_dtype),
        compiler_params=_cparams("parallel", "parallel"),
        name=name,
    )(*args)


def _mla_down_kernel(h_ref, w_ref, gq_ref, gkv_ref, c_ref, sa_ref, sb_ref,
                     cq_ref, ckv_ref, kr_ref):
    acc = _dot(h_ref[...], w_ref[...])
    cq_ref[...] = _rms(acc[:, :Q_LORA], gq_ref[...]).astype(cq_ref.dtype)
    ckv_ref[...] = _rms(acc[:, Q_LORA:Q_LORA + KV_LORA], gkv_ref[...]).astype(ckv_ref.dtype)
    kr = acc[:, Q_LORA + KV_LORA:]
    kr_ref[...] = _rope64(kr, c_ref[...], sa_ref[...], sb_ref[...]).astype(kr_ref.dtype)


def _mla_down(h, w, g_cq, g_ckv, rope_r, seq, tm=1024):
    m, k = h.shape
    n = w.shape[1]
    nt = seq // tm
    row = lambda width: pl.BlockSpec((tm, width), lambda i: (i, 0))
    full = lambda r, c: pl.BlockSpec((r, c), lambda i: (0, 0))
    pos = pl.BlockSpec((tm, LANES), lambda i: (i % nt, 0))
    return pl.pallas_call(
        _mla_down_kernel,
        grid=(m // tm,),
        in_specs=[row(k), full(k, n), full(1, Q_LORA), full(1, KV_LORA), pos, pos, pos],
        out_specs=[row(Q_LORA), row(KV_LORA), row(LANES)],
        out_shape=[jax.ShapeDtypeStruct((m, Q_LORA), BF16),
                   jax.ShapeDtypeStruct((m, KV_LORA), BF16),
                   jax.ShapeDtypeStruct((m, LANES), BF16)],
        compiler_params=_cparams("parallel"),
        name="mla_down",
    )(h, w, g_cq.reshape(1, -1), g_ckv.reshape(1, -1), *rope_r)


def _mla_q_kernel(cq_ref, w_ref, c_ref, sa_ref, sb_ref, q_ref, *, scale):
    acc = _dot(cq_ref[...], w_ref[...])
    c, sa, sb = c_ref[...], sa_ref[...], sb_ref[...]
    for hd in range(C_HEADS):
        lo = hd * C_QK
        q_ref[:, lo:lo + LANES] = (acc[:, lo:lo + LANES] * scale).astype(q_ref.dtype)
        rope = _rope64(acc[:, lo + LANES:lo + C_QK], c, sa, sb)
        q_ref[:, lo + LANES:lo + C_QK] = (rope * scale).astype(q_ref.dtype)


def _mla_q(cq, w, rope_r, seq, tm=1024):
    m, k = cq.shape
    n = w.shape[1]
    nt = seq // tm
    pos = pl.BlockSpec((tm, LANES), lambda i: (i % nt, 0))
    return pl.pallas_call(
        functools.partial(_mla_q_kernel, scale=(NOPE_DIM + ROPE_DIM) ** -0.5 * LOG2E),
        grid=(m // tm,),
        in_specs=[pl.BlockSpec((tm, k), lambda i: (i, 0)),
                  pl.BlockSpec((k, n), lambda i: (0, 0)), pos, pos, pos],
        out_specs=pl.BlockSpec((tm, n), lambda i: (i, 0)),
        out_shape=jax.ShapeDtypeStruct((m, n), BF16),
        compiler_params=_cparams("parallel"),
        name="mla_q",
    )(cq, w, *rope_r)


def _store_vt(v, vt_ref):
    vt = v.T
    for hd in range(vt_ref.shape[0]):
        vt_ref[hd, 0:V_DIM, :] = vt[hd * V_DIM:(hd + 1) * V_DIM, :].astype(vt_ref.dtype)
        vt_ref[hd, V_DIM:VT_ROWS, :] = jnp.ones((VT_ROWS - V_DIM, vt.shape[1]), vt_ref.dtype)


def _mla_kv_kernel(ckv_ref, kr_ref, wk_ref, wv_ref, k_ref, vt_ref):
    ckv = ckv_ref[...]
    kn = _dot(ckv, wk_ref[...])
    kr = kr_ref[...]
    for hd in range(C_HEADS):
        lo = hd * C_QK
        k_ref[:, lo:lo + LANES] = kn[:, hd * LANES:(hd + 1) * LANES].astype(k_ref.dtype)
        k_ref[:, lo + LANES:lo + C_QK] = kr
    _store_vt(_dot(ckv, wv_ref[...]), vt_ref)


def _mla_kv(ckv, kr, wk, wv, bsz, seq, tm=1024):
    m, k = ckv.shape
    nt = seq // tm
    return pl.pallas_call(
        _mla_kv_kernel,
        grid=(m // tm,),
        in_specs=[pl.BlockSpec((tm, k), lambda i: (i, 0)),
                  pl.BlockSpec((tm, LANES), lambda i: (i, 0)),
                  pl.BlockSpec(wk.shape, lambda i: (0, 0)),
                  pl.BlockSpec(wv.shape, lambda i: (0, 0))],
        out_specs=[pl.BlockSpec((tm, C_HEADS * C_QK), lambda i: (i, 0)),
                   pl.BlockSpec((None, C_HEADS, VT_ROWS, tm), lambda i: (i // nt, 0, 0, i % nt))],
        out_shape=[jax.ShapeDtypeStruct((m, C_HEADS * C_QK), BF16),
                   jax.ShapeDtypeStruct((bsz, C_HEADS, VT_ROWS, seq), BF16)],
        compiler_params=_cparams("parallel"),
        name="mla_kv",
    )(ckv, kr, wk, wv)


def _attend_chunks(qT, k_ref, vt_ref, scratch, *, tk, n_full, mask_tail, tail_steps, tail_col,
                   unroll):
    m_sc, acc_sc, sa_sc, sb_sc, pa_sc, pb_sc, ala_sc, alb_sc, mxa_sc, mxb_sc = scratch
    s_bufs = (sa_sc, sb_sc)
    p_bufs = (pa_sc, pb_sc)
    al_bufs = (ala_sc, alb_sc)
    mx_bufs = (mxa_sc, mxb_sc)
    last_chunk = k_ref.shape[0] // tk - 1
    m_sc[...] = jnp.full(m_sc.shape, MASKED, F32)
    acc_sc[...] = jnp.zeros(acc_sc.shape, F32)
    for p_ref, al_ref in zip(p_bufs, al_bufs):
        p_ref[...] = jnp.zeros(p_ref.shape, p_ref.dtype)
        al_ref[...] = jnp.ones(al_ref.shape, F32)

    def rows(c):
        return pl.ds(pl.multiple_of(jnp.clip(c, 0, last_chunk) * tk, tk), tk)

    def scores(c, slot, col=0):
        sT = _dot(k_ref[rows(c), :], qT[:, col:])
        s_bufs[slot][:, col:] = sT
        mx_bufs[slot][:, col:] = jnp.max(sT, axis=0, keepdims=True)

    def flush(c, slot, col=0):
        acc_sc[:, col:] = (al_bufs[slot][:, col:] * acc_sc[:, col:]
                           + _dot(vt_ref[:, rows(c)], p_bufs[slot][:, col:]))

    def softmax(sT, top, slot, col):
        m_old = m_sc[:, col:]
        m_new = jnp.maximum(m_old, top)
        al_bufs[slot][:, col:] = jnp.exp2(m_old - m_new)
        p_bufs[slot][:, col:] = jnp.exp2(sT - m_new).astype(p_bufs[slot].dtype)
        m_sc[:, col:] = m_new

    def step(tau, slot, mask, cols):
        col_flush, col, col_next = cols
        flush(tau - 2, slot, col_flush)
        if col_next is not None:
            scores(tau + 1, 1 - slot, col_next)
        sT = s_bufs[slot][:, col:]
        if mask is None:
            softmax(sT, mx_bufs[slot][:, col:], slot, col)
        else:
            sT = mask(sT, tau, col)
            softmax(sT, jnp.max(sT, axis=0, keepdims=True), slot, col)

    def full_steps(tau0, count):
        for j in range(count):
            step(tau0 + j, j % 2, None, (0, 0, 0))

    scores(0, 0)
    trips = n_full // unroll
    lax.fori_loop(0, trips, lambda u, c: (full_steps(unroll * u, unroll), c)[1], 0)
    done = unroll * trips
    pairs = (n_full - done) // 2
    lax.fori_loop(0, pairs, lambda u, c: (full_steps(done + 2 * u, 2), c)[1], 0)
    tau = done + 2 * pairs
    cols = [tail_col(j) for j in range(tail_steps)]
    for j in range(tail_steps):
        col_flush = cols[j - 2] if j >= 2 else 0
        col_next = cols[j + 1] if j + 1 < tail_steps else None
        step(tau + j, j % 2, mask_tail, (col_flush, cols[j], col_next))
    flush(tau + tail_steps - 2, 0, cols[-2])
    flush(tau + tail_steps - 1, 1, cols[-1])
    acc = acc_sc[...]
    return acc[:V_DIM, :] / acc[V_DIM:V_DIM + 1, :]


def _attend_scratch(tq, tk):
    return [pltpu.VMEM((1, tq), F32), pltpu.VMEM((VT_ROWS, tq), F32),
            pltpu.VMEM((tk, tq), F32), pltpu.VMEM((tk, tq), F32),
            pltpu.VMEM((tk, tq), BF16), pltpu.VMEM((tk, tq), BF16),
            pltpu.VMEM((1, tq), F32), pltpu.VMEM((1, tq), F32),
            pltpu.VMEM((1, tq), F32), pltpu.VMEM((1, tq), F32)]


def _transpose_q(q_ref):
    return q_ref[...].astype(F32).T.astype(BF16)


def _flash_kernel(q_ref, k_ref, vt_ref, o_ref, *scratch, tq, tk):
    i = pl.program_id(2)

    def causal(sT, c, col):
        key = lax.broadcasted_iota(jnp.int32, sT.shape, 0) + c * tk
        qry = lax.broadcasted_iota(jnp.int32, sT.shape, 1) + (i * tq + col)
        return jnp.where(key <= qry, sT, MASKED)

    per_tile = tq // tk
    oT = _attend_chunks(_transpose_q(q_ref), k_ref, vt_ref, scratch, tk=tk, n_full=i * per_tile,
                        mask_tail=causal, tail_steps=per_tile, tail_col=lambda j: j * tk,
                        unroll=2)
    o_ref[...] = oT.T.astype(o_ref.dtype)


def _flash_attention(q, k, vt, heads, qk_w, tq=2048, tk=1024):
    bsz, seq, _ = q.shape
    tq, tk = min(tq, seq), min(tk, seq // 2)
    assert tq % (2 * tk) == 0 and seq % tq == 0
    once = pl.Buffered(1)
    return pl.pallas_call(
        functools.partial(_flash_kernel, tq=tq, tk=tk),
        grid=(bsz, heads, seq // tq),
        in_specs=[pl.BlockSpec((None, tq, qk_w), lambda b, h, i: (b, i, h)),
                  pl.BlockSpec((None, seq, qk_w), lambda b, h, i: (b, 0, h), pipeline_mode=once),
                  pl.BlockSpec((None, None, VT_ROWS, seq), lambda b, h, i: (b, h, 0, 0),
                               pipeline_mode=once)],
        out_specs=pl.BlockSpec((None, tq, V_DIM), lambda b, h, i: (b, i, h)),
        out_shape=jax.ShapeDtypeStruct((bsz, seq, heads * V_DIM), BF16),
        scratch_shapes=_attend_scratch(tq, tk),
        compiler_params=_cparams("parallel", "parallel", "arbitrary"),
        name="mla_flash",
    )(q, k, vt)


def _kmean_kernel(k_ref, o_ref):
    k = k_ref[...].astype(F32)
    o_ref[...] = jnp.mean(k.reshape(SUBLANES, MOBA_BLOCK, k.shape[-1]), axis=1)


def _kmean(qk):
    bsz, seq, _ = qk.shape
    rows = SUBLANES * MOBA_BLOCK
    return pl.pallas_call(
        _kmean_kernel,
        grid=(bsz, seq // rows),
        in_specs=[pl.BlockSpec((None, rows, A_W), lambda b, i: (b, i, KA_BLK * LANES // A_W))],
        out_specs=pl.BlockSpec((None, SUBLANES, A_W), lambda b, i: (b, i, 0)),
        out_shape=jax.ShapeDtypeStruct((bsz, seq // MOBA_BLOCK, A_W), F32),
        compiler_params=_cparams("parallel", "parallel"),
        name="moba_kmean",
    )(qk)


def _block_attention(q, k, v, visible=None):
    s = _dot_nt(q, k)
    if visible is not None:
        s = jnp.where(visible, s, MASKED)
    m = jnp.max(s, axis=-1, keepdims=True)
    p = jnp.exp2(s - m).astype(BF16)
    v_ones = jnp.concatenate([v, jnp.ones((v.shape[0], LANES), v.dtype)], axis=-1)
    acc = _dot(p, v_ones)
    den = acc[:, V_DIM:]
    return acc[:, :V_DIM] / den, m + jnp.log2(den)


def _moba_gate_kernel(q_ref, km_ref, ids_ref, cnt_ref, qf_ref):
    t = MOBA_BLOCK
    i = pl.program_id(1)
    nb = km_ref.shape[0]
    blk = lax.broadcasted_iota(jnp.int32, (nb, t), 0)
    neg_inf = jnp.float32(-jnp.inf)
    not_after = (lax.broadcasted_iota(jnp.int32, (t, t), 0)
                 <= lax.broadcasted_iota(jnp.int32, (t, t), 1))
    upper = jnp.where(not_after, 1.0, 0.0).astype(BF16)
    ones = jnp.ones((SUBLANES, t), BF16)
    for hd in range(A_HEADS):
        sl = slice(hd * HEAD_DIM, (hd + 1) * HEAD_DIM)
        q = q_ref[:, sl].astype(F32)
        qf_ref[hd] = q
        qT = q.T.astype(BF16)
        km = km_ref[:, sl]
        km_hi = km.astype(BF16)
        km_lo = (km - km_hi.astype(F32)).astype(BF16)
        g = jnp.where(blk < i, _dot(km_hi, qT) + _dot(km_lo, qT), neg_inf)
        picks, ranks, counts = [], [], []
        for _ in range(MOBA_TOPK):
            mx = jnp.max(g, axis=0, keepdims=True)
            is_max = (g == mx) & (mx > neg_inf)
            first = jnp.min(jnp.where(is_max, blk, nb), axis=0, keepdims=True)
            pick = blk == first
            g = jnp.where(pick, neg_inf, g)
            onehot = jnp.where(pick, 1.0, 0.0).astype(BF16)
            before = _dot(onehot, upper)
            rank = jnp.sum(jnp.where(pick, before - 1.0, 0.0), axis=0, keepdims=True)
            picks.append(first)
            ranks.append(rank.astype(jnp.int32))
            counts.append(_dot_nt(ones, onehot)[0:1, :])
        pad_i = jnp.zeros((SUBLANES - 2 * MOBA_TOPK, t), jnp.int32)
        ids_ref[hd] = jnp.concatenate(picks + ranks + [pad_i], axis=0)
        pad_f = jnp.zeros((SUBLANES - MOBA_TOPK, nb), F32)
        cnt_ref[hd] = jnp.concatenate(counts + [pad_f], axis=0)


def _moba_gate(qk, kmean):
    bsz, seq, _ = qk.shape
    t = MOBA_BLOCK
    nb = seq // t
    return pl.pallas_call(
        _moba_gate_kernel,
        grid=(bsz, nb),
        in_specs=[pl.BlockSpec((None, t, A_W), lambda b, i: (b, i, QA_BLK * LANES // A_W)),
                  pl.BlockSpec((None, nb, A_W), lambda b, i: (b, 0, 0))],
        out_specs=[pl.BlockSpec((None, A_HEADS, SUBLANES, t), lambda b, i: (b, 0, 0, i)),
                   pl.BlockSpec((None, A_HEADS, None, SUBLANES, nb), lambda b, i: (b, 0, i, 0, 0)),
                   pl.BlockSpec((None, A_HEADS, t, HEAD_DIM), lambda b, i: (b, 0, i, 0))],
        out_shape=[jax.ShapeDtypeStruct((bsz, A_HEADS, SUBLANES, seq), jnp.int32),
                   jax.ShapeDtypeStruct((bsz, A_HEADS, nb, SUBLANES, nb), F32),
                   jax.ShapeDtypeStruct((bsz, A_HEADS, seq, HEAD_DIM), F32)],
        compiler_params=_cparams("parallel", "parallel"),
        name="moba_gate",
    )(qk, kmean)


def _moba_routes(ids, cnt, seq):
    bsz, heads = ids.shape[:2]
    bh, t = bsz * heads, MOBA_BLOCK
    nb = seq // t
    tiles = _moba_tiles(seq)
    picks = ids[:, :, 0:MOBA_TOPK, :].reshape(bh, MOBA_TOPK, nb, t)
    ranks = ids[:, :, MOBA_TOPK:2 * MOBA_TOPK, :].reshape(bh, MOBA_TOPK, nb, t)
    per_tile = cnt[:, :, :, 0:MOBA_TOPK, :].astype(jnp.int32).reshape(bh, nb * MOBA_TOPK, nb)
    before = jnp.cumsum(per_tile, axis=1) - per_tile
    total = jnp.sum(per_tile, axis=1)
    padded = -(-total // t) * t
    ends = jnp.cumsum(padded, axis=1)
    base = before + (ends - padded)[:, None, :]
    base = base.reshape(bh, nb, MOBA_TOPK, nb).transpose(0, 2, 1, 3)
    onehot = picks[..., None] == jnp.arange(nb)
    pos = jnp.sum(jnp.where(onehot, base[:, :, :, None, :], 0), axis=-1) + ranks
    pos = jnp.where(picks < nb, pos, (tiles - 1) * t)
    pos = pos + (jnp.arange(bh, dtype=jnp.int32) * (tiles * t))[:, None, None, None]
    pos = pos.reshape(bh, MOBA_TOPK, seq).transpose(1, 0, 2).reshape(MOBA_TOPK, bh * seq)
    tile_start = jnp.arange(tiles, dtype=jnp.int32) * t
    tile_blk = jnp.sum(tile_start[None, :, None] >= ends[:, None, :], axis=-1)
    tile_blk = jnp.where(tile_start[None, :] < ends[:, -1:], tile_blk, -1)
    return pos.astype(jnp.int32), tile_blk.astype(jnp.int32)


def _moba_tiles(seq):
    nb = seq // MOBA_BLOCK
    return -(-(MOBA_TOPK * nb + nb + 1) // GROUP_STEP) * GROUP_STEP


def _sc_mesh():
    return plsc.VectorSubcoreMesh(core_axis_name="core", subcore_axis_name="subcore")


def _sc_scatter_rows(x, idx, rows):
    slots, n = idx.shape
    d = x.shape[1]

    @pl.kernel(out_type=jax.ShapeDtypeStruct((rows, d), x.dtype), mesh=_sc_mesh())
    def scatter(x_hbm, i_hbm, o_hbm):
        def body(x_vmem, i_vmem):
            pltpu.sync_copy(x_vmem, o_hbm.at[i_vmem.at[0]])

        pltpu.emit_pipeline(
            body, grid=(slots, n // SC_WINDOW),
            in_specs=[pl.BlockSpec((SC_WINDOW, d), lambda s, i: (i, 0)),
                      pl.BlockSpec((1, SC_WINDOW), lambda s, i: (s, i))],
            out_specs=[],
            core_axis_name=("core", "subcore"),
            dimension_semantics=(pltpu.PARALLEL, pltpu.PARALLEL),
        )(x_hbm, i_hbm)

    return scatter(x, idx)


def _sc_gather_rows(x, idx):
    n = idx.shape[0]
    d = x.shape[1]

    @pl.kernel(out_type=jax.ShapeDtypeStruct((n, d), x.dtype), mesh=_sc_mesh())
    def gather(x_hbm, i_hbm, o_hbm):
        def body(i_vmem, o_vmem):
            pltpu.sync_copy(x_hbm.at[i_vmem.at[0]], o_vmem)

        pltpu.emit_pipeline(
            body, grid=(n // SC_WINDOW,),
            in_specs=[pl.BlockSpec((1, SC_WINDOW), lambda i: (0, i))],
            out_specs=[pl.BlockSpec((SC_WINDOW, d), lambda i: (i, 0))],
            core_axis_name=("core", "subcore"),
            dimension_semantics=(pltpu.PARALLEL,),
        )(i_hbm, o_hbm)

    return gather(x, idx.reshape(1, n))


def _moba_group_kernel(tb_ref, q_ref, *refs):
    t = MOBA_BLOCK
    k_refs, v_refs = refs[:GROUP_STEP], refs[GROUP_STEP:2 * GROUP_STEP]
    o_ref, lse_ref = refs[-2:]
    g, step = pl.program_id(0), pl.program_id(1)
    first = step * GROUP_STEP

    @pl.when(tb_ref[g, first] < 0)
    def _():
        o_ref[...] = jnp.zeros(o_ref.shape, o_ref.dtype)
        lse_ref[...] = jnp.full(lse_ref.shape, MASKED, lse_ref.dtype)

    @pl.when(tb_ref[g, first] >= 0)
    def _():
        for u in range(GROUP_STEP):
            used = tb_ref[g, first + u] >= 0
            rows = slice(u * t, (u + 1) * t)
            o, lse = _block_attention(q_ref[rows, :].astype(BF16), k_refs[u][...], v_refs[u][...])
            o_ref[rows, :] = jnp.where(used, o, 0.0)
            lse_ref[rows, :] = jnp.where(used, lse, MASKED)


def _moba_group_attention(q_grouped, tile_blk, qk, v, after):
    bh, rows, _ = q_grouped.shape
    t = MOBA_BLOCK
    tiles = rows // t
    heads = A_HEADS

    def block_of(u, first_col):
        return lambda g, s, tb: (g // heads, jnp.maximum(tb[g, s * GROUP_STEP + u], 0),
                                 first_col + g % heads)

    row_tile = pl.BlockSpec((None, GROUP_STEP * t, HEAD_DIM), lambda g, s, tb: (g, s, 0))
    key_value = lambda first_col: [pl.BlockSpec((None, t, HEAD_DIM), block_of(u, first_col))
                                   for u in range(GROUP_STEP)]
    grid_spec = pltpu.PrefetchScalarGridSpec(
        num_scalar_prefetch=1,
        grid=(bh, tiles // GROUP_STEP),
        in_specs=([row_tile] + key_value(KA_BLK) + key_value(0)
                  + [pl.BlockSpec(memory_space=pl.ANY)]),
        out_specs=[row_tile, row_tile],
    )
    return pl.pallas_call(
        _moba_group_kernel,
        grid_spec=grid_spec,
        out_shape=[jax.ShapeDtypeStruct(q_grouped.shape, F32)] * 2,
        compiler_params=_cparams("parallel", "parallel"),
        name="moba_group",
    )(tile_blk, q_grouped, *([qk] * GROUP_STEP), *([v] * GROUP_STEP), after)


def _moba_merge_kernel(q_ref, k_ref, v_ref, po_ref, pl_ref, o_ref):
    t = MOBA_BLOCK
    causal = (lax.broadcasted_iota(jnp.int32, (t, t), 1)
              <= lax.broadcasted_iota(jnp.int32, (t, t), 0))
    for hd in range(A_HEADS):
        sl = slice(hd * HEAD_DIM, (hd + 1) * HEAD_DIM)
        o_own, lse_own = _block_attention(q_ref[:, sl], k_ref[:, sl], v_ref[:, sl], causal)
        outs = [o_own] + [po_ref[s, hd] for s in range(MOBA_TOPK)]
        lses = [lse_own] + [pl_ref[s, hd] for s in range(MOBA_TOPK)]
        top = functools.reduce(jnp.maximum, lses)
        weights = [jnp.exp2(l - top) for l in lses]
        num = sum(w * o for w, o in zip(weights, outs))
        o_ref[:, sl] = (num / sum(weights)).astype(o_ref.dtype)


def _moba_merge(qk, v, part_o, part_lse):
    bsz, seq, _ = qk.shape
    t = MOBA_BLOCK
    part = pl.BlockSpec((MOBA_TOPK, None, A_HEADS, t, HEAD_DIM), lambda b, i: (0, b, 0, i, 0))
    return pl.pallas_call(
        _moba_merge_kernel,
        grid=(bsz, seq // t),
        in_specs=[pl.BlockSpec((None, t, A_W), lambda b, i: (b, i, QA_BLK * LANES // A_W)),
                  pl.BlockSpec((None, t, A_W), lambda b, i: (b, i, KA_BLK * LANES // A_W)),
                  pl.BlockSpec((None, t, A_W), lambda b, i: (b, i, 0)),
                  part, part],
        out_specs=pl.BlockSpec((None, t, A_W), lambda b, i: (b, i, 0)),
        out_shape=jax.ShapeDtypeStruct((bsz, seq, A_W), BF16),
        compiler_params=_cparams("parallel", "parallel"),
        name="moba_merge",
    )(qk, qk, v, part_o, part_lse)


def _moba_regroup(qk, kmean):
    bsz, seq, _ = qk.shape
    bh = bsz * A_HEADS
    rows = _moba_tiles(seq) * MOBA_BLOCK
    ids, cnt, q_f32 = _moba_gate(qk, kmean)
    pos, tile_blk = _moba_routes(ids, cnt, seq)
    q_grouped = _sc_scatter_rows(q_f32.reshape(bh * seq, HEAD_DIM), pos, bh * rows)
    return q_grouped.reshape(bh, rows, HEAD_DIM), pos, tile_blk


def _moba_picked_blocks(q_grouped, pos, tile_blk, qk, v, after):
    bsz, seq, _ = qk.shape
    bh, rows, _ = q_grouped.shape
    o_g, lse_g = _moba_group_attention(q_grouped, tile_blk, qk, v, after)
    flat = pos.reshape(-1)
    back = lambda a: _sc_gather_rows(a.reshape(bh * rows, HEAD_DIM), flat).reshape(
        MOBA_TOPK, bsz, A_HEADS, seq, HEAD_DIM)
    return back(o_g), back(lse_g)


def _proj_dilated_kernel(h_ref, w_ref, c_ref, s_ref, q_ref, k_ref, v_ref, sc, *, d):
    acc = _dot(h_ref[...], w_ref[...])
    c, s = c_ref[...], s_ref[...]
    q_scale = HEAD_DIM ** -0.5 * LOG2E
    for j in range(acc.shape[1] // LANES):
        blk = acc[:, j * LANES:(j + 1) * LANES]
        if j < B_HEADS:
            blk = _rope128(blk, c, s) * q_scale
        elif j < 2 * B_HEADS:
            blk = _rope128(blk, c, s)
        sc[j] = blk
    rows = acc.shape[0] // d
    for r in range(d):
        for j in range(acc.shape[1] // LANES):
            dst = (q_ref, k_ref, v_ref)[j // B_HEADS]
            col = (j % B_HEADS) * LANES
            dst[r, :, col:col + LANES] = sc[j, pl.ds(r, rows, stride=d), :].astype(dst.dtype)


def _proj_dilated(h, w, rope_h, d, bsz, seq, tm=1024):
    m, k = h.shape
    nt = seq // tm
    pos = pl.BlockSpec((tm, LANES), lambda i: (i % nt, 0))
    out = pl.BlockSpec((None, d, tm // d, B_W), lambda i: (i // nt, 0, i % nt, 0))
    return pl.pallas_call(
        functools.partial(_proj_dilated_kernel, d=d),
        grid=(m // tm,),
        in_specs=[pl.BlockSpec((tm, k), lambda i: (i, 0)), pl.BlockSpec(w.shape, lambda i: (0, 0)),
                  pos, pos],
        out_specs=[out] * 3,
        out_shape=[jax.ShapeDtypeStruct((bsz, d, seq // d, B_W), BF16)] * 3,
        scratch_shapes=[pltpu.VMEM((w.shape[1] // LANES, tm, LANES), F32)],
        compiler_params=_cparams("parallel"),
        name=f"proj_dilated_d{d}",
    )(h, w, *rope_h)


def _dilated_kernel(q_ref, kc_ref, kp_ref, vc_ref, vp_ref, o_ref, lse_ref, *, span):
    t, tp = q_ref.shape[0], kp_ref.shape[0]
    i = pl.program_id(2)
    shape = (2 * tp, tp)
    key_row = lax.broadcasted_iota(jnp.int32, shape, 0)
    dist = lax.broadcasted_iota(jnp.int32, shape, 1) + tp - key_row
    visible = (dist >= 0) & (dist <= span)
    bias = jnp.where(visible, 0.0, MASKED)
    bias_first = jnp.where(visible & ((key_row >= tp) | (i > 0)), 0.0, MASKED)
    ones = jnp.ones((BF16_ROWS, tp + t), BF16)

    def transposed(x):
        return x.astype(F32).T.astype(BF16)

    for j in range(B_HEADS):
        sl = slice(j * LANES, (j + 1) * LANES)
        qT = transposed(q_ref[:, sl])
        k_all = jnp.concatenate([kp_ref[:, sl], kc_ref[:, sl]], axis=0)
        vt_all = jnp.concatenate([transposed(vp_ref[:, sl]), transposed(vc_ref[:, sl])], axis=1)
        vt_all = jnp.concatenate([vt_all, ones], axis=0)
        outs, lses = [], []
        for u in range(t // tp):
            window = slice(u * tp, (u + 2) * tp)
            s = _dot(k_all[window, :], qT[:, u * tp:(u + 1) * tp])
            s = s + (bias_first if u == 0 else bias)
            m = jnp.max(s, axis=0, keepdims=True)
            p = jnp.exp2(s - m).astype(BF16)
            acc = _dot(vt_all[:, window], p)
            den = acc[V_DIM:V_DIM + 1, :]
            outs.append(acc[:V_DIM, :] / den)
            lses.append(m + jnp.log2(den))
        o_ref[:, sl] = jnp.concatenate(outs, axis=1).T
        lse = jnp.concatenate(lses, axis=1)
        lse_ref[:, sl] = jnp.broadcast_to(lse, (LANES, t)).T


def _dilated_attention(q, k, v, span, t=1024):
    bsz, d, length, _ = q.shape
    t = min(t, length)
    tp = B_QBLOCK
    assert span <= tp and t % tp == 0
    cur = pl.BlockSpec((None, None, t, B_W), lambda b, r, i: (b, r, i, 0))
    prev = pl.BlockSpec((None, None, tp, B_W),
                        lambda b, r, i: (b, r, jnp.maximum(i * (t // tp) - 1, 0), 0))
    return pl.pallas_call(
        functools.partial(_dilated_kernel, span=span),
        grid=(bsz, d, length // t),
        in_specs=[cur, cur, prev, cur, prev],
        out_specs=[cur, cur],
        out_shape=[jax.ShapeDtypeStruct(q.shape, F32)] * 2,
        compiler_params=_cparams("parallel", "parallel", "parallel"),
        name=f"dilated_d{d}",
    )(q, k, k, v, v)


def _natural_rows(ref, sc):
    d, rows = ref.shape[0], ref.shape[1]
    if d == 1:
        return ref[0]
    for r in range(d):
        for j in range(B_HEADS):
            sc[j, pl.ds(r, rows, stride=d), :] = ref[r, :, j * LANES:(j + 1) * LANES]
    return jnp.concatenate([sc[j] for j in range(B_HEADS)], axis=-1)


def _mixer_tail_kernel(x_ref, oa_ref, o0_ref, o1_ref, o2_ref, l0_ref, l1_ref, l2_ref, oc_ref,
                       g_ref, wpa_ref, wpb_ref, wpc_ref, wo_ref, y_ref, *scratch):
    o0, o1, o2, l0, l1, l2 = [
        _natural_rows(ref, sc)
        for ref, sc in zip((o0_ref, o1_ref, o2_ref, l0_ref, l1_ref, l2_ref), scratch)]
    mx = jnp.maximum(jnp.maximum(l0, l1), l2)
    e0, e1, e2 = jnp.exp2(l0 - mx), jnp.exp2(l1 - mx), jnp.exp2(l2 - mx)
    ob = (e0 * o0 + e1 * o1 + e2 * o2) / (e0 + e1 + e2)
    pa = _dot(oa_ref[...], wpa_ref[...])
    pb = _dot(ob.astype(BF16), wpb_ref[...])
    pc = _dot(oc_ref[...], wpc_ref[...])
    d = D_MODEL
    merged = (g_ref[:, 0:d].astype(F32) * pa + g_ref[:, d:2 * d].astype(F32) * pb
              + g_ref[:, 2 * d:3 * d].astype(F32) * pc)
    y_ref[...] = x_ref[...] + _dot(merged.astype(BF16), wo_ref[...])


def _mixer_tail(x, out_a, o_groups, lse_groups, out_c, gates, w_pa, w_pb, w_pc, w_o, seq, tm=256):
    m, d = x.shape
    nt = seq // tm
    row = lambda width: pl.BlockSpec((tm, width), lambda i: (i, 0))
    residue = lambda g: pl.BlockSpec((None, g.shape[1], tm // g.shape[1], B_W),
                                     lambda i: (i // nt, 0, i % nt, 0))
    weights = [_resident(w) for w in (w_pa, w_pb, w_pc, w_o)]
    groups = list(o_groups) + list(lse_groups)
    return pl.pallas_call(
        _mixer_tail_kernel,
        grid=(m // tm,),
        in_specs=([row(d), row(A_W)] + [residue(g) for g in groups]
                  + [row(C_W), row(3 * d)] + weights),
        out_specs=row(d),
        out_shape=jax.ShapeDtypeStruct((m, d), F32),
        scratch_shapes=[pltpu.VMEM((B_HEADS, tm, LANES), F32) for _ in groups],
        compiler_params=_cparams("parallel"),
        name="mixer_tail",
    )(x, out_a, *groups, out_c, gates, w_pa, w_pb, w_pc, w_o)


def _mem_kv_kernel(mem_ref, g_ref, wk_ref, wv_ref, k_ref, v_ref):
    memn = _rms(mem_ref[...], g_ref[...]).astype(BF16)
    k_ref[...] = _dot(memn, wk_ref[...]).astype(k_ref.dtype)
    v_ref[...] = _dot(memn, wv_ref[...]).astype(v_ref.dtype)


def _mem_kv(mem, g, wk, wv):
    bsz, n, d = mem.shape
    out = pl.BlockSpec((None, n, X_W), lambda b: (b, 0, 0))
    return pl.pallas_call(
        _mem_kv_kernel,
        grid=(bsz,),
        in_specs=[pl.BlockSpec((None, n, d), lambda b: (b, 0, 0)),
                  pl.BlockSpec((1, d), lambda b: (0, 0)),
                  pl.BlockSpec(wk.shape, lambda b: (0, 0)),
                  pl.BlockSpec(wv.shape, lambda b: (0, 0))],
        out_specs=[out, out],
        out_shape=[jax.ShapeDtypeStruct((bsz, n, X_W), BF16)] * 2,
        compiler_params=_cparams("parallel"),
        name="mem_kv",
    )(mem, g.reshape(1, d), wk, wv)


def _mem_attn_kernel(x_ref, g_ref, wq_ref, k_ref, v_ref, wo_ref, y_ref):
    x = x_ref[...]
    h = _rms(x, g_ref[...]).astype(BF16)
    q = (_dot(h, wq_ref[...]) * HEAD_DIM ** -0.5).astype(BF16)
    heads = []
    for hd in range(X_HEADS):
        sl = slice(hd * HEAD_DIM, (hd + 1) * HEAD_DIM)
        s = _dot_nt(q[:, sl], k_ref[:, sl])
        p = jnp.exp(s - jnp.max(s, axis=-1, keepdims=True))
        o = _dot(p.astype(BF16), v_ref[:, sl]) / jnp.sum(p, axis=-1, keepdims=True)
        heads.append(o.astype(BF16))
    y_ref[...] = x + _dot(jnp.concatenate(heads, axis=-1), wo_ref[...])


def _mem_attention(x, g, wq, kmem, vmem, wo, seq, tm=1024):
    m, d = x.shape
    nt = seq // tm
    n = kmem.shape[1]
    kv = pl.BlockSpec((None, n, X_W), lambda i: (i // nt, 0, 0))
    return pl.pallas_call(
        _mem_attn_kernel,
        grid=(m // tm,),
        in_specs=[pl.BlockSpec((tm, d), lambda i: (i, 0)),
                  pl.BlockSpec((1, d), lambda i: (0, 0)),
                  pl.BlockSpec(wq.shape, lambda i: (0, 0)), kv, kv,
                  pl.BlockSpec(wo.shape, lambda i: (0, 0))],
        out_specs=pl.BlockSpec((tm, d), lambda i: (i, 0)),
        out_shape=jax.ShapeDtypeStruct((m, d), F32),
        compiler_params=_cparams("parallel"),
        name="mem_attention",
    )(x, g.reshape(1, d), wq, kmem, vmem, wo)


def _ffn_up_kernel(x_ref, halo_ref, g_ref, wg_ref, wv_ref, cwg_ref, cwv_ref, cbg_ref, cbv_ref,
                   act_ref, h_sc, *, tiles_per_seq):
    i = pl.program_id(0)
    tm = x_ref.shape[0]

    @pl.when(pl.program_id(1) == 0)
    def _():
        g = g_ref[...]
        keep = (i % tiles_per_seq != 0).astype(F32)
        h_sc[0:HALO, :] = (_rms(halo_ref[...], g) * keep).astype(h_sc.dtype)
        h_sc[HALO:, :] = _rms(x_ref[...], g).astype(h_sc.dtype)

    h = h_sc[...]

    def conv(w_ref, cw_ref, cb_ref):
        u = _dot(h, w_ref[...])
        c = cb_ref[...]
        for tap in range(CONV_W):
            lo = HALO - (CONV_W - 1) + tap
            c = c + cw_ref[tap:tap + 1, :] * u[lo:lo + tm, :]
        return c

    act = jax.nn.silu(conv(wg_ref, cwg_ref, cbg_ref)) * conv(wv_ref, cwv_ref, cbv_ref)
    act_ref[...] = act.astype(act_ref.dtype)


def _ffn_down_kernel(a_ref, w_ref, x_ref, y_ref):
    y_ref[...] = x_ref[...] + _dot(a_ref[...], w_ref[...])


def _conv_ffn(x, g, w_up, conv_w, conv_b, w_down, layer, seq):
    m, d = x.shape
    act = _ffn_up(x, g, w_up, conv_w, conv_b, layer, seq)
    tm, tn = 1024, FFN_TF
    return pl.pallas_call(
        _ffn_down_kernel,
        grid=(m // tm, d // tn),
        in_specs=[pl.BlockSpec((tm, D_FF), lambda i, j: (i, 0)),
                  pl.BlockSpec((None, D_FF, tn), lambda i, j: (layer, 0, j)),
                  pl.BlockSpec((tm, tn), lambda i, j: (i, j))],
        out_specs=pl.BlockSpec((tm, tn), lambda i, j: (i, j)),
        out_shape=jax.ShapeDtypeStruct((m, d), F32),
        compiler_params=_cparams("parallel", "parallel"),
        name="ffn_down",
    )(act, w_down, x)


def _ffn_up(x, g, w_up, conv_w, conv_b, layer, seq, tm=1024, tf=FFN_TF):
    m, d = x.shape
    nf = D_FF_PAD // tf
    halo_blocks = tm // HALO
    return pl.pallas_call(
        functools.partial(_ffn_up_kernel, tiles_per_seq=seq // tm),
        grid=(m // tm, nf),
        in_specs=[pl.BlockSpec((tm, d), lambda i, f: (i, 0)),
                  pl.BlockSpec((HALO, d), lambda i, f: (jnp.maximum(i * halo_blocks - 1, 0), 0)),
                  pl.BlockSpec((1, d), lambda i, f: (0, 0)),
                  pl.BlockSpec((None, d, tf), lambda i, f: (layer, 0, f)),
                  pl.BlockSpec((None, d, tf), lambda i, f: (layer, 0, f + nf)),
                  pl.BlockSpec((CONV_W, tf), lambda i, f: (0, f)),
                  pl.BlockSpec((CONV_W, tf), lambda i, f: (0, f + nf)),
                  pl.BlockSpec((1, tf), lambda i, f: (0, f)),
                  pl.BlockSpec((1, tf), lambda i, f: (0, f + nf))],
        out_specs=pl.BlockSpec((tm, tf), lambda i, f: (i, f)),
        out_shape=jax.ShapeDtypeStruct((m, D_FF_PAD), BF16),
        scratch_shapes=[pltpu.VMEM((HALO + tm, d), BF16)],
        compiler_params=_cparams("parallel", "arbitrary"),
        name="ffn_up",
    )(x, x, g.reshape(1, d), w_up, w_up, conv_w, conv_w, conv_b, conv_b)


def _rope_tables(seq):
    def angles(dim):
        inv_freq = jnp.exp(jnp.arange(0, dim, 2, dtype=F32) * (-math.log(ROPE_THETA) / dim))
        ang = jnp.arange(seq, dtype=F32)[:, None] * inv_freq[None, :]
        return jnp.cos(ang), jnp.sin(ang)

    cos_h, sin_h = angles(HEAD_DIM)
    rope_h = (jnp.concatenate([cos_h, cos_h], axis=-1), jnp.concatenate([-sin_h, sin_h], axis=-1))
    cos_r, sin_r = angles(ROPE_DIM)
    z = jnp.zeros_like(cos_r)
    rope_r = (jnp.concatenate([cos_r, cos_r, z, z], axis=-1),
              jnp.concatenate([-sin_r, z, z, z], axis=-1),
              jnp.concatenate([z, sin_r, z, z], axis=-1))
    return rope_h, rope_r


def _split_in(w_in):
    return [w_in[:, IN_OFFSETS[k]:IN_OFFSETS[k + 1]] for k in range(len(IN_WIDTHS))]


def _pad_cols(w, width):
    return jnp.pad(w, ((0, 0), (0, width - w.shape[1])))


def _layer_params(w_in, w_uq, w_ukv, conv_w, conv_b):
    qa, ka, va, qb, kb, vb, cq, ckv, kr, gates = _split_in(w_in)
    w_qk = jnp.concatenate([qa, ka], axis=1).astype(BF16)
    group_cols = lambda w, g: w[:, g * B_W:(g + 1) * B_W]
    w_b = [jnp.concatenate([group_cols(qb, g), group_cols(kb, g), group_cols(vb, g)],
                           axis=1).astype(BF16) for g in range(len(B_GROUPS))]
    w_down_in = jnp.concatenate([cq, ckv, _pad_cols(kr, LANES)], axis=1).astype(BF16)
    uq = w_uq.reshape(Q_LORA, C_HEADS, NOPE_DIM + ROPE_DIM)
    uq = jnp.pad(uq, ((0, 0), (0, 0), (0, C_QK - NOPE_DIM - ROPE_DIM)))
    ukv = w_ukv.reshape(KV_LORA, C_HEADS, NOPE_DIM + V_DIM)
    return dict(
        w_qk=w_qk, w_va=va.astype(BF16), w_b=w_b, w_gates=gates.astype(BF16),
        w_down_in=w_down_in,
        w_uq=uq.reshape(Q_LORA, C_HEADS * C_QK).astype(BF16),
        w_uk=ukv[:, :, :NOPE_DIM].reshape(KV_LORA, C_HEADS * NOPE_DIM).astype(BF16),
        w_uv=ukv[:, :, NOPE_DIM:].reshape(KV_LORA, C_W).astype(BF16),
        conv_w=_pad_ff_halves(conv_w),
        conv_b=_pad_ff_halves(conv_b.reshape(1, -1)),
    )


def _pad_ff_halves(w):
    pad = [(0, 0)] * (w.ndim - 1) + [(0, D_FF_PAD - D_FF)]
    return jnp.concatenate([jnp.pad(w[..., :D_FF], pad), jnp.pad(w[..., D_FF:], pad)], axis=-1)


def _qk_col_scale():
    q_scale = HEAD_DIM ** -0.5
    parts = [jnp.full((A_W,), q_scale * LOG2E, F32), jnp.ones((A_W,), F32)]
    return jnp.concatenate(parts).reshape(1, QK_W)


def _mixer(x, g_mix, p, g_cq, g_ckv, w_pa, w_pb, w_pc, w_o, rope_h, rope_r, bsz, seq):
    m = x.shape[0]
    h = _rmsnorm(x, g_mix, BF16)
    qk = _matmul(h, p["w_qk"], _mm_rope_kernel, BF16, 1024, 1024, seq=seq,
                 extras=(("col", _qk_col_scale()), ("pos", rope_h[0]), ("pos", rope_h[1])),
                 name="proj_qk_rope")
    qk3 = qk.reshape(bsz, seq, QK_W)
    q_grouped, pos, tile_blk = _moba_regroup(qk3, _kmean(qk3))
    v_a = _matmul(h, p["w_va"], _mm_plain_kernel, BF16, 2048, A_W, name="proj_va")
    gates = _matmul(h, p["w_gates"], _mm_sigmoid_kernel, BF16, 1024, 1024, name="proj_gates")
    v_a3 = v_a.reshape(bsz, seq, A_W)
    part_o, part_lse = _moba_picked_blocks(q_grouped, pos, tile_blk, qk3, v_a3, after=gates)
    cq, ckv, kr = _mla_down(h, p["w_down_in"], g_cq, g_ckv, rope_r, seq)
    q_c = _mla_q(cq, p["w_uq"], rope_r, seq)
    k_c, vt_c = _mla_kv(ckv, kr, p["w_uk"], p["w_uv"], bsz, seq)
    groups = []
    for (window, d), w_g in zip(B_GROUPS, p["w_b"]):
        q_g, k_g, v_g = _proj_dilated(h, w_g, rope_h, d, bsz, seq)
        groups.append(_dilated_attention(q_g, k_g, v_g, window // d))
    out_c = _flash_attention(q_c.reshape(bsz, seq, -1), k_c.reshape(bsz, seq, -1), vt_c,
                             C_HEADS, C_QK).reshape(m, C_W)
    out_a = _moba_merge(qk3, v_a3, part_o, part_lse).reshape(m, A_W)
    return _mixer_tail(x, out_a, [g[0] for g in groups], [g[1] for g in groups], out_c, gates,
                       w_pa.astype(BF16), w_pb.astype(BF16), w_pc.astype(BF16), w_o.astype(BF16),
                       seq)


def kernel(x, mem, g_mix, w_in, g_cq, g_ckv, w_uq, w_ukv, w_pa, w_pb, w_pc, w_o, g_mem, g_memkv,
           w_xq, w_xk, w_xv, w_xo, g_ffn, w_up, conv_w, conv_b, w_down, g_final):
    bsz, seq, d = x.shape
    rope_h, rope_r = _rope_tables(seq)
    xf = x.reshape(bsz * seq, d)
    w_down = w_down.astype(BF16)
    w_up = _pad_ff_halves(w_up.astype(BF16))
    for l in range(DEPTH):
        p = _layer_params(w_in[l], w_uq[l], w_ukv[l], conv_w[l], conv_b[l])
        xf = _mixer(xf, g_mix[l], p, g_cq[l], g_ckv[l], w_pa[l], w_pb[l], w_pc[l], w_o[l],
                    rope_h, rope_r, bsz, seq)
        kmem, vmem = _mem_kv(mem, g_memkv[l], w_xk[l].astype(BF16), w_xv[l].astype(BF16))
        xf = _mem_attention(xf, g_mem[l], w_xq[l].astype(BF16), kmem, vmem,
                            w_xo[l].astype(BF16), seq)
        xf = _conv_ffn(xf, g_ffn[l], w_up, p["conv_w"], p["conv_b"], w_down, l, seq)
    return _rmsnorm(xf, g_final, F32).reshape(bsz, seq, d)
```

```python
import functools
import math

import jax
import jax.numpy as jnp
import numpy as np
from jax import lax
from jax.experimental import pallas as pl
from jax.experimental.pallas import tpu as pltpu
from jax.experimental.pallas import tpu_sc as plsc

F32 = jnp.float32
BF16 = jnp.bfloat16

LANES = 128
SUBLANES = 8
V7X_VMEM_BYTES = 64 * 1024 * 1024
VMEM_LIMIT = V7X_VMEM_BYTES * 7 // 8

D_MODEL = 2048
DEPTH = 2
HEAD_DIM = 128
ROPE_THETA = 10000.0
EPS = 1e-6

A_HEADS = 4
MOBA_BLOCK = 256
MOBA_TOPK = 3

B_GROUPS = ((128, 1), (512, 4), (2048, 16))
B_HEADS = 4
B_QBLOCK = 128

C_HEADS = 8
Q_LORA = 1536
KV_LORA = 512
NOPE_DIM = 128
ROPE_DIM = 64
V_DIM = 128

X_HEADS = 4
D_FF = 5504
CONV_W = 3

A_W = A_HEADS * HEAD_DIM
B_QKV_W = len(B_GROUPS) * B_HEADS * HEAD_DIM
B_W = B_HEADS * HEAD_DIM
C_W = C_HEADS * V_DIM
X_W = X_HEADS * HEAD_DIM
IN_WIDTHS = (A_W, A_W, A_W, B_QKV_W, B_QKV_W, B_QKV_W, Q_LORA, KV_LORA, ROPE_DIM, 3 * D_MODEL)
IN_OFFSETS = tuple(int(o) for o in np.cumsum((0,) + IN_WIDTHS))

QK_W = 2 * A_W
QA_BLK, KA_BLK = 0, A_W // LANES

C_QK = 2 * LANES
MASKED = -1e30
LOG2E = math.log2(math.e)
BF16_ROWS = 16
VT_ROWS = V_DIM + BF16_ROWS
GROUP_STEP = 16
SC_WINDOW = 128

FFN_TF = 512
D_FF_PAD = -(-D_FF // FFN_TF) * FFN_TF
HALO = SUBLANES


def _cparams(*sem):
    return pltpu.CompilerParams(dimension_semantics=sem, vmem_limit_bytes=VMEM_LIMIT)


def _resident(arr):
    zeros = (0,) * arr.ndim
    return pl.BlockSpec(arr.shape, lambda *_: zeros, pipeline_mode=pl.Buffered(1))


def _dot(a, b):
    return jnp.dot(a, b, preferred_element_type=F32)


def _dot_nt(a, b):
    return lax.dot_general(a, b, (((1,), (1,)), ((), ())), preferred_element_type=F32)


def _rms(x, g):
    return x * lax.rsqrt(jnp.mean(x * x, axis=-1, keepdims=True) + EPS) * g


def _rmsnorm_kernel(x_ref, g_ref, o_ref):
    o_ref[...] = _rms(x_ref[...], g_ref[...]).astype(o_ref.dtype)


def _rmsnorm(x, g, out_dtype, tm=1024):
    m, d = x.shape
    return pl.pallas_call(
        _rmsnorm_kernel,
        grid=(m // tm,),
        in_specs=[pl.BlockSpec((tm, d), lambda i: (i, 0)),
                  pl.BlockSpec((1, d), lambda i: (0, 0))],
        out_specs=pl.BlockSpec((tm, d), lambda i: (i, 0)),
        out_shape=jax.ShapeDtypeStruct((m, d), out_dtype),
        compiler_params=_cparams("parallel"),
        name="rmsnorm",
    )(x, g.reshape(1, d))


def _rope128(x, c, s):
    return x * c + pltpu.roll(x, HEAD_DIM // 2, 1) * s


def _rope64(x, c, sa, sb):
    half = ROPE_DIM // 2
    return x * c + pltpu.roll(x, LANES - half, 1) * sa + pltpu.roll(x, half, 1) * sb


def _mm_plain_kernel(a_ref, w_ref, o_ref):
    o_ref[...] = _dot(a_ref[...], w_ref[...]).astype(o_ref.dtype)


def _mm_sigmoid_kernel(a_ref, w_ref, o_ref):
    o_ref[...] = jax.nn.sigmoid(_dot(a_ref[...], w_ref[...])).astype(o_ref.dtype)


def _mm_rope_kernel(a_ref, w_ref, cs_ref, c_ref, s_ref, o_ref):
    acc = _dot(a_ref[...], w_ref[...])
    c = c_ref[...]
    s = s_ref[...]
    for j in range(acc.shape[1] // LANES):
        sl = slice(j * LANES, (j + 1) * LANES)
        o_ref[:, sl] = (_rope128(acc[:, sl], c, s) * cs_ref[:, sl]).astype(o_ref.dtype)


def _matmul(a, w, kernel, out_dtype, tm, tn, seq=None, extras=(), name="matmul"):
    m, k = a.shape
    n = w.shape[1]
    in_specs = [pl.BlockSpec((tm, k), lambda i, j: (i, 0)),
                pl.BlockSpec((k, tn), lambda i, j: (0, j))]
    args = [a, w]
    for kind, arr in extras:
        if kind == "col":
            in_specs.append(pl.BlockSpec((1, tn), lambda i, j: (0, j)))
        else:
            nt = seq // tm
            in_specs.append(pl.BlockSpec((tm, LANES), lambda i, j: (i % nt, 0)))
        args.append(arr)
    return pl.pallas_call(
        kernel,
        grid=(m // tm, n // tn),
        in_specs=in_specs,
        out_specs=pl.BlockSpec((tm, tn), lambda i, j: (i, j)),
        out_shape=jax.ShapeDtypeStruct((m, n), out_dtype),
        compiler_params=_cparams("parallel", "parallel"),
        name=name,
    )(*args)


def _mla_down_kernel(h_ref, w_ref, gq_ref, gkv_ref, c_ref, sa_ref, sb_ref,
                     cq_ref, ckv_ref, kr_ref):
    acc = _dot(h_ref[...], w_ref[...])
    cq_ref[...] = _rms(acc[:, :Q_LORA], gq_ref[...]).astype(cq_ref.dtype)
    ckv_ref[...] = _rms(acc[:, Q_LORA:Q_LORA + KV_LORA], gkv_ref[...]).astype(ckv_ref.dtype)
    kr = acc[:, Q_LORA + KV_LORA:]
    kr_ref[...] = _rope64(kr, c_ref[...], sa_ref[...], sb_ref[...]).astype(kr_ref.dtype)


def _mla_down(h, w, g_cq, g_ckv, rope_r, seq, tm=1024):
    m, k = h.shape
    n = w.shape[1]
    nt = seq // tm
    row = lambda width: pl.BlockSpec((tm, width), lambda i: (i, 0))
    full = lambda r, c: pl.BlockSpec((r, c), lambda i: (0, 0))
    pos = pl.BlockSpec((tm, LANES), lambda i: (i % nt, 0))
    return pl.pallas_call(
        _mla_down_kernel,
        grid=(m // tm,),
        in_specs=[row(k), full(k, n), full(1, Q_LORA), full(1, KV_LORA), pos, pos, pos],
        out_specs=[row(Q_LORA), row(KV_LORA), row(LANES)],
        out_shape=[jax.ShapeDtypeStruct((m, Q_LORA), BF16),
                   jax.ShapeDtypeStruct((m, KV_LORA), BF16),
                   jax.ShapeDtypeStruct((m, LANES), BF16)],
        compiler_params=_cparams("parallel"),
        name="mla_down",
    )(h, w, g_cq.reshape(1, -1), g_ckv.reshape(1, -1), *rope_r)


def _mla_q_kernel(cq_ref, w_ref, c_ref, sa_ref, sb_ref, q_ref, *, scale):
    acc = _dot(cq_ref[...], w_ref[...])
    c, sa, sb = c_ref[...], sa_ref[...], sb_ref[...]
    for hd in range(C_HEADS):
        lo = hd * C_QK
        q_ref[:, lo:lo + LANES] = (acc[:, lo:lo + LANES] * scale).astype(q_ref.dtype)
        rope = _rope64(acc[:, lo + LANES:lo + C_QK], c, sa, sb)
        q_ref[:, lo + LANES:lo + C_QK] = (rope * scale).astype(q_ref.dtype)


def _mla_q(cq, w, rope_r, seq, tm=1024):
    m, k = cq.shape
    n = w.shape[1]
    nt = seq // tm
    pos = pl.BlockSpec((tm, LANES), lambda i: (i % nt, 0))
    return pl.pallas_call(
        functools.partial(_mla_q_kernel, scale=(NOPE_DIM + ROPE_DIM) ** -0.5 * LOG2E),
        grid=(m // tm,),
        in_specs=[pl.BlockSpec((tm, k), lambda i: (i, 0)),
                  pl.BlockSpec((k, n), lambda i: (0, 0)), pos, pos, pos],
        out_specs=pl.BlockSpec((tm, n), lambda i: (i, 0)),
        out_shape=jax.ShapeDtypeStruct((m, n), BF16),
        compiler_params=_cparams("parallel"),
        name="mla_q",
    )(cq, w, *rope_r)


def _store_vt(v, vt_ref):
    vt = v.T
    for hd in range(vt_ref.shape[0]):
        vt_ref[hd, 0:V_DIM, :] = vt[hd * V_DIM:(hd + 1) * V_DIM, :].astype(vt_ref.dtype)
        vt_ref[hd, V_DIM:VT_ROWS, :] = jnp.ones((VT_ROWS - V_DIM, vt.shape[1]), vt_ref.dtype)


def _mla_kv_kernel(ckv_ref, kr_ref, wk_ref, wv_ref, k_ref, vt_ref):
    ckv = ckv_ref[...]
    kn = _dot(ckv, wk_ref[...])
    kr = kr_ref[...]
    for hd in range(C_HEADS):
        lo = hd * C_QK
        k_ref[:, lo:lo + LANES] = kn[:, hd * LANES:(hd + 1) * LANES].astype(k_ref.dtype)
        k_ref[:, lo + LANES:lo + C_QK] = kr
    _store_vt(_dot(ckv, wv_ref[...]), vt_ref)


def _mla_kv(ckv, kr, wk, wv, bsz, seq, tm=1024):
    m, k = ckv.shape
    nt = seq // tm
    return pl.pallas_call(
        _mla_kv_kernel,
        grid=(m // tm,),
        in_specs=[pl.BlockSpec((tm, k), lambda i: (i, 0)),
                  pl.BlockSpec((tm, LANES), lambda i: (i, 0)),
                  pl.BlockSpec(wk.shape, lambda i: (0, 0)),
                  pl.BlockSpec(wv.shape, lambda i: (0, 0))],
        out_specs=[pl.BlockSpec((tm, C_HEADS * C_QK), lambda i: (i, 0)),
                   pl.BlockSpec((None, C_HEADS, VT_ROWS, tm), lambda i: (i // nt, 0, 0, i % nt))],
        out_shape=[jax.ShapeDtypeStruct((m, C_HEADS * C_QK), BF16),
                   jax.ShapeDtypeStruct((bsz, C_HEADS, VT_ROWS, seq), BF16)],
        compiler_params=_cparams("parallel"),
        name="mla_kv",
    )(ckv, kr, wk, wv)


def _attend_chunks(qT, k_ref, vt_ref, scratch, *, tk, n_full, mask_tail, tail_steps, tail_col,
                   unroll):
    m_sc, acc_sc, sa_sc, sb_sc, pa_sc, pb_sc, ala_sc, alb_sc, mxa_sc, mxb_sc = scratch
    s_bufs = (sa_sc, sb_sc)
    p_bufs = (pa_sc, pb_sc)
    al_bufs = (ala_sc, alb_sc)
    mx_bufs = (mxa_sc, mxb_sc)
    last_chunk = k_ref.shape[0] // tk - 1
    m_sc[...] = jnp.full(m_sc.shape, MASKED, F32)
    acc_sc[...] = jnp.zeros(acc_sc.shape, F32)
    for p_ref, al_ref in zip(p_bufs, al_bufs):
        p_ref[...] = jnp.zeros(p_ref.shape, p_ref.dtype)
        al_ref[...] = jnp.ones(al_ref.shape, F32)

    def rows(c):
        return pl.ds(pl.multiple_of(jnp.clip(c, 0, last_chunk) * tk, tk), tk)

    def scores(c, slot, col=0):
        sT = _dot(k_ref[rows(c), :], qT[:, col:])
        s_bufs[slot][:, col:] = sT
        mx_bufs[slot][:, col:] = jnp.max(sT, axis=0, keepdims=True)

    def flush(c, slot, col=0):
        acc_sc[:, col:] = (al_bufs[slot][:, col:] * acc_sc[:, col:]
                           + _dot(vt_ref[:, rows(c)], p_bufs[slot][:, col:]))

    def softmax(sT, top, slot, col):
        m_old = m_sc[:, col:]
        m_new = jnp.maximum(m_old, top)
        al_bufs[slot][:, col:] = jnp.exp2(m_old - m_new)
        p_bufs[slot][:, col:] = jnp.exp2(sT - m_new).astype(p_bufs[slot].dtype)
        m_sc[:, col:] = m_new

    def step(tau, slot, mask, cols):
        col_flush, col, col_next = cols
        flush(tau - 2, slot, col_flush)
        if col_next is not None:
            scores(tau + 1, 1 - slot, col_next)
        sT = s_bufs[slot][:, col:]
        if mask is None:
            softmax(sT, mx_bufs[slot][:, col:], slot, col)
        else:
            sT = mask(sT, tau, col)
            softmax(sT, jnp.max(sT, axis=0, keepdims=True), slot, col)

    def full_steps(tau0, count):
        for j in range(count):
            step(tau0 + j, j % 2, None, (0, 0, 0))

    scores(0, 0)
    trips = n_full // unroll
    lax.fori_loop(0, trips, lambda u, c: (full_steps(unroll * u, unroll), c)[1], 0)
    done = unroll * trips
    pairs = (n_full - done) // 2
    lax.fori_loop(0, pairs, lambda u, c: (full_steps(done + 2 * u, 2), c)[1], 0)
    tau = done + 2 * pairs
    cols = [tail_col(j) for j in range(tail_steps)]
    for j in range(tail_steps):
        col_flush = cols[j - 2] if j >= 2 else 0
        col_next = cols[j + 1] if j + 1 < tail_steps else None
        step(tau + j, j % 2, mask_tail, (col_flush, cols[j], col_next))
    flush(tau + tail_steps - 2, 0, cols[-2])
    flush(tau + tail_steps - 1, 1, cols[-1])
    acc = acc_sc[...]
    return acc[:V_DIM, :] / acc[V_DIM:V_DIM + 1, :]


def _attend_scratch(tq, tk):
    return [pltpu.VMEM((1, tq), F32), pltpu.VMEM((VT_ROWS, tq), F32),
            pltpu.VMEM((tk, tq), F32), pltpu.VMEM((tk, tq), F32),
            pltpu.VMEM((tk, tq), BF16), pltpu.VMEM((tk, tq), BF16),
            pltpu.VMEM((1, tq), F32), pltpu.VMEM((1, tq), F32),
            pltpu.VMEM((1, tq), F32), pltpu.VMEM((1, tq), F32)]


def _transpose_q(q_ref):
    return q_ref[...].astype(F32).T.astype(BF16)


def _flash_kernel(q_ref, k_ref, vt_ref, o_ref, *scratch, tq, tk):
    i = pl.program_id(2)

    def causal(sT, c, col):
        key = lax.broadcasted_iota(jnp.int32, sT.shape, 0) + c * tk
        qry = lax.broadcasted_iota(jnp.int32, sT.shape, 1) + (i * tq + col)
        return jnp.where(key <= qry, sT, MASKED)

    per_tile = tq // tk
    oT = _attend_chunks(_transpose_q(q_ref), k_ref, vt_ref, scratch, tk=tk, n_full=i * per_tile,
                        mask_tail=causal, tail_steps=per_tile, tail_col=lambda j: j * tk,
                        unroll=2)
    o_ref[...] = oT.T.astype(o_ref.dtype)


def _flash_attention(q, k, vt, heads, qk_w, tq=2048, tk=1024):
    bsz, seq, _ = q.shape
    tq, tk = min(tq, seq), min(tk, seq // 2)
    assert tq % (2 * tk) == 0 and seq % tq == 0
    once = pl.Buffered(1)
    return pl.pallas_call(
        functools.partial(_flash_kernel, tq=tq, tk=tk),
        grid=(bsz, heads, seq // tq),
        in_specs=[pl.BlockSpec((None, tq, qk_w), lambda b, h, i: (b, i, h)),
                  pl.BlockSpec((None, seq, qk_w), lambda b, h, i: (b, 0, h), pipeline_mode=once),
                  pl.BlockSpec((None, None, VT_ROWS, seq), lambda b, h, i: (b, h, 0, 0),
                               pipeline_mode=once)],
        out_specs=pl.BlockSpec((None, tq, V_DIM), lambda b, h, i: (b, i, h)),
        out_shape=jax.ShapeDtypeStruct((bsz, seq, heads * V_DIM), BF16),
        scratch_shapes=_attend_scratch(tq, tk),
        compiler_params=_cparams("parallel", "parallel", "arbitrary"),
        name="mla_flash",
    )(q, k, vt)


def _kmean_kernel(k_ref, o_ref):
    k = k_ref[...].astype(F32)
    o_ref[...] = jnp.mean(k.reshape(SUBLANES, MOBA_BLOCK, k.shape[-1]), axis=1)


def _kmean(qk):
    bsz, seq, _ = qk.shape
    rows = SUBLANES * MOBA_BLOCK
    return pl.pallas_call(
        _kmean_kernel,
        grid=(bsz, seq // rows),
        in_specs=[pl.BlockSpec((None, rows, A_W), lambda b, i: (b, i, KA_BLK * LANES // A_W))],
        out_specs=pl.BlockSpec((None, SUBLANES, A_W), lambda b, i: (b, i, 0)),
        out_shape=jax.ShapeDtypeStruct((bsz, seq // MOBA_BLOCK, A_W), F32),
        compiler_params=_cparams("parallel", "parallel"),
        name="moba_kmean",
    )(qk)


def _block_attention(q, k, v, visible=None):
    s = _dot_nt(q, k)
    if visible is not None:
        s = jnp.where(visible, s, MASKED)
    m = jnp.max(s, axis=-1, keepdims=True)
    p = jnp.exp2(s - m).astype(BF16)
    v_ones = jnp.concatenate([v, jnp.ones((v.shape[0], LANES), v.dtype)], axis=-1)
    acc = _dot(p, v_ones)
    den = acc[:, V_DIM:]
    return acc[:, :V_DIM] / den, m + jnp.log2(den)


def _moba_gate_kernel(q_ref, km_ref, ids_ref, cnt_ref, qf_ref):
    t = MOBA_BLOCK
    i = pl.program_id(1)
    nb = km_ref.shape[0]
    blk = lax.broadcasted_iota(jnp.int32, (nb, t), 0)
    neg_inf = jnp.float32(-jnp.inf)
    not_after = (lax.broadcasted_iota(jnp.int32, (t, t), 0)
                 <= lax.broadcasted_iota(jnp.int32, (t, t), 1))
    upper = jnp.where(not_after, 1.0, 0.0).astype(BF16)
    ones = jnp.ones((SUBLANES, t), BF16)
    for hd in range(A_HEADS):
        sl = slice(hd * HEAD_DIM, (hd + 1) * HEAD_DIM)
        q = q_ref[:, sl].astype(F32)
        qf_ref[hd] = q
        qT = q.T.astype(BF16)
        km = km_ref[:, sl]
        km_hi = km.astype(BF16)
        km_lo = (km - km_hi.astype(F32)).astype(BF16)
        g = jnp.where(blk < i, _dot(km_hi, qT) + _dot(km_lo, qT), neg_inf)
        picks, ranks, counts = [], [], []
        for _ in range(MOBA_TOPK):
            mx = jnp.max(g, axis=0, keepdims=True)
            is_max = (g == mx) & (mx > neg_inf)
            first = jnp.min(jnp.where(is_max, blk, nb), axis=0, keepdims=True)
            pick = blk == first
            g = jnp.where(pick, neg_inf, g)
            onehot = jnp.where(pick, 1.0, 0.0).astype(BF16)
            before = _dot(onehot, upper)
            rank = jnp.sum(jnp.where(pick, before - 1.0, 0.0), axis=0, keepdims=True)
            picks.append(first)
            ranks.append(rank.astype(jnp.int32))
            counts.append(_dot_nt(ones, onehot)[0:1, :])
        pad_i = jnp.zeros((SUBLANES - 2 * MOBA_TOPK, t), jnp.int32)
        ids_ref[hd] = jnp.concatenate(picks + ranks + [pad_i], axis=0)
        pad_f = jnp.zeros((SUBLANES - MOBA_TOPK, nb), F32)
        cnt_ref[hd] = jnp.concatenate(counts + [pad_f], axis=0)


def _moba_gate(qk, kmean):
    bsz, seq, _ = qk.shape
    t = MOBA_BLOCK
    nb = seq // t
    return pl.pallas_call(
        _moba_gate_kernel,
        grid=(bsz, nb),
        in_specs=[pl.BlockSpec((None, t, A_W), lambda b, i: (b, i, QA_BLK * LANES // A_W)),
                  pl.BlockSpec((None, nb, A_W), lambda b, i: (b, 0, 0))],
        out_specs=[pl.BlockSpec((None, A_HEADS, SUBLANES, t), lambda b, i: (b, 0, 0, i)),
                   pl.BlockSpec((None, A_HEADS, None, SUBLANES, nb), lambda b, i: (b, 0, i, 0, 0)),
                   pl.BlockSpec((None, A_HEADS, t, HEAD_DIM), lambda b, i: (b, 0, i, 0))],
        out_shape=[jax.ShapeDtypeStruct((bsz, A_HEADS, SUBLANES, seq), jnp.int32),
                   jax.ShapeDtypeStruct((bsz, A_HEADS, nb, SUBLANES, nb), F32),
                   jax.ShapeDtypeStruct((bsz, A_HEADS, seq, HEAD_DIM), F32)],
        compiler_params=_cparams("parallel", "parallel"),
        name="moba_gate",
    )(qk, kmean)


def _moba_routes(ids, cnt, seq):
    bsz, heads = ids.shape[:2]
    bh, t = bsz * heads, MOBA_BLOCK
    nb = seq // t
    tiles = _moba_tiles(seq)
    picks = ids[:, :, 0:MOBA_TOPK, :].reshape(bh, MOBA_TOPK, nb, t)
    ranks = ids[:, :, MOBA_TOPK:2 * MOBA_TOPK, :].reshape(bh, MOBA_TOPK, nb, t)
    per_tile = cnt[:, :, :, 0:MOBA_TOPK, :].astype(jnp.int32).reshape(bh, nb * MOBA_TOPK, nb)
    before = jnp.cumsum(per_tile, axis=1) - per_tile
    total = jnp.sum(per_tile, axis=1)
    padded = -(-total // t) * t
    ends = jnp.cumsum(padded, axis=1)
    base = before + (ends - padded)[:, None, :]
    base = base.reshape(bh, nb, MOBA_TOPK, nb).transpose(0, 2, 1, 3)
    onehot = picks[..., None] == jnp.arange(nb)
    pos = jnp.sum(jnp.where(onehot, base[:, :, :, None, :], 0), axis=-1) + ranks
    pos = jnp.where(picks < nb, pos, (tiles - 1) * t)
    pos = pos + (jnp.arange(bh, dtype=jnp.int32) * (tiles * t))[:, None, None, None]
    pos = pos.reshape(bh, MOBA_TOPK, seq).transpose(1, 0, 2).reshape(MOBA_TOPK, bh * seq)
    tile_start = jnp.arange(tiles, dtype=jnp.int32) * t
    tile_blk = jnp.sum(tile_start[None, :, None] >= ends[:, None, :], axis=-1)
    tile_blk = jnp.where(tile_start[None, :] < ends[:, -1:], tile_blk, -1)
    return pos.astype(jnp.int32), tile_blk.astype(jnp.int32)


def _moba_tiles(seq):
    nb = seq // MOBA_BLOCK
    return -(-(MOBA_TOPK * nb + nb + 1) // GROUP_STEP) * GROUP_STEP


def _sc_mesh():
    return plsc.VectorSubcoreMesh(core_axis_name="core", subcore_axis_name="subcore")


def _sc_scatter_rows(x, idx, rows):
    slots, n = idx.shape
    d = x.shape[1]

    @pl.kernel(out_type=jax.ShapeDtypeStruct((rows, d), x.dtype), mesh=_sc_mesh())
    def scatter(x_hbm, i_hbm, o_hbm):
        def body(x_vmem, i_vmem):
            pltpu.sync_copy(x_vmem, o_hbm.at[i_vmem.at[0]])

        pltpu.emit_pipeline(
            body, grid=(slots, n // SC_WINDOW),
            in_specs=[pl.BlockSpec((SC_WINDOW, d), lambda s, i: (i, 0)),
                      pl.BlockSpec((1, SC_WINDOW), lambda s, i: (s, i))],
            out_specs=[],
            core_axis_name=("core", "subcore"),
            dimension_semantics=(pltpu.PARALLEL, pltpu.PARALLEL),
        )(x_hbm, i_hbm)

    return scatter(x, idx)


def _sc_gather_rows(x, idx):
    n = idx.shape[0]
    d = x.shape[1]

    @pl.kernel(out_type=jax.ShapeDtypeStruct((n, d), x.dtype), mesh=_sc_mesh())
    def gather(x_hbm, i_hbm, o_hbm):
        def body(i_vmem, o_vmem):
            pltpu.sync_copy(x_hbm.at[i_vmem.at[0]], o_vmem)

        pltpu.emit_pipeline(
            body, grid=(n // SC_WINDOW,),
            in_specs=[pl.BlockSpec((1, SC_WINDOW), lambda i: (0, i))],
            out_specs=[pl.BlockSpec((SC_WINDOW, d), lambda i: (i, 0))],
            core_axis_name=("core", "subcore"),
            dimension_semantics=(pltpu.PARALLEL,),
        )(i_hbm, o_hbm)

    return gather(x, idx.reshape(1, n))


def _moba_group_kernel(tb_ref, q_ref, *refs):
    t = MOBA_BLOCK
    k_refs, v_refs = refs[:GROUP_STEP], refs[GROUP_STEP:2 * GROUP_STEP]
    o_ref, lse_ref = refs[-2:]
    g, step = pl.program_id(0), pl.program_id(1)
    first = step * GROUP_STEP

    @pl.when(tb_ref[g, first] < 0)
    def _():
        o_ref[...] = jnp.zeros(o_ref.shape, o_ref.dtype)
        lse_ref[...] = jnp.full(lse_ref.shape, MASKED, lse_ref.dtype)

    @pl.when(tb_ref[g, first] >= 0)
    def _():
        for u in range(GROUP_STEP):
            used = tb_ref[g, first + u] >= 0
            rows = slice(u * t, (u + 1) * t)
            o, lse = _block_attention(q_ref[rows, :].astype(BF16), k_refs[u][...], v_refs[u][...])
            o_ref[rows, :] = jnp.where(used, o, 0.0)
            lse_ref[rows, :] = jnp.where(used, lse, MASKED)


def _moba_group_attention(q_grouped, tile_blk, qk, v, after):
    bh, rows, _ = q_grouped.shape
    t = MOBA_BLOCK
    tiles = rows // t
    heads = A_HEADS

    def block_of(u, first_col):
        return lambda g, s, tb: (g // heads, jnp.maximum(tb[g, s * GROUP_STEP + u], 0),
                                 first_col + g % heads)

    row_tile = pl.BlockSpec((None, GROUP_STEP * t, HEAD_DIM), lambda g, s, tb: (g, s, 0))
    key_value = lambda first_col: [pl.BlockSpec((None, t, HEAD_DIM), block_of(u, first_col))
                                   for u in range(GROUP_STEP)]
    grid_spec = pltpu.PrefetchScalarGridSpec(
        num_scalar_prefetch=1,
        grid=(bh, tiles // GROUP_STEP),
        in_specs=([row_tile] + key_value(KA_BLK) + key_value(0)
                  + [pl.BlockSpec(memory_space=pl.ANY)]),
        out_specs=[row_tile, row_tile],
    )
    return pl.pallas_call(
        _moba_group_kernel,
        grid_spec=grid_spec,
        out_shape=[jax.ShapeDtypeStruct(q_grouped.shape, F32)] * 2,
        compiler_params=_cparams("parallel", "parallel"),
        name="moba_group",
    )(tile_blk, q_grouped, *([qk] * GROUP_STEP), *([v] * GROUP_STEP), after)


def _moba_merge_kernel(q_ref, k_ref, v_ref, po_ref, pl_ref, o_ref):
    t = MOBA_BLOCK
    causal = (lax.broadcasted_iota(jnp.int32, (t, t), 1)
              <= lax.broadcasted_iota(jnp.int32, (t, t), 0))
    for hd in range(A_HEADS):
        sl = slice(hd * HEAD_DIM, (hd + 1) * HEAD_DIM)
        o_own, lse_own = _block_attention(q_ref[:, sl], k_ref[:, sl], v_ref[:, sl], causal)
        outs = [o_own] + [po_ref[s, hd] for s in range(MOBA_TOPK)]
        lses = [lse_own] + [pl_ref[s, hd] for s in range(MOBA_TOPK)]
        top = functools.reduce(jnp.maximum, lses)
        weights = [jnp.exp2(l - top) for l in lses]
        num = sum(w * o for w, o in zip(weights, outs))
        o_ref[:, sl] = (num / sum(weights)).astype(o_ref.dtype)


def _moba_merge(qk, v, part_o, part_lse):
    bsz, seq, _ = qk.shape
    t = MOBA_BLOCK
    part = pl.BlockSpec((MOBA_TOPK, None, A_HEADS, t, HEAD_DIM), lambda b, i: (0, b, 0, i, 0))
    return pl.pallas_call(
        _moba_merge_kernel,
        grid=(bsz, seq // t),
        in_specs=[pl.BlockSpec((None, t, A_W), lambda b, i: (b, i, QA_BLK * LANES // A_W)),
                  pl.BlockSpec((None, t, A_W), lambda b, i: (b, i, KA_BLK * LANES // A_W)),
                  pl.BlockSpec((None, t, A_W), lambda b, i: (b, i, 0)),
                  part, part],
        out_specs=pl.BlockSpec((None, t, A_W), lambda b, i: (b, i, 0)),
        out_shape=jax.ShapeDtypeStruct((bsz, seq, A_W), BF16),
        compiler_params=_cparams("parallel", "parallel"),
        name="moba_merge",
    )(qk, qk, v, part_o, part_lse)


def _moba_regroup(qk, kmean):
    bsz, seq, _ = qk.shape
    bh = bsz * A_HEADS
    rows = _moba_tiles(seq) * MOBA_BLOCK
    ids, cnt, q_f32 = _moba_gate(qk, kmean)
    pos, tile_blk = _moba_routes(ids, cnt, seq)
    q_grouped = _sc_scatter_rows(q_f32.reshape(bh * seq, HEAD_DIM), pos, bh * rows)
    return q_grouped.reshape(bh, rows, HEAD_DIM), pos, tile_blk


def _moba_picked_blocks(q_grouped, pos, tile_blk, qk, v, after):
    bsz, seq, _ = qk.shape
    bh, rows, _ = q_grouped.shape
    o_g, lse_g = _moba_group_attention(q_grouped, tile_blk, qk, v, after)
    flat = pos.reshape(-1)
    back = lambda a: _sc_gather_rows(a.reshape(bh * rows, HEAD_DIM), flat).reshape(
        MOBA_TOPK, bsz, A_HEADS, seq, HEAD_DIM)
    return back(o_g), back(lse_g)


def _proj_dilated_kernel(h_ref, w_ref, c_ref, s_ref, q_ref, k_ref, v_ref, sc, *, d):
    acc = _dot(h_ref[...], w_ref[...])
    c, s = c_ref[...], s_ref[...]
    q_scale = HEAD_DIM ** -0.5 * LOG2E
    for j in range(acc.shape[1] // LANES):
        blk = acc[:, j * LANES:(j + 1) * LANES]
        if j < B_HEADS:
            blk = _rope128(blk, c, s) * q_scale
        elif j < 2 * B_HEADS:
            blk = _rope128(blk, c, s)
        sc[j] = blk
    rows = acc.shape[0] // d
    for r in range(d):
        for j in range(acc.shape[1] // LANES):
            dst = (q_ref, k_ref, v_ref)[j // B_HEADS]
            col = (j % B_HEADS) * LANES
            dst[r, :, col:col + LANES] = sc[j, pl.ds(r, rows, stride=d), :].astype(dst.dtype)


def _proj_dilated(h, w, rope_h, d, bsz, seq, tm=1024):
    m, k = h.shape
    nt = seq // tm
    pos = pl.BlockSpec((tm, LANES), lambda i: (i % nt, 0))
    out = pl.BlockSpec((None, d, tm // d, B_W), lambda i: (i // nt, 0, i % nt, 0))
    return pl.pallas_call(
        functools.partial(_proj_dilated_kernel, d=d),
        grid=(m // tm,),
        in_specs=[pl.BlockSpec((tm, k), lambda i: (i, 0)), pl.BlockSpec(w.shape, lambda i: (0, 0)),
                  pos, pos],
        out_specs=[out] * 3,
        out_shape=[jax.ShapeDtypeStruct((bsz, d, seq // d, B_W), BF16)] * 3,
        scratch_shapes=[pltpu.VMEM((w.shape[1] // LANES, tm, LANES), F32)],
        compiler_params=_cparams("parallel"),
        name=f"proj_dilated_d{d}",
    )(h, w, *rope_h)


def _dilated_kernel(q_ref, kc_ref, kp_ref, vc_ref, vp_ref, o_ref, lse_ref, *, span):
    t, tp = q_ref.shape[0], kp_ref.shape[0]
    i = pl.program_id(2)
    shape = (2 * tp, tp)
    key_row = lax.broadcasted_iota(jnp.int32, shape, 0)
    dist = lax.broadcasted_iota(jnp.int32, shape, 1) + tp - key_row
    visible = (dist >= 0) & (dist <= span)
    bias = jnp.where(visible, 0.0, MASKED)
    bias_first = jnp.where(visible & ((key_row >= tp) | (i > 0)), 0.0, MASKED)
    ones = jnp.ones((BF16_ROWS, tp + t), BF16)

    def transposed(x):
        return x.astype(F32).T.astype(BF16)

    for j in range(B_HEADS):
        sl = slice(j * LANES, (j + 1) * LANES)
        qT = transposed(q_ref[:, sl])
        k_all = jnp.concatenate([kp_ref[:, sl], kc_ref[:, sl]], axis=0)
        vt_all = jnp.concatenate([transposed(vp_ref[:, sl]), transposed(vc_ref[:, sl])], axis=1)
        vt_all = jnp.concatenate([vt_all, ones], axis=0)
        outs, lses = [], []
        for u in range(t // tp):
            window = slice(u * tp, (u + 2) * tp)
            s = _dot(k_all[window, :], qT[:, u * tp:(u + 1) * tp])
            s = s + (bias_first if u == 0 else bias)
            m = jnp.max(s, axis=0, keepdims=True)
            p = jnp.exp2(s - m).astype(BF16)
            acc = _dot(vt_all[:, window], p)
            den = acc[V_DIM:V_DIM + 1, :]
            outs.append(acc[:V_DIM, :] / den)
            lses.append(m + jnp.log2(den))
        o_ref[:, sl] = jnp.concatenate(outs, axis=1).T
        lse = jnp.concatenate(lses, axis=1)
        lse_ref[:, sl] = jnp.broadcast_to(lse, (LANES, t)).T


def _dilated_attention(q, k, v, span, t=1024):
    bsz, d, length, _ = q.shape
    t = min(t, length)
    tp = B_QBLOCK
    assert span <= tp and t % tp == 0
    cur = pl.BlockSpec((None, None, t, B_W), lambda b, r, i: (b, r, i, 0))
    prev = pl.BlockSpec((None, None, tp, B_W),
                        lambda b, r, i: (b, r, jnp.maximum(i * (t // tp) - 1, 0), 0))
    return pl.pallas_call(
        functools.partial(_dilated_kernel, span=span),
        grid=(bsz, d, length // t),
        in_specs=[cur, cur, prev, cur, prev],
        out_specs=[cur, cur],
        out_shape=[jax.ShapeDtypeStruct(q.shape, F32)] * 2,
        compiler_params=_cparams("parallel", "parallel", "parallel"),
        name=f"dilated_d{d}",
    )(q, k, k, v, v)


def _natural_rows(ref, sc):
    d, rows = ref.shape[0], ref.shape[1]
    if d == 1:
        return ref[0]
    for r in range(d):
        for j in range(B_HEADS):
            sc[j, pl.ds(r, rows, stride=d), :] = ref[r, :, j * LANES:(j + 1) * LANES]
    return jnp.concatenate([sc[j] for j in range(B_HEADS)], axis=-1)


def _mixer_tail_kernel(oa_ref, o0_ref, o1_ref, o2_ref, l0_ref, l1_ref, l2_ref, oc_ref,
                       g_ref, wpa_ref, wpb_ref, wpc_ref, merged_ref, *scratch):
    o0, o1, o2, l0, l1, l2 = [
        _natural_rows(ref, sc)
        for ref, sc in zip((o0_ref, o1_ref, o2_ref, l0_ref, l1_ref, l2_ref), scratch)]
    mx = jnp.maximum(jnp.maximum(l0, l1), l2)
    e0, e1, e2 = jnp.exp2(l0 - mx), jnp.exp2(l1 - mx), jnp.exp2(l2 - mx)
    ob = (e0 * o0 + e1 * o1 + e2 * o2) / (e0 + e1 + e2)
    pa = _dot(oa_ref[...], wpa_ref[...])
    pb = _dot(ob.astype(BF16), wpb_ref[...])
    pc = _dot(oc_ref[...], wpc_ref[...])
    d = D_MODEL
    merged = (g_ref[:, 0:d].astype(F32) * pa + g_ref[:, d:2 * d].astype(F32) * pb
              + g_ref[:, 2 * d:3 * d].astype(F32) * pc)
    merged_ref[...] = merged.astype(merged_ref.dtype)


def _mixer_tail(x, out_a, o_groups, lse_groups, out_c, gates, w_pa, w_pb, w_pc, w_o, seq, tm=256):
    m, d = x.shape
    nt = seq // tm
    row = lambda width: pl.BlockSpec((tm, width), lambda i: (i, 0))
    residue = lambda g: pl.BlockSpec((None, g.shape[1], tm // g.shape[1], B_W),
                                     lambda i: (i // nt, 0, i % nt, 0))
    weights = [_resident(w) for w in (w_pa, w_pb, w_pc)]
    groups = list(o_groups) + list(lse_groups)
    merged = pl.pallas_call(
        _mixer_tail_kernel,
        grid=(m // tm,),
        in_specs=([row(A_W)] + [residue(g) for g in groups]
                  + [row(C_W), row(3 * d)] + weights),
        out_specs=row(d),
        out_shape=jax.ShapeDtypeStruct((m, d), BF16),
        scratch_shapes=[pltpu.VMEM((B_HEADS, tm, LANES), F32) for _ in groups],
        compiler_params=_cparams("parallel"),
        name="mixer_tail",
    )(out_a, *groups, out_c, gates, w_pa, w_pb, w_pc)
    to, tn = 1024, FFN_TF
    return pl.pallas_call(
        _ffn_down_kernel,
        grid=(m // to, d // tn),
        in_specs=[pl.BlockSpec((to, d), lambda i, j: (i, 0)),
                  pl.BlockSpec((d, tn), lambda i, j: (0, j)),
                  pl.BlockSpec((to, tn), lambda i, j: (i, j))],
        out_specs=pl.BlockSpec((to, tn), lambda i, j: (i, j)),
        out_shape=jax.ShapeDtypeStruct((m, d), F32),
        compiler_params=_cparams("parallel", "parallel"),
        name="mixer_out",
    )(merged, w_o, x)


def _mem_kv_kernel(mem_ref, g_ref, wk_ref, wv_ref, k_ref, v_ref):
    memn = _rms(mem_ref[...], g_ref[...]).astype(BF16)
    k_ref[...] = _dot(memn, wk_ref[...]).astype(k_ref.dtype)
    v_ref[...] = _dot(memn, wv_ref[...]).astype(v_ref.dtype)


def _mem_kv(mem, g, wk, wv):
    bsz, n, d = mem.shape
    out = pl.BlockSpec((None, n, X_W), lambda b: (b, 0, 0))
    return pl.pallas_call(
        _mem_kv_kernel,
        grid=(bsz,),
        in_specs=[pl.BlockSpec((None, n, d), lambda b: (b, 0, 0)),
                  pl.BlockSpec((1, d), lambda b: (0, 0)),
                  pl.BlockSpec(wk.shape, lambda b: (0, 0)),
                  pl.BlockSpec(wv.shape, lambda b: (0, 0))],
        out_specs=[out, out],
        out_shape=[jax.ShapeDtypeStruct((bsz, n, X_W), BF16)] * 2,
        compiler_params=_cparams("parallel"),
        name="mem_kv",
    )(mem, g.reshape(1, d), wk, wv)


def _mem_attn_kernel(x_ref, g_ref, wq_ref, k_ref, v_ref, wo_ref, y_ref):
    x = x_ref[...]
    h = _rms(x, g_ref[...]).astype(BF16)
    q = (_dot(h, wq_ref[...]) * HEAD_DIM ** -0.5).astype(BF16)
    heads = []
    for hd in range(X_HEADS):
        sl = slice(hd * HEAD_DIM, (hd + 1) * HEAD_DIM)
        s = _dot_nt(q[:, sl], k_ref[:, sl])
        p = jnp.exp(s - jnp.max(s, axis=-1, keepdims=True))
        o = _dot(p.astype(BF16), v_ref[:, sl]) / jnp.sum(p, axis=-1, keepdims=True)
        heads.append(o.astype(BF16))
    y_ref[...] = x + _dot(jnp.concatenate(heads, axis=-1), wo_ref[...])


def _mem_attention(x, g, wq, kmem, vmem, wo, seq, tm=1024):
    m, d = x.shape
    nt = seq // tm
    n = kmem.shape[1]
    kv = pl.BlockSpec((None, n, X_W), lambda i: (i // nt, 0, 0))
    return pl.pallas_call(
        _mem_attn_kernel,
        grid=(m // tm,),
        in_specs=[pl.BlockSpec((tm, d), lambda i: (i, 0)),
                  pl.BlockSpec((1, d), lambda i: (0, 0)),
                  pl.BlockSpec(wq.shape, lambda i: (0, 0)), kv, kv,
                  pl.BlockSpec(wo.shape, lambda i: (0, 0))],
        out_specs=pl.BlockSpec((tm, d), lambda i: (i, 0)),
        out_shape=jax.ShapeDtypeStruct((m, d), F32),
        compiler_params=_cparams("parallel"),
        name="mem_attention",
    )(x, g.reshape(1, d), wq, kmem, vmem, wo)


def _ffn_up_kernel(x_ref, halo_ref, g_ref, wg_ref, wv_ref, cwg_ref, cwv_ref, cbg_ref, cbv_ref,
                   act_ref, h_sc, *, tiles_per_seq):
    i = pl.program_id(0)
    tm = x_ref.shape[0]

    @pl.when(pl.program_id(1) == 0)
    def _():
        g = g_ref[...]
        keep = (i % tiles_per_seq != 0).astype(F32)
        h_sc[0:HALO, :] = (_rms(halo_ref[...], g) * keep).astype(h_sc.dtype)
        h_sc[HALO:, :] = _rms(x_ref[...], g).astype(h_sc.dtype)

    h = h_sc[...]

    def conv(w_ref, cw_ref, cb_ref):
        u = _dot(h, w_ref[...])
        c = cb_ref[...]
        for tap in range(CONV_W):
            lo = HALO - (CONV_W - 1) + tap
            c = c + cw_ref[tap:tap + 1, :] * u[lo:lo + tm, :]
        return c

    act = jax.nn.silu(conv(wg_ref, cwg_ref, cbg_ref)) * conv(wv_ref, cwv_ref, cbv_ref)
    act_ref[...] = act.astype(act_ref.dtype)


def _ffn_down_kernel(a_ref, w_ref, x_ref, y_ref):
    y_ref[...] = x_ref[...] + _dot(a_ref[...], w_ref[...])


def _conv_ffn(x, g, w_up, conv_w, conv_b, w_down, layer, seq):
    m, d = x.shape
    act = _ffn_up(x, g, w_up, conv_w, conv_b, layer, seq)
    tm, tn = 1024, FFN_TF
    return pl.pallas_call(
        _ffn_down_kernel,
        grid=(m // tm, d // tn),
        in_specs=[pl.BlockSpec((tm, D_FF), lambda i, j: (i, 0)),
                  pl.BlockSpec((None, D_FF, tn), lambda i, j: (layer, 0, j)),
                  pl.BlockSpec((tm, tn), lambda i, j: (i, j))],
        out_specs=pl.BlockSpec((tm, tn), lambda i, j: (i, j)),
        out_shape=jax.ShapeDtypeStruct((m, d), F32),
        compiler_params=_cparams("parallel", "parallel"),
        name="ffn_down",
    )(act, w_down, x)


def _ffn_up(x, g, w_up, conv_w, conv_b, layer, seq, tm=1024, tf=FFN_TF):
    m, d = x.shape
    nf = D_FF_PAD // tf
    halo_blocks = tm // HALO
    return pl.pallas_call(
        functools.partial(_ffn_up_kernel, tiles_per_seq=seq // tm),
        grid=(m // tm, nf),
        in_specs=[pl.BlockSpec((tm, d), lambda i, f: (i, 0)),
                  pl.BlockSpec((HALO, d), lambda i, f: (jnp.maximum(i * halo_blocks - 1, 0), 0)),
                  pl.BlockSpec((1, d), lambda i, f: (0, 0)),
                  pl.BlockSpec((None, d, tf), lambda i, f: (layer, 0, f)),
                  pl.BlockSpec((None, d, tf), lambda i, f: (layer, 0, f + nf)),
                  pl.BlockSpec((CONV_W, tf), lambda i, f: (0, f)),
                  pl.BlockSpec((CONV_W, tf), lambda i, f: (0, f + nf)),
                  pl.BlockSpec((1, tf), lambda i, f: (0, f)),
                  pl.BlockSpec((1, tf), lambda i, f: (0, f + nf))],
        out_specs=pl.BlockSpec((tm, tf), lambda i, f: (i, f)),
        out_shape=jax.ShapeDtypeStruct((m, D_FF_PAD), BF16),
        scratch_shapes=[pltpu.VMEM((HALO + tm, d), BF16)],
        compiler_params=_cparams("parallel", "arbitrary"),
        name="ffn_up",
    )(x, x, g.reshape(1, d), w_up, w_up, conv_w, conv_w, conv_b, conv_b)


def _rope_tables(seq):
    def angles(dim):
        inv_freq = jnp.exp(jnp.arange(0, dim, 2, dtype=F32) * (-math.log(ROPE_THETA) / dim))
        ang = jnp.arange(seq, dtype=F32)[:, None] * inv_freq[None, :]
        return jnp.cos(ang), jnp.sin(ang)

    cos_h, sin_h = angles(HEAD_DIM)
    rope_h = (jnp.concatenate([cos_h, cos_h], axis=-1), jnp.concatenate([-sin_h, sin_h], axis=-1))
    cos_r, sin_r = angles(ROPE_DIM)
    z = jnp.zeros_like(cos_r)
    rope_r = (jnp.concatenate([cos_r, cos_r, z, z], axis=-1),
              jnp.concatenate([-sin_r, z, z, z], axis=-1),
              jnp.concatenate([z, sin_r, z, z], axis=-1))
    return rope_h, rope_r


def _split_in(w_in):
    return [w_in[:, IN_OFFSETS[k]:IN_OFFSETS[k + 1]] for k in range(len(IN_WIDTHS))]


def _pad_cols(w, width):
    return jnp.pad(w, ((0, 0), (0, width - w.shape[1])))


def _layer_params(w_in, w_uq, w_ukv, conv_w, conv_b):
    qa, ka, va, qb, kb, vb, cq, ckv, kr, gates = _split_in(w_in)
    w_qk = jnp.concatenate([qa, ka], axis=1).astype(BF16)
    group_cols = lambda w, g: w[:, g * B_W:(g + 1) * B_W]
    w_b = [jnp.concatenate([group_cols(qb, g), group_cols(kb, g), group_cols(vb, g)],
                           axis=1).astype(BF16) for g in range(len(B_GROUPS))]
    w_down_in = jnp.concatenate([cq, ckv, _pad_cols(kr, LANES)], axis=1).astype(BF16)
    uq = w_uq.reshape(Q_LORA, C_HEADS, NOPE_DIM + ROPE_DIM)
    uq = jnp.pad(uq, ((0, 0), (0, 0), (0, C_QK - NOPE_DIM - ROPE_DIM)))
    ukv = w_ukv.reshape(KV_LORA, C_HEADS, NOPE_DIM + V_DIM)
    return dict(
        w_qk=w_qk, w_va=va.astype(BF16), w_b=w_b, w_gates=gates.astype(BF16),
        w_down_in=w_down_in,
        w_uq=uq.reshape(Q_LORA, C_HEADS * C_QK).astype(BF16),
        w_uk=ukv[:, :, :NOPE_DIM].reshape(KV_LORA, C_HEADS * NOPE_DIM).astype(BF16),
        w_uv=ukv[:, :, NOPE_DIM:].reshape(KV_LORA, C_W).astype(BF16),
        conv_w=_pad_ff_halves(conv_w),
        conv_b=_pad_ff_halves(conv_b.reshape(1, -1)),
    )


def _pad_ff_halves(w):
    pad = [(0, 0)] * (w.ndim - 1) + [(0, D_FF_PAD - D_FF)]
    return jnp.concatenate([jnp.pad(w[..., :D_FF], pad), jnp.pad(w[..., D_FF:], pad)], axis=-1)


def _qk_col_scale():
    q_scale = HEAD_DIM ** -0.5
    parts = [jnp.full((A_W,), q_scale * LOG2E, F32), jnp.ones((A_W,), F32)]
    return jnp.concatenate(parts).reshape(1, QK_W)


def _mixer(x, g_mix, p, g_cq, g_ckv, w_pa, w_pb, w_pc, w_o, rope_h, rope_r, bsz, seq):
    m = x.shape[0]
    h = _rmsnorm(x, g_mix, BF16)
    qk = _matmul(h, p["w_qk"], _mm_rope_kernel, BF16, 1024, 1024, seq=seq,
                 extras=(("col", _qk_col_scale()), ("pos", rope_h[0]), ("pos", rope_h[1])),
                 name="proj_qk_rope")
    qk3 = qk.reshape(bsz, seq, QK_W)
    q_grouped, pos, tile_blk = _moba_regroup(qk3, _kmean(qk3))
    v_a = _matmul(h, p["w_va"], _mm_plain_kernel, BF16, 2048, A_W, name="proj_va")
    gates = _matmul(h, p["w_gates"], _mm_sigmoid_kernel, BF16, 1024, 1024, name="proj_gates")
    v_a3 = v_a.reshape(bsz, seq, A_W)
    part_o, part_lse = _moba_picked_blocks(q_grouped, pos, tile_blk, qk3, v_a3, after=gates)
    cq, ckv, kr = _mla_down(h, p["w_down_in"], g_cq, g_ckv, rope_r, seq)
    q_c = _mla_q(cq, p["w_uq"], rope_r, seq)
    k_c, vt_c = _mla_kv(ckv, kr, p["w_uk"], p["w_uv"], bsz, seq)
    groups = []
    for (window, d), w_g in zip(B_GROUPS, p["w_b"]):
        q_g, k_g, v_g = _proj_dilated(h, w_g, rope_h, d, bsz, seq)
        groups.append(_dilated_attention(q_g, k_g, v_g, window // d))
    out_c = _flash_attention(q_c.reshape(bsz, seq, -1), k_c.reshape(bsz, seq, -1), vt_c,
                             C_HEADS, C_QK).reshape(m, C_W)
    out_a = _moba_merge(qk3, v_a3, part_o, part_lse).reshape(m, A_W)
    return _mixer_tail(x, out_a, [g[0] for g in groups], [g[1] for g in groups], out_c, gates,
                       w_pa.astype(BF16), w_pb.astype(BF16), w_pc.astype(BF16), w_o.astype(BF16),
                       seq)


def kernel(x, mem, g_mix, w_in, g_cq, g_ckv, w_uq, w_ukv, w_pa, w_pb, w_pc, w_o, g_mem, g_memkv,
           w_xq, w_xk, w_xv, w_xo, g_ffn, w_up, conv_w, conv_b, w_down, g_final):
    bsz, seq, d = x.shape
    rope_h, rope_r = _rope_tables(seq)
    xf = x.reshape(bsz * seq, d)
    w_down = w_down.astype(BF16)
    w_up = _pad_ff_halves(w_up.astype(BF16))
    for l in range(DEPTH):
        p = _layer_params(w_in[l], w_uq[l], w_ukv[l], conv_w[l], conv_b[l])
        xf = _mixer(xf, g_mix[l], p, g_cq[l], g_ckv[l], w_pa[l], w_pb[l], w_pc[l], w_o[l],
                    rope_h, rope_r, bsz, seq)
        kmem, vmem = _mem_kv(mem, g_memkv[l], w_xk[l].astype(BF16), w_xv[l].astype(BF16))
        xf = _mem_attention(xf, g_mem[l], w_xq[l].astype(BF16), kmem, vmem,
                            w_xo[l].astype(BF16), seq)
        xf = _conv_ffn(xf, g_ffn[l], w_up, p["conv_w"], p["conv_b"], w_down, l, seq)
    return _rmsnorm(xf, g_final, F32).reshape(bsz, seq, d)
```
